```python
import jax, jax.numpy as jnp
from jax import lax
import numpy as np

D_MODEL = 1024
BATCH = 4
SEQ = 4096
DEPTH = 2

HEAD_DIM = 64
W_SSM = 512
W_NA = 512
W_DIL = 512
MIX_WIDTH = W_SSM + W_NA + W_DIL
SSM_GROUP = 16
SSM_GROUPS = W_SSM // SSM_GROUP
SSM_STATE = 64
N_DIRS = 2
NA_HEADS = W_NA // HEAD_DIM
DIL_HEADS = W_DIL // HEAD_DIM
GRID_W = 64
NA_MAX_ROWS = 8
NA_COLS = 16
DIL_PATTERNS = ((128, 1), (512, 4), (2048, 16))
DIL_PAD = 1024
Q_BLOCK = 128
T5_BUCKETS = 32
T5_MAX_DIST = 1024
DT_MIN = 1e-3
DT_MAX = 1e-1
RMS_EPS = 1e-6
NEG_INF = -1e30
IN_COLS = 2 * W_SSM + 4 * W_NA + 4 * W_DIL
SPLIT_SIZES = (W_SSM, W_SSM, W_NA, W_NA, W_NA, W_NA, W_DIL, W_DIL, W_DIL, W_DIL)

kernel_name = "hybrid_s5_natten_dilated_encoder"


def rmsnorm(x, g):
    xf = x.astype(jnp.float32)
    y = xf * lax.rsqrt(jnp.mean(xf * xf, axis=-1, keepdims=True) + RMS_EPS)
    return (y * g.astype(jnp.float32)).astype(x.dtype)


def _t5_bucket(rel):
    nb = T5_BUCKETS // 2
    max_exact = nb // 2
    n = np.abs(rel)
    large = max_exact + (np.log(np.maximum(n, 1) / max_exact) / np.log(T5_MAX_DIST / max_exact)
                         * (nb - max_exact)).astype(np.int32)
    large = np.minimum(large, nb - 1)
    return (np.where(rel > 0, nb, 0) + np.where(n < max_exact, n, large)).astype(np.int32)


def _ssm_combine(left, right):
    a_l, b_l = left
    a_r, b_r = right
    return a_r * a_l, a_r * b_l + b_r


def ssm_branch(xa, lam_re, lam_im, log_dt, b_re, b_im, c_re, c_im, d_skip, glu_w, glu_b):
    f32 = jnp.float32
    bsz, s, _ = xa.shape
    u = xa.astype(f32).reshape(bsz, s, SSM_GROUPS, SSM_GROUP)
    y = d_skip.astype(f32).reshape(SSM_GROUPS, SSM_GROUP) * u
    for direction in range(N_DIRS):
        lam = lax.complex(lam_re[direction].astype(f32), lam_im[direction].astype(f32))
        dt = jnp.exp(log_dt[direction].astype(f32))[:, None]
        lam_bar = jnp.exp(lam * dt)
        b = lax.complex(b_re[direction].astype(f32), b_im[direction].astype(f32))
        b_bar = ((lam_bar - 1.0) / lam)[..., None] * b
        bu = jnp.einsum('gpc,bsgc->bsgp', b_bar, u)
        decay = jnp.broadcast_to(lam_bar, bu.shape)
        _, h = lax.associative_scan(_ssm_combine, (decay, bu), axis=1, reverse=(direction == 1))
        c = lax.complex(c_re[direction].astype(f32), c_im[direction].astype(f32))
        y = y + jnp.einsum('gcp,bsgp->bsgc', c, h).real
    y = y.reshape(bsz, s, W_SSM)
    g = jax.nn.gelu(y)
    out = g * jax.nn.sigmoid(g @ glu_w.astype(f32) + glu_b.astype(f32))
    return out.astype(xa.dtype)


def na_branch(q, k, v, rpb):
    bsz, s, _ = q.shape
    rows = s // GRID_W
    wr = min(NA_MAX_ROWS, rows)

    def heads(t):
        return t.reshape(bsz, rows, GRID_W, NA_HEADS, HEAD_DIM).transpose(0, 3, 1, 2, 4)

    q5 = heads(q) * (HEAD_DIM ** -0.5)
    k5 = heads(k)
    v5 = heads(v)
    j = np.arange(GRID_W)
    col_start = np.clip(j - NA_COLS // 2, 0, GRID_W - NA_COLS)
    col_idx = col_start[:, None] + np.arange(NA_COLS)[None, :]
    col_rel = col_idx - j[:, None] + (NA_COLS - 1)
    rpb_cols = rpb[:, :, col_rel]

    def row_fn(r):
        rs = jnp.clip(r - wr // 2, 0, rows - wr)
        qr = lax.dynamic_index_in_dim(q5, r, axis=2, keepdims=False)
        kr = lax.dynamic_slice_in_dim(k5, rs, wr, axis=2)[:, :, :, col_idx]
        vr = lax.dynamic_slice_in_dim(v5, rs, wr, axis=2)[:, :, :, col_idx]
        row_rel = rs + jnp.arange(wr) - r + (NA_MAX_ROWS - 1)
        bias = jnp.take(rpb_cols, row_rel, axis=1).transpose(0, 2, 1, 3)
        logits = jnp.einsum('bhjd,bhrjcd->bhjrc', qr, kr).astype(jnp.float32) + bias.astype(jnp.float32)
        p = jax.nn.softmax(logits.reshape(bsz, NA_HEADS, GRID_W, wr * NA_COLS), axis=-1)
        p = p.reshape(logits.shape).astype(v.dtype)
        return jnp.einsum('bhjrc,bhrjcd->bhjd', p, vr)

    out = lax.map(row_fn, jnp.arange(rows))
    return out.transpose(1, 0, 3, 2, 4).reshape(bsz, s, W_NA)


def dilated_branch(q, k, v, t5_bias):
    bsz, s, _ = q.shape

    def heads(t):
        return t.reshape(bsz, s, DIL_HEADS, HEAD_DIM).transpose(0, 2, 1, 3)

    pad = ((0, 0), (0, 0), (DIL_PAD, DIL_PAD), (0, 0))
    qh = heads(q) * (HEAD_DIM ** -0.5)
    kp = jnp.pad(heads(k), pad)
    vp = jnp.pad(heads(v), pad)
    qi = np.arange(Q_BLOCK)
    patterns = []
    for w, d in DIL_PATTERNS:
        half = (w // 2) // d
        steps = np.arange(-half, half + 1)
        rel = d * steps
        bias = t5_bias[_t5_bucket(rel)].T
        idx = qi[:, None] + d * (steps + half)[None, :]
        patterns.append((d, half, rel, idx, bias))

    def block_fn(s0):
        qb = lax.dynamic_slice_in_dim(qh, s0, Q_BLOCK, axis=2)
        lses, outs = [], []
        for d, half, rel, idx, bias in patterns:
            seg_len = Q_BLOCK + 2 * half * d
            start = s0 + DIL_PAD - half * d
            kg = lax.dynamic_slice_in_dim(kp, start, seg_len, axis=2)[:, :, idx]
            vg = lax.dynamic_slice_in_dim(vp, start, seg_len, axis=2)[:, :, idx]
            logits = jnp.einsum('bhqd,bhqkd->bhqk', qb, kg).astype(jnp.float32) + bias[:, None, :].astype(jnp.float32)
            pos = s0 + qi[:, None] + rel[None, :]
            logits = jnp.where((pos >= 0) & (pos < s), logits, NEG_INF)
            m = jnp.max(logits, axis=-1, keepdims=True)
            p = jnp.exp(logits - m)
            den = jnp.sum(p, axis=-1, keepdims=True)
            outs.append(jnp.einsum('bhqk,bhqkd->bhqd', p, vg.astype(jnp.float32)) / den)
            lses.append(m + jnp.log(den))
        wts = jax.nn.softmax(jnp.stack(lses), axis=0)
        return jnp.sum(wts * jnp.stack(outs), axis=0).astype(q.dtype)

    out = lax.map(block_fn, jnp.arange(s // Q_BLOCK) * Q_BLOCK)
    return out.transpose(1, 0, 3, 2, 4).reshape(bsz, s, W_DIL)


def setup_inputs(seed: int = 0) -> dict:
    key = jax.random.key(seed)
    ks = jax.random.split(key, 20)
    f32 = jnp.float32
    G, P, C = SSM_GROUPS, SSM_STATE, SSM_GROUP
    nrm = lambda k, shape, sc: jax.random.normal(k, shape, f32) * sc
    lam_im_base = jnp.pi * jnp.arange(P, dtype=f32)
    return {
        "x": nrm(ks[0], (BATCH, SEQ, D_MODEL), 1.0),
        "norm_g": 1.0 + nrm(ks[1], (DEPTH, D_MODEL), 0.02),
        "w_in": nrm(ks[2], (DEPTH, D_MODEL, IN_COLS), D_MODEL ** -0.5),
        "w_out": nrm(ks[3], (DEPTH, MIX_WIDTH, D_MODEL), MIX_WIDTH ** -0.5),
        "ssm_lam_re": -0.5 + nrm(ks[4], (DEPTH, N_DIRS, G, P), 0.01),
        "ssm_lam_im": lam_im_base + nrm(ks[5], (DEPTH, N_DIRS, G, P), 0.01),
        "ssm_log_dt": jax.random.uniform(ks[6], (DEPTH, N_DIRS, G), f32, np.log(DT_MIN), np.log(DT_MAX)),
        "ssm_b_re": nrm(ks[7], (DEPTH, N_DIRS, G, P, C), (2 * C) ** -0.5),
        "ssm_b_im": nrm(ks[8], (DEPTH, N_DIRS, G, P, C), (2 * C) ** -0.5),
        "ssm_c_re": nrm(ks[9], (DEPTH, N_DIRS, G, C, P), P ** -0.5),
        "ssm_c_im": nrm(ks[10], (DEPTH, N_DIRS, G, C, P), P ** -0.5),
        "ssm_d": nrm(ks[11], (DEPTH, W_SSM), 1.0),
        "glu_w": nrm(ks[12], (DEPTH, W_SSM, W_SSM), W_SSM ** -0.5),
        "glu_b": nrm(ks[13], (DEPTH, W_SSM), 0.01),
        "na_rpb": nrm(ks[14], (DEPTH, NA_HEADS, 2 * NA_MAX_ROWS - 1, 2 * NA_COLS - 1), 0.1),
        "t5_bias": nrm(ks[15], (T5_BUCKETS, DIL_HEADS), 0.1),
        "final_g": 1.0 + nrm(ks[16], (D_MODEL,), 0.02),
    }


def reference(x, norm_g, w_in, w_out, ssm_lam_re, ssm_lam_im, ssm_log_dt, ssm_b_re, ssm_b_im,
              ssm_c_re, ssm_c_im, ssm_d, glu_w, glu_b, na_rpb, t5_bias, final_g):
    split_points = [int(v) for v in np.cumsum(SPLIT_SIZES)[:-1]]
    for l in range(DEPTH):
        h = rmsnorm(x, norm_g[l])
        proj = h @ w_in[l]
        xa, za, qb, kb, vb, zb, qc, kc, vc, zc = jnp.split(proj, split_points, axis=-1)
        ya = ssm_branch(xa, ssm_lam_re[l], ssm_lam_im[l], ssm_log_dt[l], ssm_b_re[l], ssm_b_im[l],
                        ssm_c_re[l], ssm_c_im[l], ssm_d[l], glu_w[l], glu_b[l]) * jax.nn.silu(za)
        yb = na_branch(qb, kb, vb, na_rpb[l]) * jax.nn.silu(zb)
        yc = dilated_branch(qc, kc, vc, t5_bias) * jax.nn.silu(zc)
        x = x + jnp.concatenate([ya, yb, yc], axis=-1) @ w_out[l]
    return rmsnorm(x, final_g)
```

```python
import functools

import numpy as np
import jax
import jax.numpy as jnp
from jax import lax
from jax.experimental import pallas as pl
from jax.experimental.pallas import tpu as pltpu

F32 = jnp.float32
BF16 = jnp.bfloat16

D_MODEL = 1024
HEAD_DIM = 64
W_BRANCH = 512
N_HEADS = W_BRANCH // HEAD_DIM
N_HEAD_PAIRS = N_HEADS // 2
SSM_GROUP = 16
SSM_GROUPS = W_BRANCH // SSM_GROUP
SSM_STATE = 64
GRID_W = 64
NA_ROWS = 8
NA_COLS = 16
DIL_PATTERNS = ((128, 1), (512, 4), (2048, 16))
DIL_HALF = 64
T5_BUCKETS = 32
T5_MAX_DIST = 1024
RMS_EPS = 1e-6
NEG_INF = -1e30
IN_COLS = 10 * W_BRANCH
MIX_WIDTH = 3 * W_BRANCH

COL_XA, COL_ZA, COL_QB, COL_KB, COL_VB, COL_ZB, COL_QC, COL_KC, COL_VC, COL_ZC = (
    4 * i for i in range(10))

LANES = 128
CHUNK = 16
STATE_PITCH = 40
VMEM_LIMIT = 56 * 1024 * 1024


def _silu(z):
    return z * (1.0 / (1.0 + jnp.exp(-z)))


def _gelu_tanh(x):
    return 0.5 * x * (1.0 + jnp.tanh(np.sqrt(2.0 / np.pi).astype(np.float32) * (x + 0.044715 * (x * x * x))))


IN_TM = 512
IN_TN = 512


def _in_proj_kernel(x_ref, g_ref, w_ref, o_ref):
    x = x_ref[...]
    ms = jnp.mean(x * x, axis=-1, keepdims=True)
    h = (x * lax.rsqrt(ms + RMS_EPS) * g_ref[...]).astype(BF16)
    for n in range(IN_COLS // IN_TN):
        o_ref[:, n * IN_TN:(n + 1) * IN_TN] = jnp.dot(
            h, w_ref[:, n * IN_TN:(n + 1) * IN_TN], preferred_element_type=F32).astype(BF16)


def _in_proj(x2d, g, w_bf16):
    rows = x2d.shape[0]
    return pl.pallas_call(
        _in_proj_kernel,
        grid=(rows // IN_TM,),
        in_specs=[
            pl.BlockSpec((IN_TM, D_MODEL), lambda i: (i, 0)),
            pl.BlockSpec((1, D_MODEL), lambda i: (0, 0)),
            pl.BlockSpec((D_MODEL, IN_COLS), lambda i: (0, 0), pipeline_mode=pl.Buffered(1)),
        ],
        out_specs=pl.BlockSpec((IN_TM, IN_COLS), lambda i: (i, 0)),
        out_shape=jax.ShapeDtypeStruct((rows, IN_COLS), BF16),
        compiler_params=pltpu.CompilerParams(
            dimension_semantics=("arbitrary",), vmem_limit_bytes=VMEM_LIMIT),
        name="in_proj",
    )(x2d, g.reshape(1, D_MODEL), w_bf16)


def _na_bias_table(rpb):
    v = np.arange(NA_ROWS)[:, None]
    rho = np.arange(NA_ROWS)[None, :]
    row_idx = rho - v + (NA_ROWS - 1)
    j = np.arange(GRID_W)[:, None]
    c = np.arange(GRID_W)[None, :]
    col_start = np.clip(j - NA_COLS // 2, 0, GRID_W - NA_COLS)
    valid = (c >= col_start) & (c < col_start + NA_COLS)
    col_idx = np.clip(c - j + (NA_COLS - 1), 0, 2 * NA_COLS - 2)
    tab = rpb[:, row_idx[:, :, None, None], col_idx[None, None, :, :]]
    tab = jnp.where(valid[None, None, None], tab.astype(F32), NEG_INF)
    tab = tab.transpose(0, 1, 3, 2, 4)
    return tab.reshape(N_HEADS, NA_ROWS, GRID_W, NA_ROWS * GRID_W)


def _na_kernel(q_ref, k_ref, v_ref, z_ref, bias_ref, o_ref):
    seq = q_ref.shape[0]
    rows = seq // GRID_W
    nkeys = NA_ROWS * GRID_W
    lane = lax.broadcasted_iota(jnp.int32, (GRID_W, LANES), 1)
    first_head = lane < HEAD_DIM

    def row_fn(r, carry):
        rs = jnp.clip(r - NA_ROWS // 2, 0, rows - NA_ROWS)
        var = r - rs
        q0 = pl.multiple_of(r * GRID_W, GRID_W)
        k0 = pl.multiple_of(rs * GRID_W, GRID_W)
        qb = q_ref[pl.ds(q0, GRID_W), :]
        kw = k_ref[pl.ds(k0, nkeys), :]
        vw = v_ref[pl.ds(k0, nkeys), :]
        outs = []
        for hd in range(2):
            keep = first_head if hd == 0 else jnp.logical_not(first_head)
            qm = jnp.where(keep, qb, jnp.zeros_like(qb))
            s = lax.dot_general(qm, kw, (((1,), (1,)), ((), ())), preferred_element_type=F32)
            s = s * (HEAD_DIM ** -0.5) + bias_ref[hd, var]
            m = jnp.max(s, axis=-1, keepdims=True)
            p = jnp.exp(s - m)
            den = jnp.sum(p, axis=-1, keepdims=True)
            acc = jnp.dot(p.astype(BF16), vw, preferred_element_type=F32)
            outs.append(acc * (1.0 / den))
        y = jnp.where(first_head, outs[0], outs[1])
        z = z_ref[pl.ds(q0, GRID_W), :].astype(F32)
        o_ref[pl.ds(q0, GRID_W), :] = (y * _silu(z)).astype(BF16)
        return carry

    lax.fori_loop(0, rows, row_fn, 0)


def _na(proj, bias, batch, seq):
    def col(c0):
        return pl.BlockSpec((seq, LANES), lambda b, p, c0=c0: (b, c0 + p))

    return pl.pallas_call(
        _na_kernel,
        grid=(batch, N_HEAD_PAIRS),
        in_specs=[
            col(COL_QB), col(COL_KB), col(COL_VB), col(COL_ZB),
            pl.BlockSpec((2, NA_ROWS, GRID_W, NA_ROWS * GRID_W), lambda b, p: (p, 0, 0, 0)),
        ],
        out_specs=pl.BlockSpec((seq, LANES), lambda b, p: (b, p)),
        out_shape=jax.ShapeDtypeStruct((batch * seq, W_BRANCH), BF16),
        compiler_params=pltpu.CompilerParams(
            dimension_semantics=("arbitrary", "arbitrary"), vmem_limit_bytes=VMEM_LIMIT),
        name="na_attn",
    )(proj, proj, proj, proj, bias)


DIL_QB = 128
DIL_KW = 256
N_VARIANTS = 3


def _t5_bucket(rel):
    nb = T5_BUCKETS // 2
    max_exact = nb // 2
    n = np.abs(rel)
    large = max_exact + (np.log(np.maximum(n, 1) / max_exact) / np.log(T5_MAX_DIST / max_exact)
                         * (nb - max_exact)).astype(np.int32)
    large = np.minimum(large, nb - 1)
    return (np.where(rel > 0, nb, 0) + np.where(n < max_exact, n, large)).astype(np.int32)


def _dil_bias_table(t5_bias):
    qq = np.arange(DIL_QB)[:, None]
    kk = np.arange(DIL_KW)[None, :]
    idx, valid = [], []
    for _, d in DIL_PATTERNS:
        idx_v, valid_v = [], []
        for var in range(N_VARIANTS):
            step = kk - qq - DIL_HALF * var
            valid_v.append(np.abs(step) <= DIL_HALF)
            idx_v.append(_t5_bucket(d * np.clip(step, -DIL_HALF, DIL_HALF)))
        idx.append(np.stack(idx_v))
        valid.append(np.stack(valid_v))
    idx = np.stack(idx)
    valid = np.stack(valid)
    tab = t5_bias.astype(F32)[idx]
    tab = jnp.where(valid[..., None], tab, NEG_INF)
    return tab.transpose(4, 0, 1, 2, 3)


def _dil_kernel(q_ref, k_ref, v_ref, z_ref, bias_ref, o_ref, qf, kf, vf, m_s, l_s, acc_s):
    seq = q_ref.shape[0]
    conv_rows = 256
    lane = lax.broadcasted_iota(jnp.int32, (DIL_QB, LANES), 1)
    first_head = lane < HEAD_DIM

    def conv(i, carry):
        r0 = pl.multiple_of(i * conv_rows, conv_rows)
        qf[pl.ds(r0, conv_rows), :] = q_ref[pl.ds(r0, conv_rows), :].astype(F32) * (HEAD_DIM ** -0.5)
        kf[pl.ds(r0, conv_rows), :] = k_ref[pl.ds(r0, conv_rows), :].astype(F32)
        vf[pl.ds(r0, conv_rows), :] = v_ref[pl.ds(r0, conv_rows), :].astype(F32)
        return carry

    lax.fori_loop(0, seq // conv_rows, conv, 0)

    for pi, (_, d) in enumerate(DIL_PATTERNS):
        n_sub = seq // d
        nblk = n_sub // DIL_QB

        def rows(start, size, d=d):
            return pl.ds(start, size) if d == 1 else pl.ds(start, size, stride=d)

        def unit(u, carry, pi=pi, d=d, n_sub=n_sub, nblk=nblk, rows=rows):
            res = u // nblk
            i0 = (u % nblk) * DIL_QB
            ks = jnp.clip(i0 - DIL_HALF, 0, n_sub - DIL_KW)
            var = (i0 - ks) // DIL_HALF
            q_rows = rows(res + d * i0, DIL_QB)
            k_rows = rows(res + d * ks, DIL_KW)
            qb = qf[q_rows, :]
            kw = kf[k_rows, :].astype(BF16)
            vw = vf[k_rows, :].astype(BF16)
            ms, ls, accs = [], [], []
            for hd in range(2):
                keep = first_head if hd == 0 else jnp.logical_not(first_head)
                qm = jnp.where(keep, qb, 0.0).astype(BF16)
                s = lax.dot_general(qm, kw, (((1,), (1,)), ((), ())), preferred_element_type=F32)
                s = s + bias_ref[hd, pi, var]
                m = jnp.max(s, axis=-1, keepdims=True)
                p = jnp.exp(s - m)
                ls.append(jnp.sum(p, axis=-1, keepdims=True))
                ms.append(m)
                accs.append(jnp.dot(p.astype(BF16), vw, preferred_element_type=F32))
            m_cur = jnp.where(first_head, ms[0], ms[1])
            l_cur = jnp.where(first_head, ls[0], ls[1])
            acc_cur = jnp.where(first_head, accs[0], accs[1])
            if pi == 0:
                m_s[q_rows, :] = m_cur
                l_s[q_rows, :] = l_cur
                acc_s[q_rows, :] = acc_cur
            else:
                m_old = m_s[q_rows, :]
                m_new = jnp.maximum(m_old, m_cur)
                a_old = jnp.exp(m_old - m_new)
                a_cur = jnp.exp(m_cur - m_new)
                m_s[q_rows, :] = m_new
                l_s[q_rows, :] = a_old * l_s[q_rows, :] + a_cur * l_cur
                acc_s[q_rows, :] = a_old * acc_s[q_rows, :] + a_cur * acc_cur
            return carry

        lax.fori_loop(0, seq // DIL_QB, unit, 0)

    def fin(i, carry):
        r0 = pl.multiple_of(i * conv_rows, conv_rows)
        z = z_ref[pl.ds(r0, conv_rows), :].astype(F32)
        y = acc_s[pl.ds(r0, conv_rows), :] * (1.0 / l_s[pl.ds(r0, conv_rows), :])
        o_ref[pl.ds(r0, conv_rows), :] = (y * _silu(z)).astype(BF16)
        return carry

    lax.fori_loop(0, seq // conv_rows, fin, 0)


def _dil(proj, bias, batch, seq):
    def col(c0):
        return pl.BlockSpec((seq, LANES), lambda b, p, c0=c0: (b, c0 + p))

    scratch = [pltpu.VMEM((seq, LANES), F32) for _ in range(6)]
    return pl.pallas_call(
        _dil_kernel,
        grid=(batch, N_HEAD_PAIRS),
        in_specs=[
            col(COL_QC), col(COL_KC), col(COL_VC), col(COL_ZC),
            pl.BlockSpec((2, len(DIL_PATTERNS), N_VARIANTS, DIL_QB, DIL_KW),
                         lambda b, p: (p, 0, 0, 0, 0)),
        ],
        out_specs=pl.BlockSpec((seq, LANES), lambda b, p: (b, p)),
        out_shape=jax.ShapeDtypeStruct((batch * seq, W_BRANCH), BF16),
        scratch_shapes=scratch,
        compiler_params=pltpu.CompilerParams(
            dimension_semantics=("arbitrary", "arbitrary"), vmem_limit_bytes=VMEM_LIMIT),
        name="dil_attn",
    )(proj, proj, proj, proj, bias)


N_POW = CHUNK + 1
N_LAGS = 2 * CHUNK - 1
PREP_GROUPS = 8


def _ssm_prep_kernel(lre_ref, lim_ref, ldt_ref, btre_ref, btim_ref, cre_ref, cim_ref,
                     lbb_re_ref, lbb_im_ref, cl_re_ref, cl_nim_ref, pw_re_ref, pw_im_ref, klag_ref):
    zpad = jnp.zeros((LANES - SSM_GROUP, SSM_STATE), F32)
    nt = (((1,), (1,)), ((), ()))

    def group(g, carry):
        lag0 = None
        for dr in range(2):
            lam_re = lre_ref[0, dr, g]
            lam_im = lim_ref[0, dr, g]
            dt = jnp.exp(ldt_ref[0, dr, g])
            mag = jnp.exp(lam_re * dt)
            lb_re = mag * jnp.cos(lam_im * dt)
            lb_im = mag * jnp.sin(lam_im * dt)
            n_re = lb_re - 1.0
            den = lam_re * lam_re + lam_im * lam_im
            q_re = (n_re * lam_re + lb_im * lam_im) / den
            q_im = (lb_im * lam_re - n_re * lam_im) / den
            bt_re = btre_ref[0, dr, g]
            bt_im = btim_ref[0, dr, g]
            bb_re = q_re * bt_re - q_im * bt_im
            bb_im = q_re * bt_im + q_im * bt_re
            bb_re_pad = jnp.concatenate([bb_re, zpad], axis=0)
            bb_im_pad = jnp.concatenate([bb_im, zpad], axis=0)
            c_re = cre_ref[0, dr, g]
            c_im = cim_ref[0, dr, g]
            pw_re = jnp.ones_like(lb_re)
            pw_im = jnp.zeros_like(lb_re)
            for j in range(N_POW):
                lbb_re_ref[0, dr, g, j] = pw_re * bb_re - pw_im * bb_im
                lbb_im_ref[0, dr, g, j] = pw_re * bb_im + pw_im * bb_re
                y_re = pw_re * c_re - pw_im * c_im
                y_im = pw_re * c_im + pw_im * c_re
                cl_re_ref[0, dr, g, j] = y_re
                cl_nim_ref[0, dr, g, j] = -y_im
                if j < CHUNK:
                    lag = (lax.dot_general(y_re, bb_re_pad, nt, precision=lax.Precision.HIGHEST,
                                           preferred_element_type=F32)
                           - lax.dot_general(y_im, bb_im_pad, nt, precision=lax.Precision.HIGHEST,
                                             preferred_element_type=F32))
                    if j == 0:
                        lag0 = lag if lag0 is None else lag0 + lag
                    else:
                        klag_ref[0, g, (CHUNK - 1) + (j if dr == 0 else -j)] = lag
                if j == CHUNK:
                    pw_re_ref[0, dr, g] = pw_re
                    pw_im_ref[0, dr, g] = pw_im
                pw_re, pw_im = pw_re * lb_re - pw_im * lb_im, pw_re * lb_im + pw_im * lb_re
        klag_ref[0, g, CHUNK - 1] = lag0
        return carry

    lax.fori_loop(0, PREP_GROUPS, group, 0)


def _ssm_perm():
    n = np.arange(CHUNK * SSM_GROUP)
    half, seg, ch = n // LANES, (n % LANES) // SSM_GROUP, n % SSM_GROUP
    j = (np.arange(SSM_GROUPS) % 8)[:, None]
    step = 8 * half[None, :] + (seg[None, :] - j) % 8
    return step * SSM_GROUP + ch[None, :]


def _ssm_prep(lam_re, lam_im, log_dt, b_re, b_im, c_re, c_im):
    depth = lam_re.shape[0]
    G, P, C = SSM_GROUPS, SSM_STATE, SSM_GROUP
    rep = lambda a: jnp.broadcast_to(a.astype(F32)[:, :, :, None, :], (depth, 2, G, C, P))
    ldt = jnp.broadcast_to(log_dt.astype(F32)[:, :, :, None, None], (depth, 2, G, C, P))
    bt = lambda a: a.astype(F32).transpose(0, 1, 2, 4, 3)
    gb = PREP_GROUPS
    spec5 = pl.BlockSpec((1, 2, gb, C, P), lambda l, i: (l, 0, i, 0, 0))
    spec6 = pl.BlockSpec((1, 2, gb, N_POW, C, P), lambda l, i: (l, 0, i, 0, 0, 0))
    shp6 = jax.ShapeDtypeStruct((depth, 2, G, N_POW, C, P), F32)
    shp5 = jax.ShapeDtypeStruct((depth, 2, G, C, P), F32)
    lbb_re, lbb_im, cl_re, cl_nim, pw_re, pw_im, klag = pl.pallas_call(
        _ssm_prep_kernel,
        grid=(depth, G // gb),
        in_specs=[spec5] * 7,
        out_specs=[spec6, spec6, spec6, spec6, spec5, spec5,
                   pl.BlockSpec((1, gb, N_LAGS, C, LANES), lambda l, i: (l, i, 0, 0, 0))],
        out_shape=[shp6, shp6, shp6, shp6, shp5, shp5,
                   jax.ShapeDtypeStruct((depth, G, N_LAGS, C, LANES), F32)],
        compiler_params=pltpu.CompilerParams(
            dimension_semantics=("arbitrary", "arbitrary"), vmem_limit_bytes=VMEM_LIMIT),
        name="ssm_prep",
    )(rep(lam_re), rep(lam_im), ldt, bt(b_re), bt(b_im), c_re.astype(F32), c_im.astype(F32))

    n = CHUNK * C
    f_idx = np.arange(CHUNK - 1, -1, -1)
    b_idx = np.arange(CHUNK)
    w_in = jnp.concatenate([lbb_re[:, 0, :, f_idx], lbb_im[:, 0, :, f_idx],
                            lbb_re[:, 1, :, b_idx], lbb_im[:, 1, :, b_idx]], axis=-1)
    w_in = w_in.transpose(1, 2, 0, 3, 4).reshape(depth, G, n, 4 * P)
    fo_idx = np.arange(1, CHUNK + 1)
    bo_idx = np.arange(CHUNK, 0, -1)
    w_out = jnp.concatenate([cl_re[:, 0, :, fo_idx], cl_nim[:, 0, :, fo_idx],
                             cl_re[:, 1, :, bo_idx], cl_nim[:, 1, :, bo_idx]], axis=-1)
    w_out = w_out.transpose(1, 2, 4, 0, 3).reshape(depth, G, 4 * P, n)
    lag_idx = (CHUNK - 1) + np.arange(CHUNK)[None, :] - np.arange(CHUNK)[:, None]
    toep = klag[:, :, lag_idx, :, :C]
    toep = toep.transpose(0, 1, 2, 5, 3, 4).reshape(depth, G, n, n)
    perm = _ssm_perm()
    gi = np.arange(G)[:, None, None]
    toep = toep[:, gi, perm[:, :, None], perm[:, None, :]]
    w_in = w_in[:, gi[:, :, 0], perm]
    w_out = jnp.take_along_axis(w_out, jnp.asarray(perm)[None, :, None, :], axis=3)
    dup = lambda a: jnp.concatenate([a[:, :, :, 0, :], a[:, :, :, 0, :]], axis=-1)
    a_re, a_im = dup(pw_re), dup(pw_im)
    decay = jnp.stack([a_re[:, 0], a_im[:, 0], a_re[:, 1], a_im[:, 1]], axis=1)
    return toep.astype(BF16), w_in.astype(BF16), w_out.astype(BF16), decay


GROUPS_PER_SLAB = LANES // SSM_GROUP
N_SLABS = W_BRANCH // LANES
RELAYOUT_ROWS = 32


def _ssm_kernel(xa_ref, toep_ref, win_ref, wout_ref, decay_ref, d_ref, o_ref, xf, u_s, st, yq):
    seq = xa_ref.shape[0]
    nchunk = seq // CHUNK
    nrb = nchunk // RELAYOUT_ROWS
    gps = GROUPS_PER_SLAB
    lane_rb = lax.broadcasted_iota(jnp.int32, (RELAYOUT_ROWS, LANES), 1)
    seg_masks = [(lane_rb >= SSM_GROUP * sg) & (lane_rb < SSM_GROUP * (sg + 1)) for sg in range(gps)]
    lane8 = lax.broadcasted_iota(jnp.int32, (gps, LANES), 1)
    low_half = lane8 < SSM_STATE
    conv_rows = 512

    for q in range(N_SLABS):
        c_lo = q * LANES

        def conv(i, carry, c_lo=c_lo):
            r0 = pl.multiple_of(i * conv_rows, conv_rows)
            xf[pl.ds(r0, conv_rows), :] = xa_ref[pl.ds(r0, conv_rows), c_lo:c_lo + LANES].astype(F32)
            return carry

        lax.fori_loop(0, seq // conv_rows, conv, 0)

        for hh in range(2):
            def fwd_relayout(rb, carry, hh=hh):
                c0 = pl.multiple_of(rb * RELAYOUT_ROWS, RELAYOUT_ROWS)
                rolled = []
                for r in range(gps):
                    xs = xf[pl.ds(CHUNK * c0 + gps * hh + r, RELAYOUT_ROWS, stride=CHUNK), :]
                    rolled.append(xs if r == 0 else pltpu.roll(xs, SSM_GROUP * r, axis=1))
                for j in range(gps):
                    out = rolled[(0 - j) % gps]
                    for sg in range(1, gps):
                        out = jnp.where(seg_masks[sg], rolled[(sg - j) % gps], out)
                    u_s[j, pl.ds(c0, RELAYOUT_ROWS), hh * LANES:(hh + 1) * LANES] = out.astype(BF16)
                return carry

            lax.fori_loop(0, nrb, fwd_relayout, 0)

        for j in range(gps):
            s_in = jnp.dot(u_s[j], win_ref[gps * q + j], preferred_element_type=F32)
            st[0, pl.ds(j, nchunk, stride=gps), :] = s_in[:, :LANES]
            st[1, pl.ds(j, nchunk, stride=gps), :] = s_in[:, LANES:]

        g_lo = gps * q
        af_re = decay_ref[0, g_lo:g_lo + gps, :]
        af_im = decay_ref[1, g_lo:g_lo + gps, :]
        ab_re = decay_ref[2, g_lo:g_lo + gps, :]
        ab_im = decay_ref[3, g_lo:g_lo + gps, :]

        def scan(i, carry):
            hrf, hif, hrb, hib = carry
            rf = pl.multiple_of(i * gps, gps)
            rb_ = pl.multiple_of((nchunk - 1 - i) * gps, gps)
            sf = st[0, pl.ds(rf, gps), :]
            sb = st[1, pl.ds(rb_, gps), :]
            st[0, pl.ds(rf, gps), :] = jnp.where(low_half, hrf, pltpu.roll(hif, SSM_STATE, axis=1))
            st[1, pl.ds(rb_, gps), :] = jnp.where(low_half, hrb, pltpu.roll(hib, SSM_STATE, axis=1))
            sf_im = pltpu.roll(sf, SSM_STATE, axis=1)
            sb_im = pltpu.roll(sb, SSM_STATE, axis=1)
            return (af_re * hrf - af_im * hif + sf, af_re * hif + af_im * hrf + sf_im,
                    ab_re * hrb - ab_im * hib + sb, ab_re * hib + ab_im * hrb + sb_im)

        zero = jnp.zeros((gps, LANES), F32)
        lax.fori_loop(0, nchunk, scan, (zero, zero, zero, zero))

        for j in range(gps):
            g = gps * q + j
            h_f = st[0, pl.ds(j, nchunk, stride=gps), :].astype(BF16)
            h_b = st[1, pl.ds(j, nchunk, stride=gps), :].astype(BF16)
            yq[j] = (jnp.dot(u_s[j], toep_ref[g], preferred_element_type=F32)
                     + jnp.dot(h_f, wout_ref[g, :LANES, :], preferred_element_type=F32)
                     + jnp.dot(h_b, wout_ref[g, LANES:, :], preferred_element_type=F32))

        d_row = d_ref[:, c_lo:c_lo + LANES]
        for hh in range(2):
            def bwd_relayout(rb, carry, hh=hh, d_row=d_row, q=q):
                c0 = pl.multiple_of(rb * RELAYOUT_ROWS, RELAYOUT_ROWS)
                ys = [yq[j, pl.ds(c0, RELAYOUT_ROWS), hh * LANES:(hh + 1) * LANES] for j in range(gps)]
                for r in range(gps):
                    merged = ys[(0 - r) % gps]
                    for sg in range(1, gps):
                        merged = jnp.where(seg_masks[sg], ys[(sg - r) % gps], merged)
                    if r:
                        merged = pltpu.roll(merged, LANES - SSM_GROUP * r, axis=1)
                    tok = pl.ds(CHUNK * c0 + gps * hh + r, RELAYOUT_ROWS, stride=CHUNK)
                    o_ref[q, tok, :] = merged + d_row * xf[tok, :]
                return carry

            lax.fori_loop(0, nrb, bwd_relayout, 0)


def _ssm(proj, toep, w_in, w_out, decay, d_skip, batch, seq):
    nchunk = seq // CHUNK
    n = CHUNK * SSM_GROUP
    wspec = pl.BlockSpec((SSM_GROUPS, n, n), lambda b: (0, 0, 0), pipeline_mode=pl.Buffered(1))
    return pl.pallas_call(
        _ssm_kernel,
        grid=(batch,),
        in_specs=[
            pl.BlockSpec((seq, W_BRANCH), lambda b: (b, COL_XA // N_SLABS)),
            wspec, wspec, wspec,
            pl.BlockSpec((4, SSM_GROUPS, LANES), lambda b: (0, 0, 0)),
            pl.BlockSpec((1, W_BRANCH), lambda b: (0, 0)),
        ],
        out_specs=pl.BlockSpec((N_SLABS, seq, LANES), lambda b: (0, b, 0)),
        out_shape=jax.ShapeDtypeStruct((N_SLABS, batch * seq, LANES), F32),
        scratch_shapes=[
            pltpu.VMEM((seq, LANES), F32),
            pltpu.VMEM((GROUPS_PER_SLAB, nchunk, n), BF16),
            pltpu.VMEM((2, nchunk * GROUPS_PER_SLAB, LANES), F32),
            pltpu.VMEM((GROUPS_PER_SLAB, nchunk, n), F32),
        ],
        compiler_params=pltpu.CompilerParams(
            dimension_semantics=("arbitrary",), vmem_limit_bytes=VMEM_LIMIT),
        name="ssm_mix",
    )(proj, toep, w_in, w_out, decay, d_skip.astype(F32).reshape(1, W_BRANCH))


OUT_TM = 512


def _out_proj_kernel(ya_ref, za_ref, yb_ref, yc_ref, x_ref, w_ref, gw_ref, gb_ref, fg_ref, o_ref, *, final):
    y = jnp.concatenate([ya_ref[i] for i in range(N_SLABS)], axis=-1)
    g = _gelu_tanh(y)
    gate = jnp.dot(g.astype(BF16), gw_ref[...], preferred_element_type=F32) + gb_ref[...]
    ya = g * (1.0 / (1.0 + jnp.exp(-gate))) * _silu(za_ref[...].astype(F32))
    delta = (jnp.dot(ya.astype(BF16), w_ref[:W_BRANCH, :], preferred_element_type=F32)
             + jnp.dot(yb_ref[...], w_ref[W_BRANCH:2 * W_BRANCH, :], preferred_element_type=F32)
             + jnp.dot(yc_ref[...], w_ref[2 * W_BRANCH:, :], preferred_element_type=F32))
    x = x_ref[...] + delta
    if final:
        ms = jnp.mean(x * x, axis=-1, keepdims=True)
        x = x * lax.rsqrt(ms + RMS_EPS) * fg_ref[...]
    o_ref[...] = x


def _out_proj(ya_pre, proj, yb, yc, x2d, w_bf16, glu_w_bf16, glu_b, final_g, final):
    rows = x2d.shape[0]
    row_blk = lambda width: pl.BlockSpec((OUT_TM, width), lambda i: (i, 0))
    const = lambda shape: pl.BlockSpec(shape, lambda i: (0,) * len(shape), pipeline_mode=pl.Buffered(1))
    return pl.pallas_call(
        functools.partial(_out_proj_kernel, final=final),
        grid=(rows // OUT_TM,),
        in_specs=[
            pl.BlockSpec((N_SLABS, OUT_TM, LANES), lambda i: (0, i, 0)),
            pl.BlockSpec((OUT_TM, W_BRANCH), lambda i: (i, COL_ZA // N_SLABS)),
            row_blk(W_BRANCH), row_blk(W_BRANCH), row_blk(D_MODEL),
            const((MIX_WIDTH, D_MODEL)), const((W_BRANCH, W_BRANCH)),
            const((1, W_BRANCH)), const((1, D_MODEL)),
        ],
        out_specs=row_blk(D_MODEL),
        out_shape=jax.ShapeDtypeStruct((rows, D_MODEL), F32),
        compiler_params=pltpu.CompilerParams(
            dimension_semantics=("arbitrary",), vmem_limit_bytes=VMEM_LIMIT),
        name="out_proj_final" if final else "out_proj",
    )(ya_pre, proj, yb, yc, x2d, w_bf16, glu_w_bf16,
      glu_b.astype(F32).reshape(1, W_BRANCH), final_g.astype(F32).reshape(1, D_MODEL))


def kernel(x, norm_g, w_in, w_out, ssm_lam_re, ssm_lam_im, ssm_log_dt, ssm_b_re, ssm_b_im, ssm_c_re,
           ssm_c_im, ssm_d, glu_w, glu_b, na_rpb, t5_bias, final_g):
    batch, seq, _ = x.shape
    depth = w_in.shape[0]
    x2d = x.astype(F32).reshape(batch * seq, D_MODEL)
    toep, s_in, s_out, decay = _ssm_prep(ssm_lam_re, ssm_lam_im, ssm_log_dt, ssm_b_re, ssm_b_im,
                                         ssm_c_re, ssm_c_im)
    dil_bias = _dil_bias_table(t5_bias)
    for l in range(depth):
        proj = _in_proj(x2d, norm_g[l].astype(F32), w_in[l].astype(BF16))
        ya_pre = _ssm(proj, toep[l], s_in[l], s_out[l], decay[l], ssm_d[l], batch, seq)
        yb = _na(proj, _na_bias_table(na_rpb[l]), batch, seq)
        yc = _dil(proj, dil_bias, batch, seq)
        x2d = _out_proj(ya_pre, proj, yb, yc, x2d, w_out[l].astype(BF16), glu_w[l].astype(BF16),
                        glu_b[l], final_g, final=(l == depth - 1))
    return x2d.reshape(batch, seq, D_MODEL).astype(x.dtype)
```

```python
import functools

import numpy as np
import jax
import jax.numpy as jnp
from jax import lax
from jax.experimental import pallas as pl
from jax.experimental.pallas import tpu as pltpu

F32 = jnp.float32
BF16 = jnp.bfloat16

D_MODEL = 1024
HEAD_DIM = 64
W_BRANCH = 512
N_HEADS = W_BRANCH // HEAD_DIM
N_HEAD_PAIRS = N_HEADS // 2
SSM_GROUP = 16
SSM_GROUPS = W_BRANCH // SSM_GROUP
SSM_STATE = 64
GRID_W = 64
NA_ROWS = 8
NA_COLS = 16
DIL_PATTERNS = ((128, 1), (512, 4), (2048, 16))
DIL_HALF = 64
T5_BUCKETS = 32
T5_MAX_DIST = 1024
RMS_EPS = 1e-6
NEG_INF = -1e30
IN_COLS = 10 * W_BRANCH
MIX_WIDTH = 3 * W_BRANCH

COL_XA, COL_ZA, COL_QB, COL_KB, COL_VB, COL_ZB, COL_QC, COL_KC, COL_VC, COL_ZC = (
    4 * i for i in range(10))

LANES = 128
CHUNK = 16
STATE_PITCH = 40
VMEM_LIMIT = 56 * 1024 * 1024


def _silu(z):
    return z * (1.0 / (1.0 + jnp.exp(-z)))


def _toeplitz_rows(vec, n_rows, n_cols):
    m = vec.shape[-1]
    assert m >= n_rows + n_cols - 1
    tiled = jnp.tile(vec, (1,) * (vec.ndim - 1) + (n_rows,))[..., :n_rows * (m - 1)]
    return tiled.reshape(vec.shape[:-1] + (n_rows, m - 1))[..., :n_cols]


def _gelu_tanh(x):
    return 0.5 * x * (1.0 + jnp.tanh(np.sqrt(2.0 / np.pi).astype(np.float32) * (x + 0.044715 * (x * x * x))))


IN_TM = 512
IN_TN = 512


def _in_proj_kernel(x_ref, g_ref, w_ref, o_ref):
    x = x_ref[...]
    ms = jnp.mean(x * x, axis=-1, keepdims=True)
    h = (x * lax.rsqrt(ms + RMS_EPS) * g_ref[...]).astype(BF16)
    for n in range(IN_COLS // IN_TN):
        o_ref[:, n * IN_TN:(n + 1) * IN_TN] = jnp.dot(
            h, w_ref[:, n * IN_TN:(n + 1) * IN_TN], preferred_element_type=F32).astype(BF16)


def _in_proj(x2d, g, w_bf16):
    rows = x2d.shape[0]
    return pl.pallas_call(
        _in_proj_kernel,
        grid=(rows // IN_TM,),
        in_specs=[
            pl.BlockSpec((IN_TM, D_MODEL), lambda i: (i, 0)),
            pl.BlockSpec((1, D_MODEL), lambda i: (0, 0)),
            pl.BlockSpec((D_MODEL, IN_COLS), lambda i: (0, 0), pipeline_mode=pl.Buffered(1)),
        ],
        out_specs=pl.BlockSpec((IN_TM, IN_COLS), lambda i: (i, 0)),
        out_shape=jax.ShapeDtypeStruct((rows, IN_COLS), BF16),
        compiler_params=pltpu.CompilerParams(
            dimension_semantics=("arbitrary",), vmem_limit_bytes=VMEM_LIMIT),
        name="in_proj",
    )(x2d, g.reshape(1, D_MODEL), w_bf16)


def _na_bias_table(rpb):
    rpb = rpb.astype(F32)
    rows = jnp.stack([rpb[:, NA_ROWS - 1 - v:2 * NA_ROWS - 1 - v] for v in range(NA_ROWS)], axis=1)
    vec = jnp.concatenate([rows[..., NA_COLS - 1:],
                           jnp.zeros(rows.shape[:-1] + (2 * GRID_W - 2 * NA_COLS + 1,), F32),
                           rows[..., :NA_COLS - 1]], axis=-1)
    tab = _toeplitz_rows(vec, GRID_W, GRID_W)
    j = np.arange(GRID_W)[:, None]
    c = np.arange(GRID_W)[None, :]
    col_start = np.clip(j - NA_COLS // 2, 0, GRID_W - NA_COLS)
    valid = (c >= col_start) & (c < col_start + NA_COLS)
    tab = jnp.where(valid[None, None, None], tab, NEG_INF)
    tab = tab.transpose(0, 1, 3, 2, 4)
    return tab.reshape(N_HEADS, NA_ROWS, GRID_W, NA_ROWS * GRID_W)


def _na_kernel(q_ref, k_ref, v_ref, z_ref, bias_ref, o_ref):
    seq = q_ref.shape[0]
    rows = seq // GRID_W
    nkeys = NA_ROWS * GRID_W
    lane = lax.broadcasted_iota(jnp.int32, (GRID_W, LANES), 1)
    first_head = lane < HEAD_DIM

    def row_fn(r, carry):
        rs = jnp.clip(r - NA_ROWS // 2, 0, rows - NA_ROWS)
        var = r - rs
        q0 = pl.multiple_of(r * GRID_W, GRID_W)
        k0 = pl.multiple_of(rs * GRID_W, GRID_W)
        qb = q_ref[pl.ds(q0, GRID_W), :]
        kw = k_ref[pl.ds(k0, nkeys), :]
        vw = v_ref[pl.ds(k0, nkeys), :]
        outs = []
        for hd in range(2):
            keep = first_head if hd == 0 else jnp.logical_not(first_head)
            qm = jnp.where(keep, qb, jnp.zeros_like(qb))
            s = lax.dot_general(qm, kw, (((1,), (1,)), ((), ())), preferred_element_type=F32)
            s = s * (HEAD_DIM ** -0.5) + bias_ref[hd, var]
            m = jnp.max(s, axis=-1, keepdims=True)
            p = jnp.exp(s - m)
            den = jnp.sum(p, axis=-1, keepdims=True)
            acc = jnp.dot(p.astype(BF16), vw, preferred_element_type=F32)
            outs.append(acc * (1.0 / den))
        y = jnp.where(first_head, outs[0], outs[1])
        z = z_ref[pl.ds(q0, GRID_W), :].astype(F32)
        o_ref[pl.ds(q0, GRID_W), :] = (y * _silu(z)).astype(BF16)
        return carry

    lax.fori_loop(0, rows, row_fn, 0)


def _na(proj, bias, batch, seq):
    def col(c0):
        return pl.BlockSpec((seq, LANES), lambda b, p, c0=c0: (b, c0 + p))

    return pl.pallas_call(
        _na_kernel,
        grid=(batch, N_HEAD_PAIRS),
        in_specs=[
            col(COL_QB), col(COL_KB), col(COL_VB), col(COL_ZB),
            pl.BlockSpec((2, NA_ROWS, GRID_W, NA_ROWS * GRID_W), lambda b, p: (p, 0, 0, 0)),
        ],
        out_specs=pl.BlockSpec((seq, LANES), lambda b, p: (b, p)),
        out_shape=jax.ShapeDtypeStruct((batch * seq, W_BRANCH), BF16),
        compiler_params=pltpu.CompilerParams(
            dimension_semantics=("arbitrary", "arbitrary"), vmem_limit_bytes=VMEM_LIMIT),
        name="na_attn",
    )(proj, proj, proj, proj, bias)


DIL_QB = 128
DIL_KW = 256
N_VARIANTS = 3


def _t5_bucket(rel):
    nb = T5_BUCKETS // 2
    max_exact = nb // 2
    n = np.abs(rel)
    large = max_exact + (np.log(np.maximum(n, 1) / max_exact) / np.log(T5_MAX_DIST / max_exact)
                         * (nb - max_exact)).astype(np.int32)
    large = np.minimum(large, nb - 1)
    return (np.where(rel > 0, nb, 0) + np.where(n < max_exact, n, large)).astype(np.int32)


def _dil_bias_table(t5_bias):
    steps = np.arange(-DIL_HALF, DIL_HALF + 1)
    bucket = np.stack([_t5_bucket(d * steps) for _, d in DIL_PATTERNS])
    band = t5_bias.astype(F32)[bucket].transpose(2, 0, 1)
    pad = DIL_QB + DIL_KW - 1 - DIL_HALF
    full = jnp.pad(band, ((0, 0), (0, 0), (pad, pad)), constant_values=NEG_INF)
    zero_at = pad + DIL_HALF
    m = DIL_QB + DIL_KW
    tabs = []
    for var in range(N_VARIANTS):
        lo = zero_at - DIL_HALF * var
        vec = jnp.concatenate([full[..., lo:lo + DIL_KW],
                               jnp.full(band.shape[:-1] + (1,), NEG_INF, F32),
                               full[..., lo - (DIL_QB - 1):lo]], axis=-1)
        tabs.append(_toeplitz_rows(vec, DIL_QB, DIL_KW))
    return jnp.stack(tabs, axis=2)


def _dil_kernel(q_ref, k_ref, v_ref, z_ref, bias_ref, o_ref, qf, kf, vf, m_s, l_s, acc_s):
    seq = q_ref.shape[0]
    conv_rows = 256
    lane = lax.broadcasted_iota(jnp.int32, (DIL_QB, LANES), 1)
    first_head = lane < HEAD_DIM

    def conv(i, carry):
        r0 = pl.multiple_of(i * conv_rows, conv_rows)
        qf[pl.ds(r0, conv_rows), :] = q_ref[pl.ds(r0, conv_rows), :].astype(F32) * (HEAD_DIM ** -0.5)
        kf[pl.ds(r0, conv_rows), :] = k_ref[pl.ds(r0, conv_rows), :].astype(F32)
        vf[pl.ds(r0, conv_rows), :] = v_ref[pl.ds(r0, conv_rows), :].astype(F32)
        return carry

    lax.fori_loop(0, seq // conv_rows, conv, 0)

    for pi, (_, d) in enumerate(DIL_PATTERNS):
        n_sub = seq // d
        nblk = n_sub // DIL_QB

        def rows(start, size, d=d):
            return pl.ds(start, size) if d == 1 else pl.ds(start, size, stride=d)

        def unit(u, carry, pi=pi, d=d, n_sub=n_sub, nblk=nblk, rows=rows):
            res = u // nblk
            i0 = (u % nblk) * DIL_QB
            ks = jnp.clip(i0 - DIL_HALF, 0, n_sub - DIL_KW)
            var = (i0 - ks) // DIL_HALF
            q_rows = rows(res + d * i0, DIL_QB)
            k_rows = rows(res + d * ks, DIL_KW)
            qb = qf[q_rows, :]
            kw = kf[k_rows, :].astype(BF16)
            vw = vf[k_rows, :].astype(BF16)
            ms, ls, accs = [], [], []
            for hd in range(2):
                keep = first_head if hd == 0 else jnp.logical_not(first_head)
                qm = jnp.where(keep, qb, 0.0).astype(BF16)
                s = lax.dot_general(qm, kw, (((1,), (1,)), ((), ())), preferred_element_type=F32)
                s = s + bias_ref[hd, pi, var]
                m = jnp.max(s, axis=-1, keepdims=True)
                p = jnp.exp(s - m)
                ls.append(jnp.sum(p, axis=-1, keepdims=True))
                ms.append(m)
                accs.append(jnp.dot(p.astype(BF16), vw, preferred_element_type=F32))
            m_cur = jnp.where(first_head, ms[0], ms[1])
            l_cur = jnp.where(first_head, ls[0], ls[1])
            acc_cur = jnp.where(first_head, accs[0], accs[1])
            if pi == 0:
                m_s[q_rows, :] = m_cur
                l_s[q_rows, :] = l_cur
                acc_s[q_rows, :] = acc_cur
            else:
                m_old = m_s[q_rows, :]
                m_new = jnp.maximum(m_old, m_cur)
                a_old = jnp.exp(m_old - m_new)
                a_cur = jnp.exp(m_cur - m_new)
                m_s[q_rows, :] = m_new
                l_s[q_rows, :] = a_old * l_s[q_rows, :] + a_cur * l_cur
                acc_s[q_rows, :] = a_old * acc_s[q_rows, :] + a_cur * acc_cur
            return carry

        lax.fori_loop(0, seq // DIL_QB, unit, 0)

    def fin(i, carry):
        r0 = pl.multiple_of(i * conv_rows, conv_rows)
        z = z_ref[pl.ds(r0, conv_rows), :].astype(F32)
        y = acc_s[pl.ds(r0, conv_rows), :] * (1.0 / l_s[pl.ds(r0, conv_rows), :])
        o_ref[pl.ds(r0, conv_rows), :] = (y * _silu(z)).astype(BF16)
        return carry

    lax.fori_loop(0, seq // conv_rows, fin, 0)


def _dil(proj, bias, batch, seq):
    def col(c0):
        return pl.BlockSpec((seq, LANES), lambda b, p, c0=c0: (b, c0 + p))

    scratch = [pltpu.VMEM((seq, LANES), F32) for _ in range(6)]
    return pl.pallas_call(
        _dil_kernel,
        grid=(batch, N_HEAD_PAIRS),
        in_specs=[
            col(COL_QC), col(COL_KC), col(COL_VC), col(COL_ZC),
            pl.BlockSpec((2, len(DIL_PATTERNS), N_VARIANTS, DIL_QB, DIL_KW),
                         lambda b, p: (p, 0, 0, 0, 0)),
        ],
        out_specs=pl.BlockSpec((seq, LANES), lambda b, p: (b, p)),
        out_shape=jax.ShapeDtypeStruct((batch * seq, W_BRANCH), BF16),
        scratch_shapes=scratch,
        compiler_params=pltpu.CompilerParams(
            dimension_semantics=("arbitrary", "arbitrary"), vmem_limit_bytes=VMEM_LIMIT),
        name="dil_attn",
    )(proj, proj, proj, proj, bias)


N_POW = CHUNK + 1
N_LAGS = 2 * CHUNK - 1
PREP_GROUPS = 8


def _ssm_prep_kernel(lre_ref, lim_ref, ldt_ref, btre_ref, btim_ref, cre_ref, cim_ref,
                     lbb_re_ref, lbb_im_ref, cl_re_ref, cl_nim_ref, pw_re_ref, pw_im_ref, klag_ref):
    zpad = jnp.zeros((LANES - SSM_GROUP, SSM_STATE), F32)
    nt = (((1,), (1,)), ((), ()))

    def group(g, carry):
        lag0 = None
        for dr in range(2):
            lam_re = lre_ref[0, dr, g]
            lam_im = lim_ref[0, dr, g]
            dt = jnp.exp(ldt_ref[0, dr, g])
            mag = jnp.exp(lam_re * dt)
            lb_re = mag * jnp.cos(lam_im * dt)
            lb_im = mag * jnp.sin(lam_im * dt)
            n_re = lb_re - 1.0
            den = lam_re * lam_re + lam_im * lam_im
            q_re = (n_re * lam_re + lb_im * lam_im) / den
            q_im = (lb_im * lam_re - n_re * lam_im) / den
            bt_re = btre_ref[0, dr, g]
            bt_im = btim_ref[0, dr, g]
            bb_re = q_re * bt_re - q_im * bt_im
            bb_im = q_re * bt_im + q_im * bt_re
            bb_re_pad = jnp.concatenate([bb_re, zpad], axis=0)
            bb_im_pad = jnp.concatenate([bb_im, zpad], axis=0)
            c_re = cre_ref[0, dr, g]
            c_im = cim_ref[0, dr, g]
            pw_re = jnp.ones_like(lb_re)
            pw_im = jnp.zeros_like(lb_re)
            for j in range(N_POW):
                lbb_re_ref[0, dr, g, j] = pw_re * bb_re - pw_im * bb_im
                lbb_im_ref[0, dr, g, j] = pw_re * bb_im + pw_im * bb_re
                y_re = pw_re * c_re - pw_im * c_im
                y_im = pw_re * c_im + pw_im * c_re
                cl_re_ref[0, dr, g, j] = y_re
                cl_nim_ref[0, dr, g, j] = -y_im
                if j < CHUNK:
                    lag = (lax.dot_general(y_re, bb_re_pad, nt, precision=lax.Precision.HIGHEST,
                                           preferred_element_type=F32)
                           - lax.dot_general(y_im, bb_im_pad, nt, precision=lax.Precision.HIGHEST,
                                             preferred_element_type=F32))
                    if j == 0:
                        lag0 = lag if lag0 is None else lag0 + lag
                    else:
                        klag_ref[0, g, (CHUNK - 1) + (j if dr == 0 else -j)] = lag
                if j == CHUNK:
                    pw_re_ref[0, dr, g] = pw_re
                    pw_im_ref[0, dr, g] = pw_im
                pw_re, pw_im = pw_re * lb_re - pw_im * lb_im, pw_re * lb_im + pw_im * lb_re
        klag_ref[0, g, CHUNK - 1] = lag0
        return carry

    lax.fori_loop(0, PREP_GROUPS, group, 0)


def _ssm_chunk_order(w, axis):
    shape = w.shape
    per = LANES // SSM_GROUP
    w = w.reshape(shape[:1] + (SSM_GROUPS // per, per) + shape[2:axis]
                  + (CHUNK // per, per, SSM_GROUP) + shape[axis + 1:])
    seg_axis = axis + 2
    w = jnp.stack([jnp.roll(w[:, :, j], j, axis=seg_axis - 1) for j in range(per)], axis=2)
    return w.reshape(shape)


def _ssm_prep(lam_re, lam_im, log_dt, b_re, b_im, c_re, c_im):
    depth = lam_re.shape[0]
    G, P, C = SSM_GROUPS, SSM_STATE, SSM_GROUP
    rep = lambda a: jnp.broadcast_to(a.astype(F32)[:, :, :, None, :], (depth, 2, G, C, P))
    ldt = jnp.broadcast_to(log_dt.astype(F32)[:, :, :, None, None], (depth, 2, G, C, P))
    bt = lambda a: a.astype(F32).transpose(0, 1, 2, 4, 3)
    gb = PREP_GROUPS
    spec5 = pl.BlockSpec((1, 2, gb, C, P), lambda l, i: (l, 0, i, 0, 0))
    spec6 = pl.BlockSpec((1, 2, gb, N_POW, C, P), lambda l, i: (l, 0, i, 0, 0, 0))
    shp6 = jax.ShapeDtypeStruct((depth, 2, G, N_POW, C, P), F32)
    shp5 = jax.ShapeDtypeStruct((depth, 2, G, C, P), F32)
    lbb_re, lbb_im, cl_re, cl_nim, pw_re, pw_im, klag = pl.pallas_call(
        _ssm_prep_kernel,
        grid=(depth, G // gb),
        in_specs=[spec5] * 7,
        out_specs=[spec6, spec6, spec6, spec6, spec5, spec5,
                   pl.BlockSpec((1, gb, N_LAGS, C, LANES), lambda l, i: (l, i, 0, 0, 0))],
        out_shape=[shp6, shp6, shp6, shp6, shp5, shp5,
                   jax.ShapeDtypeStruct((depth, G, N_LAGS, C, LANES), F32)],
        compiler_params=pltpu.CompilerParams(
            dimension_semantics=("arbitrary", "arbitrary"), vmem_limit_bytes=VMEM_LIMIT),
        name="ssm_prep",
    )(rep(lam_re), rep(lam_im), ldt, bt(b_re), bt(b_im), c_re.astype(F32), c_im.astype(F32))

    n = CHUNK * C
    flip = lambda a: a[:, :, ::-1]
    w_in = jnp.concatenate([flip(lbb_re[:, 0, :, :CHUNK]), flip(lbb_im[:, 0, :, :CHUNK]),
                            lbb_re[:, 1, :, :CHUNK], lbb_im[:, 1, :, :CHUNK]], axis=-1)
    w_in = w_in.reshape(depth, G, n, 4 * P)
    w_out = jnp.concatenate([cl_re[:, 0, :, 1:], cl_nim[:, 0, :, 1:],
                             flip(cl_re[:, 1, :, 1:]), flip(cl_nim[:, 1, :, 1:])], axis=-1)
    w_out = w_out.transpose(0, 1, 4, 2, 3).reshape(depth, G, 4 * P, n)
    toep = jnp.stack([klag[:, :, CHUNK - 1 - s:2 * CHUNK - 1 - s, :, :C] for s in range(CHUNK)],
                     axis=2)
    toep = toep.transpose(0, 1, 2, 5, 3, 4).reshape(depth, G, n, n)
    toep = _ssm_chunk_order(_ssm_chunk_order(toep, 2), 3)
    w_in = _ssm_chunk_order(w_in, 2)
    w_out = _ssm_chunk_order(w_out, 3)
    dup = lambda a: jnp.concatenate([a[:, :, :, 0, :], a[:, :, :, 0, :]], axis=-1)
    a_re, a_im = dup(pw_re), dup(pw_im)
    decay = jnp.stack([a_re[:, 0], a_im[:, 0], a_re[:, 1], a_im[:, 1]], axis=1)
    return toep.astype(BF16), w_in.astype(BF16), w_out.astype(BF16), decay


GROUPS_PER_SLAB = LANES // SSM_GROUP
N_SLABS = W_BRANCH // LANES
RELAYOUT_ROWS = 32


def _ssm_kernel(xa_ref, toep_ref, win_ref, wout_ref, decay_ref, d_ref, o_ref, xf, u_s, st, yq):
    seq = xa_ref.shape[0]
    nchunk = seq // CHUNK
    nrb = nchunk // RELAYOUT_ROWS
    gps = GROUPS_PER_SLAB
    lane_rb = lax.broadcasted_iota(jnp.int32, (RELAYOUT_ROWS, LANES), 1)
    seg_masks = [(lane_rb >= SSM_GROUP * sg) & (lane_rb < SSM_GROUP * (sg + 1)) for sg in range(gps)]
    lane8 = lax.broadcasted_iota(jnp.int32, (gps, LANES), 1)
    low_half = lane8 < SSM_STATE
    conv_rows = 512

    for q in range(N_SLABS):
        c_lo = q * LANES

        def conv(i, carry, c_lo=c_lo):
            r0 = pl.multiple_of(i * conv_rows, conv_rows)
            xf[pl.ds(r0, conv_rows), :] = xa_ref[pl.ds(r0, conv_rows), c_lo:c_lo + LANES].astype(F32)
            return carry

        lax.fori_loop(0, seq // conv_rows, conv, 0)

        for hh in range(2):
            def fwd_relayout(rb, carry, hh=hh):
                c0 = pl.multiple_of(rb * RELAYOUT_ROWS, RELAYOUT_ROWS)
                rolled = []
                for r in range(gps):
                    xs = xf[pl.ds(CHUNK * c0 + gps * hh + r, RELAYOUT_ROWS, stride=CHUNK), :]
                    rolled.append(xs if r == 0 else pltpu.roll(xs, SSM_GROUP * r, axis=1))
                for j in range(gps):
                    out = rolled[(0 - j) % gps]
                    for sg in range(1, gps):
                        out = jnp.where(seg_masks[sg], rolled[(sg - j) % gps], out)
                    u_s[j, pl.ds(c0, RELAYOUT_ROWS), hh * LANES:(hh + 1) * LANES] = out.astype(BF16)
                return carry

            lax.fori_loop(0, nrb, fwd_relayout, 0)

        for j in range(gps):
            s_in = jnp.dot(u_s[j], win_ref[gps * q + j], preferred_element_type=F32)
            st[0, pl.ds(j, nchunk, stride=gps), :] = s_in[:, :LANES]
            st[1, pl.ds(j, nchunk, stride=gps), :] = s_in[:, LANES:]

        g_lo = gps * q
        af_re = decay_ref[0, g_lo:g_lo + gps, :]
        af_im = decay_ref[1, g_lo:g_lo + gps, :]
        ab_re = decay_ref[2, g_lo:g_lo + gps, :]
        ab_im = decay_ref[3, g_lo:g_lo + gps, :]

        def scan(i, carry):
            hrf, hif, hrb, hib = carry
            rf = pl.multiple_of(i * gps, gps)
            rb_ = pl.multiple_of((nchunk - 1 - i) * gps, gps)
            sf = st[0, pl.ds(rf, gps), :]
            sb = st[1, pl.ds(rb_, gps), :]
            st[0, pl.ds(rf, gps), :] = jnp.where(low_half, hrf, pltpu.roll(hif, SSM_STATE, axis=1))
            st[1, pl.ds(rb_, gps), :] = jnp.where(low_half, hrb, pltpu.roll(hib, SSM_STATE, axis=1))
            sf_im = pltpu.roll(sf, SSM_STATE, axis=1)
            sb_im = pltpu.roll(sb, SSM_STATE, axis=1)
            return (af_re * hrf - af_im * hif + sf, af_re * hif + af_im * hrf + sf_im,
                    ab_re * hrb - ab_im * hib + sb, ab_re * hib + ab_im * hrb + sb_im)

        zero = jnp.zeros((gps, LANES), F32)
        lax.fori_loop(0, nchunk, scan, (zero, zero, zero, zero))

        for j in range(gps):
            g = gps * q + j
            h_f = st[0, pl.ds(j, nchunk, stride=gps), :].astype(BF16)
            h_b = st[1, pl.ds(j, nchunk, stride=gps), :].astype(BF16)
            yq[j] = (jnp.dot(u_s[j], toep_ref[g], preferred_element_type=F32)
                     + jnp.dot(h_f, wout_ref[g, :LANES, :], preferred_element_type=F32)
                     + jnp.dot(h_b, wout_ref[g, LANES:, :], preferred_element_type=F32))

        d_row = d_ref[:, c_lo:c_lo + LANES]
        for hh in range(2):
            def bwd_relayout(rb, carry, hh=hh, d_row=d_row, q=q):
                c0 = pl.multiple_of(rb * RELAYOUT_ROWS, RELAYOUT_ROWS)
                ys = [yq[j, pl.ds(c0, RELAYOUT_ROWS), hh * LANES:(hh + 1) * LANES] for j in range(gps)]
                for r in range(gps):
                    merged = ys[(0 - r) % gps]
                    for sg in range(1, gps):
                        merged = jnp.where(seg_masks[sg], ys[(sg - r) % gps], merged)
                    if r:
                        merged = pltpu.roll(merged, LANES - SSM_GROUP * r, axis=1)
                    tok = pl.ds(CHUNK * c0 + gps * hh + r, RELAYOUT_ROWS, stride=CHUNK)
                    o_ref[q, tok, :] = merged + d_row * xf[tok, :]
                return carry

            lax.fori_loop(0, nrb, bwd_relayout, 0)


def _ssm(proj, toep, w_in, w_out, decay, d_skip, batch, seq):
    nchunk = seq // CHUNK
    n = CHUNK * SSM_GROUP
    wspec = pl.BlockSpec((SSM_GROUPS, n, n), lambda b: (0, 0, 0), pipeline_mode=pl.Buffered(1))
    return pl.pallas_call(
        _ssm_kernel,
        grid=(batch,),
        in_specs=[
            pl.BlockSpec((seq, W_BRANCH), lambda b: (b, COL_XA // N_SLABS)),
            wspec, wspec, wspec,
            pl.BlockSpec((4, SSM_GROUPS, LANES), lambda b: (0, 0, 0)),
            pl.BlockSpec((1, W_BRANCH), lambda b: (0, 0)),
        ],
        out_specs=pl.BlockSpec((N_SLABS, seq, LANES), lambda b: (0, b, 0)),
        out_shape=jax.ShapeDtypeStruct((N_SLABS, batch * seq, LANES), F32),
        scratch_shapes=[
            pltpu.VMEM((seq, LANES), F32),
            pltpu.VMEM((GROUPS_PER_SLAB, nchunk, n), BF16),
            pltpu.VMEM((2, nchunk * GROUPS_PER_SLAB, LANES), F32),
            pltpu.VMEM((GROUPS_PER_SLAB, nchunk, n), F32),
        ],
        compiler_params=pltpu.CompilerParams(
            dimension_semantics=("arbitrary",), vmem_limit_bytes=VMEM_LIMIT),
        name="ssm_mix",
    )(proj, toep, w_in, w_out, decay, d_skip.astype(F32).reshape(1, W_BRANCH))


OUT_TM = 512


def _out_proj_kernel(ya_ref, za_ref, yb_ref, yc_ref, x_ref, w_ref, gw_ref, gb_ref, fg_ref, o_ref, *, final):
    y = jnp.concatenate([ya_ref[i] for i in range(N_SLABS)], axis=-1)
    g = _gelu_tanh(y)
    gate = jnp.dot(g.astype(BF16), gw_ref[...], preferred_element_type=F32) + gb_ref[...]
    ya = g * (1.0 / (1.0 + jnp.exp(-gate))) * _silu(za_ref[...].astype(F32))
    delta = (jnp.dot(ya.astype(BF16), w_ref[:W_BRANCH, :], preferred_element_type=F32)
             + jnp.dot(yb_ref[...], w_ref[W_BRANCH:2 * W_BRANCH, :], preferred_element_type=F32)
             + jnp.dot(yc_ref[...], w_ref[2 * W_BRANCH:, :], preferred_element_type=F32))
    x = x_ref[...] + delta
    if final:
        ms = jnp.mean(x * x, axis=-1, keepdims=True)
        x = x * lax.rsqrt(ms + RMS_EPS) * fg_ref[...]
    o_ref[...] = x


def _out_proj(ya_pre, proj, yb, yc, x2d, w_bf16, glu_w_bf16, glu_b, final_g, final):
    rows = x2d.shape[0]
    row_blk = lambda width: pl.BlockSpec((OUT_TM, width), lambda i: (i, 0))
    const = lambda shape: pl.BlockSpec(shape, lambda i: (0,) * len(shape), pipeline_mode=pl.Buffered(1))
    return pl.pallas_call(
        functools.partial(_out_proj_kernel, final=final),
        grid=(rows // OUT_TM,),
        in_specs=[
            pl.BlockSpec((N_SLABS, OUT_TM, LANES), lambda i: (0, i, 0)),
            pl.BlockSpec((OUT_TM, W_BRANCH), lambda i: (i, COL_ZA // N_SLABS)),
            row_blk(W_BRANCH), row_blk(W_BRANCH), row_blk(D_MODEL),
            const((MIX_WIDTH, D_MODEL)), const((W_BRANCH, W_BRANCH)),
            const((1, W_BRANCH)), const((1, D_MODEL)),
        ],
        out_specs=row_blk(D_MODEL),
        out_shape=jax.ShapeDtypeStruct((rows, D_MODEL), F32),
        compiler_params=pltpu.CompilerParams(
            dimension_semantics=("arbitrary",), vmem_limit_bytes=VMEM_LIMIT),
        name="out_proj_final" if final else "out_proj",
    )(ya_pre, proj, yb, yc, x2d, w_bf16, glu_w_bf16,
      glu_b.astype(F32).reshape(1, W_BRANCH), final_g.astype(F32).reshape(1, D_MODEL))


def kernel(x, norm_g, w_in, w_out, ssm_lam_re, ssm_lam_im, ssm_log_dt, ssm_b_re, ssm_b_im, ssm_c_re,
           ssm_c_im, ssm_d, glu_w, glu_b, na_rpb, t5_bias, final_g):
    batch, seq, _ = x.shape
    depth = w_in.shape[0]
    x2d = x.astype(F32).reshape(batch * seq, D_MODEL)
    toep, s_in, s_out, decay = _ssm_prep(ssm_lam_re, ssm_lam_im, ssm_log_dt, ssm_b_re, ssm_b_im,
                                         ssm_c_re, ssm_c_im)
    dil_bias = _dil_bias_table(t5_bias)
    for l in range(depth):
        proj = _in_proj(x2d, norm_g[l].astype(F32), w_in[l].astype(BF16))
        ya_pre = _ssm(proj, toep[l], s_in[l], s_out[l], decay[l], ssm_d[l], batch, seq)
        yb = _na(proj, _na_bias_table(na_rpb[l]), batch, seq)
        yc = _dil(proj, dil_bias, batch, seq)
        x2d = _out_proj(ya_pre, proj, yb, yc, x2d, w_out[l].astype(BF16), glu_w[l].astype(BF16),
                        glu_b[l], final_g, final=(l == depth - 1))
    return x2d.reshape(batch, seq, D_MODEL).astype(x.dtype)
```

```python
import functools

import numpy as np
import jax
import jax.numpy as jnp
from jax import lax
from jax.experimental import pallas as pl
from jax.experimental.pallas import tpu as pltpu

F32 = jnp.float32
BF16 = jnp.bfloat16

D_MODEL = 1024
HEAD_DIM = 64
W_BRANCH = 512
N_HEADS = W_BRANCH // HEAD_DIM
N_HEAD_PAIRS = N_HEADS // 2
SSM_GROUP = 16
SSM_GROUPS = W_BRANCH // SSM_GROUP
SSM_STATE = 64
GRID_W = 64
NA_ROWS = 8
NA_COLS = 16
DIL_PATTERNS = ((128, 1), (512, 4), (2048, 16))
DIL_HALF = 64
T5_BUCKETS = 32
T5_MAX_DIST = 1024
RMS_EPS = 1e-6
NEG_INF = -1e30
IN_COLS = 10 * W_BRANCH
MIX_WIDTH = 3 * W_BRANCH

COL_XA, COL_ZA, COL_QB, COL_KB, COL_VB, COL_ZB, COL_QC, COL_KC, COL_VC, COL_ZC = (
    4 * i for i in range(10))

LANES = 128
CHUNK = 16
STATE_PITCH = 40
VMEM_LIMIT = 56 * 1024 * 1024


def _silu(z):
    return z * (1.0 / (1.0 + jnp.exp(-z)))


def _toeplitz_rows(vec, n_rows, n_cols):
    m = vec.shape[-1]
    assert m >= n_rows + n_cols - 1
    tiled = jnp.tile(vec, (1,) * (vec.ndim - 1) + (n_rows,))[..., :n_rows * (m - 1)]
    return tiled.reshape(vec.shape[:-1] + (n_rows, m - 1))[..., :n_cols]


def _gelu_tanh(x):
    return 0.5 * x * (1.0 + jnp.tanh(np.sqrt(2.0 / np.pi).astype(np.float32) * (x + 0.044715 * (x * x * x))))


IN_TM = 512
IN_TN = 512


def _in_proj_kernel(x_ref, g_ref, w_ref, o_ref):
    x = x_ref[...]
    ms = jnp.mean(x * x, axis=-1, keepdims=True)
    h = (x * lax.rsqrt(ms + RMS_EPS) * g_ref[...]).astype(BF16)
    for n in range(IN_COLS // IN_TN):
        o_ref[:, n * IN_TN:(n + 1) * IN_TN] = jnp.dot(
            h, w_ref[:, n * IN_TN:(n + 1) * IN_TN], preferred_element_type=F32).astype(BF16)


def _in_proj(x2d, g, w_bf16):
    rows = x2d.shape[0]
    return pl.pallas_call(
        _in_proj_kernel,
        grid=(rows // IN_TM,),
        in_specs=[
            pl.BlockSpec((IN_TM, D_MODEL), lambda i: (i, 0)),
            pl.BlockSpec((1, D_MODEL), lambda i: (0, 0)),
            pl.BlockSpec((D_MODEL, IN_COLS), lambda i: (0, 0), pipeline_mode=pl.Buffered(1)),
        ],
        out_specs=pl.BlockSpec((IN_TM, IN_COLS), lambda i: (i, 0)),
        out_shape=jax.ShapeDtypeStruct((rows, IN_COLS), BF16),
        compiler_params=pltpu.CompilerParams(
            dimension_semantics=("arbitrary",), vmem_limit_bytes=VMEM_LIMIT),
        name="in_proj",
    )(x2d, g.reshape(1, D_MODEL), w_bf16)


def _na_bias_table(rpb):
    rpb = rpb.astype(F32)
    rows = jnp.stack([rpb[:, NA_ROWS - 1 - v:2 * NA_ROWS - 1 - v] for v in range(NA_ROWS)], axis=1)
    vec = jnp.concatenate([rows[..., NA_COLS - 1:],
                           jnp.zeros(rows.shape[:-1] + (2 * GRID_W - 2 * NA_COLS + 1,), F32),
                           rows[..., :NA_COLS - 1]], axis=-1)
    tab = _toeplitz_rows(vec, GRID_W, GRID_W)
    j = np.arange(GRID_W)[:, None]
    c = np.arange(GRID_W)[None, :]
    col_start = np.clip(j - NA_COLS // 2, 0, GRID_W - NA_COLS)
    valid = (c >= col_start) & (c < col_start + NA_COLS)
    tab = jnp.where(valid[None, None, None], tab, NEG_INF)
    tab = tab.transpose(0, 1, 3, 2, 4)
    return tab.reshape(N_HEADS, NA_ROWS, GRID_W, NA_ROWS * GRID_W)


def _na_kernel(q_ref, k_ref, v_ref, z_ref, bias_ref, o_ref):
    seq = q_ref.shape[0]
    rows = seq // GRID_W
    nkeys = NA_ROWS * GRID_W
    lane = lax.broadcasted_iota(jnp.int32, (GRID_W, LANES), 1)
    first_head = lane < HEAD_DIM

    def row_fn(r, carry):
        rs = jnp.clip(r - NA_ROWS // 2, 0, rows - NA_ROWS)
        var = r - rs
        q0 = pl.multiple_of(r * GRID_W, GRID_W)
        k0 = pl.multiple_of(rs * GRID_W, GRID_W)
        qb = q_ref[pl.ds(q0, GRID_W), :]
        kw = k_ref[pl.ds(k0, nkeys), :]
        vw = v_ref[pl.ds(k0, nkeys), :]
        outs = []
        for hd in range(2):
            keep = first_head if hd == 0 else jnp.logical_not(first_head)
            qm = jnp.where(keep, qb, jnp.zeros_like(qb))
            s = lax.dot_general(qm, kw, (((1,), (1,)), ((), ())), preferred_element_type=F32)
            s = s * (HEAD_DIM ** -0.5) + bias_ref[hd, var]
            m = jnp.max(s, axis=-1, keepdims=True)
            p = jnp.exp(s - m)
            den = jnp.sum(p, axis=-1, keepdims=True)
            acc = jnp.dot(p.astype(BF16), vw, preferred_element_type=F32)
            outs.append(acc * (1.0 / den))
        y = jnp.where(first_head, outs[0], outs[1])
        z = z_ref[pl.ds(q0, GRID_W), :].astype(F32)
        o_ref[pl.ds(q0, GRID_W), :] = (y * _silu(z)).astype(BF16)
        return carry

    lax.fori_loop(0, rows, row_fn, 0, unroll=4)


def _na(proj, bias, batch, seq):
    def col(c0):
        return pl.BlockSpec((seq, LANES), lambda b, p, c0=c0: (b, c0 + p))

    return pl.pallas_call(
        _na_kernel,
        grid=(batch, N_HEAD_PAIRS),
        in_specs=[
            col(COL_QB), col(COL_KB), col(COL_VB), col(COL_ZB),
            pl.BlockSpec((2, NA_ROWS, GRID_W, NA_ROWS * GRID_W), lambda b, p: (p, 0, 0, 0)),
        ],
        out_specs=pl.BlockSpec((seq, LANES), lambda b, p: (b, p)),
        out_shape=jax.ShapeDtypeStruct((batch * seq, W_BRANCH), BF16),
        compiler_params=pltpu.CompilerParams(
            dimension_semantics=("arbitrary", "arbitrary"), vmem_limit_bytes=VMEM_LIMIT),
        name="na_attn",
    )(proj, proj, proj, proj, bias)


DIL_QB = 128
DIL_KW = 256
N_VARIANTS = 3


def _t5_bucket(rel):
    nb = T5_BUCKETS // 2
    max_exact = nb // 2
    n = np.abs(rel)
    large = max_exact + (np.log(np.maximum(n, 1) / max_exact) / np.log(T5_MAX_DIST / max_exact)
                         * (nb - max_exact)).astype(np.int32)
    large = np.minimum(large, nb - 1)
    return (np.where(rel > 0, nb, 0) + np.where(n < max_exact, n, large)).astype(np.int32)


def _dil_bias_table(t5_bias):
    steps = np.arange(-DIL_HALF, DIL_HALF + 1)
    bucket = np.stack([_t5_bucket(d * steps) for _, d in DIL_PATTERNS])
    band = t5_bias.astype(F32)[bucket].transpose(2, 0, 1)
    pad = DIL_QB + DIL_KW - 1 - DIL_HALF
    full = jnp.pad(band, ((0, 0), (0, 0), (pad, pad)), constant_values=NEG_INF)
    zero_at = pad + DIL_HALF
    m = DIL_QB + DIL_KW
    tabs = []
    for var in range(N_VARIANTS):
        lo = zero_at - DIL_HALF * var
        vec = jnp.concatenate([full[..., lo:lo + DIL_KW],
                               jnp.full(band.shape[:-1] + (1,), NEG_INF, F32),
                               full[..., lo - (DIL_QB - 1):lo]], axis=-1)
        tabs.append(_toeplitz_rows(vec, DIL_QB, DIL_KW))
    return jnp.stack(tabs, axis=2)


def _dil_kernel(q_ref, k_ref, v_ref, z_ref, bias_ref, o_ref, qf, kf, vf, m_s, l_s, acc_s):
    seq = q_ref.shape[0]
    conv_rows = 256
    lane = lax.broadcasted_iota(jnp.int32, (DIL_QB, LANES), 1)
    first_head = lane < HEAD_DIM

    def conv(i, carry):
        r0 = pl.multiple_of(i * conv_rows, conv_rows)
        qf[pl.ds(r0, conv_rows), :] = q_ref[pl.ds(r0, conv_rows), :].astype(F32) * (HEAD_DIM ** -0.5)
        kf[pl.ds(r0, conv_rows), :] = k_ref[pl.ds(r0, conv_rows), :].astype(F32)
        vf[pl.ds(r0, conv_rows), :] = v_ref[pl.ds(r0, conv_rows), :].astype(F32)
        return carry

    lax.fori_loop(0, seq // conv_rows, conv, 0)

    for pi, (_, d) in enumerate(DIL_PATTERNS):
        n_sub = seq // d
        nblk = n_sub // DIL_QB

        def rows(start, size, d=d):
            return pl.ds(start, size) if d == 1 else pl.ds(start, size, stride=d)

        def unit(u, carry, pi=pi, d=d, n_sub=n_sub, nblk=nblk, rows=rows):
            res = u // nblk
            i0 = (u % nblk) * DIL_QB
            ks = jnp.clip(i0 - DIL_HALF, 0, n_sub - DIL_KW)
            var = (i0 - ks) // DIL_HALF
            q_rows = rows(res + d * i0, DIL_QB)
            k_rows = rows(res + d * ks, DIL_KW)
            qb = qf[q_rows, :]
            kw = kf[k_rows, :].astype(BF16)
            vw = vf[k_rows, :].astype(BF16)
            ms, ls, accs = [], [], []
            for hd in range(2):
                keep = first_head if hd == 0 else jnp.logical_not(first_head)
                qm = jnp.where(keep, qb, 0.0).astype(BF16)
                s = lax.dot_general(qm, kw, (((1,), (1,)), ((), ())), preferred_element_type=F32)
                s = s + bias_ref[hd, pi, var]
                m = jnp.max(s, axis=-1, keepdims=True)
                p = jnp.exp(s - m)
                ls.append(jnp.sum(p, axis=-1, keepdims=True))
                ms.append(m)
                accs.append(jnp.dot(p.astype(BF16), vw, preferred_element_type=F32))
            m_cur = jnp.where(first_head, ms[0], ms[1])
            l_cur = jnp.where(first_head, ls[0], ls[1])
            acc_cur = jnp.where(first_head, accs[0], accs[1])
            if pi == 0:
                m_s[q_rows, :] = m_cur
                l_s[q_rows, :] = l_cur
                acc_s[q_rows, :] = acc_cur
            else:
                m_old = m_s[q_rows, :]
                m_new = jnp.maximum(m_old, m_cur)
                a_old = jnp.exp(m_old - m_new)
                a_cur = jnp.exp(m_cur - m_new)
                m_s[q_rows, :] = m_new
                l_s[q_rows, :] = a_old * l_s[q_rows, :] + a_cur * l_cur
                acc_s[q_rows, :] = a_old * acc_s[q_rows, :] + a_cur * acc_cur
            return carry

        lax.fori_loop(0, seq // DIL_QB, unit, 0, unroll=4)

    def fin(i, carry):
        r0 = pl.multiple_of(i * conv_rows, conv_rows)
        z = z_ref[pl.ds(r0, conv_rows), :].astype(F32)
        y = acc_s[pl.ds(r0, conv_rows), :] * (1.0 / l_s[pl.ds(r0, conv_rows), :])
        o_ref[pl.ds(r0, conv_rows), :] = (y * _silu(z)).astype(BF16)
        return carry

    lax.fori_loop(0, seq // conv_rows, fin, 0)


def _dil(proj, bias, batch, seq):
    def col(c0):
        return pl.BlockSpec((seq, LANES), lambda b, p, c0=c0: (b, c0 + p))

    scratch = [pltpu.VMEM((seq, LANES), F32) for _ in range(6)]
    return pl.pallas_call(
        _dil_kernel,
        grid=(batch, N_HEAD_PAIRS),
        in_specs=[
            col(COL_QC), col(COL_KC), col(COL_VC), col(COL_ZC),
            pl.BlockSpec((2, len(DIL_PATTERNS), N_VARIANTS, DIL_QB, DIL_KW),
                         lambda b, p: (p, 0, 0, 0, 0)),
        ],
        out_specs=pl.BlockSpec((seq, LANES), lambda b, p: (b, p)),
        out_shape=jax.ShapeDtypeStruct((batch * seq, W_BRANCH), BF16),
        scratch_shapes=scratch,
        compiler_params=pltpu.CompilerParams(
            dimension_semantics=("arbitrary", "arbitrary"), vmem_limit_bytes=VMEM_LIMIT),
        name="dil_attn",
    )(proj, proj, proj, proj, bias)


N_POW = CHUNK + 1
N_LAGS = 2 * CHUNK - 1
PREP_GROUPS = 8


def _ssm_prep_kernel(lre_ref, lim_ref, ldt_ref, btre_ref, btim_ref, cre_ref, cim_ref,
                     lbb_re_ref, lbb_im_ref, cl_re_ref, cl_nim_ref, pw_re_ref, pw_im_ref, klag_ref):
    zpad = jnp.zeros((LANES - SSM_GROUP, SSM_STATE), F32)
    nt = (((1,), (1,)), ((), ()))

    def group(g, carry):
        lag0 = None
        for dr in range(2):
            lam_re = lre_ref[0, dr, g]
            lam_im = lim_ref[0, dr, g]
            dt = jnp.exp(ldt_ref[0, dr, g])
            mag = jnp.exp(lam_re * dt)
            lb_re = mag * jnp.cos(lam_im * dt)
            lb_im = mag * jnp.sin(lam_im * dt)
            n_re = lb_re - 1.0
            den = lam_re * lam_re + lam_im * lam_im
            q_re = (n_re * lam_re + lb_im * lam_im) / den
            q_im = (lb_im * lam_re - n_re * lam_im) / den
            bt_re = btre_ref[0, dr, g]
            bt_im = btim_ref[0, dr, g]
            bb_re = q_re * bt_re - q_im * bt_im
            bb_im = q_re * bt_im + q_im * bt_re
            bb_re_pad = jnp.concatenate([bb_re, zpad], axis=0)
            bb_im_pad = jnp.concatenate([bb_im, zpad], axis=0)
            c_re = cre_ref[0, dr, g]
            c_im = cim_ref[0, dr, g]
            pw_re = jnp.ones_like(lb_re)
            pw_im = jnp.zeros_like(lb_re)
            for j in range(N_POW):
                lbb_re_ref[0, dr, g, j] = pw_re * bb_re - pw_im * bb_im
                lbb_im_ref[0, dr, g, j] = pw_re * bb_im + pw_im * bb_re
                y_re = pw_re * c_re - pw_im * c_im
                y_im = pw_re * c_im + pw_im * c_re
                cl_re_ref[0, dr, g, j] = y_re
                cl_nim_ref[0, dr, g, j] = -y_im
                if j < CHUNK:
                    lag = (lax.dot_general(y_re, bb_re_pad, nt, precision=lax.Precision.HIGHEST,
                                           preferred_element_type=F32)
                           - lax.dot_general(y_im, bb_im_pad, nt, precision=lax.Precision.HIGHEST,
                                             preferred_element_type=F32))
                    if j == 0:
                        lag0 = lag if lag0 is None else lag0 + lag
                    else:
                        klag_ref[0, g, (CHUNK - 1) + (j if dr == 0 else -j)] = lag
                if j == CHUNK:
                    pw_re_ref[0, dr, g] = pw_re
                    pw_im_ref[0, dr, g] = pw_im
                pw_re, pw_im = pw_re * lb_re - pw_im * lb_im, pw_re * lb_im + pw_im * lb_re
        klag_ref[0, g, CHUNK - 1] = lag0
        return carry

    lax.fori_loop(0, PREP_GROUPS, group, 0)


def _ssm_chunk_order(w, axis):
    shape = w.shape
    per = LANES // SSM_GROUP
    w = w.reshape(shape[:1] + (SSM_GROUPS // per, per) + shape[2:axis]
                  + (CHUNK // per, per, SSM_GROUP) + shape[axis + 1:])
    seg_axis = axis + 2
    w = jnp.stack([jnp.roll(w[:, :, j], j, axis=seg_axis - 1) for j in range(per)], axis=2)
    return w.reshape(shape)


def _ssm_prep(lam_re, lam_im, log_dt, b_re, b_im, c_re, c_im):
    depth = lam_re.shape[0]
    G, P, C = SSM_GROUPS, SSM_STATE, SSM_GROUP
    rep = lambda a: jnp.broadcast_to(a.astype(F32)[:, :, :, None, :], (depth, 2, G, C, P))
    ldt = jnp.broadcast_to(log_dt.astype(F32)[:, :, :, None, None], (depth, 2, G, C, P))
    bt = lambda a: a.astype(F32).transpose(0, 1, 2, 4, 3)
    gb = PREP_GROUPS
    spec5 = pl.BlockSpec((1, 2, gb, C, P), lambda l, i: (l, 0, i, 0, 0))
    spec6 = pl.BlockSpec((1, 2, gb, N_POW, C, P), lambda l, i: (l, 0, i, 0, 0, 0))
    shp6 = jax.ShapeDtypeStruct((depth, 2, G, N_POW, C, P), F32)
    shp5 = jax.ShapeDtypeStruct((depth, 2, G, C, P), F32)
    lbb_re, lbb_im, cl_re, cl_nim, pw_re, pw_im, klag = pl.pallas_call(
        _ssm_prep_kernel,
        grid=(depth, G // gb),
        in_specs=[spec5] * 7,
        out_specs=[spec6, spec6, spec6, spec6, spec5, spec5,
                   pl.BlockSpec((1, gb, N_LAGS, C, LANES), lambda l, i: (l, i, 0, 0, 0))],
        out_shape=[shp6, shp6, shp6, shp6, shp5, shp5,
                   jax.ShapeDtypeStruct((depth, G, N_LAGS, C, LANES), F32)],
        compiler_params=pltpu.CompilerParams(
            dimension_semantics=("arbitrary", "arbitrary"), vmem_limit_bytes=VMEM_LIMIT),
        name="ssm_prep",
    )(rep(lam_re), rep(lam_im), ldt, bt(b_re), bt(b_im), c_re.astype(F32), c_im.astype(F32))

    n = CHUNK * C
    flip = lambda a: a[:, :, ::-1]
    w_in = jnp.concatenate([flip(lbb_re[:, 0, :, :CHUNK]), flip(lbb_im[:, 0, :, :CHUNK]),
                            lbb_re[:, 1, :, :CHUNK], lbb_im[:, 1, :, :CHUNK]], axis=-1)
    w_in = w_in.reshape(depth, G, n, 4 * P)
    w_out = jnp.concatenate([cl_re[:, 0, :, 1:], cl_nim[:, 0, :, 1:],
                             flip(cl_re[:, 1, :, 1:]), flip(cl_nim[:, 1, :, 1:])], axis=-1)
    w_out = w_out.transpose(0, 1, 4, 2, 3).reshape(depth, G, 4 * P, n)
    toep = jnp.stack([klag[:, :, CHUNK - 1 - s:2 * CHUNK - 1 - s, :, :C] for s in range(CHUNK)],
                     axis=2)
    toep = toep.transpose(0, 1, 2, 5, 3, 4).reshape(depth, G, n, n)
    toep = _ssm_chunk_order(_ssm_chunk_order(toep, 2), 3)
    w_in = _ssm_chunk_order(w_in, 2)
    w_out = _ssm_chunk_order(w_out, 3)
    dup = lambda a: jnp.concatenate([a[:, :, :, 0, :], a[:, :, :, 0, :]], axis=-1)
    a_re, a_im = dup(pw_re), dup(pw_im)
    decay = jnp.stack([a_re[:, 0], a_im[:, 0], a_re[:, 1], a_im[:, 1]], axis=1)
    return toep.astype(BF16), w_in.astype(BF16), w_out.astype(BF16), decay


GROUPS_PER_SLAB = LANES // SSM_GROUP
N_SLABS = W_BRANCH // LANES
RELAYOUT_ROWS = 32


def _ssm_kernel(xa_ref, toep_ref, win_ref, wout_ref, decay_ref, d_ref, o_ref, xf, u_s, st, yq):
    seq = xa_ref.shape[0]
    nchunk = seq // CHUNK
    nrb = nchunk // RELAYOUT_ROWS
    gps = GROUPS_PER_SLAB
    lane_rb = lax.broadcasted_iota(jnp.int32, (RELAYOUT_ROWS, LANES), 1)
    seg_masks = [(lane_rb >= SSM_GROUP * sg) & (lane_rb < SSM_GROUP * (sg + 1)) for sg in range(gps)]
    lane8 = lax.broadcasted_iota(jnp.int32, (gps, LANES), 1)
    low_half = lane8 < SSM_STATE
    conv_rows = 512

    for q in range(N_SLABS):
        c_lo = q * LANES

        def conv(i, carry, c_lo=c_lo):
            r0 = pl.multiple_of(i * conv_rows, conv_rows)
            xf[pl.ds(r0, conv_rows), :] = xa_ref[pl.ds(r0, conv_rows), c_lo:c_lo + LANES].astype(F32)
            return carry

        lax.fori_loop(0, seq // conv_rows, conv, 0)

        for hh in range(2):
            def fwd_relayout(rb, carry, hh=hh):
                c0 = pl.multiple_of(rb * RELAYOUT_ROWS, RELAYOUT_ROWS)
                rolled = []
                for r in range(gps):
                    xs = xf[pl.ds(CHUNK * c0 + gps * hh + r, RELAYOUT_ROWS, stride=CHUNK), :]
                    rolled.append(xs if r == 0 else pltpu.roll(xs, SSM_GROUP * r, axis=1))
                for j in range(gps):
                    out = rolled[(0 - j) % gps]
                    for sg in range(1, gps):
                        out = jnp.where(seg_masks[sg], rolled[(sg - j) % gps], out)
                    u_s[j, pl.ds(c0, RELAYOUT_ROWS), hh * LANES:(hh + 1) * LANES] = out.astype(BF16)
                return carry

            lax.fori_loop(0, nrb, fwd_relayout, 0)

        for j in range(gps):
            s_in = jnp.dot(u_s[j], win_ref[gps * q + j], preferred_element_type=F32)
            st[0, pl.ds(j, nchunk, stride=gps), :] = s_in[:, :LANES]
            st[1, pl.ds(j, nchunk, stride=gps), :] = s_in[:, LANES:]

        g_lo = gps * q
        af_re = decay_ref[0, g_lo:g_lo + gps, :]
        af_im = decay_ref[1, g_lo:g_lo + gps, :]
        ab_re = decay_ref[2, g_lo:g_lo + gps, :]
        ab_im = decay_ref[3, g_lo:g_lo + gps, :]

        def scan(i, carry):
            hrf, hif, hrb, hib = carry
            rf = pl.multiple_of(i * gps, gps)
            rb_ = pl.multiple_of((nchunk - 1 - i) * gps, gps)
            sf = st[0, pl.ds(rf, gps), :]
            sb = st[1, pl.ds(rb_, gps), :]
            st[0, pl.ds(rf, gps), :] = jnp.where(low_half, hrf, pltpu.roll(hif, SSM_STATE, axis=1))
            st[1, pl.ds(rb_, gps), :] = jnp.where(low_half, hrb, pltpu.roll(hib, SSM_STATE, axis=1))
            sf_im = pltpu.roll(sf, SSM_STATE, axis=1)
            sb_im = pltpu.roll(sb, SSM_STATE, axis=1)
            return (af_re * hrf - af_im * hif + sf, af_re * hif + af_im * hrf + sf_im,
                    ab_re * hrb - ab_im * hib + sb, ab_re * hib + ab_im * hrb + sb_im)

        zero = jnp.zeros((gps, LANES), F32)
        lax.fori_loop(0, nchunk, scan, (zero, zero, zero, zero))

        for j in range(gps):
            g = gps * q + j
            h_f = st[0, pl.ds(j, nchunk, stride=gps), :].astype(BF16)
            h_b = st[1, pl.ds(j, nchunk, stride=gps), :].astype(BF16)
            yq[j] = (jnp.dot(u_s[j], toep_ref[g], preferred_element_type=F32)
                     + jnp.dot(h_f, wout_ref[g, :LANES, :], preferred_element_type=F32)
                     + jnp.dot(h_b, wout_ref[g, LANES:, :], preferred_element_type=F32))

        d_row = d_ref[:, c_lo:c_lo + LANES]
        for hh in range(2):
            def bwd_relayout(rb, carry, hh=hh, d_row=d_row, q=q):
                c0 = pl.multiple_of(rb * RELAYOUT_ROWS, RELAYOUT_ROWS)
                ys = [yq[j, pl.ds(c0, RELAYOUT_ROWS), hh * LANES:(hh + 1) * LANES] for j in range(gps)]
                for r in range(gps):
                    merged = ys[(0 - r) % gps]
                    for sg in range(1, gps):
                        merged = jnp.where(seg_masks[sg], ys[(sg - r) % gps], merged)
                    if r:
                        merged = pltpu.roll(merged, LANES - SSM_GROUP * r, axis=1)
                    tok = pl.ds(CHUNK * c0 + gps * hh + r, RELAYOUT_ROWS, stride=CHUNK)
                    o_ref[q, tok, :] = merged + d_row * xf[tok, :]
                return carry

            lax.fori_loop(0, nrb, bwd_relayout, 0)


def _ssm(proj, toep, w_in, w_out, decay, d_skip, batch, seq):
    nchunk = seq // CHUNK
    n = CHUNK * SSM_GROUP
    wspec = pl.BlockSpec((SSM_GROUPS, n, n), lambda b: (0, 0, 0), pipeline_mode=pl.Buffered(1))
    return pl.pallas_call(
        _ssm_kernel,
        grid=(batch,),
        in_specs=[
            pl.BlockSpec((seq, W_BRANCH), lambda b: (b, COL_XA // N_SLABS)),
            wspec, wspec, wspec,
            pl.BlockSpec((4, SSM_GROUPS, LANES), lambda b: (0, 0, 0)),
            pl.BlockSpec((1, W_BRANCH), lambda b: (0, 0)),
        ],
        out_specs=pl.BlockSpec((N_SLABS, seq, LANES), lambda b: (0, b, 0)),
        out_shape=jax.ShapeDtypeStruct((N_SLABS, batch * seq, LANES), F32),
        scratch_shapes=[
            pltpu.VMEM((seq, LANES), F32),
            pltpu.VMEM((GROUPS_PER_SLAB, nchunk, n), BF16),
            pltpu.VMEM((2, nchunk * GROUPS_PER_SLAB, LANES), F32),
            pltpu.VMEM((GROUPS_PER_SLAB, nchunk, n), F32),
        ],
        compiler_params=pltpu.CompilerParams(
            dimension_semantics=("arbitrary",), vmem_limit_bytes=VMEM_LIMIT),
        name="ssm_mix",
    )(proj, toep, w_in, w_out, decay, d_skip.astype(F32).reshape(1, W_BRANCH))


OUT_TM = 512


def _out_proj_kernel(ya_ref, za_ref, yb_ref, yc_ref, x_ref, w_ref, gw_ref, gb_ref, fg_ref, o_ref, *, final):
    y = jnp.concatenate([ya_ref[i] for i in range(N_SLABS)], axis=-1)
    g = _gelu_tanh(y)
    gate = jnp.dot(g.astype(BF16), gw_ref[...], preferred_element_type=F32) + gb_ref[...]
    ya = g * (1.0 / (1.0 + jnp.exp(-gate))) * _silu(za_ref[...].astype(F32))
    delta = (jnp.dot(ya.astype(BF16), w_ref[:W_BRANCH, :], preferred_element_type=F32)
             + jnp.dot(yb_ref[...], w_ref[W_BRANCH:2 * W_BRANCH, :], preferred_element_type=F32)
             + jnp.dot(yc_ref[...], w_ref[2 * W_BRANCH:, :], preferred_element_type=F32))
    x = x_ref[...] + delta
    if final:
        ms = jnp.mean(x * x, axis=-1, keepdims=True)
        x = x * lax.rsqrt(ms + RMS_EPS) * fg_ref[...]
    o_ref[...] = x


def _out_proj(ya_pre, proj, yb, yc, x2d, w_bf16, glu_w_bf16, glu_b, final_g, final):
    rows = x2d.shape[0]
    row_blk = lambda width: pl.BlockSpec((OUT_TM, width), lambda i: (i, 0))
    const = lambda shape: pl.BlockSpec(shape, lambda i: (0,) * len(shape), pipeline_mode=pl.Buffered(1))
    return pl.pallas_call(
        functools.partial(_out_proj_kernel, final=final),
        grid=(rows // OUT_TM,),
        in_specs=[
            pl.BlockSpec((N_SLABS, OUT_TM, LANES), lambda i: (0, i, 0)),
            pl.BlockSpec((OUT_TM, W_BRANCH), lambda i: (i, COL_ZA // N_SLABS)),
            row_blk(W_BRANCH), row_blk(W_BRANCH), row_blk(D_MODEL),
            const((MIX_WIDTH, D_MODEL)), const((W_BRANCH, W_BRANCH)),
            const((1, W_BRANCH)), const((1, D_MODEL)),
        ],
        out_specs=row_blk(D_MODEL),
        out_shape=jax.ShapeDtypeStruct((rows, D_MODEL), F32),
        compiler_params=pltpu.CompilerParams(
            dimension_semantics=("arbitrary",), vmem_limit_bytes=VMEM_LIMIT),
        name="out_proj_final" if final else "out_proj",
    )(ya_pre, proj, yb, yc, x2d, w_bf16, glu_w_bf16,
      glu_b.astype(F32).reshape(1, W_BRANCH), final_g.astype(F32).reshape(1, D_MODEL))


def kernel(x, norm_g, w_in, w_out, ssm_lam_re, ssm_lam_im, ssm_log_dt, ssm_b_re, ssm_b_im, ssm_c_re,
           ssm_c_im, ssm_d, glu_w, glu_b, na_rpb, t5_bias, final_g):
    batch, seq, _ = x.shape
    depth = w_in.shape[0]
    x2d = x.astype(F32).reshape(batch * seq, D_MODEL)
    toep, s_in, s_out, decay = _ssm_prep(ssm_lam_re, ssm_lam_im, ssm_log_dt, ssm_b_re, ssm_b_im,
                                         ssm_c_re, ssm_c_im)
    dil_bias = _dil_bias_table(t5_bias)
    for l in range(depth):
        proj = _in_proj(x2d, norm_g[l].astype(F32), w_in[l].astype(BF16))
        ya_pre = _ssm(proj, toep[l], s_in[l], s_out[l], decay[l], ssm_d[l], batch, seq)
        yb = _na(proj, _na_bias_table(na_rpb[l]), batch, seq)
        yc = _dil(proj, dil_bias, batch, seq)
        x2d = _out_proj(ya_pre, proj, yb, yc, x2d, w_out[l].astype(BF16), glu_w[l].astype(BF16),
                        glu_b[l], final_g, final=(l == depth - 1))
    return x2d.reshape(batch, seq, D_MODEL).astype(x.dtype)
```

```python
import functools

import numpy as np
import jax
import jax.numpy as jnp
from jax import lax
from jax.experimental import pallas as pl
from jax.experimental.pallas import tpu as pltpu

F32 = jnp.float32
BF16 = jnp.bfloat16

D_MODEL = 1024
HEAD_DIM = 64
W_BRANCH = 512
N_HEADS = W_BRANCH // HEAD_DIM
N_HEAD_PAIRS = N_HEADS // 2
SSM_GROUP = 16
SSM_GROUPS = W_BRANCH // SSM_GROUP
SSM_STATE = 64
GRID_W = 64
NA_ROWS = 8
NA_COLS = 16
DIL_PATTERNS = ((128, 1), (512, 4), (2048, 16))
DIL_HALF = 64
T5_BUCKETS = 32
T5_MAX_DIST = 1024
RMS_EPS = 1e-6
NEG_INF = -1e30
IN_COLS = 10 * W_BRANCH
MIX_WIDTH = 3 * W_BRANCH

COL_XA, COL_ZA, COL_QB, COL_KB, COL_VB, COL_ZB, COL_QC, COL_KC, COL_VC, COL_ZC = (
    4 * i for i in range(10))

LANES = 128
CHUNK = 16
STATE_PITCH = 40
VMEM_LIMIT = 56 * 1024 * 1024


def _silu(z):
    return z * (1.0 / (1.0 + jnp.exp(-z)))


def _toeplitz_rows(vec, n_rows, n_cols):
    m = vec.shape[-1]
    assert m >= n_rows + n_cols - 1
    tiled = jnp.tile(vec, (1,) * (vec.ndim - 1) + (n_rows,))[..., :n_rows * (m - 1)]
    return tiled.reshape(vec.shape[:-1] + (n_rows, m - 1))[..., :n_cols]


def _gelu_tanh(x):
    return 0.5 * x * (1.0 + jnp.tanh(np.sqrt(2.0 / np.pi).astype(np.float32) * (x + 0.044715 * (x * x * x))))


IN_TM = 512
IN_TN = 512


def _in_proj_kernel(x_ref, g_ref, w_ref, o_ref):
    x = x_ref[...]
    ms = jnp.mean(x * x, axis=-1, keepdims=True)
    h = (x * lax.rsqrt(ms + RMS_EPS) * g_ref[...]).astype(BF16)
    for n in range(IN_COLS // IN_TN):
        o_ref[:, n * IN_TN:(n + 1) * IN_TN] = jnp.dot(
            h, w_ref[:, n * IN_TN:(n + 1) * IN_TN], preferred_element_type=F32).astype(BF16)


def _in_proj(x2d, g, w_bf16):
    rows = x2d.shape[0]
    return pl.pallas_call(
        _in_proj_kernel,
        grid=(rows // IN_TM,),
        in_specs=[
            pl.BlockSpec((IN_TM, D_MODEL), lambda i: (i, 0)),
            pl.BlockSpec((1, D_MODEL), lambda i: (0, 0)),
            pl.BlockSpec((D_MODEL, IN_COLS), lambda i: (0, 0), pipeline_mode=pl.Buffered(1)),
        ],
        out_specs=pl.BlockSpec((IN_TM, IN_COLS), lambda i: (i, 0)),
        out_shape=jax.ShapeDtypeStruct((rows, IN_COLS), BF16),
        compiler_params=pltpu.CompilerParams(
            dimension_semantics=("arbitrary",), vmem_limit_bytes=VMEM_LIMIT),
        name="in_proj",
    )(x2d, g.reshape(1, D_MODEL), w_bf16)


def _na_bias_table(rpb):
    rpb = rpb.astype(F32)
    rows = jnp.stack([rpb[:, NA_ROWS - 1 - v:2 * NA_ROWS - 1 - v] for v in range(NA_ROWS)], axis=1)
    vec = jnp.concatenate([rows[..., NA_COLS - 1:],
                           jnp.zeros(rows.shape[:-1] + (2 * GRID_W - 2 * NA_COLS + 1,), F32),
                           rows[..., :NA_COLS - 1]], axis=-1)
    tab = _toeplitz_rows(vec, GRID_W, GRID_W)
    j = np.arange(GRID_W)[:, None]
    c = np.arange(GRID_W)[None, :]
    col_start = np.clip(j - NA_COLS // 2, 0, GRID_W - NA_COLS)
    valid = (c >= col_start) & (c < col_start + NA_COLS)
    tab = jnp.where(valid[None, None, None], tab, NEG_INF)
    tab = tab.transpose(0, 1, 3, 2, 4)
    return tab.reshape(N_HEADS, NA_ROWS, GRID_W, NA_ROWS * GRID_W)


def _na_kernel(q_ref, k_ref, v_ref, z_ref, bias_ref, o_ref, s_scr, mx_scr, p_scr):
    seq = q_ref.shape[0]
    rows = seq // GRID_W
    nkeys = NA_ROWS * GRID_W
    lane = lax.broadcasted_iota(jnp.int32, (GRID_W, LANES), 1)
    first_head = lane < HEAD_DIM
    nt = (((1,), (1,)), ((), ()))

    def window(r):
        rs = jnp.clip(r - NA_ROWS // 2, 0, rows - NA_ROWS)
        return r - rs, pl.multiple_of(r * GRID_W, GRID_W), pl.multiple_of(rs * GRID_W, GRID_W)

    def group(gi, carry):
        r_lo = gi * NA_GROUP

        def logits(i, c):
            var, q0, k0 = window(r_lo + i)
            qb = q_ref[pl.ds(q0, GRID_W), :] * jnp.asarray(HEAD_DIM ** -0.5, BF16)
            kw = k_ref[pl.ds(k0, nkeys), :]
            for hd in range(2):
                keep = first_head if hd == 0 else jnp.logical_not(first_head)
                qm = jnp.where(keep, qb, jnp.zeros_like(qb))
                s = lax.dot_general(qm, kw, nt, preferred_element_type=F32) + bias_ref[hd, var]
                s_scr[2 * i + hd] = s
                mx_scr[2 * i + hd] = jnp.broadcast_to(jnp.max(s, axis=-1, keepdims=True), (GRID_W, LANES))
            return c

        lax.fori_loop(0, NA_GROUP, logits, 0, unroll=True)

        def probs(t, c):
            m = mx_scr[t]
            psum = None
            for kc in range(nkeys // LANES):
                sl = slice(kc * LANES, (kc + 1) * LANES)
                p = jnp.exp(s_scr[t, :, sl] - m)
                p_scr[t, :, sl] = p.astype(BF16)
                psum = p if psum is None else psum + p
            mx_scr[t] = jnp.broadcast_to(jnp.sum(psum, axis=-1, keepdims=True), (GRID_W, LANES))
            return c

        lax.fori_loop(0, 2 * NA_GROUP, probs, 0, unroll=True)

        def outputs(i, c):
            _, q0, k0 = window(r_lo + i)
            vw = v_ref[pl.ds(k0, nkeys), :]
            r0 = jnp.dot(p_scr[2 * i], vw, preferred_element_type=F32)
            r1 = jnp.dot(p_scr[2 * i + 1], vw, preferred_element_type=F32)
            acc = jnp.where(first_head, r0, r1)
            den = jnp.where(first_head, mx_scr[2 * i], mx_scr[2 * i + 1])
            z = z_ref[pl.ds(q0, GRID_W), :].astype(F32)
            o_ref[pl.ds(q0, GRID_W), :] = (acc * (1.0 / den) * _silu(z)).astype(BF16)
            return c

        lax.fori_loop(0, NA_GROUP, outputs, 0, unroll=True)
        return carry

    lax.fori_loop(0, rows // NA_GROUP, group, 0)


NA_GROUP = 8


def _na(proj, bias, batch, seq):
    def col(c0):
        return pl.BlockSpec((seq, LANES), lambda b, p, c0=c0: (b, c0 + p))

    nkeys = NA_ROWS * GRID_W
    scratch = [pltpu.VMEM((2 * NA_GROUP, GRID_W, nkeys), F32),
               pltpu.VMEM((2 * NA_GROUP, GRID_W, LANES), F32),
               pltpu.VMEM((2 * NA_GROUP, GRID_W, nkeys), BF16)]

    return pl.pallas_call(
        _na_kernel,
        grid=(batch, N_HEAD_PAIRS),
        in_specs=[
            col(COL_QB), col(COL_KB), col(COL_VB), col(COL_ZB),
            pl.BlockSpec((2, NA_ROWS, GRID_W, NA_ROWS * GRID_W), lambda b, p: (p, 0, 0, 0)),
        ],
        out_specs=pl.BlockSpec((seq, LANES), lambda b, p: (b, p)),
        out_shape=jax.ShapeDtypeStruct((batch * seq, W_BRANCH), BF16),
        scratch_shapes=scratch,
        compiler_params=pltpu.CompilerParams(
            dimension_semantics=("arbitrary", "arbitrary"), vmem_limit_bytes=VMEM_LIMIT),
        name="na_attn",
    )(proj, proj, proj, proj, bias)


DIL_QB = 128
DIL_KW = 256
N_VARIANTS = 3


def _t5_bucket(rel):
    nb = T5_BUCKETS // 2
    max_exact = nb // 2
    n = np.abs(rel)
    large = max_exact + (np.log(np.maximum(n, 1) / max_exact) / np.log(T5_MAX_DIST / max_exact)
                         * (nb - max_exact)).astype(np.int32)
    large = np.minimum(large, nb - 1)
    return (np.where(rel > 0, nb, 0) + np.where(n < max_exact, n, large)).astype(np.int32)


def _dil_bias_table(t5_bias):
    steps = np.arange(-DIL_HALF, DIL_HALF + 1)
    bucket = np.stack([_t5_bucket(d * steps) for _, d in DIL_PATTERNS])
    band = t5_bias.astype(F32)[bucket].transpose(2, 0, 1)
    pad = DIL_QB + DIL_KW - 1 - DIL_HALF
    full = jnp.pad(band, ((0, 0), (0, 0), (pad, pad)), constant_values=NEG_INF)
    zero_at = pad + DIL_HALF
    m = DIL_QB + DIL_KW
    tabs = []
    for var in range(N_VARIANTS):
        lo = zero_at - DIL_HALF * var
        vec = jnp.concatenate([full[..., lo:lo + DIL_KW],
                               jnp.full(band.shape[:-1] + (1,), NEG_INF, F32),
                               full[..., lo - (DIL_QB - 1):lo]], axis=-1)
        tabs.append(_toeplitz_rows(vec, DIL_QB, DIL_KW))
    return jnp.stack(tabs, axis=2)


DIL_GROUP = 8


def _dil_kernel(q_ref, k_ref, v_ref, z_ref, bias_ref, o_ref, qf, kf, vf, qg, kg, vg, qd, kd, vd,
                m_s, l_s, acc_s):
    seq = q_ref.shape[0]
    conv_rows = 256
    lane = lax.broadcasted_iota(jnp.int32, (DIL_QB, LANES), 1)
    first_head = lane < HEAD_DIM
    nt = (((1,), (1,)), ((), ()))

    def conv(i, carry):
        r0 = pl.multiple_of(i * conv_rows, conv_rows)
        qf[pl.ds(r0, conv_rows), :] = q_ref[pl.ds(r0, conv_rows), :].astype(F32) * (HEAD_DIM ** -0.5)
        kf[pl.ds(r0, conv_rows), :] = k_ref[pl.ds(r0, conv_rows), :].astype(F32)
        vf[pl.ds(r0, conv_rows), :] = v_ref[pl.ds(r0, conv_rows), :].astype(F32)
        return carry

    lax.fori_loop(0, seq // conv_rows, conv, 0)

    n4 = seq // 4
    n16 = seq // 16

    def to_order4(i, carry):
        dst = pl.multiple_of(i * conv_rows, conv_rows)
        src = pl.ds(dst // n4 + 4 * (dst % n4), conv_rows, stride=4)
        for a, b in ((qf, qg), (kf, kg), (vf, vg)):
            b[pl.ds(dst, conv_rows), :] = a[src, :]
        return carry

    lax.fori_loop(0, seq // conv_rows, to_order4, 0)

    def order16_rows_in_order4(rho):
        return pl.ds((rho % 4) * n4 + rho // 4, n16, stride=4)

    def to_order16(rho, carry):
        src = order16_rows_in_order4(rho)
        dst = pl.ds(pl.multiple_of(rho * n16, n16), n16)
        for a, b in ((qg, qd), (kg, kd), (vg, vd)):
            b[dst, :] = a[src, :].astype(BF16)
        return carry

    lax.fori_loop(0, 16, to_order16, 0)

    plan = ((2, 16, (qd, kd, vd), (m_s, l_s, acc_s)),
            (1, 4, (qd, kd, vd), (qf, kf, vf)),
            (0, 1, (qd, k_ref, v_ref), (m_s, l_s, acc_s)))

    for step, (pi, d, (q_src, k_src, v_src), (m_st, l_st, acc_st)) in enumerate(plan):
        n_sub = seq // d
        nblk = n_sub // DIL_QB

        if step == 1:
            def state_to_order4(rho, carry):
                dst = order16_rows_in_order4(rho)
                src = pl.ds(pl.multiple_of(rho * n16, n16), n16)
                for a, b in ((m_s, qf), (l_s, kf), (acc_s, vf)):
                    b[dst, :] = a[src, :]
                return carry

            lax.fori_loop(0, 16, state_to_order4, 0)

            def cast_order4(i, carry):
                r0 = pl.multiple_of(i * conv_rows, conv_rows)
                for a, b in ((qg, qd), (kg, kd), (vg, vd)):
                    b[pl.ds(r0, conv_rows), :] = a[pl.ds(r0, conv_rows), :].astype(BF16)
                return carry

            lax.fori_loop(0, seq // conv_rows, cast_order4, 0)
        if step == 2:
            def state_to_tokens(i, carry):
                src = pl.multiple_of(i * conv_rows, conv_rows)
                dst = pl.ds(src // n4 + 4 * (src % n4), conv_rows, stride=4)
                for a, b in ((qf, m_s), (kf, l_s), (vf, acc_s)):
                    b[dst, :] = a[pl.ds(src, conv_rows), :]
                qd[pl.ds(src, conv_rows), :] = q_ref[pl.ds(src, conv_rows), :] * jnp.asarray(
                    HEAD_DIM ** -0.5, BF16)
                return carry

            lax.fori_loop(0, seq // conv_rows, state_to_tokens, 0)

        def group(gi, carry, pi=pi, n_sub=n_sub, nblk=nblk, first=(step == 0), q_src=q_src,
                  k_src=k_src, v_src=v_src, m_st=m_st, l_st=l_st, acc_st=acc_st):
            units = []
            for i in range(DIL_GROUP):
                u = gi * DIL_GROUP + i
                res = u // nblk
                i0 = (u % nblk) * DIL_QB
                ks = jnp.clip(i0 - DIL_HALF, 0, n_sub - DIL_KW)
                units.append((res, i0, ks, (i0 - ks) // DIL_HALF))
            s_tiles, m_tiles = [], []
            for res, i0, ks, var in units:
                qb = q_src[pl.ds(pl.multiple_of(res * n_sub + i0, DIL_HALF), DIL_QB), :]
                kw = k_src[pl.ds(pl.multiple_of(res * n_sub + ks, DIL_HALF), DIL_KW), :]
                for hd in range(2):
                    keep = first_head if hd == 0 else jnp.logical_not(first_head)
                    qm = jnp.where(keep, qb, jnp.zeros_like(qb))
                    s = lax.dot_general(qm, kw, nt, preferred_element_type=F32) + bias_ref[hd, pi, var]
                    s_tiles.append(s)
                    m_tiles.append(jnp.max(s, axis=-1, keepdims=True))
            p_tiles, l_tiles = [], []
            for s, m in zip(s_tiles, m_tiles):
                p = jnp.exp(s - m)
                l_tiles.append(jnp.sum(p, axis=-1, keepdims=True))
                p_tiles.append(p.astype(BF16))
            for i, (res, i0, ks, var) in enumerate(units):
                vw = v_src[pl.ds(pl.multiple_of(res * n_sub + ks, DIL_HALF), DIL_KW), :]
                acc0 = jnp.dot(p_tiles[2 * i], vw, preferred_element_type=F32)
                acc1 = jnp.dot(p_tiles[2 * i + 1], vw, preferred_element_type=F32)
                m_cur = jnp.where(first_head, m_tiles[2 * i], m_tiles[2 * i + 1])
                l_cur = jnp.where(first_head, l_tiles[2 * i], l_tiles[2 * i + 1])
                acc_cur = jnp.where(first_head, acc0, acc1)
                q_rows = pl.ds(pl.multiple_of(res * n_sub + i0, DIL_HALF), DIL_QB)
                if first:
                    m_st[q_rows, :] = m_cur
                    l_st[q_rows, :] = l_cur
                    acc_st[q_rows, :] = acc_cur
                else:
                    m_old = m_st[q_rows, :]
                    m_new = jnp.maximum(m_old, m_cur)
                    a_old = jnp.exp(m_old - m_new)
                    a_cur = jnp.exp(m_cur - m_new)
                    m_st[q_rows, :] = m_new
                    l_st[q_rows, :] = a_old * l_st[q_rows, :] + a_cur * l_cur
                    acc_st[q_rows, :] = a_old * acc_st[q_rows, :] + a_cur * acc_cur
            return carry

        lax.fori_loop(0, seq // DIL_QB // DIL_GROUP, group, 0)

    def fin(i, carry):
        r0 = pl.multiple_of(i * conv_rows, conv_rows)
        z = z_ref[pl.ds(r0, conv_rows), :].astype(F32)
        y = acc_s[pl.ds(r0, conv_rows), :] * (1.0 / l_s[pl.ds(r0, conv_rows), :])
        o_ref[pl.ds(r0, conv_rows), :] = (y * _silu(z)).astype(BF16)
        return carry

    lax.fori_loop(0, seq // conv_rows, fin, 0)


def _dil(proj, bias, batch, seq):
    def col(c0):
        return pl.BlockSpec((seq, LANES), lambda b, p, c0=c0: (b, c0 + p))

    scratch = ([pltpu.VMEM((seq, LANES), F32) for _ in range(6)]
               + [pltpu.VMEM((seq, LANES), BF16) for _ in range(3)]
               + [pltpu.VMEM((seq, LANES), F32) for _ in range(3)])
    return pl.pallas_call(
        _dil_kernel,
        grid=(batch, N_HEAD_PAIRS),
        in_specs=[
            col(COL_QC), col(COL_KC), col(COL_VC), col(COL_ZC),
            pl.BlockSpec((2, len(DIL_PATTERNS), N_VARIANTS, DIL_QB, DIL_KW),
                         lambda b, p: (p, 0, 0, 0, 0)),
        ],
        out_specs=pl.BlockSpec((seq, LANES), lambda b, p: (b, p)),
        out_shape=jax.ShapeDtypeStruct((batch * seq, W_BRANCH), BF16),
        scratch_shapes=scratch,
        compiler_params=pltpu.CompilerParams(
            dimension_semantics=("arbitrary", "arbitrary"), vmem_limit_bytes=VMEM_LIMIT),
        name="dil_attn",
    )(proj, proj, proj, proj, bias)


N_POW = CHUNK + 1
N_LAGS = 2 * CHUNK - 1
PREP_GROUPS = 8


def _ssm_prep_kernel(lre_ref, lim_ref, ldt_ref, btre_ref, btim_ref, cre_ref, cim_ref,
                     lbb_re_ref, lbb_im_ref, cl_re_ref, cl_nim_ref, pw_re_ref, pw_im_ref, klag_ref):
    zpad = jnp.zeros((LANES - SSM_GROUP, SSM_STATE), F32)
    nt = (((1,), (1,)), ((), ()))

    def group(g, carry):
        lag0 = None
        for dr in range(2):
            lam_re = lre_ref[0, dr, g]
            lam_im = lim_ref[0, dr, g]
            dt = jnp.exp(ldt_ref[0, dr, g])
            mag = jnp.exp(lam_re * dt)
            lb_re = mag * jnp.cos(lam_im * dt)
            lb_im = mag * jnp.sin(lam_im * dt)
            n_re = lb_re - 1.0
            den = lam_re * lam_re + lam_im * lam_im
            q_re = (n_re * lam_re + lb_im * lam_im) / den
            q_im = (lb_im * lam_re - n_re * lam_im) / den
            bt_re = btre_ref[0, dr, g]
            bt_im = btim_ref[0, dr, g]
            bb_re = q_re * bt_re - q_im * bt_im
            bb_im = q_re * bt_im + q_im * bt_re
            bb_re_pad = jnp.concatenate([bb_re, zpad], axis=0)
            bb_im_pad = jnp.concatenate([bb_im, zpad], axis=0)
            c_re = cre_ref[0, dr, g]
            c_im = cim_ref[0, dr, g]
            pw_re = jnp.ones_like(lb_re)
            pw_im = jnp.zeros_like(lb_re)
            for j in range(N_POW):
                lbb_re_ref[0, dr, g, j] = pw_re * bb_re - pw_im * bb_im
                lbb_im_ref[0, dr, g, j] = pw_re * bb_im + pw_im * bb_re
                y_re = pw_re * c_re - pw_im * c_im
                y_im = pw_re * c_im + pw_im * c_re
                cl_re_ref[0, dr, g, j] = y_re
                cl_nim_ref[0, dr, g, j] = -y_im
                if j < CHUNK:
                    lag = (lax.dot_general(y_re, bb_re_pad, nt, precision=lax.Precision.HIGHEST,
                                           preferred_element_type=F32)
                           - lax.dot_general(y_im, bb_im_pad, nt, precision=lax.Precision.HIGHEST,
                                             preferred_element_type=F32))
                    if j == 0:
                        lag0 = lag if lag0 is None else lag0 + lag
                    else:
                        klag_ref[0, g, (CHUNK - 1) + (j if dr == 0 else -j)] = lag
                if j == CHUNK:
                    pw_re_ref[0, dr, g] = pw_re
                    pw_im_ref[0, dr, g] = pw_im
                pw_re, pw_im = pw_re * lb_re - pw_im * lb_im, pw_re * lb_im + pw_im * lb_re
        klag_ref[0, g, CHUNK - 1] = lag0
        return carry

    lax.fori_loop(0, PREP_GROUPS, group, 0)


def _ssm_chunk_order(w, axis):
    shape = w.shape
    per = LANES // SSM_GROUP
    w = w.reshape(shape[:1] + (SSM_GROUPS // per, per) + shape[2:axis]
                  + (CHUNK // per, per, SSM_GROUP) + shape[axis + 1:])
    seg_axis = axis + 2
    w = jnp.stack([jnp.roll(w[:, :, j], j, axis=seg_axis - 1) for j in range(per)], axis=2)
    return w.reshape(shape)


def _ssm_prep(lam_re, lam_im, log_dt, b_re, b_im, c_re, c_im):
    depth = lam_re.shape[0]
    G, P, C = SSM_GROUPS, SSM_STATE, SSM_GROUP
    rep = lambda a: jnp.broadcast_to(a.astype(F32)[:, :, :, None, :], (depth, 2, G, C, P))
    ldt = jnp.broadcast_to(log_dt.astype(F32)[:, :, :, None, None], (depth, 2, G, C, P))
    bt = lambda a: a.astype(F32).transpose(0, 1, 2, 4, 3)
    gb = PREP_GROUPS
    spec5 = pl.BlockSpec((1, 2, gb, C, P), lambda l, i: (l, 0, i, 0, 0))
    spec6 = pl.BlockSpec((1, 2, gb, N_POW, C, P), lambda l, i: (l, 0, i, 0, 0, 0))
    shp6 = jax.ShapeDtypeStruct((depth, 2, G, N_POW, C, P), F32)
    shp5 = jax.ShapeDtypeStruct((depth, 2, G, C, P), F32)
    lbb_re, lbb_im, cl_re, cl_nim, pw_re, pw_im, klag = pl.pallas_call(
        _ssm_prep_kernel,
        grid=(depth, G // gb),
        in_specs=[spec5] * 7,
        out_specs=[spec6, spec6, spec6, spec6, spec5, spec5,
                   pl.BlockSpec((1, gb, N_LAGS, C, LANES), lambda l, i: (l, i, 0, 0, 0))],
        out_shape=[shp6, shp6, shp6, shp6, shp5, shp5,
                   jax.ShapeDtypeStruct((depth, G, N_LAGS, C, LANES), F32)],
        compiler_params=pltpu.CompilerParams(
            dimension_semantics=("arbitrary", "arbitrary"), vmem_limit_bytes=VMEM_LIMIT),
        name="ssm_prep",
    )(rep(lam_re), rep(lam_im), ldt, bt(b_re), bt(b_im), c_re.astype(F32), c_im.astype(F32))

    n = CHUNK * C
    flip = lambda a: a[:, :, ::-1]
    w_in = jnp.concatenate([flip(lbb_re[:, 0, :, :CHUNK]), flip(lbb_im[:, 0, :, :CHUNK]),
                            lbb_re[:, 1, :, :CHUNK], lbb_im[:, 1, :, :CHUNK]], axis=-1)
    w_in = w_in.reshape(depth, G, n, 4 * P)
    w_out = jnp.concatenate([cl_re[:, 0, :, 1:], cl_nim[:, 0, :, 1:],
                             flip(cl_re[:, 1, :, 1:]), flip(cl_nim[:, 1, :, 1:])], axis=-1)
    w_out = w_out.transpose(0, 1, 4, 2, 3).reshape(depth, G, 4 * P, n)
    toep = jnp.stack([klag[:, :, CHUNK - 1 - s:2 * CHUNK - 1 - s, :, :C] for s in range(CHUNK)],
                     axis=2)
    toep = toep.transpose(0, 1, 2, 5, 3, 4).reshape(depth, G, n, n)
    toep = _ssm_chunk_order(_ssm_chunk_order(toep, 2), 3)
    w_in = _ssm_chunk_order(w_in, 2)
    w_out = _ssm_chunk_order(w_out, 3)
    dup = lambda a: jnp.concatenate([a[:, :, :, 0, :], a[:, :, :, 0, :]], axis=-1)
    a_re, a_im = dup(pw_re), dup(pw_im)
    decay = jnp.stack([a_re[:, 0], a_im[:, 0], a_re[:, 1], a_im[:, 1]], axis=1)
    return toep.astype(BF16), w_in.astype(BF16), w_out.astype(BF16), decay


GROUPS_PER_SLAB = LANES // SSM_GROUP
N_SLABS = W_BRANCH // LANES
RELAYOUT_ROWS = 32


def _ssm_kernel(xa_ref, toep_ref, win_ref, wout_ref, decay_ref, d_ref, o_ref, xf, u_s, st, yq):
    seq = xa_ref.shape[0]
    nchunk = seq // CHUNK
    nrb = nchunk // RELAYOUT_ROWS
    gps = GROUPS_PER_SLAB
    lane_rb = lax.broadcasted_iota(jnp.int32, (RELAYOUT_ROWS, LANES), 1)
    seg_masks = [(lane_rb >= SSM_GROUP * sg) & (lane_rb < SSM_GROUP * (sg + 1)) for sg in range(gps)]
    lane8 = lax.broadcasted_iota(jnp.int32, (gps, LANES), 1)
    low_half = lane8 < SSM_STATE
    conv_rows = 512

    for q in range(N_SLABS):
        c_lo = q * LANES

        def conv(i, carry, c_lo=c_lo):
            r0 = pl.multiple_of(i * conv_rows, conv_rows)
            xf[pl.ds(r0, conv_rows), :] = xa_ref[pl.ds(r0, conv_rows), c_lo:c_lo + LANES].astype(F32)
            return carry

        lax.fori_loop(0, seq // conv_rows, conv, 0)

        for hh in range(2):
            def fwd_relayout(rb, carry, hh=hh):
                c0 = pl.multiple_of(rb * RELAYOUT_ROWS, RELAYOUT_ROWS)
                rolled = []
                for r in range(gps):
                    xs = xf[pl.ds(CHUNK * c0 + gps * hh + r, RELAYOUT_ROWS, stride=CHUNK), :]
                    rolled.append(xs if r == 0 else pltpu.roll(xs, SSM_GROUP * r, axis=1))
                for j in range(gps):
                    out = rolled[(0 - j) % gps]
                    for sg in range(1, gps):
                        out = jnp.where(seg_masks[sg], rolled[(sg - j) % gps], out)
                    u_s[j, pl.ds(c0, RELAYOUT_ROWS), hh * LANES:(hh + 1) * LANES] = out.astype(BF16)
                return carry

            lax.fori_loop(0, nrb, fwd_relayout, 0)

        for j in range(gps):
            s_in = jnp.dot(u_s[j], win_ref[gps * q + j], preferred_element_type=F32)
            st[0, pl.ds(j, nchunk, stride=gps), :] = s_in[:, :LANES]
            st[1, pl.ds(j, nchunk, stride=gps), :] = s_in[:, LANES:]

        g_lo = gps * q
        af_re = decay_ref[0, g_lo:g_lo + gps, :]
        af_im = decay_ref[1, g_lo:g_lo + gps, :]
        ab_re = decay_ref[2, g_lo:g_lo + gps, :]
        ab_im = decay_ref[3, g_lo:g_lo + gps, :]

        def scan(i, carry):
            hrf, hif, hrb, hib = carry
            rf = pl.multiple_of(i * gps, gps)
            rb_ = pl.multiple_of((nchunk - 1 - i) * gps, gps)
            sf = st[0, pl.ds(rf, gps), :]
            sb = st[1, pl.ds(rb_, gps), :]
            st[0, pl.ds(rf, gps), :] = jnp.where(low_half, hrf, pltpu.roll(hif, SSM_STATE, axis=1))
            st[1, pl.ds(rb_, gps), :] = jnp.where(low_half, hrb, pltpu.roll(hib, SSM_STATE, axis=1))
            sf_im = pltpu.roll(sf, SSM_STATE, axis=1)
            sb_im = pltpu.roll(sb, SSM_STATE, axis=1)
            return (af_re * hrf - af_im * hif + sf, af_re * hif + af_im * hrf + sf_im,
                    ab_re * hrb - ab_im * hib + sb, ab_re * hib + ab_im * hrb + sb_im)

        zero = jnp.zeros((gps, LANES), F32)
        lax.fori_loop(0, nchunk, scan, (zero, zero, zero, zero))

        for j in range(gps):
            g = gps * q + j
            h_f = st[0, pl.ds(j, nchunk, stride=gps), :].astype(BF16)
            h_b = st[1, pl.ds(j, nchunk, stride=gps), :].astype(BF16)
            yq[j] = (jnp.dot(u_s[j], toep_ref[g], preferred_element_type=F32)
                     + jnp.dot(h_f, wout_ref[g, :LANES, :], preferred_element_type=F32)
                     + jnp.dot(h_b, wout_ref[g, LANES:, :], preferred_element_type=F32))

        d_row = d_ref[:, c_lo:c_lo + LANES]
        for hh in range(2):
            def bwd_relayout(rb, carry, hh=hh, d_row=d_row, q=q):
                c0 = pl.multiple_of(rb * RELAYOUT_ROWS, RELAYOUT_ROWS)
                ys = [yq[j, pl.ds(c0, RELAYOUT_ROWS), hh * LANES:(hh + 1) * LANES] for j in range(gps)]
                for r in range(gps):
                    merged = ys[(0 - r) % gps]
                    for sg in range(1, gps):
                        merged = jnp.where(seg_masks[sg], ys[(sg - r) % gps], merged)
                    if r:
                        merged = pltpu.roll(merged, LANES - SSM_GROUP * r, axis=1)
                    tok = pl.ds(CHUNK * c0 + gps * hh + r, RELAYOUT_ROWS, stride=CHUNK)
                    o_ref[q, tok, :] = merged + d_row * xf[tok, :]
                return carry

            lax.fori_loop(0, nrb, bwd_relayout, 0)


def _ssm(proj, toep, w_in, w_out, decay, d_skip, batch, seq):
    nchunk = seq // CHUNK
    n = CHUNK * SSM_GROUP
    wspec = pl.BlockSpec((SSM_GROUPS, n, n), lambda b: (0, 0, 0), pipeline_mode=pl.Buffered(1))
    return pl.pallas_call(
        _ssm_kernel,
        grid=(batch,),
        in_specs=[
            pl.BlockSpec((seq, W_BRANCH), lambda b: (b, COL_XA // N_SLABS)),
            wspec, wspec, wspec,
            pl.BlockSpec((4, SSM_GROUPS, LANES), lambda b: (0, 0, 0)),
            pl.BlockSpec((1, W_BRANCH), lambda b: (0, 0)),
        ],
        out_specs=pl.BlockSpec((N_SLABS, seq, LANES), lambda b: (0, b, 0)),
        out_shape=jax.ShapeDtypeStruct((N_SLABS, batch * seq, LANES), F32),
        scratch_shapes=[
            pltpu.VMEM((seq, LANES), F32),
            pltpu.VMEM((GROUPS_PER_SLAB, nchunk, n), BF16),
            pltpu.VMEM((2, nchunk * GROUPS_PER_SLAB, LANES), F32),
            pltpu.VMEM((GROUPS_PER_SLAB, nchunk, n), F32),
        ],
        compiler_params=pltpu.CompilerParams(
            dimension_semantics=("arbitrary",), vmem_limit_bytes=VMEM_LIMIT),
        name="ssm_mix",
    )(proj, toep, w_in, w_out, decay, d_skip.astype(F32).reshape(1, W_BRANCH))


OUT_TM = 512


def _out_proj_kernel(ya_ref, za_ref, yb_ref, yc_ref, x_ref, w_ref, gw_ref, gb_ref, fg_ref, o_ref, *, final):
    y = jnp.concatenate([ya_ref[i] for i in range(N_SLABS)], axis=-1)
    g = _gelu_tanh(y)
    gate = jnp.dot(g.astype(BF16), gw_ref[...], preferred_element_type=F32) + gb_ref[...]
    ya = g * (1.0 / (1.0 + jnp.exp(-gate))) * _silu(za_ref[...].astype(F32))
    delta = (jnp.dot(ya.astype(BF16), w_ref[:W_BRANCH, :], preferred_element_type=F32)
             + jnp.dot(yb_ref[...], w_ref[W_BRANCH:2 * W_BRANCH, :], preferred_element_type=F32)
             + jnp.dot(yc_ref[...], w_ref[2 * W_BRANCH:, :], preferred_element_type=F32))
    x = x_ref[...] + delta
    if final:
        ms = jnp.mean(x * x, axis=-1, keepdims=True)
        x = x * lax.rsqrt(ms + RMS_EPS) * fg_ref[...]
    o_ref[...] = x


def _out_proj(ya_pre, proj, yb, yc, x2d, w_bf16, glu_w_bf16, glu_b, final_g, final):
    rows = x2d.shape[0]
    row_blk = lambda width: pl.BlockSpec((OUT_TM, width), lambda i: (i, 0))
    const = lambda shape: pl.BlockSpec(shape, lambda i: (0,) * len(shape), pipeline_mode=pl.Buffered(1))
    return pl.pallas_call(
        functools.partial(_out_proj_kernel, final=final),
        grid=(rows // OUT_TM,),
        in_specs=[
            pl.BlockSpec((N_SLABS, OUT_TM, LANES), lambda i: (0, i, 0)),
            pl.BlockSpec((OUT_TM, W_BRANCH), lambda i: (i, COL_ZA // N_SLABS)),
            row_blk(W_BRANCH), row_blk(W_BRANCH), row_blk(D_MODEL),
            const((MIX_WIDTH, D_MODEL)), const((W_BRANCH, W_BRANCH)),
            const((1, W_BRANCH)), const((1, D_MODEL)),
        ],
        out_specs=row_blk(D_MODEL),
        out_shape=jax.ShapeDtypeStruct((rows, D_MODEL), F32),
        compiler_params=pltpu.CompilerParams(
            dimension_semantics=("arbitrary",), vmem_limit_bytes=VMEM_LIMIT),
        name="out_proj_final" if final else "out_proj",
    )(ya_pre, proj, yb, yc, x2d, w_bf16, glu_w_bf16,
      glu_b.astype(F32).reshape(1, W_BRANCH), final_g.astype(F32).reshape(1, D_MODEL))


def kernel(x, norm_g, w_in, w_out, ssm_lam_re, ssm_lam_im, ssm_log_dt, ssm_b_re, ssm_b_im, ssm_c_re,
           ssm_c_im, ssm_d, glu_w, glu_b, na_rpb, t5_bias, final_g):
    batch, seq, _ = x.shape
    depth = w_in.shape[0]
    x2d = x.astype(F32).reshape(batch * seq, D_MODEL)
    toep, s_in, s_out, decay = _ssm_prep(ssm_lam_re, ssm_lam_im, ssm_log_dt, ssm_b_re, ssm_b_im,
                                         ssm_c_re, ssm_c_im)
    dil_bias = _dil_bias_table(t5_bias)
    for l in range(depth):
        proj = _in_proj(x2d, norm_g[l].astype(F32), w_in[l].astype(BF16))
        ya_pre = _ssm(proj, toep[l], s_in[l], s_out[l], decay[l], ssm_d[l], batch, seq)
        yb = _na(proj, _na_bias_table(na_rpb[l]), batch, seq)
        yc = _dil(proj, dil_bias, batch, seq)
        x2d = _out_proj(ya_pre, proj, yb, yc, x2d, w_out[l].astype(BF16), glu_w[l].astype(BF16),
                        glu_b[l], final_g, final=(l == depth - 1))
    return x2d.reshape(batch, seq, D_MODEL).astype(x.dtype)
```

```python
import functools

import numpy as np
import jax
import jax.numpy as jnp
from jax import lax
from jax.experimental import pallas as pl
from jax.experimental.pallas import tpu as pltpu

F32 = jnp.float32
BF16 = jnp.bfloat16

D_MODEL = 1024
HEAD_DIM = 64
W_BRANCH = 512
N_HEADS = W_BRANCH // HEAD_DIM
N_HEAD_PAIRS = N_HEADS // 2
SSM_GROUP = 16
SSM_GROUPS = W_BRANCH // SSM_GROUP
SSM_STATE = 64
GRID_W = 64
NA_ROWS = 8
NA_COLS = 16
DIL_PATTERNS = ((128, 1), (512, 4), (2048, 16))
DIL_HALF = 64
T5_BUCKETS = 32
T5_MAX_DIST = 1024
RMS_EPS = 1e-6
NEG_INF = -1e30
IN_COLS = 10 * W_BRANCH
MIX_WIDTH = 3 * W_BRANCH

COL_XA, COL_ZA, COL_QB, COL_KB, COL_VB, COL_ZB, COL_QC, COL_KC, COL_VC, COL_ZC = (
    4 * i for i in range(10))

LANES = 128
CHUNK = 16
VMEM_LIMIT = 56 * 1024 * 1024


def _silu(z):
    return z * (1.0 / (1.0 + jnp.exp(-z)))


def _toeplitz_rows(vec, n_rows, n_cols):
    m = vec.shape[-1]
    assert m >= n_rows + n_cols - 1
    tiled = jnp.tile(vec, (1,) * (vec.ndim - 1) + (n_rows,))[..., :n_rows * (m - 1)]
    return tiled.reshape(vec.shape[:-1] + (n_rows, m - 1))[..., :n_cols]


def _gelu_tanh(x):
    return 0.5 * x * (1.0 + jnp.tanh(np.sqrt(2.0 / np.pi).astype(np.float32) * (x + 0.044715 * (x * x * x))))


IN_TM = 512
IN_TN = 512


def _in_proj_kernel(x_ref, g_ref, w_ref, o_ref):
    x = x_ref[...]
    ms = jnp.mean(x * x, axis=-1, keepdims=True)
    h = (x * lax.rsqrt(ms + RMS_EPS) * g_ref[...]).astype(BF16)
    for n in range(IN_COLS // IN_TN):
        o_ref[:, n * IN_TN:(n + 1) * IN_TN] = jnp.dot(
            h, w_ref[:, n * IN_TN:(n + 1) * IN_TN], preferred_element_type=F32).astype(BF16)


def _in_proj(x2d, g, w_bf16):
    rows = x2d.shape[0]
    return pl.pallas_call(
        _in_proj_kernel,
        grid=(rows // IN_TM,),
        in_specs=[
            pl.BlockSpec((IN_TM, D_MODEL), lambda i: (i, 0)),
            pl.BlockSpec((1, D_MODEL), lambda i: (0, 0)),
            pl.BlockSpec((D_MODEL, IN_COLS), lambda i: (0, 0), pipeline_mode=pl.Buffered(1)),
        ],
        out_specs=pl.BlockSpec((IN_TM, IN_COLS), lambda i: (i, 0)),
        out_shape=jax.ShapeDtypeStruct((rows, IN_COLS), BF16),
        compiler_params=pltpu.CompilerParams(
            dimension_semantics=("arbitrary",), vmem_limit_bytes=VMEM_LIMIT),
        name="in_proj",
    )(x2d, g.reshape(1, D_MODEL), w_bf16)


def _one_hot_place(eq, *operands):
    return jnp.einsum(eq, *operands, precision=lax.Precision.HIGHEST, preferred_element_type=F32)


def _na_selectors():
    v = np.arange(NA_ROWS)[:, None, None]
    rho = np.arange(NA_ROWS)[None, :, None]
    a = np.arange(2 * NA_ROWS - 1)[None, None, :]
    row_sel = (a == rho - v + (NA_ROWS - 1)).astype(np.float32)
    j = np.arange(GRID_W)[:, None]
    c = np.arange(GRID_W)[None, :]
    col_start = np.clip(j - NA_COLS // 2, 0, GRID_W - NA_COLS)
    valid = (c >= col_start) & (c < col_start + NA_COLS)
    b = np.arange(2 * NA_COLS - 1)[None, None, :]
    col_sel = ((b == (c - j + (NA_COLS - 1))[:, :, None]) & valid[:, :, None]).astype(np.float32)
    neg = np.where(valid, 0.0, NEG_INF).astype(np.float32)
    return row_sel, col_sel, neg


def _na_bias_table(rpb):
    row_sel, col_sel, neg = _na_selectors()
    tab = _one_hot_place('vra,jcb,lhab->lhvjrc', row_sel, col_sel, rpb.astype(F32))
    tab = tab + neg[:, None, :]
    return tab.reshape(rpb.shape[0], N_HEADS, NA_ROWS, GRID_W, NA_ROWS * GRID_W)


def _na_kernel(q_ref, k_ref, v_ref, z_ref, bias_ref, o_ref, s_scr, mx_scr, p_scr):
    seq = q_ref.shape[0]
    rows = seq // GRID_W
    nkeys = NA_ROWS * GRID_W
    lane = lax.broadcasted_iota(jnp.int32, (GRID_W, LANES), 1)
    first_head = lane < HEAD_DIM
    nt = (((1,), (1,)), ((), ()))

    def window(r):
        rs = jnp.clip(r - NA_ROWS // 2, 0, rows - NA_ROWS)
        return r - rs, pl.multiple_of(r * GRID_W, GRID_W), pl.multiple_of(rs * GRID_W, GRID_W)

    def group(gi, carry):
        r_lo = gi * NA_GROUP

        def logits(i, c):
            var, q0, k0 = window(r_lo + i)
            qb = q_ref[pl.ds(q0, GRID_W), :] * jnp.asarray(HEAD_DIM ** -0.5, BF16)
            kw = k_ref[pl.ds(k0, nkeys), :]
            for hd in range(2):
                keep = first_head if hd == 0 else jnp.logical_not(first_head)
                qm = jnp.where(keep, qb, jnp.zeros_like(qb))
                s = lax.dot_general(qm, kw, nt, preferred_element_type=F32) + bias_ref[hd, var]
                s_scr[2 * i + hd] = s
                mx_scr[2 * i + hd] = jnp.broadcast_to(jnp.max(s, axis=-1, keepdims=True), (GRID_W, LANES))
            return c

        lax.fori_loop(0, NA_GROUP, logits, 0, unroll=True)

        def probs(t, c):
            m = mx_scr[t]
            psum = None
            for kc in range(nkeys // LANES):
                sl = slice(kc * LANES, (kc + 1) * LANES)
                p = jnp.exp(s_scr[t, :, sl] - m)
                p_scr[t, :, sl] = p.astype(BF16)
                psum = p if psum is None else psum + p
            mx_scr[t] = jnp.broadcast_to(jnp.sum(psum, axis=-1, keepdims=True), (GRID_W, LANES))
            return c

        lax.fori_loop(0, 2 * NA_GROUP, probs, 0, unroll=True)

        def outputs(i, c):
            _, q0, k0 = window(r_lo + i)
            vw = v_ref[pl.ds(k0, nkeys), :]
            r0 = jnp.dot(p_scr[2 * i], vw, preferred_element_type=F32)
            r1 = jnp.dot(p_scr[2 * i + 1], vw, preferred_element_type=F32)
            acc = jnp.where(first_head, r0, r1)
            den = jnp.where(first_head, mx_scr[2 * i], mx_scr[2 * i + 1])
            z = z_ref[pl.ds(q0, GRID_W), :].astype(F32)
            o_ref[pl.ds(q0, GRID_W), :] = (acc * (1.0 / den) * _silu(z)).astype(BF16)
            return c

        lax.fori_loop(0, NA_GROUP, outputs, 0, unroll=True)
        return carry

    lax.fori_loop(0, rows // NA_GROUP, group, 0)


NA_GROUP = 8


def _na(proj, bias, batch, seq):
    def col(c0):
        return pl.BlockSpec((seq, LANES), lambda b, p, c0=c0: (b, c0 + p))

    nkeys = NA_ROWS * GRID_W
    scratch = [pltpu.VMEM((2 * NA_GROUP, GRID_W, nkeys), F32),
               pltpu.VMEM((2 * NA_GROUP, GRID_W, LANES), F32),
               pltpu.VMEM((2 * NA_GROUP, GRID_W, nkeys), BF16)]

    return pl.pallas_call(
        _na_kernel,
        grid=(batch, N_HEAD_PAIRS),
        in_specs=[
            col(COL_QB), col(COL_KB), col(COL_VB), col(COL_ZB),
            pl.BlockSpec((2, NA_ROWS, GRID_W, NA_ROWS * GRID_W), lambda b, p: (p, 0, 0, 0)),
        ],
        out_specs=pl.BlockSpec((seq, LANES), lambda b, p: (b, p)),
        out_shape=jax.ShapeDtypeStruct((batch * seq, W_BRANCH), BF16),
        scratch_shapes=scratch,
        compiler_params=pltpu.CompilerParams(
            dimension_semantics=("arbitrary", "arbitrary"), vmem_limit_bytes=VMEM_LIMIT),
        name="na_attn",
    )(proj, proj, proj, proj, bias)


DIL_QB = 128
DIL_KW = 256
N_VARIANTS = 3


def _t5_bucket(rel):
    nb = T5_BUCKETS // 2
    max_exact = nb // 2
    n = np.abs(rel)
    large = max_exact + (np.log(np.maximum(n, 1) / max_exact) / np.log(T5_MAX_DIST / max_exact)
                         * (nb - max_exact)).astype(np.int32)
    large = np.minimum(large, nb - 1)
    return (np.where(rel > 0, nb, 0) + np.where(n < max_exact, n, large)).astype(np.int32)


def _dil_bias_table(t5_bias):
    steps = np.arange(-DIL_HALF, DIL_HALF + 1)
    bucket = np.stack([_t5_bucket(d * steps) for _, d in DIL_PATTERNS])
    band = t5_bias.astype(F32)[bucket].transpose(2, 0, 1)
    pad = DIL_QB + DIL_KW - 1 - DIL_HALF
    full = jnp.pad(band, ((0, 0), (0, 0), (pad, pad)), constant_values=NEG_INF)
    zero_at = pad + DIL_HALF
    m = DIL_QB + DIL_KW
    tabs = []
    for var in range(N_VARIANTS):
        lo = zero_at - DIL_HALF * var
        vec = jnp.concatenate([full[..., lo:lo + DIL_KW],
                               jnp.full(band.shape[:-1] + (1,), NEG_INF, F32),
                               full[..., lo - (DIL_QB - 1):lo]], axis=-1)
        tabs.append(_toeplitz_rows(vec, DIL_QB, DIL_KW))
    return jnp.stack(tabs, axis=2)


DIL_GROUP = 8


def _dil_kernel(q_ref, k_ref, v_ref, z_ref, bias_ref, o_ref, qf, kf, vf, qg, kg, vg, qd, kd, vd,
                m_s, l_s, acc_s):
    seq = q_ref.shape[0]
    conv_rows = 256
    lane = lax.broadcasted_iota(jnp.int32, (DIL_QB, LANES), 1)
    first_head = lane < HEAD_DIM
    nt = (((1,), (1,)), ((), ()))

    def conv(i, carry):
        r0 = pl.multiple_of(i * conv_rows, conv_rows)
        qf[pl.ds(r0, conv_rows), :] = q_ref[pl.ds(r0, conv_rows), :].astype(F32) * (HEAD_DIM ** -0.5)
        kf[pl.ds(r0, conv_rows), :] = k_ref[pl.ds(r0, conv_rows), :].astype(F32)
        vf[pl.ds(r0, conv_rows), :] = v_ref[pl.ds(r0, conv_rows), :].astype(F32)
        return carry

    lax.fori_loop(0, seq // conv_rows, conv, 0)

    n4 = seq // 4
    n16 = seq // 16

    def to_order4(i, carry):
        dst = pl.multiple_of(i * conv_rows, conv_rows)
        src = pl.ds(dst // n4 + 4 * (dst % n4), conv_rows, stride=4)
        for a, b in ((qf, qg), (kf, kg), (vf, vg)):
            b[pl.ds(dst, conv_rows), :] = a[src, :]
        return carry

    lax.fori_loop(0, seq // conv_rows, to_order4, 0)

    def order16_rows_in_order4(rho):
        return pl.ds((rho % 4) * n4 + rho // 4, n16, stride=4)

    def to_order16(rho, carry):
        src = order16_rows_in_order4(rho)
        dst = pl.ds(pl.multiple_of(rho * n16, n16), n16)
        for a, b in ((qg, qd), (kg, kd), (vg, vd)):
            b[dst, :] = a[src, :].astype(BF16)
        return carry

    lax.fori_loop(0, 16, to_order16, 0)

    plan = ((2, 16, (qd, kd, vd), (m_s, l_s, acc_s)),
            (1, 4, (qd, kd, vd), (qf, kf, vf)),
            (0, 1, (qd, k_ref, v_ref), (m_s, l_s, acc_s)))

    for step, (pi, d, (q_src, k_src, v_src), (m_st, l_st, acc_st)) in enumerate(plan):
        n_sub = seq // d
        nblk = n_sub // DIL_QB

        if step == 1:
            def state_to_order4(rho, carry):
                dst = order16_rows_in_order4(rho)
                src = pl.ds(pl.multiple_of(rho * n16, n16), n16)
                for a, b in ((m_s, qf), (l_s, kf), (acc_s, vf)):
                    b[dst, :] = a[src, :]
                return carry

            lax.fori_loop(0, 16, state_to_order4, 0)

            def cast_order4(i, carry):
                r0 = pl.multiple_of(i * conv_rows, conv_rows)
                for a, b in ((qg, qd), (kg, kd), (vg, vd)):
                    b[pl.ds(r0, conv_rows), :] = a[pl.ds(r0, conv_rows), :].astype(BF16)
                return carry

            lax.fori_loop(0, seq // conv_rows, cast_order4, 0)
        if step == 2:
            def state_to_tokens(i, carry):
                src = pl.multiple_of(i * conv_rows, conv_rows)
                dst = pl.ds(src // n4 + 4 * (src % n4), conv_rows, stride=4)
                for a, b in ((qf, m_s), (kf, l_s), (vf, acc_s)):
                    b[dst, :] = a[pl.ds(src, conv_rows), :]
                qd[pl.ds(src, conv_rows), :] = q_ref[pl.ds(src, conv_rows), :] * jnp.asarray(
                    HEAD_DIM ** -0.5, BF16)
                return carry

            lax.fori_loop(0, seq // conv_rows, state_to_tokens, 0)

        def group(gi, carry, pi=pi, n_sub=n_sub, nblk=nblk, first=(step == 0), q_src=q_src,
                  k_src=k_src, v_src=v_src, m_st=m_st, l_st=l_st, acc_st=acc_st):
            units = []
            for i in range(DIL_GROUP):
                u = gi * DIL_GROUP + i
                res = u // nblk
                i0 = (u % nblk) * DIL_QB
                ks = jnp.clip(i0 - DIL_HALF, 0, n_sub - DIL_KW)
                units.append((res, i0, ks, (i0 - ks) // DIL_HALF))
            s_tiles, m_tiles = [], []
            for res, i0, ks, var in units:
                qb = q_src[pl.ds(pl.multiple_of(res * n_sub + i0, DIL_HALF), DIL_QB), :]
                kw = k_src[pl.ds(pl.multiple_of(res * n_sub + ks, DIL_HALF), DIL_KW), :]
                for hd in range(2):
                    keep = first_head if hd == 0 else jnp.logical_not(first_head)
                    qm = jnp.where(keep, qb, jnp.zeros_like(qb))
                    s = lax.dot_general(qm, kw, nt, preferred_element_type=F32) + bias_ref[hd, pi, var]
                    s_tiles.append(s)
                    m_tiles.append(jnp.max(s, axis=-1, keepdims=True))
            p_tiles, l_tiles = [], []
            for s, m in zip(s_tiles, m_tiles):
                p = jnp.exp(s - m)
                l_tiles.append(jnp.sum(p, axis=-1, keepdims=True))
                p_tiles.append(p.astype(BF16))
            for i, (res, i0, ks, var) in enumerate(units):
                vw = v_src[pl.ds(pl.multiple_of(res * n_sub + ks, DIL_HALF), DIL_KW), :]
                acc0 = jnp.dot(p_tiles[2 * i], vw, preferred_element_type=F32)
                acc1 = jnp.dot(p_tiles[2 * i + 1], vw, preferred_element_type=F32)
                m_cur = jnp.where(first_head, m_tiles[2 * i], m_tiles[2 * i + 1])
                l_cur = jnp.where(first_head, l_tiles[2 * i], l_tiles[2 * i + 1])
                acc_cur = jnp.where(first_head, acc0, acc1)
                q_rows = pl.ds(pl.multiple_of(res * n_sub + i0, DIL_HALF), DIL_QB)
                if first:
                    m_st[q_rows, :] = m_cur
                    l_st[q_rows, :] = l_cur
                    acc_st[q_rows, :] = acc_cur
                else:
                    m_old = m_st[q_rows, :]
                    m_new = jnp.maximum(m_old, m_cur)
                    a_old = jnp.exp(m_old - m_new)
                    a_cur = jnp.exp(m_cur - m_new)
                    m_st[q_rows, :] = m_new
                    l_st[q_rows, :] = a_old * l_st[q_rows, :] + a_cur * l_cur
                    acc_st[q_rows, :] = a_old * acc_st[q_rows, :] + a_cur * acc_cur
            return carry

        lax.fori_loop(0, seq // DIL_QB // DIL_GROUP, group, 0)

    def fin(i, carry):
        r0 = pl.multiple_of(i * conv_rows, conv_rows)
        z = z_ref[pl.ds(r0, conv_rows), :].astype(F32)
        y = acc_s[pl.ds(r0, conv_rows), :] * (1.0 / l_s[pl.ds(r0, conv_rows), :])
        o_ref[pl.ds(r0, conv_rows), :] = (y * _silu(z)).astype(BF16)
        return carry

    lax.fori_loop(0, seq // conv_rows, fin, 0)


def _dil(proj, bias, batch, seq):
    def col(c0):
        return pl.BlockSpec((seq, LANES), lambda b, p, c0=c0: (b, c0 + p))

    scratch = ([pltpu.VMEM((seq, LANES), F32) for _ in range(6)]
               + [pltpu.VMEM((seq, LANES), BF16) for _ in range(3)]
               + [pltpu.VMEM((seq, LANES), F32) for _ in range(3)])
    return pl.pallas_call(
        _dil_kernel,
        grid=(batch, N_HEAD_PAIRS),
        in_specs=[
            col(COL_QC), col(COL_KC), col(COL_VC), col(COL_ZC),
            pl.BlockSpec((2, len(DIL_PATTERNS), N_VARIANTS, DIL_QB, DIL_KW),
                         lambda b, p: (p, 0, 0, 0, 0)),
        ],
        out_specs=pl.BlockSpec((seq, LANES), lambda b, p: (b, p)),
        out_shape=jax.ShapeDtypeStruct((batch * seq, W_BRANCH), BF16),
        scratch_shapes=scratch,
        compiler_params=pltpu.CompilerParams(
            dimension_semantics=("arbitrary", "arbitrary"), vmem_limit_bytes=VMEM_LIMIT),
        name="dil_attn",
    )(proj, proj, proj, proj, bias)


N_POW = CHUNK + 1
N_LAGS = 2 * CHUNK - 1
PREP_GROUPS = 8


def _ssm_prep_kernel(lre_ref, lim_ref, ldt_ref, btre_ref, btim_ref, cre_ref, cim_ref,
                     lbb_re_ref, lbb_im_ref, cl_re_ref, cl_nim_ref, pw_re_ref, pw_im_ref, klag_ref):
    zpad = jnp.zeros((LANES - SSM_GROUP, SSM_STATE), F32)
    nt = (((1,), (1,)), ((), ()))

    def group(g, carry):
        lag0 = None
        for dr in range(2):
            lam_re = lre_ref[0, dr, g]
            lam_im = lim_ref[0, dr, g]
            dt = jnp.exp(ldt_ref[0, dr, g])
            mag = jnp.exp(lam_re * dt)
            lb_re = mag * jnp.cos(lam_im * dt)
            lb_im = mag * jnp.sin(lam_im * dt)
            n_re = lb_re - 1.0
            den = lam_re * lam_re + lam_im * lam_im
            q_re = (n_re * lam_re + lb_im * lam_im) / den
            q_im = (lb_im * lam_re - n_re * lam_im) / den
            bt_re = btre_ref[0, dr, g]
            bt_im = btim_ref[0, dr, g]
            bb_re = q_re * bt_re - q_im * bt_im
            bb_im = q_re * bt_im + q_im * bt_re
            bb_re_pad = jnp.concatenate([bb_re, zpad], axis=0)
            bb_im_pad = jnp.concatenate([bb_im, zpad], axis=0)
            c_re = cre_ref[0, dr, g]
            c_im = cim_ref[0, dr, g]
            pw_re = jnp.ones_like(lb_re)
            pw_im = jnp.zeros_like(lb_re)
            for j in range(N_POW):
                lbb_re_ref[0, dr, g, j] = pw_re * bb_re - pw_im * bb_im
                lbb_im_ref[0, dr, g, j] = pw_re * bb_im + pw_im * bb_re
                y_re = pw_re * c_re - pw_im * c_im
                y_im = pw_re * c_im + pw_im * c_re
                cl_re_ref[0, dr, g, j] = y_re
                cl_nim_ref[0, dr, g, j] = -y_im
                if j < CHUNK:
                    lag = (lax.dot_general(y_re, bb_re_pad, nt, precision=lax.Precision.HIGHEST,
                                           preferred_element_type=F32)
                           - lax.dot_general(y_im, bb_im_pad, nt, precision=lax.Precision.HIGHEST,
                                             preferred_element_type=F32))
                    if j == 0:
                        lag0 = lag if lag0 is None else lag0 + lag
                    else:
                        klag_ref[0, g, (CHUNK - 1) + (j if dr == 0 else -j)] = lag
                if j == CHUNK:
                    pw_re_ref[0, dr, g] = pw_re
                    pw_im_ref[0, dr, g] = pw_im
                pw_re, pw_im = pw_re * lb_re - pw_im * lb_im, pw_re * lb_im + pw_im * lb_re
        klag_ref[0, g, CHUNK - 1] = lag0
        return carry

    lax.fori_loop(0, PREP_GROUPS, group, 0)


def _ssm_selectors():
    per = LANES // SSM_GROUP
    j = np.arange(per)[:, None]
    a = np.arange(CHUNK)[None, :]
    step = per * (a // per) + (a % per - j) % per
    k = np.arange(N_POW)[None, None, :]
    one_hot = lambda idx: (k == idx[:, :, None]).astype(np.float32)
    sel_in_f = one_hot(CHUNK - 1 - step)
    sel_in_b = one_hot(step)
    sel_out_f = one_hot(step + 1)
    sel_out_b = one_hot(CHUNK - step)
    m = np.arange(N_LAGS)[None, None, None, :]
    sel_lag = (m == (CHUNK - 1 + step[:, None, :] - step[:, :, None])[..., None]).astype(np.float32)
    return sel_in_f, sel_in_b, sel_out_f, sel_out_b, sel_lag


def _ssm_prep(lam_re, lam_im, log_dt, b_re, b_im, c_re, c_im):
    depth = lam_re.shape[0]
    G, P, C = SSM_GROUPS, SSM_STATE, SSM_GROUP
    rep = lambda a: jnp.broadcast_to(a.astype(F32)[:, :, :, None, :], (depth, 2, G, C, P))
    ldt = jnp.broadcast_to(log_dt.astype(F32)[:, :, :, None, None], (depth, 2, G, C, P))
    bt = lambda a: a.astype(F32).transpose(0, 1, 2, 4, 3)
    gb = PREP_GROUPS
    spec5 = pl.BlockSpec((1, 2, gb, C, P), lambda l, i: (l, 0, i, 0, 0))
    spec6 = pl.BlockSpec((1, 2, gb, N_POW, C, P), lambda l, i: (l, 0, i, 0, 0, 0))
    shp6 = jax.ShapeDtypeStruct((depth, 2, G, N_POW, C, P), F32)
    shp5 = jax.ShapeDtypeStruct((depth, 2, G, C, P), F32)
    lbb_re, lbb_im, cl_re, cl_nim, pw_re, pw_im, klag = pl.pallas_call(
        _ssm_prep_kernel,
        grid=(depth, G // gb),
        in_specs=[spec5] * 7,
        out_specs=[spec6, spec6, spec6, spec6, spec5, spec5,
                   pl.BlockSpec((1, gb, N_LAGS, C, LANES), lambda l, i: (l, i, 0, 0, 0))],
        out_shape=[shp6, shp6, shp6, shp6, shp5, shp5,
                   jax.ShapeDtypeStruct((depth, G, N_LAGS, C, LANES), F32)],
        compiler_params=pltpu.CompilerParams(
            dimension_semantics=("arbitrary", "arbitrary"), vmem_limit_bytes=VMEM_LIMIT),
        name="ssm_prep",
    )(rep(lam_re), rep(lam_im), ldt, bt(b_re), bt(b_im), c_re.astype(F32), c_im.astype(F32))

    n = CHUNK * C
    per = GROUPS_PER_SLAB
    sel_in_f, sel_in_b, sel_out_f, sel_out_b, sel_lag = _ssm_selectors()
    by_slab = lambda a: a.reshape(depth, G // per, per, *a.shape[2:])
    place_in = lambda sel, a: _one_hot_place('jak,lqjkcp->lqjacp', sel, by_slab(a))
    w_in = jnp.concatenate([place_in(sel_in_f, lbb_re[:, 0]), place_in(sel_in_b, lbb_re[:, 1]),
                            place_in(sel_in_f, lbb_im[:, 0]), place_in(sel_in_b, lbb_im[:, 1])], axis=-1)
    w_in = w_in.reshape(depth, G, n, 4 * P)
    place_out = lambda sel, a: _one_hot_place('jbk,lqjkcp->lqjpbc', sel, by_slab(a)).reshape(depth, G, P, n)
    w_out = jnp.concatenate([place_out(sel_out_f, cl_re[:, 0]), place_out(sel_out_b, cl_re[:, 1]),
                             place_out(sel_out_f, cl_nim[:, 0]), place_out(sel_out_b, cl_nim[:, 1])], axis=2)
    toep = _one_hot_place('jabm,lqjmcd->lqjadbc', sel_lag, by_slab(klag[..., :C]))
    toep = toep.reshape(depth, G, n, n)
    both = lambda a: jnp.concatenate([a[:, 0, :, 0, :], a[:, 1, :, 0, :]], axis=-1)
    decay = jnp.stack([both(pw_re), both(pw_im)], axis=1)
    return toep.astype(BF16), w_in.astype(BF16), w_out.astype(BF16), decay


GROUPS_PER_SLAB = LANES // SSM_GROUP
N_SLABS = W_BRANCH // LANES
RELAYOUT_ROWS = 32


def _ssm_kernel(xa_ref, toep_ref, win_ref, wout_ref, decay_ref, d_ref, o_ref, xf, u_s, st, hs, yq):
    seq = xa_ref.shape[0]
    nchunk = seq // CHUNK
    nrb = nchunk // RELAYOUT_ROWS
    gps = GROUPS_PER_SLAB
    lane_rb = lax.broadcasted_iota(jnp.int32, (RELAYOUT_ROWS, LANES), 1)
    seg_masks = [(lane_rb >= SSM_GROUP * sg) & (lane_rb < SSM_GROUP * (sg + 1)) for sg in range(gps)]
    conv_rows = 512

    for q in range(N_SLABS):
        c_lo = q * LANES

        def conv(i, carry, c_lo=c_lo):
            r0 = pl.multiple_of(i * conv_rows, conv_rows)
            xf[pl.ds(r0, conv_rows), :] = xa_ref[pl.ds(r0, conv_rows), c_lo:c_lo + LANES].astype(F32)
            return carry

        lax.fori_loop(0, seq // conv_rows, conv, 0)

        for hh in range(2):
            def fwd_relayout(rb, carry, hh=hh):
                c0 = pl.multiple_of(rb * RELAYOUT_ROWS, RELAYOUT_ROWS)
                rolled = []
                for r in range(gps):
                    xs = xf[pl.ds(CHUNK * c0 + gps * hh + r, RELAYOUT_ROWS, stride=CHUNK), :]
                    rolled.append(xs if r == 0 else pltpu.roll(xs, SSM_GROUP * r, axis=1))
                for j in range(gps):
                    out = rolled[(0 - j) % gps]
                    for sg in range(1, gps):
                        out = jnp.where(seg_masks[sg], rolled[(sg - j) % gps], out)
                    u_s[j, pl.ds(c0, RELAYOUT_ROWS), hh * LANES:(hh + 1) * LANES] = out.astype(BF16)
                return carry

            lax.fori_loop(0, nrb, fwd_relayout, 0, unroll=2)

        for j in range(gps):
            s_in = jnp.dot(u_s[j], win_ref[gps * q + j], preferred_element_type=F32)
            st[0, pl.ds(j, nchunk, stride=gps), :] = s_in[:, :LANES]
            st[1, pl.ds(j, nchunk, stride=gps), :] = s_in[:, LANES:]

        g_lo = gps * q
        a_re = decay_ref[0, g_lo:g_lo + gps, :]
        a_im = decay_ref[1, g_lo:g_lo + gps, :]
        lo, hi = slice(0, SSM_STATE), slice(SSM_STATE, LANES)

        def scan(i, carry):
            hrf, hif, hrb, hib = carry
            rf = pl.ds(pl.multiple_of(i * gps, gps), gps)
            rb_ = pl.ds(pl.multiple_of((nchunk - 1 - i) * gps, gps), gps)
            hs[0, rf, lo] = hrf[:, lo]
            hs[1, rf, lo] = hif[:, lo]
            hs[0, rb_, hi] = hrb[:, hi]
            hs[1, rb_, hi] = hib[:, hi]
            return (a_re * hrf - a_im * hif + st[0, rf, :], a_re * hif + a_im * hrf + st[1, rf, :],
                    a_re * hrb - a_im * hib + st[0, rb_, :], a_re * hib + a_im * hrb + st[1, rb_, :])

        zero = jnp.zeros((gps, LANES), F32)
        lax.fori_loop(0, nchunk, scan, (zero, zero, zero, zero), unroll=8)

        for j in range(gps):
            g = gps * q + j
            h_in = jnp.concatenate([hs[0, pl.ds(j, nchunk, stride=gps), :].astype(BF16),
                                    hs[1, pl.ds(j, nchunk, stride=gps), :].astype(BF16)], axis=-1)
            yq[j] = (jnp.dot(u_s[j], toep_ref[g], preferred_element_type=F32)
                     + jnp.dot(h_in, wout_ref[g], preferred_element_type=F32))

        d_row = d_ref[:, c_lo:c_lo + LANES]
        for hh in range(2):
            def bwd_relayout(rb, carry, hh=hh, d_row=d_row, q=q):
                c0 = pl.multiple_of(rb * RELAYOUT_ROWS, RELAYOUT_ROWS)
                ys = [yq[j, pl.ds(c0, RELAYOUT_ROWS), hh * LANES:(hh + 1) * LANES] for j in range(gps)]
                for r in range(gps):
                    merged = ys[(0 - r) % gps]
                    for sg in range(1, gps):
                        merged = jnp.where(seg_masks[sg], ys[(sg - r) % gps], merged)
                    if r:
                        merged = pltpu.roll(merged, LANES - SSM_GROUP * r, axis=1)
                    tok = pl.ds(CHUNK * c0 + gps * hh + r, RELAYOUT_ROWS, stride=CHUNK)
                    o_ref[q, tok, :] = merged + d_row * xf[tok, :]
                return carry

            lax.fori_loop(0, nrb, bwd_relayout, 0, unroll=2)


def _ssm(proj, toep, w_in, w_out, decay, d_skip, batch, seq):
    nchunk = seq // CHUNK
    n = CHUNK * SSM_GROUP
    wspec = pl.BlockSpec((SSM_GROUPS, n, n), lambda b: (0, 0, 0), pipeline_mode=pl.Buffered(1))
    return pl.pallas_call(
        _ssm_kernel,
        grid=(batch,),
        in_specs=[
            pl.BlockSpec((seq, W_BRANCH), lambda b: (b, COL_XA // N_SLABS)),
            wspec, wspec, wspec,
            pl.BlockSpec((2, SSM_GROUPS, LANES), lambda b: (0, 0, 0)),
            pl.BlockSpec((1, W_BRANCH), lambda b: (0, 0)),
        ],
        out_specs=pl.BlockSpec((N_SLABS, seq, LANES), lambda b: (0, b, 0)),
        out_shape=jax.ShapeDtypeStruct((N_SLABS, batch * seq, LANES), F32),
        scratch_shapes=[
            pltpu.VMEM((seq, LANES), F32),
            pltpu.VMEM((GROUPS_PER_SLAB, nchunk, n), BF16),
            pltpu.VMEM((2, nchunk * GROUPS_PER_SLAB, LANES), F32),
            pltpu.VMEM((2, nchunk * GROUPS_PER_SLAB, LANES), F32),
            pltpu.VMEM((GROUPS_PER_SLAB, nchunk, n), F32),
        ],
        compiler_params=pltpu.CompilerParams(
            dimension_semantics=("arbitrary",), vmem_limit_bytes=VMEM_LIMIT),
        name="ssm_mix",
    )(proj, toep, w_in, w_out, decay, d_skip.astype(F32).reshape(1, W_BRANCH))


OUT_TM = 512


def _out_proj_kernel(ya_ref, za_ref, yb_ref, yc_ref, x_ref, w_ref, gw_ref, gb_ref, fg_ref, o_ref, *, final):
    y = jnp.concatenate([ya_ref[i] for i in range(N_SLABS)], axis=-1)
    g = _gelu_tanh(y)
    gate = jnp.dot(g.astype(BF16), gw_ref[...], preferred_element_type=F32) + gb_ref[...]
    ya = g * (1.0 / (1.0 + jnp.exp(-gate))) * _silu(za_ref[...].astype(F32))
    delta = (jnp.dot(ya.astype(BF16), w_ref[:W_BRANCH, :], preferred_element_type=F32)
             + jnp.dot(yb_ref[...], w_ref[W_BRANCH:2 * W_BRANCH, :], preferred_element_type=F32)
             + jnp.dot(yc_ref[...], w_ref[2 * W_BRANCH:, :], preferred_element_type=F32))
    x = x_ref[...] + delta
    if final:
        ms = jnp.mean(x * x, axis=-1, keepdims=True)
        x = x * lax.rsqrt(ms + RMS_EPS) * fg_ref[...]
    o_ref[...] = x


def _out_proj(ya_pre, proj, yb, yc, x2d, w_bf16, glu_w_bf16, glu_b, final_g, final):
    rows = x2d.shape[0]
    row_blk = lambda width: pl.BlockSpec((OUT_TM, width), lambda i: (i, 0))
    const = lambda shape: pl.BlockSpec(shape, lambda i: (0,) * len(shape), pipeline_mode=pl.Buffered(1))
    return pl.pallas_call(
        functools.partial(_out_proj_kernel, final=final),
        grid=(rows // OUT_TM,),
        in_specs=[
            pl.BlockSpec((N_SLABS, OUT_TM, LANES), lambda i: (0, i, 0)),
            pl.BlockSpec((OUT_TM, W_BRANCH), lambda i: (i, COL_ZA // N_SLABS)),
            row_blk(W_BRANCH), row_blk(W_BRANCH), row_blk(D_MODEL),
            const((MIX_WIDTH, D_MODEL)), const((W_BRANCH, W_BRANCH)),
            const((1, W_BRANCH)), const((1, D_MODEL)),
        ],
        out_specs=row_blk(D_MODEL),
        out_shape=jax.ShapeDtypeStruct((rows, D_MODEL), F32),
        compiler_params=pltpu.CompilerParams(
            dimension_semantics=("arbitrary",), vmem_limit_bytes=VMEM_LIMIT),
        name="out_proj_final" if final else "out_proj",
    )(ya_pre, proj, yb, yc, x2d, w_bf16, glu_w_bf16,
      glu_b.astype(F32).reshape(1, W_BRANCH), final_g.astype(F32).reshape(1, D_MODEL))


def kernel(x, norm_g, w_in, w_out, ssm_lam_re, ssm_lam_im, ssm_log_dt, ssm_b_re, ssm_b_im, ssm_c_re,
           ssm_c_im, ssm_d, glu_w, glu_b, na_rpb, t5_bias, final_g):
    batch, seq, _ = x.shape
    depth = w_in.shape[0]
    x2d = x.astype(F32).reshape(batch * seq, D_MODEL)
    toep, s_in, s_out, decay = _ssm_prep(ssm_lam_re, ssm_lam_im, ssm_log_dt, ssm_b_re, ssm_b_im,
                                         ssm_c_re, ssm_c_im)
    dil_bias = _dil_bias_table(t5_bias)
    na_bias = _na_bias_table(na_rpb)
    w_in_bf, w_out_bf, glu_w_bf = w_in.astype(BF16), w_out.astype(BF16), glu_w.astype(BF16)
    for l in range(depth):
        proj = _in_proj(x2d, norm_g[l].astype(F32), w_in_bf[l])
        ya_pre = _ssm(proj, toep[l], s_in[l], s_out[l], decay[l], ssm_d[l], batch, seq)
        yb = _na(proj, na_bias[l], batch, seq)
        yc = _dil(proj, dil_bias, batch, seq)
        x2d = _out_proj(ya_pre, proj, yb, yc, x2d, w_out_bf[l], glu_w_bf[l],
                        glu_b[l], final_g, final=(l == depth - 1))
    return x2d.reshape(batch, seq, D_MODEL).astype(x.dtype)
```

```python
import functools

import numpy as np
import jax
import jax.numpy as jnp
from jax import lax
from jax.experimental import pallas as pl
from jax.experimental.pallas import tpu as pltpu

F32 = jnp.float32
BF16 = jnp.bfloat16

D_MODEL = 1024
HEAD_DIM = 64
W_BRANCH = 512
N_HEADS = W_BRANCH // HEAD_DIM
N_HEAD_PAIRS = N_HEADS // 2
SSM_GROUP = 16
SSM_GROUPS = W_BRANCH // SSM_GROUP
SSM_STATE = 64
GRID_W = 64
NA_ROWS = 8
NA_COLS = 16
DIL_PATTERNS = ((128, 1), (512, 4), (2048, 16))
DIL_HALF = 64
T5_BUCKETS = 32
T5_MAX_DIST = 1024
RMS_EPS = 1e-6
NEG_INF = -1e30
IN_COLS = 10 * W_BRANCH
MIX_WIDTH = 3 * W_BRANCH

COL_XA, COL_ZA, COL_QB, COL_KB, COL_VB, COL_ZB, COL_QC, COL_KC, COL_VC, COL_ZC = (
    4 * i for i in range(10))

LANES = 128
CHUNK = 16
VMEM_LIMIT = 56 * 1024 * 1024


def _silu(z):
    return z * (1.0 / (1.0 + jnp.exp(-z)))


def _toeplitz_rows(vec, n_rows, n_cols):
    m = vec.shape[-1]
    assert m >= n_rows + n_cols - 1
    tiled = jnp.tile(vec, (1,) * (vec.ndim - 1) + (n_rows,))[..., :n_rows * (m - 1)]
    return tiled.reshape(vec.shape[:-1] + (n_rows, m - 1))[..., :n_cols]


def _gelu_tanh(x):
    return 0.5 * x * (1.0 + jnp.tanh(np.sqrt(2.0 / np.pi).astype(np.float32) * (x + 0.044715 * (x * x * x))))


IN_TM = 512
IN_TN = 512


def _in_proj_kernel(x_ref, g_ref, w_ref, o_ref):
    x = x_ref[...]
    ms = jnp.mean(x * x, axis=-1, keepdims=True)
    h = (x * lax.rsqrt(ms + RMS_EPS) * g_ref[...]).astype(BF16)
    for n in range(IN_COLS // IN_TN):
        o_ref[:, n * IN_TN:(n + 1) * IN_TN] = jnp.dot(
            h, w_ref[:, n * IN_TN:(n + 1) * IN_TN], preferred_element_type=F32).astype(BF16)


def _in_proj(x2d, g, w_bf16, layer):
    rows = x2d.shape[0]
    return pl.pallas_call(
        _in_proj_kernel,
        grid=(rows // IN_TM,),
        in_specs=[
            pl.BlockSpec((IN_TM, D_MODEL), lambda i: (i, 0)),
            pl.BlockSpec((None, 1, D_MODEL), lambda i: (layer, 0, 0)),
            pl.BlockSpec((None, D_MODEL, IN_COLS), lambda i: (layer, 0, 0), pipeline_mode=pl.Buffered(1)),
        ],
        out_specs=pl.BlockSpec((IN_TM, IN_COLS), lambda i: (i, 0)),
        out_shape=jax.ShapeDtypeStruct((rows, IN_COLS), BF16),
        compiler_params=pltpu.CompilerParams(
            dimension_semantics=("arbitrary",), vmem_limit_bytes=VMEM_LIMIT),
        name="in_proj",
    )(x2d, g, w_bf16)


def _one_hot_place(eq, *operands):
    return jnp.einsum(eq, *operands, precision=lax.Precision.HIGHEST, preferred_element_type=F32)


def _na_selectors():
    v = np.arange(NA_ROWS)[:, None, None]
    rho = np.arange(NA_ROWS)[None, :, None]
    a = np.arange(2 * NA_ROWS - 1)[None, None, :]
    row_sel = (a == rho - v + (NA_ROWS - 1)).astype(np.float32)
    j = np.arange(GRID_W)[:, None]
    c = np.arange(GRID_W)[None, :]
    col_start = np.clip(j - NA_COLS // 2, 0, GRID_W - NA_COLS)
    valid = (c >= col_start) & (c < col_start + NA_COLS)
    b = np.arange(2 * NA_COLS - 1)[None, None, :]
    col_sel = ((b == (c - j + (NA_COLS - 1))[:, :, None]) & valid[:, :, None]).astype(np.float32)
    neg = np.where(valid, 0.0, NEG_INF).astype(np.float32)
    return row_sel, col_sel, neg


def _na_bias_table(rpb):
    row_sel, col_sel, neg = _na_selectors()
    tab = _one_hot_place('vra,jcb,lhab->lhvjrc', row_sel, col_sel, rpb.astype(F32))
    tab = tab + neg[:, None, :]
    return tab.reshape(rpb.shape[0], N_HEADS, NA_ROWS, GRID_W, NA_ROWS * GRID_W)


def _na_kernel(q_ref, k_ref, v_ref, z_ref, bias_ref, o_ref, s_scr, mx_scr, p_scr):
    seq = q_ref.shape[0]
    rows = seq // GRID_W
    nkeys = NA_ROWS * GRID_W
    lane = lax.broadcasted_iota(jnp.int32, (GRID_W, LANES), 1)
    first_head = lane < HEAD_DIM
    nt = (((1,), (1,)), ((), ()))

    def window(r):
        rs = jnp.clip(r - NA_ROWS // 2, 0, rows - NA_ROWS)
        return r - rs, pl.multiple_of(r * GRID_W, GRID_W), pl.multiple_of(rs * GRID_W, GRID_W)

    def group(gi, carry):
        r_lo = gi * NA_GROUP

        def logits(i, c):
            var, q0, k0 = window(r_lo + i)
            qb = q_ref[pl.ds(q0, GRID_W), :] * jnp.asarray(HEAD_DIM ** -0.5, BF16)
            kw = k_ref[pl.ds(k0, nkeys), :]
            for hd in range(2):
                keep = first_head if hd == 0 else jnp.logical_not(first_head)
                qm = jnp.where(keep, qb, jnp.zeros_like(qb))
                s = lax.dot_general(qm, kw, nt, preferred_element_type=F32) + bias_ref[hd, var]
                s_scr[2 * i + hd] = s
                mx_scr[2 * i + hd] = jnp.broadcast_to(jnp.max(s, axis=-1, keepdims=True), (GRID_W, LANES))
            return c

        lax.fori_loop(0, NA_GROUP, logits, 0, unroll=True)

        def probs(t, c):
            m = mx_scr[t]
            psum = None
            for kc in range(nkeys // LANES):
                sl = slice(kc * LANES, (kc + 1) * LANES)
                p = jnp.exp(s_scr[t, :, sl] - m)
                p_scr[t, :, sl] = p.astype(BF16)
                psum = p if psum is None else psum + p
            mx_scr[t] = jnp.broadcast_to(jnp.sum(psum, axis=-1, keepdims=True), (GRID_W, LANES))
            return c

        lax.fori_loop(0, 2 * NA_GROUP, probs, 0, unroll=True)

        def outputs(i, c):
            _, q0, k0 = window(r_lo + i)
            vw = v_ref[pl.ds(k0, nkeys), :]
            r0 = jnp.dot(p_scr[2 * i], vw, preferred_element_type=F32)
            r1 = jnp.dot(p_scr[2 * i + 1], vw, preferred_element_type=F32)
            acc = jnp.where(first_head, r0, r1)
            den = jnp.where(first_head, mx_scr[2 * i], mx_scr[2 * i + 1])
            z = z_ref[pl.ds(q0, GRID_W), :].astype(F32)
            o_ref[pl.ds(q0, GRID_W), :] = (acc * (1.0 / den) * _silu(z)).astype(BF16)
            return c

        lax.fori_loop(0, NA_GROUP, outputs, 0, unroll=True)
        return carry

    lax.fori_loop(0, rows // NA_GROUP, group, 0)


NA_GROUP = 8


def _na(proj, bias, layer, batch, seq):
    def col(c0):
        return pl.BlockSpec((seq, LANES), lambda b, p, c0=c0: (b, c0 + p))

    nkeys = NA_ROWS * GRID_W
    scratch = [pltpu.VMEM((2 * NA_GROUP, GRID_W, nkeys), F32),
               pltpu.VMEM((2 * NA_GROUP, GRID_W, LANES), F32),
               pltpu.VMEM((2 * NA_GROUP, GRID_W, nkeys), BF16)]

    return pl.pallas_call(
        _na_kernel,
        grid=(batch, N_HEAD_PAIRS),
        in_specs=[
            col(COL_QB), col(COL_KB), col(COL_VB), col(COL_ZB),
            pl.BlockSpec((None, 2, NA_ROWS, GRID_W, NA_ROWS * GRID_W), lambda b, p: (layer, p, 0, 0, 0)),
        ],
        out_specs=pl.BlockSpec((seq, LANES), lambda b, p: (b, p)),
        out_shape=jax.ShapeDtypeStruct((batch * seq, W_BRANCH), BF16),
        scratch_shapes=scratch,
        compiler_params=pltpu.CompilerParams(
            dimension_semantics=("arbitrary", "arbitrary"), vmem_limit_bytes=VMEM_LIMIT),
        name="na_attn",
    )(proj, proj, proj, proj, bias)


DIL_QB = 128
DIL_KW = 256
N_VARIANTS = 3


def _t5_bucket(rel):
    nb = T5_BUCKETS // 2
    max_exact = nb // 2
    n = np.abs(rel)
    large = max_exact + (np.log(np.maximum(n, 1) / max_exact) / np.log(T5_MAX_DIST / max_exact)
                         * (nb - max_exact)).astype(np.int32)
    large = np.minimum(large, nb - 1)
    return (np.where(rel > 0, nb, 0) + np.where(n < max_exact, n, large)).astype(np.int32)


def _dil_bias_table(t5_bias):
    steps = np.arange(-DIL_HALF, DIL_HALF + 1)
    bucket = np.stack([_t5_bucket(d * steps) for _, d in DIL_PATTERNS])
    band = t5_bias.astype(F32)[bucket].transpose(2, 0, 1)
    pad = DIL_QB + DIL_KW - 1 - DIL_HALF
    full = jnp.pad(band, ((0, 0), (0, 0), (pad, pad)), constant_values=NEG_INF)
    zero_at = pad + DIL_HALF
    m = DIL_QB + DIL_KW
    tabs = []
    for var in range(N_VARIANTS):
        lo = zero_at - DIL_HALF * var
        vec = jnp.concatenate([full[..., lo:lo + DIL_KW],
                               jnp.full(band.shape[:-1] + (1,), NEG_INF, F32),
                               full[..., lo - (DIL_QB - 1):lo]], axis=-1)
        tabs.append(_toeplitz_rows(vec, DIL_QB, DIL_KW))
    return jnp.stack(tabs, axis=2)


DIL_GROUP = 8


def _dil_kernel(q_ref, k_ref, v_ref, z_ref, bias_ref, o_ref, qf, kf, vf, qg, kg, vg, qd, kd, vd,
                m_s, l_s, acc_s):
    seq = q_ref.shape[0]
    conv_rows = 256
    lane = lax.broadcasted_iota(jnp.int32, (DIL_QB, LANES), 1)
    first_head = lane < HEAD_DIM
    nt = (((1,), (1,)), ((), ()))

    def conv(i, carry):
        r0 = pl.multiple_of(i * conv_rows, conv_rows)
        qf[pl.ds(r0, conv_rows), :] = q_ref[pl.ds(r0, conv_rows), :].astype(F32) * (HEAD_DIM ** -0.5)
        kf[pl.ds(r0, conv_rows), :] = k_ref[pl.ds(r0, conv_rows), :].astype(F32)
        vf[pl.ds(r0, conv_rows), :] = v_ref[pl.ds(r0, conv_rows), :].astype(F32)
        return carry

    lax.fori_loop(0, seq // conv_rows, conv, 0)

    n4 = seq // 4
    n16 = seq // 16

    def to_order4(i, carry):
        dst = pl.multiple_of(i * conv_rows, conv_rows)
        src = pl.ds(dst // n4 + 4 * (dst % n4), conv_rows, stride=4)
        for a, b in ((qf, qg), (kf, kg), (vf, vg)):
            b[pl.ds(dst, conv_rows), :] = a[src, :]
        return carry

    lax.fori_loop(0, seq // conv_rows, to_order4, 0)

    def order16_rows_in_order4(rho):
        return pl.ds((rho % 4) * n4 + rho // 4, n16, stride=4)

    def to_order16(rho, carry):
        src = order16_rows_in_order4(rho)
        dst = pl.ds(pl.multiple_of(rho * n16, n16), n16)
        for a, b in ((qg, qd), (kg, kd), (vg, vd)):
            b[dst, :] = a[src, :].astype(BF16)
        return carry

    lax.fori_loop(0, 16, to_order16, 0)

    plan = ((2, 16, (qd, kd, vd), (m_s, l_s, acc_s)),
            (1, 4, (qd, kd, vd), (qf, kf, vf)),
            (0, 1, (qd, k_ref, v_ref), (m_s, l_s, acc_s)))

    for step, (pi, d, (q_src, k_src, v_src), (m_st, l_st, acc_st)) in enumerate(plan):
        n_sub = seq // d
        nblk = n_sub // DIL_QB

        if step == 1:
            def state_to_order4(rho, carry):
                dst = order16_rows_in_order4(rho)
                src = pl.ds(pl.multiple_of(rho * n16, n16), n16)
                for a, b in ((m_s, qf), (l_s, kf), (acc_s, vf)):
                    b[dst, :] = a[src, :]
                return carry

            lax.fori_loop(0, 16, state_to_order4, 0)

            def cast_order4(i, carry):
                r0 = pl.multiple_of(i * conv_rows, conv_rows)
                for a, b in ((qg, qd), (kg, kd), (vg, vd)):
                    b[pl.ds(r0, conv_rows), :] = a[pl.ds(r0, conv_rows), :].astype(BF16)
                return carry

            lax.fori_loop(0, seq // conv_rows, cast_order4, 0)
        if step == 2:
            def state_to_tokens(i, carry):
                src = pl.multiple_of(i * conv_rows, conv_rows)
                dst = pl.ds(src // n4 + 4 * (src % n4), conv_rows, stride=4)
                for a, b in ((qf, m_s), (kf, l_s), (vf, acc_s)):
                    b[dst, :] = a[pl.ds(src, conv_rows), :]
                qd[pl.ds(src, conv_rows), :] = q_ref[pl.ds(src, conv_rows), :] * jnp.asarray(
                    HEAD_DIM ** -0.5, BF16)
                return carry

            lax.fori_loop(0, seq // conv_rows, state_to_tokens, 0)

        def group(gi, carry, pi=pi, n_sub=n_sub, nblk=nblk, first=(step == 0), q_src=q_src,
                  k_src=k_src, v_src=v_src, m_st=m_st, l_st=l_st, acc_st=acc_st):
            units = []
            for i in range(DIL_GROUP):
                u = gi * DIL_GROUP + i
                res = u // nblk
                i0 = (u % nblk) * DIL_QB
                ks = jnp.clip(i0 - DIL_HALF, 0, n_sub - DIL_KW)
                units.append((res, i0, ks, (i0 - ks) // DIL_HALF))
            s_tiles, m_tiles = [], []
            for res, i0, ks, var in units:
                qb = q_src[pl.ds(pl.multiple_of(res * n_sub + i0, DIL_HALF), DIL_QB), :]
                kw = k_src[pl.ds(pl.multiple_of(res * n_sub + ks, DIL_HALF), DIL_KW), :]
                for hd in range(2):
                    keep = first_head if hd == 0 else jnp.logical_not(first_head)
                    qm = jnp.where(keep, qb, jnp.zeros_like(qb))
                    s = lax.dot_general(qm, kw, nt, preferred_element_type=F32) + bias_ref[hd, pi, var]
                    s_tiles.append(s)
                    m_tiles.append(jnp.max(s, axis=-1, keepdims=True))
            p_tiles, l_tiles = [], []
            for s, m in zip(s_tiles, m_tiles):
                p = jnp.exp(s - m)
                l_tiles.append(jnp.sum(p, axis=-1, keepdims=True))
                p_tiles.append(p.astype(BF16))
            for i, (res, i0, ks, var) in enumerate(units):
                vw = v_src[pl.ds(pl.multiple_of(res * n_sub + ks, DIL_HALF), DIL_KW), :]
                acc0 = jnp.dot(p_tiles[2 * i], vw, preferred_element_type=F32)
                acc1 = jnp.dot(p_tiles[2 * i + 1], vw, preferred_element_type=F32)
                m_cur = jnp.where(first_head, m_tiles[2 * i], m_tiles[2 * i + 1])
                l_cur = jnp.where(first_head, l_tiles[2 * i], l_tiles[2 * i + 1])
                acc_cur = jnp.where(first_head, acc0, acc1)
                q_rows = pl.ds(pl.multiple_of(res * n_sub + i0, DIL_HALF), DIL_QB)
                if first:
                    m_st[q_rows, :] = m_cur
                    l_st[q_rows, :] = l_cur
                    acc_st[q_rows, :] = acc_cur
                else:
                    m_old = m_st[q_rows, :]
                    m_new = jnp.maximum(m_old, m_cur)
                    a_old = jnp.exp(m_old - m_new)
                    a_cur = jnp.exp(m_cur - m_new)
                    m_st[q_rows, :] = m_new
                    l_st[q_rows, :] = a_old * l_st[q_rows, :] + a_cur * l_cur
                    acc_st[q_rows, :] = a_old * acc_st[q_rows, :] + a_cur * acc_cur
            return carry

        lax.fori_loop(0, seq // DIL_QB // DIL_GROUP, group, 0)

    def fin(i, carry):
        r0 = pl.multiple_of(i * conv_rows, conv_rows)
        z = z_ref[pl.ds(r0, conv_rows), :].astype(F32)
        y = acc_s[pl.ds(r0, conv_rows), :] * (1.0 / l_s[pl.ds(r0, conv_rows), :])
        o_ref[pl.ds(r0, conv_rows), :] = (y * _silu(z)).astype(BF16)
        return carry

    lax.fori_loop(0, seq // conv_rows, fin, 0)


def _dil(proj, bias, batch, seq):
    def col(c0):
        return pl.BlockSpec((seq, LANES), lambda b, p, c0=c0: (b, c0 + p))

    scratch = ([pltpu.VMEM((seq, LANES), F32) for _ in range(6)]
               + [pltpu.VMEM((seq, LANES), BF16) for _ in range(3)]
               + [pltpu.VMEM((seq, LANES), F32) for _ in range(3)])
    return pl.pallas_call(
        _dil_kernel,
        grid=(batch, N_HEAD_PAIRS),
        in_specs=[
            col(COL_QC), col(COL_KC), col(COL_VC), col(COL_ZC),
            pl.BlockSpec((2, len(DIL_PATTERNS), N_VARIANTS, DIL_QB, DIL_KW),
                         lambda b, p: (p, 0, 0, 0, 0)),
        ],
        out_specs=pl.BlockSpec((seq, LANES), lambda b, p: (b, p)),
        out_shape=jax.ShapeDtypeStruct((batch * seq, W_BRANCH), BF16),
        scratch_shapes=scratch,
        compiler_params=pltpu.CompilerParams(
            dimension_semantics=("arbitrary", "arbitrary"), vmem_limit_bytes=VMEM_LIMIT),
        name="dil_attn",
    )(proj, proj, proj, proj, bias)


N_POW = CHUNK + 1
N_LAGS = 2 * CHUNK - 1
PREP_GROUPS = 8


def _ssm_prep_kernel(lre_ref, lim_ref, ldt_ref, btre_ref, btim_ref, cre_ref, cim_ref,
                     toep_ref, win_ref, wout_ref, decay_ref):
    n = CHUNK * SSM_GROUP
    lane = lax.broadcasted_iota(jnp.int32, (SSM_GROUP, LANES), 1)
    lo = lane < SSM_STATE
    lane_n = lax.broadcasted_iota(jnp.int32, (SSM_GROUP, n), 1)
    nt = (((1,), (1,)), ((), ()))
    pair = lambda x, y: jnp.where(lo, x, y)

    for j in range(PREP_GROUPS):
        xs, ys, bbs, pw_last = {}, {}, {}, {}
        for dr in range(2):
            lam_re = lre_ref[0, dr, j]
            lam_im = lim_ref[0, dr, j]
            dt = jnp.exp(ldt_ref[0, dr, j])
            mag = jnp.exp(lam_re * dt)
            lb_re = mag * jnp.cos(lam_im * dt)
            lb_im = mag * jnp.sin(lam_im * dt)
            n_re = lb_re - 1.0
            den = lam_re * lam_re + lam_im * lam_im
            q_re = (n_re * lam_re + lb_im * lam_im) / den
            q_im = (lb_im * lam_re - n_re * lam_im) / den
            bt_re = btre_ref[0, dr, j]
            bt_im = btim_ref[0, dr, j]
            bb_re = q_re * bt_re - q_im * bt_im
            bb_im = q_re * bt_im + q_im * bt_re
            bbs[dr] = (bb_re, bb_im)
            c_re = cre_ref[0, dr, j]
            c_im = cim_ref[0, dr, j]
            pw_re = jnp.ones_like(lb_re)
            pw_im = jnp.zeros_like(lb_re)
            for k in range(N_POW):
                xs[dr, k] = (pw_re * bb_re - pw_im * bb_im, pw_re * bb_im + pw_im * bb_re)
                ys[dr, k] = (pw_re * c_re - pw_im * c_im, pw_re * c_im + pw_im * c_re)
                pw_last[dr] = (pw_re, pw_im)
                pw_re, pw_im = pw_re * lb_re - pw_im * lb_im, pw_re * lb_im + pw_im * lb_re

        step = [GROUPS_PER_SLAB * (a // GROUPS_PER_SLAB) + (a % GROUPS_PER_SLAB - j) % GROUPS_PER_SLAB
                for a in range(CHUNK)]
        rows = lambda a: slice(a * SSM_GROUP, (a + 1) * SSM_GROUP)

        decay_ref[0, 0, j:j + 1, :] = pair(pw_last[0][0], pw_last[1][0])[0:1]
        decay_ref[0, 1, j:j + 1, :] = pair(pw_last[0][1], pw_last[1][1])[0:1]

        for a, s in enumerate(step):
            win_ref[0, j, rows(a), :LANES] = pair(xs[0, CHUNK - 1 - s][0], xs[1, s][0]).astype(BF16)
            win_ref[0, j, rows(a), LANES:] = pair(xs[0, CHUNK - 1 - s][1], xs[1, s][1]).astype(BF16)

        z_re = jnp.concatenate([pair(ys[0, t + 1][0], ys[1, CHUNK - t][0]) for t in step], axis=0)
        z_nim = jnp.concatenate([pair(-ys[0, t + 1][1], -ys[1, CHUNK - t][1]) for t in step], axis=0)
        wout_ref[0, j, :LANES, :] = z_re.T.astype(BF16)
        wout_ref[0, j, LANES:, :] = z_nim.T.astype(BF16)

        kf = lax.dot_general(
            pair(bbs[0][0], -bbs[0][1]),
            jnp.concatenate([pair(*ys[0, m]) for m in range(CHUNK)], axis=0),
            nt, precision=lax.Precision.HIGHEST, preferred_element_type=F32)
        kb = lax.dot_general(
            pair(bbs[1][0], -bbs[1][1]),
            jnp.concatenate([pair(*ys[1, CHUNK - 1 - i]) for i in range(CHUNK)], axis=0),
            nt, precision=lax.Precision.HIGHEST, preferred_element_type=F32)
        for a, s in enumerate(step):
            fwd = pltpu.roll(kf, SSM_GROUP * s, axis=1) if s else kf
            shift_b = (n - SSM_GROUP * (CHUNK - 1 - s)) % n
            bwd = pltpu.roll(kb, shift_b, axis=1) if shift_b else kb
            t_nat = (jnp.where(lane_n >= SSM_GROUP * s, fwd, 0.0)
                     + jnp.where(lane_n < SSM_GROUP * (s + 1), bwd, 0.0))
            for h in range(2):
                half = t_nat[:, h * LANES:(h + 1) * LANES]
                if j:
                    half = pltpu.roll(half, SSM_GROUP * j, axis=1)
                toep_ref[0, j, rows(a), h * LANES:(h + 1) * LANES] = half.astype(BF16)


def _ssm_prep(lam_re, lam_im, log_dt, b_re, b_im, c_re, c_im):
    depth = lam_re.shape[0]
    G, P, C = SSM_GROUPS, SSM_STATE, SSM_GROUP
    n = CHUNK * C
    gb = PREP_GROUPS
    twice = lambda a: jnp.concatenate([a, a], axis=-1)
    rep = lambda a: jnp.broadcast_to(twice(a.astype(F32))[:, :, :, None, :], (depth, 2, G, C, LANES))
    ldt = jnp.broadcast_to(log_dt.astype(F32)[:, :, :, None, None], (depth, 2, G, C, LANES))
    bt = lambda a: twice(a.astype(F32).transpose(0, 1, 2, 4, 3))
    spec_in = pl.BlockSpec((1, 2, gb, C, LANES), lambda l, i: (l, 0, i, 0, 0))
    spec_w = pl.BlockSpec((1, gb, n, n), lambda l, i: (l, i, 0, 0))
    shp_w = jax.ShapeDtypeStruct((depth, G, n, n), BF16)
    return pl.pallas_call(
        _ssm_prep_kernel,
        grid=(depth, G // gb),
        in_specs=[spec_in] * 7,
        out_specs=[spec_w, spec_w, spec_w, pl.BlockSpec((1, 2, gb, LANES), lambda l, i: (l, 0, i, 0))],
        out_shape=[shp_w, shp_w, shp_w, jax.ShapeDtypeStruct((depth, 2, G, LANES), F32)],
        compiler_params=pltpu.CompilerParams(
            dimension_semantics=("arbitrary", "arbitrary"), vmem_limit_bytes=VMEM_LIMIT),
        name="ssm_prep",
    )(rep(lam_re), rep(lam_im), ldt, bt(b_re), bt(b_im), twice(c_re.astype(F32)), twice(c_im.astype(F32)))


GROUPS_PER_SLAB = LANES // SSM_GROUP
N_SLABS = W_BRANCH // LANES
RELAYOUT_ROWS = 32


def _ssm_kernel(xa_ref, toep_ref, win_ref, wout_ref, decay_ref, d_ref, o_ref, xf, u_s, st, hs, yq):
    seq = xa_ref.shape[0]
    nchunk = seq // CHUNK
    nrb = nchunk // RELAYOUT_ROWS
    gps = GROUPS_PER_SLAB
    lane_rb = lax.broadcasted_iota(jnp.int32, (RELAYOUT_ROWS, LANES), 1)
    seg_masks = [(lane_rb >= SSM_GROUP * sg) & (lane_rb < SSM_GROUP * (sg + 1)) for sg in range(gps)]
    conv_rows = 512

    for q in range(N_SLABS):
        c_lo = q * LANES

        def conv(i, carry, c_lo=c_lo):
            r0 = pl.multiple_of(i * conv_rows, conv_rows)
            xf[pl.ds(r0, conv_rows), :] = xa_ref[pl.ds(r0, conv_rows), c_lo:c_lo + LANES].astype(F32)
            return carry

        lax.fori_loop(0, seq // conv_rows, conv, 0)

        for hh in range(2):
            def fwd_relayout(rb, carry, hh=hh):
                c0 = pl.multiple_of(rb * RELAYOUT_ROWS, RELAYOUT_ROWS)
                rolled = []
                for r in range(gps):
                    xs = xf[pl.ds(CHUNK * c0 + gps * hh + r, RELAYOUT_ROWS, stride=CHUNK), :]
                    rolled.append(xs if r == 0 else pltpu.roll(xs, SSM_GROUP * r, axis=1))
                for j in range(gps):
                    out = rolled[(0 - j) % gps]
                    for sg in range(1, gps):
                        out = jnp.where(seg_masks[sg], rolled[(sg - j) % gps], out)
                    u_s[j, pl.ds(c0, RELAYOUT_ROWS), hh * LANES:(hh + 1) * LANES] = out.astype(BF16)
                return carry

            lax.fori_loop(0, nrb, fwd_relayout, 0, unroll=2)

        for j in range(gps):
            s_in = jnp.dot(u_s[j], win_ref[gps * q + j], preferred_element_type=F32)
            st[0, pl.ds(j, nchunk, stride=gps), :] = s_in[:, :LANES]
            st[1, pl.ds(j, nchunk, stride=gps), :] = s_in[:, LANES:]

        g_lo = gps * q
        a_re = decay_ref[0, g_lo:g_lo + gps, :]
        a_im = decay_ref[1, g_lo:g_lo + gps, :]
        lo, hi = slice(0, SSM_STATE), slice(SSM_STATE, LANES)

        def scan(i, carry):
            hrf, hif, hrb, hib = carry
            rf = pl.ds(pl.multiple_of(i * gps, gps), gps)
            rb_ = pl.ds(pl.multiple_of((nchunk - 1 - i) * gps, gps), gps)
            hs[0, rf, lo] = hrf[:, lo]
            hs[1, rf, lo] = hif[:, lo]
            hs[0, rb_, hi] = hrb[:, hi]
            hs[1, rb_, hi] = hib[:, hi]
            return (a_re * hrf - a_im * hif + st[0, rf, :], a_re * hif + a_im * hrf + st[1, rf, :],
                    a_re * hrb - a_im * hib + st[0, rb_, :], a_re * hib + a_im * hrb + st[1, rb_, :])

        zero = jnp.zeros((gps, LANES), F32)
        lax.fori_loop(0, nchunk, scan, (zero, zero, zero, zero), unroll=8)

        for j in range(gps):
            g = gps * q + j
            h_in = jnp.concatenate([hs[0, pl.ds(j, nchunk, stride=gps), :].astype(BF16),
                                    hs[1, pl.ds(j, nchunk, stride=gps), :].astype(BF16)], axis=-1)
            yq[j] = (jnp.dot(u_s[j], toep_ref[g], preferred_element_type=F32)
                     + jnp.dot(h_in, wout_ref[g], preferred_element_type=F32))

        d_row = d_ref[:, c_lo:c_lo + LANES]
        for hh in range(2):
            def bwd_relayout(rb, carry, hh=hh, d_row=d_row, q=q):
                c0 = pl.multiple_of(rb * RELAYOUT_ROWS, RELAYOUT_ROWS)
                ys = [yq[j, pl.ds(c0, RELAYOUT_ROWS), hh * LANES:(hh + 1) * LANES] for j in range(gps)]
                for r in range(gps):
                    merged = ys[(0 - r) % gps]
                    for sg in range(1, gps):
                        merged = jnp.where(seg_masks[sg], ys[(sg - r) % gps], merged)
                    if r:
                        merged = pltpu.roll(merged, LANES - SSM_GROUP * r, axis=1)
                    tok = pl.ds(CHUNK * c0 + gps * hh + r, RELAYOUT_ROWS, stride=CHUNK)
                    o_ref[q, tok, :] = merged + d_row * xf[tok, :]
                return carry

            lax.fori_loop(0, nrb, bwd_relayout, 0, unroll=2)


def _ssm(proj, toep, w_in, w_out, decay, d_skip, layer, batch, seq):
    nchunk = seq // CHUNK
    n = CHUNK * SSM_GROUP
    wspec = pl.BlockSpec((None, SSM_GROUPS, n, n), lambda b: (layer, 0, 0, 0), pipeline_mode=pl.Buffered(1))
    return pl.pallas_call(
        _ssm_kernel,
        grid=(batch,),
        in_specs=[
            pl.BlockSpec((seq, W_BRANCH), lambda b: (b, COL_XA // N_SLABS)),
            wspec, wspec, wspec,
            pl.BlockSpec((None, 2, SSM_GROUPS, LANES), lambda b: (layer, 0, 0, 0)),
            pl.BlockSpec((None, 1, W_BRANCH), lambda b: (layer, 0, 0)),
        ],
        out_specs=pl.BlockSpec((N_SLABS, seq, LANES), lambda b: (0, b, 0)),
        out_shape=jax.ShapeDtypeStruct((N_SLABS, batch * seq, LANES), F32),
        scratch_shapes=[
            pltpu.VMEM((seq, LANES), F32),
            pltpu.VMEM((GROUPS_PER_SLAB, nchunk, n), BF16),
            pltpu.VMEM((2, nchunk * GROUPS_PER_SLAB, LANES), F32),
            pltpu.VMEM((2, nchunk * GROUPS_PER_SLAB, LANES), F32),
            pltpu.VMEM((GROUPS_PER_SLAB, nchunk, n), F32),
        ],
        compiler_params=pltpu.CompilerParams(
            dimension_semantics=("arbitrary",), vmem_limit_bytes=VMEM_LIMIT),
        name="ssm_mix",
    )(proj, toep, w_in, w_out, decay, d_skip)


OUT_TM = 512


def _out_proj_kernel(ya_ref, za_ref, yb_ref, yc_ref, x_ref, w_ref, gw_ref, gb_ref, fg_ref, o_ref, *, final):
    y = jnp.concatenate([ya_ref[i] for i in range(N_SLABS)], axis=-1)
    g = _gelu_tanh(y)
    gate = jnp.dot(g.astype(BF16), gw_ref[...], preferred_element_type=F32) + gb_ref[...]
    ya = g * (1.0 / (1.0 + jnp.exp(-gate))) * _silu(za_ref[...].astype(F32))
    delta = (jnp.dot(ya.astype(BF16), w_ref[:W_BRANCH, :], preferred_element_type=F32)
             + jnp.dot(yb_ref[...], w_ref[W_BRANCH:2 * W_BRANCH, :], preferred_element_type=F32)
             + jnp.dot(yc_ref[...], w_ref[2 * W_BRANCH:, :], preferred_element_type=F32))
    x = x_ref[...] + delta
    if final:
        ms = jnp.mean(x * x, axis=-1, keepdims=True)
        x = x * lax.rsqrt(ms + RMS_EPS) * fg_ref[...]
    o_ref[...] = x


def _out_proj(ya_pre, proj, yb, yc, x2d, w_bf16, glu_w_bf16, glu_b, final_g, layer, final):
    rows = x2d.shape[0]
    row_blk = lambda width: pl.BlockSpec((OUT_TM, width), lambda i: (i, 0))
    const = lambda shape: pl.BlockSpec((None,) + shape, lambda i: (layer,) + (0,) * len(shape),
                                       pipeline_mode=pl.Buffered(1))
    return pl.pallas_call(
        functools.partial(_out_proj_kernel, final=final),
        grid=(rows // OUT_TM,),
        in_specs=[
            pl.BlockSpec((N_SLABS, OUT_TM, LANES), lambda i: (0, i, 0)),
            pl.BlockSpec((OUT_TM, W_BRANCH), lambda i: (i, COL_ZA // N_SLABS)),
            row_blk(W_BRANCH), row_blk(W_BRANCH), row_blk(D_MODEL),
            const((MIX_WIDTH, D_MODEL)), const((W_BRANCH, W_BRANCH)),
            const((1, W_BRANCH)),
            pl.BlockSpec((None, 1, D_MODEL), lambda i: (0, 0, 0), pipeline_mode=pl.Buffered(1)),
        ],
        out_specs=row_blk(D_MODEL),
        out_shape=jax.ShapeDtypeStruct((rows, D_MODEL), F32),
        compiler_params=pltpu.CompilerParams(
            dimension_semantics=("arbitrary",), vmem_limit_bytes=VMEM_LIMIT),
        name="out_proj_final" if final else "out_proj",
    )(ya_pre, proj, yb, yc, x2d, w_bf16, glu_w_bf16, glu_b, final_g)


def kernel(x, norm_g, w_in, w_out, ssm_lam_re, ssm_lam_im, ssm_log_dt, ssm_b_re, ssm_b_im, ssm_c_re,
           ssm_c_im, ssm_d, glu_w, glu_b, na_rpb, t5_bias, final_g):
    batch, seq, _ = x.shape
    depth = w_in.shape[0]
    x2d = x.astype(F32).reshape(batch * seq, D_MODEL)
    toep, s_in, s_out, decay = _ssm_prep(ssm_lam_re, ssm_lam_im, ssm_log_dt, ssm_b_re, ssm_b_im,
                                         ssm_c_re, ssm_c_im)
    dil_bias = _dil_bias_table(t5_bias)
    na_bias = _na_bias_table(na_rpb)
    w_in_bf, w_out_bf, glu_w_bf = w_in.astype(BF16), w_out.astype(BF16), glu_w.astype(BF16)
    row3 = lambda a: a.astype(F32).reshape(a.shape[0], 1, a.shape[1])
    norm_g3, ssm_d3, glu_b3 = row3(norm_g), row3(ssm_d), row3(glu_b)
    final_g3 = final_g.astype(F32).reshape(1, 1, D_MODEL)
    for l in range(depth):
        proj = _in_proj(x2d, norm_g3, w_in_bf, l)
        ya_pre = _ssm(proj, toep, s_in, s_out, decay, ssm_d3, l, batch, seq)
        yb = _na(proj, na_bias, l, batch, seq)
        yc = _dil(proj, dil_bias, batch, seq)
        x2d = _out_proj(ya_pre, proj, yb, yc, x2d, w_out_bf, glu_w_bf, glu_b3, final_g3, l,
                        final=(l == depth - 1))
    return x2d.reshape(batch, seq, D_MODEL).astype(x.dtype)
```

```python
import functools

import numpy as np
import jax
import jax.numpy as jnp
from jax import lax
from jax.experimental import pallas as pl
from jax.experimental.pallas import tpu as pltpu

F32 = jnp.float32
BF16 = jnp.bfloat16

D_MODEL = 1024
HEAD_DIM = 64
W_BRANCH = 512
N_HEADS = W_BRANCH // HEAD_DIM
N_HEAD_PAIRS = N_HEADS // 2
SSM_GROUP = 16
SSM_GROUPS = W_BRANCH // SSM_GROUP
SSM_STATE = 64
GRID_W = 64
NA_ROWS = 8
NA_COLS = 16
DIL_PATTERNS = ((128, 1), (512, 4), (2048, 16))
DIL_HALF = 64
T5_BUCKETS = 32
T5_MAX_DIST = 1024
RMS_EPS = 1e-6
NEG_INF = -1e30
IN_COLS = 10 * W_BRANCH
MIX_WIDTH = 3 * W_BRANCH

COL_XA, COL_ZA, COL_QB, COL_KB, COL_VB, COL_ZB, COL_QC, COL_KC, COL_VC, COL_ZC = (
    4 * i for i in range(10))

LANES = 128
CHUNK = 16
VMEM_LIMIT = 56 * 1024 * 1024


def _silu(z):
    return z * (1.0 / (1.0 + jnp.exp(-z)))


def _toeplitz_rows(vec, n_rows, n_cols):
    m = vec.shape[-1]
    assert m >= n_rows + n_cols - 1
    tiled = jnp.tile(vec, (1,) * (vec.ndim - 1) + (n_rows,))[..., :n_rows * (m - 1)]
    return tiled.reshape(vec.shape[:-1] + (n_rows, m - 1))[..., :n_cols]


def _gelu_tanh(x):
    return 0.5 * x * (1.0 + jnp.tanh(np.sqrt(2.0 / np.pi).astype(np.float32) * (x + 0.044715 * (x * x * x))))


IN_TM = 512
IN_TN = 512


def _in_proj_kernel(x_ref, g_ref, w_ref, o_ref):
    x = x_ref[...]
    ms = jnp.mean(x * x, axis=-1, keepdims=True)
    h = (x * lax.rsqrt(ms + RMS_EPS) * g_ref[...]).astype(BF16)
    for n in range(IN_COLS // IN_TN):
        o_ref[:, n * IN_TN:(n + 1) * IN_TN] = jnp.dot(
            h, w_ref[:, n * IN_TN:(n + 1) * IN_TN], preferred_element_type=F32).astype(BF16)


def _in_proj(x2d, g, w_bf16, layer):
    rows = x2d.shape[0]
    return pl.pallas_call(
        _in_proj_kernel,
        grid=(rows // IN_TM,),
        in_specs=[
            pl.BlockSpec((IN_TM, D_MODEL), lambda i: (i, 0)),
            pl.BlockSpec((None, 1, D_MODEL), lambda i: (layer, 0, 0)),
            pl.BlockSpec((None, D_MODEL, IN_COLS), lambda i: (layer, 0, 0), pipeline_mode=pl.Buffered(1)),
        ],
        out_specs=pl.BlockSpec((IN_TM, IN_COLS), lambda i: (i, 0)),
        out_shape=jax.ShapeDtypeStruct((rows, IN_COLS), BF16),
        compiler_params=pltpu.CompilerParams(
            dimension_semantics=("arbitrary",), vmem_limit_bytes=VMEM_LIMIT),
        name="in_proj",
    )(x2d, g, w_bf16)


def _na_bias_kernel(rpb_ref, o_ref):
    lane = lax.broadcasted_iota(jnp.int32, (GRID_W, LANES), 1)
    j = lax.broadcasted_iota(jnp.int32, (GRID_W, LANES), 0)
    c = lane & (GRID_W - 1)
    col_start = jnp.clip(j - NA_COLS // 2, 0, GRID_W - NA_COLS)
    valid = (c >= col_start) & (c < col_start + NA_COLS)
    first = lane < GRID_W
    even, odd = [], []
    for a in range(2 * NA_ROWS - 1):
        vec = pltpu.roll(rpb_ref[a:a + 1, :], LANES - (NA_COLS - 1), axis=1)
        x = jnp.broadcast_to(vec, (GRID_W, LANES))
        even.append(pltpu.roll(x, 0, axis=1, stride=1, stride_axis=0))
        odd.append(pltpu.roll(x, GRID_W, axis=1, stride=1, stride_axis=0))
    for v in range(NA_ROWS):
        for i in range(NA_ROWS // 2):
            a0 = 2 * i - v + (NA_ROWS - 1)
            tile = jnp.where(first, even[a0], odd[a0 + 1])
            o_ref[v, :, i * LANES:(i + 1) * LANES] = jnp.where(valid, tile, NEG_INF)


def _na_bias_table(rpb):
    depth = rpb.shape[0]
    padded = jnp.pad(rpb.astype(F32), ((0, 0), (0, 0), (0, 1), (0, LANES - (2 * NA_COLS - 1))))
    nkeys = NA_ROWS * GRID_W
    return pl.pallas_call(
        _na_bias_kernel,
        grid=(depth, N_HEADS),
        in_specs=[pl.BlockSpec((None, None, 2 * NA_ROWS, LANES), lambda l, h: (l, h, 0, 0))],
        out_specs=pl.BlockSpec((None, None, NA_ROWS, GRID_W, nkeys), lambda l, h: (l, h, 0, 0, 0)),
        out_shape=jax.ShapeDtypeStruct((depth, N_HEADS, NA_ROWS, GRID_W, nkeys), F32),
        compiler_params=pltpu.CompilerParams(dimension_semantics=("arbitrary", "arbitrary")),
        name="na_bias",
    )(padded)


def _na_kernel(q_ref, k_ref, v_ref, z_ref, bias_ref, o_ref, s_scr, mx_scr, p_scr):
    seq = q_ref.shape[0]
    rows = seq // GRID_W
    nkeys = NA_ROWS * GRID_W
    lane = lax.broadcasted_iota(jnp.int32, (GRID_W, LANES), 1)
    first_head = lane < HEAD_DIM
    nt = (((1,), (1,)), ((), ()))

    def window(r):
        rs = jnp.clip(r - NA_ROWS // 2, 0, rows - NA_ROWS)
        return r - rs, pl.multiple_of(r * GRID_W, GRID_W), pl.multiple_of(rs * GRID_W, GRID_W)

    def group(gi, carry):
        r_lo = gi * NA_GROUP

        def logits(i, c):
            var, q0, k0 = window(r_lo + i)
            qb = q_ref[pl.ds(q0, GRID_W), :] * jnp.asarray(HEAD_DIM ** -0.5, BF16)
            kw = k_ref[pl.ds(k0, nkeys), :]
            for hd in range(2):
                keep = first_head if hd == 0 else jnp.logical_not(first_head)
                qm = jnp.where(keep, qb, jnp.zeros_like(qb))
                s = lax.dot_general(qm, kw, nt, preferred_element_type=F32) + bias_ref[hd, var]
                s_scr[2 * i + hd] = s
                mx_scr[2 * i + hd] = jnp.broadcast_to(jnp.max(s, axis=-1, keepdims=True), (GRID_W, LANES))
            return c

        lax.fori_loop(0, NA_GROUP, logits, 0, unroll=True)

        def probs(t, c):
            m = mx_scr[t]
            psum = None
            for kc in range(nkeys // LANES):
                sl = slice(kc * LANES, (kc + 1) * LANES)
                p = jnp.exp(s_scr[t, :, sl] - m)
                p_scr[t, :, sl] = p.astype(BF16)
                psum = p if psum is None else psum + p
            mx_scr[t] = jnp.broadcast_to(jnp.sum(psum, axis=-1, keepdims=True), (GRID_W, LANES))
            return c

        lax.fori_loop(0, 2 * NA_GROUP, probs, 0, unroll=True)

        def outputs(i, c):
            _, q0, k0 = window(r_lo + i)
            vw = v_ref[pl.ds(k0, nkeys), :]
            r0 = jnp.dot(p_scr[2 * i], vw, preferred_element_type=F32)
            r1 = jnp.dot(p_scr[2 * i + 1], vw, preferred_element_type=F32)
            acc = jnp.where(first_head, r0, r1)
            den = jnp.where(first_head, mx_scr[2 * i], mx_scr[2 * i + 1])
            z = z_ref[pl.ds(q0, GRID_W), :].astype(F32)
            o_ref[pl.ds(q0, GRID_W), :] = (acc * (1.0 / den) * _silu(z)).astype(BF16)
            return c

        lax.fori_loop(0, NA_GROUP, outputs, 0, unroll=True)
        return carry

    lax.fori_loop(0, rows // NA_GROUP, group, 0)


NA_GROUP = 8


def _na(proj, bias, layer, batch, seq):
    def col(c0):
        return pl.BlockSpec((seq, LANES), lambda b, p, c0=c0: (b, c0 + p))

    nkeys = NA_ROWS * GRID_W
    scratch = [pltpu.VMEM((2 * NA_GROUP, GRID_W, nkeys), F32),
               pltpu.VMEM((2 * NA_GROUP, GRID_W, LANES), F32),
               pltpu.VMEM((2 * NA_GROUP, GRID_W, nkeys), BF16)]

    return pl.pallas_call(
        _na_kernel,
        grid=(batch, N_HEAD_PAIRS),
        in_specs=[
            col(COL_QB), col(COL_KB), col(COL_VB), col(COL_ZB),
            pl.BlockSpec((None, 2, NA_ROWS, GRID_W, NA_ROWS * GRID_W), lambda b, p: (layer, p, 0, 0, 0)),
        ],
        out_specs=pl.BlockSpec((seq, LANES), lambda b, p: (b, p)),
        out_shape=jax.ShapeDtypeStruct((batch * seq, W_BRANCH), BF16),
        scratch_shapes=scratch,
        compiler_params=pltpu.CompilerParams(
            dimension_semantics=("arbitrary", "arbitrary"), vmem_limit_bytes=VMEM_LIMIT),
        name="na_attn",
    )(proj, proj, proj, proj, bias)


DIL_QB = 128
DIL_KW = 256
N_VARIANTS = 3


def _t5_bucket(rel):
    nb = T5_BUCKETS // 2
    max_exact = nb // 2
    n = np.abs(rel)
    large = max_exact + (np.log(np.maximum(n, 1) / max_exact) / np.log(T5_MAX_DIST / max_exact)
                         * (nb - max_exact)).astype(np.int32)
    large = np.minimum(large, nb - 1)
    return (np.where(rel > 0, nb, 0) + np.where(n < max_exact, n, large)).astype(np.int32)


def _dil_bias_table(t5_bias):
    n_pat = len(DIL_PATTERNS)
    lane = np.arange(DIL_VEC)
    n = np.where(lane < DIL_KW, lane, lane - DIL_VEC)
    in_range = (lane < DIL_KW) | (lane > DIL_VEC - DIL_QB)
    pick = np.zeros((n_pat, N_VARIANTS, T5_BUCKETS, DIL_VEC), np.float32)
    neg = np.zeros((n_pat, N_VARIANTS, 1, DIL_VEC), np.float32)
    for pi, (_, d) in enumerate(DIL_PATTERNS):
        for var in range(N_VARIANTS):
            step = n - DIL_HALF * var
            ok = in_range & (np.abs(step) <= DIL_HALF)
            bucket = _t5_bucket(d * np.clip(step, -DIL_HALF, DIL_HALF))
            pick[pi, var, bucket[ok], lane[ok]] = 1.0
            neg[pi, var, 0, ~ok] = NEG_INF
    vecs = jnp.einsum('bh,pvbn->hpvn', t5_bias.astype(F32), pick,
                      precision=lax.Precision.HIGHEST) + neg[:, :, 0][None]
    return pl.pallas_call(
        _dil_bias_kernel,
        grid=(N_HEADS,),
        in_specs=[pl.BlockSpec((None, n_pat * N_VARIANTS, DIL_VEC), lambda h: (h, 0, 0))],
        out_specs=pl.BlockSpec((None, n_pat, N_VARIANTS, DIL_QB, DIL_KW), lambda h: (h, 0, 0, 0, 0)),
        out_shape=jax.ShapeDtypeStruct((N_HEADS, n_pat, N_VARIANTS, DIL_QB, DIL_KW), F32),
        compiler_params=pltpu.CompilerParams(dimension_semantics=("arbitrary",)),
        name="dil_bias",
    )(vecs.reshape(N_HEADS, n_pat * N_VARIANTS, DIL_VEC))


DIL_VEC = 512


def _dil_bias_kernel(vec_ref, o_ref):
    for pi in range(len(DIL_PATTERNS)):
        for var in range(N_VARIANTS):
            i = pi * N_VARIANTS + var
            x = jnp.broadcast_to(vec_ref[i:i + 1, :], (DIL_QB, DIL_VEC))
            o_ref[pi, var] = pltpu.roll(x, 0, axis=1, stride=1, stride_axis=0)[:, :DIL_KW]


DIL_GROUP = 8


def _dil_kernel(q_ref, k_ref, v_ref, z_ref, bias_ref, o_ref, qf, kf, vf, qg, kg, vg, qd, kd, vd,
                m_s, l_s, acc_s):
    seq = q_ref.shape[0]
    conv_rows = 256
    lane = lax.broadcasted_iota(jnp.int32, (DIL_QB, LANES), 1)
    first_head = lane < HEAD_DIM
    nt = (((1,), (1,)), ((), ()))

    def conv(i, carry):
        r0 = pl.multiple_of(i * conv_rows, conv_rows)
        qf[pl.ds(r0, conv_rows), :] = q_ref[pl.ds(r0, conv_rows), :].astype(F32) * (HEAD_DIM ** -0.5)
        kf[pl.ds(r0, conv_rows), :] = k_ref[pl.ds(r0, conv_rows), :].astype(F32)
        vf[pl.ds(r0, conv_rows), :] = v_ref[pl.ds(r0, conv_rows), :].astype(F32)
        return carry

    lax.fori_loop(0, seq // conv_rows, conv, 0)

    n4 = seq // 4
    n16 = seq // 16

    def to_order4(i, carry):
        dst = pl.multiple_of(i * conv_rows, conv_rows)
        src = pl.ds(dst // n4 + 4 * (dst % n4), conv_rows, stride=4)
        for a, b in ((qf, qg), (kf, kg), (vf, vg)):
            b[pl.ds(dst, conv_rows), :] = a[src, :]
        return carry

    lax.fori_loop(0, seq // conv_rows, to_order4, 0, unroll=2)

    def order16_rows_in_order4(rho):
        return pl.ds((rho % 4) * n4 + rho // 4, n16, stride=4)

    def to_order16(rho, carry):
        src = order16_rows_in_order4(rho)
        dst = pl.ds(pl.multiple_of(rho * n16, n16), n16)
        for a, b in ((qg, qd), (kg, kd), (vg, vd)):
            b[dst, :] = a[src, :].astype(BF16)
        return carry

    lax.fori_loop(0, 16, to_order16, 0, unroll=2)

    plan = ((2, 16, (qd, kd, vd), (m_s, l_s, acc_s)),
            (1, 4, (qd, kd, vd), (qf, kf, vf)),
            (0, 1, (qd, k_ref, v_ref), (m_s, l_s, acc_s)))

    for step, (pi, d, (q_src, k_src, v_src), (m_st, l_st, acc_st)) in enumerate(plan):
        n_sub = seq // d
        nblk = n_sub // DIL_QB

        if step == 1:
            def state_to_order4(rho, carry):
                dst = order16_rows_in_order4(rho)
                src = pl.ds(pl.multiple_of(rho * n16, n16), n16)
                for a, b in ((m_s, qf), (l_s, kf), (acc_s, vf)):
                    b[dst, :] = a[src, :]
                return carry

            lax.fori_loop(0, 16, state_to_order4, 0, unroll=2)

            def cast_order4(i, carry):
                r0 = pl.multiple_of(i * conv_rows, conv_rows)
                for a, b in ((qg, qd), (kg, kd), (vg, vd)):
                    b[pl.ds(r0, conv_rows), :] = a[pl.ds(r0, conv_rows), :].astype(BF16)
                return carry

            lax.fori_loop(0, seq // conv_rows, cast_order4, 0)
        if step == 2:
            def state_to_tokens(i, carry):
                src = pl.multiple_of(i * conv_rows, conv_rows)
                dst = pl.ds(src // n4 + 4 * (src % n4), conv_rows, stride=4)
                for a, b in ((qf, m_s), (kf, l_s), (vf, acc_s)):
                    b[dst, :] = a[pl.ds(src, conv_rows), :]
                qd[pl.ds(src, conv_rows), :] = q_ref[pl.ds(src, conv_rows), :] * jnp.asarray(
                    HEAD_DIM ** -0.5, BF16)
                return carry

            lax.fori_loop(0, seq // conv_rows, state_to_tokens, 0, unroll=2)

        def group(gi, carry, pi=pi, n_sub=n_sub, nblk=nblk, first=(step == 0), q_src=q_src,
                  k_src=k_src, v_src=v_src, m_st=m_st, l_st=l_st, acc_st=acc_st):
            units = []
            for i in range(DIL_GROUP):
                u = gi * DIL_GROUP + i
                res = u // nblk
                i0 = (u % nblk) * DIL_QB
                ks = jnp.clip(i0 - DIL_HALF, 0, n_sub - DIL_KW)
                units.append((res, i0, ks, (i0 - ks) // DIL_HALF))
            s_tiles, m_tiles = [], []
            for res, i0, ks, var in units:
                qb = q_src[pl.ds(pl.multiple_of(res * n_sub + i0, DIL_HALF), DIL_QB), :]
                kw = k_src[pl.ds(pl.multiple_of(res * n_sub + ks, DIL_HALF), DIL_KW), :]
                for hd in range(2):
                    keep = first_head if hd == 0 else jnp.logical_not(first_head)
                    qm = jnp.where(keep, qb, jnp.zeros_like(qb))
                    s = lax.dot_general(qm, kw, nt, preferred_element_type=F32) + bias_ref[hd, pi, var]
                    s_tiles.append(s)
                    m_tiles.append(jnp.max(s, axis=-1, keepdims=True))
            p_tiles, l_tiles = [], []
            for s, m in zip(s_tiles, m_tiles):
                p = jnp.exp(s - m)
                l_tiles.append(jnp.sum(p, axis=-1, keepdims=True))
                p_tiles.append(p.astype(BF16))
            for i, (res, i0, ks, var) in enumerate(units):
                vw = v_src[pl.ds(pl.multiple_of(res * n_sub + ks, DIL_HALF), DIL_KW), :]
                acc0 = jnp.dot(p_tiles[2 * i], vw, preferred_element_type=F32)
                acc1 = jnp.dot(p_tiles[2 * i + 1], vw, preferred_element_type=F32)
                m_cur = jnp.where(first_head, m_tiles[2 * i], m_tiles[2 * i + 1])
                l_cur = jnp.where(first_head, l_tiles[2 * i], l_tiles[2 * i + 1])
                acc_cur = jnp.where(first_head, acc0, acc1)
                q_rows = pl.ds(pl.multiple_of(res * n_sub + i0, DIL_HALF), DIL_QB)
                if first:
                    m_st[q_rows, :] = m_cur
                    l_st[q_rows, :] = l_cur
                    acc_st[q_rows, :] = acc_cur
                else:
                    m_old = m_st[q_rows, :]
                    m_new = jnp.maximum(m_old, m_cur)
                    a_old = jnp.exp(m_old - m_new)
                    a_cur = jnp.exp(m_cur - m_new)
                    m_st[q_rows, :] = m_new
                    l_st[q_rows, :] = a_old * l_st[q_rows, :] + a_cur * l_cur
                    acc_st[q_rows, :] = a_old * acc_st[q_rows, :] + a_cur * acc_cur
            return carry

        lax.fori_loop(0, seq // DIL_QB // DIL_GROUP, group, 0)

    def fin(i, carry):
        r0 = pl.multiple_of(i * conv_rows, conv_rows)
        z = z_ref[pl.ds(r0, conv_rows), :].astype(F32)
        y = acc_s[pl.ds(r0, conv_rows), :] * (1.0 / l_s[pl.ds(r0, conv_rows), :])
        o_ref[pl.ds(r0, conv_rows), :] = (y * _silu(z)).astype(BF16)
        return carry

    lax.fori_loop(0, seq // conv_rows, fin, 0, unroll=2)


def _dil(proj, bias, batch, seq):
    def col(c0):
        return pl.BlockSpec((seq, LANES), lambda b, p, c0=c0: (b, c0 + p))

    scratch = ([pltpu.VMEM((seq, LANES), F32) for _ in range(6)]
               + [pltpu.VMEM((seq, LANES), BF16) for _ in range(3)]
               + [pltpu.VMEM((seq, LANES), F32) for _ in range(3)])
    return pl.pallas_call(
        _dil_kernel,
        grid=(batch, N_HEAD_PAIRS),
        in_specs=[
            col(COL_QC), col(COL_KC), col(COL_VC), col(COL_ZC),
            pl.BlockSpec((2, len(DIL_PATTERNS), N_VARIANTS, DIL_QB, DIL_KW),
                         lambda b, p: (p, 0, 0, 0, 0)),
        ],
        out_specs=pl.BlockSpec((seq, LANES), lambda b, p: (b, p)),
        out_shape=jax.ShapeDtypeStruct((batch * seq, W_BRANCH), BF16),
        scratch_shapes=scratch,
        compiler_params=pltpu.CompilerParams(
            dimension_semantics=("arbitrary", "arbitrary"), vmem_limit_bytes=VMEM_LIMIT),
        name="dil_attn",
    )(proj, proj, proj, proj, bias)


N_POW = CHUNK + 1
N_LAGS = 2 * CHUNK - 1
PREP_GROUPS = 8


def _ssm_prep_kernel(lre_ref, lim_ref, ldt_ref, btre_ref, btim_ref, cre_ref, cim_ref,
                     toep_ref, win_ref, wout_ref, decay_ref):
    n = CHUNK * SSM_GROUP
    lane = lax.broadcasted_iota(jnp.int32, (SSM_GROUP, LANES), 1)
    lo = lane < SSM_STATE
    lane_n = lax.broadcasted_iota(jnp.int32, (SSM_GROUP, n), 1)
    nt = (((1,), (1,)), ((), ()))
    pair = lambda x, y: jnp.where(lo, x, y)

    for j in range(PREP_GROUPS):
        xs, ys, bbs, pw_last = {}, {}, {}, {}
        for dr in range(2):
            lam_re = lre_ref[0, dr, j]
            lam_im = lim_ref[0, dr, j]
            dt = jnp.exp(ldt_ref[0, dr, j])
            mag = jnp.exp(lam_re * dt)
            lb_re = mag * jnp.cos(lam_im * dt)
            lb_im = mag * jnp.sin(lam_im * dt)
            n_re = lb_re - 1.0
            den = lam_re * lam_re + lam_im * lam_im
            q_re = (n_re * lam_re + lb_im * lam_im) / den
            q_im = (lb_im * lam_re - n_re * lam_im) / den
            bt_re = btre_ref[0, dr, j]
            bt_im = btim_ref[0, dr, j]
            bb_re = q_re * bt_re - q_im * bt_im
            bb_im = q_re * bt_im + q_im * bt_re
            bbs[dr] = (bb_re, bb_im)
            c_re = cre_ref[0, dr, j]
            c_im = cim_ref[0, dr, j]
            pw_re = jnp.ones_like(lb_re)
            pw_im = jnp.zeros_like(lb_re)
            for k in range(N_POW):
                xs[dr, k] = (pw_re * bb_re - pw_im * bb_im, pw_re * bb_im + pw_im * bb_re)
                ys[dr, k] = (pw_re * c_re - pw_im * c_im, pw_re * c_im + pw_im * c_re)
                pw_last[dr] = (pw_re, pw_im)
                pw_re, pw_im = pw_re * lb_re - pw_im * lb_im, pw_re * lb_im + pw_im * lb_re

        step = [GROUPS_PER_SLAB * (a // GROUPS_PER_SLAB) + (a % GROUPS_PER_SLAB - j) % GROUPS_PER_SLAB
                for a in range(CHUNK)]
        rows = lambda a: slice(a * SSM_GROUP, (a + 1) * SSM_GROUP)

        decay_ref[0, 0, j:j + 1, :] = pair(pw_last[0][0], pw_last[1][0])[0:1]
        decay_ref[0, 1, j:j + 1, :] = pair(pw_last[0][1], pw_last[1][1])[0:1]

        for a, s in enumerate(step):
            win_ref[0, j, rows(a), :LANES] = pair(xs[0, CHUNK - 1 - s][0], xs[1, s][0]).astype(BF16)
            win_ref[0, j, rows(a), LANES:] = pair(xs[0, CHUNK - 1 - s][1], xs[1, s][1]).astype(BF16)

        z_re = jnp.concatenate([pair(ys[0, t + 1][0], ys[1, CHUNK - t][0]) for t in step], axis=0)
        z_nim = jnp.concatenate([pair(-ys[0, t + 1][1], -ys[1, CHUNK - t][1]) for t in step], axis=0)
        wout_ref[0, j, :LANES, :] = z_re.T.astype(BF16)
        wout_ref[0, j, LANES:, :] = z_nim.T.astype(BF16)

        kf = lax.dot_general(
            pair(bbs[0][0], -bbs[0][1]),
            jnp.concatenate([pair(*ys[0, m]) for m in range(CHUNK)], axis=0),
            nt, precision=lax.Precision.HIGHEST, preferred_element_type=F32)
        kb = lax.dot_general(
            pair(bbs[1][0], -bbs[1][1]),
            jnp.concatenate([pair(*ys[1, CHUNK - 1 - i]) for i in range(CHUNK)], axis=0),
            nt, precision=lax.Precision.HIGHEST, preferred_element_type=F32)
        for a, s in enumerate(step):
            fwd = pltpu.roll(kf, SSM_GROUP * s, axis=1) if s else kf
            shift_b = (n - SSM_GROUP * (CHUNK - 1 - s)) % n
            bwd = pltpu.roll(kb, shift_b, axis=1) if shift_b else kb
            t_nat = (jnp.where(lane_n >= SSM_GROUP * s, fwd, 0.0)
                     + jnp.where(lane_n < SSM_GROUP * (s + 1), bwd, 0.0))
            for h in range(2):
                half = t_nat[:, h * LANES:(h + 1) * LANES]
                if j:
                    half = pltpu.roll(half, SSM_GROUP * j, axis=1)
                toep_ref[0, j, rows(a), h * LANES:(h + 1) * LANES] = half.astype(BF16)


def _ssm_prep(lam_re, lam_im, log_dt, b_re, b_im, c_re, c_im):
    depth = lam_re.shape[0]
    G, P, C = SSM_GROUPS, SSM_STATE, SSM_GROUP
    n = CHUNK * C
    gb = PREP_GROUPS
    twice = lambda a: jnp.concatenate([a, a], axis=-1)
    rep = lambda a: jnp.broadcast_to(twice(a.astype(F32))[:, :, :, None, :], (depth, 2, G, C, LANES))
    ldt = jnp.broadcast_to(log_dt.astype(F32)[:, :, :, None, None], (depth, 2, G, C, LANES))
    bt = lambda a: twice(a.astype(F32).transpose(0, 1, 2, 4, 3))
    spec_in = pl.BlockSpec((1, 2, gb, C, LANES), lambda l, i: (l, 0, i, 0, 0))
    spec_w = pl.BlockSpec((1, gb, n, n), lambda l, i: (l, i, 0, 0))
    shp_w = jax.ShapeDtypeStruct((depth, G, n, n), BF16)
    return pl.pallas_call(
        _ssm_prep_kernel,
        grid=(depth, G // gb),
        in_specs=[spec_in] * 7,
        out_specs=[spec_w, spec_w, spec_w, pl.BlockSpec((1, 2, gb, LANES), lambda l, i: (l, 0, i, 0))],
        out_shape=[shp_w, shp_w, shp_w, jax.ShapeDtypeStruct((depth, 2, G, LANES), F32)],
        compiler_params=pltpu.CompilerParams(
            dimension_semantics=("arbitrary", "arbitrary"), vmem_limit_bytes=VMEM_LIMIT),
        name="ssm_prep",
    )(rep(lam_re), rep(lam_im), ldt, bt(b_re), bt(b_im), twice(c_re.astype(F32)), twice(c_im.astype(F32)))


GROUPS_PER_SLAB = LANES // SSM_GROUP
N_SLABS = W_BRANCH // LANES
RELAYOUT_ROWS = 32


def _ssm_kernel(xa_ref, toep_ref, win_ref, wout_ref, decay_ref, d_ref, o_ref, xf, u_s, st, hs, yq):
    seq = xa_ref.shape[0]
    nchunk = seq // CHUNK
    nrb = nchunk // RELAYOUT_ROWS
    gps = GROUPS_PER_SLAB
    lane_rb = lax.broadcasted_iota(jnp.int32, (RELAYOUT_ROWS, LANES), 1)
    seg_masks = [(lane_rb >= SSM_GROUP * sg) & (lane_rb < SSM_GROUP * (sg + 1)) for sg in range(gps)]
    conv_rows = 512

    for q in range(N_SLABS):
        c_lo = q * LANES

        def conv(i, carry, c_lo=c_lo):
            r0 = pl.multiple_of(i * conv_rows, conv_rows)
            xf[pl.ds(r0, conv_rows), :] = xa_ref[pl.ds(r0, conv_rows), c_lo:c_lo + LANES].astype(F32)
            return carry

        lax.fori_loop(0, seq // conv_rows, conv, 0)

        for hh in range(2):
            def fwd_relayout(rb, carry, hh=hh):
                c0 = pl.multiple_of(rb * RELAYOUT_ROWS, RELAYOUT_ROWS)
                rolled = []
                for r in range(gps):
                    xs = xf[pl.ds(CHUNK * c0 + gps * hh + r, RELAYOUT_ROWS, stride=CHUNK), :]
                    rolled.append(xs if r == 0 else pltpu.roll(xs, SSM_GROUP * r, axis=1))
                for j in range(gps):
                    out = rolled[(0 - j) % gps]
                    for sg in range(1, gps):
                        out = jnp.where(seg_masks[sg], rolled[(sg - j) % gps], out)
                    u_s[j, pl.ds(c0, RELAYOUT_ROWS), hh * LANES:(hh + 1) * LANES] = out.astype(BF16)
                return carry

            lax.fori_loop(0, nrb, fwd_relayout, 0, unroll=2)

        for j in range(gps):
            s_in = jnp.dot(u_s[j], win_ref[gps * q + j], preferred_element_type=F32)
            st[0, pl.ds(j, nchunk, stride=gps), :] = s_in[:, :LANES]
            st[1, pl.ds(j, nchunk, stride=gps), :] = s_in[:, LANES:]

        g_lo = gps * q
        a_re = decay_ref[0, g_lo:g_lo + gps, :]
        a_im = decay_ref[1, g_lo:g_lo + gps, :]
        lo, hi = slice(0, SSM_STATE), slice(SSM_STATE, LANES)

        def scan(i, carry):
            hrf, hif, hrb, hib = carry
            rf = pl.ds(pl.multiple_of(i * gps, gps), gps)
            rb_ = pl.ds(pl.multiple_of((nchunk - 1 - i) * gps, gps), gps)
            hs[0, rf, lo] = hrf[:, lo]
            hs[1, rf, lo] = hif[:, lo]
            hs[0, rb_, hi] = hrb[:, hi]
            hs[1, rb_, hi] = hib[:, hi]
            return (a_re * hrf - a_im * hif + st[0, rf, :], a_re * hif + a_im * hrf + st[1, rf, :],
                    a_re * hrb - a_im * hib + st[0, rb_, :], a_re * hib + a_im * hrb + st[1, rb_, :])

        zero = jnp.zeros((gps, LANES), F32)
        lax.fori_loop(0, nchunk, scan, (zero, zero, zero, zero), unroll=8)

        for j in range(gps):
            g = gps * q + j
            h_in = jnp.concatenate([hs[0, pl.ds(j, nchunk, stride=gps), :].astype(BF16),
                                    hs[1, pl.ds(j, nchunk, stride=gps), :].astype(BF16)], axis=-1)
            yq[j] = (jnp.dot(u_s[j], toep_ref[g], preferred_element_type=F32)
                     + jnp.dot(h_in, wout_ref[g], preferred_element_type=F32))

        d_row = d_ref[:, c_lo:c_lo + LANES]
        for hh in range(2):
            def bwd_relayout(rb, carry, hh=hh, d_row=d_row, q=q):
                c0 = pl.multiple_of(rb * RELAYOUT_ROWS, RELAYOUT_ROWS)
                ys = [yq[j, pl.ds(c0, RELAYOUT_ROWS), hh * LANES:(hh + 1) * LANES] for j in range(gps)]
                for r in range(gps):
                    merged = ys[(0 - r) % gps]
                    for sg in range(1, gps):
                        merged = jnp.where(seg_masks[sg], ys[(sg - r) % gps], merged)
                    if r:
                        merged = pltpu.roll(merged, LANES - SSM_GROUP * r, axis=1)
                    tok = pl.ds(CHUNK * c0 + gps * hh + r, RELAYOUT_ROWS, stride=CHUNK)
                    o_ref[q, tok, :] = merged + d_row * xf[tok, :]
                return carry

            lax.fori_loop(0, nrb, bwd_relayout, 0, unroll=2)


def _ssm(proj, toep, w_in, w_out, decay, d_skip, layer, batch, seq):
    nchunk = seq // CHUNK
    n = CHUNK * SSM_GROUP
    wspec = pl.BlockSpec((None, SSM_GROUPS, n, n), lambda b: (layer, 0, 0, 0), pipeline_mode=pl.Buffered(1))
    return pl.pallas_call(
        _ssm_kernel,
        grid=(batch,),
        in_specs=[
            pl.BlockSpec((seq, W_BRANCH), lambda b: (b, COL_XA // N_SLABS)),
            wspec, wspec, wspec,
            pl.BlockSpec((None, 2, SSM_GROUPS, LANES), lambda b: (layer, 0, 0, 0)),
            pl.BlockSpec((None, 1, W_BRANCH), lambda b: (layer, 0, 0)),
        ],
        out_specs=pl.BlockSpec((N_SLABS, seq, LANES), lambda b: (0, b, 0)),
        out_shape=jax.ShapeDtypeStruct((N_SLABS, batch * seq, LANES), F32),
        scratch_shapes=[
            pltpu.VMEM((seq, LANES), F32),
            pltpu.VMEM((GROUPS_PER_SLAB, nchunk, n), BF16),
            pltpu.VMEM((2, nchunk * GROUPS_PER_SLAB, LANES), F32),
            pltpu.VMEM((2, nchunk * GROUPS_PER_SLAB, LANES), F32),
            pltpu.VMEM((GROUPS_PER_SLAB, nchunk, n), F32),
        ],
        compiler_params=pltpu.CompilerParams(
            dimension_semantics=("arbitrary",), vmem_limit_bytes=VMEM_LIMIT),
        name="ssm_mix",
    )(proj, toep, w_in, w_out, decay, d_skip)


OUT_TM = 512


def _out_proj_kernel(ya_ref, za_ref, yb_ref, yc_ref, x_ref, w_ref, gw_ref, gb_ref, fg_ref, o_ref, *, final):
    y = jnp.concatenate([ya_ref[i] for i in range(N_SLABS)], axis=-1)
    g = _gelu_tanh(y)
    gate = jnp.dot(g.astype(BF16), gw_ref[...], preferred_element_type=F32) + gb_ref[...]
    ya = g * (1.0 / (1.0 + jnp.exp(-gate))) * _silu(za_ref[...].astype(F32))
    delta = (jnp.dot(ya.astype(BF16), w_ref[:W_BRANCH, :], preferred_element_type=F32)
             + jnp.dot(yb_ref[...], w_ref[W_BRANCH:2 * W_BRANCH, :], preferred_element_type=F32)
             + jnp.dot(yc_ref[...], w_ref[2 * W_BRANCH:, :], preferred_element_type=F32))
    x = x_ref[...] + delta
    if final:
        ms = jnp.mean(x * x, axis=-1, keepdims=True)
        x = x * lax.rsqrt(ms + RMS_EPS) * fg_ref[...]
    o_ref[...] = x


def _out_proj(ya_pre, proj, yb, yc, x2d, w_bf16, glu_w_bf16, glu_b, final_g, layer, final):
    rows = x2d.shape[0]
    row_blk = lambda width: pl.BlockSpec((OUT_TM, width), lambda i: (i, 0))
    const = lambda shape: pl.BlockSpec((None,) + shape, lambda i: (layer,) + (0,) * len(shape),
                                       pipeline_mode=pl.Buffered(1))
    return pl.pallas_call(
        functools.partial(_out_proj_kernel, final=final),
        grid=(rows // OUT_TM,),
        in_specs=[
            pl.BlockSpec((N_SLABS, OUT_TM, LANES), lambda i: (0, i, 0)),
            pl.BlockSpec((OUT_TM, W_BRANCH), lambda i: (i, COL_ZA // N_SLABS)),
            row_blk(W_BRANCH), row_blk(W_BRANCH), row_blk(D_MODEL),
            const((MIX_WIDTH, D_MODEL)), const((W_BRANCH, W_BRANCH)),
            const((1, W_BRANCH)),
            pl.BlockSpec((None, 1, D_MODEL), lambda i: (0, 0, 0), pipeline_mode=pl.Buffered(1)),
        ],
        out_specs=row_blk(D_MODEL),
        out_shape=jax.ShapeDtypeStruct((rows, D_MODEL), F32),
        compiler_params=pltpu.CompilerParams(
            dimension_semantics=("arbitrary",), vmem_limit_bytes=VMEM_LIMIT),
        name="out_proj_final" if final else "out_proj",
    )(ya_pre, proj, yb, yc, x2d, w_bf16, glu_w_bf16, glu_b, final_g)


def kernel(x, norm_g, w_in, w_out, ssm_lam_re, ssm_lam_im, ssm_log_dt, ssm_b_re, ssm_b_im, ssm_c_re,
           ssm_c_im, ssm_d, glu_w, glu_b, na_rpb, t5_bias, final_g):
    batch, seq, _ = x.shape
    depth = w_in.shape[0]
    x2d = x.astype(F32).reshape(batch * seq, D_MODEL)
    toep, s_in, s_out, decay = _ssm_prep(ssm_lam_re, ssm_lam_im, ssm_log_dt, ssm_b_re, ssm_b_im,
                                         ssm_c_re, ssm_c_im)
    dil_bias = _dil_bias_table(t5_bias)
    na_bias = _na_bias_table(na_rpb)
    w_in_bf, w_out_bf, glu_w_bf = w_in.astype(BF16), w_out.astype(BF16), glu_w.astype(BF16)
    row3 = lambda a: a.astype(F32).reshape(a.shape[0], 1, a.shape[1])
    norm_g3, ssm_d3, glu_b3 = row3(norm_g), row3(ssm_d), row3(glu_b)
    final_g3 = final_g.astype(F32).reshape(1, 1, D_MODEL)
    for l in range(depth):
        proj = _in_proj(x2d, norm_g3, w_in_bf, l)
        ya_pre = _ssm(proj, toep, s_in, s_out, decay, ssm_d3, l, batch, seq)
        yb = _na(proj, na_bias, l, batch, seq)
        yc = _dil(proj, dil_bias, batch, seq)
        x2d = _out_proj(ya_pre, proj, yb, yc, x2d, w_out_bf, glu_w_bf, glu_b3, final_g3, l,
                        final=(l == depth - 1))
    return x2d.reshape(batch, seq, D_MODEL).astype(x.dtype)
```

```python
import functools

import numpy as np
import jax
import jax.numpy as jnp
from jax import lax
from jax.experimental import pallas as pl
from jax.experimental.pallas import tpu as pltpu

F32 = jnp.float32
BF16 = jnp.bfloat16

D_MODEL = 1024
HEAD_DIM = 64
W_BRANCH = 512
N_HEADS = W_BRANCH // HEAD_DIM
N_HEAD_PAIRS = N_HEADS // 2
SSM_GROUP = 16
SSM_GROUPS = W_BRANCH // SSM_GROUP
SSM_STATE = 64
GRID_W = 64
NA_ROWS = 8
NA_COLS = 16
DIL_PATTERNS = ((128, 1), (512, 4), (2048, 16))
DIL_HALF = 64
T5_BUCKETS = 32
T5_MAX_DIST = 1024
RMS_EPS = 1e-6
NEG_INF = -1e30
IN_COLS = 10 * W_BRANCH
MIX_WIDTH = 3 * W_BRANCH

COL_XA, COL_ZA, COL_QB, COL_KB, COL_VB, COL_ZB, COL_QC, COL_KC, COL_VC, COL_ZC = (
    4 * i for i in range(10))

LOG2E = float(np.log2(np.e))
Q_SCALE_LOG2 = HEAD_DIM ** -0.5 * LOG2E

LANES = 128
CHUNK = 16
VMEM_LIMIT = 56 * 1024 * 1024


def _silu(z):
    return z * (1.0 / (1.0 + jnp.exp(-z)))


def _toeplitz_rows(vec, n_rows, n_cols):
    m = vec.shape[-1]
    assert m >= n_rows + n_cols - 1
    tiled = jnp.tile(vec, (1,) * (vec.ndim - 1) + (n_rows,))[..., :n_rows * (m - 1)]
    return tiled.reshape(vec.shape[:-1] + (n_rows, m - 1))[..., :n_cols]


def _gelu_tanh(x):
    return 0.5 * x * (1.0 + jnp.tanh(np.sqrt(2.0 / np.pi).astype(np.float32) * (x + 0.044715 * (x * x * x))))


IN_TM = 512
IN_TN = 512


def _in_proj_kernel(x_ref, g_ref, w_ref, o_ref):
    x = x_ref[...]
    ms = jnp.mean(x * x, axis=-1, keepdims=True)
    h = (x * lax.rsqrt(ms + RMS_EPS) * g_ref[...]).astype(BF16)
    for n in range(IN_COLS // IN_TN):
        o_ref[:, n * IN_TN:(n + 1) * IN_TN] = jnp.dot(
            h, w_ref[:, n * IN_TN:(n + 1) * IN_TN], preferred_element_type=F32).astype(BF16)


def _in_proj(x2d, g, w_bf16, layer):
    rows = x2d.shape[0]
    return pl.pallas_call(
        _in_proj_kernel,
        grid=(rows // IN_TM,),
        in_specs=[
            pl.BlockSpec((IN_TM, D_MODEL), lambda i: (i, 0)),
            pl.BlockSpec((None, 1, D_MODEL), lambda i: (layer, 0, 0)),
            pl.BlockSpec((None, D_MODEL, IN_COLS), lambda i: (layer, 0, 0), pipeline_mode=pl.Buffered(1)),
        ],
        out_specs=pl.BlockSpec((IN_TM, IN_COLS), lambda i: (i, 0)),
        out_shape=jax.ShapeDtypeStruct((rows, IN_COLS), BF16),
        compiler_params=pltpu.CompilerParams(
            dimension_semantics=("arbitrary",), vmem_limit_bytes=VMEM_LIMIT),
        name="in_proj",
    )(x2d, g, w_bf16)


def _na_bias_kernel(rpb_ref, o_ref):
    lane = lax.broadcasted_iota(jnp.int32, (GRID_W, LANES), 1)
    j = lax.broadcasted_iota(jnp.int32, (GRID_W, LANES), 0)
    c = lane & (GRID_W - 1)
    col_start = jnp.clip(j - NA_COLS // 2, 0, GRID_W - NA_COLS)
    valid = (c >= col_start) & (c < col_start + NA_COLS)
    first = lane < GRID_W
    even, odd = [], []
    for a in range(2 * NA_ROWS - 1):
        vec = pltpu.roll(rpb_ref[a:a + 1, :], LANES - (NA_COLS - 1), axis=1)
        x = jnp.broadcast_to(vec, (GRID_W, LANES))
        even.append(pltpu.roll(x, 0, axis=1, stride=1, stride_axis=0))
        odd.append(pltpu.roll(x, GRID_W, axis=1, stride=1, stride_axis=0))
    for v in range(NA_ROWS):
        for i in range(NA_ROWS // 2):
            a0 = 2 * i - v + (NA_ROWS - 1)
            tile = jnp.where(first, even[a0], odd[a0 + 1])
            o_ref[v, :, i * LANES:(i + 1) * LANES] = jnp.where(valid, tile * LOG2E, NEG_INF)


def _na_bias_table(rpb):
    depth = rpb.shape[0]
    padded = jnp.pad(rpb.astype(F32), ((0, 0), (0, 0), (0, 1), (0, LANES - (2 * NA_COLS - 1))))
    nkeys = NA_ROWS * GRID_W
    return pl.pallas_call(
        _na_bias_kernel,
        grid=(depth, N_HEADS),
        in_specs=[pl.BlockSpec((None, None, 2 * NA_ROWS, LANES), lambda l, h: (l, h, 0, 0))],
        out_specs=pl.BlockSpec((None, None, NA_ROWS, GRID_W, nkeys), lambda l, h: (l, h, 0, 0, 0)),
        out_shape=jax.ShapeDtypeStruct((depth, N_HEADS, NA_ROWS, GRID_W, nkeys), F32),
        compiler_params=pltpu.CompilerParams(dimension_semantics=("arbitrary", "arbitrary")),
        name="na_bias",
    )(padded)


def _na_kernel(q_ref, k_ref, v_ref, z_ref, bias_ref, o_ref, s_scr, mx_scr, p_scr):
    seq = q_ref.shape[0]
    rows = seq // GRID_W
    nkeys = NA_ROWS * GRID_W
    lane = lax.broadcasted_iota(jnp.int32, (GRID_W, LANES), 1)
    first_head = lane < HEAD_DIM
    nt = (((1,), (1,)), ((), ()))

    def window(r):
        rs = jnp.clip(r - NA_ROWS // 2, 0, rows - NA_ROWS)
        return r - rs, pl.multiple_of(r * GRID_W, GRID_W), pl.multiple_of(rs * GRID_W, GRID_W)

    def group(gi, carry):
        r_lo = gi * NA_GROUP

        def logits(i, c):
            var, q0, k0 = window(r_lo + i)
            qb = (q_ref[pl.ds(q0, GRID_W), :].astype(F32) * Q_SCALE_LOG2).astype(BF16)
            kw = k_ref[pl.ds(k0, nkeys), :]
            for hd in range(2):
                keep = first_head if hd == 0 else jnp.logical_not(first_head)
                qm = jnp.where(keep, qb, jnp.zeros_like(qb))
                s = lax.dot_general(qm, kw, nt, preferred_element_type=F32) + bias_ref[hd, var]
                s_scr[2 * i + hd] = s
                mx_scr[2 * i + hd] = jnp.broadcast_to(jnp.max(s, axis=-1, keepdims=True), (GRID_W, LANES))
            return c

        lax.fori_loop(0, NA_GROUP, logits, 0, unroll=True)

        def probs(t, c):
            m = mx_scr[t]
            psum = None
            for kc in range(nkeys // LANES):
                sl = slice(kc * LANES, (kc + 1) * LANES)
                p = jnp.exp2(s_scr[t, :, sl] - m)
                p_scr[t, :, sl] = p.astype(BF16)
                psum = p if psum is None else psum + p
            mx_scr[t] = jnp.broadcast_to(jnp.sum(psum, axis=-1, keepdims=True), (GRID_W, LANES))
            return c

        lax.fori_loop(0, 2 * NA_GROUP, probs, 0, unroll=True)

        def outputs(i, c):
            _, q0, k0 = window(r_lo + i)
            vw = v_ref[pl.ds(k0, nkeys), :]
            r0 = jnp.dot(p_scr[2 * i], vw, preferred_element_type=F32)
            r1 = jnp.dot(p_scr[2 * i + 1], vw, preferred_element_type=F32)
            acc = jnp.where(first_head, r0, r1)
            den = jnp.where(first_head, mx_scr[2 * i], mx_scr[2 * i + 1])
            z = z_ref[pl.ds(q0, GRID_W), :].astype(F32)
            o_ref[pl.ds(q0, GRID_W), :] = (acc * (1.0 / den) * _silu(z)).astype(BF16)
            return c

        lax.fori_loop(0, NA_GROUP, outputs, 0, unroll=True)
        return carry

    lax.fori_loop(0, rows // NA_GROUP, group, 0)


NA_GROUP = 16


def _na(proj, bias, layer, batch, seq):
    def col(c0):
        return pl.BlockSpec((seq, LANES), lambda b, p, c0=c0: (b, c0 + p))

    nkeys = NA_ROWS * GRID_W
    scratch = [pltpu.VMEM((2 * NA_GROUP, GRID_W, nkeys), F32),
               pltpu.VMEM((2 * NA_GROUP, GRID_W, LANES), F32),
               pltpu.VMEM((2 * NA_GROUP, GRID_W, nkeys), BF16)]

    return pl.pallas_call(
        _na_kernel,
        grid=(batch, N_HEAD_PAIRS),
        in_specs=[
            col(COL_QB), col(COL_KB), col(COL_VB), col(COL_ZB),
            pl.BlockSpec((None, 2, NA_ROWS, GRID_W, NA_ROWS * GRID_W), lambda b, p: (layer, p, 0, 0, 0)),
        ],
        out_specs=pl.BlockSpec((seq, LANES), lambda b, p: (b, p)),
        out_shape=jax.ShapeDtypeStruct((batch * seq, W_BRANCH), BF16),
        scratch_shapes=scratch,
        compiler_params=pltpu.CompilerParams(
            dimension_semantics=("arbitrary", "arbitrary"), vmem_limit_bytes=VMEM_LIMIT),
        name="na_attn",
    )(proj, proj, proj, proj, bias)


DIL_QB = 128
DIL_KW = 256
N_VARIANTS = 3


def _t5_bucket(rel):
    nb = T5_BUCKETS // 2
    max_exact = nb // 2
    n = np.abs(rel)
    large = max_exact + (np.log(np.maximum(n, 1) / max_exact) / np.log(T5_MAX_DIST / max_exact)
                         * (nb - max_exact)).astype(np.int32)
    large = np.minimum(large, nb - 1)
    return (np.where(rel > 0, nb, 0) + np.where(n < max_exact, n, large)).astype(np.int32)


def _dil_bias_table(t5_bias):
    n_pat = len(DIL_PATTERNS)
    lane = np.arange(DIL_VEC)
    n = np.where(lane < DIL_KW, lane, lane - DIL_VEC)
    in_range = (lane < DIL_KW) | (lane > DIL_VEC - DIL_QB)
    pick = np.zeros((n_pat, N_VARIANTS, T5_BUCKETS, DIL_VEC), np.float32)
    neg = np.zeros((n_pat, N_VARIANTS, 1, DIL_VEC), np.float32)
    for pi, (_, d) in enumerate(DIL_PATTERNS):
        for var in range(N_VARIANTS):
            step = n - DIL_HALF * var
            ok = in_range & (np.abs(step) <= DIL_HALF)
            bucket = _t5_bucket(d * np.clip(step, -DIL_HALF, DIL_HALF))
            pick[pi, var, bucket[ok], lane[ok]] = 1.0
            neg[pi, var, 0, ~ok] = NEG_INF
    vecs = jnp.einsum('bh,pvbn->hpvn', t5_bias.astype(F32), pick,
                      precision=lax.Precision.HIGHEST) * LOG2E + neg[:, :, 0][None]
    return pl.pallas_call(
        _dil_bias_kernel,
        grid=(N_HEADS,),
        in_specs=[pl.BlockSpec((None, n_pat * N_VARIANTS, DIL_VEC), lambda h: (h, 0, 0))],
        out_specs=pl.BlockSpec((None, n_pat, N_VARIANTS, DIL_QB, DIL_KW), lambda h: (h, 0, 0, 0, 0)),
        out_shape=jax.ShapeDtypeStruct((N_HEADS, n_pat, N_VARIANTS, DIL_QB, DIL_KW), F32),
        compiler_params=pltpu.CompilerParams(dimension_semantics=("arbitrary",)),
        name="dil_bias",
    )(vecs.reshape(N_HEADS, n_pat * N_VARIANTS, DIL_VEC))


DIL_VEC = 512


def _dil_bias_kernel(vec_ref, o_ref):
    for pi in range(len(DIL_PATTERNS)):
        for var in range(N_VARIANTS):
            i = pi * N_VARIANTS + var
            x = jnp.broadcast_to(vec_ref[i:i + 1, :], (DIL_QB, DIL_VEC))
            o_ref[pi, var] = pltpu.roll(x, 0, axis=1, stride=1, stride_axis=0)[:, :DIL_KW]


DIL_GROUP = 8
DIL_GROUP_FIRST = 16


def _dil_kernel(q_ref, k_ref, v_ref, z_ref, bias_ref, o_ref, qf, kf, vf, qg, kg, vg, qd, kd, vd,
                m_s, l_s, acc_s):
    seq = q_ref.shape[0]
    conv_rows = 256
    lane = lax.broadcasted_iota(jnp.int32, (DIL_QB, LANES), 1)
    first_head = lane < HEAD_DIM
    nt = (((1,), (1,)), ((), ()))

    def conv(i, carry):
        r0 = pl.multiple_of(i * conv_rows, conv_rows)
        qf[pl.ds(r0, conv_rows), :] = q_ref[pl.ds(r0, conv_rows), :].astype(F32) * Q_SCALE_LOG2
        kf[pl.ds(r0, conv_rows), :] = k_ref[pl.ds(r0, conv_rows), :].astype(F32)
        vf[pl.ds(r0, conv_rows), :] = v_ref[pl.ds(r0, conv_rows), :].astype(F32)
        return carry

    lax.fori_loop(0, seq // conv_rows, conv, 0)

    n4 = seq // 4
    n16 = seq // 16

    def to_order4(i, carry):
        dst = pl.multiple_of(i * conv_rows, conv_rows)
        src = pl.ds(dst // n4 + 4 * (dst % n4), conv_rows, stride=4)
        for a, b in ((qf, qg), (kf, kg), (vf, vg)):
            b[pl.ds(dst, conv_rows), :] = a[src, :]
        return carry

    lax.fori_loop(0, seq // conv_rows, to_order4, 0, unroll=2)

    def order16_rows_in_order4(rho):
        return pl.ds((rho % 4) * n4 + rho // 4, n16, stride=4)

    def to_order16(rho, carry):
        src = order16_rows_in_order4(rho)
        dst = pl.ds(pl.multiple_of(rho * n16, n16), n16)
        for a, b in ((qg, qd), (kg, kd), (vg, vd)):
            b[dst, :] = a[src, :].astype(BF16)
        return carry

    lax.fori_loop(0, 16, to_order16, 0, unroll=2)

    plan = ((2, 16, (qd, kd, vd), (m_s, l_s, acc_s)),
            (1, 4, (qd, kd, vd), (qf, kf, vf)),
            (0, 1, (qd, k_ref, v_ref), (m_s, l_s, acc_s)))

    for step, (pi, d, (q_src, k_src, v_src), (m_st, l_st, acc_st)) in enumerate(plan):
        n_sub = seq // d
        nblk = n_sub // DIL_QB

        if step == 1:
            def state_to_order4(rho, carry):
                dst = order16_rows_in_order4(rho)
                src = pl.ds(pl.multiple_of(rho * n16, n16), n16)
                for a, b in ((m_s, qf), (l_s, kf), (acc_s, vf)):
                    b[dst, :] = a[src, :]
                return carry

            lax.fori_loop(0, 16, state_to_order4, 0, unroll=2)

            def cast_order4(i, carry):
                r0 = pl.multiple_of(i * conv_rows, conv_rows)
                for a, b in ((qg, qd), (kg, kd), (vg, vd)):
                    b[pl.ds(r0, conv_rows), :] = a[pl.ds(r0, conv_rows), :].astype(BF16)
                return carry

            lax.fori_loop(0, seq // conv_rows, cast_order4, 0)
        if step == 2:
            def state_to_tokens(i, carry):
                src = pl.multiple_of(i * conv_rows, conv_rows)
                dst = pl.ds(src // n4 + 4 * (src % n4), conv_rows, stride=4)
                for a, b in ((qf, m_s), (kf, l_s), (vf, acc_s)):
                    b[dst, :] = a[pl.ds(src, conv_rows), :]
                qd[pl.ds(src, conv_rows), :] = (
                    q_ref[pl.ds(src, conv_rows), :].astype(F32) * Q_SCALE_LOG2).astype(BF16)
                return carry

            lax.fori_loop(0, seq // conv_rows, state_to_tokens, 0, unroll=2)

        n_group = DIL_GROUP_FIRST if step == 0 else DIL_GROUP

        def group(gi, carry, pi=pi, n_sub=n_sub, nblk=nblk, first=(step == 0), q_src=q_src,
                  k_src=k_src, v_src=v_src, m_st=m_st, l_st=l_st, acc_st=acc_st, n_group=n_group):
            units = []
            for i in range(n_group):
                u = gi * n_group + i
                res = u // nblk
                i0 = (u % nblk) * DIL_QB
                ks = jnp.clip(i0 - DIL_HALF, 0, n_sub - DIL_KW)
                units.append((res, i0, ks, (i0 - ks) // DIL_HALF))
            s_tiles, m_tiles = [], []
            for res, i0, ks, var in units:
                qb = q_src[pl.ds(pl.multiple_of(res * n_sub + i0, DIL_HALF), DIL_QB), :]
                kw = k_src[pl.ds(pl.multiple_of(res * n_sub + ks, DIL_HALF), DIL_KW), :]
                for hd in range(2):
                    keep = first_head if hd == 0 else jnp.logical_not(first_head)
                    qm = jnp.where(keep, qb, jnp.zeros_like(qb))
                    s = lax.dot_general(qm, kw, nt, preferred_element_type=F32) + bias_ref[hd, pi, var]
                    s_tiles.append(s)
                    m_tiles.append(jnp.max(s, axis=-1, keepdims=True))
            p_tiles, l_tiles = [], []
            for s, m in zip(s_tiles, m_tiles):
                p = jnp.exp2(s - m)
                l_tiles.append(jnp.sum(p, axis=-1, keepdims=True))
                p_tiles.append(p.astype(BF16))
            for i, (res, i0, ks, var) in enumerate(units):
                vw = v_src[pl.ds(pl.multiple_of(res * n_sub + ks, DIL_HALF), DIL_KW), :]
                acc0 = jnp.dot(p_tiles[2 * i], vw, preferred_element_type=F32)
                acc1 = jnp.dot(p_tiles[2 * i + 1], vw, preferred_element_type=F32)
                m_cur = jnp.where(first_head, m_tiles[2 * i], m_tiles[2 * i + 1])
                l_cur = jnp.where(first_head, l_tiles[2 * i], l_tiles[2 * i + 1])
                acc_cur = jnp.where(first_head, acc0, acc1)
                q_rows = pl.ds(pl.multiple_of(res * n_sub + i0, DIL_HALF), DIL_QB)
                if first:
                    m_st[q_rows, :] = m_cur
                    l_st[q_rows, :] = l_cur
                    acc_st[q_rows, :] = acc_cur
                else:
                    m_old = m_st[q_rows, :]
                    m_new = jnp.maximum(m_old, m_cur)
                    a_old = jnp.exp2(m_old - m_new)
                    a_cur = jnp.exp2(m_cur - m_new)
                    m_st[q_rows, :] = m_new
                    l_st[q_rows, :] = a_old * l_st[q_rows, :] + a_cur * l_cur
                    acc_st[q_rows, :] = a_old * acc_st[q_rows, :] + a_cur * acc_cur
            return carry

        lax.fori_loop(0, seq // DIL_QB // n_group, group, 0)

    def fin(i, carry):
        r0 = pl.multiple_of(i * conv_rows, conv_rows)
        z = z_ref[pl.ds(r0, conv_rows), :].astype(F32)
        y = acc_s[pl.ds(r0, conv_rows), :] * (1.0 / l_s[pl.ds(r0, conv_rows), :])
        o_ref[pl.ds(r0, conv_rows), :] = (y * _silu(z)).astype(BF16)
        return carry

    lax.fori_loop(0, seq // conv_rows, fin, 0, unroll=2)


def _dil(proj, bias, batch, seq):
    def col(c0):
        return pl.BlockSpec((seq, LANES), lambda b, p, c0=c0: (b, c0 + p))

    scratch = ([pltpu.VMEM((seq, LANES), F32) for _ in range(6)]
               + [pltpu.VMEM((seq, LANES), BF16) for _ in range(3)]
               + [pltpu.VMEM((seq, LANES), F32) for _ in range(3)])
    return pl.pallas_call(
        _dil_kernel,
        grid=(batch, N_HEAD_PAIRS),
        in_specs=[
            col(COL_QC), col(COL_KC), col(COL_VC), col(COL_ZC),
            pl.BlockSpec((2, len(DIL_PATTERNS), N_VARIANTS, DIL_QB, DIL_KW),
                         lambda b, p: (p, 0, 0, 0, 0)),
        ],
        out_specs=pl.BlockSpec((seq, LANES), lambda b, p: (b, p)),
        out_shape=jax.ShapeDtypeStruct((batch * seq, W_BRANCH), BF16),
        scratch_shapes=scratch,
        compiler_params=pltpu.CompilerParams(
            dimension_semantics=("arbitrary", "arbitrary"), vmem_limit_bytes=VMEM_LIMIT),
        name="dil_attn",
    )(proj, proj, proj, proj, bias)


N_POW = CHUNK + 1
N_LAGS = 2 * CHUNK - 1
PREP_GROUPS = 8


def _ssm_prep_kernel(lre_ref, lim_ref, ldt_ref, btre_ref, btim_ref, cre_ref, cim_ref,
                     toep_ref, win_ref, wout_ref, decay_ref):
    n = CHUNK * SSM_GROUP
    lane = lax.broadcasted_iota(jnp.int32, (SSM_GROUP, LANES), 1)
    lo = lane < SSM_STATE
    lane_n = lax.broadcasted_iota(jnp.int32, (SSM_GROUP, n), 1)
    nt = (((1,), (1,)), ((), ()))
    pair = lambda x, y: jnp.where(lo, x, y)

    for j in range(PREP_GROUPS):
        xs, ys, bbs, pw_last = {}, {}, {}, {}
        for dr in range(2):
            lam_re = lre_ref[0, dr, j]
            lam_im = lim_ref[0, dr, j]
            dt = jnp.exp(ldt_ref[0, dr, j])
            mag = jnp.exp(lam_re * dt)
            lb_re = mag * jnp.cos(lam_im * dt)
            lb_im = mag * jnp.sin(lam_im * dt)
            n_re = lb_re - 1.0
            den = lam_re * lam_re + lam_im * lam_im
            q_re = (n_re * lam_re + lb_im * lam_im) / den
            q_im = (lb_im * lam_re - n_re * lam_im) / den
            bt_re = btre_ref[0, dr, j]
            bt_im = btim_ref[0, dr, j]
            bb_re = q_re * bt_re - q_im * bt_im
            bb_im = q_re * bt_im + q_im * bt_re
            bbs[dr] = (bb_re, bb_im)
            c_re = cre_ref[0, dr, j]
            c_im = cim_ref[0, dr, j]
            pw_re = jnp.ones_like(lb_re)
            pw_im = jnp.zeros_like(lb_re)
            for k in range(N_POW):
                xs[dr, k] = (pw_re * bb_re - pw_im * bb_im, pw_re * bb_im + pw_im * bb_re)
                ys[dr, k] = (pw_re * c_re - pw_im * c_im, pw_re * c_im + pw_im * c_re)
                pw_last[dr] = (pw_re, pw_im)
                pw_re, pw_im = pw_re * lb_re - pw_im * lb_im, pw_re * lb_im + pw_im * lb_re

        step = [GROUPS_PER_SLAB * (a // GROUPS_PER_SLAB) + (a % GROUPS_PER_SLAB - j) % GROUPS_PER_SLAB
                for a in range(CHUNK)]
        rows = lambda a: slice(a * SSM_GROUP, (a + 1) * SSM_GROUP)

        decay_ref[0, 0, j:j + 1, :] = pair(pw_last[0][0], pw_last[1][0])[0:1]
        decay_ref[0, 1, j:j + 1, :] = pair(pw_last[0][1], pw_last[1][1])[0:1]

        for a, s in enumerate(step):
            win_ref[0, j, rows(a), :LANES] = pair(xs[0, CHUNK - 1 - s][0], xs[1, s][0]).astype(BF16)
            win_ref[0, j, rows(a), LANES:] = pair(xs[0, CHUNK - 1 - s][1], xs[1, s][1]).astype(BF16)

        z_re = jnp.concatenate([pair(ys[0, t + 1][0], ys[1, CHUNK - t][0]) for t in step], axis=0)
        z_nim = jnp.concatenate([pair(-ys[0, t + 1][1], -ys[1, CHUNK - t][1]) for t in step], axis=0)
        wout_ref[0, j, :LANES, :] = z_re.T.astype(BF16)
        wout_ref[0, j, LANES:, :] = z_nim.T.astype(BF16)

        kf = lax.dot_general(
            pair(bbs[0][0], -bbs[0][1]),
            jnp.concatenate([pair(*ys[0, m]) for m in range(CHUNK)], axis=0),
            nt, precision=lax.Precision.HIGHEST, preferred_element_type=F32)
        kb = lax.dot_general(
            pair(bbs[1][0], -bbs[1][1]),
            jnp.concatenate([pair(*ys[1, CHUNK - 1 - i]) for i in range(CHUNK)], axis=0),
            nt, precision=lax.Precision.HIGHEST, preferred_element_type=F32)
        for a, s in enumerate(step):
            fwd = pltpu.roll(kf, SSM_GROUP * s, axis=1) if s else kf
            shift_b = (n - SSM_GROUP * (CHUNK - 1 - s)) % n
            bwd = pltpu.roll(kb, shift_b, axis=1) if shift_b else kb
            t_nat = (jnp.where(lane_n >= SSM_GROUP * s, fwd, 0.0)
                     + jnp.where(lane_n < SSM_GROUP * (s + 1), bwd, 0.0))
            for h in range(2):
                half = t_nat[:, h * LANES:(h + 1) * LANES]
                if j:
                    half = pltpu.roll(half, SSM_GROUP * j, axis=1)
                toep_ref[0, j, rows(a), h * LANES:(h + 1) * LANES] = half.astype(BF16)


def _ssm_prep(lam_re, lam_im, log_dt, b_re, b_im, c_re, c_im):
    depth = lam_re.shape[0]
    G, P, C = SSM_GROUPS, SSM_STATE, SSM_GROUP
    n = CHUNK * C
    gb = PREP_GROUPS
    twice = lambda a: jnp.concatenate([a, a], axis=-1)
    rep = lambda a: jnp.broadcast_to(twice(a.astype(F32))[:, :, :, None, :], (depth, 2, G, C, LANES))
    ldt = jnp.broadcast_to(log_dt.astype(F32)[:, :, :, None, None], (depth, 2, G, C, LANES))
    bt = lambda a: twice(a.astype(F32).transpose(0, 1, 2, 4, 3))
    spec_in = pl.BlockSpec((1, 2, gb, C, LANES), lambda l, i: (l, 0, i, 0, 0))
    spec_w = pl.BlockSpec((1, gb, n, n), lambda l, i: (l, i, 0, 0))
    shp_w = jax.ShapeDtypeStruct((depth, G, n, n), BF16)
    return pl.pallas_call(
        _ssm_prep_kernel,
        grid=(depth, G // gb),
        in_specs=[spec_in] * 7,
        out_specs=[spec_w, spec_w, spec_w, pl.BlockSpec((1, 2, gb, LANES), lambda l, i: (l, 0, i, 0))],
        out_shape=[shp_w, shp_w, shp_w, jax.ShapeDtypeStruct((depth, 2, G, LANES), F32)],
        compiler_params=pltpu.CompilerParams(
            dimension_semantics=("arbitrary", "arbitrary"), vmem_limit_bytes=VMEM_LIMIT),
        name="ssm_prep",
    )(rep(lam_re), rep(lam_im), ldt, bt(b_re), bt(b_im), twice(c_re.astype(F32)), twice(c_im.astype(F32)))


GROUPS_PER_SLAB = LANES // SSM_GROUP
N_SLABS = W_BRANCH // LANES
RELAYOUT_ROWS = 32


def _ssm_kernel(xa_ref, toep_ref, win_ref, wout_ref, decay_ref, d_ref, o_ref, xf, u_s, st, hs, yq):
    seq = xa_ref.shape[0]
    nchunk = seq // CHUNK
    nrb = nchunk // RELAYOUT_ROWS
    gps = GROUPS_PER_SLAB
    lane_rb = lax.broadcasted_iota(jnp.int32, (RELAYOUT_ROWS, LANES), 1)
    seg_masks = [(lane_rb >= SSM_GROUP * sg) & (lane_rb < SSM_GROUP * (sg + 1)) for sg in range(gps)]
    conv_rows = 512

    for q in range(N_SLABS):
        c_lo = q * LANES

        def conv(i, carry, c_lo=c_lo):
            r0 = pl.multiple_of(i * conv_rows, conv_rows)
            xf[pl.ds(r0, conv_rows), :] = xa_ref[pl.ds(r0, conv_rows), c_lo:c_lo + LANES].astype(F32)
            return carry

        lax.fori_loop(0, seq // conv_rows, conv, 0)

        for hh in range(2):
            def fwd_relayout(rb, carry, hh=hh):
                c0 = pl.multiple_of(rb * RELAYOUT_ROWS, RELAYOUT_ROWS)
                rolled = []
                for r in range(gps):
                    xs = xf[pl.ds(CHUNK * c0 + gps * hh + r, RELAYOUT_ROWS, stride=CHUNK), :]
                    rolled.append(xs if r == 0 else pltpu.roll(xs, SSM_GROUP * r, axis=1))
                for j in range(gps):
                    out = rolled[(0 - j) % gps]
                    for sg in range(1, gps):
                        out = jnp.where(seg_masks[sg], rolled[(sg - j) % gps], out)
                    u_s[j, pl.ds(c0, RELAYOUT_ROWS), hh * LANES:(hh + 1) * LANES] = out.astype(BF16)
                return carry

            lax.fori_loop(0, nrb, fwd_relayout, 0, unroll=2)

        for j in range(gps):
            s_in = jnp.dot(u_s[j], win_ref[gps * q + j], preferred_element_type=F32)
            st[0, pl.ds(j, nchunk, stride=gps), :] = s_in[:, :LANES]
            st[1, pl.ds(j, nchunk, stride=gps), :] = s_in[:, LANES:]

        g_lo = gps * q
        a_re = decay_ref[0, g_lo:g_lo + gps, :]
        a_im = decay_ref[1, g_lo:g_lo + gps, :]
        lo, hi = slice(0, SSM_STATE), slice(SSM_STATE, LANES)

        def scan(i, carry):
            hrf, hif, hrb, hib = carry
            rf = pl.ds(pl.multiple_of(i * gps, gps), gps)
            rb_ = pl.ds(pl.multiple_of((nchunk - 1 - i) * gps, gps), gps)
            hs[0, rf, lo] = hrf[:, lo]
            hs[1, rf, lo] = hif[:, lo]
            hs[0, rb_, hi] = hrb[:, hi]
            hs[1, rb_, hi] = hib[:, hi]
            return (a_re * hrf - a_im * hif + st[0, rf, :], a_re * hif + a_im * hrf + st[1, rf, :],
                    a_re * hrb - a_im * hib + st[0, rb_, :], a_re * hib + a_im * hrb + st[1, rb_, :])

        zero = jnp.zeros((gps, LANES), F32)
        lax.fori_loop(0, nchunk, scan, (zero, zero, zero, zero), unroll=8)

        for j in range(gps):
            g = gps * q + j
            h_in = jnp.concatenate([hs[0, pl.ds(j, nchunk, stride=gps), :].astype(BF16),
                                    hs[1, pl.ds(j, nchunk, stride=gps), :].astype(BF16)], axis=-1)
            yq[j] = (jnp.dot(u_s[j], toep_ref[g], preferred_element_type=F32)
                     + jnp.dot(h_in, wout_ref[g], preferred_element_type=F32))

        d_row = d_ref[:, c_lo:c_lo + LANES]
        for hh in range(2):
            def bwd_relayout(rb, carry, hh=hh, d_row=d_row, q=q):
                c0 = pl.multiple_of(rb * RELAYOUT_ROWS, RELAYOUT_ROWS)
                ys = [yq[j, pl.ds(c0, RELAYOUT_ROWS), hh * LANES:(hh + 1) * LANES] for j in range(gps)]
                for r in range(gps):
                    merged = ys[(0 - r) % gps]
                    for sg in range(1, gps):
                        merged = jnp.where(seg_masks[sg], ys[(sg - r) % gps], merged)
                    if r:
                        merged = pltpu.roll(merged, LANES - SSM_GROUP * r, axis=1)
                    tok = pl.ds(CHUNK * c0 + gps * hh + r, RELAYOUT_ROWS, stride=CHUNK)
                    o_ref[q, tok, :] = merged + d_row * xf[tok, :]
                return carry

            lax.fori_loop(0, nrb, bwd_relayout, 0, unroll=2)


def _ssm(proj, toep, w_in, w_out, decay, d_skip, layer, batch, seq):
    nchunk = seq // CHUNK
    n = CHUNK * SSM_GROUP
    wspec = pl.BlockSpec((None, SSM_GROUPS, n, n), lambda b: (layer, 0, 0, 0), pipeline_mode=pl.Buffered(1))
    return pl.pallas_call(
        _ssm_kernel,
        grid=(batch,),
        in_specs=[
            pl.BlockSpec((seq, W_BRANCH), lambda b: (b, COL_XA // N_SLABS)),
            wspec, wspec, wspec,
            pl.BlockSpec((None, 2, SSM_GROUPS, LANES), lambda b: (layer, 0, 0, 0)),
            pl.BlockSpec((None, 1, W_BRANCH), lambda b: (layer, 0, 0)),
        ],
        out_specs=pl.BlockSpec((N_SLABS, seq, LANES), lambda b: (0, b, 0)),
        out_shape=jax.ShapeDtypeStruct((N_SLABS, batch * seq, LANES), F32),
        scratch_shapes=[
            pltpu.VMEM((seq, LANES), F32),
            pltpu.VMEM((GROUPS_PER_SLAB, nchunk, n), BF16),
            pltpu.VMEM((2, nchunk * GROUPS_PER_SLAB, LANES), F32),
            pltpu.VMEM((2, nchunk * GROUPS_PER_SLAB, LANES), F32),
            pltpu.VMEM((GROUPS_PER_SLAB, nchunk, n), F32),
        ],
        compiler_params=pltpu.CompilerParams(
            dimension_semantics=("arbitrary",), vmem_limit_bytes=VMEM_LIMIT),
        name="ssm_mix",
    )(proj, toep, w_in, w_out, decay, d_skip)


OUT_TM = 512


def _out_proj_kernel(ya_ref, za_ref, yb_ref, yc_ref, x_ref, w_ref, gw_ref, gb_ref, fg_ref, o_ref, *, final):
    y = jnp.concatenate([ya_ref[i] for i in range(N_SLABS)], axis=-1)
    g = _gelu_tanh(y)
    gate = jnp.dot(g.astype(BF16), gw_ref[...], preferred_element_type=F32) + gb_ref[...]
    ya = g * (1.0 / (1.0 + jnp.exp(-gate))) * _silu(za_ref[...].astype(F32))
    delta = (jnp.dot(ya.astype(BF16), w_ref[:W_BRANCH, :], preferred_element_type=F32)
             + jnp.dot(yb_ref[...], w_ref[W_BRANCH:2 * W_BRANCH, :], preferred_element_type=F32)
             + jnp.dot(yc_ref[...], w_ref[2 * W_BRANCH:, :], preferred_element_type=F32))
    x = x_ref[...] + delta
    if final:
        ms = jnp.mean(x * x, axis=-1, keepdims=True)
        x = x * lax.rsqrt(ms + RMS_EPS) * fg_ref[...]
    o_ref[...] = x


def _out_proj(ya_pre, proj, yb, yc, x2d, w_bf16, glu_w_bf16, glu_b, final_g, layer, final):
    rows = x2d.shape[0]
    row_blk = lambda width: pl.BlockSpec((OUT_TM, width), lambda i: (i, 0))
    const = lambda shape: pl.BlockSpec((None,) + shape, lambda i: (layer,) + (0,) * len(shape),
                                       pipeline_mode=pl.Buffered(1))
    return pl.pallas_call(
        functools.partial(_out_proj_kernel, final=final),
        grid=(rows // OUT_TM,),
        in_specs=[
            pl.BlockSpec((N_SLABS, OUT_TM, LANES), lambda i: (0, i, 0)),
            pl.BlockSpec((OUT_TM, W_BRANCH), lambda i: (i, COL_ZA // N_SLABS)),
            row_blk(W_BRANCH), row_blk(W_BRANCH), row_blk(D_MODEL),
            const((MIX_WIDTH, D_MODEL)), const((W_BRANCH, W_BRANCH)),
            const((1, W_BRANCH)),
            pl.BlockSpec((None, 1, D_MODEL), lambda i: (0, 0, 0), pipeline_mode=pl.Buffered(1)),
        ],
        out_specs=row_blk(D_MODEL),
        out_shape=jax.ShapeDtypeStruct((rows, D_MODEL), F32),
        compiler_params=pltpu.CompilerParams(
            dimension_semantics=("arbitrary",), vmem_limit_bytes=VMEM_LIMIT),
        name="out_proj_final" if final else "out_proj",
    )(ya_pre, proj, yb, yc, x2d, w_bf16, glu_w_bf16, glu_b, final_g)


def kernel(x, norm_g, w_in, w_out, ssm_lam_re, ssm_lam_im, ssm_log_dt, ssm_b_re, ssm_b_im, ssm_c_re,
           ssm_c_im, ssm_d, glu_w, glu_b, na_rpb, t5_bias, final_g):
    batch, seq, _ = x.shape
    depth = w_in.shape[0]
    x2d = x.astype(F32).reshape(batch * seq, D_MODEL)
    toep, s_in, s_out, decay = _ssm_prep(ssm_lam_re, ssm_lam_im, ssm_log_dt, ssm_b_re, ssm_b_im,
                                         ssm_c_re, ssm_c_im)
    dil_bias = _dil_bias_table(t5_bias)
    na_bias = _na_bias_table(na_rpb)
    w_in_bf, w_out_bf, glu_w_bf = w_in.astype(BF16), w_out.astype(BF16), glu_w.astype(BF16)
    row3 = lambda a: a.astype(F32).reshape(a.shape[0], 1, a.shape[1])
    norm_g3, ssm_d3, glu_b3 = row3(norm_g), row3(ssm_d), row3(glu_b)
    final_g3 = final_g.astype(F32).reshape(1, 1, D_MODEL)
    for l in range(depth):
        proj = _in_proj(x2d, norm_g3, w_in_bf, l)
        ya_pre = _ssm(proj, toep, s_in, s_out, decay, ssm_d3, l, batch, seq)
        yb = _na(proj, na_bias, l, batch, seq)
        yc = _dil(proj, dil_bias, batch, seq)
        x2d = _out_proj(ya_pre, proj, yb, yc, x2d, w_out_bf, glu_w_bf, glu_b3, final_g3, l,
                        final=(l == depth - 1))
    return x2d.reshape(batch, seq, D_MODEL).astype(x.dtype)
```

```python
import functools

import numpy as np
import jax
import jax.numpy as jnp
from jax import lax
from jax.experimental import pallas as pl
from jax.experimental.pallas import tpu as pltpu

F32 = jnp.float32
BF16 = jnp.bfloat16

D_MODEL = 1024
HEAD_DIM = 64
W_BRANCH = 512
N_HEADS = W_BRANCH // HEAD_DIM
N_HEAD_PAIRS = N_HEADS // 2
SSM_GROUP = 16
SSM_GROUPS = W_BRANCH // SSM_GROUP
SSM_STATE = 64
GRID_W = 64
NA_ROWS = 8
NA_COLS = 16
DIL_PATTERNS = ((128, 1), (512, 4), (2048, 16))
DIL_HALF = 64
T5_BUCKETS = 32
T5_MAX_DIST = 1024
RMS_EPS = 1e-6
NEG_INF = -1e30
IN_COLS = 10 * W_BRANCH
MIX_WIDTH = 3 * W_BRANCH

COL_XA, COL_ZA, COL_QB, COL_KB, COL_VB, COL_ZB, COL_QC, COL_KC, COL_VC, COL_ZC = (
    4 * i for i in range(10))

LOG2E = float(np.log2(np.e))
Q_SCALE_LOG2 = HEAD_DIM ** -0.5 * LOG2E

LANES = 128
CHUNK = 16
VMEM_LIMIT = 56 * 1024 * 1024


def _silu(z):
    return z * (1.0 / (1.0 + jnp.exp(-z)))


def _toeplitz_rows(vec, n_rows, n_cols):
    m = vec.shape[-1]
    assert m >= n_rows + n_cols - 1
    tiled = jnp.tile(vec, (1,) * (vec.ndim - 1) + (n_rows,))[..., :n_rows * (m - 1)]
    return tiled.reshape(vec.shape[:-1] + (n_rows, m - 1))[..., :n_cols]


def _gelu_tanh(x):
    return 0.5 * x * (1.0 + jnp.tanh(np.sqrt(2.0 / np.pi).astype(np.float32) * (x + 0.044715 * (x * x * x))))


IN_TM = 512
IN_TN = 512


def _in_proj_kernel(x_ref, g_ref, w_ref, o_ref):
    x = x_ref[...]
    ms = jnp.mean(x * x, axis=-1, keepdims=True)
    h = (x * lax.rsqrt(ms + RMS_EPS) * g_ref[...]).astype(BF16)
    for n in range(IN_COLS // IN_TN):
        o_ref[:, n * IN_TN:(n + 1) * IN_TN] = jnp.dot(
            h, w_ref[:, n * IN_TN:(n + 1) * IN_TN], preferred_element_type=F32).astype(BF16)


def _in_proj(x2d, g, w_bf16, layer):
    rows = x2d.shape[0]
    return pl.pallas_call(
        _in_proj_kernel,
        grid=(rows // IN_TM,),
        in_specs=[
            pl.BlockSpec((IN_TM, D_MODEL), lambda i: (i, 0)),
            pl.BlockSpec((None, 1, D_MODEL), lambda i: (layer, 0, 0)),
            pl.BlockSpec((None, D_MODEL, IN_COLS), lambda i: (layer, 0, 0), pipeline_mode=pl.Buffered(1)),
        ],
        out_specs=pl.BlockSpec((IN_TM, IN_COLS), lambda i: (i, 0)),
        out_shape=jax.ShapeDtypeStruct((rows, IN_COLS), BF16),
        compiler_params=pltpu.CompilerParams(
            dimension_semantics=("arbitrary",), vmem_limit_bytes=VMEM_LIMIT),
        name="in_proj",
    )(x2d, g, w_bf16)


def _na_bias_kernel(rpb_ref, o_ref):
    lane = lax.broadcasted_iota(jnp.int32, (GRID_W, LANES), 1)
    j = lax.broadcasted_iota(jnp.int32, (GRID_W, LANES), 0)
    c = lane & (GRID_W - 1)
    col_start = jnp.clip(j - NA_COLS // 2, 0, GRID_W - NA_COLS)
    valid = (c >= col_start) & (c < col_start + NA_COLS)
    first = lane < GRID_W
    even, odd = [], []
    for a in range(2 * NA_ROWS - 1):
        vec = pltpu.roll(rpb_ref[a:a + 1, :], LANES - (NA_COLS - 1), axis=1)
        x = jnp.broadcast_to(vec, (GRID_W, LANES))
        even.append(pltpu.roll(x, 0, axis=1, stride=1, stride_axis=0))
        odd.append(pltpu.roll(x, GRID_W, axis=1, stride=1, stride_axis=0))
    for v in range(NA_ROWS):
        for i in range(NA_ROWS // 2):
            a0 = 2 * i - v + (NA_ROWS - 1)
            tile = jnp.where(first, even[a0], odd[a0 + 1])
            o_ref[v, :, i * LANES:(i + 1) * LANES] = jnp.where(valid, tile * LOG2E, NEG_INF)


def _na_bias_table(rpb):
    depth = rpb.shape[0]
    padded = jnp.pad(rpb.astype(F32), ((0, 0), (0, 0), (0, 1), (0, LANES - (2 * NA_COLS - 1))))
    nkeys = NA_ROWS * GRID_W
    return pl.pallas_call(
        _na_bias_kernel,
        grid=(depth, N_HEADS),
        in_specs=[pl.BlockSpec((None, None, 2 * NA_ROWS, LANES), lambda l, h: (l, h, 0, 0))],
        out_specs=pl.BlockSpec((None, None, NA_ROWS, GRID_W, nkeys), lambda l, h: (l, h, 0, 0, 0)),
        out_shape=jax.ShapeDtypeStruct((depth, N_HEADS, NA_ROWS, GRID_W, nkeys), F32),
        compiler_params=pltpu.CompilerParams(dimension_semantics=("arbitrary", "arbitrary")),
        name="na_bias",
    )(padded)


def _na_kernel(q_ref, k_ref, v_ref, z_ref, bias_ref, o_ref, s_scr, mx_scr, p_scr):
    seq = q_ref.shape[0]
    rows = seq // GRID_W
    nkeys = NA_ROWS * GRID_W
    lane = lax.broadcasted_iota(jnp.int32, (GRID_W, LANES), 1)
    first_head = lane < HEAD_DIM
    nt = (((1,), (1,)), ((), ()))

    def window(r):
        rs = jnp.clip(r - NA_ROWS // 2, 0, rows - NA_ROWS)
        return r - rs, pl.multiple_of(r * GRID_W, GRID_W), pl.multiple_of(rs * GRID_W, GRID_W)

    def group(gi, carry):
        r_lo = gi * NA_GROUP

        def logits(i, c):
            var, q0, k0 = window(r_lo + i)
            qb = (q_ref[pl.ds(q0, GRID_W), :].astype(F32) * Q_SCALE_LOG2).astype(BF16)
            kw = k_ref[pl.ds(k0, nkeys), :]
            for hd in range(2):
                keep = first_head if hd == 0 else jnp.logical_not(first_head)
                qm = jnp.where(keep, qb, jnp.zeros_like(qb))
                s = lax.dot_general(qm, kw, nt, preferred_element_type=F32) + bias_ref[hd, var]
                s_scr[2 * i + hd] = s
                mx_scr[2 * i + hd] = jnp.broadcast_to(jnp.max(s, axis=-1, keepdims=True), (GRID_W, LANES))
            return c

        lax.fori_loop(0, NA_GROUP, logits, 0, unroll=True)

        def probs(t, c):
            m = mx_scr[t]
            psum = None
            for kc in range(nkeys // LANES):
                sl = slice(kc * LANES, (kc + 1) * LANES)
                p = jnp.exp2(s_scr[t, :, sl] - m)
                p_scr[t, :, sl] = p.astype(BF16)
                psum = p if psum is None else psum + p
            mx_scr[t] = jnp.broadcast_to(jnp.sum(psum, axis=-1, keepdims=True), (GRID_W, LANES))
            return c

        lax.fori_loop(0, 2 * NA_GROUP, probs, 0, unroll=True)

        def outputs(i, c):
            _, q0, k0 = window(r_lo + i)
            vw = v_ref[pl.ds(k0, nkeys), :]
            r0 = jnp.dot(p_scr[2 * i], vw, preferred_element_type=F32)
            r1 = jnp.dot(p_scr[2 * i + 1], vw, preferred_element_type=F32)
            acc = jnp.where(first_head, r0, r1)
            den = jnp.where(first_head, mx_scr[2 * i], mx_scr[2 * i + 1])
            z = z_ref[pl.ds(q0, GRID_W), :].astype(F32)
            o_ref[pl.ds(q0, GRID_W), :] = (acc * (1.0 / den) * _silu(z)).astype(BF16)
            return c

        lax.fori_loop(0, NA_GROUP, outputs, 0, unroll=True)
        return carry

    lax.fori_loop(0, rows // NA_GROUP, group, 0)


NA_GROUP = 16


def _na(proj, bias, layer, batch, seq):
    def col(c0):
        return pl.BlockSpec((seq, LANES), lambda b, p, c0=c0: (b, c0 + p))

    nkeys = NA_ROWS * GRID_W
    scratch = [pltpu.VMEM((2 * NA_GROUP, GRID_W, nkeys), F32),
               pltpu.VMEM((2 * NA_GROUP, GRID_W, LANES), F32),
               pltpu.VMEM((2 * NA_GROUP, GRID_W, nkeys), BF16)]

    return pl.pallas_call(
        _na_kernel,
        grid=(batch, N_HEAD_PAIRS),
        in_specs=[
            col(COL_QB), col(COL_KB), col(COL_VB), col(COL_ZB),
            pl.BlockSpec((None, 2, NA_ROWS, GRID_W, NA_ROWS * GRID_W), lambda b, p: (layer, p, 0, 0, 0)),
        ],
        out_specs=pl.BlockSpec((seq, LANES), lambda b, p: (b, p)),
        out_shape=jax.ShapeDtypeStruct((batch * seq, W_BRANCH), BF16),
        scratch_shapes=scratch,
        compiler_params=pltpu.CompilerParams(
            dimension_semantics=("arbitrary", "arbitrary"), vmem_limit_bytes=VMEM_LIMIT),
        name="na_attn",
    )(proj, proj, proj, proj, bias)


DIL_QB = 128
DIL_KW = 256
N_VARIANTS = 3


def _t5_bucket(rel):
    nb = T5_BUCKETS // 2
    max_exact = nb // 2
    n = np.abs(rel)
    large = max_exact + (np.log(np.maximum(n, 1) / max_exact) / np.log(T5_MAX_DIST / max_exact)
                         * (nb - max_exact)).astype(np.int32)
    large = np.minimum(large, nb - 1)
    return (np.where(rel > 0, nb, 0) + np.where(n < max_exact, n, large)).astype(np.int32)


def _dil_bias_table(t5_bias):
    n_pat = len(DIL_PATTERNS)
    lane = np.arange(DIL_VEC)
    n = np.where(lane < DIL_KW, lane, lane - DIL_VEC)
    in_range = (lane < DIL_KW) | (lane > DIL_VEC - DIL_QB)
    pick = np.zeros((n_pat, N_VARIANTS, T5_BUCKETS, DIL_VEC), np.float32)
    neg = np.zeros((n_pat, N_VARIANTS, 1, DIL_VEC), np.float32)
    for pi, (_, d) in enumerate(DIL_PATTERNS):
        for var in range(N_VARIANTS):
            step = n - DIL_HALF * var
            ok = in_range & (np.abs(step) <= DIL_HALF)
            bucket = _t5_bucket(d * np.clip(step, -DIL_HALF, DIL_HALF))
            pick[pi, var, bucket[ok], lane[ok]] = 1.0
            neg[pi, var, 0, ~ok] = NEG_INF
    vecs = jnp.einsum('bh,pvbn->hpvn', t5_bias.astype(F32), pick,
                      precision=lax.Precision.HIGHEST) * LOG2E + neg[:, :, 0][None]
    return pl.pallas_call(
        _dil_bias_kernel,
        grid=(N_HEADS,),
        in_specs=[pl.BlockSpec((None, n_pat * N_VARIANTS, DIL_VEC), lambda h: (h, 0, 0))],
        out_specs=pl.BlockSpec((None, n_pat, N_VARIANTS, DIL_QB, DIL_KW), lambda h: (h, 0, 0, 0, 0)),
        out_shape=jax.ShapeDtypeStruct((N_HEADS, n_pat, N_VARIANTS, DIL_QB, DIL_KW), F32),
        compiler_params=pltpu.CompilerParams(dimension_semantics=("arbitrary",)),
        name="dil_bias",
    )(vecs.reshape(N_HEADS, n_pat * N_VARIANTS, DIL_VEC))


DIL_VEC = 512


def _dil_bias_kernel(vec_ref, o_ref):
    for pi in range(len(DIL_PATTERNS)):
        for var in range(N_VARIANTS):
            i = pi * N_VARIANTS + var
            x = jnp.broadcast_to(vec_ref[i:i + 1, :], (DIL_QB, DIL_VEC))
            o_ref[pi, var] = pltpu.roll(x, 0, axis=1, stride=1, stride_axis=0)[:, :DIL_KW]


DIL_GROUP = 8
DIL_GROUP_FIRST = 16


def _dil_kernel(q_ref, k_ref, v_ref, z_ref, bias_ref, o_ref, qf, kf, vf, qg, kg, vg, qd, kd, vd,
                m_s, l_s, acc_s):
    seq = q_ref.shape[0]
    conv_rows = 256
    lane = lax.broadcasted_iota(jnp.int32, (DIL_QB, LANES), 1)
    first_head = lane < HEAD_DIM
    nt = (((1,), (1,)), ((), ()))
    ones = jnp.ones((DIL_KW, LANES), BF16)

    def conv(i, carry):
        r0 = pl.multiple_of(i * conv_rows, conv_rows)
        qf[pl.ds(r0, conv_rows), :] = q_ref[pl.ds(r0, conv_rows), :].astype(F32) * Q_SCALE_LOG2
        kf[pl.ds(r0, conv_rows), :] = k_ref[pl.ds(r0, conv_rows), :].astype(F32)
        vf[pl.ds(r0, conv_rows), :] = v_ref[pl.ds(r0, conv_rows), :].astype(F32)
        return carry

    lax.fori_loop(0, seq // conv_rows, conv, 0)

    n4 = seq // 4
    n16 = seq // 16

    def to_order4(i, carry):
        dst = pl.multiple_of(i * conv_rows, conv_rows)
        src = pl.ds(dst // n4 + 4 * (dst % n4), conv_rows, stride=4)
        for a, b in ((qf, qg), (kf, kg), (vf, vg)):
            b[pl.ds(dst, conv_rows), :] = a[src, :]
        return carry

    lax.fori_loop(0, seq // conv_rows, to_order4, 0, unroll=2)

    def order16_rows_in_order4(rho):
        return pl.ds((rho % 4) * n4 + rho // 4, n16, stride=4)

    def to_order16(rho, carry):
        src = order16_rows_in_order4(rho)
        dst = pl.ds(pl.multiple_of(rho * n16, n16), n16)
        for a, b in ((qg, qd), (kg, kd), (vg, vd)):
            b[dst, :] = a[src, :].astype(BF16)
        return carry

    lax.fori_loop(0, 16, to_order16, 0, unroll=2)

    plan = ((2, 16, (qd, kd, vd), (m_s, l_s, acc_s)),
            (1, 4, (qd, kd, vd), (qf, kf, vf)),
            (0, 1, (qd, k_ref, v_ref), (m_s, l_s, acc_s)))

    for step, (pi, d, (q_src, k_src, v_src), (m_st, l_st, acc_st)) in enumerate(plan):
        n_sub = seq // d
        nblk = n_sub // DIL_QB

        if step == 1:
            def state_to_order4(rho, carry):
                dst = order16_rows_in_order4(rho)
                src = pl.ds(pl.multiple_of(rho * n16, n16), n16)
                for a, b in ((m_s, qf), (l_s, kf), (acc_s, vf)):
                    b[dst, :] = a[src, :]
                return carry

            lax.fori_loop(0, 16, state_to_order4, 0, unroll=2)

            def cast_order4(i, carry):
                r0 = pl.multiple_of(i * conv_rows, conv_rows)
                for a, b in ((qg, qd), (kg, kd), (vg, vd)):
                    b[pl.ds(r0, conv_rows), :] = a[pl.ds(r0, conv_rows), :].astype(BF16)
                return carry

            lax.fori_loop(0, seq // conv_rows, cast_order4, 0)
        if step == 2:
            def state_to_tokens(i, carry):
                src = pl.multiple_of(i * conv_rows, conv_rows)
                dst = pl.ds(src // n4 + 4 * (src % n4), conv_rows, stride=4)
                for a, b in ((qf, m_s), (kf, l_s), (vf, acc_s)):
                    b[dst, :] = a[pl.ds(src, conv_rows), :]
                qd[pl.ds(src, conv_rows), :] = (
                    q_ref[pl.ds(src, conv_rows), :].astype(F32) * Q_SCALE_LOG2).astype(BF16)
                return carry

            lax.fori_loop(0, seq // conv_rows, state_to_tokens, 0, unroll=2)

        n_group = DIL_GROUP_FIRST if step == 0 else DIL_GROUP

        def group(gi, carry, pi=pi, n_sub=n_sub, nblk=nblk, first=(step == 0), q_src=q_src,
                  k_src=k_src, v_src=v_src, m_st=m_st, l_st=l_st, acc_st=acc_st, n_group=n_group):
            units = []
            for i in range(n_group):
                u = gi * n_group + i
                res = u // nblk
                i0 = (u % nblk) * DIL_QB
                ks = jnp.clip(i0 - DIL_HALF, 0, n_sub - DIL_KW)
                units.append((res, i0, ks, (i0 - ks) // DIL_HALF))
            s_tiles, m_tiles = [], []
            for res, i0, ks, var in units:
                qb = q_src[pl.ds(pl.multiple_of(res * n_sub + i0, DIL_HALF), DIL_QB), :]
                kw = k_src[pl.ds(pl.multiple_of(res * n_sub + ks, DIL_HALF), DIL_KW), :]
                for hd in range(2):
                    keep = first_head if hd == 0 else jnp.logical_not(first_head)
                    qm = jnp.where(keep, qb, jnp.zeros_like(qb))
                    s = lax.dot_general(qm, kw, nt, preferred_element_type=F32) + bias_ref[hd, pi, var]
                    s = s.astype(BF16)
                    s_tiles.append(s)
                    m_tiles.append(jnp.max(s, axis=-1, keepdims=True))
            p_tiles = [jnp.exp2(s - m) for s, m in zip(s_tiles, m_tiles)]
            for i, (res, i0, ks, var) in enumerate(units):
                vw = v_src[pl.ds(pl.multiple_of(res * n_sub + ks, DIL_HALF), DIL_KW), :]
                vext = jnp.concatenate([vw, ones], axis=-1)
                r0 = jnp.dot(p_tiles[2 * i], vext, preferred_element_type=F32)
                r1 = jnp.dot(p_tiles[2 * i + 1], vext, preferred_element_type=F32)
                m_cur = jnp.where(first_head, m_tiles[2 * i].astype(F32), m_tiles[2 * i + 1].astype(F32))
                l_cur = jnp.where(first_head, r0[:, LANES:], r1[:, LANES:])
                acc_cur = jnp.where(first_head, r0[:, :LANES], r1[:, :LANES])
                q_rows = pl.ds(pl.multiple_of(res * n_sub + i0, DIL_HALF), DIL_QB)
                if first:
                    m_st[q_rows, :] = m_cur
                    l_st[q_rows, :] = l_cur
                    acc_st[q_rows, :] = acc_cur
                else:
                    m_old = m_st[q_rows, :]
                    m_new = jnp.maximum(m_old, m_cur)
                    a_old = jnp.exp2(m_old - m_new)
                    a_cur = jnp.exp2(m_cur - m_new)
                    m_st[q_rows, :] = m_new
                    l_st[q_rows, :] = a_old * l_st[q_rows, :] + a_cur * l_cur
                    acc_st[q_rows, :] = a_old * acc_st[q_rows, :] + a_cur * acc_cur
            return carry

        lax.fori_loop(0, seq // DIL_QB // n_group, group, 0)

    def fin(i, carry):
        r0 = pl.multiple_of(i * conv_rows, conv_rows)
        z = z_ref[pl.ds(r0, conv_rows), :].astype(F32)
        y = acc_s[pl.ds(r0, conv_rows), :] * (1.0 / l_s[pl.ds(r0, conv_rows), :])
        o_ref[pl.ds(r0, conv_rows), :] = (y * _silu(z)).astype(BF16)
        return carry

    lax.fori_loop(0, seq // conv_rows, fin, 0, unroll=2)


def _dil(proj, bias, batch, seq):
    def col(c0):
        return pl.BlockSpec((seq, LANES), lambda b, p, c0=c0: (b, c0 + p))

    scratch = ([pltpu.VMEM((seq, LANES), F32) for _ in range(6)]
               + [pltpu.VMEM((seq, LANES), BF16) for _ in range(3)]
               + [pltpu.VMEM((seq, LANES), F32) for _ in range(3)])
    return pl.pallas_call(
        _dil_kernel,
        grid=(batch, N_HEAD_PAIRS),
        in_specs=[
            col(COL_QC), col(COL_KC), col(COL_VC), col(COL_ZC),
            pl.BlockSpec((2, len(DIL_PATTERNS), N_VARIANTS, DIL_QB, DIL_KW),
                         lambda b, p: (p, 0, 0, 0, 0)),
        ],
        out_specs=pl.BlockSpec((seq, LANES), lambda b, p: (b, p)),
        out_shape=jax.ShapeDtypeStruct((batch * seq, W_BRANCH), BF16),
        scratch_shapes=scratch,
        compiler_params=pltpu.CompilerParams(
            dimension_semantics=("arbitrary", "arbitrary"), vmem_limit_bytes=VMEM_LIMIT),
        name="dil_attn",
    )(proj, proj, proj, proj, bias)


N_POW = CHUNK + 1
N_LAGS = 2 * CHUNK - 1
PREP_GROUPS = 8


def _ssm_prep_kernel(lre_ref, lim_ref, ldt_ref, btre_ref, btim_ref, cre_ref, cim_ref,
                     toep_ref, win_ref, wout_ref, decay_ref):
    n = CHUNK * SSM_GROUP
    lane = lax.broadcasted_iota(jnp.int32, (SSM_GROUP, LANES), 1)
    lo = lane < SSM_STATE
    lane_n = lax.broadcasted_iota(jnp.int32, (SSM_GROUP, n), 1)
    nt = (((1,), (1,)), ((), ()))
    pair = lambda x, y: jnp.where(lo, x, y)

    for j in range(PREP_GROUPS):
        xs, ys, bbs, pw_last = {}, {}, {}, {}
        for dr in range(2):
            lam_re = lre_ref[0, dr, j]
            lam_im = lim_ref[0, dr, j]
            dt = jnp.exp(ldt_ref[0, dr, j])
            mag = jnp.exp(lam_re * dt)
            lb_re = mag * jnp.cos(lam_im * dt)
            lb_im = mag * jnp.sin(lam_im * dt)
            n_re = lb_re - 1.0
            den = lam_re * lam_re + lam_im * lam_im
            q_re = (n_re * lam_re + lb_im * lam_im) / den
            q_im = (lb_im * lam_re - n_re * lam_im) / den
            bt_re = btre_ref[0, dr, j]
            bt_im = btim_ref[0, dr, j]
            bb_re = q_re * bt_re - q_im * bt_im
            bb_im = q_re * bt_im + q_im * bt_re
            bbs[dr] = (bb_re, bb_im)
            c_re = cre_ref[0, dr, j]
            c_im = cim_ref[0, dr, j]
            pw_re = jnp.ones_like(lb_re)
            pw_im = jnp.zeros_like(lb_re)
            for k in range(N_POW):
                xs[dr, k] = (pw_re * bb_re - pw_im * bb_im, pw_re * bb_im + pw_im * bb_re)
                ys[dr, k] = (pw_re * c_re - pw_im * c_im, pw_re * c_im + pw_im * c_re)
                pw_last[dr] = (pw_re, pw_im)
                pw_re, pw_im = pw_re * lb_re - pw_im * lb_im, pw_re * lb_im + pw_im * lb_re

        step = [GROUPS_PER_SLAB * (a // GROUPS_PER_SLAB) + (a % GROUPS_PER_SLAB - j) % GROUPS_PER_SLAB
                for a in range(CHUNK)]
        rows = lambda a: slice(a * SSM_GROUP, (a + 1) * SSM_GROUP)

        decay_ref[0, 0, j:j + 1, :] = pair(pw_last[0][0], pw_last[1][0])[0:1]
        decay_ref[0, 1, j:j + 1, :] = pair(pw_last[0][1], pw_last[1][1])[0:1]

        for a, s in enumerate(step):
            win_ref[0, j, rows(a), :LANES] = pair(xs[0, CHUNK - 1 - s][0], xs[1, s][0]).astype(BF16)
            win_ref[0, j, rows(a), LANES:] = pair(xs[0, CHUNK - 1 - s][1], xs[1, s][1]).astype(BF16)

        z_re = jnp.concatenate([pair(ys[0, t + 1][0], ys[1, CHUNK - t][0]) for t in step], axis=0)
        z_nim = jnp.concatenate([pair(-ys[0, t + 1][1], -ys[1, CHUNK - t][1]) for t in step], axis=0)
        wout_ref[0, j, :LANES, :] = z_re.T.astype(BF16)
        wout_ref[0, j, LANES:, :] = z_nim.T.astype(BF16)

        kf = lax.dot_general(
            pair(bbs[0][0], -bbs[0][1]),
            jnp.concatenate([pair(*ys[0, m]) for m in range(CHUNK)], axis=0),
            nt, precision=lax.Precision.HIGHEST, preferred_element_type=F32)
        kb = lax.dot_general(
            pair(bbs[1][0], -bbs[1][1]),
            jnp.concatenate([pair(*ys[1, CHUNK - 1 - i]) for i in range(CHUNK)], axis=0),
            nt, precision=lax.Precision.HIGHEST, preferred_element_type=F32)
        for a, s in enumerate(step):
            fwd = pltpu.roll(kf, SSM_GROUP * s, axis=1) if s else kf
            shift_b = (n - SSM_GROUP * (CHUNK - 1 - s)) % n
            bwd = pltpu.roll(kb, shift_b, axis=1) if shift_b else kb
            t_nat = (jnp.where(lane_n >= SSM_GROUP * s, fwd, 0.0)
                     + jnp.where(lane_n < SSM_GROUP * (s + 1), bwd, 0.0))
            for h in range(2):
                half = t_nat[:, h * LANES:(h + 1) * LANES]
                if j:
                    half = pltpu.roll(half, SSM_GROUP * j, axis=1)
                toep_ref[0, j, rows(a), h * LANES:(h + 1) * LANES] = half.astype(BF16)


def _ssm_prep(lam_re, lam_im, log_dt, b_re, b_im, c_re, c_im):
    depth = lam_re.shape[0]
    G, P, C = SSM_GROUPS, SSM_STATE, SSM_GROUP
    n = CHUNK * C
    gb = PREP_GROUPS
    twice = lambda a: jnp.concatenate([a, a], axis=-1)
    rep = lambda a: jnp.broadcast_to(twice(a.astype(F32))[:, :, :, None, :], (depth, 2, G, C, LANES))
    ldt = jnp.broadcast_to(log_dt.astype(F32)[:, :, :, None, None], (depth, 2, G, C, LANES))
    bt = lambda a: twice(a.astype(F32).transpose(0, 1, 2, 4, 3))
    spec_in = pl.BlockSpec((1, 2, gb, C, LANES), lambda l, i: (l, 0, i, 0, 0))
    spec_w = pl.BlockSpec((1, gb, n, n), lambda l, i: (l, i, 0, 0))
    shp_w = jax.ShapeDtypeStruct((depth, G, n, n), BF16)
    return pl.pallas_call(
        _ssm_prep_kernel,
        grid=(depth, G // gb),
        in_specs=[spec_in] * 7,
        out_specs=[spec_w, spec_w, spec_w, pl.BlockSpec((1, 2, gb, LANES), lambda l, i: (l, 0, i, 0))],
        out_shape=[shp_w, shp_w, shp_w, jax.ShapeDtypeStruct((depth, 2, G, LANES), F32)],
        compiler_params=pltpu.CompilerParams(
            dimension_semantics=("arbitrary", "arbitrary"), vmem_limit_bytes=VMEM_LIMIT),
        name="ssm_prep",
    )(rep(lam_re), rep(lam_im), ldt, bt(b_re), bt(b_im), twice(c_re.astype(F32)), twice(c_im.astype(F32)))


GROUPS_PER_SLAB = LANES // SSM_GROUP
N_SLABS = W_BRANCH // LANES
RELAYOUT_ROWS = 32


def _ssm_kernel(xa_ref, toep_ref, win_ref, wout_ref, decay_ref, d_ref, o_ref, xf, u_s, st, hs, yq):
    seq = xa_ref.shape[0]
    nchunk = seq // CHUNK
    nrb = nchunk // RELAYOUT_ROWS
    gps = GROUPS_PER_SLAB
    lane_rb = lax.broadcasted_iota(jnp.int32, (RELAYOUT_ROWS, LANES), 1)
    seg_masks = [(lane_rb >= SSM_GROUP * sg) & (lane_rb < SSM_GROUP * (sg + 1)) for sg in range(gps)]
    conv_rows = 512

    for q in range(N_SLABS):
        c_lo = q * LANES

        def conv(i, carry, c_lo=c_lo):
            r0 = pl.multiple_of(i * conv_rows, conv_rows)
            xf[pl.ds(r0, conv_rows), :] = xa_ref[pl.ds(r0, conv_rows), c_lo:c_lo + LANES].astype(F32)
            return carry

        lax.fori_loop(0, seq // conv_rows, conv, 0)

        for hh in range(2):
            def fwd_relayout(rb, carry, hh=hh):
                c0 = pl.multiple_of(rb * RELAYOUT_ROWS, RELAYOUT_ROWS)
                rolled = []
                for r in range(gps):
                    xs = xf[pl.ds(CHUNK * c0 + gps * hh + r, RELAYOUT_ROWS, stride=CHUNK), :]
                    rolled.append(xs if r == 0 else pltpu.roll(xs, SSM_GROUP * r, axis=1))
                for j in range(gps):
                    out = rolled[(0 - j) % gps]
                    for sg in range(1, gps):
                        out = jnp.where(seg_masks[sg], rolled[(sg - j) % gps], out)
                    u_s[j, pl.ds(c0, RELAYOUT_ROWS), hh * LANES:(hh + 1) * LANES] = out.astype(BF16)
                return carry

            lax.fori_loop(0, nrb, fwd_relayout, 0, unroll=2)

        for j in range(gps):
            s_in = jnp.dot(u_s[j], win_ref[gps * q + j], preferred_element_type=F32)
            st[0, pl.ds(j, nchunk, stride=gps), :] = s_in[:, :LANES]
            st[1, pl.ds(j, nchunk, stride=gps), :] = s_in[:, LANES:]

        g_lo = gps * q
        a_re = decay_ref[0, g_lo:g_lo + gps, :]
        a_im = decay_ref[1, g_lo:g_lo + gps, :]
        lo, hi = slice(0, SSM_STATE), slice(SSM_STATE, LANES)

        def scan(i, carry):
            hrf, hif, hrb, hib = carry
            rf = pl.ds(pl.multiple_of(i * gps, gps), gps)
            rb_ = pl.ds(pl.multiple_of((nchunk - 1 - i) * gps, gps), gps)
            hs[0, rf, lo] = hrf[:, lo]
            hs[1, rf, lo] = hif[:, lo]
            hs[0, rb_, hi] = hrb[:, hi]
            hs[1, rb_, hi] = hib[:, hi]
            return (a_re * hrf - a_im * hif + st[0, rf, :], a_re * hif + a_im * hrf + st[1, rf, :],
                    a_re * hrb - a_im * hib + st[0, rb_, :], a_re * hib + a_im * hrb + st[1, rb_, :])

        zero = jnp.zeros((gps, LANES), F32)
        lax.fori_loop(0, nchunk, scan, (zero, zero, zero, zero), unroll=8)

        for j in range(gps):
            g = gps * q + j
            h_in = jnp.concatenate([hs[0, pl.ds(j, nchunk, stride=gps), :].astype(BF16),
                                    hs[1, pl.ds(j, nchunk, stride=gps), :].astype(BF16)], axis=-1)
            yq[j] = (jnp.dot(u_s[j], toep_ref[g], preferred_element_type=F32)
                     + jnp.dot(h_in, wout_ref[g], preferred_element_type=F32))

        d_row = d_ref[:, c_lo:c_lo + LANES]
        for hh in range(2):
            def bwd_relayout(rb, carry, hh=hh, d_row=d_row, q=q):
                c0 = pl.multiple_of(rb * RELAYOUT_ROWS, RELAYOUT_ROWS)
                ys = [yq[j, pl.ds(c0, RELAYOUT_ROWS), hh * LANES:(hh + 1) * LANES] for j in range(gps)]
                for r in range(gps):
                    merged = ys[(0 - r) % gps]
                    for sg in range(1, gps):
                        merged = jnp.where(seg_masks[sg], ys[(sg - r) % gps], merged)
                    if r:
                        merged = pltpu.roll(merged, LANES - SSM_GROUP * r, axis=1)
                    tok = pl.ds(CHUNK * c0 + gps * hh + r, RELAYOUT_ROWS, stride=CHUNK)
                    o_ref[q, tok, :] = merged + d_row * xf[tok, :]
                return carry

            lax.fori_loop(0, nrb, bwd_relayout, 0, unroll=2)


def _ssm(proj, toep, w_in, w_out, decay, d_skip, layer, batch, seq):
    nchunk = seq // CHUNK
    n = CHUNK * SSM_GROUP
    wspec = pl.BlockSpec((None, SSM_GROUPS, n, n), lambda b: (layer, 0, 0, 0), pipeline_mode=pl.Buffered(1))
    return pl.pallas_call(
        _ssm_kernel,
        grid=(batch,),
        in_specs=[
            pl.BlockSpec((seq, W_BRANCH), lambda b: (b, COL_XA // N_SLABS)),
            wspec, wspec, wspec,
            pl.BlockSpec((None, 2, SSM_GROUPS, LANES), lambda b: (layer, 0, 0, 0)),
            pl.BlockSpec((None, 1, W_BRANCH), lambda b: (layer, 0, 0)),
        ],
        out_specs=pl.BlockSpec((N_SLABS, seq, LANES), lambda b: (0, b, 0)),
        out_shape=jax.ShapeDtypeStruct((N_SLABS, batch * seq, LANES), F32),
        scratch_shapes=[
            pltpu.VMEM((seq, LANES), F32),
            pltpu.VMEM((GROUPS_PER_SLAB, nchunk, n), BF16),
            pltpu.VMEM((2, nchunk * GROUPS_PER_SLAB, LANES), F32),
            pltpu.VMEM((2, nchunk * GROUPS_PER_SLAB, LANES), F32),
            pltpu.VMEM((GROUPS_PER_SLAB, nchunk, n), F32),
        ],
        compiler_params=pltpu.CompilerParams(
            dimension_semantics=("arbitrary",), vmem_limit_bytes=VMEM_LIMIT),
        name="ssm_mix",
    )(proj, toep, w_in, w_out, decay, d_skip)


OUT_TM = 512


def _out_proj_kernel(ya_ref, za_ref, yb_ref, yc_ref, x_ref, w_ref, gw_ref, gb_ref, fg_ref, o_ref, *, final):
    y = jnp.concatenate([ya_ref[i] for i in range(N_SLABS)], axis=-1)
    g = _gelu_tanh(y)
    gate = jnp.dot(g.astype(BF16), gw_ref[...], preferred_element_type=F32) + gb_ref[...]
    ya = g * (1.0 / (1.0 + jnp.exp(-gate))) * _silu(za_ref[...].astype(F32))
    delta = (jnp.dot(ya.astype(BF16), w_ref[:W_BRANCH, :], preferred_element_type=F32)
             + jnp.dot(yb_ref[...], w_ref[W_BRANCH:2 * W_BRANCH, :], preferred_element_type=F32)
             + jnp.dot(yc_ref[...], w_ref[2 * W_BRANCH:, :], preferred_element_type=F32))
    x = x_ref[...] + delta
    if final:
        ms = jnp.mean(x * x, axis=-1, keepdims=True)
        x = x * lax.rsqrt(ms + RMS_EPS) * fg_ref[...]
    o_ref[...] = x


def _out_proj(ya_pre, proj, yb, yc, x2d, w_bf16, glu_w_bf16, glu_b, final_g, layer, final):
    rows = x2d.shape[0]
    row_blk = lambda width: pl.BlockSpec((OUT_TM, width), lambda i: (i, 0))
    const = lambda shape: pl.BlockSpec((None,) + shape, lambda i: (layer,) + (0,) * len(shape),
                                       pipeline_mode=pl.Buffered(1))
    return pl.pallas_call(
        functools.partial(_out_proj_kernel, final=final),
        grid=(rows // OUT_TM,),
        in_specs=[
            pl.BlockSpec((N_SLABS, OUT_TM, LANES), lambda i: (0, i, 0)),
            pl.BlockSpec((OUT_TM, W_BRANCH), lambda i: (i, COL_ZA // N_SLABS)),
            row_blk(W_BRANCH), row_blk(W_BRANCH), row_blk(D_MODEL),
            const((MIX_WIDTH, D_MODEL)), const((W_BRANCH, W_BRANCH)),
            const((1, W_BRANCH)),
            pl.BlockSpec((None, 1, D_MODEL), lambda i: (0, 0, 0), pipeline_mode=pl.Buffered(1)),
        ],
        out_specs=row_blk(D_MODEL),
        out_shape=jax.ShapeDtypeStruct((rows, D_MODEL), F32),
        compiler_params=pltpu.CompilerParams(
            dimension_semantics=("arbitrary",), vmem_limit_bytes=VMEM_LIMIT),
        name="out_proj_final" if final else "out_proj",
    )(ya_pre, proj, yb, yc, x2d, w_bf16, glu_w_bf16, glu_b, final_g)


def kernel(x, norm_g, w_in, w_out, ssm_lam_re, ssm_lam_im, ssm_log_dt, ssm_b_re, ssm_b_im, ssm_c_re,
           ssm_c_im, ssm_d, glu_w, glu_b, na_rpb, t5_bias, final_g):
    batch, seq, _ = x.shape
    depth = w_in.shape[0]
    x2d = x.astype(F32).reshape(batch * seq, D_MODEL)
    toep, s_in, s_out, decay = _ssm_prep(ssm_lam_re, ssm_lam_im, ssm_log_dt, ssm_b_re, ssm_b_im,
                                         ssm_c_re, ssm_c_im)
    dil_bias = _dil_bias_table(t5_bias)
    na_bias = _na_bias_table(na_rpb)
    w_in_bf, w_out_bf, glu_w_bf = w_in.astype(BF16), w_out.astype(BF16), glu_w.astype(BF16)
    row3 = lambda a: a.astype(F32).reshape(a.shape[0], 1, a.shape[1])
    norm_g3, ssm_d3, glu_b3 = row3(norm_g), row3(ssm_d), row3(glu_b)
    final_g3 = final_g.astype(F32).reshape(1, 1, D_MODEL)
    for l in range(depth):
        proj = _in_proj(x2d, norm_g3, w_in_bf, l)
        ya_pre = _ssm(proj, toep, s_in, s_out, decay, ssm_d3, l, batch, seq)
        yb = _na(proj, na_bias, l, batch, seq)
        yc = _dil(proj, dil_bias, batch, seq)
        x2d = _out_proj(ya_pre, proj, yb, yc, x2d, w_out_bf, glu_w_bf, glu_b3, final_g3, l,
                        final=(l == depth - 1))
    return x2d.reshape(batch, seq, D_MODEL).astype(x.dtype)
```

```python
import functools

import numpy as np
import jax
import jax.numpy as jnp
from jax import lax
from jax.experimental import pallas as pl
from jax.experimental.pallas import tpu as pltpu

F32 = jnp.float32
BF16 = jnp.bfloat16

D_MODEL = 1024
HEAD_DIM = 64
W_BRANCH = 512
N_HEADS = W_BRANCH // HEAD_DIM
N_HEAD_PAIRS = N_HEADS // 2
SSM_GROUP = 16
SSM_GROUPS = W_BRANCH // SSM_GROUP
SSM_STATE = 64
GRID_W = 64
NA_ROWS = 8
NA_COLS = 16
DIL_PATTERNS = ((128, 1), (512, 4), (2048, 16))
DIL_HALF = 64
T5_BUCKETS = 32
T5_MAX_DIST = 1024
RMS_EPS = 1e-6
NEG_INF = -1e30
IN_COLS = 10 * W_BRANCH
MIX_WIDTH = 3 * W_BRANCH

COL_XA, COL_ZA, COL_QB, COL_KB, COL_VB, COL_ZB, COL_QC, COL_KC, COL_VC, COL_ZC = (
    4 * i for i in range(10))

LOG2E = float(np.log2(np.e))
Q_SCALE_LOG2 = HEAD_DIM ** -0.5 * LOG2E

LANES = 128
CHUNK = 16
VMEM_LIMIT = 56 * 1024 * 1024


def _sigmoid(z):
    return 0.5 * (1.0 + jnp.tanh(0.5 * z))


def _silu(z):
    return z * _sigmoid(z)


def _toeplitz_rows(vec, n_rows, n_cols):
    m = vec.shape[-1]
    assert m >= n_rows + n_cols - 1
    tiled = jnp.tile(vec, (1,) * (vec.ndim - 1) + (n_rows,))[..., :n_rows * (m - 1)]
    return tiled.reshape(vec.shape[:-1] + (n_rows, m - 1))[..., :n_cols]


def _gelu_tanh(x):
    return 0.5 * x * (1.0 + jnp.tanh(np.sqrt(2.0 / np.pi).astype(np.float32) * (x + 0.044715 * (x * x * x))))


IN_TM = 512
IN_TN = 512


def _in_proj_kernel(x_ref, g_ref, w_ref, o_ref, w_bf):
    @pl.when(pl.program_id(0) == 0)
    def _():
        for n in range(IN_COLS // IN_TN):
            w_bf[:, n * IN_TN:(n + 1) * IN_TN] = w_ref[:, n * IN_TN:(n + 1) * IN_TN].astype(BF16)

    x = x_ref[...]
    ms = jnp.mean(x * x, axis=-1, keepdims=True)
    h = (x * lax.rsqrt(ms + RMS_EPS) * g_ref[...]).astype(BF16)
    for n in range(IN_COLS // IN_TN):
        o_ref[:, n * IN_TN:(n + 1) * IN_TN] = jnp.dot(
            h, w_bf[:, n * IN_TN:(n + 1) * IN_TN], preferred_element_type=F32).astype(BF16)


def _in_proj(x2d, g, w, layer):
    rows = x2d.shape[0]
    return pl.pallas_call(
        _in_proj_kernel,
        grid=(rows // IN_TM,),
        in_specs=[
            pl.BlockSpec((IN_TM, D_MODEL), lambda i: (i, 0)),
            pl.BlockSpec((None, 1, D_MODEL), lambda i: (layer, 0, 0)),
            pl.BlockSpec((None, D_MODEL, IN_COLS), lambda i: (layer, 0, 0), pipeline_mode=pl.Buffered(1)),
        ],
        out_specs=pl.BlockSpec((IN_TM, IN_COLS), lambda i: (i, 0)),
        out_shape=jax.ShapeDtypeStruct((rows, IN_COLS), BF16),
        scratch_shapes=[pltpu.VMEM((D_MODEL, IN_COLS), BF16)],
        compiler_params=pltpu.CompilerParams(
            dimension_semantics=("arbitrary",), vmem_limit_bytes=VMEM_LIMIT),
        name="in_proj",
    )(x2d, g, w)


def _na_bias_kernel(rpb_ref, o_ref):
    lane = lax.broadcasted_iota(jnp.int32, (GRID_W, LANES), 1)
    j = lax.broadcasted_iota(jnp.int32, (GRID_W, LANES), 0)
    c = lane & (GRID_W - 1)
    col_start = jnp.clip(j - NA_COLS // 2, 0, GRID_W - NA_COLS)
    valid = (c >= col_start) & (c < col_start + NA_COLS)
    first = lane < GRID_W
    even, odd = [], []
    for a in range(2 * NA_ROWS - 1):
        vec = pltpu.roll(rpb_ref[a:a + 1, :], LANES - (NA_COLS - 1), axis=1)
        x = jnp.broadcast_to(vec, (GRID_W, LANES))
        even.append(pltpu.roll(x, 0, axis=1, stride=1, stride_axis=0))
        odd.append(pltpu.roll(x, GRID_W, axis=1, stride=1, stride_axis=0))
    for v in range(NA_ROWS):
        for i in range(NA_ROWS // 2):
            a0 = 2 * i - v + (NA_ROWS - 1)
            tile = jnp.where(first, even[a0], odd[a0 + 1])
            o_ref[v, :, i * LANES:(i + 1) * LANES] = jnp.where(valid, tile * LOG2E, NEG_INF)


def _na_bias_table(rpb):
    depth = rpb.shape[0]
    padded = jnp.pad(rpb.astype(F32), ((0, 0), (0, 0), (0, 1), (0, LANES - (2 * NA_COLS - 1))))
    nkeys = NA_ROWS * GRID_W
    return pl.pallas_call(
        _na_bias_kernel,
        grid=(depth, N_HEADS),
        in_specs=[pl.BlockSpec((None, None, 2 * NA_ROWS, LANES), lambda l, h: (l, h, 0, 0))],
        out_specs=pl.BlockSpec((None, None, NA_ROWS, GRID_W, nkeys), lambda l, h: (l, h, 0, 0, 0)),
        out_shape=jax.ShapeDtypeStruct((depth, N_HEADS, NA_ROWS, GRID_W, nkeys), F32),
        compiler_params=pltpu.CompilerParams(dimension_semantics=("arbitrary", "arbitrary")),
        name="na_bias",
    )(padded)


def _na_kernel(q_ref, k_ref, v_ref, z_ref, bias_ref, o_ref, s_scr, mx_scr, p_scr):
    seq = q_ref.shape[0]
    rows = seq // GRID_W
    nkeys = NA_ROWS * GRID_W
    lane = lax.broadcasted_iota(jnp.int32, (GRID_W, LANES), 1)
    first_head = lane < HEAD_DIM
    nt = (((1,), (1,)), ((), ()))

    def window(r):
        rs = jnp.clip(r - NA_ROWS // 2, 0, rows - NA_ROWS)
        return r - rs, pl.multiple_of(r * GRID_W, GRID_W), pl.multiple_of(rs * GRID_W, GRID_W)

    def group(gi, carry):
        r_lo = gi * NA_GROUP

        def logits(i, c):
            var, q0, k0 = window(r_lo + i)
            qb = (q_ref[pl.ds(q0, GRID_W), :].astype(F32) * Q_SCALE_LOG2).astype(BF16)
            kw = k_ref[pl.ds(k0, nkeys), :]
            for hd in range(2):
                keep = first_head if hd == 0 else jnp.logical_not(first_head)
                qm = jnp.where(keep, qb, jnp.zeros_like(qb))
                s = lax.dot_general(qm, kw, nt, preferred_element_type=F32) + bias_ref[hd, var]
                s_scr[2 * i + hd] = s
                mx_scr[2 * i + hd] = jnp.broadcast_to(jnp.max(s, axis=-1, keepdims=True), (GRID_W, LANES))
            return c

        lax.fori_loop(0, NA_GROUP, logits, 0, unroll=True)

        def probs(t, c):
            m = mx_scr[t]
            psum = None
            for kc in range(nkeys // LANES):
                sl = slice(kc * LANES, (kc + 1) * LANES)
                p = jnp.exp2(s_scr[t, :, sl] - m)
                p_scr[t, :, sl] = p.astype(BF16)
                psum = p if psum is None else psum + p
            mx_scr[t] = jnp.broadcast_to(jnp.sum(psum, axis=-1, keepdims=True), (GRID_W, LANES))
            return c

        lax.fori_loop(0, 2 * NA_GROUP, probs, 0, unroll=True)

        def outputs(i, c):
            _, q0, k0 = window(r_lo + i)
            vw = v_ref[pl.ds(k0, nkeys), :]
            r0 = jnp.dot(p_scr[2 * i], vw, preferred_element_type=F32)
            r1 = jnp.dot(p_scr[2 * i + 1], vw, preferred_element_type=F32)
            acc = jnp.where(first_head, r0, r1)
            den = jnp.where(first_head, mx_scr[2 * i], mx_scr[2 * i + 1])
            z = z_ref[pl.ds(q0, GRID_W), :].astype(F32)
            o_ref[pl.ds(q0, GRID_W), :] = (acc * (1.0 / den) * _silu(z)).astype(BF16)
            return c

        lax.fori_loop(0, NA_GROUP, outputs, 0, unroll=True)
        return carry

    lax.fori_loop(0, rows // NA_GROUP, group, 0)


NA_GROUP = 16


def _na(proj, bias, layer, batch, seq):
    def col(c0):
        return pl.BlockSpec((seq, LANES), lambda b, p, c0=c0: (b, c0 + p))

    nkeys = NA_ROWS * GRID_W
    scratch = [pltpu.VMEM((2 * NA_GROUP, GRID_W, nkeys), F32),
               pltpu.VMEM((2 * NA_GROUP, GRID_W, LANES), F32),
               pltpu.VMEM((2 * NA_GROUP, GRID_W, nkeys), BF16)]

    return pl.pallas_call(
        _na_kernel,
        grid=(batch, N_HEAD_PAIRS),
        in_specs=[
            col(COL_QB), col(COL_KB), col(COL_VB), col(COL_ZB),
            pl.BlockSpec((None, 2, NA_ROWS, GRID_W, NA_ROWS * GRID_W), lambda b, p: (layer, p, 0, 0, 0)),
        ],
        out_specs=pl.BlockSpec((seq, LANES), lambda b, p: (b, p)),
        out_shape=jax.ShapeDtypeStruct((batch * seq, W_BRANCH), BF16),
        scratch_shapes=scratch,
        compiler_params=pltpu.CompilerParams(
            dimension_semantics=("arbitrary", "arbitrary"), vmem_limit_bytes=VMEM_LIMIT),
        name="na_attn",
    )(proj, proj, proj, proj, bias)


DIL_QB = 128
DIL_KW = 256
N_VARIANTS = 3


def _t5_bucket(rel):
    nb = T5_BUCKETS // 2
    max_exact = nb // 2
    n = np.abs(rel)
    large = max_exact + (np.log(np.maximum(n, 1) / max_exact) / np.log(T5_MAX_DIST / max_exact)
                         * (nb - max_exact)).astype(np.int32)
    large = np.minimum(large, nb - 1)
    return (np.where(rel > 0, nb, 0) + np.where(n < max_exact, n, large)).astype(np.int32)


def _dil_bias_table(t5_bias):
    n_pat = len(DIL_PATTERNS)
    lane = np.arange(DIL_VEC)
    n = np.where(lane < DIL_KW, lane, lane - DIL_VEC)
    in_range = (lane < DIL_KW) | (lane > DIL_VEC - DIL_QB)
    pick = np.zeros((n_pat, N_VARIANTS, T5_BUCKETS, DIL_VEC), np.float32)
    neg = np.zeros((n_pat, N_VARIANTS, 1, DIL_VEC), np.float32)
    for pi, (_, d) in enumerate(DIL_PATTERNS):
        for var in range(N_VARIANTS):
            step = n - DIL_HALF * var
            ok = in_range & (np.abs(step) <= DIL_HALF)
            bucket = _t5_bucket(d * np.clip(step, -DIL_HALF, DIL_HALF))
            pick[pi, var, bucket[ok], lane[ok]] = 1.0
            neg[pi, var, 0, ~ok] = NEG_INF
    vecs = jnp.einsum('bh,pvbn->hpvn', t5_bias.astype(F32), pick,
                      precision=lax.Precision.HIGHEST) * LOG2E + neg[:, :, 0][None]
    return pl.pallas_call(
        _dil_bias_kernel,
        grid=(N_HEADS,),
        in_specs=[pl.BlockSpec((None, n_pat * N_VARIANTS, DIL_VEC), lambda h: (h, 0, 0))],
        out_specs=pl.BlockSpec((None, n_pat, N_VARIANTS, DIL_QB, DIL_KW), lambda h: (h, 0, 0, 0, 0)),
        out_shape=jax.ShapeDtypeStruct((N_HEADS, n_pat, N_VARIANTS, DIL_QB, DIL_KW), F32),
        compiler_params=pltpu.CompilerParams(dimension_semantics=("arbitrary",)),
        name="dil_bias",
    )(vecs.reshape(N_HEADS, n_pat * N_VARIANTS, DIL_VEC))


DIL_VEC = 512


def _dil_bias_kernel(vec_ref, o_ref):
    for pi in range(len(DIL_PATTERNS)):
        for var in range(N_VARIANTS):
            i = pi * N_VARIANTS + var
            x = jnp.broadcast_to(vec_ref[i:i + 1, :], (DIL_QB, DIL_VEC))
            o_ref[pi, var] = pltpu.roll(x, 0, axis=1, stride=1, stride_axis=0)[:, :DIL_KW]


DIL_GROUP = 8
DIL_GROUP_FIRST = 16


def _dil_kernel(q_ref, k_ref, v_ref, z_ref, bias_ref, o_ref, qf, kf, vf, qg, kg, vg, qd, kd, vd,
                m_s, l_s, acc_s):
    seq = q_ref.shape[0]
    conv_rows = 256
    lane = lax.broadcasted_iota(jnp.int32, (DIL_QB, LANES), 1)
    first_head = lane < HEAD_DIM
    nt = (((1,), (1,)), ((), ()))
    ones = jnp.ones((DIL_KW, LANES), BF16)

    def conv(i, carry):
        r0 = pl.multiple_of(i * conv_rows, conv_rows)
        qf[pl.ds(r0, conv_rows), :] = q_ref[pl.ds(r0, conv_rows), :].astype(F32) * Q_SCALE_LOG2
        kf[pl.ds(r0, conv_rows), :] = k_ref[pl.ds(r0, conv_rows), :].astype(F32)
        vf[pl.ds(r0, conv_rows), :] = v_ref[pl.ds(r0, conv_rows), :].astype(F32)
        return carry

    lax.fori_loop(0, seq // conv_rows, conv, 0)

    n4 = seq // 4
    n16 = seq // 16

    def to_order4(i, carry):
        dst = pl.multiple_of(i * conv_rows, conv_rows)
        src = pl.ds(dst // n4 + 4 * (dst % n4), conv_rows, stride=4)
        for a, b in ((qf, qg), (kf, kg), (vf, vg)):
            b[pl.ds(dst, conv_rows), :] = a[src, :]
        return carry

    lax.fori_loop(0, seq // conv_rows, to_order4, 0, unroll=2)

    def order16_rows_in_order4(rho):
        return pl.ds((rho % 4) * n4 + rho // 4, n16, stride=4)

    def to_order16(rho, carry):
        src = order16_rows_in_order4(rho)
        dst = pl.ds(pl.multiple_of(rho * n16, n16), n16)
        for a, b in ((qg, qd), (kg, kd), (vg, vd)):
            b[dst, :] = a[src, :].astype(BF16)
        return carry

    lax.fori_loop(0, 16, to_order16, 0, unroll=2)

    plan = ((2, 16, (qd, kd, vd), (m_s, l_s, acc_s)),
            (1, 4, (qd, kd, vd), (qf, kf, vf)),
            (0, 1, (qd, k_ref, v_ref), (m_s, l_s, acc_s)))

    for step, (pi, d, (q_src, k_src, v_src), (m_st, l_st, acc_st)) in enumerate(plan):
        n_sub = seq // d
        nblk = n_sub // DIL_QB

        if step == 1:
            def state_to_order4(rho, carry):
                dst = order16_rows_in_order4(rho)
                src = pl.ds(pl.multiple_of(rho * n16, n16), n16)
                for a, b in ((m_s, qf), (l_s, kf), (acc_s, vf)):
                    b[dst, :] = a[src, :]
                return carry

            lax.fori_loop(0, 16, state_to_order4, 0, unroll=2)

            def cast_order4(i, carry):
                r0 = pl.multiple_of(i * conv_rows, conv_rows)
                for a, b in ((qg, qd), (kg, kd), (vg, vd)):
                    b[pl.ds(r0, conv_rows), :] = a[pl.ds(r0, conv_rows), :].astype(BF16)
                return carry

            lax.fori_loop(0, seq // conv_rows, cast_order4, 0)
        if step == 2:
            def state_to_tokens(i, carry):
                src = pl.multiple_of(i * conv_rows, conv_rows)
                dst = pl.ds(src // n4 + 4 * (src % n4), conv_rows, stride=4)
                for a, b in ((qf, m_s), (kf, l_s), (vf, acc_s)):
                    b[dst, :] = a[pl.ds(src, conv_rows), :]
                qd[pl.ds(src, conv_rows), :] = (
                    q_ref[pl.ds(src, conv_rows), :].astype(F32) * Q_SCALE_LOG2).astype(BF16)
                return carry

            lax.fori_loop(0, seq // conv_rows, state_to_tokens, 0, unroll=2)

        n_group = DIL_GROUP_FIRST if step == 0 else DIL_GROUP

        def group(gi, carry, pi=pi, n_sub=n_sub, nblk=nblk, first=(step == 0), q_src=q_src,
                  k_src=k_src, v_src=v_src, m_st=m_st, l_st=l_st, acc_st=acc_st, n_group=n_group):
            units = []
            for i in range(n_group):
                u = gi * n_group + i
                res = u // nblk
                i0 = (u % nblk) * DIL_QB
                ks = jnp.clip(i0 - DIL_HALF, 0, n_sub - DIL_KW)
                units.append((res, i0, ks, (i0 - ks) // DIL_HALF))
            s_tiles, m_tiles = [], []
            for res, i0, ks, var in units:
                qb = q_src[pl.ds(pl.multiple_of(res * n_sub + i0, DIL_HALF), DIL_QB), :]
                kw = k_src[pl.ds(pl.multiple_of(res * n_sub + ks, DIL_HALF), DIL_KW), :]
                for hd in range(2):
                    keep = first_head if hd == 0 else jnp.logical_not(first_head)
                    qm = jnp.where(keep, qb, jnp.zeros_like(qb))
                    s = lax.dot_general(qm, kw, nt, preferred_element_type=F32) + bias_ref[hd, pi, var]
                    s = s.astype(BF16)
                    s_tiles.append(s)
                    m_tiles.append(jnp.max(s, axis=-1, keepdims=True))
            p_tiles = [jnp.exp2(s - m) for s, m in zip(s_tiles, m_tiles)]
            for i, (res, i0, ks, var) in enumerate(units):
                vw = v_src[pl.ds(pl.multiple_of(res * n_sub + ks, DIL_HALF), DIL_KW), :]
                vext = jnp.concatenate([vw, ones], axis=-1)
                r0 = jnp.dot(p_tiles[2 * i], vext, preferred_element_type=F32)
                r1 = jnp.dot(p_tiles[2 * i + 1], vext, preferred_element_type=F32)
                m_cur = jnp.where(first_head, m_tiles[2 * i].astype(F32), m_tiles[2 * i + 1].astype(F32))
                l_cur = jnp.where(first_head, r0[:, LANES:], r1[:, LANES:])
                acc_cur = jnp.where(first_head, r0[:, :LANES], r1[:, :LANES])
                q_rows = pl.ds(pl.multiple_of(res * n_sub + i0, DIL_HALF), DIL_QB)
                if first:
                    m_st[q_rows, :] = m_cur
                    l_st[q_rows, :] = l_cur
                    acc_st[q_rows, :] = acc_cur
                else:
                    m_old = m_st[q_rows, :]
                    m_new = jnp.maximum(m_old, m_cur)
                    a_old = jnp.exp2(m_old - m_new)
                    a_cur = jnp.exp2(m_cur - m_new)
                    m_st[q_rows, :] = m_new
                    l_st[q_rows, :] = a_old * l_st[q_rows, :] + a_cur * l_cur
                    acc_st[q_rows, :] = a_old * acc_st[q_rows, :] + a_cur * acc_cur
            return carry

        lax.fori_loop(0, seq // DIL_QB // n_group, group, 0)

    def fin(i, carry):
        r0 = pl.multiple_of(i * conv_rows, conv_rows)
        z = z_ref[pl.ds(r0, conv_rows), :].astype(F32)
        y = acc_s[pl.ds(r0, conv_rows), :] * (1.0 / l_s[pl.ds(r0, conv_rows), :])
        o_ref[pl.ds(r0, conv_rows), :] = (y * _silu(z)).astype(BF16)
        return carry

    lax.fori_loop(0, seq // conv_rows, fin, 0, unroll=2)


def _dil(proj, bias, batch, seq):
    def col(c0):
        return pl.BlockSpec((seq, LANES), lambda b, p, c0=c0: (b, c0 + p))

    scratch = ([pltpu.VMEM((seq, LANES), F32) for _ in range(6)]
               + [pltpu.VMEM((seq, LANES), BF16) for _ in range(3)]
               + [pltpu.VMEM((seq, LANES), F32) for _ in range(3)])
    return pl.pallas_call(
        _dil_kernel,
        grid=(batch, N_HEAD_PAIRS),
        in_specs=[
            col(COL_QC), col(COL_KC), col(COL_VC), col(COL_ZC),
            pl.BlockSpec((2, len(DIL_PATTERNS), N_VARIANTS, DIL_QB, DIL_KW),
                         lambda b, p: (p, 0, 0, 0, 0)),
        ],
        out_specs=pl.BlockSpec((seq, LANES), lambda b, p: (b, p)),
        out_shape=jax.ShapeDtypeStruct((batch * seq, W_BRANCH), BF16),
        scratch_shapes=scratch,
        compiler_params=pltpu.CompilerParams(
            dimension_semantics=("arbitrary", "arbitrary"), vmem_limit_bytes=VMEM_LIMIT),
        name="dil_attn",
    )(proj, proj, proj, proj, bias)


N_POW = CHUNK + 1
N_LAGS = 2 * CHUNK - 1
PREP_GROUPS = 8


def _ssm_prep_kernel(lre_ref, lim_ref, ldt_ref, btre_ref, btim_ref, cre_ref, cim_ref,
                     toep_ref, win_ref, wout_ref, decay_ref):
    n = CHUNK * SSM_GROUP
    lane = lax.broadcasted_iota(jnp.int32, (SSM_GROUP, LANES), 1)
    lo = lane < SSM_STATE
    lane_n = lax.broadcasted_iota(jnp.int32, (SSM_GROUP, n), 1)
    nt = (((1,), (1,)), ((), ()))
    pair = lambda x, y: jnp.where(lo, x, y)

    for j in range(PREP_GROUPS):
        xs, ys, bbs, pw_last = {}, {}, {}, {}
        for dr in range(2):
            lam_re = lre_ref[0, dr, j]
            lam_im = lim_ref[0, dr, j]
            dt = jnp.exp(ldt_ref[0, dr, j])
            mag = jnp.exp(lam_re * dt)
            lb_re = mag * jnp.cos(lam_im * dt)
            lb_im = mag * jnp.sin(lam_im * dt)
            n_re = lb_re - 1.0
            den = lam_re * lam_re + lam_im * lam_im
            q_re = (n_re * lam_re + lb_im * lam_im) / den
            q_im = (lb_im * lam_re - n_re * lam_im) / den
            bt_re = btre_ref[0, dr, j]
            bt_im = btim_ref[0, dr, j]
            bb_re = q_re * bt_re - q_im * bt_im
            bb_im = q_re * bt_im + q_im * bt_re
            bbs[dr] = (bb_re, bb_im)
            c_re = cre_ref[0, dr, j]
            c_im = cim_ref[0, dr, j]
            pw_re = jnp.ones_like(lb_re)
            pw_im = jnp.zeros_like(lb_re)
            for k in range(N_POW):
                xs[dr, k] = (pw_re * bb_re - pw_im * bb_im, pw_re * bb_im + pw_im * bb_re)
                ys[dr, k] = (pw_re * c_re - pw_im * c_im, pw_re * c_im + pw_im * c_re)
                pw_last[dr] = (pw_re, pw_im)
                pw_re, pw_im = pw_re * lb_re - pw_im * lb_im, pw_re * lb_im + pw_im * lb_re

        step = [GROUPS_PER_SLAB * (a // GROUPS_PER_SLAB) + (a % GROUPS_PER_SLAB - j) % GROUPS_PER_SLAB
                for a in range(CHUNK)]
        rows = lambda a: slice(a * SSM_GROUP, (a + 1) * SSM_GROUP)

        decay_ref[0, 0, j:j + 1, :] = pair(pw_last[0][0], pw_last[1][0])[0:1]
        decay_ref[0, 1, j:j + 1, :] = pair(pw_last[0][1], pw_last[1][1])[0:1]

        for a, s in enumerate(step):
            win_ref[0, j, rows(a), :LANES] = pair(xs[0, CHUNK - 1 - s][0], xs[1, s][0]).astype(BF16)
            win_ref[0, j, rows(a), LANES:] = pair(xs[0, CHUNK - 1 - s][1], xs[1, s][1]).astype(BF16)

        z_re = jnp.concatenate([pair(ys[0, t + 1][0], ys[1, CHUNK - t][0]) for t in step], axis=0)
        z_nim = jnp.concatenate([pair(-ys[0, t + 1][1], -ys[1, CHUNK - t][1]) for t in step], axis=0)
        wout_ref[0, j, :LANES, :] = z_re.T.astype(BF16)
        wout_ref[0, j, LANES:, :] = z_nim.T.astype(BF16)

        kf = lax.dot_general(
            pair(bbs[0][0], -bbs[0][1]),
            jnp.concatenate([pair(*ys[0, m]) for m in range(CHUNK)], axis=0),
            nt, precision=lax.Precision.HIGHEST, preferred_element_type=F32)
        kb = lax.dot_general(
            pair(bbs[1][0], -bbs[1][1]),
            jnp.concatenate([pair(*ys[1, CHUNK - 1 - i]) for i in range(CHUNK)], axis=0),
            nt, precision=lax.Precision.HIGHEST, preferred_element_type=F32)
        for a, s in enumerate(step):
            fwd = pltpu.roll(kf, SSM_GROUP * s, axis=1) if s else kf
            shift_b = (n - SSM_GROUP * (CHUNK - 1 - s)) % n
            bwd = pltpu.roll(kb, shift_b, axis=1) if shift_b else kb
            t_nat = (jnp.where(lane_n >= SSM_GROUP * s, fwd, 0.0)
                     + jnp.where(lane_n < SSM_GROUP * (s + 1), bwd, 0.0))
            for h in range(2):
                half = t_nat[:, h * LANES:(h + 1) * LANES]
                if j:
                    half = pltpu.roll(half, SSM_GROUP * j, axis=1)
                toep_ref[0, j, rows(a), h * LANES:(h + 1) * LANES] = half.astype(BF16)


def _ssm_prep(lam_re, lam_im, log_dt, b_re, b_im, c_re, c_im):
    depth = lam_re.shape[0]
    G, P, C = SSM_GROUPS, SSM_STATE, SSM_GROUP
    n = CHUNK * C
    gb = PREP_GROUPS
    twice = lambda a: jnp.concatenate([a, a], axis=-1)
    rep = lambda a: jnp.broadcast_to(twice(a.astype(F32))[:, :, :, None, :], (depth, 2, G, C, LANES))
    ldt = jnp.broadcast_to(log_dt.astype(F32)[:, :, :, None, None], (depth, 2, G, C, LANES))
    bt = lambda a: twice(a.astype(F32).transpose(0, 1, 2, 4, 3))
    spec_in = pl.BlockSpec((1, 2, gb, C, LANES), lambda l, i: (l, 0, i, 0, 0))
    spec_w = pl.BlockSpec((1, gb, n, n), lambda l, i: (l, i, 0, 0))
    shp_w = jax.ShapeDtypeStruct((depth, G, n, n), BF16)
    return pl.pallas_call(
        _ssm_prep_kernel,
        grid=(depth, G // gb),
        in_specs=[spec_in] * 7,
        out_specs=[spec_w, spec_w, spec_w, pl.BlockSpec((1, 2, gb, LANES), lambda l, i: (l, 0, i, 0))],
        out_shape=[shp_w, shp_w, shp_w, jax.ShapeDtypeStruct((depth, 2, G, LANES), F32)],
        compiler_params=pltpu.CompilerParams(
            dimension_semantics=("arbitrary", "arbitrary"), vmem_limit_bytes=VMEM_LIMIT),
        name="ssm_prep",
    )(rep(lam_re), rep(lam_im), ldt, bt(b_re), bt(b_im), twice(c_re.astype(F32)), twice(c_im.astype(F32)))


GROUPS_PER_SLAB = LANES // SSM_GROUP
N_SLABS = W_BRANCH // LANES
RELAYOUT_ROWS = 32


def _ssm_kernel(xa_ref, toep_ref, win_ref, wout_ref, decay_ref, d_ref, o_ref, xf, u_s, st, hs, yq):
    seq = xa_ref.shape[0]
    nchunk = seq // CHUNK
    nrb = nchunk // RELAYOUT_ROWS
    gps = GROUPS_PER_SLAB
    lane_rb = lax.broadcasted_iota(jnp.int32, (RELAYOUT_ROWS, LANES), 1)
    seg_masks = [(lane_rb >= SSM_GROUP * sg) & (lane_rb < SSM_GROUP * (sg + 1)) for sg in range(gps)]
    conv_rows = 512

    for q in range(N_SLABS):
        c_lo = q * LANES

        def conv(i, carry, c_lo=c_lo):
            r0 = pl.multiple_of(i * conv_rows, conv_rows)
            xf[pl.ds(r0, conv_rows), :] = xa_ref[pl.ds(r0, conv_rows), c_lo:c_lo + LANES].astype(F32)
            return carry

        lax.fori_loop(0, seq // conv_rows, conv, 0)

        for hh in range(2):
            def fwd_relayout(rb, carry, hh=hh):
                c0 = pl.multiple_of(rb * RELAYOUT_ROWS, RELAYOUT_ROWS)
                rolled = []
                for r in range(gps):
                    xs = xf[pl.ds(CHUNK * c0 + gps * hh + r, RELAYOUT_ROWS, stride=CHUNK), :]
                    rolled.append(xs if r == 0 else pltpu.roll(xs, SSM_GROUP * r, axis=1))
                for j in range(gps):
                    out = rolled[(0 - j) % gps]
                    for sg in range(1, gps):
                        out = jnp.where(seg_masks[sg], rolled[(sg - j) % gps], out)
                    u_s[j, pl.ds(c0, RELAYOUT_ROWS), hh * LANES:(hh + 1) * LANES] = out.astype(BF16)
                return carry

            lax.fori_loop(0, nrb, fwd_relayout, 0, unroll=2)

        for j in range(gps):
            s_in = jnp.dot(u_s[j], win_ref[gps * q + j], preferred_element_type=F32)
            st[0, pl.ds(j, nchunk, stride=gps), :] = s_in[:, :LANES]
            st[1, pl.ds(j, nchunk, stride=gps), :] = s_in[:, LANES:]

        g_lo = gps * q
        a_re = decay_ref[0, g_lo:g_lo + gps, :]
        a_im = decay_ref[1, g_lo:g_lo + gps, :]
        lo, hi = slice(0, SSM_STATE), slice(SSM_STATE, LANES)

        def scan(i, carry):
            hrf, hif, hrb, hib = carry
            rf = pl.ds(pl.multiple_of(i * gps, gps), gps)
            rb_ = pl.ds(pl.multiple_of((nchunk - 1 - i) * gps, gps), gps)
            hs[0, rf, lo] = hrf[:, lo]
            hs[1, rf, lo] = hif[:, lo]
            hs[0, rb_, hi] = hrb[:, hi]
            hs[1, rb_, hi] = hib[:, hi]
            return (a_re * hrf - a_im * hif + st[0, rf, :], a_re * hif + a_im * hrf + st[1, rf, :],
                    a_re * hrb - a_im * hib + st[0, rb_, :], a_re * hib + a_im * hrb + st[1, rb_, :])

        zero = jnp.zeros((gps, LANES), F32)
        lax.fori_loop(0, nchunk, scan, (zero, zero, zero, zero), unroll=8)

        for j in range(gps):
            g = gps * q + j
            h_in = jnp.concatenate([hs[0, pl.ds(j, nchunk, stride=gps), :].astype(BF16),
                                    hs[1, pl.ds(j, nchunk, stride=gps), :].astype(BF16)], axis=-1)
            yq[j] = (jnp.dot(u_s[j], toep_ref[g], preferred_element_type=F32)
                     + jnp.dot(h_in, wout_ref[g], preferred_element_type=F32))

        d_row = d_ref[:, c_lo:c_lo + LANES]
        for hh in range(2):
            def bwd_relayout(rb, carry, hh=hh, d_row=d_row, q=q):
                c0 = pl.multiple_of(rb * RELAYOUT_ROWS, RELAYOUT_ROWS)
                ys = [yq[j, pl.ds(c0, RELAYOUT_ROWS), hh * LANES:(hh + 1) * LANES] for j in range(gps)]
                for r in range(gps):
                    merged = ys[(0 - r) % gps]
                    for sg in range(1, gps):
                        merged = jnp.where(seg_masks[sg], ys[(sg - r) % gps], merged)
                    if r:
                        merged = pltpu.roll(merged, LANES - SSM_GROUP * r, axis=1)
                    tok = pl.ds(CHUNK * c0 + gps * hh + r, RELAYOUT_ROWS, stride=CHUNK)
                    o_ref[q, tok, :] = merged + d_row * xf[tok, :]
                return carry

            lax.fori_loop(0, nrb, bwd_relayout, 0, unroll=2)


def _ssm(proj, toep, w_in, w_out, decay, d_skip, layer, batch, seq):
    nchunk = seq // CHUNK
    n = CHUNK * SSM_GROUP
    wspec = pl.BlockSpec((None, SSM_GROUPS, n, n), lambda b: (layer, 0, 0, 0), pipeline_mode=pl.Buffered(1))
    return pl.pallas_call(
        _ssm_kernel,
        grid=(batch,),
        in_specs=[
            pl.BlockSpec((seq, W_BRANCH), lambda b: (b, COL_XA // N_SLABS)),
            wspec, wspec, wspec,
            pl.BlockSpec((None, 2, SSM_GROUPS, LANES), lambda b: (layer, 0, 0, 0)),
            pl.BlockSpec((None, 1, W_BRANCH), lambda b: (layer, 0, 0)),
        ],
        out_specs=pl.BlockSpec((N_SLABS, seq, LANES), lambda b: (0, b, 0)),
        out_shape=jax.ShapeDtypeStruct((N_SLABS, batch * seq, LANES), F32),
        scratch_shapes=[
            pltpu.VMEM((seq, LANES), F32),
            pltpu.VMEM((GROUPS_PER_SLAB, nchunk, n), BF16),
            pltpu.VMEM((2, nchunk * GROUPS_PER_SLAB, LANES), F32),
            pltpu.VMEM((2, nchunk * GROUPS_PER_SLAB, LANES), F32),
            pltpu.VMEM((GROUPS_PER_SLAB, nchunk, n), F32),
        ],
        compiler_params=pltpu.CompilerParams(
            dimension_semantics=("arbitrary",), vmem_limit_bytes=VMEM_LIMIT),
        name="ssm_mix",
    )(proj, toep, w_in, w_out, decay, d_skip)


OUT_TM = 512


def _out_proj_kernel(ya_ref, za_ref, yb_ref, yc_ref, x_ref, w_ref, gw_ref, gb_ref, fg_ref, o_ref,
                     w_bf, gw_bf, *, final):
    @pl.when(pl.program_id(0) == 0)
    def _():
        for n in range(MIX_WIDTH // W_BRANCH):
            w_bf[n * W_BRANCH:(n + 1) * W_BRANCH, :] = w_ref[n * W_BRANCH:(n + 1) * W_BRANCH, :].astype(BF16)
        gw_bf[...] = gw_ref[...].astype(BF16)

    y = jnp.concatenate([ya_ref[i] for i in range(N_SLABS)], axis=-1)
    g = _gelu_tanh(y)
    gate = jnp.dot(g.astype(BF16), gw_bf[...], preferred_element_type=F32) + gb_ref[...]
    ya = g * _sigmoid(gate) * _silu(za_ref[...].astype(F32))
    delta = (jnp.dot(ya.astype(BF16), w_bf[:W_BRANCH, :], preferred_element_type=F32)
             + jnp.dot(yb_ref[...], w_bf[W_BRANCH:2 * W_BRANCH, :], preferred_element_type=F32)
             + jnp.dot(yc_ref[...], w_bf[2 * W_BRANCH:, :], preferred_element_type=F32))
    x = x_ref[...] + delta
    if final:
        ms = jnp.mean(x * x, axis=-1, keepdims=True)
        x = x * lax.rsqrt(ms + RMS_EPS) * fg_ref[...]
    o_ref[...] = x


def _out_proj(ya_pre, proj, yb, yc, x2d, w, glu_w, glu_b, final_g, layer, final):
    rows = x2d.shape[0]
    row_blk = lambda width: pl.BlockSpec((OUT_TM, width), lambda i: (i, 0))
    const = lambda shape: pl.BlockSpec((None,) + shape, lambda i: (layer,) + (0,) * len(shape),
                                       pipeline_mode=pl.Buffered(1))
    return pl.pallas_call(
        functools.partial(_out_proj_kernel, final=final),
        grid=(rows // OUT_TM,),
        in_specs=[
            pl.BlockSpec((N_SLABS, OUT_TM, LANES), lambda i: (0, i, 0)),
            pl.BlockSpec((OUT_TM, W_BRANCH), lambda i: (i, COL_ZA // N_SLABS)),
            row_blk(W_BRANCH), row_blk(W_BRANCH), row_blk(D_MODEL),
            const((MIX_WIDTH, D_MODEL)), const((W_BRANCH, W_BRANCH)),
            const((1, W_BRANCH)),
            pl.BlockSpec((None, 1, D_MODEL), lambda i: (0, 0, 0), pipeline_mode=pl.Buffered(1)),
        ],
        out_specs=row_blk(D_MODEL),
        out_shape=jax.ShapeDtypeStruct((rows, D_MODEL), F32),
        scratch_shapes=[pltpu.VMEM((MIX_WIDTH, D_MODEL), BF16), pltpu.VMEM((W_BRANCH, W_BRANCH), BF16)],
        compiler_params=pltpu.CompilerParams(
            dimension_semantics=("arbitrary",), vmem_limit_bytes=VMEM_LIMIT),
        name="out_proj_final" if final else "out_proj",
    )(ya_pre, proj, yb, yc, x2d, w, glu_w, glu_b, final_g)


def kernel(x, norm_g, w_in, w_out, ssm_lam_re, ssm_lam_im, ssm_log_dt, ssm_b_re, ssm_b_im, ssm_c_re,
           ssm_c_im, ssm_d, glu_w, glu_b, na_rpb, t5_bias, final_g):
    batch, seq, _ = x.shape
    depth = w_in.shape[0]
    x2d = x.astype(F32).reshape(batch * seq, D_MODEL)
    toep, s_in, s_out, decay = _ssm_prep(ssm_lam_re, ssm_lam_im, ssm_log_dt, ssm_b_re, ssm_b_im,
                                         ssm_c_re, ssm_c_im)
    dil_bias = _dil_bias_table(t5_bias)
    na_bias = _na_bias_table(na_rpb)
    w_in_f, w_out_f, glu_w_f = w_in.astype(F32), w_out.astype(F32), glu_w.astype(F32)
    row3 = lambda a: a.astype(F32).reshape(a.shape[0], 1, a.shape[1])
    norm_g3, ssm_d3, glu_b3 = row3(norm_g), row3(ssm_d), row3(glu_b)
    final_g3 = final_g.astype(F32).reshape(1, 1, D_MODEL)
    for l in range(depth):
        proj = _in_proj(x2d, norm_g3, w_in_f, l)
        ya_pre = _ssm(proj, toep, s_in, s_out, decay, ssm_d3, l, batch, seq)
        yb = _na(proj, na_bias, l, batch, seq)
        yc = _dil(proj, dil_bias, batch, seq)
        x2d = _out_proj(ya_pre, proj, yb, yc, x2d, w_out_f, glu_w_f, glu_b3, final_g3, l,
                        final=(l == depth - 1))
    return x2d.reshape(batch, seq, D_MODEL).astype(x.dtype)
```

```python
import functools

import numpy as np
import jax
import jax.numpy as jnp
from jax import lax
from jax.experimental import pallas as pl
from jax.experimental.pallas import tpu as pltpu

F32 = jnp.float32
BF16 = jnp.bfloat16

D_MODEL = 1024
HEAD_DIM = 64
W_BRANCH = 512
N_HEADS = W_BRANCH // HEAD_DIM
N_HEAD_PAIRS = N_HEADS // 2
SSM_GROUP = 16
SSM_GROUPS = W_BRANCH // SSM_GROUP
SSM_STATE = 64
GRID_W = 64
NA_ROWS = 8
NA_COLS = 16
DIL_PATTERNS = ((128, 1), (512, 4), (2048, 16))
DIL_HALF = 64
T5_BUCKETS = 32
T5_MAX_DIST = 1024
RMS_EPS = 1e-6
NEG_INF = -1e30
IN_COLS = 10 * W_BRANCH
MIX_WIDTH = 3 * W_BRANCH

COL_XA, COL_ZA, COL_QB, COL_KB, COL_VB, COL_ZB, COL_QC, COL_KC, COL_VC, COL_ZC = (
    4 * i for i in range(10))

LOG2E = float(np.log2(np.e))
Q_SCALE_LOG2 = HEAD_DIM ** -0.5 * LOG2E

LANES = 128
CHUNK = 16
VMEM_LIMIT = 56 * 1024 * 1024


def _sigmoid(z):
    return 0.5 * (1.0 + jnp.tanh(0.5 * z))


def _silu(z):
    return z * _sigmoid(z)


def _toeplitz_rows(vec, n_rows, n_cols):
    m = vec.shape[-1]
    assert m >= n_rows + n_cols - 1
    tiled = jnp.tile(vec, (1,) * (vec.ndim - 1) + (n_rows,))[..., :n_rows * (m - 1)]
    return tiled.reshape(vec.shape[:-1] + (n_rows, m - 1))[..., :n_cols]


def _gelu_tanh(x):
    return 0.5 * x * (1.0 + jnp.tanh(np.sqrt(2.0 / np.pi).astype(np.float32) * (x + 0.044715 * (x * x * x))))


IN_TM = 256
IN_TN = 512
DIL_QKV = 3 * W_BRANCH


def _in_proj_kernel(x_ref, g_ref, w_ref, o_ref, o4_ref, o16_ref, w_bf, stage, stage4):
    @pl.when(pl.program_id(0) == 0)
    def _():
        for n in range(IN_COLS // IN_TN):
            w_bf[:, n * IN_TN:(n + 1) * IN_TN] = w_ref[:, n * IN_TN:(n + 1) * IN_TN].astype(BF16)

    x = x_ref[...]
    ms = jnp.mean(x * x, axis=-1, keepdims=True)
    h = (x * lax.rsqrt(ms + RMS_EPS) * g_ref[...]).astype(BF16)
    first_dil = COL_QC * LANES // IN_TN
    for n in range(IN_COLS // IN_TN):
        res = jnp.dot(h, w_bf[:, n * IN_TN:(n + 1) * IN_TN], preferred_element_type=F32)
        o_ref[:, n * IN_TN:(n + 1) * IN_TN] = res.astype(BF16)
        c = n - first_dil
        if 0 <= c < DIL_QKV // IN_TN:
            if c == 0:
                res = res * Q_SCALE_LOG2
            for s in range(IN_TN // LANES):
                stage[s] = res[:, s * LANES:(s + 1) * LANES]
            n4, n16 = IN_TM // 4, IN_TM // 16
            for s in range(IN_TN // LANES):
                col = slice(c * IN_TN + s * LANES, c * IN_TN + (s + 1) * LANES)
                for r in range(4):
                    part = stage[s, pl.ds(r, n4, stride=4), :]
                    o4_ref[r, :, col] = part.astype(BF16)
                    stage4[s, r * n4:(r + 1) * n4, :] = part
                for r in range(16):
                    part = stage4[s, pl.ds((r % 4) * n4 + r // 4, n16, stride=4), :]
                    o16_ref[r, :, col] = part.astype(BF16)


def _in_proj(x2d, g, w, layer, batch, seq):
    rows = x2d.shape[0]
    per_batch = seq // IN_TM
    return pl.pallas_call(
        _in_proj_kernel,
        grid=(rows // IN_TM,),
        in_specs=[
            pl.BlockSpec((IN_TM, D_MODEL), lambda i: (i, 0)),
            pl.BlockSpec((None, 1, D_MODEL), lambda i: (layer, 0, 0)),
            pl.BlockSpec((None, D_MODEL, IN_COLS), lambda i: (layer, 0, 0), pipeline_mode=pl.Buffered(1)),
        ],
        out_specs=[
            pl.BlockSpec((IN_TM, IN_COLS), lambda i: (i, 0)),
            pl.BlockSpec((None, 4, IN_TM // 4, DIL_QKV), lambda i: (i // per_batch, 0, i % per_batch, 0)),
            pl.BlockSpec((None, 16, IN_TM // 16, DIL_QKV), lambda i: (i // per_batch, 0, i % per_batch, 0)),
        ],
        out_shape=[
            jax.ShapeDtypeStruct((rows, IN_COLS), BF16),
            jax.ShapeDtypeStruct((batch, 4, seq // 4, DIL_QKV), BF16),
            jax.ShapeDtypeStruct((batch, 16, seq // 16, DIL_QKV), BF16),
        ],
        scratch_shapes=[pltpu.VMEM((D_MODEL, IN_COLS), BF16),
                        pltpu.VMEM((IN_TN // LANES, IN_TM, LANES), F32),
                        pltpu.VMEM((IN_TN // LANES, IN_TM, LANES), F32)],
        compiler_params=pltpu.CompilerParams(
            dimension_semantics=("arbitrary",), vmem_limit_bytes=VMEM_LIMIT),
        name="in_proj",
    )(x2d, g, w)


def _na_bias_kernel(rpb_ref, o_ref):
    lane = lax.broadcasted_iota(jnp.int32, (GRID_W, LANES), 1)
    j = lax.broadcasted_iota(jnp.int32, (GRID_W, LANES), 0)
    c = lane & (GRID_W - 1)
    col_start = jnp.clip(j - NA_COLS // 2, 0, GRID_W - NA_COLS)
    valid = (c >= col_start) & (c < col_start + NA_COLS)
    first = lane < GRID_W
    even, odd = [], []
    for a in range(2 * NA_ROWS - 1):
        vec = pltpu.roll(rpb_ref[a:a + 1, :], LANES - (NA_COLS - 1), axis=1)
        x = jnp.broadcast_to(vec, (GRID_W, LANES))
        even.append(pltpu.roll(x, 0, axis=1, stride=1, stride_axis=0))
        odd.append(pltpu.roll(x, GRID_W, axis=1, stride=1, stride_axis=0))
    for v in range(NA_ROWS):
        for i in range(NA_ROWS // 2):
            a0 = 2 * i - v + (NA_ROWS - 1)
            tile = jnp.where(first, even[a0], odd[a0 + 1])
            o_ref[v, :, i * LANES:(i + 1) * LANES] = jnp.where(valid, tile * LOG2E, NEG_INF)


def _na_bias_table(rpb):
    depth = rpb.shape[0]
    padded = jnp.pad(rpb.astype(F32), ((0, 0), (0, 0), (0, 1), (0, LANES - (2 * NA_COLS - 1))))
    nkeys = NA_ROWS * GRID_W
    return pl.pallas_call(
        _na_bias_kernel,
        grid=(depth, N_HEADS),
        in_specs=[pl.BlockSpec((None, None, 2 * NA_ROWS, LANES), lambda l, h: (l, h, 0, 0))],
        out_specs=pl.BlockSpec((None, None, NA_ROWS, GRID_W, nkeys), lambda l, h: (l, h, 0, 0, 0)),
        out_shape=jax.ShapeDtypeStruct((depth, N_HEADS, NA_ROWS, GRID_W, nkeys), F32),
        compiler_params=pltpu.CompilerParams(dimension_semantics=("arbitrary", "arbitrary")),
        name="na_bias",
    )(padded)


def _na_kernel(q_ref, k_ref, v_ref, z_ref, bias_ref, o_ref, s_scr, mx_scr, p_scr):
    seq = q_ref.shape[0]
    rows = seq // GRID_W
    nkeys = NA_ROWS * GRID_W
    lane = lax.broadcasted_iota(jnp.int32, (GRID_W, LANES), 1)
    first_head = lane < HEAD_DIM
    nt = (((1,), (1,)), ((), ()))

    def window(r):
        rs = jnp.clip(r - NA_ROWS // 2, 0, rows - NA_ROWS)
        return r - rs, pl.multiple_of(r * GRID_W, GRID_W), pl.multiple_of(rs * GRID_W, GRID_W)

    def group(gi, carry):
        r_lo = gi * NA_GROUP

        def logits(i, c):
            var, q0, k0 = window(r_lo + i)
            qb = (q_ref[pl.ds(q0, GRID_W), :].astype(F32) * Q_SCALE_LOG2).astype(BF16)
            kw = k_ref[pl.ds(k0, nkeys), :]
            for hd in range(2):
                keep = first_head if hd == 0 else jnp.logical_not(first_head)
                qm = jnp.where(keep, qb, jnp.zeros_like(qb))
                s = lax.dot_general(qm, kw, nt, preferred_element_type=F32) + bias_ref[hd, var]
                s_scr[2 * i + hd] = s
                mx_scr[2 * i + hd] = jnp.broadcast_to(jnp.max(s, axis=-1, keepdims=True), (GRID_W, LANES))
            return c

        lax.fori_loop(0, NA_GROUP, logits, 0, unroll=True)

        def probs(t, c):
            m = mx_scr[t]
            psum = None
            for kc in range(nkeys // LANES):
                sl = slice(kc * LANES, (kc + 1) * LANES)
                p = jnp.exp2(s_scr[t, :, sl] - m)
                p_scr[t, :, sl] = p.astype(BF16)
                psum = p if psum is None else psum + p
            mx_scr[t] = jnp.broadcast_to(jnp.sum(psum, axis=-1, keepdims=True), (GRID_W, LANES))
            return c

        lax.fori_loop(0, 2 * NA_GROUP, probs, 0, unroll=True)

        def outputs(i, c):
            _, q0, k0 = window(r_lo + i)
            vw = v_ref[pl.ds(k0, nkeys), :]
            r0 = jnp.dot(p_scr[2 * i], vw, preferred_element_type=F32)
            r1 = jnp.dot(p_scr[2 * i + 1], vw, preferred_element_type=F32)
            acc = jnp.where(first_head, r0, r1)
            den = jnp.where(first_head, mx_scr[2 * i], mx_scr[2 * i + 1])
            z = z_ref[pl.ds(q0, GRID_W), :].astype(F32)
            o_ref[pl.ds(q0, GRID_W), :] = (acc * (1.0 / den) * _silu(z)).astype(BF16)
            return c

        lax.fori_loop(0, NA_GROUP, outputs, 0, unroll=True)
        return carry

    lax.fori_loop(0, rows // NA_GROUP, group, 0)


NA_GROUP = 16


def _na(proj, bias, layer, batch, seq):
    def col(c0):
        return pl.BlockSpec((seq, LANES), lambda b, p, c0=c0: (b, c0 + p))

    nkeys = NA_ROWS * GRID_W
    scratch = [pltpu.VMEM((2 * NA_GROUP, GRID_W, nkeys), F32),
               pltpu.VMEM((2 * NA_GROUP, GRID_W, LANES), F32),
               pltpu.VMEM((2 * NA_GROUP, GRID_W, nkeys), BF16)]

    return pl.pallas_call(
        _na_kernel,
        grid=(batch, N_HEAD_PAIRS),
        in_specs=[
            col(COL_QB), col(COL_KB), col(COL_VB), col(COL_ZB),
            pl.BlockSpec((None, 2, NA_ROWS, GRID_W, NA_ROWS * GRID_W), lambda b, p: (layer, p, 0, 0, 0)),
        ],
        out_specs=pl.BlockSpec((seq, LANES), lambda b, p: (b, p)),
        out_shape=jax.ShapeDtypeStruct((batch * seq, W_BRANCH), BF16),
        scratch_shapes=scratch,
        compiler_params=pltpu.CompilerParams(
            dimension_semantics=("arbitrary", "arbitrary"), vmem_limit_bytes=VMEM_LIMIT),
        name="na_attn",
    )(proj, proj, proj, proj, bias)


DIL_QB = 128
DIL_KW = 256
N_VARIANTS = 3


def _t5_bucket(rel):
    nb = T5_BUCKETS // 2
    max_exact = nb // 2
    n = np.abs(rel)
    large = max_exact + (np.log(np.maximum(n, 1) / max_exact) / np.log(T5_MAX_DIST / max_exact)
                         * (nb - max_exact)).astype(np.int32)
    large = np.minimum(large, nb - 1)
    return (np.where(rel > 0, nb, 0) + np.where(n < max_exact, n, large)).astype(np.int32)


def _dil_bias_table(t5_bias):
    n_pat = len(DIL_PATTERNS)
    lane = np.arange(DIL_VEC)
    n = np.where(lane < DIL_KW, lane, lane - DIL_VEC)
    in_range = (lane < DIL_KW) | (lane > DIL_VEC - DIL_QB)
    pick = np.zeros((n_pat, N_VARIANTS, T5_BUCKETS, DIL_VEC), np.float32)
    neg = np.zeros((n_pat, N_VARIANTS, 1, DIL_VEC), np.float32)
    for pi, (_, d) in enumerate(DIL_PATTERNS):
        for var in range(N_VARIANTS):
            step = n - DIL_HALF * var
            ok = in_range & (np.abs(step) <= DIL_HALF)
            bucket = _t5_bucket(d * np.clip(step, -DIL_HALF, DIL_HALF))
            pick[pi, var, bucket[ok], lane[ok]] = 1.0
            neg[pi, var, 0, ~ok] = NEG_INF
    vecs = jnp.einsum('bh,pvbn->hpvn', t5_bias.astype(F32), pick,
                      precision=lax.Precision.HIGHEST) * LOG2E + neg[:, :, 0][None]
    return pl.pallas_call(
        _dil_bias_kernel,
        grid=(N_HEADS,),
        in_specs=[pl.BlockSpec((None, n_pat * N_VARIANTS, DIL_VEC), lambda h: (h, 0, 0))],
        out_specs=pl.BlockSpec((None, n_pat, N_VARIANTS, DIL_QB, DIL_KW), lambda h: (h, 0, 0, 0, 0)),
        out_shape=jax.ShapeDtypeStruct((N_HEADS, n_pat, N_VARIANTS, DIL_QB, DIL_KW), F32),
        compiler_params=pltpu.CompilerParams(dimension_semantics=("arbitrary",)),
        name="dil_bias",
    )(vecs.reshape(N_HEADS, n_pat * N_VARIANTS, DIL_VEC))


DIL_VEC = 512


def _dil_bias_kernel(vec_ref, o_ref):
    for pi in range(len(DIL_PATTERNS)):
        for var in range(N_VARIANTS):
            i = pi * N_VARIANTS + var
            x = jnp.broadcast_to(vec_ref[i:i + 1, :], (DIL_QB, DIL_VEC))
            o_ref[pi, var] = pltpu.roll(x, 0, axis=1, stride=1, stride_axis=0)[:, :DIL_KW]


DIL_GROUP = 16
DIL_GROUP_FIRST = 16


def _dil_kernel(q_ref, k_ref, v_ref, z_ref, q4_ref, k4_ref, v4_ref, q16_ref, k16_ref, v16_ref, bias_ref,
                o_ref, qf, kf, vf, qd, m_s, l_s, acc_s):
    seq = q_ref.shape[0]
    conv_rows = 256
    lane = lax.broadcasted_iota(jnp.int32, (DIL_QB, LANES), 1)
    first_head = lane < HEAD_DIM
    nt = (((1,), (1,)), ((), ()))
    ones = jnp.ones((DIL_KW, LANES), BF16)

    n4 = seq // 4
    n16 = seq // 16

    def order16_rows_in_order4(rho):
        return pl.ds((rho % 4) * n4 + rho // 4, n16, stride=4)

    def window(src, res, start, size):
        start = pl.multiple_of(start, DIL_HALF)
        if len(src.shape) == 3:
            return src[res, pl.ds(start, size), :]
        return src[pl.ds(start, size), :]

    plan = ((2, 16, (q16_ref, k16_ref, v16_ref), (m_s, l_s, acc_s)),
            (1, 4, (q4_ref, k4_ref, v4_ref), (qf, kf, vf)),
            (0, 1, (qd, k_ref, v_ref), (m_s, l_s, acc_s)))

    for step, (pi, d, (q_src, k_src, v_src), (m_st, l_st, acc_st)) in enumerate(plan):
        n_sub = seq // d
        nblk = n_sub // DIL_QB

        if step == 1:
            def state_to_order4(rho, carry):
                dst = order16_rows_in_order4(rho)
                src = pl.ds(pl.multiple_of(rho * n16, n16), n16)
                for a, b in ((m_s, qf), (l_s, kf), (acc_s, vf)):
                    b[dst, :] = a[src, :]
                return carry

            lax.fori_loop(0, 16, state_to_order4, 0, unroll=2)
        if step == 2:
            def state_to_tokens(i, carry):
                src = pl.multiple_of(i * conv_rows, conv_rows)
                dst = pl.ds(src // n4 + 4 * (src % n4), conv_rows, stride=4)
                for a, b in ((qf, m_s), (kf, l_s), (vf, acc_s)):
                    b[dst, :] = a[pl.ds(src, conv_rows), :]
                qd[pl.ds(src, conv_rows), :] = (
                    q_ref[pl.ds(src, conv_rows), :].astype(F32) * Q_SCALE_LOG2).astype(BF16)
                return carry

            lax.fori_loop(0, seq // conv_rows, state_to_tokens, 0, unroll=2)

        n_group = DIL_GROUP_FIRST if step == 0 else DIL_GROUP

        def group(gi, carry, pi=pi, n_sub=n_sub, nblk=nblk, first=(step == 0), q_src=q_src,
                  k_src=k_src, v_src=v_src, m_st=m_st, l_st=l_st, acc_st=acc_st, n_group=n_group):
            units = []
            for i in range(n_group):
                u = gi * n_group + i
                res = u // nblk
                i0 = (u % nblk) * DIL_QB
                ks = jnp.clip(i0 - DIL_HALF, 0, n_sub - DIL_KW)
                units.append((res, i0, ks, (i0 - ks) // DIL_HALF))
            s_tiles, m_tiles = [], []
            for res, i0, ks, var in units:
                qb = window(q_src, res, i0, DIL_QB)
                kw = window(k_src, res, ks, DIL_KW)
                for hd in range(2):
                    keep = first_head if hd == 0 else jnp.logical_not(first_head)
                    qm = jnp.where(keep, qb, jnp.zeros_like(qb))
                    s = lax.dot_general(qm, kw, nt, preferred_element_type=F32) + bias_ref[hd, pi, var]
                    s = s.astype(BF16)
                    s_tiles.append(s)
                    m_tiles.append(jnp.max(s, axis=-1, keepdims=True))
            p_tiles = [jnp.exp2(s - m) for s, m in zip(s_tiles, m_tiles)]
            for i, (res, i0, ks, var) in enumerate(units):
                vw = window(v_src, res, ks, DIL_KW)
                vext = jnp.concatenate([vw, ones], axis=-1)
                r0 = jnp.dot(p_tiles[2 * i], vext, preferred_element_type=F32)
                r1 = jnp.dot(p_tiles[2 * i + 1], vext, preferred_element_type=F32)
                m_cur = jnp.where(first_head, m_tiles[2 * i].astype(F32), m_tiles[2 * i + 1].astype(F32))
                l_cur = jnp.where(first_head, r0[:, LANES:], r1[:, LANES:])
                acc_cur = jnp.where(first_head, r0[:, :LANES], r1[:, :LANES])
                q_rows = pl.ds(pl.multiple_of(res * n_sub + i0, DIL_HALF), DIL_QB)
                if first:
                    m_st[q_rows, :] = m_cur
                    l_st[q_rows, :] = l_cur
                    acc_st[q_rows, :] = acc_cur
                else:
                    m_old = m_st[q_rows, :]
                    m_new = jnp.maximum(m_old, m_cur)
                    a_old = jnp.exp2(m_old - m_new)
                    a_cur = jnp.exp2(m_cur - m_new)
                    m_st[q_rows, :] = m_new
                    l_st[q_rows, :] = a_old * l_st[q_rows, :] + a_cur * l_cur
                    acc_st[q_rows, :] = a_old * acc_st[q_rows, :] + a_cur * acc_cur
            return carry

        lax.fori_loop(0, seq // DIL_QB // n_group, group, 0)

    def fin(i, carry):
        r0 = pl.multiple_of(i * conv_rows, conv_rows)
        z = z_ref[pl.ds(r0, conv_rows), :].astype(F32)
        y = acc_s[pl.ds(r0, conv_rows), :] * (1.0 / l_s[pl.ds(r0, conv_rows), :])
        o_ref[pl.ds(r0, conv_rows), :] = (y * _silu(z)).astype(BF16)
        return carry

    lax.fori_loop(0, seq // conv_rows, fin, 0, unroll=2)


def _dil(proj, proj4, proj16, bias, batch, seq):
    def col(c0):
        return pl.BlockSpec((seq, LANES), lambda b, p, c0=c0: (b, c0 + p))

    def res_major(d, which):
        return pl.BlockSpec((None, d, seq // d, LANES),
                            lambda b, p, which=which: (b, 0, 0, which * N_HEAD_PAIRS + p))

    scratch = ([pltpu.VMEM((seq, LANES), F32) for _ in range(3)]
               + [pltpu.VMEM((seq, LANES), BF16)]
               + [pltpu.VMEM((seq, LANES), F32) for _ in range(3)])
    return pl.pallas_call(
        _dil_kernel,
        grid=(batch, N_HEAD_PAIRS),
        in_specs=[
            col(COL_QC), col(COL_KC), col(COL_VC), col(COL_ZC),
            res_major(4, 0), res_major(4, 1), res_major(4, 2),
            res_major(16, 0), res_major(16, 1), res_major(16, 2),
            pl.BlockSpec((2, len(DIL_PATTERNS), N_VARIANTS, DIL_QB, DIL_KW),
                         lambda b, p: (p, 0, 0, 0, 0)),
        ],
        out_specs=pl.BlockSpec((seq, LANES), lambda b, p: (b, p)),
        out_shape=jax.ShapeDtypeStruct((batch * seq, W_BRANCH), BF16),
        scratch_shapes=scratch,
        compiler_params=pltpu.CompilerParams(
            dimension_semantics=("arbitrary", "arbitrary"), vmem_limit_bytes=VMEM_LIMIT),
        name="dil_attn",
    )(proj, proj, proj, proj, proj4, proj4, proj4, proj16, proj16, proj16, bias)


N_POW = CHUNK + 1
N_LAGS = 2 * CHUNK - 1
PREP_GROUPS = 8


def _ssm_prep_kernel(lre_ref, lim_ref, ldt_ref, btre_ref, btim_ref, cre_ref, cim_ref,
                     toep_ref, win_ref, wout_ref, decay_ref):
    n = CHUNK * SSM_GROUP
    lane = lax.broadcasted_iota(jnp.int32, (SSM_GROUP, LANES), 1)
    lo = lane < SSM_STATE
    lane_n = lax.broadcasted_iota(jnp.int32, (SSM_GROUP, n), 1)
    nt = (((1,), (1,)), ((), ()))
    pair = lambda x, y: jnp.where(lo, x, y)

    for j in range(PREP_GROUPS):
        xs, ys, bbs, pw_last = {}, {}, {}, {}
        for dr in range(2):
            lam_re = lre_ref[0, dr, j]
            lam_im = lim_ref[0, dr, j]
            dt = jnp.exp(ldt_ref[0, dr, j])
            mag = jnp.exp(lam_re * dt)
            lb_re = mag * jnp.cos(lam_im * dt)
            lb_im = mag * jnp.sin(lam_im * dt)
            n_re = lb_re - 1.0
            den = lam_re * lam_re + lam_im * lam_im
            q_re = (n_re * lam_re + lb_im * lam_im) / den
            q_im = (lb_im * lam_re - n_re * lam_im) / den
            bt_re = btre_ref[0, dr, j]
            bt_im = btim_ref[0, dr, j]
            bb_re = q_re * bt_re - q_im * bt_im
            bb_im = q_re * bt_im + q_im * bt_re
            bbs[dr] = (bb_re, bb_im)
            c_re = cre_ref[0, dr, j]
            c_im = cim_ref[0, dr, j]
            pw_re = jnp.ones_like(lb_re)
            pw_im = jnp.zeros_like(lb_re)
            for k in range(N_POW):
                xs[dr, k] = (pw_re * bb_re - pw_im * bb_im, pw_re * bb_im + pw_im * bb_re)
                ys[dr, k] = (pw_re * c_re - pw_im * c_im, pw_re * c_im + pw_im * c_re)
                pw_last[dr] = (pw_re, pw_im)
                pw_re, pw_im = pw_re * lb_re - pw_im * lb_im, pw_re * lb_im + pw_im * lb_re

        step = [GROUPS_PER_SLAB * (a // GROUPS_PER_SLAB) + (a % GROUPS_PER_SLAB - j) % GROUPS_PER_SLAB
                for a in range(CHUNK)]
        rows = lambda a: slice(a * SSM_GROUP, (a + 1) * SSM_GROUP)

        decay_ref[0, 0, j:j + 1, :] = pair(pw_last[0][0], pw_last[1][0])[0:1]
        decay_ref[0, 1, j:j + 1, :] = pair(pw_last[0][1], pw_last[1][1])[0:1]

        for a, s in enumerate(step):
            win_ref[0, j, rows(a), :LANES] = pair(xs[0, CHUNK - 1 - s][0], xs[1, s][0]).astype(BF16)
            win_ref[0, j, rows(a), LANES:] = pair(xs[0, CHUNK - 1 - s][1], xs[1, s][1]).astype(BF16)

        z_re = jnp.concatenate([pair(ys[0, t + 1][0], ys[1, CHUNK - t][0]) for t in step], axis=0)
        z_nim = jnp.concatenate([pair(-ys[0, t + 1][1], -ys[1, CHUNK - t][1]) for t in step], axis=0)
        wout_ref[0, j, :LANES, :] = z_re.T.astype(BF16)
        wout_ref[0, j, LANES:, :] = z_nim.T.astype(BF16)

        kf = lax.dot_general(
            pair(bbs[0][0], -bbs[0][1]),
            jnp.concatenate([pair(*ys[0, m]) for m in range(CHUNK)], axis=0),
            nt, precision=lax.Precision.HIGHEST, preferred_element_type=F32)
        kb = lax.dot_general(
            pair(bbs[1][0], -bbs[1][1]),
            jnp.concatenate([pair(*ys[1, CHUNK - 1 - i]) for i in range(CHUNK)], axis=0),
            nt, precision=lax.Precision.HIGHEST, preferred_element_type=F32)
        for a, s in enumerate(step):
            fwd = pltpu.roll(kf, SSM_GROUP * s, axis=1) if s else kf
            shift_b = (n - SSM_GROUP * (CHUNK - 1 - s)) % n
            bwd = pltpu.roll(kb, shift_b, axis=1) if shift_b else kb
            t_nat = (jnp.where(lane_n >= SSM_GROUP * s, fwd, 0.0)
                     + jnp.where(lane_n < SSM_GROUP * (s + 1), bwd, 0.0))
            for h in range(2):
                half = t_nat[:, h * LANES:(h + 1) * LANES]
                if j:
                    half = pltpu.roll(half, SSM_GROUP * j, axis=1)
                toep_ref[0, j, rows(a), h * LANES:(h + 1) * LANES] = half.astype(BF16)


def _ssm_prep(lam_re, lam_im, log_dt, b_re, b_im, c_re, c_im):
    depth = lam_re.shape[0]
    G, P, C = SSM_GROUPS, SSM_STATE, SSM_GROUP
    n = CHUNK * C
    gb = PREP_GROUPS
    twice = lambda a: jnp.concatenate([a, a], axis=-1)
    rep = lambda a: jnp.broadcast_to(twice(a.astype(F32))[:, :, :, None, :], (depth, 2, G, C, LANES))
    ldt = jnp.broadcast_to(log_dt.astype(F32)[:, :, :, None, None], (depth, 2, G, C, LANES))
    bt = lambda a: twice(a.astype(F32).transpose(0, 1, 2, 4, 3))
    spec_in = pl.BlockSpec((1, 2, gb, C, LANES), lambda l, i: (l, 0, i, 0, 0))
    spec_w = pl.BlockSpec((1, gb, n, n), lambda l, i: (l, i, 0, 0))
    shp_w = jax.ShapeDtypeStruct((depth, G, n, n), BF16)
    return pl.pallas_call(
        _ssm_prep_kernel,
        grid=(depth, G // gb),
        in_specs=[spec_in] * 7,
        out_specs=[spec_w, spec_w, spec_w, pl.BlockSpec((1, 2, gb, LANES), lambda l, i: (l, 0, i, 0))],
        out_shape=[shp_w, shp_w, shp_w, jax.ShapeDtypeStruct((depth, 2, G, LANES), F32)],
        compiler_params=pltpu.CompilerParams(
            dimension_semantics=("arbitrary", "arbitrary"), vmem_limit_bytes=VMEM_LIMIT),
        name="ssm_prep",
    )(rep(lam_re), rep(lam_im), ldt, bt(b_re), bt(b_im), twice(c_re.astype(F32)), twice(c_im.astype(F32)))


GROUPS_PER_SLAB = LANES // SSM_GROUP
N_SLABS = W_BRANCH // LANES
RELAYOUT_ROWS = 32


def _ssm_kernel(xa_ref, toep_ref, win_ref, wout_ref, decay_ref, d_ref, o_ref, xf, u_s, st, hs, yq):
    seq = xa_ref.shape[0]
    nchunk = seq // CHUNK
    nrb = nchunk // RELAYOUT_ROWS
    gps = GROUPS_PER_SLAB
    lane_rb = lax.broadcasted_iota(jnp.int32, (RELAYOUT_ROWS, LANES), 1)
    seg_masks = [(lane_rb >= SSM_GROUP * sg) & (lane_rb < SSM_GROUP * (sg + 1)) for sg in range(gps)]
    conv_rows = 512

    for q in range(N_SLABS):
        c_lo = q * LANES

        def conv(i, carry, c_lo=c_lo):
            r0 = pl.multiple_of(i * conv_rows, conv_rows)
            xf[pl.ds(r0, conv_rows), :] = xa_ref[pl.ds(r0, conv_rows), c_lo:c_lo + LANES].astype(F32)
            return carry

        lax.fori_loop(0, seq // conv_rows, conv, 0)

        for hh in range(2):
            def fwd_relayout(rb, carry, hh=hh):
                c0 = pl.multiple_of(rb * RELAYOUT_ROWS, RELAYOUT_ROWS)
                rolled = []
                for r in range(gps):
                    xs = xf[pl.ds(CHUNK * c0 + gps * hh + r, RELAYOUT_ROWS, stride=CHUNK), :]
                    rolled.append(xs if r == 0 else pltpu.roll(xs, SSM_GROUP * r, axis=1))
                for j in range(gps):
                    out = rolled[(0 - j) % gps]
                    for sg in range(1, gps):
                        out = jnp.where(seg_masks[sg], rolled[(sg - j) % gps], out)
                    u_s[j, pl.ds(c0, RELAYOUT_ROWS), hh * LANES:(hh + 1) * LANES] = out.astype(BF16)
                return carry

            lax.fori_loop(0, nrb, fwd_relayout, 0, unroll=2)

        for j in range(gps):
            s_in = jnp.dot(u_s[j], win_ref[gps * q + j], preferred_element_type=F32)
            st[0, pl.ds(j, nchunk, stride=gps), :] = s_in[:, :LANES]
            st[1, pl.ds(j, nchunk, stride=gps), :] = s_in[:, LANES:]

        g_lo = gps * q
        a_re = decay_ref[0, g_lo:g_lo + gps, :]
        a_im = decay_ref[1, g_lo:g_lo + gps, :]
        lo, hi = slice(0, SSM_STATE), slice(SSM_STATE, LANES)

        def scan(i, carry):
            hrf, hif, hrb, hib = carry
            rf = pl.ds(pl.multiple_of(i * gps, gps), gps)
            rb_ = pl.ds(pl.multiple_of((nchunk - 1 - i) * gps, gps), gps)
            hs[0, rf, lo] = hrf[:, lo]
            hs[1, rf, lo] = hif[:, lo]
            hs[0, rb_, hi] = hrb[:, hi]
            hs[1, rb_, hi] = hib[:, hi]
            return (a_re * hrf - a_im * hif + st[0, rf, :], a_re * hif + a_im * hrf + st[1, rf, :],
                    a_re * hrb - a_im * hib + st[0, rb_, :], a_re * hib + a_im * hrb + st[1, rb_, :])

        zero = jnp.zeros((gps, LANES), F32)
        lax.fori_loop(0, nchunk, scan, (zero, zero, zero, zero), unroll=8)

        for j in range(gps):
            g = gps * q + j
            h_in = jnp.concatenate([hs[0, pl.ds(j, nchunk, stride=gps), :].astype(BF16),
                                    hs[1, pl.ds(j, nchunk, stride=gps), :].astype(BF16)], axis=-1)
            yq[j] = (jnp.dot(u_s[j], toep_ref[g], preferred_element_type=F32)
                     + jnp.dot(h_in, wout_ref[g], preferred_element_type=F32))

        d_row = d_ref[:, c_lo:c_lo + LANES]
        for hh in range(2):
            def bwd_relayout(rb, carry, hh=hh, d_row=d_row, q=q):
                c0 = pl.multiple_of(rb * RELAYOUT_ROWS, RELAYOUT_ROWS)
                ys = [yq[j, pl.ds(c0, RELAYOUT_ROWS), hh * LANES:(hh + 1) * LANES] for j in range(gps)]
                for r in range(gps):
                    merged = ys[(0 - r) % gps]
                    for sg in range(1, gps):
                        merged = jnp.where(seg_masks[sg], ys[(sg - r) % gps], merged)
                    if r:
                        merged = pltpu.roll(merged, LANES - SSM_GROUP * r, axis=1)
                    tok = pl.ds(CHUNK * c0 + gps * hh + r, RELAYOUT_ROWS, stride=CHUNK)
                    o_ref[q, tok, :] = merged + d_row * xf[tok, :]
                return carry

            lax.fori_loop(0, nrb, bwd_relayout, 0, unroll=2)


def _ssm(proj, toep, w_in, w_out, decay, d_skip, layer, batch, seq):
    nchunk = seq // CHUNK
    n = CHUNK * SSM_GROUP
    wspec = pl.BlockSpec((None, SSM_GROUPS, n, n), lambda b: (layer, 0, 0, 0), pipeline_mode=pl.Buffered(1))
    return pl.pallas_call(
        _ssm_kernel,
        grid=(batch,),
        in_specs=[
            pl.BlockSpec((seq, W_BRANCH), lambda b: (b, COL_XA // N_SLABS)),
            wspec, wspec, wspec,
            pl.BlockSpec((None, 2, SSM_GROUPS, LANES), lambda b: (layer, 0, 0, 0)),
            pl.BlockSpec((None, 1, W_BRANCH), lambda b: (layer, 0, 0)),
        ],
        out_specs=pl.BlockSpec((N_SLABS, seq, LANES), lambda b: (0, b, 0)),
        out_shape=jax.ShapeDtypeStruct((N_SLABS, batch * seq, LANES), F32),
        scratch_shapes=[
            pltpu.VMEM((seq, LANES), F32),
            pltpu.VMEM((GROUPS_PER_SLAB, nchunk, n), BF16),
            pltpu.VMEM((2, nchunk * GROUPS_PER_SLAB, LANES), F32),
            pltpu.VMEM((2, nchunk * GROUPS_PER_SLAB, LANES), F32),
            pltpu.VMEM((GROUPS_PER_SLAB, nchunk, n), F32),
        ],
        compiler_params=pltpu.CompilerParams(
            dimension_semantics=("arbitrary",), vmem_limit_bytes=VMEM_LIMIT),
        name="ssm_mix",
    )(proj, toep, w_in, w_out, decay, d_skip)


OUT_TM = 512


def _out_proj_kernel(ya_ref, za_ref, yb_ref, yc_ref, x_ref, w_ref, gw_ref, gb_ref, fg_ref, o_ref,
                     w_bf, gw_bf, *, final):
    @pl.when(pl.program_id(0) == 0)
    def _():
        for n in range(MIX_WIDTH // W_BRANCH):
            w_bf[n * W_BRANCH:(n + 1) * W_BRANCH, :] = w_ref[n * W_BRANCH:(n + 1) * W_BRANCH, :].astype(BF16)
        gw_bf[...] = gw_ref[...].astype(BF16)

    y = jnp.concatenate([ya_ref[i] for i in range(N_SLABS)], axis=-1)
    g = _gelu_tanh(y)
    gate = jnp.dot(g.astype(BF16), gw_bf[...], preferred_element_type=F32) + gb_ref[...]
    ya = g * _sigmoid(gate) * _silu(za_ref[...].astype(F32))
    delta = (jnp.dot(ya.astype(BF16), w_bf[:W_BRANCH, :], preferred_element_type=F32)
             + jnp.dot(yb_ref[...], w_bf[W_BRANCH:2 * W_BRANCH, :], preferred_element_type=F32)
             + jnp.dot(yc_ref[...], w_bf[2 * W_BRANCH:, :], preferred_element_type=F32))
    x = x_ref[...] + delta
    if final:
        ms = jnp.mean(x * x, axis=-1, keepdims=True)
        x = x * lax.rsqrt(ms + RMS_EPS) * fg_ref[...]
    o_ref[...] = x


def _out_proj(ya_pre, proj, yb, yc, x2d, w, glu_w, glu_b, final_g, layer, final):
    rows = x2d.shape[0]
    row_blk = lambda width: pl.BlockSpec((OUT_TM, width), lambda i: (i, 0))
    const = lambda shape: pl.BlockSpec((None,) + shape, lambda i: (layer,) + (0,) * len(shape),
                                       pipeline_mode=pl.Buffered(1))
    return pl.pallas_call(
        functools.partial(_out_proj_kernel, final=final),
        grid=(rows // OUT_TM,),
        in_specs=[
            pl.BlockSpec((N_SLABS, OUT_TM, LANES), lambda i: (0, i, 0)),
            pl.BlockSpec((OUT_TM, W_BRANCH), lambda i: (i, COL_ZA // N_SLABS)),
            row_blk(W_BRANCH), row_blk(W_BRANCH), row_blk(D_MODEL),
            const((MIX_WIDTH, D_MODEL)), const((W_BRANCH, W_BRANCH)),
            const((1, W_BRANCH)),
            pl.BlockSpec((None, 1, D_MODEL), lambda i: (0, 0, 0), pipeline_mode=pl.Buffered(1)),
        ],
        out_specs=row_blk(D_MODEL),
        out_shape=jax.ShapeDtypeStruct((rows, D_MODEL), F32),
        scratch_shapes=[pltpu.VMEM((MIX_WIDTH, D_MODEL), BF16), pltpu.VMEM((W_BRANCH, W_BRANCH), BF16)],
        compiler_params=pltpu.CompilerParams(
            dimension_semantics=("arbitrary",), vmem_limit_bytes=VMEM_LIMIT),
        name="out_proj_final" if final else "out_proj",
    )(ya_pre, proj, yb, yc, x2d, w, glu_w, glu_b, final_g)


def kernel(x, norm_g, w_in, w_out, ssm_lam_re, ssm_lam_im, ssm_log_dt, ssm_b_re, ssm_b_im, ssm_c_re,
           ssm_c_im, ssm_d, glu_w, glu_b, na_rpb, t5_bias, final_g):
    batch, seq, _ = x.shape
    depth = w_in.shape[0]
    x2d = x.astype(F32).reshape(batch * seq, D_MODEL)
    toep, s_in, s_out, decay = _ssm_prep(ssm_lam_re, ssm_lam_im, ssm_log_dt, ssm_b_re, ssm_b_im,
                                         ssm_c_re, ssm_c_im)
    dil_bias = _dil_bias_table(t5_bias)
    na_bias = _na_bias_table(na_rpb)
    w_in_f, w_out_f, glu_w_f = w_in.astype(F32), w_out.astype(F32), glu_w.astype(F32)
    row3 = lambda a: a.astype(F32).reshape(a.shape[0], 1, a.shape[1])
    norm_g3, ssm_d3, glu_b3 = row3(norm_g), row3(ssm_d), row3(glu_b)
    final_g3 = final_g.astype(F32).reshape(1, 1, D_MODEL)
    for l in range(depth):
        proj, proj4, proj16 = _in_proj(x2d, norm_g3, w_in_f, l, batch, seq)
        ya_pre = _ssm(proj, toep, s_in, s_out, decay, ssm_d3, l, batch, seq)
        yb = _na(proj, na_bias, l, batch, seq)
        yc = _dil(proj, proj4, proj16, dil_bias, batch, seq)
        x2d = _out_proj(ya_pre, proj, yb, yc, x2d, w_out_f, glu_w_f, glu_b3, final_g3, l,
                        final=(l == depth - 1))
    return x2d.reshape(batch, seq, D_MODEL).astype(x.dtype)
```

```python
import functools

import numpy as np
import jax
import jax.numpy as jnp
from jax import lax
from jax.experimental import pallas as pl
from jax.experimental.pallas import tpu as pltpu

F32 = jnp.float32
BF16 = jnp.bfloat16

D_MODEL = 1024
HEAD_DIM = 64
W_BRANCH = 512
N_HEADS = W_BRANCH // HEAD_DIM
N_HEAD_PAIRS = N_HEADS // 2
SSM_GROUP = 16
SSM_GROUPS = W_BRANCH // SSM_GROUP
SSM_STATE = 64
GRID_W = 64
NA_ROWS = 8
NA_COLS = 16
DIL_PATTERNS = ((128, 1), (512, 4), (2048, 16))
DIL_HALF = 64
T5_BUCKETS = 32
T5_MAX_DIST = 1024
RMS_EPS = 1e-6
NEG_INF = -1e30
IN_COLS = 10 * W_BRANCH
MIX_WIDTH = 3 * W_BRANCH

COL_XA, COL_ZA, COL_QB, COL_KB, COL_VB, COL_ZB, COL_QC, COL_KC, COL_VC, COL_ZC = (
    4 * i for i in range(10))

LOG2E = float(np.log2(np.e))
Q_SCALE_LOG2 = HEAD_DIM ** -0.5 * LOG2E

LANES = 128
CHUNK = 16
VMEM_LIMIT = 56 * 1024 * 1024


def _sigmoid(z):
    return 0.5 * (1.0 + jnp.tanh(0.5 * z))


def _silu(z):
    return z * _sigmoid(z)


def _gelu_tanh(x):
    return 0.5 * x * (1.0 + jnp.tanh(np.sqrt(2.0 / np.pi).astype(np.float32) * (x + 0.044715 * (x * x * x))))


IN_TM = 512
IN_TN = 512


def _in_proj_kernel(x_ref, g_ref, w_ref, o_ref, w_bf):
    @pl.when(pl.program_id(0) == 0)
    def _():
        for n in range(IN_COLS // IN_TN):
            w_bf[:, n * IN_TN:(n + 1) * IN_TN] = w_ref[:, n * IN_TN:(n + 1) * IN_TN].astype(BF16)

    x = x_ref[...]
    ms = jnp.mean(x * x, axis=-1, keepdims=True)
    h = (x * lax.rsqrt(ms + RMS_EPS) * g_ref[...]).astype(BF16)
    for n in range(IN_COLS // IN_TN):
        o_ref[:, n * IN_TN:(n + 1) * IN_TN] = jnp.dot(
            h, w_bf[:, n * IN_TN:(n + 1) * IN_TN], preferred_element_type=F32).astype(BF16)


def _in_proj(x2d, g, w, layer):
    rows = x2d.shape[0]
    return pl.pallas_call(
        _in_proj_kernel,
        grid=(rows // IN_TM,),
        in_specs=[
            pl.BlockSpec((IN_TM, D_MODEL), lambda i: (i, 0)),
            pl.BlockSpec((None, 1, D_MODEL), lambda i: (layer, 0, 0)),
            pl.BlockSpec((None, D_MODEL, IN_COLS), lambda i: (layer, 0, 0), pipeline_mode=pl.Buffered(1)),
        ],
        out_specs=pl.BlockSpec((IN_TM, IN_COLS), lambda i: (i, 0)),
        out_shape=jax.ShapeDtypeStruct((rows, IN_COLS), BF16),
        scratch_shapes=[pltpu.VMEM((D_MODEL, IN_COLS), BF16)],
        compiler_params=pltpu.CompilerParams(
            dimension_semantics=("arbitrary",), vmem_limit_bytes=VMEM_LIMIT),
        name="in_proj",
    )(x2d, g, w)


def _na_bias_kernel(rpb_ref, o_ref):
    lane = lax.broadcasted_iota(jnp.int32, (GRID_W, LANES), 1)
    j = lax.broadcasted_iota(jnp.int32, (GRID_W, LANES), 0)
    c = lane & (GRID_W - 1)
    col_start = jnp.clip(j - NA_COLS // 2, 0, GRID_W - NA_COLS)
    valid = (c >= col_start) & (c < col_start + NA_COLS)
    first = lane < GRID_W
    even, odd = [], []
    for a in range(2 * NA_ROWS - 1):
        vec = pltpu.roll(rpb_ref[a:a + 1, :], LANES - (NA_COLS - 1), axis=1)
        x = jnp.broadcast_to(vec, (GRID_W, LANES))
        even.append(pltpu.roll(x, 0, axis=1, stride=1, stride_axis=0))
        odd.append(pltpu.roll(x, GRID_W, axis=1, stride=1, stride_axis=0))
    for v in range(NA_ROWS):
        for i in range(NA_ROWS // 2):
            a0 = 2 * i - v + (NA_ROWS - 1)
            tile = jnp.where(first, even[a0], odd[a0 + 1])
            o_ref[v, :, i * LANES:(i + 1) * LANES] = jnp.where(valid, tile * LOG2E, NEG_INF)


def _na_bias_table(rpb):
    depth = rpb.shape[0]
    padded = jnp.pad(rpb.astype(F32), ((0, 0), (0, 0), (0, 1), (0, LANES - (2 * NA_COLS - 1))))
    nkeys = NA_ROWS * GRID_W
    return pl.pallas_call(
        _na_bias_kernel,
        grid=(depth, N_HEADS),
        in_specs=[pl.BlockSpec((None, None, 2 * NA_ROWS, LANES), lambda l, h: (l, h, 0, 0))],
        out_specs=pl.BlockSpec((None, None, NA_ROWS, GRID_W, nkeys), lambda l, h: (l, h, 0, 0, 0)),
        out_shape=jax.ShapeDtypeStruct((depth, N_HEADS, NA_ROWS, GRID_W, nkeys), F32),
        compiler_params=pltpu.CompilerParams(dimension_semantics=("arbitrary", "arbitrary")),
        name="na_bias",
    )(padded)


def _na_kernel(q_ref, k_ref, v_ref, z_ref, bias_ref, o_ref, s_scr, mx_scr, p_scr):
    seq = q_ref.shape[0]
    rows = seq // GRID_W
    nkeys = NA_ROWS * GRID_W
    lane = lax.broadcasted_iota(jnp.int32, (GRID_W, LANES), 1)
    first_head = lane < HEAD_DIM
    nt = (((1,), (1,)), ((), ()))

    def window(r):
        rs = jnp.clip(r - NA_ROWS // 2, 0, rows - NA_ROWS)
        return r - rs, pl.multiple_of(r * GRID_W, GRID_W), pl.multiple_of(rs * GRID_W, GRID_W)

    def group(gi, carry):
        r_lo = gi * NA_GROUP

        def logits(i, c):
            var, q0, k0 = window(r_lo + i)
            qb = (q_ref[pl.ds(q0, GRID_W), :].astype(F32) * Q_SCALE_LOG2).astype(BF16)
            kw = k_ref[pl.ds(k0, nkeys), :]
            for hd in range(2):
                keep = first_head if hd == 0 else jnp.logical_not(first_head)
                qm = jnp.where(keep, qb, jnp.zeros_like(qb))
                s = lax.dot_general(qm, kw, nt, preferred_element_type=F32) + bias_ref[hd, var]
                s_scr[2 * i + hd] = s
                mx_scr[2 * i + hd] = jnp.broadcast_to(jnp.max(s, axis=-1, keepdims=True), (GRID_W, LANES))
            return c

        lax.fori_loop(0, NA_GROUP, logits, 0, unroll=True)

        def probs(t, c):
            m = mx_scr[t]
            psum = None
            for kc in range(nkeys // LANES):
                sl = slice(kc * LANES, (kc + 1) * LANES)
                p = jnp.exp2(s_scr[t, :, sl] - m)
                p_scr[t, :, sl] = p.astype(BF16)
                psum = p if psum is None else psum + p
            mx_scr[t] = jnp.broadcast_to(jnp.sum(psum, axis=-1, keepdims=True), (GRID_W, LANES))
            return c

        lax.fori_loop(0, 2 * NA_GROUP, probs, 0, unroll=True)

        def outputs(i, c):
            _, q0, k0 = window(r_lo + i)
            vw = v_ref[pl.ds(k0, nkeys), :]
            r0 = jnp.dot(p_scr[2 * i], vw, preferred_element_type=F32)
            r1 = jnp.dot(p_scr[2 * i + 1], vw, preferred_element_type=F32)
            acc = jnp.where(first_head, r0, r1)
            den = jnp.where(first_head, mx_scr[2 * i], mx_scr[2 * i + 1])
            z = z_ref[pl.ds(q0, GRID_W), :].astype(F32)
            o_ref[pl.ds(q0, GRID_W), :] = (acc * (1.0 / den) * _silu(z)).astype(BF16)
            return c

        lax.fori_loop(0, NA_GROUP, outputs, 0, unroll=True)
        return carry

    lax.fori_loop(0, rows // NA_GROUP, group, 0)


NA_GROUP = 16


def _na(proj, bias, layer, batch, seq):
    def col(c0):
        return pl.BlockSpec((seq, LANES), lambda b, p, c0=c0: (b, c0 + p))

    nkeys = NA_ROWS * GRID_W
    scratch = [pltpu.VMEM((2 * NA_GROUP, GRID_W, nkeys), F32),
               pltpu.VMEM((2 * NA_GROUP, GRID_W, LANES), F32),
               pltpu.VMEM((2 * NA_GROUP, GRID_W, nkeys), BF16)]

    return pl.pallas_call(
        _na_kernel,
        grid=(batch, N_HEAD_PAIRS),
        in_specs=[
            col(COL_QB), col(COL_KB), col(COL_VB), col(COL_ZB),
            pl.BlockSpec((None, 2, NA_ROWS, GRID_W, NA_ROWS * GRID_W), lambda b, p: (layer, p, 0, 0, 0)),
        ],
        out_specs=pl.BlockSpec((seq, LANES), lambda b, p: (b, p)),
        out_shape=jax.ShapeDtypeStruct((batch * seq, W_BRANCH), BF16),
        scratch_shapes=scratch,
        compiler_params=pltpu.CompilerParams(
            dimension_semantics=("arbitrary", "arbitrary"), vmem_limit_bytes=VMEM_LIMIT),
        name="na_attn",
    )(proj, proj, proj, proj, bias)


DIL_QB = 128
DIL_KW = 256
N_VARIANTS = 3


def _t5_bucket(rel):
    nb = T5_BUCKETS // 2
    max_exact = nb // 2
    n = np.abs(rel)
    large = max_exact + (np.log(np.maximum(n, 1) / max_exact) / np.log(T5_MAX_DIST / max_exact)
                         * (nb - max_exact)).astype(np.int32)
    large = np.minimum(large, nb - 1)
    return (np.where(rel > 0, nb, 0) + np.where(n < max_exact, n, large)).astype(np.int32)


def _dil_bias_table(t5_bias):
    n_pat = len(DIL_PATTERNS)
    lane = np.arange(DIL_VEC)
    n = np.where(lane < DIL_KW, lane, lane - DIL_VEC)
    in_range = (lane < DIL_KW) | (lane > DIL_VEC - DIL_QB)
    pick = np.zeros((n_pat, N_VARIANTS, T5_BUCKETS, DIL_VEC), np.float32)
    neg = np.zeros((n_pat, N_VARIANTS, 1, DIL_VEC), np.float32)
    for pi, (_, d) in enumerate(DIL_PATTERNS):
        for var in range(N_VARIANTS):
            step = n - DIL_HALF * var
            ok = in_range & (np.abs(step) <= DIL_HALF)
            bucket = _t5_bucket(d * np.clip(step, -DIL_HALF, DIL_HALF))
            pick[pi, var, bucket[ok], lane[ok]] = 1.0
            neg[pi, var, 0, ~ok] = NEG_INF
    vecs = jnp.einsum('bh,pvbn->hpvn', t5_bias.astype(F32), pick,
                      precision=lax.Precision.HIGHEST) * LOG2E + neg[:, :, 0][None]
    return pl.pallas_call(
        _dil_bias_kernel,
        grid=(N_HEADS,),
        in_specs=[pl.BlockSpec((None, n_pat * N_VARIANTS, DIL_VEC), lambda h: (h, 0, 0))],
        out_specs=pl.BlockSpec((None, n_pat, N_VARIANTS, DIL_QB, DIL_KW), lambda h: (h, 0, 0, 0, 0)),
        out_shape=jax.ShapeDtypeStruct((N_HEADS, n_pat, N_VARIANTS, DIL_QB, DIL_KW), F32),
        compiler_params=pltpu.CompilerParams(dimension_semantics=("arbitrary",)),
        name="dil_bias",
    )(vecs.reshape(N_HEADS, n_pat * N_VARIANTS, DIL_VEC))


DIL_VEC = 512


def _dil_bias_kernel(vec_ref, o_ref):
    for pi in range(len(DIL_PATTERNS)):
        for var in range(N_VARIANTS):
            i = pi * N_VARIANTS + var
            x = jnp.broadcast_to(vec_ref[i:i + 1, :], (DIL_QB, DIL_VEC))
            o_ref[pi, var] = pltpu.roll(x, 0, axis=1, stride=1, stride_axis=0)[:, :DIL_KW]


DIL_GROUP = 16
DIL_GROUP_FIRST = 16


def _dil_kernel(q_ref, k_ref, v_ref, z_ref, bias_ref, o_ref, qf, kf, vf, qg, kg, vg, qd, kd, vd,
                m_s, l_s, acc_s):
    seq = q_ref.shape[0]
    conv_rows = 256
    lane = lax.broadcasted_iota(jnp.int32, (DIL_QB, LANES), 1)
    first_head = lane < HEAD_DIM
    nt = (((1,), (1,)), ((), ()))
    ones = jnp.ones((DIL_KW, LANES), BF16)

    def conv(i, carry):
        r0 = pl.multiple_of(i * conv_rows, conv_rows)
        qf[pl.ds(r0, conv_rows), :] = q_ref[pl.ds(r0, conv_rows), :].astype(F32) * Q_SCALE_LOG2
        kf[pl.ds(r0, conv_rows), :] = k_ref[pl.ds(r0, conv_rows), :].astype(F32)
        vf[pl.ds(r0, conv_rows), :] = v_ref[pl.ds(r0, conv_rows), :].astype(F32)
        return carry

    lax.fori_loop(0, seq // conv_rows, conv, 0)

    n4 = seq // 4
    n16 = seq // 16

    def to_order4(i, carry):
        dst = pl.multiple_of(i * conv_rows, conv_rows)
        src = pl.ds(dst // n4 + 4 * (dst % n4), conv_rows, stride=4)
        for a, b in ((qf, qg), (kf, kg), (vf, vg)):
            b[pl.ds(dst, conv_rows), :] = a[src, :]
        return carry

    lax.fori_loop(0, seq // conv_rows, to_order4, 0, unroll=2)

    def order16_rows_in_order4(rho):
        return pl.ds((rho % 4) * n4 + rho // 4, n16, stride=4)

    def to_order16(rho, carry):
        src = order16_rows_in_order4(rho)
        dst = pl.ds(pl.multiple_of(rho * n16, n16), n16)
        for a, b in ((qg, qd), (kg, kd), (vg, vd)):
            b[dst, :] = a[src, :].astype(BF16)
        return carry

    lax.fori_loop(0, 16, to_order16, 0, unroll=2)

    def window(src, res, start, size, n_sub):
        return src[pl.ds(pl.multiple_of(res * n_sub + start, DIL_HALF), size), :]

    plan = ((2, 16, (qd, kd, vd), (m_s, l_s, acc_s)),
            (1, 4, (qd, kd, vd), (qf, kf, vf)),
            (0, 1, (qd, k_ref, v_ref), (m_s, l_s, acc_s)))

    for step, (pi, d, (q_src, k_src, v_src), (m_st, l_st, acc_st)) in enumerate(plan):
        n_sub = seq // d
        nblk = n_sub // DIL_QB

        if step == 1:
            def state_to_order4(rho, carry):
                dst = order16_rows_in_order4(rho)
                src = pl.ds(pl.multiple_of(rho * n16, n16), n16)
                for a, b in ((m_s, qf), (l_s, kf), (acc_s, vf)):
                    b[dst, :] = a[src, :]
                return carry

            lax.fori_loop(0, 16, state_to_order4, 0, unroll=2)

            def cast_order4(i, carry):
                r0 = pl.multiple_of(i * conv_rows, conv_rows)
                for a, b in ((qg, qd), (kg, kd), (vg, vd)):
                    b[pl.ds(r0, conv_rows), :] = a[pl.ds(r0, conv_rows), :].astype(BF16)
                return carry

            lax.fori_loop(0, seq // conv_rows, cast_order4, 0)
        if step == 2:
            def state_to_tokens(i, carry):
                src = pl.multiple_of(i * conv_rows, conv_rows)
                dst = pl.ds(src // n4 + 4 * (src % n4), conv_rows, stride=4)
                for a, b in ((qf, m_s), (kf, l_s), (vf, acc_s)):
                    b[dst, :] = a[pl.ds(src, conv_rows), :]
                qd[pl.ds(src, conv_rows), :] = (
                    q_ref[pl.ds(src, conv_rows), :].astype(F32) * Q_SCALE_LOG2).astype(BF16)
                return carry

            lax.fori_loop(0, seq // conv_rows, state_to_tokens, 0, unroll=2)

        n_group = DIL_GROUP_FIRST if step == 0 else DIL_GROUP

        def group(gi, carry, pi=pi, n_sub=n_sub, nblk=nblk, first=(step == 0), q_src=q_src,
                  k_src=k_src, v_src=v_src, m_st=m_st, l_st=l_st, acc_st=acc_st, n_group=n_group):
            units = []
            for i in range(n_group):
                u = gi * n_group + i
                res = u // nblk
                i0 = (u % nblk) * DIL_QB
                ks = jnp.clip(i0 - DIL_HALF, 0, n_sub - DIL_KW)
                units.append((res, i0, ks, (i0 - ks) // DIL_HALF))
            s_tiles, m_tiles = [], []
            for res, i0, ks, var in units:
                qb = window(q_src, res, i0, DIL_QB, n_sub)
                kw = window(k_src, res, ks, DIL_KW, n_sub)
                for hd in range(2):
                    keep = first_head if hd == 0 else jnp.logical_not(first_head)
                    qm = jnp.where(keep, qb, jnp.zeros_like(qb))
                    s = lax.dot_general(qm, kw, nt, preferred_element_type=F32) + bias_ref[hd, pi, var]
                    s = s.astype(BF16)
                    s_tiles.append(s)
                    m_tiles.append(jnp.max(s, axis=-1, keepdims=True))
            p_tiles = [jnp.exp2(s - m) for s, m in zip(s_tiles, m_tiles)]
            for i, (res, i0, ks, var) in enumerate(units):
                vw = window(v_src, res, ks, DIL_KW, n_sub)
                vext = jnp.concatenate([vw, ones], axis=-1)
                r0 = jnp.dot(p_tiles[2 * i], vext, preferred_element_type=F32)
                r1 = jnp.dot(p_tiles[2 * i + 1], vext, preferred_element_type=F32)
                m_cur = jnp.where(first_head, m_tiles[2 * i].astype(F32), m_tiles[2 * i + 1].astype(F32))
                l_cur = jnp.where(first_head, r0[:, LANES:], r1[:, LANES:])
                acc_cur = jnp.where(first_head, r0[:, :LANES], r1[:, :LANES])
                q_rows = pl.ds(pl.multiple_of(res * n_sub + i0, DIL_HALF), DIL_QB)
                if first:
                    m_st[q_rows, :] = m_cur
                    l_st[q_rows, :] = l_cur
                    acc_st[q_rows, :] = acc_cur
                else:
                    m_old = m_st[q_rows, :]
                    m_new = jnp.maximum(m_old, m_cur)
                    a_old = jnp.exp2(m_old - m_new)
                    a_cur = jnp.exp2(m_cur - m_new)
                    m_st[q_rows, :] = m_new
                    l_st[q_rows, :] = a_old * l_st[q_rows, :] + a_cur * l_cur
                    acc_st[q_rows, :] = a_old * acc_st[q_rows, :] + a_cur * acc_cur
            return carry

        lax.fori_loop(0, seq // DIL_QB // n_group, group, 0)

    def fin(i, carry):
        r0 = pl.multiple_of(i * conv_rows, conv_rows)
        z = z_ref[pl.ds(r0, conv_rows), :].astype(F32)
        y = acc_s[pl.ds(r0, conv_rows), :] * (1.0 / l_s[pl.ds(r0, conv_rows), :])
        o_ref[pl.ds(r0, conv_rows), :] = (y * _silu(z)).astype(BF16)
        return carry

    lax.fori_loop(0, seq // conv_rows, fin, 0, unroll=2)


def _dil(proj, bias, batch, seq):
    def col(c0):
        return pl.BlockSpec((seq, LANES), lambda b, p, c0=c0: (b, c0 + p))

    scratch = ([pltpu.VMEM((seq, LANES), F32) for _ in range(6)]
               + [pltpu.VMEM((seq, LANES), BF16) for _ in range(3)]
               + [pltpu.VMEM((seq, LANES), F32) for _ in range(3)])
    return pl.pallas_call(
        _dil_kernel,
        grid=(batch, N_HEAD_PAIRS),
        in_specs=[
            col(COL_QC), col(COL_KC), col(COL_VC), col(COL_ZC),
            pl.BlockSpec((2, len(DIL_PATTERNS), N_VARIANTS, DIL_QB, DIL_KW),
                         lambda b, p: (p, 0, 0, 0, 0)),
        ],
        out_specs=pl.BlockSpec((seq, LANES), lambda b, p: (b, p)),
        out_shape=jax.ShapeDtypeStruct((batch * seq, W_BRANCH), BF16),
        scratch_shapes=scratch,
        compiler_params=pltpu.CompilerParams(
            dimension_semantics=("arbitrary", "arbitrary"), vmem_limit_bytes=VMEM_LIMIT),
        name="dil_attn",
    )(proj, proj, proj, proj, bias)


N_POW = CHUNK + 1
PREP_GROUPS = 8


def _ssm_prep_kernel(lre_ref, lim_ref, ldt_ref, btre_ref, btim_ref, cre_ref, cim_ref,
                     toep_ref, win_ref, wout_ref, decay_ref):
    n = CHUNK * SSM_GROUP
    lane = lax.broadcasted_iota(jnp.int32, (SSM_GROUP, LANES), 1)
    lo = lane < SSM_STATE
    lane_n = lax.broadcasted_iota(jnp.int32, (SSM_GROUP, n), 1)
    nt = (((1,), (1,)), ((), ()))
    pair = lambda x, y: jnp.where(lo, x, y)

    for j in range(PREP_GROUPS):
        xs, ys, bbs, pw_last = {}, {}, {}, {}
        for dr in range(2):
            lam_re = lre_ref[0, dr, j]
            lam_im = lim_ref[0, dr, j]
            dt = jnp.exp(ldt_ref[0, dr, j])
            mag = jnp.exp(lam_re * dt)
            lb_re = mag * jnp.cos(lam_im * dt)
            lb_im = mag * jnp.sin(lam_im * dt)
            n_re = lb_re - 1.0
            den = lam_re * lam_re + lam_im * lam_im
            q_re = (n_re * lam_re + lb_im * lam_im) / den
            q_im = (lb_im * lam_re - n_re * lam_im) / den
            bt_re = btre_ref[0, dr, j]
            bt_im = btim_ref[0, dr, j]
            bb_re = q_re * bt_re - q_im * bt_im
            bb_im = q_re * bt_im + q_im * bt_re
            bbs[dr] = (bb_re, bb_im)
            c_re = cre_ref[0, dr, j]
            c_im = cim_ref[0, dr, j]
            pw_re = jnp.ones_like(lb_re)
            pw_im = jnp.zeros_like(lb_re)
            for k in range(N_POW):
                xs[dr, k] = (pw_re * bb_re - pw_im * bb_im, pw_re * bb_im + pw_im * bb_re)
                ys[dr, k] = (pw_re * c_re - pw_im * c_im, pw_re * c_im + pw_im * c_re)
                pw_last[dr] = (pw_re, pw_im)
                pw_re, pw_im = pw_re * lb_re - pw_im * lb_im, pw_re * lb_im + pw_im * lb_re

        step = [GROUPS_PER_SLAB * (a // GROUPS_PER_SLAB) + (a % GROUPS_PER_SLAB - j) % GROUPS_PER_SLAB
                for a in range(CHUNK)]
        rows = lambda a: slice(a * SSM_GROUP, (a + 1) * SSM_GROUP)

        decay_ref[0, 0, j:j + 1, :] = pair(pw_last[0][0], pw_last[1][0])[0:1]
        decay_ref[0, 1, j:j + 1, :] = pair(pw_last[0][1], pw_last[1][1])[0:1]

        for a, s in enumerate(step):
            win_ref[0, j, rows(a), :LANES] = pair(xs[0, CHUNK - 1 - s][0], xs[1, s][0]).astype(BF16)
            win_ref[0, j, rows(a), LANES:] = pair(xs[0, CHUNK - 1 - s][1], xs[1, s][1]).astype(BF16)

        z_re = jnp.concatenate([pair(ys[0, t + 1][0], ys[1, CHUNK - t][0]) for t in step], axis=0)
        z_nim = jnp.concatenate([pair(-ys[0, t + 1][1], -ys[1, CHUNK - t][1]) for t in step], axis=0)
        wout_ref[0, j, :LANES, :] = z_re.T.astype(BF16)
        wout_ref[0, j, LANES:, :] = z_nim.T.astype(BF16)

        kf = lax.dot_general(
            pair(bbs[0][0], -bbs[0][1]),
            jnp.concatenate([pair(*ys[0, m]) for m in range(CHUNK)], axis=0),
            nt, precision=lax.Precision.HIGHEST, preferred_element_type=F32)
        kb = lax.dot_general(
            pair(bbs[1][0], -bbs[1][1]),
            jnp.concatenate([pair(*ys[1, CHUNK - 1 - i]) for i in range(CHUNK)], axis=0),
            nt, precision=lax.Precision.HIGHEST, preferred_element_type=F32)
        for a, s in enumerate(step):
            fwd = pltpu.roll(kf, SSM_GROUP * s, axis=1) if s else kf
            shift_b = (n - SSM_GROUP * (CHUNK - 1 - s)) % n
            bwd = pltpu.roll(kb, shift_b, axis=1) if shift_b else kb
            t_nat = (jnp.where(lane_n >= SSM_GROUP * s, fwd, 0.0)
                     + jnp.where(lane_n < SSM_GROUP * (s + 1), bwd, 0.0))
            for h in range(2):
                half = t_nat[:, h * LANES:(h + 1) * LANES]
                if j:
                    half = pltpu.roll(half, SSM_GROUP * j, axis=1)
                toep_ref[0, j, rows(a), h * LANES:(h + 1) * LANES] = half.astype(BF16)


def _ssm_prep(lam_re, lam_im, log_dt, b_re, b_im, c_re, c_im):
    depth = lam_re.shape[0]
    G, P, C = SSM_GROUPS, SSM_STATE, SSM_GROUP
    n = CHUNK * C
    gb = PREP_GROUPS
    twice = lambda a: jnp.concatenate([a, a], axis=-1)
    rep = lambda a: jnp.broadcast_to(twice(a.astype(F32))[:, :, :, None, :], (depth, 2, G, C, LANES))
    ldt = jnp.broadcast_to(log_dt.astype(F32)[:, :, :, None, None], (depth, 2, G, C, LANES))
    bt = lambda a: twice(a.astype(F32).transpose(0, 1, 2, 4, 3))
    spec_in = pl.BlockSpec((1, 2, gb, C, LANES), lambda l, i: (l, 0, i, 0, 0))
    spec_w = pl.BlockSpec((1, gb, n, n), lambda l, i: (l, i, 0, 0))
    shp_w = jax.ShapeDtypeStruct((depth, G, n, n), BF16)
    return pl.pallas_call(
        _ssm_prep_kernel,
        grid=(depth, G // gb),
        in_specs=[spec_in] * 7,
        out_specs=[spec_w, spec_w, spec_w, pl.BlockSpec((1, 2, gb, LANES), lambda l, i: (l, 0, i, 0))],
        out_shape=[shp_w, shp_w, shp_w, jax.ShapeDtypeStruct((depth, 2, G, LANES), F32)],
        compiler_params=pltpu.CompilerParams(
            dimension_semantics=("arbitrary", "arbitrary"), vmem_limit_bytes=VMEM_LIMIT),
        name="ssm_prep",
    )(rep(lam_re), rep(lam_im), ldt, bt(b_re), bt(b_im), twice(c_re.astype(F32)), twice(c_im.astype(F32)))


GROUPS_PER_SLAB = LANES // SSM_GROUP
N_SLABS = W_BRANCH // LANES
RELAYOUT_ROWS = 32


def _ssm_kernel(xa_ref, toep_ref, win_ref, wout_ref, decay_ref, d_ref, o_ref, xf, u_s, st, hs, yq):
    seq = xa_ref.shape[0]
    nchunk = seq // CHUNK
    nrb = nchunk // RELAYOUT_ROWS
    gps = GROUPS_PER_SLAB
    lane_rb = lax.broadcasted_iota(jnp.int32, (RELAYOUT_ROWS, LANES), 1)
    seg_masks = [(lane_rb >= SSM_GROUP * sg) & (lane_rb < SSM_GROUP * (sg + 1)) for sg in range(gps)]
    conv_rows = 512

    for q in range(N_SLABS):
        c_lo = q * LANES

        def conv(i, carry, c_lo=c_lo):
            r0 = pl.multiple_of(i * conv_rows, conv_rows)
            xf[pl.ds(r0, conv_rows), :] = xa_ref[pl.ds(r0, conv_rows), c_lo:c_lo + LANES].astype(F32)
            return carry

        lax.fori_loop(0, seq // conv_rows, conv, 0)

        for hh in range(2):
            def fwd_relayout(rb, carry, hh=hh):
                c0 = pl.multiple_of(rb * RELAYOUT_ROWS, RELAYOUT_ROWS)
                rolled = []
                for r in range(gps):
                    xs = xf[pl.ds(CHUNK * c0 + gps * hh + r, RELAYOUT_ROWS, stride=CHUNK), :]
                    rolled.append(xs if r == 0 else pltpu.roll(xs, SSM_GROUP * r, axis=1))
                for j in range(gps):
                    out = rolled[(0 - j) % gps]
                    for sg in range(1, gps):
                        out = jnp.where(seg_masks[sg], rolled[(sg - j) % gps], out)
                    u_s[j, pl.ds(c0, RELAYOUT_ROWS), hh * LANES:(hh + 1) * LANES] = out.astype(BF16)
                return carry

            lax.fori_loop(0, nrb, fwd_relayout, 0, unroll=2)

        for j in range(gps):
            s_in = jnp.dot(u_s[j], win_ref[gps * q + j], preferred_element_type=F32)
            st[0, pl.ds(j, nchunk, stride=gps), :] = s_in[:, :LANES]
            st[1, pl.ds(j, nchunk, stride=gps), :] = s_in[:, LANES:]

        g_lo = gps * q
        a_re = decay_ref[0, g_lo:g_lo + gps, :]
        a_im = decay_ref[1, g_lo:g_lo + gps, :]
        lo, hi = slice(0, SSM_STATE), slice(SSM_STATE, LANES)

        def scan(i, carry):
            hrf, hif, hrb, hib = carry
            rf = pl.ds(pl.multiple_of(i * gps, gps), gps)
            rb_ = pl.ds(pl.multiple_of((nchunk - 1 - i) * gps, gps), gps)
            hs[0, rf, lo] = hrf[:, lo]
            hs[1, rf, lo] = hif[:, lo]
            hs[0, rb_, hi] = hrb[:, hi]
            hs[1, rb_, hi] = hib[:, hi]
            return (a_re * hrf - a_im * hif + st[0, rf, :], a_re * hif + a_im * hrf + st[1, rf, :],
                    a_re * hrb - a_im * hib + st[0, rb_, :], a_re * hib + a_im * hrb + st[1, rb_, :])

        zero = jnp.zeros((gps, LANES), F32)
        lax.fori_loop(0, nchunk, scan, (zero, zero, zero, zero), unroll=8)

        for j in range(gps):
            g = gps * q + j
            h_in = jnp.concatenate([hs[0, pl.ds(j, nchunk, stride=gps), :].astype(BF16),
                                    hs[1, pl.ds(j, nchunk, stride=gps), :].astype(BF16)], axis=-1)
            yq[j] = (jnp.dot(u_s[j], toep_ref[g], preferred_element_type=F32)
                     + jnp.dot(h_in, wout_ref[g], preferred_element_type=F32))

        d_row = d_ref[:, c_lo:c_lo + LANES]
        for hh in range(2):
            def bwd_relayout(rb, carry, hh=hh, d_row=d_row, q=q):
                c0 = pl.multiple_of(rb * RELAYOUT_ROWS, RELAYOUT_ROWS)
                ys = [yq[j, pl.ds(c0, RELAYOUT_ROWS), hh * LANES:(hh + 1) * LANES] for j in range(gps)]
                for r in range(gps):
                    merged = ys[(0 - r) % gps]
                    for sg in range(1, gps):
                        merged = jnp.where(seg_masks[sg], ys[(sg - r) % gps], merged)
                    if r:
                        merged = pltpu.roll(merged, LANES - SSM_GROUP * r, axis=1)
                    tok = pl.ds(CHUNK * c0 + gps * hh + r, RELAYOUT_ROWS, stride=CHUNK)
                    o_ref[q, tok, :] = merged + d_row * xf[tok, :]
                return carry

            lax.fori_loop(0, nrb, bwd_relayout, 0, unroll=2)


def _ssm(proj, toep, w_in, w_out, decay, d_skip, layer, batch, seq):
    nchunk = seq // CHUNK
    n = CHUNK * SSM_GROUP
    wspec = pl.BlockSpec((None, SSM_GROUPS, n, n), lambda b: (layer, 0, 0, 0), pipeline_mode=pl.Buffered(1))
    return pl.pallas_call(
        _ssm_kernel,
        grid=(batch,),
        in_specs=[
            pl.BlockSpec((seq, W_BRANCH), lambda b: (b, COL_XA // N_SLABS)),
            wspec, wspec, wspec,
            pl.BlockSpec((None, 2, SSM_GROUPS, LANES), lambda b: (layer, 0, 0, 0)),
            pl.BlockSpec((None, 1, W_BRANCH), lambda b: (layer, 0, 0)),
        ],
        out_specs=pl.BlockSpec((N_SLABS, seq, LANES), lambda b: (0, b, 0)),
        out_shape=jax.ShapeDtypeStruct((N_SLABS, batch * seq, LANES), F32),
        scratch_shapes=[
            pltpu.VMEM((seq, LANES), F32),
            pltpu.VMEM((GROUPS_PER_SLAB, nchunk, n), BF16),
            pltpu.VMEM((2, nchunk * GROUPS_PER_SLAB, LANES), F32),
            pltpu.VMEM((2, nchunk * GROUPS_PER_SLAB, LANES), F32),
            pltpu.VMEM((GROUPS_PER_SLAB, nchunk, n), F32),
        ],
        compiler_params=pltpu.CompilerParams(
            dimension_semantics=("arbitrary",), vmem_limit_bytes=VMEM_LIMIT),
        name="ssm_mix",
    )(proj, toep, w_in, w_out, decay, d_skip)


OUT_TM = 512


def _out_proj_kernel(ya_ref, za_ref, yb_ref, yc_ref, x_ref, w_ref, gw_ref, gb_ref, fg_ref, o_ref,
                     w_bf, gw_bf, *, final):
    @pl.when(pl.program_id(0) == 0)
    def _():
        for n in range(MIX_WIDTH // W_BRANCH):
            w_bf[n * W_BRANCH:(n + 1) * W_BRANCH, :] = w_ref[n * W_BRANCH:(n + 1) * W_BRANCH, :].astype(BF16)
        gw_bf[...] = gw_ref[...].astype(BF16)

    y = jnp.concatenate([ya_ref[i] for i in range(N_SLABS)], axis=-1)
    g = _gelu_tanh(y)
    gate = jnp.dot(g.astype(BF16), gw_bf[...], preferred_element_type=F32) + gb_ref[...]
    ya = g * _sigmoid(gate) * _silu(za_ref[...].astype(F32))
    delta = (jnp.dot(ya.astype(BF16), w_bf[:W_BRANCH, :], preferred_element_type=F32)
             + jnp.dot(yb_ref[...], w_bf[W_BRANCH:2 * W_BRANCH, :], preferred_element_type=F32)
             + jnp.dot(yc_ref[...], w_bf[2 * W_BRANCH:, :], preferred_element_type=F32))
    x = x_ref[...] + delta
    if final:
        ms = jnp.mean(x * x, axis=-1, keepdims=True)
        x = x * lax.rsqrt(ms + RMS_EPS) * fg_ref[...]
    o_ref[...] = x


def _out_proj(ya_pre, proj, yb, yc, x2d, w, glu_w, glu_b, final_g, layer, final):
    rows = x2d.shape[0]
    row_blk = lambda width: pl.BlockSpec((OUT_TM, width), lambda i: (i, 0))
    const = lambda shape: pl.BlockSpec((None,) + shape, lambda i: (layer,) + (0,) * len(shape),
                                       pipeline_mode=pl.Buffered(1))
    return pl.pallas_call(
        functools.partial(_out_proj_kernel, final=final),
        grid=(rows // OUT_TM,),
        in_specs=[
            pl.BlockSpec((N_SLABS, OUT_TM, LANES), lambda i: (0, i, 0)),
            pl.BlockSpec((OUT_TM, W_BRANCH), lambda i: (i, COL_ZA // N_SLABS)),
            row_blk(W_BRANCH), row_blk(W_BRANCH), row_blk(D_MODEL),
            const((MIX_WIDTH, D_MODEL)), const((W_BRANCH, W_BRANCH)),
            const((1, W_BRANCH)),
            pl.BlockSpec((None, 1, D_MODEL), lambda i: (0, 0, 0), pipeline_mode=pl.Buffered(1)),
        ],
        out_specs=row_blk(D_MODEL),
        out_shape=jax.ShapeDtypeStruct((rows, D_MODEL), F32),
        scratch_shapes=[pltpu.VMEM((MIX_WIDTH, D_MODEL), BF16), pltpu.VMEM((W_BRANCH, W_BRANCH), BF16)],
        compiler_params=pltpu.CompilerParams(
            dimension_semantics=("arbitrary",), vmem_limit_bytes=VMEM_LIMIT),
        name="out_proj_final" if final else "out_proj",
    )(ya_pre, proj, yb, yc, x2d, w, glu_w, glu_b, final_g)


def kernel(x, norm_g, w_in, w_out, ssm_lam_re, ssm_lam_im, ssm_log_dt, ssm_b_re, ssm_b_im, ssm_c_re,
           ssm_c_im, ssm_d, glu_w, glu_b, na_rpb, t5_bias, final_g):
    batch, seq, _ = x.shape
    depth = w_in.shape[0]
    x2d = x.astype(F32).reshape(batch * seq, D_MODEL)
    toep, s_in, s_out, decay = _ssm_prep(ssm_lam_re, ssm_lam_im, ssm_log_dt, ssm_b_re, ssm_b_im,
                                         ssm_c_re, ssm_c_im)
    dil_bias = _dil_bias_table(t5_bias)
    na_bias = _na_bias_table(na_rpb)
    w_in_f, w_out_f, glu_w_f = w_in.astype(F32), w_out.astype(F32), glu_w.astype(F32)
    row3 = lambda a: a.astype(F32).reshape(a.shape[0], 1, a.shape[1])
    norm_g3, ssm_d3, glu_b3 = row3(norm_g), row3(ssm_d), row3(glu_b)
    final_g3 = final_g.astype(F32).reshape(1, 1, D_MODEL)
    for l in range(depth):
        proj = _in_proj(x2d, norm_g3, w_in_f, l)
        ya_pre = _ssm(proj, toep, s_in, s_out, decay, ssm_d3, l, batch, seq)
        yb = _na(proj, na_bias, l, batch, seq)
        yc = _dil(proj, dil_bias, batch, seq)
        x2d = _out_proj(ya_pre, proj, yb, yc, x2d, w_out_f, glu_w_f, glu_b3, final_g3, l,
                        final=(l == depth - 1))
    return x2d.reshape(batch, seq, D_MODEL).astype(x.dtype)
```

```python
import functools

import numpy as np
import jax
import jax.numpy as jnp
from jax import lax
from jax.experimental import pallas as pl
from jax.experimental.pallas import tpu as pltpu

F32 = jnp.float32
BF16 = jnp.bfloat16

D_MODEL = 1024
HEAD_DIM = 64
W_BRANCH = 512
N_HEADS = W_BRANCH // HEAD_DIM
N_HEAD_PAIRS = N_HEADS // 2
SSM_GROUP = 16
SSM_GROUPS = W_BRANCH // SSM_GROUP
SSM_STATE = 64
GRID_W = 64
NA_ROWS = 8
NA_COLS = 16
DIL_PATTERNS = ((128, 1), (512, 4), (2048, 16))
DIL_HALF = 64
T5_BUCKETS = 32
T5_MAX_DIST = 1024
RMS_EPS = 1e-6
NEG_INF = -1e30
IN_COLS = 10 * W_BRANCH
MIX_WIDTH = 3 * W_BRANCH

COL_XA, COL_ZA, COL_QB, COL_KB, COL_VB, COL_ZB, COL_QC, COL_KC, COL_VC, COL_ZC = (
    4 * i for i in range(10))

LOG2E = float(np.log2(np.e))
Q_SCALE_LOG2 = HEAD_DIM ** -0.5 * LOG2E

LANES = 128
CHUNK = 16
VMEM_LIMIT = 56 * 1024 * 1024


def _sigmoid(z):
    return 0.5 * (1.0 + jnp.tanh(0.5 * z))


def _silu(z):
    return z * _sigmoid(z)


def _gelu_tanh(x):
    return 0.5 * x * (1.0 + jnp.tanh(np.sqrt(2.0 / np.pi).astype(np.float32) * (x + 0.044715 * (x * x * x))))


IN_TM = 512
IN_TN = 512


def _in_proj_kernel(x_ref, g_ref, w_ref, o_ref, w_bf):
    @pl.when(pl.program_id(0) == 0)
    def _():
        for n in range(IN_COLS // IN_TN):
            w_bf[:, n * IN_TN:(n + 1) * IN_TN] = w_ref[:, n * IN_TN:(n + 1) * IN_TN].astype(BF16)

    x = x_ref[...]
    ms = jnp.mean(x * x, axis=-1, keepdims=True)
    h = (x * lax.rsqrt(ms + RMS_EPS) * g_ref[...]).astype(BF16)
    for n in range(IN_COLS // IN_TN):
        o_ref[:, n * IN_TN:(n + 1) * IN_TN] = jnp.dot(
            h, w_bf[:, n * IN_TN:(n + 1) * IN_TN], preferred_element_type=F32).astype(BF16)


def _in_proj(x2d, g, w, layer):
    rows = x2d.shape[0]
    return pl.pallas_call(
        _in_proj_kernel,
        grid=(rows // IN_TM,),
        in_specs=[
            pl.BlockSpec((IN_TM, D_MODEL), lambda i: (i, 0)),
            pl.BlockSpec((None, 1, D_MODEL), lambda i: (layer, 0, 0)),
            pl.BlockSpec((None, D_MODEL, IN_COLS), lambda i: (layer, 0, 0), pipeline_mode=pl.Buffered(1)),
        ],
        out_specs=pl.BlockSpec((IN_TM, IN_COLS), lambda i: (i, 0)),
        out_shape=jax.ShapeDtypeStruct((rows, IN_COLS), BF16),
        scratch_shapes=[pltpu.VMEM((D_MODEL, IN_COLS), BF16)],
        compiler_params=pltpu.CompilerParams(
            dimension_semantics=("arbitrary",), vmem_limit_bytes=VMEM_LIMIT),
        name="in_proj",
    )(x2d, g, w)


def _na_bias_kernel(rpb_ref, o_ref):
    lane = lax.broadcasted_iota(jnp.int32, (GRID_W, LANES), 1)
    j = lax.broadcasted_iota(jnp.int32, (GRID_W, LANES), 0)
    c = lane & (GRID_W - 1)
    col_start = jnp.clip(j - NA_COLS // 2, 0, GRID_W - NA_COLS)
    valid = (c >= col_start) & (c < col_start + NA_COLS)
    first = lane < GRID_W
    even, odd = [], []
    for a in range(2 * NA_ROWS - 1):
        vec = pltpu.roll(rpb_ref[a:a + 1, :], LANES - (NA_COLS - 1), axis=1)
        x = jnp.broadcast_to(vec, (GRID_W, LANES))
        even.append(pltpu.roll(x, 0, axis=1, stride=1, stride_axis=0))
        odd.append(pltpu.roll(x, GRID_W, axis=1, stride=1, stride_axis=0))
    for v in range(NA_ROWS):
        for i in range(NA_ROWS // 2):
            a0 = 2 * i - v + (NA_ROWS - 1)
            tile = jnp.where(first, even[a0], odd[a0 + 1])
            o_ref[v, :, i * LANES:(i + 1) * LANES] = jnp.where(valid, tile * LOG2E, NEG_INF)


def _na_bias_table(rpb):
    depth = rpb.shape[0]
    padded = jnp.pad(rpb.astype(F32), ((0, 0), (0, 0), (0, 1), (0, LANES - (2 * NA_COLS - 1))))
    nkeys = NA_ROWS * GRID_W
    return pl.pallas_call(
        _na_bias_kernel,
        grid=(depth, N_HEADS),
        in_specs=[pl.BlockSpec((None, None, 2 * NA_ROWS, LANES), lambda l, h: (l, h, 0, 0))],
        out_specs=pl.BlockSpec((None, None, NA_ROWS, GRID_W, nkeys), lambda l, h: (l, h, 0, 0, 0)),
        out_shape=jax.ShapeDtypeStruct((depth, N_HEADS, NA_ROWS, GRID_W, nkeys), F32),
        compiler_params=pltpu.CompilerParams(dimension_semantics=("arbitrary", "arbitrary")),
        name="na_bias",
    )(padded)


def _na_kernel(q_ref, k_ref, v_ref, z_ref, bias_ref, o_ref):
    seq = q_ref.shape[0]
    rows = seq // GRID_W
    half_rows = NA_ROWS // 2
    tile_keys = half_rows * GRID_W
    lane = lax.broadcasted_iota(jnp.int32, (GRID_W, LANES), 1)
    first_head = lane < HEAD_DIM
    masks = (first_head, jnp.logical_not(first_head))
    nt = (((1,), (1,)), ((), ()))
    row_start = lambda r: min(max(r - NA_ROWS // 2, 0), rows - NA_ROWS)

    for g0 in range(0, rows, NA_GROUP):
        group = range(g0, g0 + NA_GROUP)
        users = {}
        for r in group:
            for half in range(2):
                users.setdefault(row_start(r) + half_rows * half, []).append((r, half))
        qm = {}
        for r in group:
            qb = (q_ref[r * GRID_W:(r + 1) * GRID_W, :].astype(F32) * Q_SCALE_LOG2).astype(BF16)
            for hd in range(2):
                qm[r, hd] = jnp.where(masks[hd], qb, jnp.zeros_like(qb))
        s_half = {}
        for k, us in sorted(users.items()):
            lhs = jnp.concatenate([qm[r, hd] for r, _ in us for hd in range(2)], axis=0)
            part = lax.dot_general(lhs, k_ref[k * GRID_W:k * GRID_W + tile_keys, :], nt,
                                   preferred_element_type=F32)
            for i, (r, half) in enumerate(us):
                for hd in range(2):
                    blk = part[(2 * i + hd) * GRID_W:(2 * i + hd + 1) * GRID_W, :]
                    bias = bias_ref[hd, r - row_start(r), :, half * tile_keys:(half + 1) * tile_keys]
                    s_half[r, hd, half] = blk + bias
        p_half, den = {}, {}
        for r in group:
            for hd in range(2):
                s0, s1 = s_half[r, hd, 0], s_half[r, hd, 1]
                m = jnp.max(jnp.maximum(s0, s1), axis=-1, keepdims=True)
                p0, p1 = jnp.exp2(s0 - m), jnp.exp2(s1 - m)
                den[r, hd] = jnp.sum(p0 + p1, axis=-1, keepdims=True)
                p_half[r, hd, 0], p_half[r, hd, 1] = p0.astype(BF16), p1.astype(BF16)
        acc = {}
        for k, us in sorted(users.items()):
            lhs = jnp.concatenate([p_half[r, hd, half] for r, half in us for hd in range(2)], axis=0)
            part = jnp.dot(lhs, v_ref[k * GRID_W:k * GRID_W + tile_keys, :], preferred_element_type=F32)
            for i, (r, half) in enumerate(us):
                for hd in range(2):
                    blk = part[(2 * i + hd) * GRID_W:(2 * i + hd + 1) * GRID_W, :]
                    acc[r, hd] = blk if (r, hd) not in acc else acc[r, hd] + blk
        for r in group:
            y = jnp.where(first_head, acc[r, 0], acc[r, 1])
            d = jnp.where(first_head, den[r, 0], den[r, 1])
            z = z_ref[r * GRID_W:(r + 1) * GRID_W, :].astype(F32)
            o_ref[r * GRID_W:(r + 1) * GRID_W, :] = (y * (1.0 / d) * _silu(z)).astype(BF16)


NA_GROUP = 16


def _na(proj, bias, layer, batch, seq):
    def col(c0):
        return pl.BlockSpec((seq, LANES), lambda b, p, c0=c0: (b, c0 + p))

    return pl.pallas_call(
        _na_kernel,
        grid=(batch, N_HEAD_PAIRS),
        in_specs=[
            col(COL_QB), col(COL_KB), col(COL_VB), col(COL_ZB),
            pl.BlockSpec((None, 2, NA_ROWS, GRID_W, NA_ROWS * GRID_W), lambda b, p: (layer, p, 0, 0, 0)),
        ],
        out_specs=pl.BlockSpec((seq, LANES), lambda b, p: (b, p)),
        out_shape=jax.ShapeDtypeStruct((batch * seq, W_BRANCH), BF16),
        compiler_params=pltpu.CompilerParams(
            dimension_semantics=("arbitrary", "arbitrary"), vmem_limit_bytes=VMEM_LIMIT),
        name="na_attn",
    )(proj, proj, proj, proj, bias)


DIL_QB = 128
DIL_KW = 256
N_VARIANTS = 3


def _t5_bucket(rel):
    nb = T5_BUCKETS // 2
    max_exact = nb // 2
    n = np.abs(rel)
    large = max_exact + (np.log(np.maximum(n, 1) / max_exact) / np.log(T5_MAX_DIST / max_exact)
                         * (nb - max_exact)).astype(np.int32)
    large = np.minimum(large, nb - 1)
    return (np.where(rel > 0, nb, 0) + np.where(n < max_exact, n, large)).astype(np.int32)


def _dil_bias_table(t5_bias):
    n_pat = len(DIL_PATTERNS)
    lane = np.arange(DIL_VEC)
    n = np.where(lane < DIL_KW, lane, lane - DIL_VEC)
    in_range = (lane < DIL_KW) | (lane > DIL_VEC - DIL_QB)
    pick = np.zeros((n_pat, N_VARIANTS, T5_BUCKETS, DIL_VEC), np.float32)
    neg = np.zeros((n_pat, N_VARIANTS, 1, DIL_VEC), np.float32)
    for pi, (_, d) in enumerate(DIL_PATTERNS):
        for var in range(N_VARIANTS):
            step = n - DIL_HALF * var
            ok = in_range & (np.abs(step) <= DIL_HALF)
            bucket = _t5_bucket(d * np.clip(step, -DIL_HALF, DIL_HALF))
            pick[pi, var, bucket[ok], lane[ok]] = 1.0
            neg[pi, var, 0, ~ok] = NEG_INF
    vecs = jnp.einsum('bh,pvbn->hpvn', t5_bias.astype(F32), pick,
                      precision=lax.Precision.HIGHEST) * LOG2E + neg[:, :, 0][None]
    return pl.pallas_call(
        _dil_bias_kernel,
        grid=(N_HEADS,),
        in_specs=[pl.BlockSpec((None, n_pat * N_VARIANTS, DIL_VEC), lambda h: (h, 0, 0))],
        out_specs=pl.BlockSpec((None, n_pat, N_VARIANTS, DIL_QB, DIL_KW), lambda h: (h, 0, 0, 0, 0)),
        out_shape=jax.ShapeDtypeStruct((N_HEADS, n_pat, N_VARIANTS, DIL_QB, DIL_KW), F32),
        compiler_params=pltpu.CompilerParams(dimension_semantics=("arbitrary",)),
        name="dil_bias",
    )(vecs.reshape(N_HEADS, n_pat * N_VARIANTS, DIL_VEC))


DIL_VEC = 512


def _dil_bias_kernel(vec_ref, o_ref):
    for pi in range(len(DIL_PATTERNS)):
        for var in range(N_VARIANTS):
            i = pi * N_VARIANTS + var
            x = jnp.broadcast_to(vec_ref[i:i + 1, :], (DIL_QB, DIL_VEC))
            o_ref[pi, var] = pltpu.roll(x, 0, axis=1, stride=1, stride_axis=0)[:, :DIL_KW]


DIL_GROUP = 16
DIL_GROUP_FIRST = 16


def _dil_kernel(q_ref, k_ref, v_ref, z_ref, bias_ref, o_ref, qf, kf, vf, qg, kg, vg, qd, kd, vd,
                m_s, l_s, acc_s):
    seq = q_ref.shape[0]
    conv_rows = 256
    lane = lax.broadcasted_iota(jnp.int32, (DIL_QB, LANES), 1)
    first_head = lane < HEAD_DIM
    nt = (((1,), (1,)), ((), ()))
    ones = jnp.ones((DIL_KW, LANES), BF16)

    def conv(i, carry):
        r0 = pl.multiple_of(i * conv_rows, conv_rows)
        qf[pl.ds(r0, conv_rows), :] = q_ref[pl.ds(r0, conv_rows), :].astype(F32) * Q_SCALE_LOG2
        kf[pl.ds(r0, conv_rows), :] = k_ref[pl.ds(r0, conv_rows), :].astype(F32)
        vf[pl.ds(r0, conv_rows), :] = v_ref[pl.ds(r0, conv_rows), :].astype(F32)
        return carry

    lax.fori_loop(0, seq // conv_rows, conv, 0)

    n4 = seq // 4
    n16 = seq // 16

    def to_order4(i, carry):
        dst = pl.multiple_of(i * conv_rows, conv_rows)
        src = pl.ds(dst // n4 + 4 * (dst % n4), conv_rows, stride=4)
        for a, b in ((qf, qg), (kf, kg), (vf, vg)):
            b[pl.ds(dst, conv_rows), :] = a[src, :]
        return carry

    lax.fori_loop(0, seq // conv_rows, to_order4, 0, unroll=2)

    def order16_rows_in_order4(rho):
        return pl.ds((rho % 4) * n4 + rho // 4, n16, stride=4)

    def to_order16(rho, carry):
        src = order16_rows_in_order4(rho)
        dst = pl.ds(pl.multiple_of(rho * n16, n16), n16)
        for a, b in ((qg, qd), (kg, kd), (vg, vd)):
            b[dst, :] = a[src, :].astype(BF16)
        return carry

    lax.fori_loop(0, 16, to_order16, 0, unroll=2)

    def window(src, res, start, size, n_sub):
        return src[pl.ds(pl.multiple_of(res * n_sub + start, DIL_HALF), size), :]

    plan = ((2, 16, (qd, kd, vd), (m_s, l_s, acc_s)),
            (1, 4, (qd, kd, vd), (qf, kf, vf)),
            (0, 1, (qd, k_ref, v_ref), (m_s, l_s, acc_s)))

    for step, (pi, d, (q_src, k_src, v_src), (m_st, l_st, acc_st)) in enumerate(plan):
        n_sub = seq // d
        nblk = n_sub // DIL_QB

        if step == 1:
            def state_to_order4(rho, carry):
                dst = order16_rows_in_order4(rho)
                src = pl.ds(pl.multiple_of(rho * n16, n16), n16)
                for a, b in ((m_s, qf), (l_s, kf), (acc_s, vf)):
                    b[dst, :] = a[src, :]
                return carry

            lax.fori_loop(0, 16, state_to_order4, 0, unroll=2)

            def cast_order4(i, carry):
                r0 = pl.multiple_of(i * conv_rows, conv_rows)
                for a, b in ((qg, qd), (kg, kd), (vg, vd)):
                    b[pl.ds(r0, conv_rows), :] = a[pl.ds(r0, conv_rows), :].astype(BF16)
                return carry

            lax.fori_loop(0, seq // conv_rows, cast_order4, 0)
        if step == 2:
            def state_to_tokens(i, carry):
                src = pl.multiple_of(i * conv_rows, conv_rows)
                dst = pl.ds(src // n4 + 4 * (src % n4), conv_rows, stride=4)
                for a, b in ((qf, m_s), (kf, l_s), (vf, acc_s)):
                    b[dst, :] = a[pl.ds(src, conv_rows), :]
                qd[pl.ds(src, conv_rows), :] = (
                    q_ref[pl.ds(src, conv_rows), :].astype(F32) * Q_SCALE_LOG2).astype(BF16)
                return carry

            lax.fori_loop(0, seq // conv_rows, state_to_tokens, 0, unroll=2)

        n_group = DIL_GROUP_FIRST if step == 0 else DIL_GROUP

        def group(gi, carry, pi=pi, n_sub=n_sub, nblk=nblk, first=(step == 0), q_src=q_src,
                  k_src=k_src, v_src=v_src, m_st=m_st, l_st=l_st, acc_st=acc_st, n_group=n_group):
            units = []
            for i in range(n_group):
                u = gi * n_group + i
                res = u // nblk
                i0 = (u % nblk) * DIL_QB
                ks = jnp.clip(i0 - DIL_HALF, 0, n_sub - DIL_KW)
                units.append((res, i0, ks, (i0 - ks) // DIL_HALF))
            s_tiles, m_tiles = [], []
            for res, i0, ks, var in units:
                qb = window(q_src, res, i0, DIL_QB, n_sub)
                kw = window(k_src, res, ks, DIL_KW, n_sub)
                for hd in range(2):
                    keep = first_head if hd == 0 else jnp.logical_not(first_head)
                    qm = jnp.where(keep, qb, jnp.zeros_like(qb))
                    s = lax.dot_general(qm, kw, nt, preferred_element_type=F32) + bias_ref[hd, pi, var]
                    s = s.astype(BF16)
                    s_tiles.append(s)
                    m_tiles.append(jnp.max(s, axis=-1, keepdims=True))
            p_tiles = [jnp.exp2(s - m) for s, m in zip(s_tiles, m_tiles)]
            for i, (res, i0, ks, var) in enumerate(units):
                vw = window(v_src, res, ks, DIL_KW, n_sub)
                vext = jnp.concatenate([vw, ones], axis=-1)
                r0 = jnp.dot(p_tiles[2 * i], vext, preferred_element_type=F32)
                r1 = jnp.dot(p_tiles[2 * i + 1], vext, preferred_element_type=F32)
                m_cur = jnp.where(first_head, m_tiles[2 * i].astype(F32), m_tiles[2 * i + 1].astype(F32))
                l_cur = jnp.where(first_head, r0[:, LANES:], r1[:, LANES:])
                acc_cur = jnp.where(first_head, r0[:, :LANES], r1[:, :LANES])
                q_rows = pl.ds(pl.multiple_of(res * n_sub + i0, DIL_HALF), DIL_QB)
                if first:
                    m_st[q_rows, :] = m_cur
                    l_st[q_rows, :] = l_cur
                    acc_st[q_rows, :] = acc_cur
                else:
                    m_old = m_st[q_rows, :]
                    m_new = jnp.maximum(m_old, m_cur)
                    a_old = jnp.exp2(m_old - m_new)
                    a_cur = jnp.exp2(m_cur - m_new)
                    m_st[q_rows, :] = m_new
                    l_st[q_rows, :] = a_old * l_st[q_rows, :] + a_cur * l_cur
                    acc_st[q_rows, :] = a_old * acc_st[q_rows, :] + a_cur * acc_cur
            return carry

        lax.fori_loop(0, seq // DIL_QB // n_group, group, 0)

    def fin(i, carry):
        r0 = pl.multiple_of(i * conv_rows, conv_rows)
        z = z_ref[pl.ds(r0, conv_rows), :].astype(F32)
        y = acc_s[pl.ds(r0, conv_rows), :] * (1.0 / l_s[pl.ds(r0, conv_rows), :])
        o_ref[pl.ds(r0, conv_rows), :] = (y * _silu(z)).astype(BF16)
        return carry

    lax.fori_loop(0, seq // conv_rows, fin, 0, unroll=2)


def _dil(proj, bias, batch, seq):
    def col(c0):
        return pl.BlockSpec((seq, LANES), lambda b, p, c0=c0: (b, c0 + p))

    scratch = ([pltpu.VMEM((seq, LANES), F32) for _ in range(6)]
               + [pltpu.VMEM((seq, LANES), BF16) for _ in range(3)]
               + [pltpu.VMEM((seq, LANES), F32) for _ in range(3)])
    return pl.pallas_call(
        _dil_kernel,
        grid=(batch, N_HEAD_PAIRS),
        in_specs=[
            col(COL_QC), col(COL_KC), col(COL_VC), col(COL_ZC),
            pl.BlockSpec((2, len(DIL_PATTERNS), N_VARIANTS, DIL_QB, DIL_KW),
                         lambda b, p: (p, 0, 0, 0, 0)),
        ],
        out_specs=pl.BlockSpec((seq, LANES), lambda b, p: (b, p)),
        out_shape=jax.ShapeDtypeStruct((batch * seq, W_BRANCH), BF16),
        scratch_shapes=scratch,
        compiler_params=pltpu.CompilerParams(
            dimension_semantics=("arbitrary", "arbitrary"), vmem_limit_bytes=VMEM_LIMIT),
        name="dil_attn",
    )(proj, proj, proj, proj, bias)


N_POW = CHUNK + 1
PREP_GROUPS = 8


def _ssm_prep_kernel(lre_ref, lim_ref, ldt_ref, btre_ref, btim_ref, cre_ref, cim_ref,
                     toep_ref, win_ref, wout_ref, decay_ref):
    n = CHUNK * SSM_GROUP
    lane = lax.broadcasted_iota(jnp.int32, (SSM_GROUP, LANES), 1)
    lo = lane < SSM_STATE
    lane_n = lax.broadcasted_iota(jnp.int32, (SSM_GROUP, n), 1)
    nt = (((1,), (1,)), ((), ()))
    pair = lambda x, y: jnp.where(lo, x, y)

    for j in range(PREP_GROUPS):
        xs, ys, bbs, pw_last = {}, {}, {}, {}
        for dr in range(2):
            lam_re = lre_ref[0, dr, j]
            lam_im = lim_ref[0, dr, j]
            dt = jnp.exp(ldt_ref[0, dr, j])
            mag = jnp.exp(lam_re * dt)
            lb_re = mag * jnp.cos(lam_im * dt)
            lb_im = mag * jnp.sin(lam_im * dt)
            n_re = lb_re - 1.0
            den = lam_re * lam_re + lam_im * lam_im
            q_re = (n_re * lam_re + lb_im * lam_im) / den
            q_im = (lb_im * lam_re - n_re * lam_im) / den
            bt_re = btre_ref[0, dr, j]
            bt_im = btim_ref[0, dr, j]
            bb_re = q_re * bt_re - q_im * bt_im
            bb_im = q_re * bt_im + q_im * bt_re
            bbs[dr] = (bb_re, bb_im)
            c_re = cre_ref[0, dr, j]
            c_im = cim_ref[0, dr, j]
            pw_re = jnp.ones_like(lb_re)
            pw_im = jnp.zeros_like(lb_re)
            for k in range(N_POW):
                xs[dr, k] = (pw_re * bb_re - pw_im * bb_im, pw_re * bb_im + pw_im * bb_re)
                ys[dr, k] = (pw_re * c_re - pw_im * c_im, pw_re * c_im + pw_im * c_re)
                pw_last[dr] = (pw_re, pw_im)
                pw_re, pw_im = pw_re * lb_re - pw_im * lb_im, pw_re * lb_im + pw_im * lb_re

        step = [GROUPS_PER_SLAB * (a // GROUPS_PER_SLAB) + (a % GROUPS_PER_SLAB - j) % GROUPS_PER_SLAB
                for a in range(CHUNK)]
        rows = lambda a: slice(a * SSM_GROUP, (a + 1) * SSM_GROUP)

        decay_ref[0, 0, j:j + 1, :] = pair(pw_last[0][0], pw_last[1][0])[0:1]
        decay_ref[0, 1, j:j + 1, :] = pair(pw_last[0][1], pw_last[1][1])[0:1]

        for a, s in enumerate(step):
            win_ref[0, j, rows(a), :LANES] = pair(xs[0, CHUNK - 1 - s][0], xs[1, s][0]).astype(BF16)
            win_ref[0, j, rows(a), LANES:] = pair(xs[0, CHUNK - 1 - s][1], xs[1, s][1]).astype(BF16)

        z_re = jnp.concatenate([pair(ys[0, t + 1][0], ys[1, CHUNK - t][0]) for t in step], axis=0)
        z_nim = jnp.concatenate([pair(-ys[0, t + 1][1], -ys[1, CHUNK - t][1]) for t in step], axis=0)
        wout_ref[0, j, :LANES, :] = z_re.T.astype(BF16)
        wout_ref[0, j, LANES:, :] = z_nim.T.astype(BF16)

        kf = lax.dot_general(
            pair(bbs[0][0], -bbs[0][1]),
            jnp.concatenate([pair(*ys[0, m]) for m in range(CHUNK)], axis=0),
            nt, precision=lax.Precision.HIGHEST, preferred_element_type=F32)
        kb = lax.dot_general(
            pair(bbs[1][0], -bbs[1][1]),
            jnp.concatenate([pair(*ys[1, CHUNK - 1 - i]) for i in range(CHUNK)], axis=0),
            nt, precision=lax.Precision.HIGHEST, preferred_element_type=F32)
        for a, s in enumerate(step):
            fwd = pltpu.roll(kf, SSM_GROUP * s, axis=1) if s else kf
            shift_b = (n - SSM_GROUP * (CHUNK - 1 - s)) % n
            bwd = pltpu.roll(kb, shift_b, axis=1) if shift_b else kb
            t_nat = (jnp.where(lane_n >= SSM_GROUP * s, fwd, 0.0)
                     + jnp.where(lane_n < SSM_GROUP * (s + 1), bwd, 0.0))
            for h in range(2):
                half = t_nat[:, h * LANES:(h + 1) * LANES]
                if j:
                    half = pltpu.roll(half, SSM_GROUP * j, axis=1)
                toep_ref[0, j, rows(a), h * LANES:(h + 1) * LANES] = half.astype(BF16)


def _ssm_prep(lam_re, lam_im, log_dt, b_re, b_im, c_re, c_im):
    depth = lam_re.shape[0]
    G, P, C = SSM_GROUPS, SSM_STATE, SSM_GROUP
    n = CHUNK * C
    gb = PREP_GROUPS
    twice = lambda a: jnp.concatenate([a, a], axis=-1)
    rep = lambda a: jnp.broadcast_to(twice(a.astype(F32))[:, :, :, None, :], (depth, 2, G, C, LANES))
    ldt = jnp.broadcast_to(log_dt.astype(F32)[:, :, :, None, None], (depth, 2, G, C, LANES))
    bt = lambda a: twice(a.astype(F32).transpose(0, 1, 2, 4, 3))
    spec_in = pl.BlockSpec((1, 2, gb, C, LANES), lambda l, i: (l, 0, i, 0, 0))
    spec_w = pl.BlockSpec((1, gb, n, n), lambda l, i: (l, i, 0, 0))
    shp_w = jax.ShapeDtypeStruct((depth, G, n, n), BF16)
    return pl.pallas_call(
        _ssm_prep_kernel,
        grid=(depth, G // gb),
        in_specs=[spec_in] * 7,
        out_specs=[spec_w, spec_w, spec_w, pl.BlockSpec((1, 2, gb, LANES), lambda l, i: (l, 0, i, 0))],
        out_shape=[shp_w, shp_w, shp_w, jax.ShapeDtypeStruct((depth, 2, G, LANES), F32)],
        compiler_params=pltpu.CompilerParams(
            dimension_semantics=("arbitrary", "arbitrary"), vmem_limit_bytes=VMEM_LIMIT),
        name="ssm_prep",
    )(rep(lam_re), rep(lam_im), ldt, bt(b_re), bt(b_im), twice(c_re.astype(F32)), twice(c_im.astype(F32)))


GROUPS_PER_SLAB = LANES // SSM_GROUP
N_SLABS = W_BRANCH // LANES
RELAYOUT_ROWS = 32


def _ssm_kernel(xa_ref, toep_ref, win_ref, wout_ref, decay_ref, d_ref, o_ref, xf, u_s, st, hs, yq):
    seq = xa_ref.shape[0]
    nchunk = seq // CHUNK
    nrb = nchunk // RELAYOUT_ROWS
    gps = GROUPS_PER_SLAB
    lane_rb = lax.broadcasted_iota(jnp.int32, (RELAYOUT_ROWS, LANES), 1)
    seg_masks = [(lane_rb >= SSM_GROUP * sg) & (lane_rb < SSM_GROUP * (sg + 1)) for sg in range(gps)]
    conv_rows = 512

    for q in range(N_SLABS):
        c_lo = q * LANES

        def conv(i, carry, c_lo=c_lo):
            r0 = pl.multiple_of(i * conv_rows, conv_rows)
            xf[pl.ds(r0, conv_rows), :] = xa_ref[pl.ds(r0, conv_rows), c_lo:c_lo + LANES].astype(F32)
            return carry

        lax.fori_loop(0, seq // conv_rows, conv, 0)

        for hh in range(2):
            def fwd_relayout(rb, carry, hh=hh):
                c0 = pl.multiple_of(rb * RELAYOUT_ROWS, RELAYOUT_ROWS)
                rolled = []
                for r in range(gps):
                    xs = xf[pl.ds(CHUNK * c0 + gps * hh + r, RELAYOUT_ROWS, stride=CHUNK), :]
                    rolled.append(xs if r == 0 else pltpu.roll(xs, SSM_GROUP * r, axis=1))
                for j in range(gps):
                    out = rolled[(0 - j) % gps]
                    for sg in range(1, gps):
                        out = jnp.where(seg_masks[sg], rolled[(sg - j) % gps], out)
                    u_s[j, pl.ds(c0, RELAYOUT_ROWS), hh * LANES:(hh + 1) * LANES] = out.astype(BF16)
                return carry

            lax.fori_loop(0, nrb, fwd_relayout, 0, unroll=2)

        for j in range(gps):
            s_in = jnp.dot(u_s[j], win_ref[gps * q + j], preferred_element_type=F32)
            st[0, pl.ds(j, nchunk, stride=gps), :] = s_in[:, :LANES]
            st[1, pl.ds(j, nchunk, stride=gps), :] = s_in[:, LANES:]

        g_lo = gps * q
        a_re = decay_ref[0, g_lo:g_lo + gps, :]
        a_im = decay_ref[1, g_lo:g_lo + gps, :]
        lo, hi = slice(0, SSM_STATE), slice(SSM_STATE, LANES)

        def scan(i, carry):
            hrf, hif, hrb, hib = carry
            rf = pl.ds(pl.multiple_of(i * gps, gps), gps)
            rb_ = pl.ds(pl.multiple_of((nchunk - 1 - i) * gps, gps), gps)
            hs[0, rf, lo] = hrf[:, lo]
            hs[1, rf, lo] = hif[:, lo]
            hs[0, rb_, hi] = hrb[:, hi]
            hs[1, rb_, hi] = hib[:, hi]
            return (a_re * hrf - a_im * hif + st[0, rf, :], a_re * hif + a_im * hrf + st[1, rf, :],
                    a_re * hrb - a_im * hib + st[0, rb_, :], a_re * hib + a_im * hrb + st[1, rb_, :])

        zero = jnp.zeros((gps, LANES), F32)
        lax.fori_loop(0, nchunk, scan, (zero, zero, zero, zero), unroll=8)

        for j in range(gps):
            g = gps * q + j
            h_in = jnp.concatenate([hs[0, pl.ds(j, nchunk, stride=gps), :].astype(BF16),
                                    hs[1, pl.ds(j, nchunk, stride=gps), :].astype(BF16)], axis=-1)
            yq[j] = (jnp.dot(u_s[j], toep_ref[g], preferred_element_type=F32)
                     + jnp.dot(h_in, wout_ref[g], preferred_element_type=F32))

        d_row = d_ref[:, c_lo:c_lo + LANES]
        for hh in range(2):
            def bwd_relayout(rb, carry, hh=hh, d_row=d_row, q=q):
                c0 = pl.multiple_of(rb * RELAYOUT_ROWS, RELAYOUT_ROWS)
                ys = [yq[j, pl.ds(c0, RELAYOUT_ROWS), hh * LANES:(hh + 1) * LANES] for j in range(gps)]
                for r in range(gps):
                    merged = ys[(0 - r) % gps]
                    for sg in range(1, gps):
                        merged = jnp.where(seg_masks[sg], ys[(sg - r) % gps], merged)
                    if r:
                        merged = pltpu.roll(merged, LANES - SSM_GROUP * r, axis=1)
                    tok = pl.ds(CHUNK * c0 + gps * hh + r, RELAYOUT_ROWS, stride=CHUNK)
                    o_ref[q, tok, :] = merged + d_row * xf[tok, :]
                return carry

            lax.fori_loop(0, nrb, bwd_relayout, 0, unroll=2)


def _ssm(proj, toep, w_in, w_out, decay, d_skip, layer, batch, seq):
    nchunk = seq // CHUNK
    n = CHUNK * SSM_GROUP
    wspec = pl.BlockSpec((None, SSM_GROUPS, n, n), lambda b: (layer, 0, 0, 0), pipeline_mode=pl.Buffered(1))
    return pl.pallas_call(
        _ssm_kernel,
        grid=(batch,),
        in_specs=[
            pl.BlockSpec((seq, W_BRANCH), lambda b: (b, COL_XA // N_SLABS)),
            wspec, wspec, wspec,
            pl.BlockSpec((None, 2, SSM_GROUPS, LANES), lambda b: (layer, 0, 0, 0)),
            pl.BlockSpec((None, 1, W_BRANCH), lambda b: (layer, 0, 0)),
        ],
        out_specs=pl.BlockSpec((N_SLABS, seq, LANES), lambda b: (0, b, 0)),
        out_shape=jax.ShapeDtypeStruct((N_SLABS, batch * seq, LANES), F32),
        scratch_shapes=[
            pltpu.VMEM((seq, LANES), F32),
            pltpu.VMEM((GROUPS_PER_SLAB, nchunk, n), BF16),
            pltpu.VMEM((2, nchunk * GROUPS_PER_SLAB, LANES), F32),
            pltpu.VMEM((2, nchunk * GROUPS_PER_SLAB, LANES), F32),
            pltpu.VMEM((GROUPS_PER_SLAB, nchunk, n), F32),
        ],
        compiler_params=pltpu.CompilerParams(
            dimension_semantics=("arbitrary",), vmem_limit_bytes=VMEM_LIMIT),
        name="ssm_mix",
    )(proj, toep, w_in, w_out, decay, d_skip)


OUT_TM = 512


def _out_proj_kernel(ya_ref, za_ref, yb_ref, yc_ref, x_ref, w_ref, gw_ref, gb_ref, fg_ref, o_ref,
                     w_bf, gw_bf, *, final):
    @pl.when(pl.program_id(0) == 0)
    def _():
        for n in range(MIX_WIDTH // W_BRANCH):
            w_bf[n * W_BRANCH:(n + 1) * W_BRANCH, :] = w_ref[n * W_BRANCH:(n + 1) * W_BRANCH, :].astype(BF16)
        gw_bf[...] = gw_ref[...].astype(BF16)

    y = jnp.concatenate([ya_ref[i] for i in range(N_SLABS)], axis=-1)
    g = _gelu_tanh(y)
    gate = jnp.dot(g.astype(BF16), gw_bf[...], preferred_element_type=F32) + gb_ref[...]
    ya = g * _sigmoid(gate) * _silu(za_ref[...].astype(F32))
    delta = (jnp.dot(ya.astype(BF16), w_bf[:W_BRANCH, :], preferred_element_type=F32)
             + jnp.dot(yb_ref[...], w_bf[W_BRANCH:2 * W_BRANCH, :], preferred_element_type=F32)
             + jnp.dot(yc_ref[...], w_bf[2 * W_BRANCH:, :], preferred_element_type=F32))
    x = x_ref[...] + delta
    if final:
        ms = jnp.mean(x * x, axis=-1, keepdims=True)
        x = x * lax.rsqrt(ms + RMS_EPS) * fg_ref[...]
    o_ref[...] = x


def _out_proj(ya_pre, proj, yb, yc, x2d, w, glu_w, glu_b, final_g, layer, final):
    rows = x2d.shape[0]
    row_blk = lambda width: pl.BlockSpec((OUT_TM, width), lambda i: (i, 0))
    const = lambda shape: pl.BlockSpec((None,) + shape, lambda i: (layer,) + (0,) * len(shape),
                                       pipeline_mode=pl.Buffered(1))
    return pl.pallas_call(
        functools.partial(_out_proj_kernel, final=final),
        grid=(rows // OUT_TM,),
        in_specs=[
            pl.BlockSpec((N_SLABS, OUT_TM, LANES), lambda i: (0, i, 0)),
            pl.BlockSpec((OUT_TM, W_BRANCH), lambda i: (i, COL_ZA // N_SLABS)),
            row_blk(W_BRANCH), row_blk(W_BRANCH), row_blk(D_MODEL),
            const((MIX_WIDTH, D_MODEL)), const((W_BRANCH, W_BRANCH)),
            const((1, W_BRANCH)),
            pl.BlockSpec((None, 1, D_MODEL), lambda i: (0, 0, 0), pipeline_mode=pl.Buffered(1)),
        ],
        out_specs=row_blk(D_MODEL),
        out_shape=jax.ShapeDtypeStruct((rows, D_MODEL), F32),
        scratch_shapes=[pltpu.VMEM((MIX_WIDTH, D_MODEL), BF16), pltpu.VMEM((W_BRANCH, W_BRANCH), BF16)],
        compiler_params=pltpu.CompilerParams(
            dimension_semantics=("arbitrary",), vmem_limit_bytes=VMEM_LIMIT),
        name="out_proj_final" if final else "out_proj",
    )(ya_pre, proj, yb, yc, x2d, w, glu_w, glu_b, final_g)


def kernel(x, norm_g, w_in, w_out, ssm_lam_re, ssm_lam_im, ssm_log_dt, ssm_b_re, ssm_b_im, ssm_c_re,
           ssm_c_im, ssm_d, glu_w, glu_b, na_rpb, t5_bias, final_g):
    batch, seq, _ = x.shape
    depth = w_in.shape[0]
    x2d = x.astype(F32).reshape(batch * seq, D_MODEL)
    toep, s_in, s_out, decay = _ssm_prep(ssm_lam_re, ssm_lam_im, ssm_log_dt, ssm_b_re, ssm_b_im,
                                         ssm_c_re, ssm_c_im)
    dil_bias = _dil_bias_table(t5_bias)
    na_bias = _na_bias_table(na_rpb)
    w_in_f, w_out_f, glu_w_f = w_in.astype(F32), w_out.astype(F32), glu_w.astype(F32)
    row3 = lambda a: a.astype(F32).reshape(a.shape[0], 1, a.shape[1])
    norm_g3, ssm_d3, glu_b3 = row3(norm_g), row3(ssm_d), row3(glu_b)
    final_g3 = final_g.astype(F32).reshape(1, 1, D_MODEL)
    for l in range(depth):
        proj = _in_proj(x2d, norm_g3, w_in_f, l)
        ya_pre = _ssm(proj, toep, s_in, s_out, decay, ssm_d3, l, batch, seq)
        yb = _na(proj, na_bias, l, batch, seq)
        yc = _dil(proj, dil_bias, batch, seq)
        x2d = _out_proj(ya_pre, proj, yb, yc, x2d, w_out_f, glu_w_f, glu_b3, final_g3, l,
                        final=(l == depth - 1))
    return x2d.reshape(batch, seq, D_MODEL).astype(x.dtype)
```

```python
import functools

import numpy as np
import jax
import jax.numpy as jnp
from jax import lax
from jax.experimental import pallas as pl
from jax.experimental.pallas import tpu as pltpu

F32 = jnp.float32
BF16 = jnp.bfloat16

D_MODEL = 1024
HEAD_DIM = 64
W_BRANCH = 512
N_HEADS = W_BRANCH // HEAD_DIM
N_HEAD_PAIRS = N_HEADS // 2
SSM_GROUP = 16
SSM_GROUPS = W_BRANCH // SSM_GROUP
SSM_STATE = 64
GRID_W = 64
NA_ROWS = 8
NA_COLS = 16
DIL_PATTERNS = ((128, 1), (512, 4), (2048, 16))
DIL_HALF = 64
T5_BUCKETS = 32
T5_MAX_DIST = 1024
RMS_EPS = 1e-6
NEG_INF = -1e30
IN_COLS = 10 * W_BRANCH
MIX_WIDTH = 3 * W_BRANCH

COL_XA, COL_ZA, COL_QB, COL_KB, COL_VB, COL_ZB, COL_QC, COL_KC, COL_VC, COL_ZC = (
    4 * i for i in range(10))

LOG2E = float(np.log2(np.e))
Q_SCALE_LOG2 = HEAD_DIM ** -0.5 * LOG2E

LANES = 128
CHUNK = 16
VMEM_LIMIT = 56 * 1024 * 1024


def _sigmoid(z):
    return 0.5 * (1.0 + jnp.tanh(0.5 * z))


def _silu(z):
    return z * _sigmoid(z)


def _gelu_tanh(x):
    return 0.5 * x * (1.0 + jnp.tanh(np.sqrt(2.0 / np.pi).astype(np.float32) * (x + 0.044715 * (x * x * x))))


IN_TM = 512
IN_TN = 512


def _in_proj_kernel(x_ref, g_ref, w_ref, o_ref, w_bf):
    @pl.when(pl.program_id(0) == 0)
    def _():
        for n in range(IN_COLS // IN_TN):
            w_bf[:, n * IN_TN:(n + 1) * IN_TN] = w_ref[:, n * IN_TN:(n + 1) * IN_TN].astype(BF16)

    x = x_ref[...]
    ms = jnp.mean(x * x, axis=-1, keepdims=True)
    h = (x * lax.rsqrt(ms + RMS_EPS) * g_ref[...]).astype(BF16)
    for n in range(IN_COLS // IN_TN):
        o_ref[:, n * IN_TN:(n + 1) * IN_TN] = jnp.dot(
            h, w_bf[:, n * IN_TN:(n + 1) * IN_TN], preferred_element_type=F32).astype(BF16)


def _in_proj(x2d, g, w, layer):
    rows = x2d.shape[0]
    return pl.pallas_call(
        _in_proj_kernel,
        grid=(rows // IN_TM,),
        in_specs=[
            pl.BlockSpec((IN_TM, D_MODEL), lambda i: (i, 0)),
            pl.BlockSpec((None, 1, D_MODEL), lambda i: (layer, 0, 0)),
            pl.BlockSpec((None, D_MODEL, IN_COLS), lambda i: (layer, 0, 0), pipeline_mode=pl.Buffered(1)),
        ],
        out_specs=pl.BlockSpec((IN_TM, IN_COLS), lambda i: (i, 0)),
        out_shape=jax.ShapeDtypeStruct((rows, IN_COLS), BF16),
        scratch_shapes=[pltpu.VMEM((D_MODEL, IN_COLS), BF16)],
        compiler_params=pltpu.CompilerParams(
            dimension_semantics=("arbitrary",), vmem_limit_bytes=VMEM_LIMIT),
        name="in_proj",
    )(x2d, g, w)


def _na_bias_kernel(rpb_ref, o_ref):
    lane = lax.broadcasted_iota(jnp.int32, (GRID_W, LANES), 1)
    j = lax.broadcasted_iota(jnp.int32, (GRID_W, LANES), 0)
    c = lane & (GRID_W - 1)
    col_start = jnp.clip(j - NA_COLS // 2, 0, GRID_W - NA_COLS)
    valid = (c >= col_start) & (c < col_start + NA_COLS)
    first = lane < GRID_W
    even, odd = [], []
    for a in range(2 * NA_ROWS - 1):
        vec = pltpu.roll(rpb_ref[a:a + 1, :], LANES - (NA_COLS - 1), axis=1)
        x = jnp.broadcast_to(vec, (GRID_W, LANES))
        even.append(pltpu.roll(x, 0, axis=1, stride=1, stride_axis=0))
        odd.append(pltpu.roll(x, GRID_W, axis=1, stride=1, stride_axis=0))
    for v in range(NA_ROWS):
        for i in range(NA_ROWS // 2):
            a0 = 2 * i - v + (NA_ROWS - 1)
            tile = jnp.where(first, even[a0], odd[a0 + 1])
            o_ref[v, :, i * LANES:(i + 1) * LANES] = jnp.where(valid, tile * LOG2E, NEG_INF)


def _na_bias_table(rpb):
    depth = rpb.shape[0]
    padded = jnp.pad(rpb.astype(F32), ((0, 0), (0, 0), (0, 1), (0, LANES - (2 * NA_COLS - 1))))
    nkeys = NA_ROWS * GRID_W
    return pl.pallas_call(
        _na_bias_kernel,
        grid=(depth, N_HEADS),
        in_specs=[pl.BlockSpec((None, None, 2 * NA_ROWS, LANES), lambda l, h: (l, h, 0, 0))],
        out_specs=pl.BlockSpec((None, None, NA_ROWS, GRID_W, nkeys), lambda l, h: (l, h, 0, 0, 0)),
        out_shape=jax.ShapeDtypeStruct((depth, N_HEADS, NA_ROWS, GRID_W, nkeys), F32),
        compiler_params=pltpu.CompilerParams(dimension_semantics=("arbitrary", "arbitrary")),
        name="na_bias",
    )(padded)


def _na_kernel(q_ref, k_ref, v_ref, z_ref, bias_ref, o_ref):
    seq = q_ref.shape[0]
    rows = seq // GRID_W
    half_rows = NA_ROWS // 2
    tile_keys = half_rows * GRID_W
    lane = lax.broadcasted_iota(jnp.int32, (GRID_W, LANES), 1)
    first_head = lane < HEAD_DIM
    masks = (first_head, jnp.logical_not(first_head))
    nt = (((1,), (1,)), ((), ()))
    ones = jnp.ones((tile_keys, LANES), BF16)
    row_start = lambda r: min(max(r - NA_ROWS // 2, 0), rows - NA_ROWS)

    for g0 in range(0, rows, NA_GROUP):
        group = range(g0, g0 + NA_GROUP)
        users = {}
        for r in group:
            for half in range(2):
                users.setdefault(row_start(r) + half_rows * half, []).append((r, half))
        qm = {}
        for r in group:
            qb = (q_ref[r * GRID_W:(r + 1) * GRID_W, :].astype(F32) * Q_SCALE_LOG2).astype(BF16)
            for hd in range(2):
                qm[r, hd] = jnp.where(masks[hd], qb, jnp.zeros_like(qb))
        s_half = {}
        for k, us in sorted(users.items()):
            lhs = jnp.concatenate([qm[r, hd] for r, _ in us for hd in range(2)], axis=0)
            part = lax.dot_general(lhs, k_ref[k * GRID_W:k * GRID_W + tile_keys, :], nt,
                                   preferred_element_type=F32)
            for i, (r, half) in enumerate(us):
                for hd in range(2):
                    blk = part[(2 * i + hd) * GRID_W:(2 * i + hd + 1) * GRID_W, :]
                    bias = bias_ref[hd, r - row_start(r), :, half * tile_keys:(half + 1) * tile_keys]
                    s_half[r, hd, half] = (blk + bias).astype(BF16)
        p_half = {}
        for r in group:
            for hd in range(2):
                s0, s1 = s_half[r, hd, 0], s_half[r, hd, 1]
                m = jnp.max(jnp.maximum(s0, s1), axis=-1, keepdims=True)
                p_half[r, hd, 0], p_half[r, hd, 1] = jnp.exp2(s0 - m), jnp.exp2(s1 - m)
        acc = {}
        for k, us in sorted(users.items()):
            lhs = jnp.concatenate([p_half[r, hd, half] for r, half in us for hd in range(2)], axis=0)
            vext = jnp.concatenate([v_ref[k * GRID_W:k * GRID_W + tile_keys, :], ones], axis=-1)
            part = jnp.dot(lhs, vext, preferred_element_type=F32)
            for i, (r, half) in enumerate(us):
                for hd in range(2):
                    blk = part[(2 * i + hd) * GRID_W:(2 * i + hd + 1) * GRID_W, :]
                    acc[r, hd] = blk if (r, hd) not in acc else acc[r, hd] + blk
        for r in group:
            y = jnp.where(first_head, acc[r, 0][:, :LANES], acc[r, 1][:, :LANES])
            d = jnp.where(first_head, acc[r, 0][:, LANES:], acc[r, 1][:, LANES:])
            z = z_ref[r * GRID_W:(r + 1) * GRID_W, :].astype(F32)
            o_ref[r * GRID_W:(r + 1) * GRID_W, :] = (y * (1.0 / d) * _silu(z)).astype(BF16)


NA_GROUP = 16


def _na(proj, bias, layer, batch, seq):
    def col(c0):
        return pl.BlockSpec((seq, LANES), lambda b, p, c0=c0: (b, c0 + p))

    return pl.pallas_call(
        _na_kernel,
        grid=(batch, N_HEAD_PAIRS),
        in_specs=[
            col(COL_QB), col(COL_KB), col(COL_VB), col(COL_ZB),
            pl.BlockSpec((None, 2, NA_ROWS, GRID_W, NA_ROWS * GRID_W), lambda b, p: (layer, p, 0, 0, 0)),
        ],
        out_specs=pl.BlockSpec((seq, LANES), lambda b, p: (b, p)),
        out_shape=jax.ShapeDtypeStruct((batch * seq, W_BRANCH), BF16),
        compiler_params=pltpu.CompilerParams(
            dimension_semantics=("arbitrary", "arbitrary"), vmem_limit_bytes=VMEM_LIMIT),
        name="na_attn",
    )(proj, proj, proj, proj, bias)


DIL_QB = 128
DIL_KW = 256
N_VARIANTS = 3


def _t5_bucket(rel):
    nb = T5_BUCKETS // 2
    max_exact = nb // 2
    n = np.abs(rel)
    large = max_exact + (np.log(np.maximum(n, 1) / max_exact) / np.log(T5_MAX_DIST / max_exact)
                         * (nb - max_exact)).astype(np.int32)
    large = np.minimum(large, nb - 1)
    return (np.where(rel > 0, nb, 0) + np.where(n < max_exact, n, large)).astype(np.int32)


def _dil_bias_table(t5_bias):
    n_pat = len(DIL_PATTERNS)
    lane = np.arange(DIL_VEC)
    n = np.where(lane < DIL_KW, lane, lane - DIL_VEC)
    in_range = (lane < DIL_KW) | (lane > DIL_VEC - DIL_QB)
    pick = np.zeros((n_pat, N_VARIANTS, T5_BUCKETS, DIL_VEC), np.float32)
    neg = np.zeros((n_pat, N_VARIANTS, 1, DIL_VEC), np.float32)
    for pi, (_, d) in enumerate(DIL_PATTERNS):
        for var in range(N_VARIANTS):
            step = n - DIL_HALF * var
            ok = in_range & (np.abs(step) <= DIL_HALF)
            bucket = _t5_bucket(d * np.clip(step, -DIL_HALF, DIL_HALF))
            pick[pi, var, bucket[ok], lane[ok]] = 1.0
            neg[pi, var, 0, ~ok] = NEG_INF
    vecs = jnp.einsum('bh,pvbn->hpvn', t5_bias.astype(F32), pick,
                      precision=lax.Precision.HIGHEST) * LOG2E + neg[:, :, 0][None]
    return pl.pallas_call(
        _dil_bias_kernel,
        grid=(N_HEADS,),
        in_specs=[pl.BlockSpec((None, n_pat * N_VARIANTS, DIL_VEC), lambda h: (h, 0, 0))],
        out_specs=pl.BlockSpec((None, n_pat, N_VARIANTS, DIL_QB, DIL_KW), lambda h: (h, 0, 0, 0, 0)),
        out_shape=jax.ShapeDtypeStruct((N_HEADS, n_pat, N_VARIANTS, DIL_QB, DIL_KW), F32),
        compiler_params=pltpu.CompilerParams(dimension_semantics=("arbitrary",)),
        name="dil_bias",
    )(vecs.reshape(N_HEADS, n_pat * N_VARIANTS, DIL_VEC))


DIL_VEC = 512


def _dil_bias_kernel(vec_ref, o_ref):
    for pi in range(len(DIL_PATTERNS)):
        for var in range(N_VARIANTS):
            i = pi * N_VARIANTS + var
            x = jnp.broadcast_to(vec_ref[i:i + 1, :], (DIL_QB, DIL_VEC))
            o_ref[pi, var] = pltpu.roll(x, 0, axis=1, stride=1, stride_axis=0)[:, :DIL_KW]


DIL_GROUP = 16
DIL_GROUP_FIRST = 16


def _dil_kernel(q_ref, k_ref, v_ref, z_ref, bias_ref, o_ref, qf, kf, vf, qg, kg, vg, qd, kd, vd,
                m_s, l_s, acc_s):
    seq = q_ref.shape[0]
    conv_rows = 256
    lane = lax.broadcasted_iota(jnp.int32, (DIL_QB, LANES), 1)
    first_head = lane < HEAD_DIM
    nt = (((1,), (1,)), ((), ()))
    ones = jnp.ones((DIL_KW, LANES), BF16)

    def conv(i, carry):
        r0 = pl.multiple_of(i * conv_rows, conv_rows)
        qf[pl.ds(r0, conv_rows), :] = q_ref[pl.ds(r0, conv_rows), :].astype(F32) * Q_SCALE_LOG2
        kf[pl.ds(r0, conv_rows), :] = k_ref[pl.ds(r0, conv_rows), :].astype(F32)
        vf[pl.ds(r0, conv_rows), :] = v_ref[pl.ds(r0, conv_rows), :].astype(F32)
        return carry

    lax.fori_loop(0, seq // conv_rows, conv, 0)

    n4 = seq // 4
    n16 = seq // 16

    def to_order4(i, carry):
        dst = pl.multiple_of(i * conv_rows, conv_rows)
        src = pl.ds(dst // n4 + 4 * (dst % n4), conv_rows, stride=4)
        for a, b in ((qf, qg), (kf, kg), (vf, vg)):
            b[pl.ds(dst, conv_rows), :] = a[src, :]
        return carry

    lax.fori_loop(0, seq // conv_rows, to_order4, 0, unroll=2)

    def order16_rows_in_order4(rho):
        return pl.ds((rho % 4) * n4 + rho // 4, n16, stride=4)

    def to_order16(rho, carry):
        src = order16_rows_in_order4(rho)
        dst = pl.ds(pl.multiple_of(rho * n16, n16), n16)
        for a, b in ((qg, qd), (kg, kd), (vg, vd)):
            b[dst, :] = a[src, :].astype(BF16)
        return carry

    lax.fori_loop(0, 16, to_order16, 0, unroll=2)

    def window(src, res, start, size, n_sub):
        return src[pl.ds(pl.multiple_of(res * n_sub + start, DIL_HALF), size), :]

    plan = ((2, 16, (qd, kd, vd), (m_s, l_s, acc_s)),
            (1, 4, (qd, kd, vd), (qf, kf, vf)),
            (0, 1, (qd, k_ref, v_ref), (m_s, l_s, acc_s)))

    for step, (pi, d, (q_src, k_src, v_src), (m_st, l_st, acc_st)) in enumerate(plan):
        n_sub = seq // d
        nblk = n_sub // DIL_QB

        if step == 1:
            def state_to_order4(rho, carry):
                dst = order16_rows_in_order4(rho)
                src = pl.ds(pl.multiple_of(rho * n16, n16), n16)
                for a, b in ((m_s, qf), (l_s, kf), (acc_s, vf)):
                    b[dst, :] = a[src, :]
                return carry

            lax.fori_loop(0, 16, state_to_order4, 0, unroll=2)

            def cast_order4(i, carry):
                r0 = pl.multiple_of(i * conv_rows, conv_rows)
                for a, b in ((qg, qd), (kg, kd), (vg, vd)):
                    b[pl.ds(r0, conv_rows), :] = a[pl.ds(r0, conv_rows), :].astype(BF16)
                return carry

            lax.fori_loop(0, seq // conv_rows, cast_order4, 0)
        if step == 2:
            def state_to_tokens(i, carry):
                src = pl.multiple_of(i * conv_rows, conv_rows)
                dst = pl.ds(src // n4 + 4 * (src % n4), conv_rows, stride=4)
                for a, b in ((qf, m_s), (kf, l_s), (vf, acc_s)):
                    b[dst, :] = a[pl.ds(src, conv_rows), :]
                qd[pl.ds(src, conv_rows), :] = (
                    q_ref[pl.ds(src, conv_rows), :].astype(F32) * Q_SCALE_LOG2).astype(BF16)
                return carry

            lax.fori_loop(0, seq // conv_rows, state_to_tokens, 0, unroll=2)

        n_group = DIL_GROUP_FIRST if step == 0 else DIL_GROUP

        def group(gi, carry, pi=pi, n_sub=n_sub, nblk=nblk, first=(step == 0), q_src=q_src,
                  k_src=k_src, v_src=v_src, m_st=m_st, l_st=l_st, acc_st=acc_st, n_group=n_group):
            units = []
            for i in range(n_group):
                u = gi * n_group + i
                res = u // nblk
                i0 = (u % nblk) * DIL_QB
                ks = jnp.clip(i0 - DIL_HALF, 0, n_sub - DIL_KW)
                units.append((res, i0, ks, (i0 - ks) // DIL_HALF))
            s_tiles, m_tiles = [], []
            for res, i0, ks, var in units:
                qb = window(q_src, res, i0, DIL_QB, n_sub)
                kw = window(k_src, res, ks, DIL_KW, n_sub)
                for hd in range(2):
                    keep = first_head if hd == 0 else jnp.logical_not(first_head)
                    qm = jnp.where(keep, qb, jnp.zeros_like(qb))
                    s = lax.dot_general(qm, kw, nt, preferred_element_type=F32) + bias_ref[hd, pi, var]
                    s = s.astype(BF16)
                    s_tiles.append(s)
                    m_tiles.append(jnp.max(s, axis=-1, keepdims=True))
            p_tiles = [jnp.exp2(s - m) for s, m in zip(s_tiles, m_tiles)]
            for i, (res, i0, ks, var) in enumerate(units):
                vw = window(v_src, res, ks, DIL_KW, n_sub)
                vext = jnp.concatenate([vw, ones], axis=-1)
                r0 = jnp.dot(p_tiles[2 * i], vext, preferred_element_type=F32)
                r1 = jnp.dot(p_tiles[2 * i + 1], vext, preferred_element_type=F32)
                m_cur = jnp.where(first_head, m_tiles[2 * i].astype(F32), m_tiles[2 * i + 1].astype(F32))
                l_cur = jnp.where(first_head, r0[:, LANES:], r1[:, LANES:])
                acc_cur = jnp.where(first_head, r0[:, :LANES], r1[:, :LANES])
                q_rows = pl.ds(pl.multiple_of(res * n_sub + i0, DIL_HALF), DIL_QB)
                if first:
                    m_st[q_rows, :] = m_cur
                    l_st[q_rows, :] = l_cur
                    acc_st[q_rows, :] = acc_cur
                else:
                    m_old = m_st[q_rows, :]
                    m_new = jnp.maximum(m_old, m_cur)
                    a_old = jnp.exp2(m_old - m_new)
                    a_cur = jnp.exp2(m_cur - m_new)
                    m_st[q_rows, :] = m_new
                    l_st[q_rows, :] = a_old * l_st[q_rows, :] + a_cur * l_cur
                    acc_st[q_rows, :] = a_old * acc_st[q_rows, :] + a_cur * acc_cur
            return carry

        lax.fori_loop(0, seq // DIL_QB // n_group, group, 0)

    def fin(i, carry):
        r0 = pl.multiple_of(i * conv_rows, conv_rows)
        z = z_ref[pl.ds(r0, conv_rows), :].astype(F32)
        y = acc_s[pl.ds(r0, conv_rows), :] * (1.0 / l_s[pl.ds(r0, conv_rows), :])
        o_ref[pl.ds(r0, conv_rows), :] = (y * _silu(z)).astype(BF16)
        return carry

    lax.fori_loop(0, seq // conv_rows, fin, 0, unroll=2)


def _dil(proj, bias, batch, seq):
    def col(c0):
        return pl.BlockSpec((seq, LANES), lambda b, p, c0=c0: (b, c0 + p))

    scratch = ([pltpu.VMEM((seq, LANES), F32) for _ in range(6)]
               + [pltpu.VMEM((seq, LANES), BF16) for _ in range(3)]
               + [pltpu.VMEM((seq, LANES), F32) for _ in range(3)])
    return pl.pallas_call(
        _dil_kernel,
        grid=(batch, N_HEAD_PAIRS),
        in_specs=[
            col(COL_QC), col(COL_KC), col(COL_VC), col(COL_ZC),
            pl.BlockSpec((2, len(DIL_PATTERNS), N_VARIANTS, DIL_QB, DIL_KW),
                         lambda b, p: (p, 0, 0, 0, 0)),
        ],
        out_specs=pl.BlockSpec((seq, LANES), lambda b, p: (b, p)),
        out_shape=jax.ShapeDtypeStruct((batch * seq, W_BRANCH), BF16),
        scratch_shapes=scratch,
        compiler_params=pltpu.CompilerParams(
            dimension_semantics=("arbitrary", "arbitrary"), vmem_limit_bytes=VMEM_LIMIT),
        name="dil_attn",
    )(proj, proj, proj, proj, bias)


N_POW = CHUNK + 1
PREP_GROUPS = 8


def _ssm_prep_kernel(lre_ref, lim_ref, ldt_ref, btre_ref, btim_ref, cre_ref, cim_ref,
                     toep_ref, win_ref, wout_ref, decay_ref):
    n = CHUNK * SSM_GROUP
    lane = lax.broadcasted_iota(jnp.int32, (SSM_GROUP, LANES), 1)
    lo = lane < SSM_STATE
    lane_n = lax.broadcasted_iota(jnp.int32, (SSM_GROUP, n), 1)
    nt = (((1,), (1,)), ((), ()))
    pair = lambda x, y: jnp.where(lo, x, y)

    for j in range(PREP_GROUPS):
        xs, ys, bbs, pw_last = {}, {}, {}, {}
        for dr in range(2):
            lam_re = lre_ref[0, dr, j]
            lam_im = lim_ref[0, dr, j]
            dt = jnp.exp(ldt_ref[0, dr, j])
            mag = jnp.exp(lam_re * dt)
            lb_re = mag * jnp.cos(lam_im * dt)
            lb_im = mag * jnp.sin(lam_im * dt)
            n_re = lb_re - 1.0
            den = lam_re * lam_re + lam_im * lam_im
            q_re = (n_re * lam_re + lb_im * lam_im) / den
            q_im = (lb_im * lam_re - n_re * lam_im) / den
            bt_re = btre_ref[0, dr, j]
            bt_im = btim_ref[0, dr, j]
            bb_re = q_re * bt_re - q_im * bt_im
            bb_im = q_re * bt_im + q_im * bt_re
            bbs[dr] = (bb_re, bb_im)
            c_re = cre_ref[0, dr, j]
            c_im = cim_ref[0, dr, j]
            pw_re = jnp.ones_like(lb_re)
            pw_im = jnp.zeros_like(lb_re)
            for k in range(N_POW):
                xs[dr, k] = (pw_re * bb_re - pw_im * bb_im, pw_re * bb_im + pw_im * bb_re)
                ys[dr, k] = (pw_re * c_re - pw_im * c_im, pw_re * c_im + pw_im * c_re)
                pw_last[dr] = (pw_re, pw_im)
                pw_re, pw_im = pw_re * lb_re - pw_im * lb_im, pw_re * lb_im + pw_im * lb_re

        step = [GROUPS_PER_SLAB * (a // GROUPS_PER_SLAB) + (a % GROUPS_PER_SLAB - j) % GROUPS_PER_SLAB
                for a in range(CHUNK)]
        rows = lambda a: slice(a * SSM_GROUP, (a + 1) * SSM_GROUP)

        decay_ref[0, 0, j:j + 1, :] = pair(pw_last[0][0], pw_last[1][0])[0:1]
        decay_ref[0, 1, j:j + 1, :] = pair(pw_last[0][1], pw_last[1][1])[0:1]

        for a, s in enumerate(step):
            win_ref[0, j, rows(a), :LANES] = pair(xs[0, CHUNK - 1 - s][0], xs[1, s][0]).astype(BF16)
            win_ref[0, j, rows(a), LANES:] = pair(xs[0, CHUNK - 1 - s][1], xs[1, s][1]).astype(BF16)

        z_re = jnp.concatenate([pair(ys[0, t + 1][0], ys[1, CHUNK - t][0]) for t in step], axis=0)
        z_nim = jnp.concatenate([pair(-ys[0, t + 1][1], -ys[1, CHUNK - t][1]) for t in step], axis=0)
        wout_ref[0, j, :LANES, :] = z_re.T.astype(BF16)
        wout_ref[0, j, LANES:, :] = z_nim.T.astype(BF16)

        kf = lax.dot_general(
            pair(bbs[0][0], -bbs[0][1]),
            jnp.concatenate([pair(*ys[0, m]) for m in range(CHUNK)], axis=0),
            nt, precision=lax.Precision.HIGHEST, preferred_element_type=F32)
        kb = lax.dot_general(
            pair(bbs[1][0], -bbs[1][1]),
            jnp.concatenate([pair(*ys[1, CHUNK - 1 - i]) for i in range(CHUNK)], axis=0),
            nt, precision=lax.Precision.HIGHEST, preferred_element_type=F32)
        for a, s in enumerate(step):
            fwd = pltpu.roll(kf, SSM_GROUP * s, axis=1) if s else kf
            shift_b = (n - SSM_GROUP * (CHUNK - 1 - s)) % n
            bwd = pltpu.roll(kb, shift_b, axis=1) if shift_b else kb
            t_nat = (jnp.where(lane_n >= SSM_GROUP * s, fwd, 0.0)
                     + jnp.where(lane_n < SSM_GROUP * (s + 1), bwd, 0.0))
            for h in range(2):
                half = t_nat[:, h * LANES:(h + 1) * LANES]
                if j:
                    half = pltpu.roll(half, SSM_GROUP * j, axis=1)
                toep_ref[0, j, rows(a), h * LANES:(h + 1) * LANES] = half.astype(BF16)


def _ssm_prep(lam_re, lam_im, log_dt, b_re, b_im, c_re, c_im):
    depth = lam_re.shape[0]
    G, P, C = SSM_GROUPS, SSM_STATE, SSM_GROUP
    n = CHUNK * C
    gb = PREP_GROUPS
    twice = lambda a: jnp.concatenate([a, a], axis=-1)
    rep = lambda a: jnp.broadcast_to(twice(a.astype(F32))[:, :, :, None, :], (depth, 2, G, C, LANES))
    ldt = jnp.broadcast_to(log_dt.astype(F32)[:, :, :, None, None], (depth, 2, G, C, LANES))
    bt = lambda a: twice(a.astype(F32).transpose(0, 1, 2, 4, 3))
    spec_in = pl.BlockSpec((1, 2, gb, C, LANES), lambda l, i: (l, 0, i, 0, 0))
    spec_w = pl.BlockSpec((1, gb, n, n), lambda l, i: (l, i, 0, 0))
    shp_w = jax.ShapeDtypeStruct((depth, G, n, n), BF16)
    return pl.pallas_call(
        _ssm_prep_kernel,
        grid=(depth, G // gb),
        in_specs=[spec_in] * 7,
        out_specs=[spec_w, spec_w, spec_w, pl.BlockSpec((1, 2, gb, LANES), lambda l, i: (l, 0, i, 0))],
        out_shape=[shp_w, shp_w, shp_w, jax.ShapeDtypeStruct((depth, 2, G, LANES), F32)],
        compiler_params=pltpu.CompilerParams(
            dimension_semantics=("arbitrary", "arbitrary"), vmem_limit_bytes=VMEM_LIMIT),
        name="ssm_prep",
    )(rep(lam_re), rep(lam_im), ldt, bt(b_re), bt(b_im), twice(c_re.astype(F32)), twice(c_im.astype(F32)))


GROUPS_PER_SLAB = LANES // SSM_GROUP
N_SLABS = W_BRANCH // LANES
RELAYOUT_ROWS = 32


def _ssm_kernel(xa_ref, toep_ref, win_ref, wout_ref, decay_ref, d_ref, o_ref, xf, u_s, st, hs, yq):
    seq = xa_ref.shape[0]
    nchunk = seq // CHUNK
    nrb = nchunk // RELAYOUT_ROWS
    gps = GROUPS_PER_SLAB
    lane_rb = lax.broadcasted_iota(jnp.int32, (RELAYOUT_ROWS, LANES), 1)
    seg_masks = [(lane_rb >= SSM_GROUP * sg) & (lane_rb < SSM_GROUP * (sg + 1)) for sg in range(gps)]
    conv_rows = 512

    for q in range(N_SLABS):
        c_lo = q * LANES

        def conv(i, carry, c_lo=c_lo):
            r0 = pl.multiple_of(i * conv_rows, conv_rows)
            xf[pl.ds(r0, conv_rows), :] = xa_ref[pl.ds(r0, conv_rows), c_lo:c_lo + LANES].astype(F32)
            return carry

        lax.fori_loop(0, seq // conv_rows, conv, 0)

        for hh in range(2):
            def fwd_relayout(rb, carry, hh=hh):
                c0 = pl.multiple_of(rb * RELAYOUT_ROWS, RELAYOUT_ROWS)
                rolled = []
                for r in range(gps):
                    xs = xf[pl.ds(CHUNK * c0 + gps * hh + r, RELAYOUT_ROWS, stride=CHUNK), :]
                    rolled.append(xs if r == 0 else pltpu.roll(xs, SSM_GROUP * r, axis=1))
                for j in range(gps):
                    out = rolled[(0 - j) % gps]
                    for sg in range(1, gps):
                        out = jnp.where(seg_masks[sg], rolled[(sg - j) % gps], out)
                    u_s[j, pl.ds(c0, RELAYOUT_ROWS), hh * LANES:(hh + 1) * LANES] = out.astype(BF16)
                return carry

            lax.fori_loop(0, nrb, fwd_relayout, 0, unroll=2)

        for j in range(gps):
            s_in = jnp.dot(u_s[j], win_ref[gps * q + j], preferred_element_type=F32)
            st[0, pl.ds(j, nchunk, stride=gps), :] = s_in[:, :LANES]
            st[1, pl.ds(j, nchunk, stride=gps), :] = s_in[:, LANES:]

        g_lo = gps * q
        a_re = decay_ref[0, g_lo:g_lo + gps, :]
        a_im = decay_ref[1, g_lo:g_lo + gps, :]
        lo, hi = slice(0, SSM_STATE), slice(SSM_STATE, LANES)

        def scan(i, carry):
            hrf, hif, hrb, hib = carry
            rf = pl.ds(pl.multiple_of(i * gps, gps), gps)
            rb_ = pl.ds(pl.multiple_of((nchunk - 1 - i) * gps, gps), gps)
            hs[0, rf, lo] = hrf[:, lo]
            hs[1, rf, lo] = hif[:, lo]
            hs[0, rb_, hi] = hrb[:, hi]
            hs[1, rb_, hi] = hib[:, hi]
            return (a_re * hrf - a_im * hif + st[0, rf, :], a_re * hif + a_im * hrf + st[1, rf, :],
                    a_re * hrb - a_im * hib + st[0, rb_, :], a_re * hib + a_im * hrb + st[1, rb_, :])

        zero = jnp.zeros((gps, LANES), F32)
        lax.fori_loop(0, nchunk, scan, (zero, zero, zero, zero), unroll=8)

        for j in range(gps):
            g = gps * q + j
            h_in = jnp.concatenate([hs[0, pl.ds(j, nchunk, stride=gps), :].astype(BF16),
                                    hs[1, pl.ds(j, nchunk, stride=gps), :].astype(BF16)], axis=-1)
            yq[j] = (jnp.dot(u_s[j], toep_ref[g], preferred_element_type=F32)
                     + jnp.dot(h_in, wout_ref[g], preferred_element_type=F32))

        d_row = d_ref[:, c_lo:c_lo + LANES]
        for hh in range(2):
            def bwd_relayout(rb, carry, hh=hh, d_row=d_row, q=q):
                c0 = pl.multiple_of(rb * RELAYOUT_ROWS, RELAYOUT_ROWS)
                ys = [yq[j, pl.ds(c0, RELAYOUT_ROWS), hh * LANES:(hh + 1) * LANES] for j in range(gps)]
                for r in range(gps):
                    merged = ys[(0 - r) % gps]
                    for sg in range(1, gps):
                        merged = jnp.where(seg_masks[sg], ys[(sg - r) % gps], merged)
                    if r:
                        merged = pltpu.roll(merged, LANES - SSM_GROUP * r, axis=1)
                    tok = pl.ds(CHUNK * c0 + gps * hh + r, RELAYOUT_ROWS, stride=CHUNK)
                    o_ref[q, tok, :] = merged + d_row * xf[tok, :]
                return carry

            lax.fori_loop(0, nrb, bwd_relayout, 0, unroll=2)


def _ssm(proj, toep, w_in, w_out, decay, d_skip, layer, batch, seq):
    nchunk = seq // CHUNK
    n = CHUNK * SSM_GROUP
    wspec = pl.BlockSpec((None, SSM_GROUPS, n, n), lambda b: (layer, 0, 0, 0), pipeline_mode=pl.Buffered(1))
    return pl.pallas_call(
        _ssm_kernel,
        grid=(batch,),
        in_specs=[
            pl.BlockSpec((seq, W_BRANCH), lambda b: (b, COL_XA // N_SLABS)),
            wspec, wspec, wspec,
            pl.BlockSpec((None, 2, SSM_GROUPS, LANES), lambda b: (layer, 0, 0, 0)),
            pl.BlockSpec((None, 1, W_BRANCH), lambda b: (layer, 0, 0)),
        ],
        out_specs=pl.BlockSpec((N_SLABS, seq, LANES), lambda b: (0, b, 0)),
        out_shape=jax.ShapeDtypeStruct((N_SLABS, batch * seq, LANES), F32),
        scratch_shapes=[
            pltpu.VMEM((seq, LANES), F32),
            pltpu.VMEM((GROUPS_PER_SLAB, nchunk, n), BF16),
            pltpu.VMEM((2, nchunk * GROUPS_PER_SLAB, LANES), F32),
            pltpu.VMEM((2, nchunk * GROUPS_PER_SLAB, LANES), F32),
            pltpu.VMEM((GROUPS_PER_SLAB, nchunk, n), F32),
        ],
        compiler_params=pltpu.CompilerParams(
            dimension_semantics=("arbitrary",), vmem_limit_bytes=VMEM_LIMIT),
        name="ssm_mix",
    )(proj, toep, w_in, w_out, decay, d_skip)


OUT_TM = 512


def _out_proj_kernel(ya_ref, za_ref, yb_ref, yc_ref, x_ref, w_ref, gw_ref, gb_ref, fg_ref, o_ref,
                     w_bf, gw_bf, *, final):
    @pl.when(pl.program_id(0) == 0)
    def _():
        for n in range(MIX_WIDTH // W_BRANCH):
            w_bf[n * W_BRANCH:(n + 1) * W_BRANCH, :] = w_ref[n * W_BRANCH:(n + 1) * W_BRANCH, :].astype(BF16)
        gw_bf[...] = gw_ref[...].astype(BF16)

    y = jnp.concatenate([ya_ref[i] for i in range(N_SLABS)], axis=-1)
    g = _gelu_tanh(y)
    gate = jnp.dot(g.astype(BF16), gw_bf[...], preferred_element_type=F32) + gb_ref[...]
    ya = g * _sigmoid(gate) * _silu(za_ref[...].astype(F32))
    delta = (jnp.dot(ya.astype(BF16), w_bf[:W_BRANCH, :], preferred_element_type=F32)
             + jnp.dot(yb_ref[...], w_bf[W_BRANCH:2 * W_BRANCH, :], preferred_element_type=F32)
             + jnp.dot(yc_ref[...], w_bf[2 * W_BRANCH:, :], preferred_element_type=F32))
    x = x_ref[...] + delta
    if final:
        ms = jnp.mean(x * x, axis=-1, keepdims=True)
        x = x * lax.rsqrt(ms + RMS_EPS) * fg_ref[...]
    o_ref[...] = x


def _out_proj(ya_pre, proj, yb, yc, x2d, w, glu_w, glu_b, final_g, layer, final):
    rows = x2d.shape[0]
    row_blk = lambda width: pl.BlockSpec((OUT_TM, width), lambda i: (i, 0))
    const = lambda shape: pl.BlockSpec((None,) + shape, lambda i: (layer,) + (0,) * len(shape),
                                       pipeline_mode=pl.Buffered(1))
    return pl.pallas_call(
        functools.partial(_out_proj_kernel, final=final),
        grid=(rows // OUT_TM,),
        in_specs=[
            pl.BlockSpec((N_SLABS, OUT_TM, LANES), lambda i: (0, i, 0)),
            pl.BlockSpec((OUT_TM, W_BRANCH), lambda i: (i, COL_ZA // N_SLABS)),
            row_blk(W_BRANCH), row_blk(W_BRANCH), row_blk(D_MODEL),
            const((MIX_WIDTH, D_MODEL)), const((W_BRANCH, W_BRANCH)),
            const((1, W_BRANCH)),
            pl.BlockSpec((None, 1, D_MODEL), lambda i: (0, 0, 0), pipeline_mode=pl.Buffered(1)),
        ],
        out_specs=row_blk(D_MODEL),
        out_shape=jax.ShapeDtypeStruct((rows, D_MODEL), F32),
        scratch_shapes=[pltpu.VMEM((MIX_WIDTH, D_MODEL), BF16), pltpu.VMEM((W_BRANCH, W_BRANCH), BF16)],
        compiler_params=pltpu.CompilerParams(
            dimension_semantics=("arbitrary",), vmem_limit_bytes=VMEM_LIMIT),
        name="out_proj_final" if final else "out_proj",
    )(ya_pre, proj, yb, yc, x2d, w, glu_w, glu_b, final_g)


def kernel(x, norm_g, w_in, w_out, ssm_lam_re, ssm_lam_im, ssm_log_dt, ssm_b_re, ssm_b_im, ssm_c_re,
           ssm_c_im, ssm_d, glu_w, glu_b, na_rpb, t5_bias, final_g):
    batch, seq, _ = x.shape
    depth = w_in.shape[0]
    x2d = x.astype(F32).reshape(batch * seq, D_MODEL)
    toep, s_in, s_out, decay = _ssm_prep(ssm_lam_re, ssm_lam_im, ssm_log_dt, ssm_b_re, ssm_b_im,
                                         ssm_c_re, ssm_c_im)
    dil_bias = _dil_bias_table(t5_bias)
    na_bias = _na_bias_table(na_rpb)
    w_in_f, w_out_f, glu_w_f = w_in.astype(F32), w_out.astype(F32), glu_w.astype(F32)
    row3 = lambda a: a.astype(F32).reshape(a.shape[0], 1, a.shape[1])
    norm_g3, ssm_d3, glu_b3 = row3(norm_g), row3(ssm_d), row3(glu_b)
    final_g3 = final_g.astype(F32).reshape(1, 1, D_MODEL)
    for l in range(depth):
        proj = _in_proj(x2d, norm_g3, w_in_f, l)
        ya_pre = _ssm(proj, toep, s_in, s_out, decay, ssm_d3, l, batch, seq)
        yb = _na(proj, na_bias, l, batch, seq)
        yc = _dil(proj, dil_bias, batch, seq)
        x2d = _out_proj(ya_pre, proj, yb, yc, x2d, w_out_f, glu_w_f, glu_b3, final_g3, l,
                        final=(l == depth - 1))
    return x2d.reshape(batch, seq, D_MODEL).astype(x.dtype)
```

```python
import functools

import numpy as np
import jax
import jax.numpy as jnp
from jax import lax
from jax.experimental import pallas as pl
from jax.experimental.pallas import tpu as pltpu

F32 = jnp.float32
BF16 = jnp.bfloat16

D_MODEL = 1024
HEAD_DIM = 64
W_BRANCH = 512
N_HEADS = W_BRANCH // HEAD_DIM
N_HEAD_PAIRS = N_HEADS // 2
SSM_GROUP = 16
SSM_GROUPS = W_BRANCH // SSM_GROUP
SSM_STATE = 64
GRID_W = 64
NA_ROWS = 8
NA_COLS = 16
DIL_PATTERNS = ((128, 1), (512, 4), (2048, 16))
DIL_HALF = 64
T5_BUCKETS = 32
T5_MAX_DIST = 1024
RMS_EPS = 1e-6
NEG_INF = -1e30
IN_COLS = 10 * W_BRANCH
MIX_WIDTH = 3 * W_BRANCH

COL_XA, COL_ZA, COL_QB, COL_KB, COL_VB, COL_ZB, COL_QC, COL_KC, COL_VC, COL_ZC = (
    4 * i for i in range(10))

LOG2E = float(np.log2(np.e))
Q_SCALE_LOG2 = HEAD_DIM ** -0.5 * LOG2E

LANES = 128
CHUNK = 16
VMEM_LIMIT = 56 * 1024 * 1024


def _sigmoid(z):
    return 0.5 * (1.0 + jnp.tanh(0.5 * z))


def _silu(z):
    return z * _sigmoid(z)


def _gelu_tanh(x):
    return 0.5 * x * (1.0 + jnp.tanh(np.sqrt(2.0 / np.pi).astype(np.float32) * (x + 0.044715 * (x * x * x))))


IN_TM = 512
IN_TN = 512


def _in_proj_kernel(x_ref, g_ref, w_ref, o_ref, w_bf):
    @pl.when(pl.program_id(0) == 0)
    def _():
        for n in range(IN_COLS // IN_TN):
            w_bf[:, n * IN_TN:(n + 1) * IN_TN] = w_ref[:, n * IN_TN:(n + 1) * IN_TN].astype(BF16)

    x = x_ref[...]
    ms = jnp.mean(x * x, axis=-1, keepdims=True)
    h = (x * lax.rsqrt(ms + RMS_EPS) * g_ref[...]).astype(BF16)
    for n in range(IN_COLS // IN_TN):
        o_ref[:, n * IN_TN:(n + 1) * IN_TN] = jnp.dot(
            h, w_bf[:, n * IN_TN:(n + 1) * IN_TN], preferred_element_type=F32).astype(BF16)


def _in_proj(x2d, g, w, layer):
    rows = x2d.shape[0]
    return pl.pallas_call(
        _in_proj_kernel,
        grid=(rows // IN_TM,),
        in_specs=[
            pl.BlockSpec((IN_TM, D_MODEL), lambda i: (i, 0)),
            pl.BlockSpec((None, 1, D_MODEL), lambda i: (layer, 0, 0)),
            pl.BlockSpec((None, D_MODEL, IN_COLS), lambda i: (layer, 0, 0), pipeline_mode=pl.Buffered(1)),
        ],
        out_specs=pl.BlockSpec((IN_TM, IN_COLS), lambda i: (i, 0)),
        out_shape=jax.ShapeDtypeStruct((rows, IN_COLS), BF16),
        scratch_shapes=[pltpu.VMEM((D_MODEL, IN_COLS), BF16)],
        compiler_params=pltpu.CompilerParams(
            dimension_semantics=("arbitrary",), vmem_limit_bytes=VMEM_LIMIT),
        name="in_proj",
    )(x2d, g, w)


def _na_bias_kernel(rpb_ref, o_ref):
    lane = lax.broadcasted_iota(jnp.int32, (GRID_W, LANES), 1)
    j = lax.broadcasted_iota(jnp.int32, (GRID_W, LANES), 0)
    c = lane & (GRID_W - 1)
    col_start = jnp.clip(j - NA_COLS // 2, 0, GRID_W - NA_COLS)
    valid = (c >= col_start) & (c < col_start + NA_COLS)
    first = lane < GRID_W
    even, odd = [], []
    for a in range(2 * NA_ROWS - 1):
        vec = pltpu.roll(rpb_ref[a:a + 1, :], LANES - (NA_COLS - 1), axis=1)
        x = jnp.broadcast_to(vec, (GRID_W, LANES))
        even.append(pltpu.roll(x, 0, axis=1, stride=1, stride_axis=0))
        odd.append(pltpu.roll(x, GRID_W, axis=1, stride=1, stride_axis=0))
    for v in range(NA_ROWS):
        for i in range(NA_ROWS // 2):
            a0 = 2 * i - v + (NA_ROWS - 1)
            tile = jnp.where(first, even[a0], odd[a0 + 1])
            o_ref[v, :, i * LANES:(i + 1) * LANES] = jnp.where(valid, tile * LOG2E, NEG_INF)


def _na_bias_table(rpb):
    depth = rpb.shape[0]
    padded = jnp.pad(rpb.astype(F32), ((0, 0), (0, 0), (0, 1), (0, LANES - (2 * NA_COLS - 1))))
    nkeys = NA_ROWS * GRID_W
    return pl.pallas_call(
        _na_bias_kernel,
        grid=(depth, N_HEADS),
        in_specs=[pl.BlockSpec((None, None, 2 * NA_ROWS, LANES), lambda l, h: (l, h, 0, 0))],
        out_specs=pl.BlockSpec((None, None, NA_ROWS, GRID_W, nkeys), lambda l, h: (l, h, 0, 0, 0)),
        out_shape=jax.ShapeDtypeStruct((depth, N_HEADS, NA_ROWS, GRID_W, nkeys), F32),
        compiler_params=pltpu.CompilerParams(dimension_semantics=("arbitrary", "arbitrary")),
        name="na_bias",
    )(padded)


def _na_kernel(q_ref, k_ref, v_ref, z_ref, bias_ref, o_ref):
    seq = q_ref.shape[0]
    rows = seq // GRID_W
    half_rows = NA_ROWS // 2
    tile_keys = half_rows * GRID_W
    lane = lax.broadcasted_iota(jnp.int32, (GRID_W, LANES), 1)
    first_head = lane < HEAD_DIM
    masks = (first_head, jnp.logical_not(first_head))
    nt = (((1,), (1,)), ((), ()))
    ones = jnp.ones((tile_keys, LANES), BF16)
    row_start = lambda r: min(max(r - NA_ROWS // 2, 0), rows - NA_ROWS)

    for g0 in range(0, rows, NA_GROUP):
        group = range(g0, g0 + NA_GROUP)
        users = {}
        for r in group:
            for half in range(2):
                users.setdefault(row_start(r) + half_rows * half, []).append((r, half))
        qm = {}
        for r in group:
            qb = (q_ref[r * GRID_W:(r + 1) * GRID_W, :].astype(F32) * Q_SCALE_LOG2).astype(BF16)
            for hd in range(2):
                qm[r, hd] = jnp.where(masks[hd], qb, jnp.zeros_like(qb))
        s_half = {}
        for k, us in sorted(users.items()):
            lhs = jnp.concatenate([qm[r, hd] for r, _ in us for hd in range(2)], axis=0)
            part = lax.dot_general(lhs, k_ref[k * GRID_W:k * GRID_W + tile_keys, :], nt,
                                   preferred_element_type=F32)
            for i, (r, half) in enumerate(us):
                for hd in range(2):
                    blk = part[(2 * i + hd) * GRID_W:(2 * i + hd + 1) * GRID_W, :]
                    bias = bias_ref[hd, r - row_start(r), :, half * tile_keys:(half + 1) * tile_keys]
                    s_half[r, hd, half] = (blk + bias).astype(BF16)
        p_half = {}
        for r in group:
            for hd in range(2):
                s0, s1 = s_half[r, hd, 0], s_half[r, hd, 1]
                m = jnp.max(jnp.maximum(s0, s1), axis=-1, keepdims=True)
                p_half[r, hd, 0], p_half[r, hd, 1] = jnp.exp2(s0 - m), jnp.exp2(s1 - m)
        acc = {}
        for k, us in sorted(users.items()):
            lhs = jnp.concatenate([p_half[r, hd, half] for r, half in us for hd in range(2)], axis=0)
            vext = jnp.concatenate([v_ref[k * GRID_W:k * GRID_W + tile_keys, :], ones], axis=-1)
            part = jnp.dot(lhs, vext, preferred_element_type=F32)
            for i, (r, half) in enumerate(us):
                for hd in range(2):
                    blk = part[(2 * i + hd) * GRID_W:(2 * i + hd + 1) * GRID_W, :]
                    acc[r, hd] = blk if (r, hd) not in acc else acc[r, hd] + blk
        for r in group:
            y = jnp.where(first_head, acc[r, 0][:, :LANES], acc[r, 1][:, :LANES])
            d = jnp.where(first_head, acc[r, 0][:, LANES:], acc[r, 1][:, LANES:])
            z = z_ref[r * GRID_W:(r + 1) * GRID_W, :].astype(F32)
            o_ref[r * GRID_W:(r + 1) * GRID_W, :] = (y * (1.0 / d) * _silu(z)).astype(BF16)


NA_GROUP = 16


def _na(proj, bias, layer, batch, seq):
    def col(c0):
        return pl.BlockSpec((seq, LANES), lambda b, p, c0=c0: (b, c0 + p))

    return pl.pallas_call(
        _na_kernel,
        grid=(batch, N_HEAD_PAIRS),
        in_specs=[
            col(COL_QB), col(COL_KB), col(COL_VB), col(COL_ZB),
            pl.BlockSpec((None, 2, NA_ROWS, GRID_W, NA_ROWS * GRID_W), lambda b, p: (layer, p, 0, 0, 0)),
        ],
        out_specs=pl.BlockSpec((seq, LANES), lambda b, p: (b, p)),
        out_shape=jax.ShapeDtypeStruct((batch * seq, W_BRANCH), BF16),
        compiler_params=pltpu.CompilerParams(
            dimension_semantics=("arbitrary", "arbitrary"), vmem_limit_bytes=VMEM_LIMIT),
        name="na_attn",
    )(proj, proj, proj, proj, bias)


DIL_QB = 128
DIL_KW = 256
N_VARIANTS = 3


def _t5_bucket(rel):
    nb = T5_BUCKETS // 2
    max_exact = nb // 2
    n = np.abs(rel)
    large = max_exact + (np.log(np.maximum(n, 1) / max_exact) / np.log(T5_MAX_DIST / max_exact)
                         * (nb - max_exact)).astype(np.int32)
    large = np.minimum(large, nb - 1)
    return (np.where(rel > 0, nb, 0) + np.where(n < max_exact, n, large)).astype(np.int32)


def _dil_bias_table(t5_bias):
    n_pat = len(DIL_PATTERNS)
    lane = np.arange(DIL_VEC)
    n = np.where(lane < DIL_KW, lane, lane - DIL_VEC)
    in_range = (lane < DIL_KW) | (lane > DIL_VEC - DIL_QB)
    pick = np.zeros((n_pat, N_VARIANTS, T5_BUCKETS, DIL_VEC), np.float32)
    neg = np.zeros((n_pat, N_VARIANTS, 1, DIL_VEC), np.float32)
    for pi, (_, d) in enumerate(DIL_PATTERNS):
        for var in range(N_VARIANTS):
            step = n - DIL_HALF * var
            ok = in_range & (np.abs(step) <= DIL_HALF)
            bucket = _t5_bucket(d * np.clip(step, -DIL_HALF, DIL_HALF))
            pick[pi, var, bucket[ok], lane[ok]] = 1.0
            neg[pi, var, 0, ~ok] = NEG_INF
    vecs = jnp.einsum('bh,pvbn->hpvn', t5_bias.astype(F32), pick,
                      precision=lax.Precision.HIGHEST) * LOG2E + neg[:, :, 0][None]
    return pl.pallas_call(
        _dil_bias_kernel,
        grid=(N_HEADS,),
        in_specs=[pl.BlockSpec((None, n_pat * N_VARIANTS, DIL_VEC), lambda h: (h, 0, 0))],
        out_specs=pl.BlockSpec((None, n_pat, N_VARIANTS, DIL_QB, DIL_KW), lambda h: (h, 0, 0, 0, 0)),
        out_shape=jax.ShapeDtypeStruct((N_HEADS, n_pat, N_VARIANTS, DIL_QB, DIL_KW), F32),
        compiler_params=pltpu.CompilerParams(dimension_semantics=("arbitrary",)),
        name="dil_bias",
    )(vecs.reshape(N_HEADS, n_pat * N_VARIANTS, DIL_VEC))


DIL_VEC = 512


def _dil_bias_kernel(vec_ref, o_ref):
    for pi in range(len(DIL_PATTERNS)):
        for var in range(N_VARIANTS):
            i = pi * N_VARIANTS + var
            x = jnp.broadcast_to(vec_ref[i:i + 1, :], (DIL_QB, DIL_VEC))
            o_ref[pi, var] = pltpu.roll(x, 0, axis=1, stride=1, stride_axis=0)[:, :DIL_KW]


DIL_GROUP = 16
DIL_GROUP_FIRST = 16


def _dil_kernel(q_ref, k_ref, v_ref, z_ref, bias_ref, o_ref, qf, kf, vf, qg, kg, vg, qd, kd, vd,
                q4, k4, v4, m_s, l_s, acc_s):
    seq = q_ref.shape[0]
    conv_rows = 256
    lane = lax.broadcasted_iota(jnp.int32, (DIL_QB, LANES), 1)
    first_head = lane < HEAD_DIM
    nt = (((1,), (1,)), ((), ()))
    ones = jnp.ones((DIL_KW, LANES), BF16)

    def conv(i, carry):
        r0 = pl.multiple_of(i * conv_rows, conv_rows)
        qf[pl.ds(r0, conv_rows), :] = q_ref[pl.ds(r0, conv_rows), :].astype(F32) * Q_SCALE_LOG2
        kf[pl.ds(r0, conv_rows), :] = k_ref[pl.ds(r0, conv_rows), :].astype(F32)
        vf[pl.ds(r0, conv_rows), :] = v_ref[pl.ds(r0, conv_rows), :].astype(F32)
        return carry

    lax.fori_loop(0, seq // conv_rows, conv, 0)

    n4 = seq // 4
    n16 = seq // 16

    def to_order4(i, carry):
        dst = pl.multiple_of(i * conv_rows, conv_rows)
        src = pl.ds(dst // n4 + 4 * (dst % n4), conv_rows, stride=4)
        for a, b, c in ((qf, qg, q4), (kf, kg, k4), (vf, vg, v4)):
            part = a[src, :]
            b[pl.ds(dst, conv_rows), :] = part
            c[pl.ds(dst, conv_rows), :] = part.astype(BF16)
        return carry

    lax.fori_loop(0, seq // conv_rows, to_order4, 0, unroll=2)

    def order16_rows_in_order4(rho):
        return pl.ds((rho % 4) * n4 + rho // 4, n16, stride=4)

    def to_order16(rho, carry):
        src = order16_rows_in_order4(rho)
        dst = pl.ds(pl.multiple_of(rho * n16, n16), n16)
        for a, b in ((qg, qd), (kg, kd), (vg, vd)):
            b[dst, :] = a[src, :].astype(BF16)
        return carry

    lax.fori_loop(0, 16, to_order16, 0, unroll=2)

    def window(src, res, start, size, n_sub):
        return src[pl.ds(pl.multiple_of(res * n_sub + start, DIL_HALF), size), :]

    order4_state = (qf, kf, vf)
    token_state = (m_s, l_s, acc_s)
    plan = ((2, 16, (qd, kd, vd), None, order4_state),
            (1, 4, (q4, k4, v4), order4_state, token_state),
            (0, 1, (qd, k_ref, v_ref), token_state, token_state))

    for step, (pi, d, (q_src, k_src, v_src), st_in, st_out) in enumerate(plan):
        n_sub = seq // d
        nblk = n_sub // DIL_QB

        if step == 2:
            def scaled_q(i, carry):
                r0 = pl.multiple_of(i * conv_rows, conv_rows)
                qd[pl.ds(r0, conv_rows), :] = (
                    q_ref[pl.ds(r0, conv_rows), :].astype(F32) * Q_SCALE_LOG2).astype(BF16)
                return carry

            lax.fori_loop(0, seq // conv_rows, scaled_q, 0, unroll=2)

        def rows_out(res, i0, step=step):
            if step == 0:
                return pl.ds((res % 4) * n4 + res // 4 + 4 * i0, DIL_QB, stride=4)
            if step == 1:
                return pl.ds(res + 4 * i0, DIL_QB, stride=4)
            return pl.ds(pl.multiple_of(i0, DIL_HALF), DIL_QB)

        n_group = DIL_GROUP_FIRST if step == 0 else DIL_GROUP

        def group(gi, carry, pi=pi, n_sub=n_sub, nblk=nblk, q_src=q_src, k_src=k_src, v_src=v_src,
                  st_in=st_in, st_out=st_out, rows_out=rows_out, n_group=n_group):
            units = []
            for i in range(n_group):
                u = gi * n_group + i
                res = u // nblk
                i0 = (u % nblk) * DIL_QB
                ks = jnp.clip(i0 - DIL_HALF, 0, n_sub - DIL_KW)
                units.append((res, i0, ks, (i0 - ks) // DIL_HALF))
            s_tiles, m_tiles = [], []
            for res, i0, ks, var in units:
                qb = window(q_src, res, i0, DIL_QB, n_sub)
                kw = window(k_src, res, ks, DIL_KW, n_sub)
                for hd in range(2):
                    keep = first_head if hd == 0 else jnp.logical_not(first_head)
                    qm = jnp.where(keep, qb, jnp.zeros_like(qb))
                    s = lax.dot_general(qm, kw, nt, preferred_element_type=F32) + bias_ref[hd, pi, var]
                    s = s.astype(BF16)
                    s_tiles.append(s)
                    m_tiles.append(jnp.max(s, axis=-1, keepdims=True))
            p_tiles = [jnp.exp2(s - m) for s, m in zip(s_tiles, m_tiles)]
            for i, (res, i0, ks, var) in enumerate(units):
                vw = window(v_src, res, ks, DIL_KW, n_sub)
                vext = jnp.concatenate([vw, ones], axis=-1)
                r0 = jnp.dot(p_tiles[2 * i], vext, preferred_element_type=F32)
                r1 = jnp.dot(p_tiles[2 * i + 1], vext, preferred_element_type=F32)
                m_cur = jnp.where(first_head, m_tiles[2 * i].astype(F32), m_tiles[2 * i + 1].astype(F32))
                l_cur = jnp.where(first_head, r0[:, LANES:], r1[:, LANES:])
                acc_cur = jnp.where(first_head, r0[:, :LANES], r1[:, :LANES])
                dst = rows_out(res, i0)
                if st_in is None:
                    m_new, l_new, acc_new = m_cur, l_cur, acc_cur
                else:
                    src = pl.ds(pl.multiple_of(res * n_sub + i0, DIL_HALF), DIL_QB)
                    m_old = st_in[0][src, :]
                    m_new = jnp.maximum(m_old, m_cur)
                    a_old = jnp.exp2(m_old - m_new)
                    a_cur = jnp.exp2(m_cur - m_new)
                    l_new = a_old * st_in[1][src, :] + a_cur * l_cur
                    acc_new = a_old * st_in[2][src, :] + a_cur * acc_cur
                st_out[0][dst, :] = m_new
                st_out[1][dst, :] = l_new
                st_out[2][dst, :] = acc_new
            return carry

        lax.fori_loop(0, seq // DIL_QB // n_group, group, 0)

    def fin(i, carry):
        r0 = pl.multiple_of(i * conv_rows, conv_rows)
        z = z_ref[pl.ds(r0, conv_rows), :].astype(F32)
        y = acc_s[pl.ds(r0, conv_rows), :] * (1.0 / l_s[pl.ds(r0, conv_rows), :])
        o_ref[pl.ds(r0, conv_rows), :] = (y * _silu(z)).astype(BF16)
        return carry

    lax.fori_loop(0, seq // conv_rows, fin, 0, unroll=2)


def _dil(proj, bias, batch, seq):
    def col(c0):
        return pl.BlockSpec((seq, LANES), lambda b, p, c0=c0: (b, c0 + p))

    scratch = ([pltpu.VMEM((seq, LANES), F32) for _ in range(6)]
               + [pltpu.VMEM((seq, LANES), BF16) for _ in range(6)]
               + [pltpu.VMEM((seq, LANES), F32) for _ in range(3)])
    return pl.pallas_call(
        _dil_kernel,
        grid=(batch, N_HEAD_PAIRS),
        in_specs=[
            col(COL_QC), col(COL_KC), col(COL_VC), col(COL_ZC),
            pl.BlockSpec((2, len(DIL_PATTERNS), N_VARIANTS, DIL_QB, DIL_KW),
                         lambda b, p: (p, 0, 0, 0, 0)),
        ],
        out_specs=pl.BlockSpec((seq, LANES), lambda b, p: (b, p)),
        out_shape=jax.ShapeDtypeStruct((batch * seq, W_BRANCH), BF16),
        scratch_shapes=scratch,
        compiler_params=pltpu.CompilerParams(
            dimension_semantics=("arbitrary", "arbitrary"), vmem_limit_bytes=VMEM_LIMIT),
        name="dil_attn",
    )(proj, proj, proj, proj, bias)


N_POW = CHUNK + 1
PREP_GROUPS = 8


def _ssm_prep_kernel(lre_ref, lim_ref, ldt_ref, btre_ref, btim_ref, cre_ref, cim_ref,
                     toep_ref, win_ref, wout_ref, decay_ref):
    n = CHUNK * SSM_GROUP
    lane = lax.broadcasted_iota(jnp.int32, (SSM_GROUP, LANES), 1)
    lo = lane < SSM_STATE
    lane_n = lax.broadcasted_iota(jnp.int32, (SSM_GROUP, n), 1)
    nt = (((1,), (1,)), ((), ()))
    pair = lambda x, y: jnp.where(lo, x, y)

    for j in range(PREP_GROUPS):
        xs, ys, bbs, pw_last = {}, {}, {}, {}
        for dr in range(2):
            lam_re = lre_ref[0, dr, j]
            lam_im = lim_ref[0, dr, j]
            dt = jnp.exp(ldt_ref[0, dr, j])
            mag = jnp.exp(lam_re * dt)
            lb_re = mag * jnp.cos(lam_im * dt)
            lb_im = mag * jnp.sin(lam_im * dt)
            n_re = lb_re - 1.0
            den = lam_re * lam_re + lam_im * lam_im
            q_re = (n_re * lam_re + lb_im * lam_im) / den
            q_im = (lb_im * lam_re - n_re * lam_im) / den
            bt_re = btre_ref[0, dr, j]
            bt_im = btim_ref[0, dr, j]
            bb_re = q_re * bt_re - q_im * bt_im
            bb_im = q_re * bt_im + q_im * bt_re
            bbs[dr] = (bb_re, bb_im)
            c_re = cre_ref[0, dr, j]
            c_im = cim_ref[0, dr, j]
            pw_re = jnp.ones_like(lb_re)
            pw_im = jnp.zeros_like(lb_re)
            for k in range(N_POW):
                xs[dr, k] = (pw_re * bb_re - pw_im * bb_im, pw_re * bb_im + pw_im * bb_re)
                ys[dr, k] = (pw_re * c_re - pw_im * c_im, pw_re * c_im + pw_im * c_re)
                pw_last[dr] = (pw_re, pw_im)
                pw_re, pw_im = pw_re * lb_re - pw_im * lb_im, pw_re * lb_im + pw_im * lb_re

        step = [GROUPS_PER_SLAB * (a // GROUPS_PER_SLAB) + (a % GROUPS_PER_SLAB - j) % GROUPS_PER_SLAB
                for a in range(CHUNK)]
        rows = lambda a: slice(a * SSM_GROUP, (a + 1) * SSM_GROUP)

        decay_ref[0, 0, j:j + 1, :] = pair(pw_last[0][0], pw_last[1][0])[0:1]
        decay_ref[0, 1, j:j + 1, :] = pair(pw_last[0][1], pw_last[1][1])[0:1]

        for a, s in enumerate(step):
            win_ref[0, j, rows(a), :LANES] = pair(xs[0, CHUNK - 1 - s][0], xs[1, s][0]).astype(BF16)
            win_ref[0, j, rows(a), LANES:] = pair(xs[0, CHUNK - 1 - s][1], xs[1, s][1]).astype(BF16)

        z_re = jnp.concatenate([pair(ys[0, t + 1][0], ys[1, CHUNK - t][0]) for t in step], axis=0)
        z_nim = jnp.concatenate([pair(-ys[0, t + 1][1], -ys[1, CHUNK - t][1]) for t in step], axis=0)
        wout_ref[0, j, :LANES, :] = z_re.T.astype(BF16)
        wout_ref[0, j, LANES:, :] = z_nim.T.astype(BF16)

        kf = lax.dot_general(
            pair(bbs[0][0], -bbs[0][1]),
            jnp.concatenate([pair(*ys[0, m]) for m in range(CHUNK)], axis=0),
            nt, precision=lax.Precision.HIGHEST, preferred_element_type=F32)
        kb = lax.dot_general(
            pair(bbs[1][0], -bbs[1][1]),
            jnp.concatenate([pair(*ys[1, CHUNK - 1 - i]) for i in range(CHUNK)], axis=0),
            nt, precision=lax.Precision.HIGHEST, preferred_element_type=F32)
        for a, s in enumerate(step):
            fwd = pltpu.roll(kf, SSM_GROUP * s, axis=1) if s else kf
            shift_b = (n - SSM_GROUP * (CHUNK - 1 - s)) % n
            bwd = pltpu.roll(kb, shift_b, axis=1) if shift_b else kb
            t_nat = (jnp.where(lane_n >= SSM_GROUP * s, fwd, 0.0)
                     + jnp.where(lane_n < SSM_GROUP * (s + 1), bwd, 0.0))
            for h in range(2):
                half = t_nat[:, h * LANES:(h + 1) * LANES]
                if j:
                    half = pltpu.roll(half, SSM_GROUP * j, axis=1)
                toep_ref[0, j, rows(a), h * LANES:(h + 1) * LANES] = half.astype(BF16)


def _ssm_prep(lam_re, lam_im, log_dt, b_re, b_im, c_re, c_im):
    depth = lam_re.shape[0]
    G, P, C = SSM_GROUPS, SSM_STATE, SSM_GROUP
    n = CHUNK * C
    gb = PREP_GROUPS
    twice = lambda a: jnp.concatenate([a, a], axis=-1)
    rep = lambda a: jnp.broadcast_to(twice(a.astype(F32))[:, :, :, None, :], (depth, 2, G, C, LANES))
    ldt = jnp.broadcast_to(log_dt.astype(F32)[:, :, :, None, None], (depth, 2, G, C, LANES))
    bt = lambda a: twice(a.astype(F32).transpose(0, 1, 2, 4, 3))
    spec_in = pl.BlockSpec((1, 2, gb, C, LANES), lambda l, i: (l, 0, i, 0, 0))
    spec_w = pl.BlockSpec((1, gb, n, n), lambda l, i: (l, i, 0, 0))
    shp_w = jax.ShapeDtypeStruct((depth, G, n, n), BF16)
    return pl.pallas_call(
        _ssm_prep_kernel,
        grid=(depth, G // gb),
        in_specs=[spec_in] * 7,
        out_specs=[spec_w, spec_w, spec_w, pl.BlockSpec((1, 2, gb, LANES), lambda l, i: (l, 0, i, 0))],
        out_shape=[shp_w, shp_w, shp_w, jax.ShapeDtypeStruct((depth, 2, G, LANES), F32)],
        compiler_params=pltpu.CompilerParams(
            dimension_semantics=("arbitrary", "arbitrary"), vmem_limit_bytes=VMEM_LIMIT),
        name="ssm_prep",
    )(rep(lam_re), rep(lam_im), ldt, bt(b_re), bt(b_im), twice(c_re.astype(F32)), twice(c_im.astype(F32)))


GROUPS_PER_SLAB = LANES // SSM_GROUP
N_SLABS = W_BRANCH // LANES
RELAYOUT_ROWS = 32


def _ssm_kernel(xa_ref, toep_ref, win_ref, wout_ref, decay_ref, d_ref, o_ref, xf, u_s, st, hs, yq):
    seq = xa_ref.shape[0]
    nchunk = seq // CHUNK
    nrb = nchunk // RELAYOUT_ROWS
    gps = GROUPS_PER_SLAB
    lane_rb = lax.broadcasted_iota(jnp.int32, (RELAYOUT_ROWS, LANES), 1)
    seg_masks = [(lane_rb >= SSM_GROUP * sg) & (lane_rb < SSM_GROUP * (sg + 1)) for sg in range(gps)]
    conv_rows = 512

    for q in range(N_SLABS):
        c_lo = q * LANES

        def conv(i, carry, c_lo=c_lo):
            r0 = pl.multiple_of(i * conv_rows, conv_rows)
            xf[pl.ds(r0, conv_rows), :] = xa_ref[pl.ds(r0, conv_rows), c_lo:c_lo + LANES].astype(F32)
            return carry

        lax.fori_loop(0, seq // conv_rows, conv, 0)

        for hh in range(2):
            def fwd_relayout(rb, carry, hh=hh):
                c0 = pl.multiple_of(rb * RELAYOUT_ROWS, RELAYOUT_ROWS)
                rolled = []
                for r in range(gps):
                    xs = xf[pl.ds(CHUNK * c0 + gps * hh + r, RELAYOUT_ROWS, stride=CHUNK), :]
                    rolled.append(xs if r == 0 else pltpu.roll(xs, SSM_GROUP * r, axis=1))
                for j in range(gps):
                    out = rolled[(0 - j) % gps]
                    for sg in range(1, gps):
                        out = jnp.where(seg_masks[sg], rolled[(sg - j) % gps], out)
                    u_s[j, pl.ds(c0, RELAYOUT_ROWS), hh * LANES:(hh + 1) * LANES] = out.astype(BF16)
                return carry

            lax.fori_loop(0, nrb, fwd_relayout, 0, unroll=2)

        for j in range(gps):
            s_in = jnp.dot(u_s[j], win_ref[gps * q + j], preferred_element_type=F32)
            st[0, pl.ds(j, nchunk, stride=gps), :] = s_in[:, :LANES]
            st[1, pl.ds(j, nchunk, stride=gps), :] = s_in[:, LANES:]

        g_lo = gps * q
        a_re = decay_ref[0, g_lo:g_lo + gps, :]
        a_im = decay_ref[1, g_lo:g_lo + gps, :]
        lo, hi = slice(0, SSM_STATE), slice(SSM_STATE, LANES)

        def scan(i, carry):
            hrf, hif, hrb, hib = carry
            rf = pl.ds(pl.multiple_of(i * gps, gps), gps)
            rb_ = pl.ds(pl.multiple_of((nchunk - 1 - i) * gps, gps), gps)
            hs[0, rf, lo] = hrf[:, lo]
            hs[1, rf, lo] = hif[:, lo]
            hs[0, rb_, hi] = hrb[:, hi]
            hs[1, rb_, hi] = hib[:, hi]
            return (a_re * hrf - a_im * hif + st[0, rf, :], a_re * hif + a_im * hrf + st[1, rf, :],
                    a_re * hrb - a_im * hib + st[0, rb_, :], a_re * hib + a_im * hrb + st[1, rb_, :])

        zero = jnp.zeros((gps, LANES), F32)
        lax.fori_loop(0, nchunk, scan, (zero, zero, zero, zero), unroll=8)

        for j in range(gps):
            g = gps * q + j
            h_in = jnp.concatenate([hs[0, pl.ds(j, nchunk, stride=gps), :].astype(BF16),
                                    hs[1, pl.ds(j, nchunk, stride=gps), :].astype(BF16)], axis=-1)
            yq[j] = (jnp.dot(u_s[j], toep_ref[g], preferred_element_type=F32)
                     + jnp.dot(h_in, wout_ref[g], preferred_element_type=F32))

        d_row = d_ref[:, c_lo:c_lo + LANES]
        for hh in range(2):
            def bwd_relayout(rb, carry, hh=hh, d_row=d_row, q=q):
                c0 = pl.multiple_of(rb * RELAYOUT_ROWS, RELAYOUT_ROWS)
                ys = [yq[j, pl.ds(c0, RELAYOUT_ROWS), hh * LANES:(hh + 1) * LANES] for j in range(gps)]
                for r in range(gps):
                    merged = ys[(0 - r) % gps]
                    for sg in range(1, gps):
                        merged = jnp.where(seg_masks[sg], ys[(sg - r) % gps], merged)
                    if r:
                        merged = pltpu.roll(merged, LANES - SSM_GROUP * r, axis=1)
                    tok = pl.ds(CHUNK * c0 + gps * hh + r, RELAYOUT_ROWS, stride=CHUNK)
                    o_ref[q, tok, :] = merged + d_row * xf[tok, :]
                return carry

            lax.fori_loop(0, nrb, bwd_relayout, 0, unroll=2)


def _ssm(proj, toep, w_in, w_out, decay, d_skip, layer, batch, seq):
    nchunk = seq // CHUNK
    n = CHUNK * SSM_GROUP
    wspec = pl.BlockSpec((None, SSM_GROUPS, n, n), lambda b: (layer, 0, 0, 0), pipeline_mode=pl.Buffered(1))
    return pl.pallas_call(
        _ssm_kernel,
        grid=(batch,),
        in_specs=[
            pl.BlockSpec((seq, W_BRANCH), lambda b: (b, COL_XA // N_SLABS)),
            wspec, wspec, wspec,
            pl.BlockSpec((None, 2, SSM_GROUPS, LANES), lambda b: (layer, 0, 0, 0)),
            pl.BlockSpec((None, 1, W_BRANCH), lambda b: (layer, 0, 0)),
        ],
        out_specs=pl.BlockSpec((N_SLABS, seq, LANES), lambda b: (0, b, 0)),
        out_shape=jax.ShapeDtypeStruct((N_SLABS, batch * seq, LANES), F32),
        scratch_shapes=[
            pltpu.VMEM((seq, LANES), F32),
            pltpu.VMEM((GROUPS_PER_SLAB, nchunk, n), BF16),
            pltpu.VMEM((2, nchunk * GROUPS_PER_SLAB, LANES), F32),
            pltpu.VMEM((2, nchunk * GROUPS_PER_SLAB, LANES), F32),
            pltpu.VMEM((GROUPS_PER_SLAB, nchunk, n), F32),
        ],
        compiler_params=pltpu.CompilerParams(
            dimension_semantics=("arbitrary",), vmem_limit_bytes=VMEM_LIMIT),
        name="ssm_mix",
    )(proj, toep, w_in, w_out, decay, d_skip)


OUT_TM = 512


def _out_proj_kernel(ya_ref, za_ref, yb_ref, yc_ref, x_ref, w_ref, gw_ref, gb_ref, fg_ref, o_ref,
                     w_bf, gw_bf, *, final):
    @pl.when(pl.program_id(0) == 0)
    def _():
        for n in range(MIX_WIDTH // W_BRANCH):
            w_bf[n * W_BRANCH:(n + 1) * W_BRANCH, :] = w_ref[n * W_BRANCH:(n + 1) * W_BRANCH, :].astype(BF16)
        gw_bf[...] = gw_ref[...].astype(BF16)

    y = jnp.concatenate([ya_ref[i] for i in range(N_SLABS)], axis=-1)
    g = _gelu_tanh(y)
    gate = jnp.dot(g.astype(BF16), gw_bf[...], preferred_element_type=F32) + gb_ref[...]
    ya = g * _sigmoid(gate) * _silu(za_ref[...].astype(F32))
    delta = (jnp.dot(ya.astype(BF16), w_bf[:W_BRANCH, :], preferred_element_type=F32)
             + jnp.dot(yb_ref[...], w_bf[W_BRANCH:2 * W_BRANCH, :], preferred_element_type=F32)
             + jnp.dot(yc_ref[...], w_bf[2 * W_BRANCH:, :], preferred_element_type=F32))
    x = x_ref[...] + delta
    if final:
        ms = jnp.mean(x * x, axis=-1, keepdims=True)
        x = x * lax.rsqrt(ms + RMS_EPS) * fg_ref[...]
    o_ref[...] = x


def _out_proj(ya_pre, proj, yb, yc, x2d, w, glu_w, glu_b, final_g, layer, final):
    rows = x2d.shape[0]
    row_blk = lambda width: pl.BlockSpec((OUT_TM, width), lambda i: (i, 0))
    const = lambda shape: pl.BlockSpec((None,) + shape, lambda i: (layer,) + (0,) * len(shape),
                                       pipeline_mode=pl.Buffered(1))
    return pl.pallas_call(
        functools.partial(_out_proj_kernel, final=final),
        grid=(rows // OUT_TM,),
        in_specs=[
            pl.BlockSpec((N_SLABS, OUT_TM, LANES), lambda i: (0, i, 0)),
            pl.BlockSpec((OUT_TM, W_BRANCH), lambda i: (i, COL_ZA // N_SLABS)),
            row_blk(W_BRANCH), row_blk(W_BRANCH), row_blk(D_MODEL),
            const((MIX_WIDTH, D_MODEL)), const((W_BRANCH, W_BRANCH)),
            const((1, W_BRANCH)),
            pl.BlockSpec((None, 1, D_MODEL), lambda i: (0, 0, 0), pipeline_mode=pl.Buffered(1)),
        ],
        out_specs=row_blk(D_MODEL),
        out_shape=jax.ShapeDtypeStruct((rows, D_MODEL), F32),
        scratch_shapes=[pltpu.VMEM((MIX_WIDTH, D_MODEL), BF16), pltpu.VMEM((W_BRANCH, W_BRANCH), BF16)],
        compiler_params=pltpu.CompilerParams(
            dimension_semantics=("arbitrary",), vmem_limit_bytes=VMEM_LIMIT),
        name="out_proj_final" if final else "out_proj",
    )(ya_pre, proj, yb, yc, x2d, w, glu_w, glu_b, final_g)


def kernel(x, norm_g, w_in, w_out, ssm_lam_re, ssm_lam_im, ssm_log_dt, ssm_b_re, ssm_b_im, ssm_c_re,
           ssm_c_im, ssm_d, glu_w, glu_b, na_rpb, t5_bias, final_g):
    batch, seq, _ = x.shape
    depth = w_in.shape[0]
    x2d = x.astype(F32).reshape(batch * seq, D_MODEL)
    toep, s_in, s_out, decay = _ssm_prep(ssm_lam_re, ssm_lam_im, ssm_log_dt, ssm_b_re, ssm_b_im,
                                         ssm_c_re, ssm_c_im)
    dil_bias = _dil_bias_table(t5_bias)
    na_bias = _na_bias_table(na_rpb)
    w_in_f, w_out_f, glu_w_f = w_in.astype(F32), w_out.astype(F32), glu_w.astype(F32)
    row3 = lambda a: a.astype(F32).reshape(a.shape[0], 1, a.shape[1])
    norm_g3, ssm_d3, glu_b3 = row3(norm_g), row3(ssm_d), row3(glu_b)
    final_g3 = final_g.astype(F32).reshape(1, 1, D_MODEL)
    for l in range(depth):
        proj = _in_proj(x2d, norm_g3, w_in_f, l)
        ya_pre = _ssm(proj, toep, s_in, s_out, decay, ssm_d3, l, batch, seq)
        yb = _na(proj, na_bias, l, batch, seq)
        yc = _dil(proj, dil_bias, batch, seq)
        x2d = _out_proj(ya_pre, proj, yb, yc, x2d, w_out_f, glu_w_f, glu_b3, final_g3, l,
                        final=(l == depth - 1))
    return x2d.reshape(batch, seq, D_MODEL).astype(x.dtype)
```

```python
import functools

import numpy as np
import jax
import jax.numpy as jnp
from jax import lax
from jax.experimental import pallas as pl
from jax.experimental.pallas import tpu as pltpu

F32 = jnp.float32
BF16 = jnp.bfloat16

D_MODEL = 1024
HEAD_DIM = 64
W_BRANCH = 512
N_HEADS = W_BRANCH // HEAD_DIM
N_HEAD_PAIRS = N_HEADS // 2
SSM_GROUP = 16
SSM_GROUPS = W_BRANCH // SSM_GROUP
SSM_STATE = 64
GRID_W = 64
NA_ROWS = 8
NA_COLS = 16
DIL_PATTERNS = ((128, 1), (512, 4), (2048, 16))
DIL_HALF = 64
T5_BUCKETS = 32
T5_MAX_DIST = 1024
RMS_EPS = 1e-6
NEG_INF = -1e30
IN_COLS = 10 * W_BRANCH
MIX_WIDTH = 3 * W_BRANCH

COL_XA, COL_ZA, COL_QB, COL_KB, COL_VB, COL_ZB, COL_QC, COL_KC, COL_VC, COL_ZC = (
    4 * i for i in range(10))

LOG2E = float(np.log2(np.e))
Q_SCALE_LOG2 = HEAD_DIM ** -0.5 * LOG2E

LANES = 128
CHUNK = 16
VMEM_LIMIT = 56 * 1024 * 1024


def _sigmoid(z):
    return 0.5 * (1.0 + jnp.tanh(0.5 * z))


def _silu(z):
    return z * _sigmoid(z)


def _gelu_tanh(x):
    return 0.5 * x * (1.0 + jnp.tanh(np.sqrt(2.0 / np.pi).astype(np.float32) * (x + 0.044715 * (x * x * x))))


IN_TM = 512
IN_TN = 512


def _in_proj_kernel(x_ref, g_ref, w_ref, o_ref, w_bf):
    @pl.when(pl.program_id(0) == 0)
    def _():
        for n in range(IN_COLS // IN_TN):
            w_bf[:, n * IN_TN:(n + 1) * IN_TN] = w_ref[:, n * IN_TN:(n + 1) * IN_TN].astype(BF16)

    x = x_ref[...]
    ms = jnp.mean(x * x, axis=-1, keepdims=True)
    h = (x * lax.rsqrt(ms + RMS_EPS) * g_ref[...]).astype(BF16)
    for n in range(IN_COLS // IN_TN):
        o_ref[:, n * IN_TN:(n + 1) * IN_TN] = jnp.dot(
            h, w_bf[:, n * IN_TN:(n + 1) * IN_TN], preferred_element_type=F32).astype(BF16)


def _in_proj(x2d, g, w, layer):
    rows = x2d.shape[0]
    return pl.pallas_call(
        _in_proj_kernel,
        grid=(rows // IN_TM,),
        in_specs=[
            pl.BlockSpec((IN_TM, D_MODEL), lambda i: (i, 0)),
            pl.BlockSpec((None, 1, D_MODEL), lambda i: (layer, 0, 0)),
            pl.BlockSpec((None, D_MODEL, IN_COLS), lambda i: (layer, 0, 0), pipeline_mode=pl.Buffered(1)),
        ],
        out_specs=pl.BlockSpec((IN_TM, IN_COLS), lambda i: (i, 0)),
        out_shape=jax.ShapeDtypeStruct((rows, IN_COLS), BF16),
        scratch_shapes=[pltpu.VMEM((D_MODEL, IN_COLS), BF16)],
        compiler_params=pltpu.CompilerParams(
            dimension_semantics=("arbitrary",), vmem_limit_bytes=VMEM_LIMIT),
        name="in_proj",
    )(x2d, g, w)


def _na_bias_kernel(rpb_ref, o_ref):
    lane = lax.broadcasted_iota(jnp.int32, (GRID_W, LANES), 1)
    j = lax.broadcasted_iota(jnp.int32, (GRID_W, LANES), 0)
    c = lane & (GRID_W - 1)
    col_start = jnp.clip(j - NA_COLS // 2, 0, GRID_W - NA_COLS)
    valid = (c >= col_start) & (c < col_start + NA_COLS)
    first = lane < GRID_W
    even, odd = [], []
    for a in range(2 * NA_ROWS - 1):
        vec = pltpu.roll(rpb_ref[a:a + 1, :], LANES - (NA_COLS - 1), axis=1)
        x = jnp.broadcast_to(vec, (GRID_W, LANES))
        even.append(pltpu.roll(x, 0, axis=1, stride=1, stride_axis=0))
        odd.append(pltpu.roll(x, GRID_W, axis=1, stride=1, stride_axis=0))
    for v in range(NA_ROWS):
        for i in range(NA_ROWS // 2):
            a0 = 2 * i - v + (NA_ROWS - 1)
            tile = jnp.where(first, even[a0], odd[a0 + 1])
            o_ref[v, :, i * LANES:(i + 1) * LANES] = jnp.where(valid, tile * LOG2E, NEG_INF)


def _na_bias_table(rpb):
    depth = rpb.shape[0]
    padded = jnp.pad(rpb.astype(F32), ((0, 0), (0, 0), (0, 1), (0, LANES - (2 * NA_COLS - 1))))
    nkeys = NA_ROWS * GRID_W
    return pl.pallas_call(
        _na_bias_kernel,
        grid=(depth, N_HEADS),
        in_specs=[pl.BlockSpec((None, None, 2 * NA_ROWS, LANES), lambda l, h: (l, h, 0, 0))],
        out_specs=pl.BlockSpec((None, None, NA_ROWS, GRID_W, nkeys), lambda l, h: (l, h, 0, 0, 0)),
        out_shape=jax.ShapeDtypeStruct((depth, N_HEADS, NA_ROWS, GRID_W, nkeys), F32),
        compiler_params=pltpu.CompilerParams(dimension_semantics=("arbitrary", "arbitrary")),
        name="na_bias",
    )(padded)


def _na_kernel(q_ref, k_ref, v_ref, z_ref, bias_ref, o_ref):
    seq = q_ref.shape[0]
    rows = seq // GRID_W
    half_rows = NA_ROWS // 2
    tile_keys = half_rows * GRID_W
    lane = lax.broadcasted_iota(jnp.int32, (GRID_W, LANES), 1)
    first_head = lane < HEAD_DIM
    masks = (first_head, jnp.logical_not(first_head))
    nt = (((1,), (1,)), ((), ()))
    ones = jnp.ones((tile_keys, LANES), BF16)
    row_start = lambda r: min(max(r - NA_ROWS // 2, 0), rows - NA_ROWS)

    for g0 in range(0, rows, NA_GROUP):
        group = range(g0, g0 + NA_GROUP)
        users = {}
        for r in group:
            for half in range(2):
                users.setdefault(row_start(r) + half_rows * half, []).append((r, half))
        qm = {}
        for r in group:
            qb = (q_ref[r * GRID_W:(r + 1) * GRID_W, :].astype(F32) * Q_SCALE_LOG2).astype(BF16)
            for hd in range(2):
                qm[r, hd] = jnp.where(masks[hd], qb, jnp.zeros_like(qb))
        s_half = {}
        for k, us in sorted(users.items()):
            lhs = jnp.concatenate([qm[r, hd] for r, _ in us for hd in range(2)], axis=0)
            part = lax.dot_general(lhs, k_ref[k * GRID_W:k * GRID_W + tile_keys, :], nt,
                                   preferred_element_type=F32)
            for i, (r, half) in enumerate(us):
                for hd in range(2):
                    blk = part[(2 * i + hd) * GRID_W:(2 * i + hd + 1) * GRID_W, :]
                    bias = bias_ref[hd, r - row_start(r), :, half * tile_keys:(half + 1) * tile_keys]
                    s_half[r, hd, half] = (blk + bias).astype(BF16)
        p_half = {}
        for r in group:
            for hd in range(2):
                s0, s1 = s_half[r, hd, 0], s_half[r, hd, 1]
                m = jnp.max(jnp.maximum(s0, s1), axis=-1, keepdims=True)
                p_half[r, hd, 0], p_half[r, hd, 1] = jnp.exp2(s0 - m), jnp.exp2(s1 - m)
        acc = {}
        for k, us in sorted(users.items()):
            lhs = jnp.concatenate([p_half[r, hd, half] for r, half in us for hd in range(2)], axis=0)
            vext = jnp.concatenate([v_ref[k * GRID_W:k * GRID_W + tile_keys, :], ones], axis=-1)
            part = jnp.dot(lhs, vext, preferred_element_type=F32)
            for i, (r, half) in enumerate(us):
                for hd in range(2):
                    blk = part[(2 * i + hd) * GRID_W:(2 * i + hd + 1) * GRID_W, :]
                    acc[r, hd] = blk if (r, hd) not in acc else acc[r, hd] + blk
        for r in group:
            y = jnp.where(first_head, acc[r, 0][:, :LANES], acc[r, 1][:, :LANES])
            d = jnp.where(first_head, acc[r, 0][:, LANES:], acc[r, 1][:, LANES:])
            z = z_ref[r * GRID_W:(r + 1) * GRID_W, :].astype(F32)
            o_ref[r * GRID_W:(r + 1) * GRID_W, :] = (y * (1.0 / d) * _silu(z)).astype(BF16)


NA_GROUP = 16


def _na(proj, bias, layer, batch, seq):
    def col(c0):
        return pl.BlockSpec((seq, LANES), lambda b, p, c0=c0: (b, c0 + p))

    return pl.pallas_call(
        _na_kernel,
        grid=(batch, N_HEAD_PAIRS),
        in_specs=[
            col(COL_QB), col(COL_KB), col(COL_VB), col(COL_ZB),
            pl.BlockSpec((None, 2, NA_ROWS, GRID_W, NA_ROWS * GRID_W), lambda b, p: (layer, p, 0, 0, 0)),
        ],
        out_specs=pl.BlockSpec((seq, LANES), lambda b, p: (b, p)),
        out_shape=jax.ShapeDtypeStruct((batch * seq, W_BRANCH), BF16),
        compiler_params=pltpu.CompilerParams(
            dimension_semantics=("arbitrary", "arbitrary"), vmem_limit_bytes=VMEM_LIMIT),
        name="na_attn",
    )(proj, proj, proj, proj, bias)


DIL_QB = 128
DIL_KW = 256
N_VARIANTS = 3


def _t5_bucket(rel):
    nb = T5_BUCKETS // 2
    max_exact = nb // 2
    n = np.abs(rel)
    large = max_exact + (np.log(np.maximum(n, 1) / max_exact) / np.log(T5_MAX_DIST / max_exact)
                         * (nb - max_exact)).astype(np.int32)
    large = np.minimum(large, nb - 1)
    return (np.where(rel > 0, nb, 0) + np.where(n < max_exact, n, large)).astype(np.int32)


def _dil_bias_table(t5_bias):
    n_pat = len(DIL_PATTERNS)
    lane = np.arange(DIL_VEC)
    n = np.where(lane < DIL_KW, lane, lane - DIL_VEC)
    in_range = (lane < DIL_KW) | (lane > DIL_VEC - DIL_QB)
    pick = np.zeros((n_pat, N_VARIANTS, T5_BUCKETS, DIL_VEC), np.float32)
    neg = np.zeros((n_pat, N_VARIANTS, 1, DIL_VEC), np.float32)
    for pi, (_, d) in enumerate(DIL_PATTERNS):
        for var in range(N_VARIANTS):
            step = n - DIL_HALF * var
            ok = in_range & (np.abs(step) <= DIL_HALF)
            bucket = _t5_bucket(d * np.clip(step, -DIL_HALF, DIL_HALF))
            pick[pi, var, bucket[ok], lane[ok]] = 1.0
            neg[pi, var, 0, ~ok] = NEG_INF
    vecs = jnp.einsum('bh,pvbn->hpvn', t5_bias.astype(F32), pick,
                      precision=lax.Precision.HIGHEST) * LOG2E + neg[:, :, 0][None]
    return pl.pallas_call(
        _dil_bias_kernel,
        grid=(N_HEADS,),
        in_specs=[pl.BlockSpec((None, n_pat * N_VARIANTS, DIL_VEC), lambda h: (h, 0, 0))],
        out_specs=pl.BlockSpec((None, n_pat, N_VARIANTS, DIL_QB, DIL_KW), lambda h: (h, 0, 0, 0, 0)),
        out_shape=jax.ShapeDtypeStruct((N_HEADS, n_pat, N_VARIANTS, DIL_QB, DIL_KW), F32),
        compiler_params=pltpu.CompilerParams(dimension_semantics=("arbitrary",)),
        name="dil_bias",
    )(vecs.reshape(N_HEADS, n_pat * N_VARIANTS, DIL_VEC))


DIL_VEC = 512


def _dil_bias_kernel(vec_ref, o_ref):
    for pi in range(len(DIL_PATTERNS)):
        for var in range(N_VARIANTS):
            i = pi * N_VARIANTS + var
            x = jnp.broadcast_to(vec_ref[i:i + 1, :], (DIL_QB, DIL_VEC))
            o_ref[pi, var] = pltpu.roll(x, 0, axis=1, stride=1, stride_axis=0)[:, :DIL_KW]


DIL_GROUP = 16
DIL_GROUP_FIRST = 16


def _dil_kernel(q_ref, k_ref, v_ref, z_ref, bias_ref, o_ref, qf, kf, vf, qg, kg, vg, qd,
                q4, k4, v4, m_s, l_s, acc_s):
    seq = q_ref.shape[0]
    conv_rows = 256
    lane = lax.broadcasted_iota(jnp.int32, (DIL_QB, LANES), 1)
    first_head = lane < HEAD_DIM
    nt = (((1,), (1,)), ((), ()))
    ones = jnp.ones((DIL_KW, LANES), BF16)

    def conv(i, carry):
        r0 = pl.multiple_of(i * conv_rows, conv_rows)
        qf[pl.ds(r0, conv_rows), :] = q_ref[pl.ds(r0, conv_rows), :].astype(F32) * Q_SCALE_LOG2
        kf[pl.ds(r0, conv_rows), :] = k_ref[pl.ds(r0, conv_rows), :].astype(F32)
        vf[pl.ds(r0, conv_rows), :] = v_ref[pl.ds(r0, conv_rows), :].astype(F32)
        return carry

    lax.fori_loop(0, seq // conv_rows, conv, 0)

    n4 = seq // 4
    n16 = seq // 16

    def to_order4(i, carry):
        dst = pl.multiple_of(i * conv_rows, conv_rows)
        src = pl.ds(dst // n4 + 4 * (dst % n4), conv_rows, stride=4)
        for a, b, c in ((qf, qg, q4), (kf, kg, k4), (vf, vg, v4)):
            part = a[src, :]
            b[pl.ds(dst, conv_rows), :] = part
            c[pl.ds(dst, conv_rows), :] = part.astype(BF16)
        return carry

    lax.fori_loop(0, seq // conv_rows, to_order4, 0, unroll=2)

    def window(src, res, start, size, n_sub):
        if n_sub == n16:
            rows = pl.ds((res % 4) * n4 + res // 4 + 4 * start, size, stride=4)
            return src[rows, :].astype(BF16)
        return src[pl.ds(pl.multiple_of(res * n_sub + start, DIL_HALF), size), :]

    order4_state = (qf, kf, vf)
    token_state = (m_s, l_s, acc_s)
    plan = ((2, 16, (qg, kg, vg), None, order4_state),
            (1, 4, (q4, k4, v4), order4_state, token_state),
            (0, 1, (qd, k_ref, v_ref), token_state, token_state))

    for step, (pi, d, (q_src, k_src, v_src), st_in, st_out) in enumerate(plan):
        n_sub = seq // d
        nblk = n_sub // DIL_QB

        if step == 2:
            def scaled_q(i, carry):
                r0 = pl.multiple_of(i * conv_rows, conv_rows)
                qd[pl.ds(r0, conv_rows), :] = (
                    q_ref[pl.ds(r0, conv_rows), :].astype(F32) * Q_SCALE_LOG2).astype(BF16)
                return carry

            lax.fori_loop(0, seq // conv_rows, scaled_q, 0, unroll=2)

        def rows_out(res, i0, step=step):
            if step == 0:
                return pl.ds((res % 4) * n4 + res // 4 + 4 * i0, DIL_QB, stride=4)
            if step == 1:
                return pl.ds(res + 4 * i0, DIL_QB, stride=4)
            return pl.ds(pl.multiple_of(i0, DIL_HALF), DIL_QB)

        n_group = DIL_GROUP_FIRST if step == 0 else DIL_GROUP

        def group(gi, carry, pi=pi, n_sub=n_sub, nblk=nblk, q_src=q_src, k_src=k_src, v_src=v_src,
                  st_in=st_in, st_out=st_out, rows_out=rows_out, n_group=n_group):
            units = []
            for i in range(n_group):
                u = gi * n_group + i
                res = u // nblk
                i0 = (u % nblk) * DIL_QB
                ks = jnp.clip(i0 - DIL_HALF, 0, n_sub - DIL_KW)
                units.append((res, i0, ks, (i0 - ks) // DIL_HALF))
            s_tiles, m_tiles = [], []
            for res, i0, ks, var in units:
                qb = window(q_src, res, i0, DIL_QB, n_sub)
                kw = window(k_src, res, ks, DIL_KW, n_sub)
                for hd in range(2):
                    keep = first_head if hd == 0 else jnp.logical_not(first_head)
                    qm = jnp.where(keep, qb, jnp.zeros_like(qb))
                    s = lax.dot_general(qm, kw, nt, preferred_element_type=F32) + bias_ref[hd, pi, var]
                    s = s.astype(BF16)
                    s_tiles.append(s)
                    m_tiles.append(jnp.max(s, axis=-1, keepdims=True))
            p_tiles = [jnp.exp2(s - m) for s, m in zip(s_tiles, m_tiles)]
            for i, (res, i0, ks, var) in enumerate(units):
                vw = window(v_src, res, ks, DIL_KW, n_sub)
                vext = jnp.concatenate([vw, ones], axis=-1)
                r0 = jnp.dot(p_tiles[2 * i], vext, preferred_element_type=F32)
                r1 = jnp.dot(p_tiles[2 * i + 1], vext, preferred_element_type=F32)
                m_cur = jnp.where(first_head, m_tiles[2 * i].astype(F32), m_tiles[2 * i + 1].astype(F32))
                l_cur = jnp.where(first_head, r0[:, LANES:], r1[:, LANES:])
                acc_cur = jnp.where(first_head, r0[:, :LANES], r1[:, :LANES])
                dst = rows_out(res, i0)
                if st_in is None:
                    m_new, l_new, acc_new = m_cur, l_cur, acc_cur
                else:
                    src = pl.ds(pl.multiple_of(res * n_sub + i0, DIL_HALF), DIL_QB)
                    m_old = st_in[0][src, :]
                    m_new = jnp.maximum(m_old, m_cur)
                    a_old = jnp.exp2(m_old - m_new)
                    a_cur = jnp.exp2(m_cur - m_new)
                    l_new = a_old * st_in[1][src, :] + a_cur * l_cur
                    acc_new = a_old * st_in[2][src, :] + a_cur * acc_cur
                st_out[0][dst, :] = m_new
                st_out[1][dst, :] = l_new
                st_out[2][dst, :] = acc_new
            return carry

        lax.fori_loop(0, seq // DIL_QB // n_group, group, 0)

    def fin(i, carry):
        r0 = pl.multiple_of(i * conv_rows, conv_rows)
        z = z_ref[pl.ds(r0, conv_rows), :].astype(F32)
        y = acc_s[pl.ds(r0, conv_rows), :] * (1.0 / l_s[pl.ds(r0, conv_rows), :])
        o_ref[pl.ds(r0, conv_rows), :] = (y * _silu(z)).astype(BF16)
        return carry

    lax.fori_loop(0, seq // conv_rows, fin, 0, unroll=2)


def _dil(proj, bias, batch, seq):
    def col(c0):
        return pl.BlockSpec((seq, LANES), lambda b, p, c0=c0: (b, c0 + p))

    scratch = ([pltpu.VMEM((seq, LANES), F32) for _ in range(6)]
               + [pltpu.VMEM((seq, LANES), BF16) for _ in range(4)]
               + [pltpu.VMEM((seq, LANES), F32) for _ in range(3)])
    return pl.pallas_call(
        _dil_kernel,
        grid=(batch, N_HEAD_PAIRS),
        in_specs=[
            col(COL_QC), col(COL_KC), col(COL_VC), col(COL_ZC),
            pl.BlockSpec((2, len(DIL_PATTERNS), N_VARIANTS, DIL_QB, DIL_KW),
                         lambda b, p: (p, 0, 0, 0, 0)),
        ],
        out_specs=pl.BlockSpec((seq, LANES), lambda b, p: (b, p)),
        out_shape=jax.ShapeDtypeStruct((batch * seq, W_BRANCH), BF16),
        scratch_shapes=scratch,
        compiler_params=pltpu.CompilerParams(
            dimension_semantics=("arbitrary", "arbitrary"), vmem_limit_bytes=VMEM_LIMIT),
        name="dil_attn",
    )(proj, proj, proj, proj, bias)


N_POW = CHUNK + 1
PREP_GROUPS = 8


def _ssm_prep_kernel(lre_ref, lim_ref, ldt_ref, btre_ref, btim_ref, cre_ref, cim_ref,
                     toep_ref, win_ref, wout_ref, decay_ref):
    n = CHUNK * SSM_GROUP
    lane = lax.broadcasted_iota(jnp.int32, (SSM_GROUP, LANES), 1)
    lo = lane < SSM_STATE
    lane_n = lax.broadcasted_iota(jnp.int32, (SSM_GROUP, n), 1)
    nt = (((1,), (1,)), ((), ()))
    pair = lambda x, y: jnp.where(lo, x, y)

    for j in range(PREP_GROUPS):
        xs, ys, bbs, pw_last = {}, {}, {}, {}
        for dr in range(2):
            lam_re = lre_ref[0, dr, j]
            lam_im = lim_ref[0, dr, j]
            dt = jnp.exp(ldt_ref[0, dr, j])
            mag = jnp.exp(lam_re * dt)
            lb_re = mag * jnp.cos(lam_im * dt)
            lb_im = mag * jnp.sin(lam_im * dt)
            n_re = lb_re - 1.0
            den = lam_re * lam_re + lam_im * lam_im
            q_re = (n_re * lam_re + lb_im * lam_im) / den
            q_im = (lb_im * lam_re - n_re * lam_im) / den
            bt_re = btre_ref[0, dr, j]
            bt_im = btim_ref[0, dr, j]
            bb_re = q_re * bt_re - q_im * bt_im
            bb_im = q_re * bt_im + q_im * bt_re
            bbs[dr] = (bb_re, bb_im)
            c_re = cre_ref[0, dr, j]
            c_im = cim_ref[0, dr, j]
            pw_re = jnp.ones_like(lb_re)
            pw_im = jnp.zeros_like(lb_re)
            for k in range(N_POW):
                xs[dr, k] = (pw_re * bb_re - pw_im * bb_im, pw_re * bb_im + pw_im * bb_re)
                ys[dr, k] = (pw_re * c_re - pw_im * c_im, pw_re * c_im + pw_im * c_re)
                pw_last[dr] = (pw_re, pw_im)
                pw_re, pw_im = pw_re * lb_re - pw_im * lb_im, pw_re * lb_im + pw_im * lb_re

        step = [GROUPS_PER_SLAB * (a // GROUPS_PER_SLAB) + (a % GROUPS_PER_SLAB - j) % GROUPS_PER_SLAB
                for a in range(CHUNK)]
        rows = lambda a: slice(a * SSM_GROUP, (a + 1) * SSM_GROUP)

        decay_ref[0, 0, j:j + 1, :] = pair(pw_last[0][0], pw_last[1][0])[0:1]
        decay_ref[0, 1, j:j + 1, :] = pair(pw_last[0][1], pw_last[1][1])[0:1]

        for a, s in enumerate(step):
            win_ref[0, j, rows(a), :LANES] = pair(xs[0, CHUNK - 1 - s][0], xs[1, s][0]).astype(BF16)
            win_ref[0, j, rows(a), LANES:] = pair(xs[0, CHUNK - 1 - s][1], xs[1, s][1]).astype(BF16)

        z_re = jnp.concatenate([pair(ys[0, t + 1][0], ys[1, CHUNK - t][0]) for t in step], axis=0)
        z_nim = jnp.concatenate([pair(-ys[0, t + 1][1], -ys[1, CHUNK - t][1]) for t in step], axis=0)
        wout_ref[0, j, :LANES, :] = z_re.T.astype(BF16)
        wout_ref[0, j, LANES:, :] = z_nim.T.astype(BF16)

        kf = lax.dot_general(
            pair(bbs[0][0], -bbs[0][1]),
            jnp.concatenate([pair(*ys[0, m]) for m in range(CHUNK)], axis=0),
            nt, precision=lax.Precision.HIGHEST, preferred_element_type=F32)
        kb = lax.dot_general(
            pair(bbs[1][0], -bbs[1][1]),
            jnp.concatenate([pair(*ys[1, CHUNK - 1 - i]) for i in range(CHUNK)], axis=0),
            nt, precision=lax.Precision.HIGHEST, preferred_element_type=F32)
        for a, s in enumerate(step):
            fwd = pltpu.roll(kf, SSM_GROUP * s, axis=1) if s else kf
            shift_b = (n - SSM_GROUP * (CHUNK - 1 - s)) % n
            bwd = pltpu.roll(kb, shift_b, axis=1) if shift_b else kb
            t_nat = (jnp.where(lane_n >= SSM_GROUP * s, fwd, 0.0)
                     + jnp.where(lane_n < SSM_GROUP * (s + 1), bwd, 0.0))
            for h in range(2):
                half = t_nat[:, h * LANES:(h + 1) * LANES]
                if j:
                    half = pltpu.roll(half, SSM_GROUP * j, axis=1)
                toep_ref[0, j, rows(a), h * LANES:(h + 1) * LANES] = half.astype(BF16)


def _ssm_prep(lam_re, lam_im, log_dt, b_re, b_im, c_re, c_im):
    depth = lam_re.shape[0]
    G, P, C = SSM_GROUPS, SSM_STATE, SSM_GROUP
    n = CHUNK * C
    gb = PREP_GROUPS
    twice = lambda a: jnp.concatenate([a, a], axis=-1)
    rep = lambda a: jnp.broadcast_to(twice(a.astype(F32))[:, :, :, None, :], (depth, 2, G, C, LANES))
    ldt = jnp.broadcast_to(log_dt.astype(F32)[:, :, :, None, None], (depth, 2, G, C, LANES))
    bt = lambda a: twice(a.astype(F32).transpose(0, 1, 2, 4, 3))
    spec_in = pl.BlockSpec((1, 2, gb, C, LANES), lambda l, i: (l, 0, i, 0, 0))
    spec_w = pl.BlockSpec((1, gb, n, n), lambda l, i: (l, i, 0, 0))
    shp_w = jax.ShapeDtypeStruct((depth, G, n, n), BF16)
    return pl.pallas_call(
        _ssm_prep_kernel,
        grid=(depth, G // gb),
        in_specs=[spec_in] * 7,
        out_specs=[spec_w, spec_w, spec_w, pl.BlockSpec((1, 2, gb, LANES), lambda l, i: (l, 0, i, 0))],
        out_shape=[shp_w, shp_w, shp_w, jax.ShapeDtypeStruct((depth, 2, G, LANES), F32)],
        compiler_params=pltpu.CompilerParams(
            dimension_semantics=("arbitrary", "arbitrary"), vmem_limit_bytes=VMEM_LIMIT),
        name="ssm_prep",
    )(rep(lam_re), rep(lam_im), ldt, bt(b_re), bt(b_im), twice(c_re.astype(F32)), twice(c_im.astype(F32)))


GROUPS_PER_SLAB = LANES // SSM_GROUP
N_SLABS = W_BRANCH // LANES
RELAYOUT_ROWS = 32


def _ssm_kernel(xa_ref, toep_ref, win_ref, wout_ref, decay_ref, d_ref, o_ref, xf, u_s, st, hs, yq):
    seq = xa_ref.shape[0]
    nchunk = seq // CHUNK
    nrb = nchunk // RELAYOUT_ROWS
    gps = GROUPS_PER_SLAB
    lane_rb = lax.broadcasted_iota(jnp.int32, (RELAYOUT_ROWS, LANES), 1)
    seg_masks = [(lane_rb >= SSM_GROUP * sg) & (lane_rb < SSM_GROUP * (sg + 1)) for sg in range(gps)]
    conv_rows = 512

    for q in range(N_SLABS):
        c_lo = q * LANES

        def conv(i, carry, c_lo=c_lo):
            r0 = pl.multiple_of(i * conv_rows, conv_rows)
            xf[pl.ds(r0, conv_rows), :] = xa_ref[pl.ds(r0, conv_rows), c_lo:c_lo + LANES].astype(F32)
            return carry

        lax.fori_loop(0, seq // conv_rows, conv, 0)

        for hh in range(2):
            def fwd_relayout(rb, carry, hh=hh):
                c0 = pl.multiple_of(rb * RELAYOUT_ROWS, RELAYOUT_ROWS)
                rolled = []
                for r in range(gps):
                    xs = xf[pl.ds(CHUNK * c0 + gps * hh + r, RELAYOUT_ROWS, stride=CHUNK), :]
                    rolled.append(xs if r == 0 else pltpu.roll(xs, SSM_GROUP * r, axis=1))
                for j in range(gps):
                    out = rolled[(0 - j) % gps]
                    for sg in range(1, gps):
                        out = jnp.where(seg_masks[sg], rolled[(sg - j) % gps], out)
                    u_s[j, pl.ds(c0, RELAYOUT_ROWS), hh * LANES:(hh + 1) * LANES] = out.astype(BF16)
                return carry

            lax.fori_loop(0, nrb, fwd_relayout, 0, unroll=2)

        for j in range(gps):
            s_in = jnp.dot(u_s[j], win_ref[gps * q + j], preferred_element_type=F32)
            st[0, pl.ds(j, nchunk, stride=gps), :] = s_in[:, :LANES]
            st[1, pl.ds(j, nchunk, stride=gps), :] = s_in[:, LANES:]

        g_lo = gps * q
        a_re = decay_ref[0, g_lo:g_lo + gps, :]
        a_im = decay_ref[1, g_lo:g_lo + gps, :]
        lo, hi = slice(0, SSM_STATE), slice(SSM_STATE, LANES)

        def scan(i, carry):
            hrf, hif, hrb, hib = carry
            rf = pl.ds(pl.multiple_of(i * gps, gps), gps)
            rb_ = pl.ds(pl.multiple_of((nchunk - 1 - i) * gps, gps), gps)
            hs[0, rf, lo] = hrf[:, lo]
            hs[1, rf, lo] = hif[:, lo]
            hs[0, rb_, hi] = hrb[:, hi]
            hs[1, rb_, hi] = hib[:, hi]
            return (a_re * hrf - a_im * hif + st[0, rf, :], a_re * hif + a_im * hrf + st[1, rf, :],
                    a_re * hrb - a_im * hib + st[0, rb_, :], a_re * hib + a_im * hrb + st[1, rb_, :])

        zero = jnp.zeros((gps, LANES), F32)
        lax.fori_loop(0, nchunk, scan, (zero, zero, zero, zero), unroll=8)

        for j in range(gps):
            g = gps * q + j
            h_in = jnp.concatenate([hs[0, pl.ds(j, nchunk, stride=gps), :].astype(BF16),
                                    hs[1, pl.ds(j, nchunk, stride=gps), :].astype(BF16)], axis=-1)
            yq[j] = (jnp.dot(u_s[j], toep_ref[g], preferred_element_type=F32)
                     + jnp.dot(h_in, wout_ref[g], preferred_element_type=F32))

        d_row = d_ref[:, c_lo:c_lo + LANES]
        for hh in range(2):
            def bwd_relayout(rb, carry, hh=hh, d_row=d_row, q=q):
                c0 = pl.multiple_of(rb * RELAYOUT_ROWS, RELAYOUT_ROWS)
                ys = [yq[j, pl.ds(c0, RELAYOUT_ROWS), hh * LANES:(hh + 1) * LANES] for j in range(gps)]
                for r in range(gps):
                    merged = ys[(0 - r) % gps]
                    for sg in range(1, gps):
                        merged = jnp.where(seg_masks[sg], ys[(sg - r) % gps], merged)
                    if r:
                        merged = pltpu.roll(merged, LANES - SSM_GROUP * r, axis=1)
                    tok = pl.ds(CHUNK * c0 + gps * hh + r, RELAYOUT_ROWS, stride=CHUNK)
                    o_ref[q, tok, :] = merged + d_row * xf[tok, :]
                return carry

            lax.fori_loop(0, nrb, bwd_relayout, 0, unroll=2)


def _ssm(proj, toep, w_in, w_out, decay, d_skip, layer, batch, seq):
    nchunk = seq // CHUNK
    n = CHUNK * SSM_GROUP
    wspec = pl.BlockSpec((None, SSM_GROUPS, n, n), lambda b: (layer, 0, 0, 0), pipeline_mode=pl.Buffered(1))
    return pl.pallas_call(
        _ssm_kernel,
        grid=(batch,),
        in_specs=[
            pl.BlockSpec((seq, W_BRANCH), lambda b: (b, COL_XA // N_SLABS)),
            wspec, wspec, wspec,
            pl.BlockSpec((None, 2, SSM_GROUPS, LANES), lambda b: (layer, 0, 0, 0)),
            pl.BlockSpec((None, 1, W_BRANCH), lambda b: (layer, 0, 0)),
        ],
        out_specs=pl.BlockSpec((N_SLABS, seq, LANES), lambda b: (0, b, 0)),
        out_shape=jax.ShapeDtypeStruct((N_SLABS, batch * seq, LANES), F32),
        scratch_shapes=[
            pltpu.VMEM((seq, LANES), F32),
            pltpu.VMEM((GROUPS_PER_SLAB, nchunk, n), BF16),
            pltpu.VMEM((2, nchunk * GROUPS_PER_SLAB, LANES), F32),
            pltpu.VMEM((2, nchunk * GROUPS_PER_SLAB, LANES), F32),
            pltpu.VMEM((GROUPS_PER_SLAB, nchunk, n), F32),
        ],
        compiler_params=pltpu.CompilerParams(
            dimension_semantics=("arbitrary",), vmem_limit_bytes=VMEM_LIMIT),
        name="ssm_mix",
    )(proj, toep, w_in, w_out, decay, d_skip)


OUT_TM = 512


def _out_proj_kernel(ya_ref, za_ref, yb_ref, yc_ref, x_ref, w_ref, gw_ref, gb_ref, fg_ref, o_ref,
                     w_bf, gw_bf, *, final):
    @pl.when(pl.program_id(0) == 0)
    def _():
        for n in range(MIX_WIDTH // W_BRANCH):
            w_bf[n * W_BRANCH:(n + 1) * W_BRANCH, :] = w_ref[n * W_BRANCH:(n + 1) * W_BRANCH, :].astype(BF16)
        gw_bf[...] = gw_ref[...].astype(BF16)

    y = jnp.concatenate([ya_ref[i] for i in range(N_SLABS)], axis=-1)
    g = _gelu_tanh(y)
    gate = jnp.dot(g.astype(BF16), gw_bf[...], preferred_element_type=F32) + gb_ref[...]
    ya = g * _sigmoid(gate) * _silu(za_ref[...].astype(F32))
    delta = (jnp.dot(ya.astype(BF16), w_bf[:W_BRANCH, :], preferred_element_type=F32)
             + jnp.dot(yb_ref[...], w_bf[W_BRANCH:2 * W_BRANCH, :], preferred_element_type=F32)
             + jnp.dot(yc_ref[...], w_bf[2 * W_BRANCH:, :], preferred_element_type=F32))
    x = x_ref[...] + delta
    if final:
        ms = jnp.mean(x * x, axis=-1, keepdims=True)
        x = x * lax.rsqrt(ms + RMS_EPS) * fg_ref[...]
    o_ref[...] = x


def _out_proj(ya_pre, proj, yb, yc, x2d, w, glu_w, glu_b, final_g, layer, final):
    rows = x2d.shape[0]
    row_blk = lambda width: pl.BlockSpec((OUT_TM, width), lambda i: (i, 0))
    const = lambda shape: pl.BlockSpec((None,) + shape, lambda i: (layer,) + (0,) * len(shape),
                                       pipeline_mode=pl.Buffered(1))
    return pl.pallas_call(
        functools.partial(_out_proj_kernel, final=final),
        grid=(rows // OUT_TM,),
        in_specs=[
            pl.BlockSpec((N_SLABS, OUT_TM, LANES), lambda i: (0, i, 0)),
            pl.BlockSpec((OUT_TM, W_BRANCH), lambda i: (i, COL_ZA // N_SLABS)),
            row_blk(W_BRANCH), row_blk(W_BRANCH), row_blk(D_MODEL),
            const((MIX_WIDTH, D_MODEL)), const((W_BRANCH, W_BRANCH)),
            const((1, W_BRANCH)),
            pl.BlockSpec((None, 1, D_MODEL), lambda i: (0, 0, 0), pipeline_mode=pl.Buffered(1)),
        ],
        out_specs=row_blk(D_MODEL),
        out_shape=jax.ShapeDtypeStruct((rows, D_MODEL), F32),
        scratch_shapes=[pltpu.VMEM((MIX_WIDTH, D_MODEL), BF16), pltpu.VMEM((W_BRANCH, W_BRANCH), BF16)],
        compiler_params=pltpu.CompilerParams(
            dimension_semantics=("arbitrary",), vmem_limit_bytes=VMEM_LIMIT),
        name="out_proj_final" if final else "out_proj",
    )(ya_pre, proj, yb, yc, x2d, w, glu_w, glu_b, final_g)


def kernel(x, norm_g, w_in, w_out, ssm_lam_re, ssm_lam_im, ssm_log_dt, ssm_b_re, ssm_b_im, ssm_c_re,
           ssm_c_im, ssm_d, glu_w, glu_b, na_rpb, t5_bias, final_g):
    batch, seq, _ = x.shape
    depth = w_in.shape[0]
    x2d = x.astype(F32).reshape(batch * seq, D_MODEL)
    toep, s_in, s_out, decay = _ssm_prep(ssm_lam_re, ssm_lam_im, ssm_log_dt, ssm_b_re, ssm_b_im,
                                         ssm_c_re, ssm_c_im)
    dil_bias = _dil_bias_table(t5_bias)
    na_bias = _na_bias_table(na_rpb)
    w_in_f, w_out_f, glu_w_f = w_in.astype(F32), w_out.astype(F32), glu_w.astype(F32)
    row3 = lambda a: a.astype(F32).reshape(a.shape[0], 1, a.shape[1])
    norm_g3, ssm_d3, glu_b3 = row3(norm_g), row3(ssm_d), row3(glu_b)
    final_g3 = final_g.astype(F32).reshape(1, 1, D_MODEL)
    for l in range(depth):
        proj = _in_proj(x2d, norm_g3, w_in_f, l)
        ya_pre = _ssm(proj, toep, s_in, s_out, decay, ssm_d3, l, batch, seq)
        yb = _na(proj, na_bias, l, batch, seq)
        yc = _dil(proj, dil_bias, batch, seq)
        x2d = _out_proj(ya_pre, proj, yb, yc, x2d, w_out_f, glu_w_f, glu_b3, final_g3, l,
                        final=(l == depth - 1))
    return x2d.reshape(batch, seq, D_MODEL).astype(x.dtype)
```

```python
import functools

import numpy as np
import jax
import jax.numpy as jnp
from jax import lax
from jax.experimental import pallas as pl
from jax.experimental.pallas import tpu as pltpu

F32 = jnp.float32
BF16 = jnp.bfloat16

D_MODEL = 1024
HEAD_DIM = 64
W_BRANCH = 512
N_HEADS = W_BRANCH // HEAD_DIM
N_HEAD_PAIRS = N_HEADS // 2
SSM_GROUP = 16
SSM_GROUPS = W_BRANCH // SSM_GROUP
SSM_STATE = 64
GRID_W = 64
NA_ROWS = 8
NA_COLS = 16
DIL_PATTERNS = ((128, 1), (512, 4), (2048, 16))
DIL_HALF = 64
T5_BUCKETS = 32
T5_MAX_DIST = 1024
RMS_EPS = 1e-6
NEG_INF = -1e30
IN_COLS = 10 * W_BRANCH
MIX_WIDTH = 3 * W_BRANCH

COL_XA, COL_ZA, COL_QB, COL_KB, COL_VB, COL_ZB, COL_QC, COL_KC, COL_VC, COL_ZC = (
    4 * i for i in range(10))

LOG2E = float(np.log2(np.e))
Q_SCALE_LOG2 = HEAD_DIM ** -0.5 * LOG2E

LANES = 128
CHUNK = 16
VMEM_LIMIT = 56 * 1024 * 1024


def _sigmoid(z):
    return 0.5 * (1.0 + jnp.tanh(0.5 * z))


def _silu(z):
    return z * _sigmoid(z)


def _gelu_tanh(x):
    return 0.5 * x * (1.0 + jnp.tanh(np.sqrt(2.0 / np.pi).astype(np.float32) * (x + 0.044715 * (x * x * x))))


IN_TM = 512
IN_TN = 512


def _in_proj_kernel(x_ref, g_ref, w_ref, o_ref, w_bf):
    @pl.when(pl.program_id(0) == 0)
    def _():
        for n in range(IN_COLS // IN_TN):
            w_bf[:, n * IN_TN:(n + 1) * IN_TN] = w_ref[:, n * IN_TN:(n + 1) * IN_TN].astype(BF16)

    x = x_ref[...]
    ms = jnp.mean(x * x, axis=-1, keepdims=True)
    h = (x * lax.rsqrt(ms + RMS_EPS) * g_ref[...]).astype(BF16)
    for n in range(IN_COLS // IN_TN):
        o_ref[:, n * IN_TN:(n + 1) * IN_TN] = jnp.dot(
            h, w_bf[:, n * IN_TN:(n + 1) * IN_TN], preferred_element_type=F32).astype(BF16)


def _in_proj(x2d, g, w, layer):
    rows = x2d.shape[0]
    return pl.pallas_call(
        _in_proj_kernel,
        grid=(rows // IN_TM,),
        in_specs=[
            pl.BlockSpec((IN_TM, D_MODEL), lambda i: (i, 0)),
            pl.BlockSpec((None, 1, D_MODEL), lambda i: (layer, 0, 0)),
            pl.BlockSpec((None, D_MODEL, IN_COLS), lambda i: (layer, 0, 0), pipeline_mode=pl.Buffered(1)),
        ],
        out_specs=pl.BlockSpec((IN_TM, IN_COLS), lambda i: (i, 0)),
        out_shape=jax.ShapeDtypeStruct((rows, IN_COLS), BF16),
        scratch_shapes=[pltpu.VMEM((D_MODEL, IN_COLS), BF16)],
        compiler_params=pltpu.CompilerParams(
            dimension_semantics=("arbitrary",), vmem_limit_bytes=VMEM_LIMIT),
        name="in_proj",
    )(x2d, g, w)


def _na_bias_kernel(rpb_ref, o_ref):
    lane = lax.broadcasted_iota(jnp.int32, (GRID_W, LANES), 1)
    j = lax.broadcasted_iota(jnp.int32, (GRID_W, LANES), 0)
    c = lane & (GRID_W - 1)
    col_start = jnp.clip(j - NA_COLS // 2, 0, GRID_W - NA_COLS)
    valid = (c >= col_start) & (c < col_start + NA_COLS)
    first = lane < GRID_W
    even, odd = [], []
    for a in range(2 * NA_ROWS - 1):
        vec = pltpu.roll(rpb_ref[a:a + 1, :], LANES - (NA_COLS - 1), axis=1)
        x = jnp.broadcast_to(vec, (GRID_W, LANES))
        even.append(pltpu.roll(x, 0, axis=1, stride=1, stride_axis=0))
        odd.append(pltpu.roll(x, GRID_W, axis=1, stride=1, stride_axis=0))
    for v in range(NA_ROWS):
        for i in range(NA_ROWS // 2):
            a0 = 2 * i - v + (NA_ROWS - 1)
            tile = jnp.where(first, even[a0], odd[a0 + 1])
            o_ref[v, :, i * LANES:(i + 1) * LANES] = jnp.where(valid, tile * LOG2E, NEG_INF)


def _na_bias_table(rpb):
    depth = rpb.shape[0]
    padded = jnp.pad(rpb.astype(F32), ((0, 0), (0, 0), (0, 1), (0, LANES - (2 * NA_COLS - 1))))
    nkeys = NA_ROWS * GRID_W
    return pl.pallas_call(
        _na_bias_kernel,
        grid=(depth, N_HEADS),
        in_specs=[pl.BlockSpec((None, None, 2 * NA_ROWS, LANES), lambda l, h: (l, h, 0, 0))],
        out_specs=pl.BlockSpec((None, None, NA_ROWS, GRID_W, nkeys), lambda l, h: (l, h, 0, 0, 0)),
        out_shape=jax.ShapeDtypeStruct((depth, N_HEADS, NA_ROWS, GRID_W, nkeys), F32),
        compiler_params=pltpu.CompilerParams(dimension_semantics=("arbitrary", "arbitrary")),
        name="na_bias",
    )(padded)


def _na_kernel(q_ref, k_ref, v_ref, z_ref, bias_ref, o_ref):
    seq = q_ref.shape[0]
    rows = seq // GRID_W
    half_rows = NA_ROWS // 2
    tile_keys = half_rows * GRID_W
    lane = lax.broadcasted_iota(jnp.int32, (GRID_W, LANES), 1)
    first_head = lane < HEAD_DIM
    masks = (first_head, jnp.logical_not(first_head))
    nt = (((1,), (1,)), ((), ()))
    ones = jnp.ones((tile_keys, LANES), BF16)
    row_start = lambda r: min(max(r - NA_ROWS // 2, 0), rows - NA_ROWS)

    for g0 in range(0, rows, NA_GROUP):
        group = range(g0, g0 + NA_GROUP)
        users = {}
        for r in group:
            for half in range(2):
                users.setdefault(row_start(r) + half_rows * half, []).append((r, half))
        qm = {}
        for r in group:
            qb = (q_ref[r * GRID_W:(r + 1) * GRID_W, :].astype(F32) * Q_SCALE_LOG2).astype(BF16)
            for hd in range(2):
                qm[r, hd] = jnp.where(masks[hd], qb, jnp.zeros_like(qb))
        s_half = {}
        for k, us in sorted(users.items()):
            lhs = jnp.concatenate([qm[r, hd] for r, _ in us for hd in range(2)], axis=0)
            part = lax.dot_general(lhs, k_ref[k * GRID_W:k * GRID_W + tile_keys, :], nt,
                                   preferred_element_type=F32)
            for i, (r, half) in enumerate(us):
                for hd in range(2):
                    blk = part[(2 * i + hd) * GRID_W:(2 * i + hd + 1) * GRID_W, :]
                    bias = bias_ref[hd, r - row_start(r), :, half * tile_keys:(half + 1) * tile_keys]
                    s_half[r, hd, half] = (blk + bias).astype(BF16)
        p_half = {}
        for r in group:
            for hd in range(2):
                s0, s1 = s_half[r, hd, 0], s_half[r, hd, 1]
                m = jnp.max(jnp.maximum(s0, s1), axis=-1, keepdims=True)
                p_half[r, hd, 0], p_half[r, hd, 1] = jnp.exp2(s0 - m), jnp.exp2(s1 - m)
        acc = {}
        for k, us in sorted(users.items()):
            lhs = jnp.concatenate([p_half[r, hd, half] for r, half in us for hd in range(2)], axis=0)
            vext = jnp.concatenate([v_ref[k * GRID_W:k * GRID_W + tile_keys, :], ones], axis=-1)
            part = jnp.dot(lhs, vext, preferred_element_type=F32)
            for i, (r, half) in enumerate(us):
                for hd in range(2):
                    blk = part[(2 * i + hd) * GRID_W:(2 * i + hd + 1) * GRID_W, :]
                    acc[r, hd] = blk if (r, hd) not in acc else acc[r, hd] + blk
        for r in group:
            y = jnp.where(first_head, acc[r, 0][:, :LANES], acc[r, 1][:, :LANES])
            d = jnp.where(first_head, acc[r, 0][:, LANES:], acc[r, 1][:, LANES:])
            z = z_ref[r * GRID_W:(r + 1) * GRID_W, :].astype(F32)
            o_ref[r * GRID_W:(r + 1) * GRID_W, :] = (y * (1.0 / d) * _silu(z)).astype(BF16)


NA_GROUP = 16


def _na(proj, bias, layer, batch, seq):
    def col(c0):
        return pl.BlockSpec((seq, LANES), lambda b, p, c0=c0: (b, c0 + p))

    return pl.pallas_call(
        _na_kernel,
        grid=(batch, N_HEAD_PAIRS),
        in_specs=[
            col(COL_QB), col(COL_KB), col(COL_VB), col(COL_ZB),
            pl.BlockSpec((None, 2, NA_ROWS, GRID_W, NA_ROWS * GRID_W), lambda b, p: (layer, p, 0, 0, 0)),
        ],
        out_specs=pl.BlockSpec((seq, LANES), lambda b, p: (b, p)),
        out_shape=jax.ShapeDtypeStruct((batch * seq, W_BRANCH), BF16),
        compiler_params=pltpu.CompilerParams(
            dimension_semantics=("arbitrary", "arbitrary"), vmem_limit_bytes=VMEM_LIMIT),
        name="na_attn",
    )(proj, proj, proj, proj, bias)


DIL_QB = 128
DIL_KW = 256
N_VARIANTS = 3


def _t5_bucket(rel):
    nb = T5_BUCKETS // 2
    max_exact = nb // 2
    n = np.abs(rel)
    large = max_exact + (np.log(np.maximum(n, 1) / max_exact) / np.log(T5_MAX_DIST / max_exact)
                         * (nb - max_exact)).astype(np.int32)
    large = np.minimum(large, nb - 1)
    return (np.where(rel > 0, nb, 0) + np.where(n < max_exact, n, large)).astype(np.int32)


def _dil_bias_table(t5_bias):
    n_pat = len(DIL_PATTERNS)
    lane = np.arange(DIL_VEC)
    n = np.where(lane < DIL_KW, lane, lane - DIL_VEC)
    in_range = (lane < DIL_KW) | (lane > DIL_VEC - DIL_QB)
    pick = np.zeros((n_pat, N_VARIANTS, T5_BUCKETS, DIL_VEC), np.float32)
    neg = np.zeros((n_pat, N_VARIANTS, 1, DIL_VEC), np.float32)
    for pi, (_, d) in enumerate(DIL_PATTERNS):
        for var in range(N_VARIANTS):
            step = n - DIL_HALF * var
            ok = in_range & (np.abs(step) <= DIL_HALF)
            bucket = _t5_bucket(d * np.clip(step, -DIL_HALF, DIL_HALF))
            pick[pi, var, bucket[ok], lane[ok]] = 1.0
            neg[pi, var, 0, ~ok] = NEG_INF
    vecs = jnp.einsum('bh,pvbn->hpvn', t5_bias.astype(F32), pick,
                      precision=lax.Precision.HIGHEST) * LOG2E + neg[:, :, 0][None]
    return pl.pallas_call(
        _dil_bias_kernel,
        grid=(N_HEADS,),
        in_specs=[pl.BlockSpec((None, n_pat * N_VARIANTS, DIL_VEC), lambda h: (h, 0, 0))],
        out_specs=pl.BlockSpec((None, n_pat, N_VARIANTS, DIL_QB, DIL_KW), lambda h: (h, 0, 0, 0, 0)),
        out_shape=jax.ShapeDtypeStruct((N_HEADS, n_pat, N_VARIANTS, DIL_QB, DIL_KW), F32),
        compiler_params=pltpu.CompilerParams(dimension_semantics=("arbitrary",)),
        name="dil_bias",
    )(vecs.reshape(N_HEADS, n_pat * N_VARIANTS, DIL_VEC))


DIL_VEC = 512


def _dil_bias_kernel(vec_ref, o_ref):
    for pi in range(len(DIL_PATTERNS)):
        for var in range(N_VARIANTS):
            i = pi * N_VARIANTS + var
            x = jnp.broadcast_to(vec_ref[i:i + 1, :], (DIL_QB, DIL_VEC))
            o_ref[pi, var] = pltpu.roll(x, 0, axis=1, stride=1, stride_axis=0)[:, :DIL_KW]


DIL_GROUP = 16
DIL_GROUP_FIRST = 16


def _dil_kernel(q_ref, k_ref, v_ref, z_ref, bias_ref, o_ref, qf, kf, vf, qg, kg, vg, qd,
                q4, k4, v4, m_s, l_s, acc_s):
    seq = q_ref.shape[0]
    conv_rows = 256
    lane = lax.broadcasted_iota(jnp.int32, (DIL_QB, LANES), 1)
    first_head = lane < HEAD_DIM
    nt = (((1,), (1,)), ((), ()))
    ones = jnp.ones((DIL_KW, LANES), BF16)

    def conv(i, carry):
        r0 = pl.multiple_of(i * conv_rows, conv_rows)
        qf[pl.ds(r0, conv_rows), :] = q_ref[pl.ds(r0, conv_rows), :].astype(F32) * Q_SCALE_LOG2
        kf[pl.ds(r0, conv_rows), :] = k_ref[pl.ds(r0, conv_rows), :].astype(F32)
        vf[pl.ds(r0, conv_rows), :] = v_ref[pl.ds(r0, conv_rows), :].astype(F32)
        return carry

    lax.fori_loop(0, seq // conv_rows, conv, 0)

    n4 = seq // 4
    n16 = seq // 16

    def to_order4(i, carry):
        dst = pl.multiple_of(i * conv_rows, conv_rows)
        src = pl.ds(dst // n4 + 4 * (dst % n4), conv_rows, stride=4)
        for a, b, c in ((qf, qg, q4), (kf, kg, k4), (vf, vg, v4)):
            part = a[src, :]
            b[pl.ds(dst, conv_rows), :] = part
            c[pl.ds(dst, conv_rows), :] = part.astype(BF16)
        return carry

    lax.fori_loop(0, seq // conv_rows, to_order4, 0, unroll=2)

    def window(src, res, start, size, n_sub):
        if n_sub == n16:
            rows = pl.ds((res % 4) * n4 + res // 4 + 4 * start, size, stride=4)
            return src[rows, :].astype(BF16)
        return src[pl.ds(pl.multiple_of(res * n_sub + start, DIL_HALF), size), :]

    order4_state = (qf, kf, vf)
    token_state = (m_s, l_s, acc_s)
    plan = ((2, 16, (qg, kg, vg), None, order4_state),
            (1, 4, (q4, k4, v4), order4_state, token_state),
            (0, 1, (qd, k_ref, v_ref), token_state, token_state))

    for step, (pi, d, (q_src, k_src, v_src), st_in, st_out) in enumerate(plan):
        n_sub = seq // d
        nblk = n_sub // DIL_QB

        if step == 2:
            def scaled_q(i, carry):
                r0 = pl.multiple_of(i * conv_rows, conv_rows)
                qd[pl.ds(r0, conv_rows), :] = (
                    q_ref[pl.ds(r0, conv_rows), :].astype(F32) * Q_SCALE_LOG2).astype(BF16)
                return carry

            lax.fori_loop(0, seq // conv_rows, scaled_q, 0, unroll=2)

        def rows_out(res, i0, step=step):
            if step == 0:
                return pl.ds((res % 4) * n4 + res // 4 + 4 * i0, DIL_QB, stride=4)
            if step == 1:
                return pl.ds(res + 4 * i0, DIL_QB, stride=4)
            return pl.ds(pl.multiple_of(i0, DIL_HALF), DIL_QB)

        n_group = DIL_GROUP_FIRST if step == 0 else DIL_GROUP

        def group(gi, carry, pi=pi, n_sub=n_sub, nblk=nblk, q_src=q_src, k_src=k_src, v_src=v_src,
                  st_in=st_in, st_out=st_out, rows_out=rows_out, n_group=n_group):
            units = []
            for i in range(n_group):
                u = gi * n_group + i
                res = u // nblk
                i0 = (u % nblk) * DIL_QB
                ks = jnp.clip(i0 - DIL_HALF, 0, n_sub - DIL_KW)
                units.append((res, i0, ks, (i0 - ks) // DIL_HALF))
            s_tiles, m_tiles = [], []
            for res, i0, ks, var in units:
                qb = window(q_src, res, i0, DIL_QB, n_sub)
                kw = window(k_src, res, ks, DIL_KW, n_sub)
                for hd in range(2):
                    keep = first_head if hd == 0 else jnp.logical_not(first_head)
                    qm = jnp.where(keep, qb, jnp.zeros_like(qb))
                    s = lax.dot_general(qm, kw, nt, preferred_element_type=F32) + bias_ref[hd, pi, var]
                    s = s.astype(BF16)
                    s_tiles.append(s)
                    m_tiles.append(jnp.max(s, axis=-1, keepdims=True))
            p_tiles = [jnp.exp2(s - m) for s, m in zip(s_tiles, m_tiles)]
            for i, (res, i0, ks, var) in enumerate(units):
                vw = window(v_src, res, ks, DIL_KW, n_sub)
                vext = jnp.concatenate([vw, ones], axis=-1)
                r0 = jnp.dot(p_tiles[2 * i], vext, preferred_element_type=F32)
                r1 = jnp.dot(p_tiles[2 * i + 1], vext, preferred_element_type=F32)
                m_cur = jnp.where(first_head, m_tiles[2 * i].astype(F32), m_tiles[2 * i + 1].astype(F32))
                l_cur = jnp.where(first_head, r0[:, LANES:], r1[:, LANES:])
                acc_cur = jnp.where(first_head, r0[:, :LANES], r1[:, :LANES])
                dst = rows_out(res, i0)
                if st_in is None:
                    m_new, l_new, acc_new = m_cur, l_cur, acc_cur
                else:
                    src = pl.ds(pl.multiple_of(res * n_sub + i0, DIL_HALF), DIL_QB)
                    m_old = st_in[0][src, :]
                    m_new = jnp.maximum(m_old, m_cur)
                    a_old = jnp.exp2(m_old - m_new)
                    a_cur = jnp.exp2(m_cur - m_new)
                    l_new = a_old * st_in[1][src, :] + a_cur * l_cur
                    acc_new = a_old * st_in[2][src, :] + a_cur * acc_cur
                st_out[0][dst, :] = m_new
                st_out[1][dst, :] = l_new
                st_out[2][dst, :] = acc_new
            return carry

        lax.fori_loop(0, seq // DIL_QB // n_group, group, 0)

    def fin(i, carry):
        r0 = pl.multiple_of(i * conv_rows, conv_rows)
        z = z_ref[pl.ds(r0, conv_rows), :].astype(F32)
        y = acc_s[pl.ds(r0, conv_rows), :] * (1.0 / l_s[pl.ds(r0, conv_rows), :])
        o_ref[pl.ds(r0, conv_rows), :] = (y * _silu(z)).astype(BF16)
        return carry

    lax.fori_loop(0, seq // conv_rows, fin, 0, unroll=2)


def _dil(proj, bias, batch, seq):
    def col(c0):
        return pl.BlockSpec((seq, LANES), lambda b, p, c0=c0: (b, c0 + p))

    scratch = ([pltpu.VMEM((seq, LANES), F32) for _ in range(6)]
               + [pltpu.VMEM((seq, LANES), BF16) for _ in range(4)]
               + [pltpu.VMEM((seq, LANES), F32) for _ in range(3)])
    return pl.pallas_call(
        _dil_kernel,
        grid=(batch, N_HEAD_PAIRS),
        in_specs=[
            col(COL_QC), col(COL_KC), col(COL_VC), col(COL_ZC),
            pl.BlockSpec((2, len(DIL_PATTERNS), N_VARIANTS, DIL_QB, DIL_KW),
                         lambda b, p: (p, 0, 0, 0, 0)),
        ],
        out_specs=pl.BlockSpec((seq, LANES), lambda b, p: (b, p)),
        out_shape=jax.ShapeDtypeStruct((batch * seq, W_BRANCH), BF16),
        scratch_shapes=scratch,
        compiler_params=pltpu.CompilerParams(
            dimension_semantics=("arbitrary", "arbitrary"), vmem_limit_bytes=VMEM_LIMIT),
        name="dil_attn",
    )(proj, proj, proj, proj, bias)


N_POW = CHUNK + 1
PREP_GROUPS = 8


def _ssm_prep_kernel(lre_ref, lim_ref, ldt_ref, btre_ref, btim_ref, cre_ref, cim_ref,
                     toep_ref, win_ref, wout_ref, decay_ref):
    n = CHUNK * SSM_GROUP
    lane = lax.broadcasted_iota(jnp.int32, (SSM_GROUP, LANES), 1)
    lo = lane < SSM_STATE
    lane_n = lax.broadcasted_iota(jnp.int32, (SSM_GROUP, n), 1)
    nt = (((1,), (1,)), ((), ()))
    pair = lambda x, y: jnp.where(lo, x, y)

    for j in range(PREP_GROUPS):
        xs, ys, bbs, pw_last = {}, {}, {}, {}
        for dr in range(2):
            lam_re = lre_ref[0, dr, j]
            lam_im = lim_ref[0, dr, j]
            dt = jnp.exp(ldt_ref[0, dr, j])
            mag = jnp.exp(lam_re * dt)
            lb_re = mag * jnp.cos(lam_im * dt)
            lb_im = mag * jnp.sin(lam_im * dt)
            n_re = lb_re - 1.0
            den = lam_re * lam_re + lam_im * lam_im
            q_re = (n_re * lam_re + lb_im * lam_im) / den
            q_im = (lb_im * lam_re - n_re * lam_im) / den
            bt_re = btre_ref[0, dr, j]
            bt_im = btim_ref[0, dr, j]
            bb_re = q_re * bt_re - q_im * bt_im
            bb_im = q_re * bt_im + q_im * bt_re
            bbs[dr] = (bb_re, bb_im)
            c_re = cre_ref[0, dr, j]
            c_im = cim_ref[0, dr, j]
            pw_re = jnp.ones_like(lb_re)
            pw_im = jnp.zeros_like(lb_re)
            for k in range(N_POW):
                xs[dr, k] = (pw_re * bb_re - pw_im * bb_im, pw_re * bb_im + pw_im * bb_re)
                ys[dr, k] = (pw_re * c_re - pw_im * c_im, pw_re * c_im + pw_im * c_re)
                pw_last[dr] = (pw_re, pw_im)
                pw_re, pw_im = pw_re * lb_re - pw_im * lb_im, pw_re * lb_im + pw_im * lb_re

        step = [GROUPS_PER_SLAB * (a // GROUPS_PER_SLAB) + (a % GROUPS_PER_SLAB - j) % GROUPS_PER_SLAB
                for a in range(CHUNK)]
        rows = lambda a: slice(a * SSM_GROUP, (a + 1) * SSM_GROUP)

        decay_ref[0, 0, j:j + 1, :] = pair(pw_last[0][0], pw_last[1][0])[0:1]
        decay_ref[0, 1, j:j + 1, :] = pair(pw_last[0][1], pw_last[1][1])[0:1]

        for a, s in enumerate(step):
            win_ref[0, j, rows(a), :LANES] = pair(xs[0, CHUNK - 1 - s][0], xs[1, s][0]).astype(BF16)
            win_ref[0, j, rows(a), LANES:] = pair(xs[0, CHUNK - 1 - s][1], xs[1, s][1]).astype(BF16)

        z_re = jnp.concatenate([pair(ys[0, t + 1][0], ys[1, CHUNK - t][0]) for t in step], axis=0)
        z_nim = jnp.concatenate([pair(-ys[0, t + 1][1], -ys[1, CHUNK - t][1]) for t in step], axis=0)
        wout_ref[0, j, :LANES, :] = z_re.T.astype(BF16)
        wout_ref[0, j, LANES:, :] = z_nim.T.astype(BF16)

        kf = lax.dot_general(
            pair(bbs[0][0], -bbs[0][1]),
            jnp.concatenate([pair(*ys[0, m]) for m in range(CHUNK)], axis=0),
            nt, precision=lax.Precision.HIGHEST, preferred_element_type=F32)
        kb = lax.dot_general(
            pair(bbs[1][0], -bbs[1][1]),
            jnp.concatenate([pair(*ys[1, CHUNK - 1 - i]) for i in range(CHUNK)], axis=0),
            nt, precision=lax.Precision.HIGHEST, preferred_element_type=F32)
        for a, s in enumerate(step):
            fwd = pltpu.roll(kf, SSM_GROUP * s, axis=1) if s else kf
            shift_b = (n - SSM_GROUP * (CHUNK - 1 - s)) % n
            bwd = pltpu.roll(kb, shift_b, axis=1) if shift_b else kb
            t_nat = (jnp.where(lane_n >= SSM_GROUP * s, fwd, 0.0)
                     + jnp.where(lane_n < SSM_GROUP * (s + 1), bwd, 0.0))
            for h in range(2):
                half = t_nat[:, h * LANES:(h + 1) * LANES]
                if j:
                    half = pltpu.roll(half, SSM_GROUP * j, axis=1)
                toep_ref[0, j, rows(a), h * LANES:(h + 1) * LANES] = half.astype(BF16)


def _ssm_prep(lam_re, lam_im, log_dt, b_re, b_im, c_re, c_im):
    depth = lam_re.shape[0]
    G, P, C = SSM_GROUPS, SSM_STATE, SSM_GROUP
    n = CHUNK * C
    gb = PREP_GROUPS
    twice = lambda a: jnp.concatenate([a, a], axis=-1)
    rep = lambda a: jnp.broadcast_to(twice(a.astype(F32))[:, :, :, None, :], (depth, 2, G, C, LANES))
    ldt = jnp.broadcast_to(log_dt.astype(F32)[:, :, :, None, None], (depth, 2, G, C, LANES))
    bt = lambda a: twice(a.astype(F32).transpose(0, 1, 2, 4, 3))
    spec_in = pl.BlockSpec((1, 2, gb, C, LANES), lambda l, i: (l, 0, i, 0, 0))
    spec_w = pl.BlockSpec((1, gb, n, n), lambda l, i: (l, i, 0, 0))
    shp_w = jax.ShapeDtypeStruct((depth, G, n, n), BF16)
    return pl.pallas_call(
        _ssm_prep_kernel,
        grid=(depth, G // gb),
        in_specs=[spec_in] * 7,
        out_specs=[spec_w, spec_w, spec_w, pl.BlockSpec((1, 2, gb, LANES), lambda l, i: (l, 0, i, 0))],
        out_shape=[shp_w, shp_w, shp_w, jax.ShapeDtypeStruct((depth, 2, G, LANES), F32)],
        compiler_params=pltpu.CompilerParams(
            dimension_semantics=("arbitrary", "arbitrary"), vmem_limit_bytes=VMEM_LIMIT),
        name="ssm_prep",
    )(rep(lam_re), rep(lam_im), ldt, bt(b_re), bt(b_im), twice(c_re.astype(F32)), twice(c_im.astype(F32)))


GROUPS_PER_SLAB = LANES // SSM_GROUP
N_SLABS = W_BRANCH // LANES
RELAYOUT_ROWS = 32


def _ssm_kernel(xa_ref, toep_ref, win_ref, wout_ref, decay_ref, d_ref, o_ref, xf, u_s, st, hs, yq):
    seq = xa_ref.shape[0]
    nchunk = seq // CHUNK
    nrb = nchunk // RELAYOUT_ROWS
    gps = GROUPS_PER_SLAB
    lane_rb = lax.broadcasted_iota(jnp.int32, (RELAYOUT_ROWS, LANES), 1)
    seg_masks = [(lane_rb >= SSM_GROUP * sg) & (lane_rb < SSM_GROUP * (sg + 1)) for sg in range(gps)]
    conv_rows = 512

    for q in range(N_SLABS):
        c_lo = q * LANES

        def conv(i, carry, c_lo=c_lo):
            r0 = pl.multiple_of(i * conv_rows, conv_rows)
            xf[pl.ds(r0, conv_rows), :] = xa_ref[pl.ds(r0, conv_rows), c_lo:c_lo + LANES].astype(F32)
            return carry

        lax.fori_loop(0, seq // conv_rows, conv, 0)

        for hh in range(2):
            def fwd_relayout(rb, carry, hh=hh):
                c0 = pl.multiple_of(rb * RELAYOUT_ROWS, RELAYOUT_ROWS)
                rolled = []
                for r in range(gps):
                    xs = xf[pl.ds(CHUNK * c0 + gps * hh + r, RELAYOUT_ROWS, stride=CHUNK), :]
                    rolled.append(xs if r == 0 else pltpu.roll(xs, SSM_GROUP * r, axis=1))
                for j in range(gps):
                    out = rolled[(0 - j) % gps]
                    for sg in range(1, gps):
                        out = jnp.where(seg_masks[sg], rolled[(sg - j) % gps], out)
                    u_s[j, pl.ds(c0, RELAYOUT_ROWS), hh * LANES:(hh + 1) * LANES] = out.astype(BF16)
                return carry

            lax.fori_loop(0, nrb, fwd_relayout, 0, unroll=2)

        for j in range(gps):
            s_in = jnp.dot(u_s[j], win_ref[gps * q + j], preferred_element_type=F32)
            st[0, pl.ds(j, nchunk, stride=gps), :] = s_in[:, :LANES]
            st[1, pl.ds(j, nchunk, stride=gps), :] = s_in[:, LANES:]

        g_lo = gps * q
        a_re = decay_ref[0, g_lo:g_lo + gps, :]
        a_im = decay_ref[1, g_lo:g_lo + gps, :]
        lo, hi = slice(0, SSM_STATE), slice(SSM_STATE, LANES)

        def scan(i, carry):
            hrf, hif, hrb, hib = carry
            rf = pl.ds(pl.multiple_of(i * gps, gps), gps)
            rb_ = pl.ds(pl.multiple_of((nchunk - 1 - i) * gps, gps), gps)
            hs[0, rf, lo] = hrf[:, lo]
            hs[1, rf, lo] = hif[:, lo]
            hs[0, rb_, hi] = hrb[:, hi]
            hs[1, rb_, hi] = hib[:, hi]
            return (a_re * hrf - a_im * hif + st[0, rf, :], a_re * hif + a_im * hrf + st[1, rf, :],
                    a_re * hrb - a_im * hib + st[0, rb_, :], a_re * hib + a_im * hrb + st[1, rb_, :])

        zero = jnp.zeros((gps, LANES), F32)
        lax.fori_loop(0, nchunk, scan, (zero, zero, zero, zero), unroll=8)

        for j in range(gps):
            g = gps * q + j
            h_in = jnp.concatenate([hs[0, pl.ds(j, nchunk, stride=gps), :].astype(BF16),
                                    hs[1, pl.ds(j, nchunk, stride=gps), :].astype(BF16)], axis=-1)
            yq[j] = (jnp.dot(u_s[j], toep_ref[g], preferred_element_type=F32)
                     + jnp.dot(h_in, wout_ref[g], preferred_element_type=F32))

        d_row = d_ref[:, c_lo:c_lo + LANES]
        for hh in range(2):
            def bwd_relayout(rb, carry, hh=hh, d_row=d_row, q=q):
                c0 = pl.multiple_of(rb * RELAYOUT_ROWS, RELAYOUT_ROWS)
                ys = [yq[j, pl.ds(c0, RELAYOUT_ROWS), hh * LANES:(hh + 1) * LANES] for j in range(gps)]
                for r in range(gps):
                    merged = ys[(0 - r) % gps]
                    for sg in range(1, gps):
                        merged = jnp.where(seg_masks[sg], ys[(sg - r) % gps], merged)
                    if r:
                        merged = pltpu.roll(merged, LANES - SSM_GROUP * r, axis=1)
                    tok = pl.ds(CHUNK * c0 + gps * hh + r, RELAYOUT_ROWS, stride=CHUNK)
                    o_ref[q, tok, :] = merged + d_row * xf[tok, :]
                return carry

            lax.fori_loop(0, nrb, bwd_relayout, 0, unroll=2)


def _ssm(proj, toep, w_in, w_out, decay, d_skip, layer, batch, seq):
    nchunk = seq // CHUNK
    n = CHUNK * SSM_GROUP
    wspec = pl.BlockSpec((None, SSM_GROUPS, n, n), lambda b: (layer, 0, 0, 0), pipeline_mode=pl.Buffered(1))
    return pl.pallas_call(
        _ssm_kernel,
        grid=(batch,),
        in_specs=[
            pl.BlockSpec((seq, W_BRANCH), lambda b: (b, COL_XA // N_SLABS)),
            wspec, wspec, wspec,
            pl.BlockSpec((None, 2, SSM_GROUPS, LANES), lambda b: (layer, 0, 0, 0)),
            pl.BlockSpec((None, 1, W_BRANCH), lambda b: (layer, 0, 0)),
        ],
        out_specs=pl.BlockSpec((N_SLABS, seq, LANES), lambda b: (0, b, 0)),
        out_shape=jax.ShapeDtypeStruct((N_SLABS, batch * seq, LANES), F32),
        scratch_shapes=[
            pltpu.VMEM((seq, LANES), F32),
            pltpu.VMEM((GROUPS_PER_SLAB, nchunk, n), BF16),
            pltpu.VMEM((2, nchunk * GROUPS_PER_SLAB, LANES), F32),
            pltpu.VMEM((2, nchunk * GROUPS_PER_SLAB, LANES), F32),
            pltpu.VMEM((GROUPS_PER_SLAB, nchunk, n), F32),
        ],
        compiler_params=pltpu.CompilerParams(
            dimension_semantics=("arbitrary",), vmem_limit_bytes=VMEM_LIMIT),
        name="ssm_mix",
    )(proj, toep, w_in, w_out, decay, d_skip)


OUT_TM = 1024


def _out_proj_kernel(ya_ref, za_ref, yb_ref, yc_ref, x_ref, w_ref, gw_ref, gb_ref, fg_ref, o_ref,
                     w_bf, gw_bf, *, final):
    @pl.when(pl.program_id(0) == 0)
    def _():
        for n in range(MIX_WIDTH // W_BRANCH):
            w_bf[n * W_BRANCH:(n + 1) * W_BRANCH, :] = w_ref[n * W_BRANCH:(n + 1) * W_BRANCH, :].astype(BF16)
        gw_bf[...] = gw_ref[...].astype(BF16)

    y = jnp.concatenate([ya_ref[i] for i in range(N_SLABS)], axis=-1)
    g = _gelu_tanh(y)
    gate = jnp.dot(g.astype(BF16), gw_bf[...], preferred_element_type=F32) + gb_ref[...]
    ya = g * _sigmoid(gate) * _silu(za_ref[...].astype(F32))
    delta = (jnp.dot(ya.astype(BF16), w_bf[:W_BRANCH, :], preferred_element_type=F32)
             + jnp.dot(yb_ref[...], w_bf[W_BRANCH:2 * W_BRANCH, :], preferred_element_type=F32)
             + jnp.dot(yc_ref[...], w_bf[2 * W_BRANCH:, :], preferred_element_type=F32))
    x = x_ref[...] + delta
    if final:
        ms = jnp.mean(x * x, axis=-1, keepdims=True)
        x = x * lax.rsqrt(ms + RMS_EPS) * fg_ref[...]
    o_ref[...] = x


def _out_proj(ya_pre, proj, yb, yc, x2d, w, glu_w, glu_b, final_g, layer, final):
    rows = x2d.shape[0]
    row_blk = lambda width: pl.BlockSpec((OUT_TM, width), lambda i: (i, 0))
    const = lambda shape: pl.BlockSpec((None,) + shape, lambda i: (layer,) + (0,) * len(shape),
                                       pipeline_mode=pl.Buffered(1))
    return pl.pallas_call(
        functools.partial(_out_proj_kernel, final=final),
        grid=(rows // OUT_TM,),
        in_specs=[
            pl.BlockSpec((N_SLABS, OUT_TM, LANES), lambda i: (0, i, 0)),
            pl.BlockSpec((OUT_TM, W_BRANCH), lambda i: (i, COL_ZA // N_SLABS)),
            row_blk(W_BRANCH), row_blk(W_BRANCH), row_blk(D_MODEL),
            const((MIX_WIDTH, D_MODEL)), const((W_BRANCH, W_BRANCH)),
            const((1, W_BRANCH)),
            pl.BlockSpec((None, 1, D_MODEL), lambda i: (0, 0, 0), pipeline_mode=pl.Buffered(1)),
        ],
        out_specs=row_blk(D_MODEL),
        out_shape=jax.ShapeDtypeStruct((rows, D_MODEL), F32),
        scratch_shapes=[pltpu.VMEM((MIX_WIDTH, D_MODEL), BF16), pltpu.VMEM((W_BRANCH, W_BRANCH), BF16)],
        compiler_params=pltpu.CompilerParams(
            dimension_semantics=("arbitrary",), vmem_limit_bytes=VMEM_LIMIT),
        name="out_proj_final" if final else "out_proj",
    )(ya_pre, proj, yb, yc, x2d, w, glu_w, glu_b, final_g)


def kernel(x, norm_g, w_in, w_out, ssm_lam_re, ssm_lam_im, ssm_log_dt, ssm_b_re, ssm_b_im, ssm_c_re,
           ssm_c_im, ssm_d, glu_w, glu_b, na_rpb, t5_bias, final_g):
    batch, seq, _ = x.shape
    depth = w_in.shape[0]
    x2d = x.astype(F32).reshape(batch * seq, D_MODEL)
    toep, s_in, s_out, decay = _ssm_prep(ssm_lam_re, ssm_lam_im, ssm_log_dt, ssm_b_re, ssm_b_im,
                                         ssm_c_re, ssm_c_im)
    dil_bias = _dil_bias_table(t5_bias)
    na_bias = _na_bias_table(na_rpb)
    w_in_f, w_out_f, glu_w_f = w_in.astype(F32), w_out.astype(F32), glu_w.astype(F32)
    row3 = lambda a: a.astype(F32).reshape(a.shape[0], 1, a.shape[1])
    norm_g3, ssm_d3, glu_b3 = row3(norm_g), row3(ssm_d), row3(glu_b)
    final_g3 = final_g.astype(F32).reshape(1, 1, D_MODEL)
    for l in range(depth):
        proj = _in_proj(x2d, norm_g3, w_in_f, l)
        ya_pre = _ssm(proj, toep, s_in, s_out, decay, ssm_d3, l, batch, seq)
        yb = _na(proj, na_bias, l, batch, seq)
        yc = _dil(proj, dil_bias, batch, seq)
        x2d = _out_proj(ya_pre, proj, yb, yc, x2d, w_out_f, glu_w_f, glu_b3, final_g3, l,
                        final=(l == depth - 1))
    return x2d.reshape(batch, seq, D_MODEL).astype(x.dtype)
```

```python
import functools

import numpy as np
import jax
import jax.numpy as jnp
from jax import lax
from jax.experimental import pallas as pl
from jax.experimental.pallas import tpu as pltpu

F32 = jnp.float32
BF16 = jnp.bfloat16

D_MODEL = 1024
HEAD_DIM = 64
W_BRANCH = 512
N_HEADS = W_BRANCH // HEAD_DIM
N_HEAD_PAIRS = N_HEADS // 2
SSM_GROUP = 16
SSM_GROUPS = W_BRANCH // SSM_GROUP
SSM_STATE = 64
GRID_W = 64
NA_ROWS = 8
NA_COLS = 16
DIL_PATTERNS = ((128, 1), (512, 4), (2048, 16))
DIL_HALF = 64
T5_BUCKETS = 32
T5_MAX_DIST = 1024
RMS_EPS = 1e-6
NEG_INF = -1e30
IN_COLS = 10 * W_BRANCH
MIX_WIDTH = 3 * W_BRANCH

COL_XA, COL_ZA, COL_QB, COL_KB, COL_VB, COL_ZB, COL_QC, COL_KC, COL_VC, COL_ZC = (
    4 * i for i in range(10))

LOG2E = float(np.log2(np.e))
Q_SCALE_LOG2 = HEAD_DIM ** -0.5 * LOG2E

LANES = 128
CHUNK = 16
VMEM_LIMIT = 56 * 1024 * 1024


def _sigmoid(z):
    return 0.5 * (1.0 + jnp.tanh(0.5 * z))


def _silu(z):
    return z * _sigmoid(z)


def _gelu_tanh(x):
    return 0.5 * x * (1.0 + jnp.tanh(np.sqrt(2.0 / np.pi).astype(np.float32) * (x + 0.044715 * (x * x * x))))


IN_TM = 512
IN_TN = 512


def _in_proj_kernel(x_ref, g_ref, w_ref, o_ref, w_bf):
    @pl.when(pl.program_id(0) == 0)
    def _():
        for n in range(IN_COLS // IN_TN):
            w_bf[:, n * IN_TN:(n + 1) * IN_TN] = w_ref[:, n * IN_TN:(n + 1) * IN_TN].astype(BF16)

    x = x_ref[...]
    ms = jnp.mean(x * x, axis=-1, keepdims=True)
    h = (x * lax.rsqrt(ms + RMS_EPS) * g_ref[...]).astype(BF16)
    for n in range(IN_COLS // IN_TN):
        o_ref[:, n * IN_TN:(n + 1) * IN_TN] = jnp.dot(
            h, w_bf[:, n * IN_TN:(n + 1) * IN_TN], preferred_element_type=F32).astype(BF16)


def _in_proj(x2d, g, w, layer):
    rows = x2d.shape[0]
    return pl.pallas_call(
        _in_proj_kernel,
        grid=(rows // IN_TM,),
        in_specs=[
            pl.BlockSpec((IN_TM, D_MODEL), lambda i: (i, 0)),
            pl.BlockSpec((None, 1, D_MODEL), lambda i: (layer, 0, 0)),
            pl.BlockSpec((None, D_MODEL, IN_COLS), lambda i: (layer, 0, 0), pipeline_mode=pl.Buffered(1)),
        ],
        out_specs=pl.BlockSpec((IN_TM, IN_COLS), lambda i: (i, 0)),
        out_shape=jax.ShapeDtypeStruct((rows, IN_COLS), BF16),
        scratch_shapes=[pltpu.VMEM((D_MODEL, IN_COLS), BF16)],
        compiler_params=pltpu.CompilerParams(
            dimension_semantics=("arbitrary",), vmem_limit_bytes=VMEM_LIMIT),
        name="in_proj",
    )(x2d, g, w)


def _na_bias_kernel(rpb_ref, o_ref):
    lane = lax.broadcasted_iota(jnp.int32, (GRID_W, LANES), 1)
    j = lax.broadcasted_iota(jnp.int32, (GRID_W, LANES), 0)
    c = lane & (GRID_W - 1)
    col_start = jnp.clip(j - NA_COLS // 2, 0, GRID_W - NA_COLS)
    valid = (c >= col_start) & (c < col_start + NA_COLS)
    first = lane < GRID_W
    even, odd = [], []
    for a in range(2 * NA_ROWS - 1):
        vec = pltpu.roll(rpb_ref[a:a + 1, :], LANES - (NA_COLS - 1), axis=1)
        x = jnp.broadcast_to(vec, (GRID_W, LANES))
        even.append(pltpu.roll(x, 0, axis=1, stride=1, stride_axis=0))
        odd.append(pltpu.roll(x, GRID_W, axis=1, stride=1, stride_axis=0))
    for v in range(NA_ROWS):
        for i in range(NA_ROWS // 2):
            a0 = 2 * i - v + (NA_ROWS - 1)
            tile = jnp.where(first, even[a0], odd[a0 + 1])
            o_ref[v, :, i * LANES:(i + 1) * LANES] = jnp.where(valid, tile * LOG2E, NEG_INF)


def _na_bias_table(rpb):
    depth = rpb.shape[0]
    padded = jnp.pad(rpb.astype(F32), ((0, 0), (0, 0), (0, 1), (0, LANES - (2 * NA_COLS - 1))))
    nkeys = NA_ROWS * GRID_W
    return pl.pallas_call(
        _na_bias_kernel,
        grid=(depth, N_HEADS),
        in_specs=[pl.BlockSpec((None, None, 2 * NA_ROWS, LANES), lambda l, h: (l, h, 0, 0))],
        out_specs=pl.BlockSpec((None, None, NA_ROWS, GRID_W, nkeys), lambda l, h: (l, h, 0, 0, 0)),
        out_shape=jax.ShapeDtypeStruct((depth, N_HEADS, NA_ROWS, GRID_W, nkeys), F32),
        compiler_params=pltpu.CompilerParams(dimension_semantics=("arbitrary", "arbitrary")),
        name="na_bias",
    )(padded)


def _na_kernel(q_ref, k_ref, v_ref, z_ref, bias_ref, o_ref):
    seq = q_ref.shape[0]
    rows = seq // GRID_W
    half_rows = NA_ROWS // 2
    tile_keys = half_rows * GRID_W
    lane = lax.broadcasted_iota(jnp.int32, (GRID_W, LANES), 1)
    first_head = lane < HEAD_DIM
    masks = (first_head, jnp.logical_not(first_head))
    nt = (((1,), (1,)), ((), ()))
    ones = jnp.ones((tile_keys, LANES), BF16)
    row_start = lambda r: min(max(r - NA_ROWS // 2, 0), rows - NA_ROWS)

    for g0 in range(0, rows, NA_GROUP):
        group = range(g0, g0 + NA_GROUP)
        users = {}
        for r in group:
            for half in range(2):
                users.setdefault(row_start(r) + half_rows * half, []).append((r, half))
        qm = {}
        for r in group:
            qb = (q_ref[r * GRID_W:(r + 1) * GRID_W, :].astype(F32) * Q_SCALE_LOG2).astype(BF16)
            for hd in range(2):
                qm[r, hd] = jnp.where(masks[hd], qb, jnp.zeros_like(qb))
        s_half = {}
        for k, us in sorted(users.items()):
            lhs = jnp.concatenate([qm[r, hd] for r, _ in us for hd in range(2)], axis=0)
            part = lax.dot_general(lhs, k_ref[k * GRID_W:k * GRID_W + tile_keys, :], nt,
                                   preferred_element_type=F32)
            for i, (r, half) in enumerate(us):
                for hd in range(2):
                    blk = part[(2 * i + hd) * GRID_W:(2 * i + hd + 1) * GRID_W, :]
                    bias = bias_ref[hd, r - row_start(r), :, half * tile_keys:(half + 1) * tile_keys]
                    s_half[r, hd, half] = (blk + bias).astype(BF16)
        p_half = {}
        for r in group:
            for hd in range(2):
                s0, s1 = s_half[r, hd, 0], s_half[r, hd, 1]
                m = jnp.max(jnp.maximum(s0, s1), axis=-1, keepdims=True)
                p_half[r, hd, 0], p_half[r, hd, 1] = jnp.exp2(s0 - m), jnp.exp2(s1 - m)
        acc = {}
        for k, us in sorted(users.items()):
            lhs = jnp.concatenate([p_half[r, hd, half] for r, half in us for hd in range(2)], axis=0)
            vext = jnp.concatenate([v_ref[k * GRID_W:k * GRID_W + tile_keys, :], ones], axis=-1)
            part = jnp.dot(lhs, vext, preferred_element_type=F32)
            for i, (r, half) in enumerate(us):
                for hd in range(2):
                    blk = part[(2 * i + hd) * GRID_W:(2 * i + hd + 1) * GRID_W, :]
                    acc[r, hd] = blk if (r, hd) not in acc else acc[r, hd] + blk
        for r in group:
            y = jnp.where(first_head, acc[r, 0][:, :LANES], acc[r, 1][:, :LANES])
            d = jnp.where(first_head, acc[r, 0][:, LANES:], acc[r, 1][:, LANES:])
            z = z_ref[r * GRID_W:(r + 1) * GRID_W, :].astype(F32)
            o_ref[r * GRID_W:(r + 1) * GRID_W, :] = (y * (1.0 / d) * _silu(z)).astype(BF16)


NA_GROUP = 16


def _na(proj, bias, layer, batch, seq):
    def col(c0):
        return pl.BlockSpec((seq, LANES), lambda b, p, c0=c0: (b, c0 + p))

    return pl.pallas_call(
        _na_kernel,
        grid=(batch, N_HEAD_PAIRS),
        in_specs=[
            col(COL_QB), col(COL_KB), col(COL_VB), col(COL_ZB),
            pl.BlockSpec((None, 2, NA_ROWS, GRID_W, NA_ROWS * GRID_W), lambda b, p: (layer, p, 0, 0, 0)),
        ],
        out_specs=pl.BlockSpec((seq, LANES), lambda b, p: (b, p)),
        out_shape=jax.ShapeDtypeStruct((batch * seq, W_BRANCH), BF16),
        compiler_params=pltpu.CompilerParams(
            dimension_semantics=("arbitrary", "arbitrary"), vmem_limit_bytes=VMEM_LIMIT),
        name="na_attn",
    )(proj, proj, proj, proj, bias)


DIL_QB = 128
DIL_KW = 256
N_VARIANTS = 3


def _t5_bucket(rel):
    nb = T5_BUCKETS // 2
    max_exact = nb // 2
    n = np.abs(rel)
    large = max_exact + (np.log(np.maximum(n, 1) / max_exact) / np.log(T5_MAX_DIST / max_exact)
                         * (nb - max_exact)).astype(np.int32)
    large = np.minimum(large, nb - 1)
    return (np.where(rel > 0, nb, 0) + np.where(n < max_exact, n, large)).astype(np.int32)


def _dil_bias_table(t5_bias):
    n_pat = len(DIL_PATTERNS)
    lane = np.arange(DIL_VEC)
    n = np.where(lane < DIL_KW, lane, lane - DIL_VEC)
    in_range = (lane < DIL_KW) | (lane > DIL_VEC - DIL_QB)
    pick = np.zeros((n_pat, N_VARIANTS, T5_BUCKETS, DIL_VEC), np.float32)
    neg = np.zeros((n_pat, N_VARIANTS, 1, DIL_VEC), np.float32)
    for pi, (_, d) in enumerate(DIL_PATTERNS):
        for var in range(N_VARIANTS):
            step = n - DIL_HALF * var
            ok = in_range & (np.abs(step) <= DIL_HALF)
            bucket = _t5_bucket(d * np.clip(step, -DIL_HALF, DIL_HALF))
            pick[pi, var, bucket[ok], lane[ok]] = 1.0
            neg[pi, var, 0, ~ok] = NEG_INF
    vecs = jnp.einsum('bh,pvbn->hpvn', t5_bias.astype(F32), pick,
                      precision=lax.Precision.HIGHEST) * LOG2E + neg[:, :, 0][None]
    return pl.pallas_call(
        _dil_bias_kernel,
        grid=(N_HEADS,),
        in_specs=[pl.BlockSpec((None, n_pat * N_VARIANTS, DIL_VEC), lambda h: (h, 0, 0))],
        out_specs=pl.BlockSpec((None, n_pat, N_VARIANTS, DIL_QB, DIL_KW), lambda h: (h, 0, 0, 0, 0)),
        out_shape=jax.ShapeDtypeStruct((N_HEADS, n_pat, N_VARIANTS, DIL_QB, DIL_KW), F32),
        compiler_params=pltpu.CompilerParams(dimension_semantics=("arbitrary",)),
        name="dil_bias",
    )(vecs.reshape(N_HEADS, n_pat * N_VARIANTS, DIL_VEC))


DIL_VEC = 512


def _dil_bias_kernel(vec_ref, o_ref):
    for pi in range(len(DIL_PATTERNS)):
        for var in range(N_VARIANTS):
            i = pi * N_VARIANTS + var
            x = jnp.broadcast_to(vec_ref[i:i + 1, :], (DIL_QB, DIL_VEC))
            o_ref[pi, var] = pltpu.roll(x, 0, axis=1, stride=1, stride_axis=0)[:, :DIL_KW]


DIL_GROUP = 16
DIL_GROUP_FIRST = 16


def _dil_kernel(q_ref, k_ref, v_ref, z_ref, bias_ref, o_ref, qf, kf, vf, qg, kg, vg, qd,
                q4, k4, v4, m_s, l_s, acc_s):
    seq = q_ref.shape[0]
    conv_rows = 256
    lane = lax.broadcasted_iota(jnp.int32, (DIL_QB, LANES), 1)
    first_head = lane < HEAD_DIM
    nt = (((1,), (1,)), ((), ()))
    ones = jnp.ones((DIL_KW, LANES), BF16)

    def conv(i, carry):
        r0 = pl.multiple_of(i * conv_rows, conv_rows)
        qf[pl.ds(r0, conv_rows), :] = q_ref[pl.ds(r0, conv_rows), :].astype(F32) * Q_SCALE_LOG2
        kf[pl.ds(r0, conv_rows), :] = k_ref[pl.ds(r0, conv_rows), :].astype(F32)
        vf[pl.ds(r0, conv_rows), :] = v_ref[pl.ds(r0, conv_rows), :].astype(F32)
        return carry

    lax.fori_loop(0, seq // conv_rows, conv, 0)

    n4 = seq // 4
    n16 = seq // 16

    def to_order4(i, carry):
        dst = pl.multiple_of(i * conv_rows, conv_rows)
        src = pl.ds(dst // n4 + 4 * (dst % n4), conv_rows, stride=4)
        for a, b, c in ((qf, qg, q4), (kf, kg, k4), (vf, vg, v4)):
            part = a[src, :]
            b[pl.ds(dst, conv_rows), :] = part
            c[pl.ds(dst, conv_rows), :] = part.astype(BF16)
        return carry

    lax.fori_loop(0, seq // conv_rows, to_order4, 0, unroll=2)

    def window(src, res, start, size, n_sub):
        if n_sub == n16:
            rows = pl.ds((res % 4) * n4 + res // 4 + 4 * start, size, stride=4)
            return src[rows, :].astype(BF16)
        return src[pl.ds(pl.multiple_of(res * n_sub + start, DIL_HALF), size), :]

    order4_state = (qf, kf, vf)
    token_state = (m_s, l_s, acc_s)
    plan = ((2, 16, (qg, kg, vg), None, order4_state),
            (1, 4, (q4, k4, v4), order4_state, token_state),
            (0, 1, (qd, k_ref, v_ref), token_state, token_state))

    for step, (pi, d, (q_src, k_src, v_src), st_in, st_out) in enumerate(plan):
        n_sub = seq // d
        nblk = n_sub // DIL_QB

        if step == 2:
            def scaled_q(i, carry):
                r0 = pl.multiple_of(i * conv_rows, conv_rows)
                qd[pl.ds(r0, conv_rows), :] = (
                    q_ref[pl.ds(r0, conv_rows), :].astype(F32) * Q_SCALE_LOG2).astype(BF16)
                return carry

            lax.fori_loop(0, seq // conv_rows, scaled_q, 0, unroll=2)

        def rows_out(res, i0, step=step):
            if step == 0:
                return pl.ds((res % 4) * n4 + res // 4 + 4 * i0, DIL_QB, stride=4)
            if step == 1:
                return pl.ds(res + 4 * i0, DIL_QB, stride=4)
            return pl.ds(pl.multiple_of(i0, DIL_HALF), DIL_QB)

        n_group = DIL_GROUP_FIRST if step == 0 else DIL_GROUP

        def group(gi, carry, pi=pi, n_sub=n_sub, nblk=nblk, q_src=q_src, k_src=k_src, v_src=v_src,
                  st_in=st_in, st_out=st_out, rows_out=rows_out, n_group=n_group):
            units = []
            for i in range(n_group):
                u = gi * n_group + i
                res = u // nblk
                i0 = (u % nblk) * DIL_QB
                ks = jnp.clip(i0 - DIL_HALF, 0, n_sub - DIL_KW)
                units.append((res, i0, ks, (i0 - ks) // DIL_HALF))
            s_tiles, m_tiles = [], []
            for res, i0, ks, var in units:
                qb = window(q_src, res, i0, DIL_QB, n_sub)
                kw = window(k_src, res, ks, DIL_KW, n_sub)
                for hd in range(2):
                    keep = first_head if hd == 0 else jnp.logical_not(first_head)
                    qm = jnp.where(keep, qb, jnp.zeros_like(qb))
                    s = lax.dot_general(qm, kw, nt, preferred_element_type=F32) + bias_ref[hd, pi, var]
                    s = s.astype(BF16)
                    s_tiles.append(s)
                    m_tiles.append(jnp.max(s, axis=-1, keepdims=True))
            p_tiles = [jnp.exp2(s - m) for s, m in zip(s_tiles, m_tiles)]
            for i, (res, i0, ks, var) in enumerate(units):
                vw = window(v_src, res, ks, DIL_KW, n_sub)
                vext = jnp.concatenate([vw, ones], axis=-1)
                r0 = jnp.dot(p_tiles[2 * i], vext, preferred_element_type=F32)
                r1 = jnp.dot(p_tiles[2 * i + 1], vext, preferred_element_type=F32)
                m_cur = jnp.where(first_head, m_tiles[2 * i].astype(F32), m_tiles[2 * i + 1].astype(F32))
                l_cur = jnp.where(first_head, r0[:, LANES:], r1[:, LANES:])
                acc_cur = jnp.where(first_head, r0[:, :LANES], r1[:, :LANES])
                dst = rows_out(res, i0)
                if st_in is None:
                    m_new, l_new, acc_new = m_cur, l_cur, acc_cur
                else:
                    src = pl.ds(pl.multiple_of(res * n_sub + i0, DIL_HALF), DIL_QB)
                    m_old = st_in[0][src, :]
                    m_new = jnp.maximum(m_old, m_cur)
                    a_old = jnp.exp2(m_old - m_new)
                    a_cur = jnp.exp2(m_cur - m_new)
                    l_new = a_old * st_in[1][src, :] + a_cur * l_cur
                    acc_new = a_old * st_in[2][src, :] + a_cur * acc_cur
                st_out[0][dst, :] = m_new
                st_out[1][dst, :] = l_new
                st_out[2][dst, :] = acc_new
            return carry

        lax.fori_loop(0, seq // DIL_QB // n_group, group, 0)

    def fin(i, carry):
        r0 = pl.multiple_of(i * conv_rows, conv_rows)
        z = z_ref[pl.ds(r0, conv_rows), :].astype(F32)
        y = acc_s[pl.ds(r0, conv_rows), :] * (1.0 / l_s[pl.ds(r0, conv_rows), :])
        o_ref[pl.ds(r0, conv_rows), :] = (y * _silu(z)).astype(BF16)
        return carry

    lax.fori_loop(0, seq // conv_rows, fin, 0, unroll=2)


def _dil(proj, bias, batch, seq):
    def col(c0):
        return pl.BlockSpec((seq, LANES), lambda b, p, c0=c0: (b, c0 + p))

    scratch = ([pltpu.VMEM((seq, LANES), F32) for _ in range(6)]
               + [pltpu.VMEM((seq, LANES), BF16) for _ in range(4)]
               + [pltpu.VMEM((seq, LANES), F32) for _ in range(3)])
    return pl.pallas_call(
        _dil_kernel,
        grid=(batch, N_HEAD_PAIRS),
        in_specs=[
            col(COL_QC), col(COL_KC), col(COL_VC), col(COL_ZC),
            pl.BlockSpec((2, len(DIL_PATTERNS), N_VARIANTS, DIL_QB, DIL_KW),
                         lambda b, p: (p, 0, 0, 0, 0)),
        ],
        out_specs=pl.BlockSpec((seq, LANES), lambda b, p: (b, p)),
        out_shape=jax.ShapeDtypeStruct((batch * seq, W_BRANCH), BF16),
        scratch_shapes=scratch,
        compiler_params=pltpu.CompilerParams(
            dimension_semantics=("arbitrary", "arbitrary"), vmem_limit_bytes=VMEM_LIMIT),
        name="dil_attn",
    )(proj, proj, proj, proj, bias)


N_POW = CHUNK + 1
PREP_GROUPS = 8


def _ssm_prep_kernel(lre_ref, lim_ref, ldt_ref, btre_ref, btim_ref, cre_ref, cim_ref,
                     toep_ref, win_ref, wout_ref, decay_ref):
    n = CHUNK * SSM_GROUP
    lane = lax.broadcasted_iota(jnp.int32, (SSM_GROUP, LANES), 1)
    lo = lane < SSM_STATE
    lane_n = lax.broadcasted_iota(jnp.int32, (SSM_GROUP, n), 1)
    nt = (((1,), (1,)), ((), ()))
    pair = lambda x, y: jnp.where(lo, x, y)

    for j in range(PREP_GROUPS):
        xs, ys, bbs, pw_last = {}, {}, {}, {}
        for dr in range(2):
            lam_re = lre_ref[0, dr, j]
            lam_im = lim_ref[0, dr, j]
            dt = jnp.exp(ldt_ref[0, dr, j])
            mag = jnp.exp(lam_re * dt)
            lb_re = mag * jnp.cos(lam_im * dt)
            lb_im = mag * jnp.sin(lam_im * dt)
            n_re = lb_re - 1.0
            den = lam_re * lam_re + lam_im * lam_im
            q_re = (n_re * lam_re + lb_im * lam_im) / den
            q_im = (lb_im * lam_re - n_re * lam_im) / den
            bt_re = btre_ref[0, dr, j]
            bt_im = btim_ref[0, dr, j]
            bb_re = q_re * bt_re - q_im * bt_im
            bb_im = q_re * bt_im + q_im * bt_re
            bbs[dr] = (bb_re, bb_im)
            c_re = cre_ref[0, dr, j]
            c_im = cim_ref[0, dr, j]
            pw_re = jnp.ones_like(lb_re)
            pw_im = jnp.zeros_like(lb_re)
            for k in range(N_POW):
                xs[dr, k] = (pw_re * bb_re - pw_im * bb_im, pw_re * bb_im + pw_im * bb_re)
                ys[dr, k] = (pw_re * c_re - pw_im * c_im, pw_re * c_im + pw_im * c_re)
                pw_last[dr] = (pw_re, pw_im)
                pw_re, pw_im = pw_re * lb_re - pw_im * lb_im, pw_re * lb_im + pw_im * lb_re

        step = [GROUPS_PER_SLAB * (a // GROUPS_PER_SLAB) + (a % GROUPS_PER_SLAB - j) % GROUPS_PER_SLAB
                for a in range(CHUNK)]
        rows = lambda a: slice(a * SSM_GROUP, (a + 1) * SSM_GROUP)

        decay_ref[0, 0, j:j + 1, :] = pair(pw_last[0][0], pw_last[1][0])[0:1]
        decay_ref[0, 1, j:j + 1, :] = pair(pw_last[0][1], pw_last[1][1])[0:1]

        for a, s in enumerate(step):
            win_ref[0, j, rows(a), :LANES] = pair(xs[0, CHUNK - 1 - s][0], xs[1, s][0]).astype(BF16)
            win_ref[0, j, rows(a), LANES:] = pair(xs[0, CHUNK - 1 - s][1], xs[1, s][1]).astype(BF16)

        z_re = jnp.concatenate([pair(ys[0, t + 1][0], ys[1, CHUNK - t][0]) for t in step], axis=0)
        z_nim = jnp.concatenate([pair(-ys[0, t + 1][1], -ys[1, CHUNK - t][1]) for t in step], axis=0)
        wout_ref[0, j, :LANES, :] = z_re.T.astype(BF16)
        wout_ref[0, j, LANES:, :] = z_nim.T.astype(BF16)

        kf = lax.dot_general(
            pair(bbs[0][0], -bbs[0][1]),
            jnp.concatenate([pair(*ys[0, m]) for m in range(CHUNK)], axis=0),
            nt, precision=lax.Precision.HIGHEST, preferred_element_type=F32)
        kb = lax.dot_general(
            pair(bbs[1][0], -bbs[1][1]),
            jnp.concatenate([pair(*ys[1, CHUNK - 1 - i]) for i in range(CHUNK)], axis=0),
            nt, precision=lax.Precision.HIGHEST, preferred_element_type=F32)
        for a, s in enumerate(step):
            fwd = pltpu.roll(kf, SSM_GROUP * s, axis=1) if s else kf
            shift_b = (n - SSM_GROUP * (CHUNK - 1 - s)) % n
            bwd = pltpu.roll(kb, shift_b, axis=1) if shift_b else kb
            t_nat = (jnp.where(lane_n >= SSM_GROUP * s, fwd, 0.0)
                     + jnp.where(lane_n < SSM_GROUP * (s + 1), bwd, 0.0))
            for h in range(2):
                half = t_nat[:, h * LANES:(h + 1) * LANES]
                if j:
                    half = pltpu.roll(half, SSM_GROUP * j, axis=1)
                toep_ref[0, j, rows(a), h * LANES:(h + 1) * LANES] = half.astype(BF16)


def _ssm_prep(lam_re, lam_im, log_dt, b_re, b_im, c_re, c_im):
    depth = lam_re.shape[0]
    G, P, C = SSM_GROUPS, SSM_STATE, SSM_GROUP
    n = CHUNK * C
    gb = PREP_GROUPS
    twice = lambda a: jnp.concatenate([a, a], axis=-1)
    rep = lambda a: jnp.broadcast_to(twice(a.astype(F32))[:, :, :, None, :], (depth, 2, G, C, LANES))
    ldt = jnp.broadcast_to(log_dt.astype(F32)[:, :, :, None, None], (depth, 2, G, C, LANES))
    bt = lambda a: twice(a.astype(F32).transpose(0, 1, 2, 4, 3))
    spec_in = pl.BlockSpec((1, 2, gb, C, LANES), lambda l, i: (l, 0, i, 0, 0))
    spec_w = pl.BlockSpec((1, gb, n, n), lambda l, i: (l, i, 0, 0))
    shp_w = jax.ShapeDtypeStruct((depth, G, n, n), BF16)
    return pl.pallas_call(
        _ssm_prep_kernel,
        grid=(depth, G // gb),
        in_specs=[spec_in] * 7,
        out_specs=[spec_w, spec_w, spec_w, pl.BlockSpec((1, 2, gb, LANES), lambda l, i: (l, 0, i, 0))],
        out_shape=[shp_w, shp_w, shp_w, jax.ShapeDtypeStruct((depth, 2, G, LANES), F32)],
        compiler_params=pltpu.CompilerParams(
            dimension_semantics=("arbitrary", "arbitrary"), vmem_limit_bytes=VMEM_LIMIT),
        name="ssm_prep",
    )(rep(lam_re), rep(lam_im), ldt, bt(b_re), bt(b_im), twice(c_re.astype(F32)), twice(c_im.astype(F32)))


GROUPS_PER_SLAB = LANES // SSM_GROUP
N_SLABS = W_BRANCH // LANES
RELAYOUT_ROWS = 32


def _ssm_kernel(xa_ref, toep_ref, win_ref, wout_ref, decay_ref, d_ref, o_ref, xf, u_s, st, hs, yq):
    seq = xa_ref.shape[0]
    nchunk = seq // CHUNK
    nrb = nchunk // RELAYOUT_ROWS
    gps = GROUPS_PER_SLAB
    lane_rb = lax.broadcasted_iota(jnp.int32, (RELAYOUT_ROWS, LANES), 1)
    seg_masks = [(lane_rb >= SSM_GROUP * sg) & (lane_rb < SSM_GROUP * (sg + 1)) for sg in range(gps)]
    conv_rows = 512

    for q in range(N_SLABS):
        c_lo = q * LANES

        def conv(i, carry, c_lo=c_lo):
            r0 = pl.multiple_of(i * conv_rows, conv_rows)
            xf[pl.ds(r0, conv_rows), :] = xa_ref[pl.ds(r0, conv_rows), c_lo:c_lo + LANES].astype(F32)
            return carry

        lax.fori_loop(0, seq // conv_rows, conv, 0)

        for hh in range(2):
            def fwd_relayout(rb, carry, hh=hh):
                c0 = pl.multiple_of(rb * RELAYOUT_ROWS, RELAYOUT_ROWS)
                rolled = []
                for r in range(gps):
                    xs = xf[pl.ds(CHUNK * c0 + gps * hh + r, RELAYOUT_ROWS, stride=CHUNK), :]
                    rolled.append((xs if r == 0 else pltpu.roll(xs, SSM_GROUP * r, axis=1)).astype(BF16))
                for j in range(gps):
                    out = rolled[(0 - j) % gps]
                    for sg in range(1, gps):
                        out = jnp.where(seg_masks[sg], rolled[(sg - j) % gps], out)
                    u_s[j, pl.ds(c0, RELAYOUT_ROWS), hh * LANES:(hh + 1) * LANES] = out
                return carry

            lax.fori_loop(0, nrb, fwd_relayout, 0, unroll=True)

        for j in range(gps):
            s_in = jnp.dot(u_s[j], win_ref[gps * q + j], preferred_element_type=F32)
            st[0, pl.ds(j, nchunk, stride=gps), :] = s_in[:, :LANES]
            st[1, pl.ds(j, nchunk, stride=gps), :] = s_in[:, LANES:]

        g_lo = gps * q
        a_re = decay_ref[0, g_lo:g_lo + gps, :]
        a_im = decay_ref[1, g_lo:g_lo + gps, :]
        lo, hi = slice(0, SSM_STATE), slice(SSM_STATE, LANES)

        def scan(i, carry):
            hrf, hif, hrb, hib = carry
            rf = pl.ds(pl.multiple_of(i * gps, gps), gps)
            rb_ = pl.ds(pl.multiple_of((nchunk - 1 - i) * gps, gps), gps)
            hs[0, rf, lo] = hrf[:, lo]
            hs[1, rf, lo] = hif[:, lo]
            hs[0, rb_, hi] = hrb[:, hi]
            hs[1, rb_, hi] = hib[:, hi]
            return (a_re * hrf - a_im * hif + st[0, rf, :], a_re * hif + a_im * hrf + st[1, rf, :],
                    a_re * hrb - a_im * hib + st[0, rb_, :], a_re * hib + a_im * hrb + st[1, rb_, :])

        zero = jnp.zeros((gps, LANES), F32)
        lax.fori_loop(0, nchunk, scan, (zero, zero, zero, zero), unroll=8)

        for j in range(gps):
            g = gps * q + j
            h_in = jnp.concatenate([hs[0, pl.ds(j, nchunk, stride=gps), :].astype(BF16),
                                    hs[1, pl.ds(j, nchunk, stride=gps), :].astype(BF16)], axis=-1)
            yq[j] = (jnp.dot(u_s[j], toep_ref[g], preferred_element_type=F32)
                     + jnp.dot(h_in, wout_ref[g], preferred_element_type=F32))

        d_row = d_ref[:, c_lo:c_lo + LANES]
        for hh in range(2):
            def bwd_relayout(rb, carry, hh=hh, d_row=d_row, q=q):
                c0 = pl.multiple_of(rb * RELAYOUT_ROWS, RELAYOUT_ROWS)
                ys = [yq[j, pl.ds(c0, RELAYOUT_ROWS), hh * LANES:(hh + 1) * LANES] for j in range(gps)]
                for r in range(gps):
                    merged = ys[(0 - r) % gps]
                    for sg in range(1, gps):
                        merged = jnp.where(seg_masks[sg], ys[(sg - r) % gps], merged)
                    if r:
                        merged = pltpu.roll(merged, LANES - SSM_GROUP * r, axis=1)
                    tok = pl.ds(CHUNK * c0 + gps * hh + r, RELAYOUT_ROWS, stride=CHUNK)
                    o_ref[q, tok, :] = merged + d_row * xf[tok, :]
                return carry

            lax.fori_loop(0, nrb, bwd_relayout, 0, unroll=True)


def _ssm(proj, toep, w_in, w_out, decay, d_skip, layer, batch, seq):
    nchunk = seq // CHUNK
    n = CHUNK * SSM_GROUP
    wspec = pl.BlockSpec((None, SSM_GROUPS, n, n), lambda b: (layer, 0, 0, 0), pipeline_mode=pl.Buffered(1))
    return pl.pallas_call(
        _ssm_kernel,
        grid=(batch,),
        in_specs=[
            pl.BlockSpec((seq, W_BRANCH), lambda b: (b, COL_XA // N_SLABS)),
            wspec, wspec, wspec,
            pl.BlockSpec((None, 2, SSM_GROUPS, LANES), lambda b: (layer, 0, 0, 0)),
            pl.BlockSpec((None, 1, W_BRANCH), lambda b: (layer, 0, 0)),
        ],
        out_specs=pl.BlockSpec((N_SLABS, seq, LANES), lambda b: (0, b, 0)),
        out_shape=jax.ShapeDtypeStruct((N_SLABS, batch * seq, LANES), F32),
        scratch_shapes=[
            pltpu.VMEM((seq, LANES), F32),
            pltpu.VMEM((GROUPS_PER_SLAB, nchunk, n), BF16),
            pltpu.VMEM((2, nchunk * GROUPS_PER_SLAB, LANES), F32),
            pltpu.VMEM((2, nchunk * GROUPS_PER_SLAB, LANES), F32),
            pltpu.VMEM((GROUPS_PER_SLAB, nchunk, n), F32),
        ],
        compiler_params=pltpu.CompilerParams(
            dimension_semantics=("arbitrary",), vmem_limit_bytes=VMEM_LIMIT),
        name="ssm_mix",
    )(proj, toep, w_in, w_out, decay, d_skip)


OUT_TM = 1024


def _out_proj_kernel(ya_ref, za_ref, yb_ref, yc_ref, x_ref, w_ref, gw_ref, gb_ref, fg_ref, o_ref,
                     w_bf, gw_bf, *, final):
    @pl.when(pl.program_id(0) == 0)
    def _():
        for n in range(MIX_WIDTH // W_BRANCH):
            w_bf[n * W_BRANCH:(n + 1) * W_BRANCH, :] = w_ref[n * W_BRANCH:(n + 1) * W_BRANCH, :].astype(BF16)
        gw_bf[...] = gw_ref[...].astype(BF16)

    y = jnp.concatenate([ya_ref[i] for i in range(N_SLABS)], axis=-1)
    g = _gelu_tanh(y)
    gate = jnp.dot(g.astype(BF16), gw_bf[...], preferred_element_type=F32) + gb_ref[...]
    ya = g * _sigmoid(gate) * _silu(za_ref[...].astype(F32))
    delta = (jnp.dot(ya.astype(BF16), w_bf[:W_BRANCH, :], preferred_element_type=F32)
             + jnp.dot(yb_ref[...], w_bf[W_BRANCH:2 * W_BRANCH, :], preferred_element_type=F32)
             + jnp.dot(yc_ref[...], w_bf[2 * W_BRANCH:, :], preferred_element_type=F32))
    x = x_ref[...] + delta
    if final:
        ms = jnp.mean(x * x, axis=-1, keepdims=True)
        x = x * lax.rsqrt(ms + RMS_EPS) * fg_ref[...]
    o_ref[...] = x


def _out_proj(ya_pre, proj, yb, yc, x2d, w, glu_w, glu_b, final_g, layer, final):
    rows = x2d.shape[0]
    row_blk = lambda width: pl.BlockSpec((OUT_TM, width), lambda i: (i, 0))
    const = lambda shape: pl.BlockSpec((None,) + shape, lambda i: (layer,) + (0,) * len(shape),
                                       pipeline_mode=pl.Buffered(1))
    return pl.pallas_call(
        functools.partial(_out_proj_kernel, final=final),
        grid=(rows // OUT_TM,),
        in_specs=[
            pl.BlockSpec((N_SLABS, OUT_TM, LANES), lambda i: (0, i, 0)),
            pl.BlockSpec((OUT_TM, W_BRANCH), lambda i: (i, COL_ZA // N_SLABS)),
            row_blk(W_BRANCH), row_blk(W_BRANCH), row_blk(D_MODEL),
            const((MIX_WIDTH, D_MODEL)), const((W_BRANCH, W_BRANCH)),
            const((1, W_BRANCH)),
            pl.BlockSpec((None, 1, D_MODEL), lambda i: (0, 0, 0), pipeline_mode=pl.Buffered(1)),
        ],
        out_specs=row_blk(D_MODEL),
        out_shape=jax.ShapeDtypeStruct((rows, D_MODEL), F32),
        scratch_shapes=[pltpu.VMEM((MIX_WIDTH, D_MODEL), BF16), pltpu.VMEM((W_BRANCH, W_BRANCH), BF16)],
        compiler_params=pltpu.CompilerParams(
            dimension_semantics=("arbitrary",), vmem_limit_bytes=VMEM_LIMIT),
        name="out_proj_final" if final else "out_proj",
    )(ya_pre, proj, yb, yc, x2d, w, glu_w, glu_b, final_g)


def kernel(x, norm_g, w_in, w_out, ssm_lam_re, ssm_lam_im, ssm_log_dt, ssm_b_re, ssm_b_im, ssm_c_re,
           ssm_c_im, ssm_d, glu_w, glu_b, na_rpb, t5_bias, final_g):
    batch, seq, _ = x.shape
    depth = w_in.shape[0]
    x2d = x.astype(F32).reshape(batch * seq, D_MODEL)
    toep, s_in, s_out, decay = _ssm_prep(ssm_lam_re, ssm_lam_im, ssm_log_dt, ssm_b_re, ssm_b_im,
                                         ssm_c_re, ssm_c_im)
    dil_bias = _dil_bias_table(t5_bias)
    na_bias = _na_bias_table(na_rpb)
    w_in_f, w_out_f, glu_w_f = w_in.astype(F32), w_out.astype(F32), glu_w.astype(F32)
    row3 = lambda a: a.astype(F32).reshape(a.shape[0], 1, a.shape[1])
    norm_g3, ssm_d3, glu_b3 = row3(norm_g), row3(ssm_d), row3(glu_b)
    final_g3 = final_g.astype(F32).reshape(1, 1, D_MODEL)
    for l in range(depth):
        proj = _in_proj(x2d, norm_g3, w_in_f, l)
        ya_pre = _ssm(proj, toep, s_in, s_out, decay, ssm_d3, l, batch, seq)
        yb = _na(proj, na_bias, l, batch, seq)
        yc = _dil(proj, dil_bias, batch, seq)
        x2d = _out_proj(ya_pre, proj, yb, yc, x2d, w_out_f, glu_w_f, glu_b3, final_g3, l,
                        final=(l == depth - 1))
    return x2d.reshape(batch, seq, D_MODEL).astype(x.dtype)
```

```python
import functools

import numpy as np
import jax
import jax.numpy as jnp
from jax import lax
from jax.experimental import pallas as pl
from jax.experimental.pallas import tpu as pltpu

F32 = jnp.float32
BF16 = jnp.bfloat16

D_MODEL = 1024
HEAD_DIM = 64
W_BRANCH = 512
N_HEADS = W_BRANCH // HEAD_DIM
N_HEAD_PAIRS = N_HEADS // 2
SSM_GROUP = 16
SSM_GROUPS = W_BRANCH // SSM_GROUP
SSM_STATE = 64
GRID_W = 64
NA_ROWS = 8
NA_COLS = 16
DIL_PATTERNS = ((128, 1), (512, 4), (2048, 16))
DIL_HALF = 64
T5_BUCKETS = 32
T5_MAX_DIST = 1024
RMS_EPS = 1e-6
NEG_INF = -1e30
IN_COLS = 10 * W_BRANCH
MIX_WIDTH = 3 * W_BRANCH

COL_XA, COL_ZA, COL_QB, COL_KB, COL_VB, COL_ZB, COL_QC, COL_KC, COL_VC, COL_ZC = (
    4 * i for i in range(10))

LOG2E = float(np.log2(np.e))
Q_SCALE_LOG2 = HEAD_DIM ** -0.5 * LOG2E

LANES = 128
CHUNK = 16
VMEM_LIMIT = 56 * 1024 * 1024


def _sigmoid(z):
    return 0.5 * (1.0 + jnp.tanh(0.5 * z))


def _silu(z):
    return z * _sigmoid(z)


def _gelu_tanh(x):
    return 0.5 * x * (1.0 + jnp.tanh(np.sqrt(2.0 / np.pi).astype(np.float32) * (x + 0.044715 * (x * x * x))))


IN_TM = 512
IN_TN = 512


def _in_proj_kernel(x_ref, g_ref, w_ref, o_ref, w_bf):
    @pl.when(pl.program_id(0) == 0)
    def _():
        for n in range(IN_COLS // IN_TN):
            w_bf[:, n * IN_TN:(n + 1) * IN_TN] = w_ref[:, n * IN_TN:(n + 1) * IN_TN].astype(BF16)

    x = x_ref[...]
    ms = jnp.mean(x * x, axis=-1, keepdims=True)
    h = (x * lax.rsqrt(ms + RMS_EPS) * g_ref[...]).astype(BF16)
    for n in range(IN_COLS // IN_TN):
        o_ref[:, n * IN_TN:(n + 1) * IN_TN] = jnp.dot(
            h, w_bf[:, n * IN_TN:(n + 1) * IN_TN], preferred_element_type=F32).astype(BF16)


def _in_proj(x2d, g, w, layer):
    rows = x2d.shape[0]
    return pl.pallas_call(
        _in_proj_kernel,
        grid=(rows // IN_TM,),
        in_specs=[
            pl.BlockSpec((IN_TM, D_MODEL), lambda i: (i, 0)),
            pl.BlockSpec((None, 1, D_MODEL), lambda i: (layer, 0, 0)),
            pl.BlockSpec((None, D_MODEL, IN_COLS), lambda i: (layer, 0, 0), pipeline_mode=pl.Buffered(1)),
        ],
        out_specs=pl.BlockSpec((IN_TM, IN_COLS), lambda i: (i, 0)),
        out_shape=jax.ShapeDtypeStruct((rows, IN_COLS), BF16),
        scratch_shapes=[pltpu.VMEM((D_MODEL, IN_COLS), BF16)],
        compiler_params=pltpu.CompilerParams(
            dimension_semantics=("arbitrary",), vmem_limit_bytes=VMEM_LIMIT),
        name="in_proj",
    )(x2d, g, w)


def _na_bias_kernel(rpb_ref, o_ref):
    lane = lax.broadcasted_iota(jnp.int32, (GRID_W, LANES), 1)
    j = lax.broadcasted_iota(jnp.int32, (GRID_W, LANES), 0)
    c = lane & (GRID_W - 1)
    col_start = jnp.clip(j - NA_COLS // 2, 0, GRID_W - NA_COLS)
    valid = (c >= col_start) & (c < col_start + NA_COLS)
    first = lane < GRID_W
    even, odd = [], []
    for a in range(2 * NA_ROWS - 1):
        vec = pltpu.roll(rpb_ref[a:a + 1, :], LANES - (NA_COLS - 1), axis=1)
        x = jnp.broadcast_to(vec, (GRID_W, LANES))
        even.append(pltpu.roll(x, 0, axis=1, stride=1, stride_axis=0))
        odd.append(pltpu.roll(x, GRID_W, axis=1, stride=1, stride_axis=0))
    for v in range(NA_ROWS):
        for i in range(NA_ROWS // 2):
            a0 = 2 * i - v + (NA_ROWS - 1)
            tile = jnp.where(first, even[a0], odd[a0 + 1])
            o_ref[v, :, i * LANES:(i + 1) * LANES] = jnp.where(valid, tile * LOG2E, NEG_INF)


def _na_bias_table(rpb):
    depth = rpb.shape[0]
    padded = jnp.pad(rpb.astype(F32), ((0, 0), (0, 0), (0, 1), (0, LANES - (2 * NA_COLS - 1))))
    nkeys = NA_ROWS * GRID_W
    return pl.pallas_call(
        _na_bias_kernel,
        grid=(depth, N_HEADS),
        in_specs=[pl.BlockSpec((None, None, 2 * NA_ROWS, LANES), lambda l, h: (l, h, 0, 0))],
        out_specs=pl.BlockSpec((None, None, NA_ROWS, GRID_W, nkeys), lambda l, h: (l, h, 0, 0, 0)),
        out_shape=jax.ShapeDtypeStruct((depth, N_HEADS, NA_ROWS, GRID_W, nkeys), F32),
        compiler_params=pltpu.CompilerParams(dimension_semantics=("arbitrary", "arbitrary")),
        name="na_bias",
    )(padded)


def _na_kernel(q_ref, k_ref, v_ref, z_ref, bias_ref, o_ref):
    seq = q_ref.shape[0]
    rows = seq // GRID_W
    half_rows = NA_ROWS // 2
    tile_keys = half_rows * GRID_W
    lane = lax.broadcasted_iota(jnp.int32, (GRID_W, LANES), 1)
    first_head = lane < HEAD_DIM
    masks = (first_head, jnp.logical_not(first_head))
    nt = (((1,), (1,)), ((), ()))
    ones = jnp.ones((tile_keys, LANES), BF16)
    row_start = lambda r: min(max(r - NA_ROWS // 2, 0), rows - NA_ROWS)

    for g0 in range(0, rows, NA_GROUP):
        group = range(g0, g0 + NA_GROUP)
        users = {}
        for r in group:
            for half in range(2):
                users.setdefault(row_start(r) + half_rows * half, []).append((r, half))
        qm = {}
        for r in group:
            qb = (q_ref[r * GRID_W:(r + 1) * GRID_W, :].astype(F32) * Q_SCALE_LOG2).astype(BF16)
            for hd in range(2):
                qm[r, hd] = jnp.where(masks[hd], qb, jnp.zeros_like(qb))
        s_half = {}
        for k, us in sorted(users.items()):
            lhs = jnp.concatenate([qm[r, hd] for r, _ in us for hd in range(2)], axis=0)
            part = lax.dot_general(lhs, k_ref[k * GRID_W:k * GRID_W + tile_keys, :], nt,
                                   preferred_element_type=F32)
            for i, (r, half) in enumerate(us):
                for hd in range(2):
                    blk = part[(2 * i + hd) * GRID_W:(2 * i + hd + 1) * GRID_W, :]
                    bias = bias_ref[hd, r - row_start(r), :, half * tile_keys:(half + 1) * tile_keys]
                    s_half[r, hd, half] = (blk + bias).astype(BF16)
        p_half = {}
        for r in group:
            for hd in range(2):
                s0, s1 = s_half[r, hd, 0], s_half[r, hd, 1]
                m = jnp.max(jnp.maximum(s0, s1), axis=-1, keepdims=True)
                p_half[r, hd, 0], p_half[r, hd, 1] = jnp.exp2(s0 - m), jnp.exp2(s1 - m)
        acc = {}
        for k, us in sorted(users.items()):
            lhs = jnp.concatenate([p_half[r, hd, half] for r, half in us for hd in range(2)], axis=0)
            vext = jnp.concatenate([v_ref[k * GRID_W:k * GRID_W + tile_keys, :], ones], axis=-1)
            part = jnp.dot(lhs, vext, preferred_element_type=F32)
            for i, (r, half) in enumerate(us):
                for hd in range(2):
                    blk = part[(2 * i + hd) * GRID_W:(2 * i + hd + 1) * GRID_W, :]
                    acc[r, hd] = blk if (r, hd) not in acc else acc[r, hd] + blk
        for r in group:
            y = jnp.where(first_head, acc[r, 0][:, :LANES], acc[r, 1][:, :LANES])
            d = jnp.where(first_head, acc[r, 0][:, LANES:], acc[r, 1][:, LANES:])
            z = z_ref[r * GRID_W:(r + 1) * GRID_W, :].astype(F32)
            o_ref[r * GRID_W:(r + 1) * GRID_W, :] = (y * (1.0 / d) * _silu(z)).astype(BF16)


NA_GROUP = 16


def _na(proj, bias, layer, batch, seq):
    def col(c0):
        return pl.BlockSpec((seq, LANES), lambda b, p, c0=c0: (b, c0 + p))

    return pl.pallas_call(
        _na_kernel,
        grid=(batch, N_HEAD_PAIRS),
        in_specs=[
            col(COL_QB), col(COL_KB), col(COL_VB), col(COL_ZB),
            pl.BlockSpec((None, 2, NA_ROWS, GRID_W, NA_ROWS * GRID_W), lambda b, p: (layer, p, 0, 0, 0)),
        ],
        out_specs=pl.BlockSpec((seq, LANES), lambda b, p: (b, p)),
        out_shape=jax.ShapeDtypeStruct((batch * seq, W_BRANCH), BF16),
        compiler_params=pltpu.CompilerParams(
            dimension_semantics=("arbitrary", "arbitrary"), vmem_limit_bytes=VMEM_LIMIT),
        name="na_attn",
    )(proj, proj, proj, proj, bias)


DIL_QB = 128
DIL_KW = 256
N_VARIANTS = 3


def _t5_bucket(rel):
    nb = T5_BUCKETS // 2
    max_exact = nb // 2
    n = np.abs(rel)
    large = max_exact + (np.log(np.maximum(n, 1) / max_exact) / np.log(T5_MAX_DIST / max_exact)
                         * (nb - max_exact)).astype(np.int32)
    large = np.minimum(large, nb - 1)
    return (np.where(rel > 0, nb, 0) + np.where(n < max_exact, n, large)).astype(np.int32)


def _dil_bias_table(t5_bias):
    n_pat = len(DIL_PATTERNS)
    lane = np.arange(DIL_VEC)
    n = np.where(lane < DIL_KW, lane, lane - DIL_VEC)
    in_range = (lane < DIL_KW) | (lane > DIL_VEC - DIL_QB)
    pick = np.zeros((n_pat, N_VARIANTS, T5_BUCKETS, DIL_VEC), np.float32)
    neg = np.zeros((n_pat, N_VARIANTS, 1, DIL_VEC), np.float32)
    for pi, (_, d) in enumerate(DIL_PATTERNS):
        for var in range(N_VARIANTS):
            step = n - DIL_HALF * var
            ok = in_range & (np.abs(step) <= DIL_HALF)
            bucket = _t5_bucket(d * np.clip(step, -DIL_HALF, DIL_HALF))
            pick[pi, var, bucket[ok], lane[ok]] = 1.0
            neg[pi, var, 0, ~ok] = NEG_INF
    vecs = jnp.einsum('bh,pvbn->hpvn', t5_bias.astype(F32), pick,
                      precision=lax.Precision.HIGHEST) * LOG2E + neg[:, :, 0][None]
    return pl.pallas_call(
        _dil_bias_kernel,
        grid=(N_HEADS,),
        in_specs=[pl.BlockSpec((None, n_pat * N_VARIANTS, DIL_VEC), lambda h: (h, 0, 0))],
        out_specs=pl.BlockSpec((None, n_pat, N_VARIANTS, DIL_QB, DIL_KW), lambda h: (h, 0, 0, 0, 0)),
        out_shape=jax.ShapeDtypeStruct((N_HEADS, n_pat, N_VARIANTS, DIL_QB, DIL_KW), F32),
        compiler_params=pltpu.CompilerParams(dimension_semantics=("arbitrary",)),
        name="dil_bias",
    )(vecs.reshape(N_HEADS, n_pat * N_VARIANTS, DIL_VEC))


DIL_VEC = 512


def _dil_bias_kernel(vec_ref, o_ref):
    for pi in range(len(DIL_PATTERNS)):
        for var in range(N_VARIANTS):
            i = pi * N_VARIANTS + var
            x = jnp.broadcast_to(vec_ref[i:i + 1, :], (DIL_QB, DIL_VEC))
            o_ref[pi, var] = pltpu.roll(x, 0, axis=1, stride=1, stride_axis=0)[:, :DIL_KW]


DIL_GROUP = 16
DIL_GROUP_FIRST = 16


def _dil_kernel(q_ref, k_ref, v_ref, z_ref, bias_ref, o_ref, qf, kf, vf, qg, kg, vg, qd,
                q4, k4, v4, m_s, l_s, acc_s):
    seq = q_ref.shape[0]
    conv_rows = 256
    lane = lax.broadcasted_iota(jnp.int32, (DIL_QB, LANES), 1)
    first_head = lane < HEAD_DIM
    nt = (((1,), (1,)), ((), ()))
    ones = jnp.ones((DIL_KW, LANES), BF16)

    def conv(i, carry):
        r0 = pl.multiple_of(i * conv_rows, conv_rows)
        qf[pl.ds(r0, conv_rows), :] = q_ref[pl.ds(r0, conv_rows), :].astype(F32) * Q_SCALE_LOG2
        kf[pl.ds(r0, conv_rows), :] = k_ref[pl.ds(r0, conv_rows), :].astype(F32)
        vf[pl.ds(r0, conv_rows), :] = v_ref[pl.ds(r0, conv_rows), :].astype(F32)
        return carry

    lax.fori_loop(0, seq // conv_rows, conv, 0)

    n4 = seq // 4
    n16 = seq // 16

    def to_order4(i, carry):
        dst = pl.multiple_of(i * conv_rows, conv_rows)
        src = pl.ds(dst // n4 + 4 * (dst % n4), conv_rows, stride=4)
        for a, b, c in ((qf, qg, q4), (kf, kg, k4), (vf, vg, v4)):
            part = a[src, :]
            b[pl.ds(dst, conv_rows), :] = part
            c[pl.ds(dst, conv_rows), :] = part.astype(BF16)
        return carry

    lax.fori_loop(0, seq // conv_rows, to_order4, 0, unroll=True)

    def window(src, res, start, size, n_sub):
        if n_sub == n16:
            rows = pl.ds((res % 4) * n4 + res // 4 + 4 * start, size, stride=4)
            return src[rows, :].astype(BF16)
        return src[pl.ds(pl.multiple_of(res * n_sub + start, DIL_HALF), size), :]

    order4_state = (qf, kf, vf)
    token_state = (m_s, l_s, acc_s)
    plan = ((2, 16, (qg, kg, vg), None, order4_state),
            (1, 4, (q4, k4, v4), order4_state, token_state),
            (0, 1, (qd, k_ref, v_ref), token_state, token_state))

    for step, (pi, d, (q_src, k_src, v_src), st_in, st_out) in enumerate(plan):
        n_sub = seq // d
        nblk = n_sub // DIL_QB

        if step == 2:
            def scaled_q(i, carry):
                r0 = pl.multiple_of(i * conv_rows, conv_rows)
                qd[pl.ds(r0, conv_rows), :] = (
                    q_ref[pl.ds(r0, conv_rows), :].astype(F32) * Q_SCALE_LOG2).astype(BF16)
                return carry

            lax.fori_loop(0, seq // conv_rows, scaled_q, 0, unroll=True)

        def rows_out(res, i0, step=step):
            if step == 0:
                return pl.ds((res % 4) * n4 + res // 4 + 4 * i0, DIL_QB, stride=4)
            if step == 1:
                return pl.ds(res + 4 * i0, DIL_QB, stride=4)
            return pl.ds(pl.multiple_of(i0, DIL_HALF), DIL_QB)

        n_group = DIL_GROUP_FIRST if step == 0 else DIL_GROUP

        def group(gi, carry, pi=pi, n_sub=n_sub, nblk=nblk, q_src=q_src, k_src=k_src, v_src=v_src,
                  st_in=st_in, st_out=st_out, rows_out=rows_out, n_group=n_group):
            units = []
            for i in range(n_group):
                u = gi * n_group + i
                res = u // nblk
                i0 = (u % nblk) * DIL_QB
                ks = jnp.clip(i0 - DIL_HALF, 0, n_sub - DIL_KW)
                units.append((res, i0, ks, (i0 - ks) // DIL_HALF))
            s_tiles, m_tiles = [], []
            for res, i0, ks, var in units:
                qb = window(q_src, res, i0, DIL_QB, n_sub)
                kw = window(k_src, res, ks, DIL_KW, n_sub)
                for hd in range(2):
                    keep = first_head if hd == 0 else jnp.logical_not(first_head)
                    qm = jnp.where(keep, qb, jnp.zeros_like(qb))
                    s = lax.dot_general(qm, kw, nt, preferred_element_type=F32) + bias_ref[hd, pi, var]
                    s = s.astype(BF16)
                    s_tiles.append(s)
                    m_tiles.append(jnp.max(s, axis=-1, keepdims=True))
            p_tiles = [jnp.exp2(s - m) for s, m in zip(s_tiles, m_tiles)]
            for i, (res, i0, ks, var) in enumerate(units):
                vw = window(v_src, res, ks, DIL_KW, n_sub)
                vext = jnp.concatenate([vw, ones], axis=-1)
                r0 = jnp.dot(p_tiles[2 * i], vext, preferred_element_type=F32)
                r1 = jnp.dot(p_tiles[2 * i + 1], vext, preferred_element_type=F32)
                m_cur = jnp.where(first_head, m_tiles[2 * i].astype(F32), m_tiles[2 * i + 1].astype(F32))
                l_cur = jnp.where(first_head, r0[:, LANES:], r1[:, LANES:])
                acc_cur = jnp.where(first_head, r0[:, :LANES], r1[:, :LANES])
                dst = rows_out(res, i0)
                if st_in is None:
                    m_new, l_new, acc_new = m_cur, l_cur, acc_cur
                else:
                    src = pl.ds(pl.multiple_of(res * n_sub + i0, DIL_HALF), DIL_QB)
                    m_old = st_in[0][src, :]
                    m_new = jnp.maximum(m_old, m_cur)
                    a_old = jnp.exp2(m_old - m_new)
                    a_cur = jnp.exp2(m_cur - m_new)
                    l_new = a_old * st_in[1][src, :] + a_cur * l_cur
                    acc_new = a_old * st_in[2][src, :] + a_cur * acc_cur
                st_out[0][dst, :] = m_new
                st_out[1][dst, :] = l_new
                st_out[2][dst, :] = acc_new
            return carry

        lax.fori_loop(0, seq // DIL_QB // n_group, group, 0)

    def fin(i, carry):
        r0 = pl.multiple_of(i * conv_rows, conv_rows)
        z = z_ref[pl.ds(r0, conv_rows), :].astype(F32)
        y = acc_s[pl.ds(r0, conv_rows), :] * (1.0 / l_s[pl.ds(r0, conv_rows), :])
        o_ref[pl.ds(r0, conv_rows), :] = (y * _silu(z)).astype(BF16)
        return carry

    lax.fori_loop(0, seq // conv_rows, fin, 0, unroll=4)


def _dil(proj, bias, batch, seq):
    def col(c0):
        return pl.BlockSpec((seq, LANES), lambda b, p, c0=c0: (b, c0 + p))

    scratch = ([pltpu.VMEM((seq, LANES), F32) for _ in range(6)]
               + [pltpu.VMEM((seq, LANES), BF16) for _ in range(4)]
               + [pltpu.VMEM((seq, LANES), F32) for _ in range(3)])
    return pl.pallas_call(
        _dil_kernel,
        grid=(batch, N_HEAD_PAIRS),
        in_specs=[
            col(COL_QC), col(COL_KC), col(COL_VC), col(COL_ZC),
            pl.BlockSpec((2, len(DIL_PATTERNS), N_VARIANTS, DIL_QB, DIL_KW),
                         lambda b, p: (p, 0, 0, 0, 0)),
        ],
        out_specs=pl.BlockSpec((seq, LANES), lambda b, p: (b, p)),
        out_shape=jax.ShapeDtypeStruct((batch * seq, W_BRANCH), BF16),
        scratch_shapes=scratch,
        compiler_params=pltpu.CompilerParams(
            dimension_semantics=("arbitrary", "arbitrary"), vmem_limit_bytes=VMEM_LIMIT),
        name="dil_attn",
    )(proj, proj, proj, proj, bias)


N_POW = CHUNK + 1
PREP_GROUPS = 8


def _ssm_prep_kernel(lre_ref, lim_ref, ldt_ref, btre_ref, btim_ref, cre_ref, cim_ref,
                     toep_ref, win_ref, wout_ref, decay_ref):
    n = CHUNK * SSM_GROUP
    lane = lax.broadcasted_iota(jnp.int32, (SSM_GROUP, LANES), 1)
    lo = lane < SSM_STATE
    lane_n = lax.broadcasted_iota(jnp.int32, (SSM_GROUP, n), 1)
    nt = (((1,), (1,)), ((), ()))
    pair = lambda x, y: jnp.where(lo, x, y)

    for j in range(PREP_GROUPS):
        xs, ys, bbs, pw_last = {}, {}, {}, {}
        for dr in range(2):
            lam_re = lre_ref[0, dr, j]
            lam_im = lim_ref[0, dr, j]
            dt = jnp.exp(ldt_ref[0, dr, j])
            mag = jnp.exp(lam_re * dt)
            lb_re = mag * jnp.cos(lam_im * dt)
            lb_im = mag * jnp.sin(lam_im * dt)
            n_re = lb_re - 1.0
            den = lam_re * lam_re + lam_im * lam_im
            q_re = (n_re * lam_re + lb_im * lam_im) / den
            q_im = (lb_im * lam_re - n_re * lam_im) / den
            bt_re = btre_ref[0, dr, j]
            bt_im = btim_ref[0, dr, j]
            bb_re = q_re * bt_re - q_im * bt_im
            bb_im = q_re * bt_im + q_im * bt_re
            bbs[dr] = (bb_re, bb_im)
            c_re = cre_ref[0, dr, j]
            c_im = cim_ref[0, dr, j]
            pw_re = jnp.ones_like(lb_re)
            pw_im = jnp.zeros_like(lb_re)
            for k in range(N_POW):
                xs[dr, k] = (pw_re * bb_re - pw_im * bb_im, pw_re * bb_im + pw_im * bb_re)
                ys[dr, k] = (pw_re * c_re - pw_im * c_im, pw_re * c_im + pw_im * c_re)
                pw_last[dr] = (pw_re, pw_im)
                pw_re, pw_im = pw_re * lb_re - pw_im * lb_im, pw_re * lb_im + pw_im * lb_re

        step = [GROUPS_PER_SLAB * (a // GROUPS_PER_SLAB) + (a % GROUPS_PER_SLAB - j) % GROUPS_PER_SLAB
                for a in range(CHUNK)]
        rows = lambda a: slice(a * SSM_GROUP, (a + 1) * SSM_GROUP)

        decay_ref[0, 0, j:j + 1, :] = pair(pw_last[0][0], pw_last[1][0])[0:1]
        decay_ref[0, 1, j:j + 1, :] = pair(pw_last[0][1], pw_last[1][1])[0:1]

        for a, s in enumerate(step):
            win_ref[0, j, rows(a), :LANES] = pair(xs[0, CHUNK - 1 - s][0], xs[1, s][0]).astype(BF16)
            win_ref[0, j, rows(a), LANES:] = pair(xs[0, CHUNK - 1 - s][1], xs[1, s][1]).astype(BF16)

        z_re = jnp.concatenate([pair(ys[0, t + 1][0], ys[1, CHUNK - t][0]) for t in step], axis=0)
        z_nim = jnp.concatenate([pair(-ys[0, t + 1][1], -ys[1, CHUNK - t][1]) for t in step], axis=0)
        wout_ref[0, j, :LANES, :] = z_re.T.astype(BF16)
        wout_ref[0, j, LANES:, :] = z_nim.T.astype(BF16)

        kf = lax.dot_general(
            pair(bbs[0][0], -bbs[0][1]),
            jnp.concatenate([pair(*ys[0, m]) for m in range(CHUNK)], axis=0),
            nt, precision=lax.Precision.HIGHEST, preferred_element_type=F32)
        kb = lax.dot_general(
            pair(bbs[1][0], -bbs[1][1]),
            jnp.concatenate([pair(*ys[1, CHUNK - 1 - i]) for i in range(CHUNK)], axis=0),
            nt, precision=lax.Precision.HIGHEST, preferred_element_type=F32)
        for a, s in enumerate(step):
            fwd = pltpu.roll(kf, SSM_GROUP * s, axis=1) if s else kf
            shift_b = (n - SSM_GROUP * (CHUNK - 1 - s)) % n
            bwd = pltpu.roll(kb, shift_b, axis=1) if shift_b else kb
            t_nat = (jnp.where(lane_n >= SSM_GROUP * s, fwd, 0.0)
                     + jnp.where(lane_n < SSM_GROUP * (s + 1), bwd, 0.0))
            for h in range(2):
                half = t_nat[:, h * LANES:(h + 1) * LANES]
                if j:
                    half = pltpu.roll(half, SSM_GROUP * j, axis=1)
                toep_ref[0, j, rows(a), h * LANES:(h + 1) * LANES] = half.astype(BF16)


def _ssm_prep(lam_re, lam_im, log_dt, b_re, b_im, c_re, c_im):
    depth = lam_re.shape[0]
    G, P, C = SSM_GROUPS, SSM_STATE, SSM_GROUP
    n = CHUNK * C
    gb = PREP_GROUPS
    twice = lambda a: jnp.concatenate([a, a], axis=-1)
    rep = lambda a: jnp.broadcast_to(twice(a.astype(F32))[:, :, :, None, :], (depth, 2, G, C, LANES))
    ldt = jnp.broadcast_to(log_dt.astype(F32)[:, :, :, None, None], (depth, 2, G, C, LANES))
    bt = lambda a: twice(a.astype(F32).transpose(0, 1, 2, 4, 3))
    spec_in = pl.BlockSpec((1, 2, gb, C, LANES), lambda l, i: (l, 0, i, 0, 0))
    spec_w = pl.BlockSpec((1, gb, n, n), lambda l, i: (l, i, 0, 0))
    shp_w = jax.ShapeDtypeStruct((depth, G, n, n), BF16)
    return pl.pallas_call(
        _ssm_prep_kernel,
        grid=(depth, G // gb),
        in_specs=[spec_in] * 7,
        out_specs=[spec_w, spec_w, spec_w, pl.BlockSpec((1, 2, gb, LANES), lambda l, i: (l, 0, i, 0))],
        out_shape=[shp_w, shp_w, shp_w, jax.ShapeDtypeStruct((depth, 2, G, LANES), F32)],
        compiler_params=pltpu.CompilerParams(
            dimension_semantics=("arbitrary", "arbitrary"), vmem_limit_bytes=VMEM_LIMIT),
        name="ssm_prep",
    )(rep(lam_re), rep(lam_im), ldt, bt(b_re), bt(b_im), twice(c_re.astype(F32)), twice(c_im.astype(F32)))


GROUPS_PER_SLAB = LANES // SSM_GROUP
N_SLABS = W_BRANCH // LANES
RELAYOUT_ROWS = 32


def _ssm_kernel(xa_ref, toep_ref, win_ref, wout_ref, decay_ref, d_ref, o_ref, xf, u_s, st, hs, yq):
    seq = xa_ref.shape[0]
    nchunk = seq // CHUNK
    nrb = nchunk // RELAYOUT_ROWS
    gps = GROUPS_PER_SLAB
    lane_rb = lax.broadcasted_iota(jnp.int32, (RELAYOUT_ROWS, LANES), 1)
    seg_masks = [(lane_rb >= SSM_GROUP * sg) & (lane_rb < SSM_GROUP * (sg + 1)) for sg in range(gps)]
    conv_rows = 512

    for q in range(N_SLABS):
        c_lo = q * LANES

        def conv(i, carry, c_lo=c_lo):
            r0 = pl.multiple_of(i * conv_rows, conv_rows)
            xf[pl.ds(r0, conv_rows), :] = xa_ref[pl.ds(r0, conv_rows), c_lo:c_lo + LANES].astype(F32)
            return carry

        lax.fori_loop(0, seq // conv_rows, conv, 0)

        for hh in range(2):
            def fwd_relayout(rb, carry, hh=hh):
                c0 = pl.multiple_of(rb * RELAYOUT_ROWS, RELAYOUT_ROWS)
                rolled = []
                for r in range(gps):
                    xs = xf[pl.ds(CHUNK * c0 + gps * hh + r, RELAYOUT_ROWS, stride=CHUNK), :]
                    rolled.append((xs if r == 0 else pltpu.roll(xs, SSM_GROUP * r, axis=1)).astype(BF16))
                for j in range(gps):
                    out = rolled[(0 - j) % gps]
                    for sg in range(1, gps):
                        out = jnp.where(seg_masks[sg], rolled[(sg - j) % gps], out)
                    u_s[j, pl.ds(c0, RELAYOUT_ROWS), hh * LANES:(hh + 1) * LANES] = out
                return carry

            lax.fori_loop(0, nrb, fwd_relayout, 0, unroll=True)

        for j in range(gps):
            s_in = jnp.dot(u_s[j], win_ref[gps * q + j], preferred_element_type=F32)
            st[0, pl.ds(j, nchunk, stride=gps), :] = s_in[:, :LANES]
            st[1, pl.ds(j, nchunk, stride=gps), :] = s_in[:, LANES:]

        g_lo = gps * q
        a_re = decay_ref[0, g_lo:g_lo + gps, :]
        a_im = decay_ref[1, g_lo:g_lo + gps, :]
        lo, hi = slice(0, SSM_STATE), slice(SSM_STATE, LANES)

        def scan(i, carry):
            hrf, hif, hrb, hib = carry
            rf = pl.ds(pl.multiple_of(i * gps, gps), gps)
            rb_ = pl.ds(pl.multiple_of((nchunk - 1 - i) * gps, gps), gps)
            hs[0, rf, lo] = hrf[:, lo]
            hs[1, rf, lo] = hif[:, lo]
            hs[0, rb_, hi] = hrb[:, hi]
            hs[1, rb_, hi] = hib[:, hi]
            return (a_re * hrf - a_im * hif + st[0, rf, :], a_re * hif + a_im * hrf + st[1, rf, :],
                    a_re * hrb - a_im * hib + st[0, rb_, :], a_re * hib + a_im * hrb + st[1, rb_, :])

        zero = jnp.zeros((gps, LANES), F32)
        lax.fori_loop(0, nchunk, scan, (zero, zero, zero, zero), unroll=8)

        for j in range(gps):
            g = gps * q + j
            h_in = jnp.concatenate([hs[0, pl.ds(j, nchunk, stride=gps), :].astype(BF16),
                                    hs[1, pl.ds(j, nchunk, stride=gps), :].astype(BF16)], axis=-1)
            yq[j] = (jnp.dot(u_s[j], toep_ref[g], preferred_element_type=F32)
                     + jnp.dot(h_in, wout_ref[g], preferred_element_type=F32))

        d_row = d_ref[:, c_lo:c_lo + LANES]
        for hh in range(2):
            def bwd_relayout(rb, carry, hh=hh, d_row=d_row, q=q):
                c0 = pl.multiple_of(rb * RELAYOUT_ROWS, RELAYOUT_ROWS)
                ys = [yq[j, pl.ds(c0, RELAYOUT_ROWS), hh * LANES:(hh + 1) * LANES] for j in range(gps)]
                for r in range(gps):
                    merged = ys[(0 - r) % gps]
                    for sg in range(1, gps):
                        merged = jnp.where(seg_masks[sg], ys[(sg - r) % gps], merged)
                    if r:
                        merged = pltpu.roll(merged, LANES - SSM_GROUP * r, axis=1)
                    tok = pl.ds(CHUNK * c0 + gps * hh + r, RELAYOUT_ROWS, stride=CHUNK)
                    o_ref[q, tok, :] = merged + d_row * xf[tok, :]
                return carry

            lax.fori_loop(0, nrb, bwd_relayout, 0, unroll=True)


def _ssm(proj, toep, w_in, w_out, decay, d_skip, layer, batch, seq):
    nchunk = seq // CHUNK
    n = CHUNK * SSM_GROUP
    wspec = pl.BlockSpec((None, SSM_GROUPS, n, n), lambda b: (layer, 0, 0, 0), pipeline_mode=pl.Buffered(1))
    return pl.pallas_call(
        _ssm_kernel,
        grid=(batch,),
        in_specs=[
            pl.BlockSpec((seq, W_BRANCH), lambda b: (b, COL_XA // N_SLABS)),
            wspec, wspec, wspec,
            pl.BlockSpec((None, 2, SSM_GROUPS, LANES), lambda b: (layer, 0, 0, 0)),
            pl.BlockSpec((None, 1, W_BRANCH), lambda b: (layer, 0, 0)),
        ],
        out_specs=pl.BlockSpec((N_SLABS, seq, LANES), lambda b: (0, b, 0)),
        out_shape=jax.ShapeDtypeStruct((N_SLABS, batch * seq, LANES), F32),
        scratch_shapes=[
            pltpu.VMEM((seq, LANES), F32),
            pltpu.VMEM((GROUPS_PER_SLAB, nchunk, n), BF16),
            pltpu.VMEM((2, nchunk * GROUPS_PER_SLAB, LANES), F32),
            pltpu.VMEM((2, nchunk * GROUPS_PER_SLAB, LANES), F32),
            pltpu.VMEM((GROUPS_PER_SLAB, nchunk, n), F32),
        ],
        compiler_params=pltpu.CompilerParams(
            dimension_semantics=("arbitrary",), vmem_limit_bytes=VMEM_LIMIT),
        name="ssm_mix",
    )(proj, toep, w_in, w_out, decay, d_skip)


OUT_TM = 1024


def _out_proj_kernel(ya_ref, za_ref, yb_ref, yc_ref, x_ref, w_ref, gw_ref, gb_ref, fg_ref, o_ref,
                     w_bf, gw_bf, *, final):
    @pl.when(pl.program_id(0) == 0)
    def _():
        for n in range(MIX_WIDTH // W_BRANCH):
            w_bf[n * W_BRANCH:(n + 1) * W_BRANCH, :] = w_ref[n * W_BRANCH:(n + 1) * W_BRANCH, :].astype(BF16)
        gw_bf[...] = gw_ref[...].astype(BF16)

    y = jnp.concatenate([ya_ref[i] for i in range(N_SLABS)], axis=-1)
    g = _gelu_tanh(y)
    gate = jnp.dot(g.astype(BF16), gw_bf[...], preferred_element_type=F32) + gb_ref[...]
    ya = g * _sigmoid(gate) * _silu(za_ref[...].astype(F32))
    delta = (jnp.dot(ya.astype(BF16), w_bf[:W_BRANCH, :], preferred_element_type=F32)
             + jnp.dot(yb_ref[...], w_bf[W_BRANCH:2 * W_BRANCH, :], preferred_element_type=F32)
             + jnp.dot(yc_ref[...], w_bf[2 * W_BRANCH:, :], preferred_element_type=F32))
    x = x_ref[...] + delta
    if final:
        ms = jnp.mean(x * x, axis=-1, keepdims=True)
        x = x * lax.rsqrt(ms + RMS_EPS) * fg_ref[...]
    o_ref[...] = x


def _out_proj(ya_pre, proj, yb, yc, x2d, w, glu_w, glu_b, final_g, layer, final):
    rows = x2d.shape[0]
    row_blk = lambda width: pl.BlockSpec((OUT_TM, width), lambda i: (i, 0))
    const = lambda shape: pl.BlockSpec((None,) + shape, lambda i: (layer,) + (0,) * len(shape),
                                       pipeline_mode=pl.Buffered(1))
    return pl.pallas_call(
        functools.partial(_out_proj_kernel, final=final),
        grid=(rows // OUT_TM,),
        in_specs=[
            pl.BlockSpec((N_SLABS, OUT_TM, LANES), lambda i: (0, i, 0)),
            pl.BlockSpec((OUT_TM, W_BRANCH), lambda i: (i, COL_ZA // N_SLABS)),
            row_blk(W_BRANCH), row_blk(W_BRANCH), row_blk(D_MODEL),
            const((MIX_WIDTH, D_MODEL)), const((W_BRANCH, W_BRANCH)),
            const((1, W_BRANCH)),
            pl.BlockSpec((None, 1, D_MODEL), lambda i: (0, 0, 0), pipeline_mode=pl.Buffered(1)),
        ],
        out_specs=row_blk(D_MODEL),
        out_shape=jax.ShapeDtypeStruct((rows, D_MODEL), F32),
        scratch_shapes=[pltpu.VMEM((MIX_WIDTH, D_MODEL), BF16), pltpu.VMEM((W_BRANCH, W_BRANCH), BF16)],
        compiler_params=pltpu.CompilerParams(
            dimension_semantics=("arbitrary",), vmem_limit_bytes=VMEM_LIMIT),
        name="out_proj_final" if final else "out_proj",
    )(ya_pre, proj, yb, yc, x2d, w, glu_w, glu_b, final_g)


def kernel(x, norm_g, w_in, w_out, ssm_lam_re, ssm_lam_im, ssm_log_dt, ssm_b_re, ssm_b_im, ssm_c_re,
           ssm_c_im, ssm_d, glu_w, glu_b, na_rpb, t5_bias, final_g):
    batch, seq, _ = x.shape
    depth = w_in.shape[0]
    x2d = x.astype(F32).reshape(batch * seq, D_MODEL)
    toep, s_in, s_out, decay = _ssm_prep(ssm_lam_re, ssm_lam_im, ssm_log_dt, ssm_b_re, ssm_b_im,
                                         ssm_c_re, ssm_c_im)
    dil_bias = _dil_bias_table(t5_bias)
    na_bias = _na_bias_table(na_rpb)
    w_in_f, w_out_f, glu_w_f = w_in.astype(F32), w_out.astype(F32), glu_w.astype(F32)
    row3 = lambda a: a.astype(F32).reshape(a.shape[0], 1, a.shape[1])
    norm_g3, ssm_d3, glu_b3 = row3(norm_g), row3(ssm_d), row3(glu_b)
    final_g3 = final_g.astype(F32).reshape(1, 1, D_MODEL)
    for l in range(depth):
        proj = _in_proj(x2d, norm_g3, w_in_f, l)
        ya_pre = _ssm(proj, toep, s_in, s_out, decay, ssm_d3, l, batch, seq)
        yb = _na(proj, na_bias, l, batch, seq)
        yc = _dil(proj, dil_bias, batch, seq)
        x2d = _out_proj(ya_pre, proj, yb, yc, x2d, w_out_f, glu_w_f, glu_b3, final_g3, l,
                        final=(l == depth - 1))
    return x2d.reshape(batch, seq, D_MODEL).astype(x.dtype)
```

```python
import functools

import numpy as np
import jax
import jax.numpy as jnp
from jax import lax
from jax.experimental import pallas as pl
from jax.experimental.pallas import tpu as pltpu

F32 = jnp.float32
BF16 = jnp.bfloat16

D_MODEL = 1024
HEAD_DIM = 64
W_BRANCH = 512
N_HEADS = W_BRANCH // HEAD_DIM
N_HEAD_PAIRS = N_HEADS // 2
SSM_GROUP = 16
SSM_GROUPS = W_BRANCH // SSM_GROUP
SSM_STATE = 64
GRID_W = 64
NA_ROWS = 8
NA_COLS = 16
DIL_PATTERNS = ((128, 1), (512, 4), (2048, 16))
DIL_HALF = 64
T5_BUCKETS = 32
T5_MAX_DIST = 1024
RMS_EPS = 1e-6
NEG_INF = -1e30
IN_COLS = 10 * W_BRANCH
MIX_WIDTH = 3 * W_BRANCH

COL_XA, COL_ZA, COL_QB, COL_KB, COL_VB, COL_ZB, COL_QC, COL_KC, COL_VC, COL_ZC = (
    4 * i for i in range(10))

LOG2E = float(np.log2(np.e))
Q_SCALE_LOG2 = HEAD_DIM ** -0.5 * LOG2E

LANES = 128
CHUNK = 16
VMEM_LIMIT = 56 * 1024 * 1024


def _sigmoid(z):
    return 0.5 * (1.0 + jnp.tanh(0.5 * z))


def _silu(z):
    return z * _sigmoid(z)


def _gelu_tanh(x):
    return 0.5 * x * (1.0 + jnp.tanh(np.sqrt(2.0 / np.pi).astype(np.float32) * (x + 0.044715 * (x * x * x))))


IN_TM = 512
IN_TN = 512


def _in_proj_kernel(x_ref, g_ref, w_ref, o_ref, w_bf):
    @pl.when(pl.program_id(0) == 0)
    def _():
        for n in range(IN_COLS // IN_TN):
            w_bf[:, n * IN_TN:(n + 1) * IN_TN] = w_ref[:, n * IN_TN:(n + 1) * IN_TN].astype(BF16)

    x = x_ref[...]
    ms = jnp.mean(x * x, axis=-1, keepdims=True)
    h = (x * lax.rsqrt(ms + RMS_EPS) * g_ref[...]).astype(BF16)
    for n in range(IN_COLS // IN_TN):
        o_ref[:, n * IN_TN:(n + 1) * IN_TN] = jnp.dot(
            h, w_bf[:, n * IN_TN:(n + 1) * IN_TN], preferred_element_type=F32).astype(BF16)


def _in_proj(x2d, g, w, layer):
    rows = x2d.shape[0]
    return pl.pallas_call(
        _in_proj_kernel,
        grid=(rows // IN_TM,),
        in_specs=[
            pl.BlockSpec((IN_TM, D_MODEL), lambda i: (i, 0)),
            pl.BlockSpec((None, 1, D_MODEL), lambda i: (layer, 0, 0)),
            pl.BlockSpec((None, D_MODEL, IN_COLS), lambda i: (layer, 0, 0), pipeline_mode=pl.Buffered(1)),
        ],
        out_specs=pl.BlockSpec((IN_TM, IN_COLS), lambda i: (i, 0)),
        out_shape=jax.ShapeDtypeStruct((rows, IN_COLS), BF16),
        scratch_shapes=[pltpu.VMEM((D_MODEL, IN_COLS), BF16)],
        compiler_params=pltpu.CompilerParams(
            dimension_semantics=("arbitrary",), vmem_limit_bytes=VMEM_LIMIT),
        name="in_proj",
    )(x2d, g, w)


def _na_bias_kernel(rpb_ref, o_ref):
    lane = lax.broadcasted_iota(jnp.int32, (GRID_W, LANES), 1)
    j = lax.broadcasted_iota(jnp.int32, (GRID_W, LANES), 0)
    c = lane & (GRID_W - 1)
    col_start = jnp.clip(j - NA_COLS // 2, 0, GRID_W - NA_COLS)
    valid = (c >= col_start) & (c < col_start + NA_COLS)
    first = lane < GRID_W
    even, odd = [], []
    for a in range(2 * NA_ROWS - 1):
        vec = pltpu.roll(rpb_ref[a:a + 1, :], LANES - (NA_COLS - 1), axis=1)
        x = jnp.broadcast_to(vec, (GRID_W, LANES))
        even.append(pltpu.roll(x, 0, axis=1, stride=1, stride_axis=0))
        odd.append(pltpu.roll(x, GRID_W, axis=1, stride=1, stride_axis=0))
    for v in range(NA_ROWS):
        for i in range(NA_ROWS // 2):
            a0 = 2 * i - v + (NA_ROWS - 1)
            tile = jnp.where(first, even[a0], odd[a0 + 1])
            o_ref[v, :, i * LANES:(i + 1) * LANES] = jnp.where(valid, tile * LOG2E, NEG_INF)


def _na_bias_table(rpb):
    depth = rpb.shape[0]
    padded = jnp.pad(rpb.astype(F32), ((0, 0), (0, 0), (0, 1), (0, LANES - (2 * NA_COLS - 1))))
    nkeys = NA_ROWS * GRID_W
    return pl.pallas_call(
        _na_bias_kernel,
        grid=(depth, N_HEADS),
        in_specs=[pl.BlockSpec((None, None, 2 * NA_ROWS, LANES), lambda l, h: (l, h, 0, 0))],
        out_specs=pl.BlockSpec((None, None, NA_ROWS, GRID_W, nkeys), lambda l, h: (l, h, 0, 0, 0)),
        out_shape=jax.ShapeDtypeStruct((depth, N_HEADS, NA_ROWS, GRID_W, nkeys), F32),
        compiler_params=pltpu.CompilerParams(dimension_semantics=("arbitrary", "arbitrary")),
        name="na_bias",
    )(padded)


def _na_kernel(q_ref, k_ref, v_ref, z_ref, bias_ref, o_ref):
    seq = q_ref.shape[0]
    rows = seq // GRID_W
    half_rows = NA_ROWS // 2
    tile_keys = half_rows * GRID_W
    lane = lax.broadcasted_iota(jnp.int32, (GRID_W, LANES), 1)
    first_head = lane < HEAD_DIM
    masks = (first_head, jnp.logical_not(first_head))
    nt = (((1,), (1,)), ((), ()))
    ones = jnp.ones((tile_keys, LANES), BF16)
    row_start = lambda r: min(max(r - NA_ROWS // 2, 0), rows - NA_ROWS)

    for g0 in range(0, rows, NA_GROUP):
        group = range(g0, g0 + NA_GROUP)
        users = {}
        for r in group:
            for half in range(2):
                users.setdefault(row_start(r) + half_rows * half, []).append((r, half))
        qm = {}
        for r in group:
            qb = (q_ref[r * GRID_W:(r + 1) * GRID_W, :].astype(F32) * Q_SCALE_LOG2).astype(BF16)
            for hd in range(2):
                qm[r, hd] = jnp.where(masks[hd], qb, jnp.zeros_like(qb))
        s_half = {}
        for k, us in sorted(users.items()):
            lhs = jnp.concatenate([qm[r, hd] for r, _ in us for hd in range(2)], axis=0)
            part = lax.dot_general(lhs, k_ref[k * GRID_W:k * GRID_W + tile_keys, :], nt,
                                   preferred_element_type=F32)
            for i, (r, half) in enumerate(us):
                for hd in range(2):
                    blk = part[(2 * i + hd) * GRID_W:(2 * i + hd + 1) * GRID_W, :]
                    bias = bias_ref[hd, r - row_start(r), :, half * tile_keys:(half + 1) * tile_keys]
                    s_half[r, hd, half] = (blk + bias).astype(BF16)
        p_half = {}
        for r in group:
            for hd in range(2):
                s0, s1 = s_half[r, hd, 0], s_half[r, hd, 1]
                m = jnp.max(jnp.maximum(s0, s1), axis=-1, keepdims=True)
                p_half[r, hd, 0], p_half[r, hd, 1] = jnp.exp2(s0 - m), jnp.exp2(s1 - m)
        acc = {}
        for k, us in sorted(users.items()):
            lhs = jnp.concatenate([p_half[r, hd, half] for r, half in us for hd in range(2)], axis=0)
            vext = jnp.concatenate([v_ref[k * GRID_W:k * GRID_W + tile_keys, :], ones], axis=-1)
            part = jnp.dot(lhs, vext, preferred_element_type=F32)
            for i, (r, half) in enumerate(us):
                for hd in range(2):
                    blk = part[(2 * i + hd) * GRID_W:(2 * i + hd + 1) * GRID_W, :]
                    acc[r, hd] = blk if (r, hd) not in acc else acc[r, hd] + blk
        for r in group:
            y = jnp.where(first_head, acc[r, 0][:, :LANES], acc[r, 1][:, :LANES])
            d = jnp.where(first_head, acc[r, 0][:, LANES:], acc[r, 1][:, LANES:])
            z = z_ref[r * GRID_W:(r + 1) * GRID_W, :].astype(F32)
            o_ref[r * GRID_W:(r + 1) * GRID_W, :] = (y * (1.0 / d) * _silu(z)).astype(BF16)


NA_GROUP = 16


def _na(proj, bias, layer, batch, seq):
    def col(c0):
        return pl.BlockSpec((seq, LANES), lambda b, p, c0=c0: (b, c0 + p))

    return pl.pallas_call(
        _na_kernel,
        grid=(batch, N_HEAD_PAIRS),
        in_specs=[
            col(COL_QB), col(COL_KB), col(COL_VB), col(COL_ZB),
            pl.BlockSpec((None, 2, NA_ROWS, GRID_W, NA_ROWS * GRID_W), lambda b, p: (layer, p, 0, 0, 0)),
        ],
        out_specs=pl.BlockSpec((seq, LANES), lambda b, p: (b, p)),
        out_shape=jax.ShapeDtypeStruct((batch * seq, W_BRANCH), BF16),
        compiler_params=pltpu.CompilerParams(
            dimension_semantics=("arbitrary", "arbitrary"), vmem_limit_bytes=VMEM_LIMIT),
        name="na_attn",
    )(proj, proj, proj, proj, bias)


DIL_QB = 128
DIL_KW = 256
N_VARIANTS = 3


def _t5_bucket(rel):
    nb = T5_BUCKETS // 2
    max_exact = nb // 2
    n = np.abs(rel)
    large = max_exact + (np.log(np.maximum(n, 1) / max_exact) / np.log(T5_MAX_DIST / max_exact)
                         * (nb - max_exact)).astype(np.int32)
    large = np.minimum(large, nb - 1)
    return (np.where(rel > 0, nb, 0) + np.where(n < max_exact, n, large)).astype(np.int32)


def _dil_bias_table(t5_bias):
    n_pat = len(DIL_PATTERNS)
    lane = np.arange(DIL_VEC)
    n = np.where(lane < DIL_KW, lane, lane - DIL_VEC)
    in_range = (lane < DIL_KW) | (lane > DIL_VEC - DIL_QB)
    pick = np.zeros((n_pat, N_VARIANTS, T5_BUCKETS, DIL_VEC), np.float32)
    neg = np.zeros((n_pat, N_VARIANTS, 1, DIL_VEC), np.float32)
    for pi, (_, d) in enumerate(DIL_PATTERNS):
        for var in range(N_VARIANTS):
            step = n - DIL_HALF * var
            ok = in_range & (np.abs(step) <= DIL_HALF)
            bucket = _t5_bucket(d * np.clip(step, -DIL_HALF, DIL_HALF))
            pick[pi, var, bucket[ok], lane[ok]] = 1.0
            neg[pi, var, 0, ~ok] = NEG_INF
    vecs = jnp.einsum('bh,pvbn->hpvn', t5_bias.astype(F32), pick,
                      precision=lax.Precision.HIGHEST) * LOG2E + neg[:, :, 0][None]
    return pl.pallas_call(
        _dil_bias_kernel,
        grid=(N_HEADS,),
        in_specs=[pl.BlockSpec((None, n_pat * N_VARIANTS, DIL_VEC), lambda h: (h, 0, 0))],
        out_specs=pl.BlockSpec((None, n_pat, N_VARIANTS, DIL_QB, DIL_KW), lambda h: (h, 0, 0, 0, 0)),
        out_shape=jax.ShapeDtypeStruct((N_HEADS, n_pat, N_VARIANTS, DIL_QB, DIL_KW), F32),
        compiler_params=pltpu.CompilerParams(dimension_semantics=("arbitrary",)),
        name="dil_bias",
    )(vecs.reshape(N_HEADS, n_pat * N_VARIANTS, DIL_VEC))


DIL_VEC = 512


def _dil_bias_kernel(vec_ref, o_ref):
    for pi in range(len(DIL_PATTERNS)):
        for var in range(N_VARIANTS):
            i = pi * N_VARIANTS + var
            x = jnp.broadcast_to(vec_ref[i:i + 1, :], (DIL_QB, DIL_VEC))
            o_ref[pi, var] = pltpu.roll(x, 0, axis=1, stride=1, stride_axis=0)[:, :DIL_KW]


DIL_GROUP = 16
DIL_GROUP_FIRST = 16


def _dil_kernel(q_ref, k_ref, v_ref, z_ref, bias_ref, o_ref, qf, kf, vf, qg, kg, vg, qd,
                q4, k4, v4, m_s, l_s, acc_s):
    seq = q_ref.shape[0]
    conv_rows = 256
    lane = lax.broadcasted_iota(jnp.int32, (DIL_QB, LANES), 1)
    first_head = lane < HEAD_DIM
    nt = (((1,), (1,)), ((), ()))
    ones = jnp.ones((DIL_KW, LANES), BF16)

    def conv(i, carry):
        r0 = pl.multiple_of(i * conv_rows, conv_rows)
        qf[pl.ds(r0, conv_rows), :] = q_ref[pl.ds(r0, conv_rows), :].astype(F32) * Q_SCALE_LOG2
        kf[pl.ds(r0, conv_rows), :] = k_ref[pl.ds(r0, conv_rows), :].astype(F32)
        vf[pl.ds(r0, conv_rows), :] = v_ref[pl.ds(r0, conv_rows), :].astype(F32)
        return carry

    lax.fori_loop(0, seq // conv_rows, conv, 0)

    n4 = seq // 4
    n16 = seq // 16

    def to_order4(i, carry):
        dst = pl.multiple_of(i * conv_rows, conv_rows)
        src = pl.ds(dst // n4 + 4 * (dst % n4), conv_rows, stride=4)
        for a, b, c in ((qf, qg, q4), (kf, kg, k4), (vf, vg, v4)):
            part = a[src, :]
            b[pl.ds(dst, conv_rows), :] = part
            c[pl.ds(dst, conv_rows), :] = part.astype(BF16)
        return carry

    lax.fori_loop(0, seq // conv_rows, to_order4, 0, unroll=True)

    def window(src, res, start, size, n_sub):
        if n_sub == n16:
            rows = pl.ds((res % 4) * n4 + res // 4 + 4 * start, size, stride=4)
            return src[rows, :].astype(BF16)
        return src[pl.ds(pl.multiple_of(res * n_sub + start, DIL_HALF), size), :]

    order4_state = (qf, kf, vf)
    token_state = (m_s, l_s, acc_s)
    plan = ((2, 16, (qg, kg, vg), None, order4_state),
            (1, 4, (q4, k4, v4), order4_state, token_state),
            (0, 1, (qd, k_ref, v_ref), token_state, token_state))

    for step, (pi, d, (q_src, k_src, v_src), st_in, st_out) in enumerate(plan):
        n_sub = seq // d
        nblk = n_sub // DIL_QB

        if step == 2:
            def scaled_q(i, carry):
                r0 = pl.multiple_of(i * conv_rows, conv_rows)
                qd[pl.ds(r0, conv_rows), :] = (
                    q_ref[pl.ds(r0, conv_rows), :].astype(F32) * Q_SCALE_LOG2).astype(BF16)
                return carry

            lax.fori_loop(0, seq // conv_rows, scaled_q, 0, unroll=True)

        def rows_out(res, i0, step=step):
            if step == 0:
                return pl.ds((res % 4) * n4 + res // 4 + 4 * i0, DIL_QB, stride=4)
            if step == 1:
                return pl.ds(res + 4 * i0, DIL_QB, stride=4)
            return pl.ds(pl.multiple_of(i0, DIL_HALF), DIL_QB)

        n_group = DIL_GROUP_FIRST if step == 0 else DIL_GROUP

        def group(gi, carry, pi=pi, n_sub=n_sub, nblk=nblk, q_src=q_src, k_src=k_src, v_src=v_src,
                  st_in=st_in, st_out=st_out, rows_out=rows_out, n_group=n_group):
            units = []
            for i in range(n_group):
                u = gi * n_group + i
                res = u // nblk
                i0 = (u % nblk) * DIL_QB
                ks = jnp.clip(i0 - DIL_HALF, 0, n_sub - DIL_KW)
                units.append((res, i0, ks, (i0 - ks) // DIL_HALF))
            s_tiles, m_tiles = [], []
            for res, i0, ks, var in units:
                qb = window(q_src, res, i0, DIL_QB, n_sub)
                kw = window(k_src, res, ks, DIL_KW, n_sub)
                for hd in range(2):
                    keep = first_head if hd == 0 else jnp.logical_not(first_head)
                    qm = jnp.where(keep, qb, jnp.zeros_like(qb))
                    s = lax.dot_general(qm, kw, nt, preferred_element_type=F32) + bias_ref[hd, pi, var]
                    s = s.astype(BF16)
                    s_tiles.append(s)
                    m_tiles.append(jnp.max(s, axis=-1, keepdims=True))
            p_tiles = [jnp.exp2(s - m) for s, m in zip(s_tiles, m_tiles)]
            for i, (res, i0, ks, var) in enumerate(units):
                vw = window(v_src, res, ks, DIL_KW, n_sub)
                vext = jnp.concatenate([vw, ones], axis=-1)
                r0 = jnp.dot(p_tiles[2 * i], vext, preferred_element_type=F32)
                r1 = jnp.dot(p_tiles[2 * i + 1], vext, preferred_element_type=F32)
                m_cur = jnp.where(first_head, m_tiles[2 * i].astype(F32), m_tiles[2 * i + 1].astype(F32))
                l_cur = jnp.where(first_head, r0[:, LANES:], r1[:, LANES:])
                acc_cur = jnp.where(first_head, r0[:, :LANES], r1[:, :LANES])
                dst = rows_out(res, i0)
                if st_in is None:
                    m_new, l_new, acc_new = m_cur, l_cur, acc_cur
                else:
                    src = pl.ds(pl.multiple_of(res * n_sub + i0, DIL_HALF), DIL_QB)
                    m_old = st_in[0][src, :]
                    m_new = jnp.maximum(m_old, m_cur)
                    a_old = jnp.exp2(m_old - m_new)
                    a_cur = jnp.exp2(m_cur - m_new)
                    l_new = a_old * st_in[1][src, :] + a_cur * l_cur
                    acc_new = a_old * st_in[2][src, :] + a_cur * acc_cur
                st_out[0][dst, :] = m_new
                st_out[1][dst, :] = l_new
                st_out[2][dst, :] = acc_new
            return carry

        lax.fori_loop(0, seq // DIL_QB // n_group, group, 0)

    def fin(i, carry):
        r0 = pl.multiple_of(i * conv_rows, conv_rows)
        z = z_ref[pl.ds(r0, conv_rows), :].astype(F32)
        y = acc_s[pl.ds(r0, conv_rows), :] * (1.0 / l_s[pl.ds(r0, conv_rows), :])
        o_ref[pl.ds(r0, conv_rows), :] = (y * _silu(z)).astype(BF16)
        return carry

    lax.fori_loop(0, seq // conv_rows, fin, 0, unroll=4)


def _dil(proj, bias, batch, seq):
    def col(c0):
        return pl.BlockSpec((seq, LANES), lambda b, p, c0=c0: (b, c0 + p))

    scratch = ([pltpu.VMEM((seq, LANES), F32) for _ in range(6)]
               + [pltpu.VMEM((seq, LANES), BF16) for _ in range(4)]
               + [pltpu.VMEM((seq, LANES), F32) for _ in range(3)])
    return pl.pallas_call(
        _dil_kernel,
        grid=(batch, N_HEAD_PAIRS),
        in_specs=[
            col(COL_QC), col(COL_KC), col(COL_VC), col(COL_ZC),
            pl.BlockSpec((2, len(DIL_PATTERNS), N_VARIANTS, DIL_QB, DIL_KW),
                         lambda b, p: (p, 0, 0, 0, 0)),
        ],
        out_specs=pl.BlockSpec((seq, LANES), lambda b, p: (b, p)),
        out_shape=jax.ShapeDtypeStruct((batch * seq, W_BRANCH), BF16),
        scratch_shapes=scratch,
        compiler_params=pltpu.CompilerParams(
            dimension_semantics=("arbitrary", "arbitrary"), vmem_limit_bytes=VMEM_LIMIT),
        name="dil_attn",
    )(proj, proj, proj, proj, bias)


N_POW = CHUNK + 1
PREP_GROUPS = 8


def _ssm_prep_kernel(lre_ref, lim_ref, ldt_ref, btre_ref, btim_ref, cre_ref, cim_ref,
                     toep_ref, win_ref, wout_ref, decay_ref):
    n = CHUNK * SSM_GROUP
    lane = lax.broadcasted_iota(jnp.int32, (SSM_GROUP, LANES), 1)
    lo = lane < SSM_STATE
    lane_n = lax.broadcasted_iota(jnp.int32, (SSM_GROUP, n), 1)
    nt = (((1,), (1,)), ((), ()))
    pair = lambda x, y: jnp.where(lo, x, y)

    swap = lambda x: pltpu.roll(x, SSM_STATE, axis=1)

    for j in range(PREP_GROUPS):
        xs, ys = {}, {}
        lam_re = lre_ref[0, j]
        lam_im = lim_ref[0, j]
        dt = jnp.exp(ldt_ref[0, j])
        mag = jnp.exp(lam_re * dt)
        lb_re = mag * jnp.cos(lam_im * dt)
        lb_im = mag * jnp.sin(lam_im * dt)
        n_re = lb_re - 1.0
        den = lam_re * lam_re + lam_im * lam_im
        q_re = (n_re * lam_re + lb_im * lam_im) / den
        q_im = (lb_im * lam_re - n_re * lam_im) / den
        bt_re = btre_ref[0, j]
        bt_im = btim_ref[0, j]
        bb_re = q_re * bt_re - q_im * bt_im
        bb_im = q_re * bt_im + q_im * bt_re
        c_re = cre_ref[0, j]
        c_im = cim_ref[0, j]
        pw_re = jnp.ones_like(lb_re)
        pw_im = jnp.zeros_like(lb_re)
        for k in range(N_POW):
            xs[k] = (pw_re * bb_re - pw_im * bb_im, pw_re * bb_im + pw_im * bb_re)
            ys[k] = (pw_re * c_re - pw_im * c_im, pw_re * c_im + pw_im * c_re)
            pw_last = (pw_re, pw_im)
            pw_re, pw_im = pw_re * lb_re - pw_im * lb_im, pw_re * lb_im + pw_im * lb_re

        step = [GROUPS_PER_SLAB * (a // GROUPS_PER_SLAB) + (a % GROUPS_PER_SLAB - j) % GROUPS_PER_SLAB
                for a in range(CHUNK)]
        rows = lambda a: slice(a * SSM_GROUP, (a + 1) * SSM_GROUP)

        decay_ref[0, 0, j:j + 1, :] = pw_last[0][0:1]
        decay_ref[0, 1, j:j + 1, :] = pw_last[1][0:1]

        for a, s in enumerate(step):
            win_ref[0, j, rows(a), :LANES] = pair(xs[CHUNK - 1 - s][0], xs[s][0]).astype(BF16)
            win_ref[0, j, rows(a), LANES:] = pair(xs[CHUNK - 1 - s][1], xs[s][1]).astype(BF16)

        z_re = jnp.concatenate([pair(ys[t + 1][0], ys[CHUNK - t][0]) for t in step], axis=0)
        z_nim = jnp.concatenate([pair(-ys[t + 1][1], -ys[CHUNK - t][1]) for t in step], axis=0)
        wout_ref[0, j, :LANES, :] = z_re.T.astype(BF16)
        wout_ref[0, j, LANES:, :] = z_nim.T.astype(BF16)

        kf = lax.dot_general(
            pair(bb_re, -swap(bb_im)),
            jnp.concatenate([pair(ys[m][0], swap(ys[m][1])) for m in range(CHUNK)], axis=0),
            nt, precision=lax.Precision.HIGHEST, preferred_element_type=F32)
        kb = lax.dot_general(
            pair(swap(bb_re), -bb_im),
            jnp.concatenate([pair(swap(ys[CHUNK - 1 - i][0]), ys[CHUNK - 1 - i][1])
                             for i in range(CHUNK)], axis=0),
            nt, precision=lax.Precision.HIGHEST, preferred_element_type=F32)
        for a, s in enumerate(step):
            fwd = pltpu.roll(kf, SSM_GROUP * s, axis=1) if s else kf
            shift_b = (n - SSM_GROUP * (CHUNK - 1 - s)) % n
            bwd = pltpu.roll(kb, shift_b, axis=1) if shift_b else kb
            t_nat = (jnp.where(lane_n >= SSM_GROUP * s, fwd, 0.0)
                     + jnp.where(lane_n < SSM_GROUP * (s + 1), bwd, 0.0))
            for h in range(2):
                half = t_nat[:, h * LANES:(h + 1) * LANES]
                if j:
                    half = pltpu.roll(half, SSM_GROUP * j, axis=1)
                toep_ref[0, j, rows(a), h * LANES:(h + 1) * LANES] = half.astype(BF16)


def _ssm_prep(lam_re, lam_im, log_dt, b_re, b_im, c_re, c_im):
    depth = lam_re.shape[0]
    G, P, C = SSM_GROUPS, SSM_STATE, SSM_GROUP
    n = CHUNK * C
    gb = PREP_GROUPS
    both = lambda a: jnp.concatenate([a[:, 0], a[:, 1]], axis=-1).astype(F32)
    rep = lambda a: jnp.broadcast_to(both(a)[:, :, None, :], (depth, G, C, LANES))
    ldt = rep(jnp.broadcast_to(log_dt[..., None], (depth, 2, G, P)))
    bt = lambda a: both(a.transpose(0, 1, 2, 4, 3))
    spec_in = pl.BlockSpec((1, gb, C, LANES), lambda l, i: (l, i, 0, 0))
    spec_w = pl.BlockSpec((1, gb, n, n), lambda l, i: (l, i, 0, 0))
    shp_w = jax.ShapeDtypeStruct((depth, G, n, n), BF16)
    return pl.pallas_call(
        _ssm_prep_kernel,
        grid=(depth, G // gb),
        in_specs=[spec_in] * 7,
        out_specs=[spec_w, spec_w, spec_w, pl.BlockSpec((1, 2, gb, LANES), lambda l, i: (l, 0, i, 0))],
        out_shape=[shp_w, shp_w, shp_w, jax.ShapeDtypeStruct((depth, 2, G, LANES), F32)],
        compiler_params=pltpu.CompilerParams(
            dimension_semantics=("arbitrary", "arbitrary"), vmem_limit_bytes=VMEM_LIMIT),
        name="ssm_prep",
    )(rep(lam_re), rep(lam_im), ldt, bt(b_re), bt(b_im), both(c_re), both(c_im))


GROUPS_PER_SLAB = LANES // SSM_GROUP
N_SLABS = W_BRANCH // LANES
RELAYOUT_ROWS = 32


def _ssm_kernel(xa_ref, toep_ref, win_ref, wout_ref, decay_ref, d_ref, o_ref, xf, u_s, st, hs, yq):
    seq = xa_ref.shape[0]
    nchunk = seq // CHUNK
    nrb = nchunk // RELAYOUT_ROWS
    gps = GROUPS_PER_SLAB
    lane_rb = lax.broadcasted_iota(jnp.int32, (RELAYOUT_ROWS, LANES), 1)
    seg_masks = [(lane_rb >= SSM_GROUP * sg) & (lane_rb < SSM_GROUP * (sg + 1)) for sg in range(gps)]
    conv_rows = 512

    for q in range(N_SLABS):
        c_lo = q * LANES

        def conv(i, carry, c_lo=c_lo):
            r0 = pl.multiple_of(i * conv_rows, conv_rows)
            xf[pl.ds(r0, conv_rows), :] = xa_ref[pl.ds(r0, conv_rows), c_lo:c_lo + LANES].astype(F32)
            return carry

        lax.fori_loop(0, seq // conv_rows, conv, 0)

        for hh in range(2):
            def fwd_relayout(rb, carry, hh=hh):
                c0 = pl.multiple_of(rb * RELAYOUT_ROWS, RELAYOUT_ROWS)
                rolled = []
                for r in range(gps):
                    xs = xf[pl.ds(CHUNK * c0 + gps * hh + r, RELAYOUT_ROWS, stride=CHUNK), :]
                    rolled.append((xs if r == 0 else pltpu.roll(xs, SSM_GROUP * r, axis=1)).astype(BF16))
                for j in range(gps):
                    out = rolled[(0 - j) % gps]
                    for sg in range(1, gps):
                        out = jnp.where(seg_masks[sg], rolled[(sg - j) % gps], out)
                    u_s[j, pl.ds(c0, RELAYOUT_ROWS), hh * LANES:(hh + 1) * LANES] = out
                return carry

            lax.fori_loop(0, nrb, fwd_relayout, 0, unroll=True)

        for j in range(gps):
            s_in = jnp.dot(u_s[j], win_ref[gps * q + j], preferred_element_type=F32)
            st[0, pl.ds(j, nchunk, stride=gps), :] = s_in[:, :LANES]
            st[1, pl.ds(j, nchunk, stride=gps), :] = s_in[:, LANES:]

        g_lo = gps * q
        a_re = decay_ref[0, g_lo:g_lo + gps, :]
        a_im = decay_ref[1, g_lo:g_lo + gps, :]
        lo, hi = slice(0, SSM_STATE), slice(SSM_STATE, LANES)

        def scan(i, carry):
            hrf, hif, hrb, hib = carry
            rf = pl.ds(pl.multiple_of(i * gps, gps), gps)
            rb_ = pl.ds(pl.multiple_of((nchunk - 1 - i) * gps, gps), gps)
            hs[0, rf, lo] = hrf[:, lo]
            hs[1, rf, lo] = hif[:, lo]
            hs[0, rb_, hi] = hrb[:, hi]
            hs[1, rb_, hi] = hib[:, hi]
            return (a_re * hrf - a_im * hif + st[0, rf, :], a_re * hif + a_im * hrf + st[1, rf, :],
                    a_re * hrb - a_im * hib + st[0, rb_, :], a_re * hib + a_im * hrb + st[1, rb_, :])

        zero = jnp.zeros((gps, LANES), F32)
        lax.fori_loop(0, nchunk, scan, (zero, zero, zero, zero), unroll=8)

        for j in range(gps):
            g = gps * q + j
            h_in = jnp.concatenate([hs[0, pl.ds(j, nchunk, stride=gps), :].astype(BF16),
                                    hs[1, pl.ds(j, nchunk, stride=gps), :].astype(BF16)], axis=-1)
            yq[j] = (jnp.dot(u_s[j], toep_ref[g], preferred_element_type=F32)
                     + jnp.dot(h_in, wout_ref[g], preferred_element_type=F32))

        d_row = d_ref[:, c_lo:c_lo + LANES]
        for hh in range(2):
            def bwd_relayout(rb, carry, hh=hh, d_row=d_row, q=q):
                c0 = pl.multiple_of(rb * RELAYOUT_ROWS, RELAYOUT_ROWS)
                ys = [yq[j, pl.ds(c0, RELAYOUT_ROWS), hh * LANES:(hh + 1) * LANES] for j in range(gps)]
                for r in range(gps):
                    merged = ys[(0 - r) % gps]
                    for sg in range(1, gps):
                        merged = jnp.where(seg_masks[sg], ys[(sg - r) % gps], merged)
                    if r:
                        merged = pltpu.roll(merged, LANES - SSM_GROUP * r, axis=1)
                    tok = pl.ds(CHUNK * c0 + gps * hh + r, RELAYOUT_ROWS, stride=CHUNK)
                    o_ref[q, tok, :] = merged + d_row * xf[tok, :]
                return carry

            lax.fori_loop(0, nrb, bwd_relayout, 0, unroll=True)


def _ssm(proj, toep, w_in, w_out, decay, d_skip, layer, batch, seq):
    nchunk = seq // CHUNK
    n = CHUNK * SSM_GROUP
    wspec = pl.BlockSpec((None, SSM_GROUPS, n, n), lambda b: (layer, 0, 0, 0), pipeline_mode=pl.Buffered(1))
    return pl.pallas_call(
        _ssm_kernel,
        grid=(batch,),
        in_specs=[
            pl.BlockSpec((seq, W_BRANCH), lambda b: (b, COL_XA // N_SLABS)),
            wspec, wspec, wspec,
            pl.BlockSpec((None, 2, SSM_GROUPS, LANES), lambda b: (layer, 0, 0, 0)),
            pl.BlockSpec((None, 1, W_BRANCH), lambda b: (layer, 0, 0)),
        ],
        out_specs=pl.BlockSpec((N_SLABS, seq, LANES), lambda b: (0, b, 0)),
        out_shape=jax.ShapeDtypeStruct((N_SLABS, batch * seq, LANES), F32),
        scratch_shapes=[
            pltpu.VMEM((seq, LANES), F32),
            pltpu.VMEM((GROUPS_PER_SLAB, nchunk, n), BF16),
            pltpu.VMEM((2, nchunk * GROUPS_PER_SLAB, LANES), F32),
            pltpu.VMEM((2, nchunk * GROUPS_PER_SLAB, LANES), F32),
            pltpu.VMEM((GROUPS_PER_SLAB, nchunk, n), F32),
        ],
        compiler_params=pltpu.CompilerParams(
            dimension_semantics=("arbitrary",), vmem_limit_bytes=VMEM_LIMIT),
        name="ssm_mix",
    )(proj, toep, w_in, w_out, decay, d_skip)


OUT_TM = 1024


def _out_proj_kernel(ya_ref, za_ref, yb_ref, yc_ref, x_ref, w_ref, gw_ref, gb_ref, fg_ref, o_ref,
                     w_bf, gw_bf, *, final):
    @pl.when(pl.program_id(0) == 0)
    def _():
        for n in range(MIX_WIDTH // W_BRANCH):
            w_bf[n * W_BRANCH:(n + 1) * W_BRANCH, :] = w_ref[n * W_BRANCH:(n + 1) * W_BRANCH, :].astype(BF16)
        gw_bf[...] = gw_ref[...].astype(BF16)

    y = jnp.concatenate([ya_ref[i] for i in range(N_SLABS)], axis=-1)
    g = _gelu_tanh(y)
    gate = jnp.dot(g.astype(BF16), gw_bf[...], preferred_element_type=F32) + gb_ref[...]
    ya = g * _sigmoid(gate) * _silu(za_ref[...].astype(F32))
    delta = (jnp.dot(ya.astype(BF16), w_bf[:W_BRANCH, :], preferred_element_type=F32)
             + jnp.dot(yb_ref[...], w_bf[W_BRANCH:2 * W_BRANCH, :], preferred_element_type=F32)
             + jnp.dot(yc_ref[...], w_bf[2 * W_BRANCH:, :], preferred_element_type=F32))
    x = x_ref[...] + delta
    if final:
        ms = jnp.mean(x * x, axis=-1, keepdims=True)
        x = x * lax.rsqrt(ms + RMS_EPS) * fg_ref[...]
    o_ref[...] = x


def _out_proj(ya_pre, proj, yb, yc, x2d, w, glu_w, glu_b, final_g, layer, final):
    rows = x2d.shape[0]
    row_blk = lambda width: pl.BlockSpec((OUT_TM, width), lambda i: (i, 0))
    const = lambda shape: pl.BlockSpec((None,) + shape, lambda i: (layer,) + (0,) * len(shape),
                                       pipeline_mode=pl.Buffered(1))
    return pl.pallas_call(
        functools.partial(_out_proj_kernel, final=final),
        grid=(rows // OUT_TM,),
        in_specs=[
            pl.BlockSpec((N_SLABS, OUT_TM, LANES), lambda i: (0, i, 0)),
            pl.BlockSpec((OUT_TM, W_BRANCH), lambda i: (i, COL_ZA // N_SLABS)),
            row_blk(W_BRANCH), row_blk(W_BRANCH), row_blk(D_MODEL),
            const((MIX_WIDTH, D_MODEL)), const((W_BRANCH, W_BRANCH)),
            const((1, W_BRANCH)),
            pl.BlockSpec((None, 1, D_MODEL), lambda i: (0, 0, 0), pipeline_mode=pl.Buffered(1)),
        ],
        out_specs=row_blk(D_MODEL),
        out_shape=jax.ShapeDtypeStruct((rows, D_MODEL), F32),
        scratch_shapes=[pltpu.VMEM((MIX_WIDTH, D_MODEL), BF16), pltpu.VMEM((W_BRANCH, W_BRANCH), BF16)],
        compiler_params=pltpu.CompilerParams(
            dimension_semantics=("arbitrary",), vmem_limit_bytes=VMEM_LIMIT),
        name="out_proj_final" if final else "out_proj",
    )(ya_pre, proj, yb, yc, x2d, w, glu_w, glu_b, final_g)


def kernel(x, norm_g, w_in, w_out, ssm_lam_re, ssm_lam_im, ssm_log_dt, ssm_b_re, ssm_b_im, ssm_c_re,
           ssm_c_im, ssm_d, glu_w, glu_b, na_rpb, t5_bias, final_g):
    batch, seq, _ = x.shape
    depth = w_in.shape[0]
    x2d = x.astype(F32).reshape(batch * seq, D_MODEL)
    toep, s_in, s_out, decay = _ssm_prep(ssm_lam_re, ssm_lam_im, ssm_log_dt, ssm_b_re, ssm_b_im,
                                         ssm_c_re, ssm_c_im)
    dil_bias = _dil_bias_table(t5_bias)
    na_bias = _na_bias_table(na_rpb)
    w_in_f, w_out_f, glu_w_f = w_in.astype(F32), w_out.astype(F32), glu_w.astype(F32)
    row3 = lambda a: a.astype(F32).reshape(a.shape[0], 1, a.shape[1])
    norm_g3, ssm_d3, glu_b3 = row3(norm_g), row3(ssm_d), row3(glu_b)
    final_g3 = final_g.astype(F32).reshape(1, 1, D_MODEL)
    for l in range(depth):
        proj = _in_proj(x2d, norm_g3, w_in_f, l)
        ya_pre = _ssm(proj, toep, s_in, s_out, decay, ssm_d3, l, batch, seq)
        yb = _na(proj, na_bias, l, batch, seq)
        yc = _dil(proj, dil_bias, batch, seq)
        x2d = _out_proj(ya_pre, proj, yb, yc, x2d, w_out_f, glu_w_f, glu_b3, final_g3, l,
                        final=(l == depth - 1))
    return x2d.reshape(batch, seq, D_MODEL).astype(x.dtype)
```

```python
import functools

import numpy as np
import jax
import jax.numpy as jnp
from jax import lax
from jax.experimental import pallas as pl
from jax.experimental.pallas import tpu as pltpu

F32 = jnp.float32
BF16 = jnp.bfloat16

D_MODEL = 1024
HEAD_DIM = 64
W_BRANCH = 512
N_HEADS = W_BRANCH // HEAD_DIM
N_HEAD_PAIRS = N_HEADS // 2
SSM_GROUP = 16
SSM_GROUPS = W_BRANCH // SSM_GROUP
SSM_STATE = 64
GRID_W = 64
NA_ROWS = 8
NA_COLS = 16
DIL_PATTERNS = ((128, 1), (512, 4), (2048, 16))
DIL_HALF = 64
T5_BUCKETS = 32
T5_MAX_DIST = 1024
RMS_EPS = 1e-6
NEG_INF = -1e30
IN_COLS = 10 * W_BRANCH
MIX_WIDTH = 3 * W_BRANCH

COL_XA, COL_ZA, COL_QB, COL_KB, COL_VB, COL_ZB, COL_QC, COL_KC, COL_VC, COL_ZC = (
    4 * i for i in range(10))

LOG2E = float(np.log2(np.e))
Q_SCALE_LOG2 = HEAD_DIM ** -0.5 * LOG2E

LANES = 128
CHUNK = 16
VMEM_LIMIT = 56 * 1024 * 1024


def _sigmoid(z):
    return 0.5 * (1.0 + jnp.tanh(0.5 * z))


def _silu(z):
    return z * _sigmoid(z)


def _gelu_tanh(x):
    return 0.5 * x * (1.0 + jnp.tanh(np.sqrt(2.0 / np.pi).astype(np.float32) * (x + 0.044715 * (x * x * x))))


IN_TM = 512
IN_TN = 512


def _in_proj_kernel(x_ref, g_ref, w_ref, o_ref, w_bf):
    @pl.when(pl.program_id(0) == 0)
    def _():
        for n in range(IN_COLS // IN_TN):
            w_bf[:, n * IN_TN:(n + 1) * IN_TN] = w_ref[:, n * IN_TN:(n + 1) * IN_TN].astype(BF16)

    x = x_ref[...]
    ms = jnp.mean(x * x, axis=-1, keepdims=True)
    h = (x * lax.rsqrt(ms + RMS_EPS) * g_ref[...]).astype(BF16)
    for n in range(IN_COLS // IN_TN):
        o_ref[:, n * IN_TN:(n + 1) * IN_TN] = jnp.dot(
            h, w_bf[:, n * IN_TN:(n + 1) * IN_TN], preferred_element_type=F32).astype(BF16)


def _in_proj(x2d, g, w, layer):
    rows = x2d.shape[0]
    return pl.pallas_call(
        _in_proj_kernel,
        grid=(rows // IN_TM,),
        in_specs=[
            pl.BlockSpec((IN_TM, D_MODEL), lambda i: (i, 0)),
            pl.BlockSpec((None, 1, D_MODEL), lambda i: (layer, 0, 0)),
            pl.BlockSpec((None, D_MODEL, IN_COLS), lambda i: (layer, 0, 0), pipeline_mode=pl.Buffered(1)),
        ],
        out_specs=pl.BlockSpec((IN_TM, IN_COLS), lambda i: (i, 0)),
        out_shape=jax.ShapeDtypeStruct((rows, IN_COLS), BF16),
        scratch_shapes=[pltpu.VMEM((D_MODEL, IN_COLS), BF16)],
        compiler_params=pltpu.CompilerParams(
            dimension_semantics=("arbitrary",), vmem_limit_bytes=VMEM_LIMIT),
        name="in_proj",
    )(x2d, g, w)


def _na_bias_kernel(rpb_ref, o_ref):
    lane = lax.broadcasted_iota(jnp.int32, (GRID_W, LANES), 1)
    j = lax.broadcasted_iota(jnp.int32, (GRID_W, LANES), 0)
    c = lane & (GRID_W - 1)
    col_start = jnp.clip(j - NA_COLS // 2, 0, GRID_W - NA_COLS)
    valid = (c >= col_start) & (c < col_start + NA_COLS)
    first = lane < GRID_W
    even, odd = [], []
    for a in range(2 * NA_ROWS - 1):
        vec = pltpu.roll(rpb_ref[a:a + 1, :], LANES - (NA_COLS - 1), axis=1)
        x = jnp.broadcast_to(vec, (GRID_W, LANES))
        even.append(pltpu.roll(x, 0, axis=1, stride=1, stride_axis=0))
        odd.append(pltpu.roll(x, GRID_W, axis=1, stride=1, stride_axis=0))
    for v in range(NA_ROWS):
        for i in range(NA_ROWS // 2):
            a0 = 2 * i - v + (NA_ROWS - 1)
            tile = jnp.where(first, even[a0], odd[a0 + 1])
            o_ref[v, :, i * LANES:(i + 1) * LANES] = jnp.where(valid, tile * LOG2E, NEG_INF)


def _na_bias_table(rpb):
    depth = rpb.shape[0]
    padded = jnp.pad(rpb.astype(F32), ((0, 0), (0, 0), (0, 1), (0, LANES - (2 * NA_COLS - 1))))
    nkeys = NA_ROWS * GRID_W
    return pl.pallas_call(
        _na_bias_kernel,
        grid=(depth, N_HEADS),
        in_specs=[pl.BlockSpec((None, None, 2 * NA_ROWS, LANES), lambda l, h: (l, h, 0, 0))],
        out_specs=pl.BlockSpec((None, None, NA_ROWS, GRID_W, nkeys), lambda l, h: (l, h, 0, 0, 0)),
        out_shape=jax.ShapeDtypeStruct((depth, N_HEADS, NA_ROWS, GRID_W, nkeys), F32),
        compiler_params=pltpu.CompilerParams(dimension_semantics=("arbitrary", "arbitrary")),
        name="na_bias",
    )(padded)


def _na_kernel(q_ref, k_ref, v_ref, z_ref, bias_ref, o_ref):
    seq = q_ref.shape[0]
    rows = seq // GRID_W
    half_rows = NA_ROWS // 2
    tile_keys = half_rows * GRID_W
    lane = lax.broadcasted_iota(jnp.int32, (GRID_W, LANES), 1)
    first_head = lane < HEAD_DIM
    masks = (first_head, jnp.logical_not(first_head))
    nt = (((1,), (1,)), ((), ()))
    ones = jnp.ones((tile_keys, LANES), BF16)
    row_start = lambda r: min(max(r - NA_ROWS // 2, 0), rows - NA_ROWS)

    for g0 in range(0, rows, NA_GROUP):
        group = range(g0, g0 + NA_GROUP)
        users = {}
        for r in group:
            for half in range(2):
                users.setdefault(row_start(r) + half_rows * half, []).append((r, half))
        qm = {}
        for r in group:
            qb = (q_ref[r * GRID_W:(r + 1) * GRID_W, :].astype(F32) * Q_SCALE_LOG2).astype(BF16)
            for hd in range(2):
                qm[r, hd] = jnp.where(masks[hd], qb, jnp.zeros_like(qb))
        s_half = {}
        for k, us in sorted(users.items()):
            lhs = jnp.concatenate([qm[r, hd] for r, _ in us for hd in range(2)], axis=0)
            part = lax.dot_general(lhs, k_ref[k * GRID_W:k * GRID_W + tile_keys, :], nt,
                                   preferred_element_type=F32)
            for i, (r, half) in enumerate(us):
                for hd in range(2):
                    blk = part[(2 * i + hd) * GRID_W:(2 * i + hd + 1) * GRID_W, :]
                    bias = bias_ref[hd, r - row_start(r), :, half * tile_keys:(half + 1) * tile_keys]
                    s_half[r, hd, half] = (blk + bias).astype(BF16)
        p_half = {}
        for r in group:
            for hd in range(2):
                s0, s1 = s_half[r, hd, 0], s_half[r, hd, 1]
                m = jnp.max(jnp.maximum(s0, s1), axis=-1, keepdims=True)
                p_half[r, hd, 0], p_half[r, hd, 1] = jnp.exp2(s0 - m), jnp.exp2(s1 - m)
        acc = {}
        for k, us in sorted(users.items()):
            lhs = jnp.concatenate([p_half[r, hd, half] for r, half in us for hd in range(2)], axis=0)
            vext = jnp.concatenate([v_ref[k * GRID_W:k * GRID_W + tile_keys, :], ones], axis=-1)
            part = jnp.dot(lhs, vext, preferred_element_type=F32)
            for i, (r, half) in enumerate(us):
                for hd in range(2):
                    blk = part[(2 * i + hd) * GRID_W:(2 * i + hd + 1) * GRID_W, :]
                    acc[r, hd] = blk if (r, hd) not in acc else acc[r, hd] + blk
        for r in group:
            y = jnp.where(first_head, acc[r, 0][:, :LANES], acc[r, 1][:, :LANES])
            d = jnp.where(first_head, acc[r, 0][:, LANES:], acc[r, 1][:, LANES:])
            z = z_ref[r * GRID_W:(r + 1) * GRID_W, :].astype(F32)
            o_ref[r * GRID_W:(r + 1) * GRID_W, :] = (y * (1.0 / d) * _silu(z)).astype(BF16)


NA_GROUP = 16


def _na(proj, bias, layer, batch, seq):
    def col(c0):
        return pl.BlockSpec((seq, LANES), lambda b, p, c0=c0: (b, c0 + p))

    return pl.pallas_call(
        _na_kernel,
        grid=(batch, N_HEAD_PAIRS),
        in_specs=[
            col(COL_QB), col(COL_KB), col(COL_VB), col(COL_ZB),
            pl.BlockSpec((None, 2, NA_ROWS, GRID_W, NA_ROWS * GRID_W), lambda b, p: (layer, p, 0, 0, 0)),
        ],
        out_specs=pl.BlockSpec((seq, LANES), lambda b, p: (b, p)),
        out_shape=jax.ShapeDtypeStruct((batch * seq, W_BRANCH), BF16),
        compiler_params=pltpu.CompilerParams(
            dimension_semantics=("arbitrary", "arbitrary"), vmem_limit_bytes=VMEM_LIMIT),
        name="na_attn",
    )(proj, proj, proj, proj, bias)


DIL_QB = 128
DIL_KW = 256
N_VARIANTS = 3


def _t5_bucket(rel):
    nb = T5_BUCKETS // 2
    max_exact = nb // 2
    n = np.abs(rel)
    large = max_exact + (np.log(np.maximum(n, 1) / max_exact) / np.log(T5_MAX_DIST / max_exact)
                         * (nb - max_exact)).astype(np.int32)
    large = np.minimum(large, nb - 1)
    return (np.where(rel > 0, nb, 0) + np.where(n < max_exact, n, large)).astype(np.int32)


def _dil_bias_table(t5_bias):
    n_pat = len(DIL_PATTERNS)
    lane = np.arange(DIL_VEC)
    n = np.where(lane < DIL_KW, lane, lane - DIL_VEC)
    in_range = (lane < DIL_KW) | (lane > DIL_VEC - DIL_QB)
    pick = np.zeros((n_pat, N_VARIANTS, T5_BUCKETS, DIL_VEC), np.float32)
    neg = np.zeros((n_pat, N_VARIANTS, 1, DIL_VEC), np.float32)
    for pi, (_, d) in enumerate(DIL_PATTERNS):
        for var in range(N_VARIANTS):
            step = n - DIL_HALF * var
            ok = in_range & (np.abs(step) <= DIL_HALF)
            bucket = _t5_bucket(d * np.clip(step, -DIL_HALF, DIL_HALF))
            pick[pi, var, bucket[ok], lane[ok]] = 1.0
            neg[pi, var, 0, ~ok] = NEG_INF
    vecs = jnp.einsum('bh,pvbn->hpvn', t5_bias.astype(F32), pick,
                      precision=lax.Precision.HIGHEST) * LOG2E + neg[:, :, 0][None]
    return pl.pallas_call(
        _dil_bias_kernel,
        grid=(N_HEADS,),
        in_specs=[pl.BlockSpec((None, n_pat * N_VARIANTS, DIL_VEC), lambda h: (h, 0, 0))],
        out_specs=pl.BlockSpec((None, n_pat, N_VARIANTS, DIL_QB, DIL_KW), lambda h: (h, 0, 0, 0, 0)),
        out_shape=jax.ShapeDtypeStruct((N_HEADS, n_pat, N_VARIANTS, DIL_QB, DIL_KW), F32),
        compiler_params=pltpu.CompilerParams(dimension_semantics=("arbitrary",)),
        name="dil_bias",
    )(vecs.reshape(N_HEADS, n_pat * N_VARIANTS, DIL_VEC))


DIL_VEC = 512


def _dil_bias_kernel(vec_ref, o_ref):
    for pi in range(len(DIL_PATTERNS)):
        for var in range(N_VARIANTS):
            i = pi * N_VARIANTS + var
            x = jnp.broadcast_to(vec_ref[i:i + 1, :], (DIL_QB, DIL_VEC))
            o_ref[pi, var] = pltpu.roll(x, 0, axis=1, stride=1, stride_axis=0)[:, :DIL_KW]


DIL_GROUP = 16


def _dil_kernel(q_ref, k_ref, v_ref, z_ref, bias_ref, o_ref, qf, kf, vf, qg, kg, vg, qd,
                q4, k4, v4, m_s, l_s, acc_s):
    seq = q_ref.shape[0]
    conv_rows = 256
    lane = lax.broadcasted_iota(jnp.int32, (DIL_QB, LANES), 1)
    first_head = lane < HEAD_DIM
    nt = (((1,), (1,)), ((), ()))
    ones = jnp.ones((DIL_KW, LANES), BF16)

    def conv(i, carry):
        r0 = pl.multiple_of(i * conv_rows, conv_rows)
        qf[pl.ds(r0, conv_rows), :] = q_ref[pl.ds(r0, conv_rows), :].astype(F32) * Q_SCALE_LOG2
        kf[pl.ds(r0, conv_rows), :] = k_ref[pl.ds(r0, conv_rows), :].astype(F32)
        vf[pl.ds(r0, conv_rows), :] = v_ref[pl.ds(r0, conv_rows), :].astype(F32)
        return carry

    lax.fori_loop(0, seq // conv_rows, conv, 0, unroll=4)

    n4 = seq // 4
    n16 = seq // 16

    def to_order4(i, carry):
        dst = pl.multiple_of(i * conv_rows, conv_rows)
        src = pl.ds(dst // n4 + 4 * (dst % n4), conv_rows, stride=4)
        for a, b, c in ((qf, qg, q4), (kf, kg, k4), (vf, vg, v4)):
            part = a[src, :]
            b[pl.ds(dst, conv_rows), :] = part
            c[pl.ds(dst, conv_rows), :] = part.astype(BF16)
        return carry

    lax.fori_loop(0, seq // conv_rows, to_order4, 0, unroll=True)

    def window(src, res, start, size, n_sub):
        if n_sub == n16:
            rows = pl.ds((res % 4) * n4 + res // 4 + 4 * start, size, stride=4)
            return src[rows, :].astype(BF16)
        return src[pl.ds(pl.multiple_of(res * n_sub + start, DIL_HALF), size), :]

    order4_state = (qf, kf, vf)
    token_state = (m_s, l_s, acc_s)
    plan = ((2, 16, (qg, kg, vg), None, order4_state),
            (1, 4, (q4, k4, v4), order4_state, token_state),
            (0, 1, (qd, k_ref, v_ref), token_state, token_state))

    for step, (pi, d, (q_src, k_src, v_src), st_in, st_out) in enumerate(plan):
        n_sub = seq // d
        nblk = n_sub // DIL_QB

        if step == 2:
            def scaled_q(i, carry):
                r0 = pl.multiple_of(i * conv_rows, conv_rows)
                qd[pl.ds(r0, conv_rows), :] = (
                    q_ref[pl.ds(r0, conv_rows), :].astype(F32) * Q_SCALE_LOG2).astype(BF16)
                return carry

            lax.fori_loop(0, seq // conv_rows, scaled_q, 0, unroll=True)

        def rows_out(res, i0, step=step):
            if step == 0:
                return pl.ds((res % 4) * n4 + res // 4 + 4 * i0, DIL_QB, stride=4)
            if step == 1:
                return pl.ds(res + 4 * i0, DIL_QB, stride=4)
            return pl.ds(pl.multiple_of(i0, DIL_HALF), DIL_QB)

        n_group = DIL_GROUP

        def group(gi, carry, pi=pi, n_sub=n_sub, nblk=nblk, q_src=q_src, k_src=k_src, v_src=v_src,
                  st_in=st_in, st_out=st_out, rows_out=rows_out, n_group=n_group):
            units = []
            for i in range(n_group):
                u = gi * n_group + i
                res = u // nblk
                i0 = (u % nblk) * DIL_QB
                ks = jnp.clip(i0 - DIL_HALF, 0, n_sub - DIL_KW)
                units.append((res, i0, ks, (i0 - ks) // DIL_HALF))
            s_tiles, m_tiles = [], []
            for res, i0, ks, var in units:
                qb = window(q_src, res, i0, DIL_QB, n_sub)
                kw = window(k_src, res, ks, DIL_KW, n_sub)
                for hd in range(2):
                    keep = first_head if hd == 0 else jnp.logical_not(first_head)
                    qm = jnp.where(keep, qb, jnp.zeros_like(qb))
                    s = lax.dot_general(qm, kw, nt, preferred_element_type=F32) + bias_ref[hd, pi, var]
                    s = s.astype(BF16)
                    s_tiles.append(s)
                    m_tiles.append(jnp.max(s, axis=-1, keepdims=True))
            p_tiles = [jnp.exp2(s - m) for s, m in zip(s_tiles, m_tiles)]
            for i, (res, i0, ks, var) in enumerate(units):
                vw = window(v_src, res, ks, DIL_KW, n_sub)
                vext = jnp.concatenate([vw, ones], axis=-1)
                r0 = jnp.dot(p_tiles[2 * i], vext, preferred_element_type=F32)
                r1 = jnp.dot(p_tiles[2 * i + 1], vext, preferred_element_type=F32)
                m_cur = jnp.where(first_head, m_tiles[2 * i].astype(F32), m_tiles[2 * i + 1].astype(F32))
                l_cur = jnp.where(first_head, r0[:, LANES:], r1[:, LANES:])
                acc_cur = jnp.where(first_head, r0[:, :LANES], r1[:, :LANES])
                dst = rows_out(res, i0)
                if st_in is None:
                    m_new, l_new, acc_new = m_cur, l_cur, acc_cur
                else:
                    src = pl.ds(pl.multiple_of(res * n_sub + i0, DIL_HALF), DIL_QB)
                    m_old = st_in[0][src, :]
                    m_new = jnp.maximum(m_old, m_cur)
                    a_old = jnp.exp2(m_old - m_new)
                    a_cur = jnp.exp2(m_cur - m_new)
                    l_new = a_old * st_in[1][src, :] + a_cur * l_cur
                    acc_new = a_old * st_in[2][src, :] + a_cur * acc_cur
                st_out[0][dst, :] = m_new
                st_out[1][dst, :] = l_new
                st_out[2][dst, :] = acc_new
            return carry

        lax.fori_loop(0, seq // DIL_QB // n_group, group, 0)

    def fin(i, carry):
        r0 = pl.multiple_of(i * conv_rows, conv_rows)
        z = z_ref[pl.ds(r0, conv_rows), :].astype(F32)
        y = acc_s[pl.ds(r0, conv_rows), :] * (1.0 / l_s[pl.ds(r0, conv_rows), :])
        o_ref[pl.ds(r0, conv_rows), :] = (y * _silu(z)).astype(BF16)
        return carry

    lax.fori_loop(0, seq // conv_rows, fin, 0, unroll=4)


def _dil(proj, bias, batch, seq):
    def col(c0):
        return pl.BlockSpec((seq, LANES), lambda b, p, c0=c0: (b, c0 + p))

    scratch = ([pltpu.VMEM((seq, LANES), F32) for _ in range(6)]
               + [pltpu.VMEM((seq, LANES), BF16) for _ in range(4)]
               + [pltpu.VMEM((seq, LANES), F32) for _ in range(3)])
    return pl.pallas_call(
        _dil_kernel,
        grid=(batch, N_HEAD_PAIRS),
        in_specs=[
            col(COL_QC), col(COL_KC), col(COL_VC), col(COL_ZC),
            pl.BlockSpec((2, len(DIL_PATTERNS), N_VARIANTS, DIL_QB, DIL_KW),
                         lambda b, p: (p, 0, 0, 0, 0)),
        ],
        out_specs=pl.BlockSpec((seq, LANES), lambda b, p: (b, p)),
        out_shape=jax.ShapeDtypeStruct((batch * seq, W_BRANCH), BF16),
        scratch_shapes=scratch,
        compiler_params=pltpu.CompilerParams(
            dimension_semantics=("arbitrary", "arbitrary"), vmem_limit_bytes=VMEM_LIMIT),
        name="dil_attn",
    )(proj, proj, proj, proj, bias)


N_POW = CHUNK + 1
PREP_GROUPS = 8


def _ssm_prep_kernel(lre_ref, lim_ref, ldt_ref, btre_ref, btim_ref, cre_ref, cim_ref,
                     toep_ref, win_ref, wout_ref, decay_ref):
    n = CHUNK * SSM_GROUP
    lane = lax.broadcasted_iota(jnp.int32, (SSM_GROUP, LANES), 1)
    lo = lane < SSM_STATE
    lane_n = lax.broadcasted_iota(jnp.int32, (SSM_GROUP, n), 1)
    nt = (((1,), (1,)), ((), ()))
    pair = lambda x, y: jnp.where(lo, x, y)

    swap = lambda x: pltpu.roll(x, SSM_STATE, axis=1)

    for j in range(PREP_GROUPS):
        xs, ys = {}, {}
        lam_re = lre_ref[0, j]
        lam_im = lim_ref[0, j]
        dt = jnp.exp(ldt_ref[0, j])
        mag = jnp.exp(lam_re * dt)
        lb_re = mag * jnp.cos(lam_im * dt)
        lb_im = mag * jnp.sin(lam_im * dt)
        n_re = lb_re - 1.0
        den = lam_re * lam_re + lam_im * lam_im
        q_re = (n_re * lam_re + lb_im * lam_im) / den
        q_im = (lb_im * lam_re - n_re * lam_im) / den
        bt_re = btre_ref[0, j]
        bt_im = btim_ref[0, j]
        bb_re = q_re * bt_re - q_im * bt_im
        bb_im = q_re * bt_im + q_im * bt_re
        c_re = cre_ref[0, j]
        c_im = cim_ref[0, j]
        pw_re = jnp.ones_like(lb_re)
        pw_im = jnp.zeros_like(lb_re)
        for k in range(N_POW):
            xs[k] = (pw_re * bb_re - pw_im * bb_im, pw_re * bb_im + pw_im * bb_re)
            ys[k] = (pw_re * c_re - pw_im * c_im, pw_re * c_im + pw_im * c_re)
            pw_last = (pw_re, pw_im)
            pw_re, pw_im = pw_re * lb_re - pw_im * lb_im, pw_re * lb_im + pw_im * lb_re

        step = [GROUPS_PER_SLAB * (a // GROUPS_PER_SLAB) + (a % GROUPS_PER_SLAB - j) % GROUPS_PER_SLAB
                for a in range(CHUNK)]
        rows = lambda a: slice(a * SSM_GROUP, (a + 1) * SSM_GROUP)

        decay_ref[0, 0, j:j + 1, :] = pw_last[0][0:1]
        decay_ref[0, 1, j:j + 1, :] = pw_last[1][0:1]

        for a, s in enumerate(step):
            win_ref[0, j, rows(a), :LANES] = pair(xs[CHUNK - 1 - s][0], xs[s][0]).astype(BF16)
            win_ref[0, j, rows(a), LANES:] = pair(xs[CHUNK - 1 - s][1], xs[s][1]).astype(BF16)

        z_re = jnp.concatenate([pair(ys[t + 1][0], ys[CHUNK - t][0]) for t in step], axis=0)
        z_nim = jnp.concatenate([pair(-ys[t + 1][1], -ys[CHUNK - t][1]) for t in step], axis=0)
        wout_ref[0, j, :LANES, :] = z_re.T.astype(BF16)
        wout_ref[0, j, LANES:, :] = z_nim.T.astype(BF16)

        kf = lax.dot_general(
            pair(bb_re, -swap(bb_im)),
            jnp.concatenate([pair(ys[m][0], swap(ys[m][1])) for m in range(CHUNK)], axis=0),
            nt, precision=lax.Precision.HIGHEST, preferred_element_type=F32)
        kb = lax.dot_general(
            pair(swap(bb_re), -bb_im),
            jnp.concatenate([pair(swap(ys[CHUNK - 1 - i][0]), ys[CHUNK - 1 - i][1])
                             for i in range(CHUNK)], axis=0),
            nt, precision=lax.Precision.HIGHEST, preferred_element_type=F32)
        for a, s in enumerate(step):
            fwd = pltpu.roll(kf, SSM_GROUP * s, axis=1) if s else kf
            shift_b = (n - SSM_GROUP * (CHUNK - 1 - s)) % n
            bwd = pltpu.roll(kb, shift_b, axis=1) if shift_b else kb
            t_nat = (jnp.where(lane_n >= SSM_GROUP * s, fwd, 0.0)
                     + jnp.where(lane_n < SSM_GROUP * (s + 1), bwd, 0.0))
            for h in range(2):
                half = t_nat[:, h * LANES:(h + 1) * LANES]
                if j:
                    half = pltpu.roll(half, SSM_GROUP * j, axis=1)
                toep_ref[0, j, rows(a), h * LANES:(h + 1) * LANES] = half.astype(BF16)


def _ssm_prep(lam_re, lam_im, log_dt, b_re, b_im, c_re, c_im):
    depth = lam_re.shape[0]
    G, P, C = SSM_GROUPS, SSM_STATE, SSM_GROUP
    n = CHUNK * C
    gb = PREP_GROUPS
    both = lambda a: jnp.concatenate([a[:, 0], a[:, 1]], axis=-1).astype(F32)
    rep = lambda a: jnp.broadcast_to(both(a)[:, :, None, :], (depth, G, C, LANES))
    ldt = rep(jnp.broadcast_to(log_dt[..., None], (depth, 2, G, P)))
    bt = lambda a: both(a.transpose(0, 1, 2, 4, 3))
    spec_in = pl.BlockSpec((1, gb, C, LANES), lambda l, i: (l, i, 0, 0))
    spec_w = pl.BlockSpec((1, gb, n, n), lambda l, i: (l, i, 0, 0))
    shp_w = jax.ShapeDtypeStruct((depth, G, n, n), BF16)
    return pl.pallas_call(
        _ssm_prep_kernel,
        grid=(depth, G // gb),
        in_specs=[spec_in] * 7,
        out_specs=[spec_w, spec_w, spec_w, pl.BlockSpec((1, 2, gb, LANES), lambda l, i: (l, 0, i, 0))],
        out_shape=[shp_w, shp_w, shp_w, jax.ShapeDtypeStruct((depth, 2, G, LANES), F32)],
        compiler_params=pltpu.CompilerParams(
            dimension_semantics=("arbitrary", "arbitrary"), vmem_limit_bytes=VMEM_LIMIT),
        name="ssm_prep",
    )(rep(lam_re), rep(lam_im), ldt, bt(b_re), bt(b_im), both(c_re), both(c_im))


GROUPS_PER_SLAB = LANES // SSM_GROUP
N_SLABS = W_BRANCH // LANES
RELAYOUT_ROWS = 32


def _ssm_kernel(xa_ref, toep_ref, win_ref, wout_ref, decay_ref, d_ref, o_ref, xf, u_s, st, hs, yq):
    seq = xa_ref.shape[0]
    nchunk = seq // CHUNK
    nrb = nchunk // RELAYOUT_ROWS
    gps = GROUPS_PER_SLAB
    lane_rb = lax.broadcasted_iota(jnp.int32, (RELAYOUT_ROWS, LANES), 1)
    seg_masks = [(lane_rb >= SSM_GROUP * sg) & (lane_rb < SSM_GROUP * (sg + 1)) for sg in range(gps)]
    conv_rows = 512

    for q in range(N_SLABS):
        c_lo = q * LANES

        def conv(i, carry, c_lo=c_lo):
            r0 = pl.multiple_of(i * conv_rows, conv_rows)
            xf[pl.ds(r0, conv_rows), :] = xa_ref[pl.ds(r0, conv_rows), c_lo:c_lo + LANES].astype(F32)
            return carry

        lax.fori_loop(0, seq // conv_rows, conv, 0)

        for hh in range(2):
            def fwd_relayout(rb, carry, hh=hh):
                c0 = pl.multiple_of(rb * RELAYOUT_ROWS, RELAYOUT_ROWS)
                rolled = []
                for r in range(gps):
                    xs = xf[pl.ds(CHUNK * c0 + gps * hh + r, RELAYOUT_ROWS, stride=CHUNK), :]
                    rolled.append((xs if r == 0 else pltpu.roll(xs, SSM_GROUP * r, axis=1)).astype(BF16))
                for j in range(gps):
                    out = rolled[(0 - j) % gps]
                    for sg in range(1, gps):
                        out = jnp.where(seg_masks[sg], rolled[(sg - j) % gps], out)
                    u_s[j, pl.ds(c0, RELAYOUT_ROWS), hh * LANES:(hh + 1) * LANES] = out
                return carry

            lax.fori_loop(0, nrb, fwd_relayout, 0, unroll=True)

        for j in range(gps):
            s_in = jnp.dot(u_s[j], win_ref[gps * q + j], preferred_element_type=F32)
            st[0, pl.ds(j, nchunk, stride=gps), :] = s_in[:, :LANES]
            st[1, pl.ds(j, nchunk, stride=gps), :] = s_in[:, LANES:]

        g_lo = gps * q
        a_re = decay_ref[0, g_lo:g_lo + gps, :]
        a_im = decay_ref[1, g_lo:g_lo + gps, :]
        lo, hi = slice(0, SSM_STATE), slice(SSM_STATE, LANES)

        def scan(i, carry):
            hrf, hif, hrb, hib = carry
            rf = pl.ds(pl.multiple_of(i * gps, gps), gps)
            rb_ = pl.ds(pl.multiple_of((nchunk - 1 - i) * gps, gps), gps)
            hs[0, rf, lo] = hrf[:, lo]
            hs[1, rf, lo] = hif[:, lo]
            hs[0, rb_, hi] = hrb[:, hi]
            hs[1, rb_, hi] = hib[:, hi]
            return (a_re * hrf - a_im * hif + st[0, rf, :], a_re * hif + a_im * hrf + st[1, rf, :],
                    a_re * hrb - a_im * hib + st[0, rb_, :], a_re * hib + a_im * hrb + st[1, rb_, :])

        zero = jnp.zeros((gps, LANES), F32)
        lax.fori_loop(0, nchunk, scan, (zero, zero, zero, zero), unroll=8)

        for j in range(gps):
            g = gps * q + j
            h_in = jnp.concatenate([hs[0, pl.ds(j, nchunk, stride=gps), :].astype(BF16),
                                    hs[1, pl.ds(j, nchunk, stride=gps), :].astype(BF16)], axis=-1)
            yq[j] = (jnp.dot(u_s[j], toep_ref[g], preferred_element_type=F32)
                     + jnp.dot(h_in, wout_ref[g], preferred_element_type=F32))

        d_row = d_ref[:, c_lo:c_lo + LANES]
        for hh in range(2):
            def bwd_relayout(rb, carry, hh=hh, d_row=d_row, q=q):
                c0 = pl.multiple_of(rb * RELAYOUT_ROWS, RELAYOUT_ROWS)
                ys = [yq[j, pl.ds(c0, RELAYOUT_ROWS), hh * LANES:(hh + 1) * LANES] for j in range(gps)]
                for r in range(gps):
                    merged = ys[(0 - r) % gps]
                    for sg in range(1, gps):
                        merged = jnp.where(seg_masks[sg], ys[(sg - r) % gps], merged)
                    if r:
                        merged = pltpu.roll(merged, LANES - SSM_GROUP * r, axis=1)
                    tok = pl.ds(CHUNK * c0 + gps * hh + r, RELAYOUT_ROWS, stride=CHUNK)
                    o_ref[q, tok, :] = merged + d_row * xf[tok, :]
                return carry

            lax.fori_loop(0, nrb, bwd_relayout, 0, unroll=True)


def _ssm(proj, toep, w_in, w_out, decay, d_skip, layer, batch, seq):
    nchunk = seq // CHUNK
    n = CHUNK * SSM_GROUP
    wspec = pl.BlockSpec((None, SSM_GROUPS, n, n), lambda b: (layer, 0, 0, 0), pipeline_mode=pl.Buffered(1))
    return pl.pallas_call(
        _ssm_kernel,
        grid=(batch,),
        in_specs=[
            pl.BlockSpec((seq, W_BRANCH), lambda b: (b, COL_XA // N_SLABS)),
            wspec, wspec, wspec,
            pl.BlockSpec((None, 2, SSM_GROUPS, LANES), lambda b: (layer, 0, 0, 0)),
            pl.BlockSpec((None, 1, W_BRANCH), lambda b: (layer, 0, 0)),
        ],
        out_specs=pl.BlockSpec((N_SLABS, seq, LANES), lambda b: (0, b, 0)),
        out_shape=jax.ShapeDtypeStruct((N_SLABS, batch * seq, LANES), F32),
        scratch_shapes=[
            pltpu.VMEM((seq, LANES), F32),
            pltpu.VMEM((GROUPS_PER_SLAB, nchunk, n), BF16),
            pltpu.VMEM((2, nchunk * GROUPS_PER_SLAB, LANES), F32),
            pltpu.VMEM((2, nchunk * GROUPS_PER_SLAB, LANES), F32),
            pltpu.VMEM((GROUPS_PER_SLAB, nchunk, n), F32),
        ],
        compiler_params=pltpu.CompilerParams(
            dimension_semantics=("arbitrary",), vmem_limit_bytes=VMEM_LIMIT),
        name="ssm_mix",
    )(proj, toep, w_in, w_out, decay, d_skip)


OUT_TM = 1024


def _out_proj_kernel(ya_ref, za_ref, yb_ref, yc_ref, x_ref, w_ref, gw_ref, gb_ref, fg_ref, o_ref,
                     w_bf, gw_bf, *, final):
    @pl.when(pl.program_id(0) == 0)
    def _():
        for n in range(MIX_WIDTH // W_BRANCH):
            w_bf[n * W_BRANCH:(n + 1) * W_BRANCH, :] = w_ref[n * W_BRANCH:(n + 1) * W_BRANCH, :].astype(BF16)
        gw_bf[...] = gw_ref[...].astype(BF16)

    y = jnp.concatenate([ya_ref[i] for i in range(N_SLABS)], axis=-1)
    g = _gelu_tanh(y)
    gate = jnp.dot(g.astype(BF16), gw_bf[...], preferred_element_type=F32) + gb_ref[...]
    ya = g * _sigmoid(gate) * _silu(za_ref[...].astype(F32))
    delta = (jnp.dot(ya.astype(BF16), w_bf[:W_BRANCH, :], preferred_element_type=F32)
             + jnp.dot(yb_ref[...], w_bf[W_BRANCH:2 * W_BRANCH, :], preferred_element_type=F32)
             + jnp.dot(yc_ref[...], w_bf[2 * W_BRANCH:, :], preferred_element_type=F32))
    x = x_ref[...] + delta
    if final:
        ms = jnp.mean(x * x, axis=-1, keepdims=True)
        x = x * lax.rsqrt(ms + RMS_EPS) * fg_ref[...]
    o_ref[...] = x


def _out_proj(ya_pre, proj, yb, yc, x2d, w, glu_w, glu_b, final_g, layer, final):
    rows = x2d.shape[0]
    row_blk = lambda width: pl.BlockSpec((OUT_TM, width), lambda i: (i, 0))
    const = lambda shape: pl.BlockSpec((None,) + shape, lambda i: (layer,) + (0,) * len(shape),
                                       pipeline_mode=pl.Buffered(1))
    return pl.pallas_call(
        functools.partial(_out_proj_kernel, final=final),
        grid=(rows // OUT_TM,),
        in_specs=[
            pl.BlockSpec((N_SLABS, OUT_TM, LANES), lambda i: (0, i, 0)),
            pl.BlockSpec((OUT_TM, W_BRANCH), lambda i: (i, COL_ZA // N_SLABS)),
            row_blk(W_BRANCH), row_blk(W_BRANCH), row_blk(D_MODEL),
            const((MIX_WIDTH, D_MODEL)), const((W_BRANCH, W_BRANCH)),
            const((1, W_BRANCH)),
            pl.BlockSpec((None, 1, D_MODEL), lambda i: (0, 0, 0), pipeline_mode=pl.Buffered(1)),
        ],
        out_specs=row_blk(D_MODEL),
        out_shape=jax.ShapeDtypeStruct((rows, D_MODEL), F32),
        scratch_shapes=[pltpu.VMEM((MIX_WIDTH, D_MODEL), BF16), pltpu.VMEM((W_BRANCH, W_BRANCH), BF16)],
        compiler_params=pltpu.CompilerParams(
            dimension_semantics=("arbitrary",), vmem_limit_bytes=VMEM_LIMIT),
        name="out_proj_final" if final else "out_proj",
    )(ya_pre, proj, yb, yc, x2d, w, glu_w, glu_b, final_g)


def kernel(x, norm_g, w_in, w_out, ssm_lam_re, ssm_lam_im, ssm_log_dt, ssm_b_re, ssm_b_im, ssm_c_re,
           ssm_c_im, ssm_d, glu_w, glu_b, na_rpb, t5_bias, final_g):
    batch, seq, _ = x.shape
    depth = w_in.shape[0]
    x2d = x.astype(F32).reshape(batch * seq, D_MODEL)
    toep, s_in, s_out, decay = _ssm_prep(ssm_lam_re, ssm_lam_im, ssm_log_dt, ssm_b_re, ssm_b_im,
                                         ssm_c_re, ssm_c_im)
    dil_bias = _dil_bias_table(t5_bias)
    na_bias = _na_bias_table(na_rpb)
    w_in_f, w_out_f, glu_w_f = w_in.astype(F32), w_out.astype(F32), glu_w.astype(F32)
    row3 = lambda a: a.astype(F32).reshape(a.shape[0], 1, a.shape[1])
    norm_g3, ssm_d3, glu_b3 = row3(norm_g), row3(ssm_d), row3(glu_b)
    final_g3 = final_g.astype(F32).reshape(1, 1, D_MODEL)
    for l in range(depth):
        proj = _in_proj(x2d, norm_g3, w_in_f, l)
        ya_pre = _ssm(proj, toep, s_in, s_out, decay, ssm_d3, l, batch, seq)
        yb = _na(proj, na_bias, l, batch, seq)
        yc = _dil(proj, dil_bias, batch, seq)
        x2d = _out_proj(ya_pre, proj, yb, yc, x2d, w_out_f, glu_w_f, glu_b3, final_g3, l,
                        final=(l == depth - 1))
    return x2d.reshape(batch, seq, D_MODEL).astype(x.dtype)
```

```python
import functools

import numpy as np
import jax
import jax.numpy as jnp
from jax import lax
from jax.experimental import pallas as pl
from jax.experimental.pallas import tpu as pltpu

F32 = jnp.float32
BF16 = jnp.bfloat16

D_MODEL = 1024
HEAD_DIM = 64
W_BRANCH = 512
N_HEADS = W_BRANCH // HEAD_DIM
N_HEAD_PAIRS = N_HEADS // 2
SSM_GROUP = 16
SSM_GROUPS = W_BRANCH // SSM_GROUP
SSM_STATE = 64
GRID_W = 64
NA_ROWS = 8
NA_COLS = 16
DIL_PATTERNS = ((128, 1), (512, 4), (2048, 16))
DIL_HALF = 64
T5_BUCKETS = 32
T5_MAX_DIST = 1024
RMS_EPS = 1e-6
NEG_INF = -1e30
IN_COLS = 10 * W_BRANCH
MIX_WIDTH = 3 * W_BRANCH

COL_XA, COL_ZA, COL_QB, COL_KB, COL_VB, COL_ZB, COL_QC, COL_KC, COL_VC, COL_ZC = (
    4 * i for i in range(10))

LOG2E = float(np.log2(np.e))
Q_SCALE_LOG2 = HEAD_DIM ** -0.5 * LOG2E

LANES = 128
CHUNK = 16
VMEM_LIMIT = 56 * 1024 * 1024


def _sigmoid(z):
    return 0.5 * (1.0 + jnp.tanh(0.5 * z))


def _silu(z):
    return z * _sigmoid(z)


def _gelu_tanh(x):
    return 0.5 * x * (1.0 + jnp.tanh(np.sqrt(2.0 / np.pi).astype(np.float32) * (x + 0.044715 * (x * x * x))))


IN_TM = 512
IN_TN = 512


def _in_proj_kernel(x_ref, g_ref, w_ref, o_ref, w_bf):
    @pl.when(pl.program_id(0) == 0)
    def _():
        for n in range(IN_COLS // IN_TN):
            w_bf[:, n * IN_TN:(n + 1) * IN_TN] = w_ref[:, n * IN_TN:(n + 1) * IN_TN].astype(BF16)

    x = x_ref[...]
    ms = jnp.mean(x * x, axis=-1, keepdims=True)
    h = (x * lax.rsqrt(ms + RMS_EPS) * g_ref[...]).astype(BF16)
    for n in range(IN_COLS // IN_TN):
        o_ref[:, n * IN_TN:(n + 1) * IN_TN] = jnp.dot(
            h, w_bf[:, n * IN_TN:(n + 1) * IN_TN], preferred_element_type=F32).astype(BF16)


def _in_proj(x2d, g, w, layer):
    rows = x2d.shape[0]
    return pl.pallas_call(
        _in_proj_kernel,
        grid=(rows // IN_TM,),
        in_specs=[
            pl.BlockSpec((IN_TM, D_MODEL), lambda i: (i, 0)),
            pl.BlockSpec((None, 1, D_MODEL), lambda i: (layer, 0, 0)),
            pl.BlockSpec((None, D_MODEL, IN_COLS), lambda i: (layer, 0, 0), pipeline_mode=pl.Buffered(1)),
        ],
        out_specs=pl.BlockSpec((IN_TM, IN_COLS), lambda i: (i, 0)),
        out_shape=jax.ShapeDtypeStruct((rows, IN_COLS), BF16),
        scratch_shapes=[pltpu.VMEM((D_MODEL, IN_COLS), BF16)],
        compiler_params=pltpu.CompilerParams(
            dimension_semantics=("arbitrary",), vmem_limit_bytes=VMEM_LIMIT),
        name="in_proj",
    )(x2d, g, w)


def _na_bias_kernel(rpb_ref, o_ref):
    lane = lax.broadcasted_iota(jnp.int32, (GRID_W, LANES), 1)
    j = lax.broadcasted_iota(jnp.int32, (GRID_W, LANES), 0)
    c = lane & (GRID_W - 1)
    col_start = jnp.clip(j - NA_COLS // 2, 0, GRID_W - NA_COLS)
    valid = (c >= col_start) & (c < col_start + NA_COLS)
    first = lane < GRID_W
    even, odd = [], []
    for a in range(2 * NA_ROWS - 1):
        vec = pltpu.roll(rpb_ref[a:a + 1, :], LANES - (NA_COLS - 1), axis=1)
        x = jnp.broadcast_to(vec, (GRID_W, LANES))
        even.append(pltpu.roll(x, 0, axis=1, stride=1, stride_axis=0))
        odd.append(pltpu.roll(x, GRID_W, axis=1, stride=1, stride_axis=0))
    for v in range(NA_ROWS):
        for i in range(NA_ROWS // 2):
            a0 = 2 * i - v + (NA_ROWS - 1)
            tile = jnp.where(first, even[a0], odd[a0 + 1])
            o_ref[v, :, i * LANES:(i + 1) * LANES] = jnp.where(valid, tile * LOG2E, NEG_INF)


def _na_bias_table(rpb):
    depth = rpb.shape[0]
    padded = jnp.pad(rpb.astype(F32), ((0, 0), (0, 0), (0, 1), (0, LANES - (2 * NA_COLS - 1))))
    nkeys = NA_ROWS * GRID_W
    return pl.pallas_call(
        _na_bias_kernel,
        grid=(depth, N_HEADS),
        in_specs=[pl.BlockSpec((None, None, 2 * NA_ROWS, LANES), lambda l, h: (l, h, 0, 0))],
        out_specs=pl.BlockSpec((None, None, NA_ROWS, GRID_W, nkeys), lambda l, h: (l, h, 0, 0, 0)),
        out_shape=jax.ShapeDtypeStruct((depth, N_HEADS, NA_ROWS, GRID_W, nkeys), F32),
        compiler_params=pltpu.CompilerParams(dimension_semantics=("arbitrary", "arbitrary")),
        name="na_bias",
    )(padded)


def _na_kernel(q_ref, k_ref, v_ref, z_ref, bias_ref, o_ref):
    seq = q_ref.shape[0]
    rows = seq // GRID_W
    half_rows = NA_ROWS // 2
    tile_keys = half_rows * GRID_W
    lane = lax.broadcasted_iota(jnp.int32, (GRID_W, LANES), 1)
    first_head = lane < HEAD_DIM
    masks = (first_head, jnp.logical_not(first_head))
    nt = (((1,), (1,)), ((), ()))
    ones = jnp.ones((tile_keys, LANES), BF16)
    row_start = lambda r: min(max(r - NA_ROWS // 2, 0), rows - NA_ROWS)

    for g0 in range(0, rows, NA_GROUP):
        group = range(g0, g0 + NA_GROUP)
        users = {}
        for r in group:
            for half in range(2):
                users.setdefault(row_start(r) + half_rows * half, []).append((r, half))
        qm = {}
        for r in group:
            qb = (q_ref[r * GRID_W:(r + 1) * GRID_W, :].astype(F32) * Q_SCALE_LOG2).astype(BF16)
            for hd in range(2):
                qm[r, hd] = jnp.where(masks[hd], qb, jnp.zeros_like(qb))
        s_half = {}
        for k, us in sorted(users.items()):
            lhs = jnp.concatenate([qm[r, hd] for r, _ in us for hd in range(2)], axis=0)
            part = lax.dot_general(lhs, k_ref[k * GRID_W:k * GRID_W + tile_keys, :], nt,
                                   preferred_element_type=F32)
            for i, (r, half) in enumerate(us):
                for hd in range(2):
                    blk = part[(2 * i + hd) * GRID_W:(2 * i + hd + 1) * GRID_W, :]
                    bias = bias_ref[hd, r - row_start(r), :, half * tile_keys:(half + 1) * tile_keys]
                    s_half[r, hd, half] = blk + bias
        p_half = {}
        for r in group:
            for hd in range(2):
                s0, s1 = s_half[r, hd, 0], s_half[r, hd, 1]
                m = jnp.max(jnp.maximum(s0, s1), axis=-1, keepdims=True)
                p_half[r, hd, 0] = jnp.exp2((s0 - m).astype(BF16))
                p_half[r, hd, 1] = jnp.exp2((s1 - m).astype(BF16))
        acc = {}
        for k, us in sorted(users.items()):
            lhs = jnp.concatenate([p_half[r, hd, half] for r, half in us for hd in range(2)], axis=0)
            vext = jnp.concatenate([v_ref[k * GRID_W:k * GRID_W + tile_keys, :], ones], axis=-1)
            part = jnp.dot(lhs, vext, preferred_element_type=F32)
            for i, (r, half) in enumerate(us):
                for hd in range(2):
                    blk = part[(2 * i + hd) * GRID_W:(2 * i + hd + 1) * GRID_W, :]
                    acc[r, hd] = blk if (r, hd) not in acc else acc[r, hd] + blk
        for r in group:
            y = jnp.where(first_head, acc[r, 0][:, :LANES], acc[r, 1][:, :LANES])
            d = jnp.where(first_head, acc[r, 0][:, LANES:], acc[r, 1][:, LANES:])
            z = z_ref[r * GRID_W:(r + 1) * GRID_W, :].astype(F32)
            o_ref[r * GRID_W:(r + 1) * GRID_W, :] = (y * (1.0 / d) * _silu(z)).astype(BF16)


NA_GROUP = 16


def _na(proj, bias, layer, batch, seq):
    def col(c0):
        return pl.BlockSpec((seq, LANES), lambda b, p, c0=c0: (b, c0 + p))

    return pl.pallas_call(
        _na_kernel,
        grid=(batch, N_HEAD_PAIRS),
        in_specs=[
            col(COL_QB), col(COL_KB), col(COL_VB), col(COL_ZB),
            pl.BlockSpec((None, 2, NA_ROWS, GRID_W, NA_ROWS * GRID_W), lambda b, p: (layer, p, 0, 0, 0)),
        ],
        out_specs=pl.BlockSpec((seq, LANES), lambda b, p: (b, p)),
        out_shape=jax.ShapeDtypeStruct((batch * seq, W_BRANCH), BF16),
        compiler_params=pltpu.CompilerParams(
            dimension_semantics=("arbitrary", "arbitrary"), vmem_limit_bytes=VMEM_LIMIT),
        name="na_attn",
    )(proj, proj, proj, proj, bias)


DIL_QB = 128
DIL_KW = 256
N_VARIANTS = 3


def _t5_bucket(rel):
    nb = T5_BUCKETS // 2
    max_exact = nb // 2
    n = np.abs(rel)
    large = max_exact + (np.log(np.maximum(n, 1) / max_exact) / np.log(T5_MAX_DIST / max_exact)
                         * (nb - max_exact)).astype(np.int32)
    large = np.minimum(large, nb - 1)
    return (np.where(rel > 0, nb, 0) + np.where(n < max_exact, n, large)).astype(np.int32)


def _dil_bias_table(t5_bias):
    n_pat = len(DIL_PATTERNS)
    lane = np.arange(DIL_VEC)
    n = np.where(lane < DIL_KW, lane, lane - DIL_VEC)
    in_range = (lane < DIL_KW) | (lane > DIL_VEC - DIL_QB)
    pick = np.zeros((n_pat, N_VARIANTS, T5_BUCKETS, DIL_VEC), np.float32)
    neg = np.zeros((n_pat, N_VARIANTS, 1, DIL_VEC), np.float32)
    for pi, (_, d) in enumerate(DIL_PATTERNS):
        for var in range(N_VARIANTS):
            step = n - DIL_HALF * var
            ok = in_range & (np.abs(step) <= DIL_HALF)
            bucket = _t5_bucket(d * np.clip(step, -DIL_HALF, DIL_HALF))
            pick[pi, var, bucket[ok], lane[ok]] = 1.0
            neg[pi, var, 0, ~ok] = NEG_INF
    vecs = jnp.einsum('bh,pvbn->hpvn', t5_bias.astype(F32), pick,
                      precision=lax.Precision.HIGHEST) * LOG2E + neg[:, :, 0][None]
    return pl.pallas_call(
        _dil_bias_kernel,
        grid=(N_HEADS,),
        in_specs=[pl.BlockSpec((None, n_pat * N_VARIANTS, DIL_VEC), lambda h: (h, 0, 0))],
        out_specs=pl.BlockSpec((None, n_pat, N_VARIANTS, DIL_QB, DIL_KW), lambda h: (h, 0, 0, 0, 0)),
        out_shape=jax.ShapeDtypeStruct((N_HEADS, n_pat, N_VARIANTS, DIL_QB, DIL_KW), F32),
        compiler_params=pltpu.CompilerParams(dimension_semantics=("arbitrary",)),
        name="dil_bias",
    )(vecs.reshape(N_HEADS, n_pat * N_VARIANTS, DIL_VEC))


DIL_VEC = 512


def _dil_bias_kernel(vec_ref, o_ref):
    for pi in range(len(DIL_PATTERNS)):
        for var in range(N_VARIANTS):
            i = pi * N_VARIANTS + var
            x = jnp.broadcast_to(vec_ref[i:i + 1, :], (DIL_QB, DIL_VEC))
            o_ref[pi, var] = pltpu.roll(x, 0, axis=1, stride=1, stride_axis=0)[:, :DIL_KW]


DIL_GROUP = 16


def _dil_kernel(q_ref, k_ref, v_ref, z_ref, bias_ref, o_ref, qf, kf, vf, qg, kg, vg, qd,
                q4, k4, v4, m_s, l_s, acc_s):
    seq = q_ref.shape[0]
    conv_rows = 256
    lane = lax.broadcasted_iota(jnp.int32, (DIL_QB, LANES), 1)
    first_head = lane < HEAD_DIM
    nt = (((1,), (1,)), ((), ()))
    ones = jnp.ones((DIL_KW, LANES), BF16)

    def conv(i, carry):
        r0 = pl.multiple_of(i * conv_rows, conv_rows)
        qf[pl.ds(r0, conv_rows), :] = q_ref[pl.ds(r0, conv_rows), :].astype(F32) * Q_SCALE_LOG2
        kf[pl.ds(r0, conv_rows), :] = k_ref[pl.ds(r0, conv_rows), :].astype(F32)
        vf[pl.ds(r0, conv_rows), :] = v_ref[pl.ds(r0, conv_rows), :].astype(F32)
        return carry

    lax.fori_loop(0, seq // conv_rows, conv, 0, unroll=4)

    n4 = seq // 4
    n16 = seq // 16

    def to_order4(i, carry):
        dst = pl.multiple_of(i * conv_rows, conv_rows)
        src = pl.ds(dst // n4 + 4 * (dst % n4), conv_rows, stride=4)
        for a, b, c in ((qf, qg, q4), (kf, kg, k4), (vf, vg, v4)):
            part = a[src, :]
            b[pl.ds(dst, conv_rows), :] = part
            c[pl.ds(dst, conv_rows), :] = part.astype(BF16)
        return carry

    lax.fori_loop(0, seq // conv_rows, to_order4, 0, unroll=True)

    def window(src, res, start, size, n_sub):
        if n_sub == n16:
            rows = pl.ds((res % 4) * n4 + res // 4 + 4 * start, size, stride=4)
            return src[rows, :].astype(BF16)
        return src[pl.ds(pl.multiple_of(res * n_sub + start, DIL_HALF), size), :]

    order4_state = (qf, kf, vf)
    token_state = (m_s, l_s, acc_s)
    plan = ((2, 16, (qg, kg, vg), None, order4_state),
            (1, 4, (q4, k4, v4), order4_state, token_state),
            (0, 1, (qd, k_ref, v_ref), token_state, token_state))

    for step, (pi, d, (q_src, k_src, v_src), st_in, st_out) in enumerate(plan):
        n_sub = seq // d
        nblk = n_sub // DIL_QB

        if step == 2:
            def scaled_q(i, carry):
                r0 = pl.multiple_of(i * conv_rows, conv_rows)
                qd[pl.ds(r0, conv_rows), :] = (
                    q_ref[pl.ds(r0, conv_rows), :].astype(F32) * Q_SCALE_LOG2).astype(BF16)
                return carry

            lax.fori_loop(0, seq // conv_rows, scaled_q, 0, unroll=True)

        def rows_out(res, i0, step=step):
            if step == 0:
                return pl.ds((res % 4) * n4 + res // 4 + 4 * i0, DIL_QB, stride=4)
            if step == 1:
                return pl.ds(res + 4 * i0, DIL_QB, stride=4)
            return pl.ds(pl.multiple_of(i0, DIL_HALF), DIL_QB)

        n_group = DIL_GROUP

        def group(gi, carry, pi=pi, n_sub=n_sub, nblk=nblk, q_src=q_src, k_src=k_src, v_src=v_src,
                  st_in=st_in, st_out=st_out, rows_out=rows_out, n_group=n_group):
            units = []
            for i in range(n_group):
                u = gi * n_group + i
                res = u // nblk
                i0 = (u % nblk) * DIL_QB
                ks = jnp.clip(i0 - DIL_HALF, 0, n_sub - DIL_KW)
                units.append((res, i0, ks, (i0 - ks) // DIL_HALF))
            s_tiles, m_tiles = [], []
            for res, i0, ks, var in units:
                qb = window(q_src, res, i0, DIL_QB, n_sub)
                kw = window(k_src, res, ks, DIL_KW, n_sub)
                for hd in range(2):
                    keep = first_head if hd == 0 else jnp.logical_not(first_head)
                    qm = jnp.where(keep, qb, jnp.zeros_like(qb))
                    s = lax.dot_general(qm, kw, nt, preferred_element_type=F32) + bias_ref[hd, pi, var]
                    s_tiles.append(s)
                    m_tiles.append(jnp.max(s, axis=-1, keepdims=True))
            p_tiles = [jnp.exp2((s - m).astype(BF16)) for s, m in zip(s_tiles, m_tiles)]
            for i, (res, i0, ks, var) in enumerate(units):
                vw = window(v_src, res, ks, DIL_KW, n_sub)
                vext = jnp.concatenate([vw, ones], axis=-1)
                r0 = jnp.dot(p_tiles[2 * i], vext, preferred_element_type=F32)
                r1 = jnp.dot(p_tiles[2 * i + 1], vext, preferred_element_type=F32)
                m_cur = jnp.where(first_head, m_tiles[2 * i], m_tiles[2 * i + 1])
                l_cur = jnp.where(first_head, r0[:, LANES:], r1[:, LANES:])
                acc_cur = jnp.where(first_head, r0[:, :LANES], r1[:, :LANES])
                dst = rows_out(res, i0)
                if st_in is None:
                    m_new, l_new, acc_new = m_cur, l_cur, acc_cur
                else:
                    src = pl.ds(pl.multiple_of(res * n_sub + i0, DIL_HALF), DIL_QB)
                    m_old = st_in[0][src, :]
                    m_new = jnp.maximum(m_old, m_cur)
                    a_old = jnp.exp2(m_old - m_new)
                    a_cur = jnp.exp2(m_cur - m_new)
                    l_new = a_old * st_in[1][src, :] + a_cur * l_cur
                    acc_new = a_old * st_in[2][src, :] + a_cur * acc_cur
                st_out[0][dst, :] = m_new
                st_out[1][dst, :] = l_new
                st_out[2][dst, :] = acc_new
            return carry

        lax.fori_loop(0, seq // DIL_QB // n_group, group, 0)

    def fin(i, carry):
        r0 = pl.multiple_of(i * conv_rows, conv_rows)
        z = z_ref[pl.ds(r0, conv_rows), :].astype(F32)
        y = acc_s[pl.ds(r0, conv_rows), :] * (1.0 / l_s[pl.ds(r0, conv_rows), :])
        o_ref[pl.ds(r0, conv_rows), :] = (y * _silu(z)).astype(BF16)
        return carry

    lax.fori_loop(0, seq // conv_rows, fin, 0, unroll=4)


def _dil(proj, bias, batch, seq):
    def col(c0):
        return pl.BlockSpec((seq, LANES), lambda b, p, c0=c0: (b, c0 + p))

    scratch = ([pltpu.VMEM((seq, LANES), F32) for _ in range(6)]
               + [pltpu.VMEM((seq, LANES), BF16) for _ in range(4)]
               + [pltpu.VMEM((seq, LANES), F32) for _ in range(3)])
    return pl.pallas_call(
        _dil_kernel,
        grid=(batch, N_HEAD_PAIRS),
        in_specs=[
            col(COL_QC), col(COL_KC), col(COL_VC), col(COL_ZC),
            pl.BlockSpec((2, len(DIL_PATTERNS), N_VARIANTS, DIL_QB, DIL_KW),
                         lambda b, p: (p, 0, 0, 0, 0)),
        ],
        out_specs=pl.BlockSpec((seq, LANES), lambda b, p: (b, p)),
        out_shape=jax.ShapeDtypeStruct((batch * seq, W_BRANCH), BF16),
        scratch_shapes=scratch,
        compiler_params=pltpu.CompilerParams(
            dimension_semantics=("arbitrary", "arbitrary"), vmem_limit_bytes=VMEM_LIMIT),
        name="dil_attn",
    )(proj, proj, proj, proj, bias)


N_POW = CHUNK + 1
PREP_GROUPS = 8


def _ssm_prep_kernel(lre_ref, lim_ref, ldt_ref, btre_ref, btim_ref, cre_ref, cim_ref,
                     toep_ref, win_ref, wout_ref, decay_ref):
    n = CHUNK * SSM_GROUP
    lane = lax.broadcasted_iota(jnp.int32, (SSM_GROUP, LANES), 1)
    lo = lane < SSM_STATE
    lane_n = lax.broadcasted_iota(jnp.int32, (SSM_GROUP, n), 1)
    nt = (((1,), (1,)), ((), ()))
    pair = lambda x, y: jnp.where(lo, x, y)

    swap = lambda x: pltpu.roll(x, SSM_STATE, axis=1)

    for j in range(PREP_GROUPS):
        xs, ys = {}, {}
        lam_re = lre_ref[0, j]
        lam_im = lim_ref[0, j]
        dt = jnp.exp(ldt_ref[0, j])
        mag = jnp.exp(lam_re * dt)
        lb_re = mag * jnp.cos(lam_im * dt)
        lb_im = mag * jnp.sin(lam_im * dt)
        n_re = lb_re - 1.0
        den = lam_re * lam_re + lam_im * lam_im
        q_re = (n_re * lam_re + lb_im * lam_im) / den
        q_im = (lb_im * lam_re - n_re * lam_im) / den
        bt_re = btre_ref[0, j]
        bt_im = btim_ref[0, j]
        bb_re = q_re * bt_re - q_im * bt_im
        bb_im = q_re * bt_im + q_im * bt_re
        c_re = cre_ref[0, j]
        c_im = cim_ref[0, j]
        pw_re = jnp.ones_like(lb_re)
        pw_im = jnp.zeros_like(lb_re)
        for k in range(N_POW):
            xs[k] = (pw_re * bb_re - pw_im * bb_im, pw_re * bb_im + pw_im * bb_re)
            ys[k] = (pw_re * c_re - pw_im * c_im, pw_re * c_im + pw_im * c_re)
            pw_last = (pw_re, pw_im)
            pw_re, pw_im = pw_re * lb_re - pw_im * lb_im, pw_re * lb_im + pw_im * lb_re

        step = [GROUPS_PER_SLAB * (a // GROUPS_PER_SLAB) + (a % GROUPS_PER_SLAB - j) % GROUPS_PER_SLAB
                for a in range(CHUNK)]
        rows = lambda a: slice(a * SSM_GROUP, (a + 1) * SSM_GROUP)

        decay_ref[0, 0, j:j + 1, :] = pw_last[0][0:1]
        decay_ref[0, 1, j:j + 1, :] = pw_last[1][0:1]

        for a, s in enumerate(step):
            win_ref[0, j, rows(a), :LANES] = pair(xs[CHUNK - 1 - s][0], xs[s][0]).astype(BF16)
            win_ref[0, j, rows(a), LANES:] = pair(xs[CHUNK - 1 - s][1], xs[s][1]).astype(BF16)

        z_re = jnp.concatenate([pair(ys[t + 1][0], ys[CHUNK - t][0]) for t in step], axis=0)
        z_nim = jnp.concatenate([pair(-ys[t + 1][1], -ys[CHUNK - t][1]) for t in step], axis=0)
        wout_ref[0, j, :LANES, :] = z_re.T.astype(BF16)
        wout_ref[0, j, LANES:, :] = z_nim.T.astype(BF16)

        kf = lax.dot_general(
            pair(bb_re, -swap(bb_im)),
            jnp.concatenate([pair(ys[m][0], swap(ys[m][1])) for m in range(CHUNK)], axis=0),
            nt, precision=lax.Precision.HIGHEST, preferred_element_type=F32)
        kb = lax.dot_general(
            pair(swap(bb_re), -bb_im),
            jnp.concatenate([pair(swap(ys[CHUNK - 1 - i][0]), ys[CHUNK - 1 - i][1])
                             for i in range(CHUNK)], axis=0),
            nt, precision=lax.Precision.HIGHEST, preferred_element_type=F32)
        for a, s in enumerate(step):
            fwd = pltpu.roll(kf, SSM_GROUP * s, axis=1) if s else kf
            shift_b = (n - SSM_GROUP * (CHUNK - 1 - s)) % n
            bwd = pltpu.roll(kb, shift_b, axis=1) if shift_b else kb
            t_nat = (jnp.where(lane_n >= SSM_GROUP * s, fwd, 0.0)
                     + jnp.where(lane_n < SSM_GROUP * (s + 1), bwd, 0.0))
            for h in range(2):
                half = t_nat[:, h * LANES:(h + 1) * LANES]
                if j:
                    half = pltpu.roll(half, SSM_GROUP * j, axis=1)
                toep_ref[0, j, rows(a), h * LANES:(h + 1) * LANES] = half.astype(BF16)


def _ssm_prep(lam_re, lam_im, log_dt, b_re, b_im, c_re, c_im):
    depth = lam_re.shape[0]
    G, P, C = SSM_GROUPS, SSM_STATE, SSM_GROUP
    n = CHUNK * C
    gb = PREP_GROUPS
    both = lambda a: jnp.concatenate([a[:, 0], a[:, 1]], axis=-1).astype(F32)
    rep = lambda a: jnp.broadcast_to(both(a)[:, :, None, :], (depth, G, C, LANES))
    ldt = rep(jnp.broadcast_to(log_dt[..., None], (depth, 2, G, P)))
    bt = lambda a: both(a.transpose(0, 1, 2, 4, 3))
    spec_in = pl.BlockSpec((1, gb, C, LANES), lambda l, i: (l, i, 0, 0))
    spec_w = pl.BlockSpec((1, gb, n, n), lambda l, i: (l, i, 0, 0))
    shp_w = jax.ShapeDtypeStruct((depth, G, n, n), BF16)
    return pl.pallas_call(
        _ssm_prep_kernel,
        grid=(depth, G // gb),
        in_specs=[spec_in] * 7,
        out_specs=[spec_w, spec_w, spec_w, pl.BlockSpec((1, 2, gb, LANES), lambda l, i: (l, 0, i, 0))],
        out_shape=[shp_w, shp_w, shp_w, jax.ShapeDtypeStruct((depth, 2, G, LANES), F32)],
        compiler_params=pltpu.CompilerParams(
            dimension_semantics=("arbitrary", "arbitrary"), vmem_limit_bytes=VMEM_LIMIT),
        name="ssm_prep",
    )(rep(lam_re), rep(lam_im), ldt, bt(b_re), bt(b_im), both(c_re), both(c_im))


GROUPS_PER_SLAB = LANES // SSM_GROUP
N_SLABS = W_BRANCH // LANES
RELAYOUT_ROWS = 32


def _ssm_kernel(xa_ref, toep_ref, win_ref, wout_ref, decay_ref, d_ref, o_ref, xf, u_s, st, hs, yq):
    seq = xa_ref.shape[0]
    nchunk = seq // CHUNK
    nrb = nchunk // RELAYOUT_ROWS
    gps = GROUPS_PER_SLAB
    lane_rb = lax.broadcasted_iota(jnp.int32, (RELAYOUT_ROWS, LANES), 1)
    seg_masks = [(lane_rb >= SSM_GROUP * sg) & (lane_rb < SSM_GROUP * (sg + 1)) for sg in range(gps)]
    conv_rows = 512

    for q in range(N_SLABS):
        c_lo = q * LANES

        def conv(i, carry, c_lo=c_lo):
            r0 = pl.multiple_of(i * conv_rows, conv_rows)
            xf[pl.ds(r0, conv_rows), :] = xa_ref[pl.ds(r0, conv_rows), c_lo:c_lo + LANES].astype(F32)
            return carry

        lax.fori_loop(0, seq // conv_rows, conv, 0)

        for hh in range(2):
            def fwd_relayout(rb, carry, hh=hh):
                c0 = pl.multiple_of(rb * RELAYOUT_ROWS, RELAYOUT_ROWS)
                rolled = []
                for r in range(gps):
                    xs = xf[pl.ds(CHUNK * c0 + gps * hh + r, RELAYOUT_ROWS, stride=CHUNK), :]
                    rolled.append((xs if r == 0 else pltpu.roll(xs, SSM_GROUP * r, axis=1)).astype(BF16))
                for j in range(gps):
                    out = rolled[(0 - j) % gps]
                    for sg in range(1, gps):
                        out = jnp.where(seg_masks[sg], rolled[(sg - j) % gps], out)
                    u_s[j, pl.ds(c0, RELAYOUT_ROWS), hh * LANES:(hh + 1) * LANES] = out
                return carry

            lax.fori_loop(0, nrb, fwd_relayout, 0, unroll=True)

        for j in range(gps):
            s_in = jnp.dot(u_s[j], win_ref[gps * q + j], preferred_element_type=F32)
            st[0, pl.ds(j, nchunk, stride=gps), :] = s_in[:, :LANES]
            st[1, pl.ds(j, nchunk, stride=gps), :] = s_in[:, LANES:]

        g_lo = gps * q
        a_re = decay_ref[0, g_lo:g_lo + gps, :]
        a_im = decay_ref[1, g_lo:g_lo + gps, :]
        lo, hi = slice(0, SSM_STATE), slice(SSM_STATE, LANES)

        def scan(i, carry):
            hrf, hif, hrb, hib = carry
            rf = pl.ds(pl.multiple_of(i * gps, gps), gps)
            rb_ = pl.ds(pl.multiple_of((nchunk - 1 - i) * gps, gps), gps)
            hs[0, rf, lo] = hrf[:, lo]
            hs[1, rf, lo] = hif[:, lo]
            hs[0, rb_, hi] = hrb[:, hi]
            hs[1, rb_, hi] = hib[:, hi]
            return (a_re * hrf - a_im * hif + st[0, rf, :], a_re * hif + a_im * hrf + st[1, rf, :],
                    a_re * hrb - a_im * hib + st[0, rb_, :], a_re * hib + a_im * hrb + st[1, rb_, :])

        zero = jnp.zeros((gps, LANES), F32)
        lax.fori_loop(0, nchunk, scan, (zero, zero, zero, zero), unroll=8)

        for j in range(gps):
            g = gps * q + j
            h_in = jnp.concatenate([hs[0, pl.ds(j, nchunk, stride=gps), :].astype(BF16),
                                    hs[1, pl.ds(j, nchunk, stride=gps), :].astype(BF16)], axis=-1)
            yq[j] = (jnp.dot(u_s[j], toep_ref[g], preferred_element_type=F32)
                     + jnp.dot(h_in, wout_ref[g], preferred_element_type=F32))

        d_row = d_ref[:, c_lo:c_lo + LANES]
        for hh in range(2):
            def bwd_relayout(rb, carry, hh=hh, d_row=d_row, q=q):
                c0 = pl.multiple_of(rb * RELAYOUT_ROWS, RELAYOUT_ROWS)
                ys = [yq[j, pl.ds(c0, RELAYOUT_ROWS), hh * LANES:(hh + 1) * LANES] for j in range(gps)]
                for r in range(gps):
                    merged = ys[(0 - r) % gps]
                    for sg in range(1, gps):
                        merged = jnp.where(seg_masks[sg], ys[(sg - r) % gps], merged)
                    if r:
                        merged = pltpu.roll(merged, LANES - SSM_GROUP * r, axis=1)
                    tok = pl.ds(CHUNK * c0 + gps * hh + r, RELAYOUT_ROWS, stride=CHUNK)
                    o_ref[q, tok, :] = merged + d_row * xf[tok, :]
                return carry

            lax.fori_loop(0, nrb, bwd_relayout, 0, unroll=True)


def _ssm(proj, toep, w_in, w_out, decay, d_skip, layer, batch, seq):
    nchunk = seq // CHUNK
    n = CHUNK * SSM_GROUP
    wspec = pl.BlockSpec((None, SSM_GROUPS, n, n), lambda b: (layer, 0, 0, 0), pipeline_mode=pl.Buffered(1))
    return pl.pallas_call(
        _ssm_kernel,
        grid=(batch,),
        in_specs=[
            pl.BlockSpec((seq, W_BRANCH), lambda b: (b, COL_XA // N_SLABS)),
            wspec, wspec, wspec,
            pl.BlockSpec((None, 2, SSM_GROUPS, LANES), lambda b: (layer, 0, 0, 0)),
            pl.BlockSpec((None, 1, W_BRANCH), lambda b: (layer, 0, 0)),
        ],
        out_specs=pl.BlockSpec((N_SLABS, seq, LANES), lambda b: (0, b, 0)),
        out_shape=jax.ShapeDtypeStruct((N_SLABS, batch * seq, LANES), F32),
        scratch_shapes=[
            pltpu.VMEM((seq, LANES), F32),
            pltpu.VMEM((GROUPS_PER_SLAB, nchunk, n), BF16),
            pltpu.VMEM((2, nchunk * GROUPS_PER_SLAB, LANES), F32),
            pltpu.VMEM((2, nchunk * GROUPS_PER_SLAB, LANES), F32),
            pltpu.VMEM((GROUPS_PER_SLAB, nchunk, n), F32),
        ],
        compiler_params=pltpu.CompilerParams(
            dimension_semantics=("arbitrary",), vmem_limit_bytes=VMEM_LIMIT),
        name="ssm_mix",
    )(proj, toep, w_in, w_out, decay, d_skip)


OUT_TM = 1024


def _out_proj_kernel(ya_ref, za_ref, yb_ref, yc_ref, x_ref, w_ref, gw_ref, gb_ref, fg_ref, o_ref,
                     w_bf, gw_bf, *, final):
    @pl.when(pl.program_id(0) == 0)
    def _():
        for n in range(MIX_WIDTH // W_BRANCH):
            w_bf[n * W_BRANCH:(n + 1) * W_BRANCH, :] = w_ref[n * W_BRANCH:(n + 1) * W_BRANCH, :].astype(BF16)
        gw_bf[...] = gw_ref[...].astype(BF16)

    y = jnp.concatenate([ya_ref[i] for i in range(N_SLABS)], axis=-1)
    g = _gelu_tanh(y)
    gate = jnp.dot(g.astype(BF16), gw_bf[...], preferred_element_type=F32) + gb_ref[...]
    ya = g * _sigmoid(gate) * _silu(za_ref[...].astype(F32))
    delta = (jnp.dot(ya.astype(BF16), w_bf[:W_BRANCH, :], preferred_element_type=F32)
             + jnp.dot(yb_ref[...], w_bf[W_BRANCH:2 * W_BRANCH, :], preferred_element_type=F32)
             + jnp.dot(yc_ref[...], w_bf[2 * W_BRANCH:, :], preferred_element_type=F32))
    x = x_ref[...] + delta
    if final:
        ms = jnp.mean(x * x, axis=-1, keepdims=True)
        x = x * lax.rsqrt(ms + RMS_EPS) * fg_ref[...]
    o_ref[...] = x


def _out_proj(ya_pre, proj, yb, yc, x2d, w, glu_w, glu_b, final_g, layer, final):
    rows = x2d.shape[0]
    row_blk = lambda width: pl.BlockSpec((OUT_TM, width), lambda i: (i, 0))
    const = lambda shape: pl.BlockSpec((None,) + shape, lambda i: (layer,) + (0,) * len(shape),
                                       pipeline_mode=pl.Buffered(1))
    return pl.pallas_call(
        functools.partial(_out_proj_kernel, final=final),
        grid=(rows // OUT_TM,),
        in_specs=[
            pl.BlockSpec((N_SLABS, OUT_TM, LANES), lambda i: (0, i, 0)),
            pl.BlockSpec((OUT_TM, W_BRANCH), lambda i: (i, COL_ZA // N_SLABS)),
            row_blk(W_BRANCH), row_blk(W_BRANCH), row_blk(D_MODEL),
            const((MIX_WIDTH, D_MODEL)), const((W_BRANCH, W_BRANCH)),
            const((1, W_BRANCH)),
            pl.BlockSpec((None, 1, D_MODEL), lambda i: (0, 0, 0), pipeline_mode=pl.Buffered(1)),
        ],
        out_specs=row_blk(D_MODEL),
        out_shape=jax.ShapeDtypeStruct((rows, D_MODEL), F32),
        scratch_shapes=[pltpu.VMEM((MIX_WIDTH, D_MODEL), BF16), pltpu.VMEM((W_BRANCH, W_BRANCH), BF16)],
        compiler_params=pltpu.CompilerParams(
            dimension_semantics=("arbitrary",), vmem_limit_bytes=VMEM_LIMIT),
        name="out_proj_final" if final else "out_proj",
    )(ya_pre, proj, yb, yc, x2d, w, glu_w, glu_b, final_g)


def kernel(x, norm_g, w_in, w_out, ssm_lam_re, ssm_lam_im, ssm_log_dt, ssm_b_re, ssm_b_im, ssm_c_re,
           ssm_c_im, ssm_d, glu_w, glu_b, na_rpb, t5_bias, final_g):
    batch, seq, _ = x.shape
    depth = w_in.shape[0]
    x2d = x.astype(F32).reshape(batch * seq, D_MODEL)
    toep, s_in, s_out, decay = _ssm_prep(ssm_lam_re, ssm_lam_im, ssm_log_dt, ssm_b_re, ssm_b_im,
                                         ssm_c_re, ssm_c_im)
    dil_bias = _dil_bias_table(t5_bias)
    na_bias = _na_bias_table(na_rpb)
    w_in_f, w_out_f, glu_w_f = w_in.astype(F32), w_out.astype(F32), glu_w.astype(F32)
    row3 = lambda a: a.astype(F32).reshape(a.shape[0], 1, a.shape[1])
    norm_g3, ssm_d3, glu_b3 = row3(norm_g), row3(ssm_d), row3(glu_b)
    final_g3 = final_g.astype(F32).reshape(1, 1, D_MODEL)
    for l in range(depth):
        proj = _in_proj(x2d, norm_g3, w_in_f, l)
        ya_pre = _ssm(proj, toep, s_in, s_out, decay, ssm_d3, l, batch, seq)
        yb = _na(proj, na_bias, l, batch, seq)
        yc = _dil(proj, dil_bias, batch, seq)
        x2d = _out_proj(ya_pre, proj, yb, yc, x2d, w_out_f, glu_w_f, glu_b3, final_g3, l,
                        final=(l == depth - 1))
    return x2d.reshape(batch, seq, D_MODEL).astype(x.dtype)
```

```python
import functools

import numpy as np
import jax
import jax.numpy as jnp
from jax import lax
from jax.experimental import pallas as pl
from jax.experimental.pallas import tpu as pltpu

F32 = jnp.float32
BF16 = jnp.bfloat16

D_MODEL = 1024
HEAD_DIM = 64
W_BRANCH = 512
N_HEADS = W_BRANCH // HEAD_DIM
N_HEAD_PAIRS = N_HEADS // 2
SSM_GROUP = 16
SSM_GROUPS = W_BRANCH // SSM_GROUP
SSM_STATE = 64
GRID_W = 64
NA_ROWS = 8
NA_COLS = 16
DIL_PATTERNS = ((128, 1), (512, 4), (2048, 16))
DIL_HALF = 64
T5_BUCKETS = 32
T5_MAX_DIST = 1024
RMS_EPS = 1e-6
NEG_INF = -1e30
IN_COLS = 10 * W_BRANCH
MIX_WIDTH = 3 * W_BRANCH

COL_XA, COL_ZA, COL_QB, COL_KB, COL_VB, COL_ZB, COL_QC, COL_KC, COL_VC, COL_ZC = (
    4 * i for i in range(10))

LOG2E = float(np.log2(np.e))
Q_SCALE_LOG2 = HEAD_DIM ** -0.5 * LOG2E

LANES = 128
CHUNK = 16
VMEM_LIMIT = 56 * 1024 * 1024


def _sigmoid(z):
    return 0.5 * (1.0 + jnp.tanh(0.5 * z))


def _silu(z):
    return z * _sigmoid(z)


def _gelu_tanh(x):
    return 0.5 * x * (1.0 + jnp.tanh(np.sqrt(2.0 / np.pi).astype(np.float32) * (x + 0.044715 * (x * x * x))))


IN_TM = 512
IN_TN = 512


def _in_proj_kernel(x_ref, g_ref, w_ref, o_ref, w_bf):
    @pl.when(pl.program_id(0) == 0)
    def _():
        for n in range(IN_COLS // IN_TN):
            w_bf[:, n * IN_TN:(n + 1) * IN_TN] = w_ref[:, n * IN_TN:(n + 1) * IN_TN].astype(BF16)

    x = x_ref[...]
    ms = jnp.mean(x * x, axis=-1, keepdims=True)
    h = (x * lax.rsqrt(ms + RMS_EPS) * g_ref[...]).astype(BF16)
    for n in range(IN_COLS // IN_TN):
        o_ref[:, n * IN_TN:(n + 1) * IN_TN] = jnp.dot(
            h, w_bf[:, n * IN_TN:(n + 1) * IN_TN], preferred_element_type=F32).astype(BF16)


def _in_proj(x2d, g, w, layer):
    rows = x2d.shape[0]
    return pl.pallas_call(
        _in_proj_kernel,
        grid=(rows // IN_TM,),
        in_specs=[
            pl.BlockSpec((IN_TM, D_MODEL), lambda i: (i, 0)),
            pl.BlockSpec((None, 1, D_MODEL), lambda i: (layer, 0, 0)),
            pl.BlockSpec((None, D_MODEL, IN_COLS), lambda i: (layer, 0, 0), pipeline_mode=pl.Buffered(1)),
        ],
        out_specs=pl.BlockSpec((IN_TM, IN_COLS), lambda i: (i, 0)),
        out_shape=jax.ShapeDtypeStruct((rows, IN_COLS), BF16),
        scratch_shapes=[pltpu.VMEM((D_MODEL, IN_COLS), BF16)],
        compiler_params=pltpu.CompilerParams(
            dimension_semantics=("arbitrary",), vmem_limit_bytes=VMEM_LIMIT),
        name="in_proj",
    )(x2d, g, w)


def _na_bias_kernel(rpb_ref, o_ref):
    lane = lax.broadcasted_iota(jnp.int32, (GRID_W, LANES), 1)
    j = lax.broadcasted_iota(jnp.int32, (GRID_W, LANES), 0)
    c = lane & (GRID_W - 1)
    col_start = jnp.clip(j - NA_COLS // 2, 0, GRID_W - NA_COLS)
    valid = (c >= col_start) & (c < col_start + NA_COLS)
    first = lane < GRID_W
    even, odd = [], []
    for a in range(2 * NA_ROWS - 1):
        vec = pltpu.roll(rpb_ref[a:a + 1, :], LANES - (NA_COLS - 1), axis=1)
        x = jnp.broadcast_to(vec, (GRID_W, LANES))
        even.append(pltpu.roll(x, 0, axis=1, stride=1, stride_axis=0))
        odd.append(pltpu.roll(x, GRID_W, axis=1, stride=1, stride_axis=0))
    for v in range(NA_ROWS):
        for i in range(NA_ROWS // 2):
            a0 = 2 * i - v + (NA_ROWS - 1)
            tile = jnp.where(first, even[a0], odd[a0 + 1])
            o_ref[v, :, i * LANES:(i + 1) * LANES] = jnp.where(valid, tile * LOG2E, NEG_INF)


def _na_bias_table(rpb):
    depth = rpb.shape[0]
    padded = jnp.pad(rpb.astype(F32), ((0, 0), (0, 0), (0, 1), (0, LANES - (2 * NA_COLS - 1))))
    nkeys = NA_ROWS * GRID_W
    return pl.pallas_call(
        _na_bias_kernel,
        grid=(depth, N_HEADS),
        in_specs=[pl.BlockSpec((None, None, 2 * NA_ROWS, LANES), lambda l, h: (l, h, 0, 0))],
        out_specs=pl.BlockSpec((None, None, NA_ROWS, GRID_W, nkeys), lambda l, h: (l, h, 0, 0, 0)),
        out_shape=jax.ShapeDtypeStruct((depth, N_HEADS, NA_ROWS, GRID_W, nkeys), F32),
        compiler_params=pltpu.CompilerParams(dimension_semantics=("arbitrary", "arbitrary")),
        name="na_bias",
    )(padded)


def _na_kernel(q_ref, k_ref, v_ref, z_ref, bias_ref, o_ref):
    seq = q_ref.shape[0]
    rows = seq // GRID_W
    half_rows = NA_ROWS // 2
    tile_keys = half_rows * GRID_W
    lane = lax.broadcasted_iota(jnp.int32, (GRID_W, LANES), 1)
    first_head = lane < HEAD_DIM
    masks = (first_head, jnp.logical_not(first_head))
    nt = (((1,), (1,)), ((), ()))
    ones = jnp.ones((tile_keys, LANES), BF16)
    row_start = lambda r: min(max(r - NA_ROWS // 2, 0), rows - NA_ROWS)

    for g0 in range(0, rows, NA_GROUP):
        group = range(g0, g0 + NA_GROUP)
        users = {}
        for r in group:
            for half in range(2):
                users.setdefault(row_start(r) + half_rows * half, []).append((r, half))
        qm = {}
        for r in group:
            qb = (q_ref[r * GRID_W:(r + 1) * GRID_W, :].astype(F32) * Q_SCALE_LOG2).astype(BF16)
            for hd in range(2):
                qm[r, hd] = jnp.where(masks[hd], qb, jnp.zeros_like(qb))
        s_half = {}
        for k, us in sorted(users.items()):
            lhs = jnp.concatenate([qm[r, hd] for r, _ in us for hd in range(2)], axis=0)
            part = lax.dot_general(lhs, k_ref[k * GRID_W:k * GRID_W + tile_keys, :], nt,
                                   preferred_element_type=F32)
            for i, (r, half) in enumerate(us):
                for hd in range(2):
                    blk = part[(2 * i + hd) * GRID_W:(2 * i + hd + 1) * GRID_W, :]
                    bias = bias_ref[hd, r - row_start(r), :, half * tile_keys:(half + 1) * tile_keys]
                    s_half[r, hd, half] = (blk + bias).astype(BF16)
        p_half = {}
        for r in group:
            for hd in range(2):
                s0, s1 = s_half[r, hd, 0], s_half[r, hd, 1]
                m = jnp.max(jnp.maximum(s0, s1), axis=-1, keepdims=True)
                p_half[r, hd, 0], p_half[r, hd, 1] = jnp.exp2(s0 - m), jnp.exp2(s1 - m)
        acc = {}
        for k, us in sorted(users.items()):
            lhs = jnp.concatenate([p_half[r, hd, half] for r, half in us for hd in range(2)], axis=0)
            vext = jnp.concatenate([v_ref[k * GRID_W:k * GRID_W + tile_keys, :], ones], axis=-1)
            part = jnp.dot(lhs, vext, preferred_element_type=F32)
            for i, (r, half) in enumerate(us):
                for hd in range(2):
                    blk = part[(2 * i + hd) * GRID_W:(2 * i + hd + 1) * GRID_W, :]
                    acc[r, hd] = blk if (r, hd) not in acc else acc[r, hd] + blk
        for r in group:
            y = jnp.where(first_head, acc[r, 0][:, :LANES], acc[r, 1][:, :LANES])
            d = jnp.where(first_head, acc[r, 0][:, LANES:], acc[r, 1][:, LANES:])
            z = z_ref[r * GRID_W:(r + 1) * GRID_W, :].astype(F32)
            o_ref[r * GRID_W:(r + 1) * GRID_W, :] = (y * (1.0 / d) * _silu(z)).astype(BF16)


NA_GROUP = 16


def _na(proj, bias, layer, batch, seq):
    def col(c0):
        return pl.BlockSpec((seq, LANES), lambda b, p, c0=c0: (b, c0 + p))

    return pl.pallas_call(
        _na_kernel,
        grid=(batch, N_HEAD_PAIRS),
        in_specs=[
            col(COL_QB), col(COL_KB), col(COL_VB), col(COL_ZB),
            pl.BlockSpec((None, 2, NA_ROWS, GRID_W, NA_ROWS * GRID_W), lambda b, p: (layer, p, 0, 0, 0)),
        ],
        out_specs=pl.BlockSpec((seq, LANES), lambda b, p: (b, p)),
        out_shape=jax.ShapeDtypeStruct((batch * seq, W_BRANCH), BF16),
        compiler_params=pltpu.CompilerParams(
            dimension_semantics=("arbitrary", "arbitrary"), vmem_limit_bytes=VMEM_LIMIT),
        name="na_attn",
    )(proj, proj, proj, proj, bias)


DIL_QB = 128
DIL_KW = 256
N_VARIANTS = 3


def _t5_bucket(rel):
    nb = T5_BUCKETS // 2
    max_exact = nb // 2
    n = np.abs(rel)
    large = max_exact + (np.log(np.maximum(n, 1) / max_exact) / np.log(T5_MAX_DIST / max_exact)
                         * (nb - max_exact)).astype(np.int32)
    large = np.minimum(large, nb - 1)
    return (np.where(rel > 0, nb, 0) + np.where(n < max_exact, n, large)).astype(np.int32)


def _dil_bias_table(t5_bias):
    n_pat = len(DIL_PATTERNS)
    lane = np.arange(DIL_VEC)
    n = np.where(lane < DIL_KW, lane, lane - DIL_VEC)
    in_range = (lane < DIL_KW) | (lane > DIL_VEC - DIL_QB)
    pick = np.zeros((n_pat, N_VARIANTS, T5_BUCKETS, DIL_VEC), np.float32)
    neg = np.zeros((n_pat, N_VARIANTS, 1, DIL_VEC), np.float32)
    for pi, (_, d) in enumerate(DIL_PATTERNS):
        for var in range(N_VARIANTS):
            step = n - DIL_HALF * var
            ok = in_range & (np.abs(step) <= DIL_HALF)
            bucket = _t5_bucket(d * np.clip(step, -DIL_HALF, DIL_HALF))
            pick[pi, var, bucket[ok], lane[ok]] = 1.0
            neg[pi, var, 0, ~ok] = NEG_INF
    vecs = jnp.einsum('bh,pvbn->hpvn', t5_bias.astype(F32), pick,
                      precision=lax.Precision.HIGHEST) * LOG2E + neg[:, :, 0][None]
    return pl.pallas_call(
        _dil_bias_kernel,
        grid=(N_HEADS,),
        in_specs=[pl.BlockSpec((None, n_pat * N_VARIANTS, DIL_VEC), lambda h: (h, 0, 0))],
        out_specs=pl.BlockSpec((None, n_pat, N_VARIANTS, DIL_QB, DIL_KW), lambda h: (h, 0, 0, 0, 0)),
        out_shape=jax.ShapeDtypeStruct((N_HEADS, n_pat, N_VARIANTS, DIL_QB, DIL_KW), F32),
        compiler_params=pltpu.CompilerParams(dimension_semantics=("arbitrary",)),
        name="dil_bias",
    )(vecs.reshape(N_HEADS, n_pat * N_VARIANTS, DIL_VEC))


DIL_VEC = 512


def _dil_bias_kernel(vec_ref, o_ref):
    for pi in range(len(DIL_PATTERNS)):
        for var in range(N_VARIANTS):
            i = pi * N_VARIANTS + var
            x = jnp.broadcast_to(vec_ref[i:i + 1, :], (DIL_QB, DIL_VEC))
            o_ref[pi, var] = pltpu.roll(x, 0, axis=1, stride=1, stride_axis=0)[:, :DIL_KW]


DIL_GROUP = 16


def _dil_kernel(q_ref, k_ref, v_ref, z_ref, bias_ref, o_ref, qf, kf, vf, qg, kg, vg, qd,
                q4, k4, v4, m_s, l_s, acc_s):
    seq = q_ref.shape[0]
    conv_rows = 256
    lane = lax.broadcasted_iota(jnp.int32, (DIL_QB, LANES), 1)
    first_head = lane < HEAD_DIM
    nt = (((1,), (1,)), ((), ()))
    ones = jnp.ones((DIL_KW, LANES), BF16)

    def conv(i, carry):
        r0 = pl.multiple_of(i * conv_rows, conv_rows)
        qf[pl.ds(r0, conv_rows), :] = q_ref[pl.ds(r0, conv_rows), :].astype(F32) * Q_SCALE_LOG2
        kf[pl.ds(r0, conv_rows), :] = k_ref[pl.ds(r0, conv_rows), :].astype(F32)
        vf[pl.ds(r0, conv_rows), :] = v_ref[pl.ds(r0, conv_rows), :].astype(F32)
        return carry

    lax.fori_loop(0, seq // conv_rows, conv, 0, unroll=4)

    n4 = seq // 4
    n16 = seq // 16

    def to_order4(i, carry):
        dst = pl.multiple_of(i * conv_rows, conv_rows)
        src = pl.ds(dst // n4 + 4 * (dst % n4), conv_rows, stride=4)
        for a, b, c in ((qf, qg, q4), (kf, kg, k4), (vf, vg, v4)):
            part = a[src, :]
            b[pl.ds(dst, conv_rows), :] = part
            c[pl.ds(dst, conv_rows), :] = part.astype(BF16)
        return carry

    lax.fori_loop(0, seq // conv_rows, to_order4, 0, unroll=True)

    def window(src, res, start, size, n_sub):
        if n_sub == n16:
            rows = pl.ds((res % 4) * n4 + res // 4 + 4 * start, size, stride=4)
            return src[rows, :].astype(BF16)
        return src[pl.ds(pl.multiple_of(res * n_sub + start, DIL_HALF), size), :]

    order4_state = (qf, kf, vf)
    token_state = (m_s, l_s, acc_s)
    plan = ((2, 16, (qg, kg, vg), None, order4_state),
            (1, 4, (q4, k4, v4), order4_state, token_state),
            (0, 1, (qd, k_ref, v_ref), token_state, None))

    for step, (pi, d, (q_src, k_src, v_src), st_in, st_out) in enumerate(plan):
        n_sub = seq // d
        nblk = n_sub // DIL_QB

        if step == 2:
            def scaled_q(i, carry):
                r0 = pl.multiple_of(i * conv_rows, conv_rows)
                qd[pl.ds(r0, conv_rows), :] = (
                    q_ref[pl.ds(r0, conv_rows), :].astype(F32) * Q_SCALE_LOG2).astype(BF16)
                return carry

            lax.fori_loop(0, seq // conv_rows, scaled_q, 0, unroll=True)

        def rows_out(res, i0, step=step):
            if step == 0:
                return pl.ds((res % 4) * n4 + res // 4 + 4 * i0, DIL_QB, stride=4)
            if step == 1:
                return pl.ds(res + 4 * i0, DIL_QB, stride=4)
            return pl.ds(pl.multiple_of(i0, DIL_HALF), DIL_QB)

        n_group = DIL_GROUP

        def group(gi, carry, pi=pi, n_sub=n_sub, nblk=nblk, q_src=q_src, k_src=k_src, v_src=v_src,
                  st_in=st_in, st_out=st_out, rows_out=rows_out, n_group=n_group):
            units = []
            for i in range(n_group):
                u = gi * n_group + i
                res = u // nblk
                i0 = (u % nblk) * DIL_QB
                ks = jnp.clip(i0 - DIL_HALF, 0, n_sub - DIL_KW)
                units.append((res, i0, ks, (i0 - ks) // DIL_HALF))
            s_tiles, m_tiles = [], []
            for res, i0, ks, var in units:
                qb = window(q_src, res, i0, DIL_QB, n_sub)
                kw = window(k_src, res, ks, DIL_KW, n_sub)
                for hd in range(2):
                    keep = first_head if hd == 0 else jnp.logical_not(first_head)
                    qm = jnp.where(keep, qb, jnp.zeros_like(qb))
                    s = lax.dot_general(qm, kw, nt, preferred_element_type=F32) + bias_ref[hd, pi, var]
                    s = s.astype(BF16)
                    s_tiles.append(s)
                    m_tiles.append(jnp.max(s, axis=-1, keepdims=True))
            p_tiles = [jnp.exp2(s - m) for s, m in zip(s_tiles, m_tiles)]
            for i, (res, i0, ks, var) in enumerate(units):
                vw = window(v_src, res, ks, DIL_KW, n_sub)
                vext = jnp.concatenate([vw, ones], axis=-1)
                r0 = jnp.dot(p_tiles[2 * i], vext, preferred_element_type=F32)
                r1 = jnp.dot(p_tiles[2 * i + 1], vext, preferred_element_type=F32)
                m_cur = jnp.where(first_head, m_tiles[2 * i].astype(F32), m_tiles[2 * i + 1].astype(F32))
                l_cur = jnp.where(first_head, r0[:, LANES:], r1[:, LANES:])
                acc_cur = jnp.where(first_head, r0[:, :LANES], r1[:, :LANES])
                dst = rows_out(res, i0)
                if st_in is None:
                    m_new, l_new, acc_new = m_cur, l_cur, acc_cur
                else:
                    src = pl.ds(pl.multiple_of(res * n_sub + i0, DIL_HALF), DIL_QB)
                    m_old = st_in[0][src, :]
                    m_new = jnp.maximum(m_old, m_cur)
                    a_old = jnp.exp2(m_old - m_new)
                    a_cur = jnp.exp2(m_cur - m_new)
                    l_new = a_old * st_in[1][src, :] + a_cur * l_cur
                    acc_new = a_old * st_in[2][src, :] + a_cur * acc_cur
                if st_out is None:
                    z = z_ref[dst, :].astype(F32)
                    o_ref[dst, :] = (acc_new * (1.0 / l_new) * _silu(z)).astype(BF16)
                else:
                    st_out[0][dst, :] = m_new
                    st_out[1][dst, :] = l_new
                    st_out[2][dst, :] = acc_new
            return carry

        lax.fori_loop(0, seq // DIL_QB // n_group, group, 0)


def _dil(proj, bias, batch, seq):
    def col(c0):
        return pl.BlockSpec((seq, LANES), lambda b, p, c0=c0: (b, c0 + p))

    scratch = ([pltpu.VMEM((seq, LANES), F32) for _ in range(6)]
               + [pltpu.VMEM((seq, LANES), BF16) for _ in range(4)]
               + [pltpu.VMEM((seq, LANES), F32) for _ in range(3)])
    return pl.pallas_call(
        _dil_kernel,
        grid=(batch, N_HEAD_PAIRS),
        in_specs=[
            col(COL_QC), col(COL_KC), col(COL_VC), col(COL_ZC),
            pl.BlockSpec((2, len(DIL_PATTERNS), N_VARIANTS, DIL_QB, DIL_KW),
                         lambda b, p: (p, 0, 0, 0, 0)),
        ],
        out_specs=pl.BlockSpec((seq, LANES), lambda b, p: (b, p)),
        out_shape=jax.ShapeDtypeStruct((batch * seq, W_BRANCH), BF16),
        scratch_shapes=scratch,
        compiler_params=pltpu.CompilerParams(
            dimension_semantics=("arbitrary", "arbitrary"), vmem_limit_bytes=VMEM_LIMIT),
        name="dil_attn",
    )(proj, proj, proj, proj, bias)


N_POW = CHUNK + 1
PREP_GROUPS = 8


def _ssm_prep_kernel(lre_ref, lim_ref, ldt_ref, btre_ref, btim_ref, cre_ref, cim_ref,
                     toep_ref, win_ref, wout_ref, decay_ref):
    n = CHUNK * SSM_GROUP
    lane = lax.broadcasted_iota(jnp.int32, (SSM_GROUP, LANES), 1)
    lo = lane < SSM_STATE
    lane_n = lax.broadcasted_iota(jnp.int32, (SSM_GROUP, n), 1)
    nt = (((1,), (1,)), ((), ()))
    pair = lambda x, y: jnp.where(lo, x, y)

    swap = lambda x: pltpu.roll(x, SSM_STATE, axis=1)

    for j in range(PREP_GROUPS):
        xs, ys = {}, {}
        lam_re = lre_ref[0, j]
        lam_im = lim_ref[0, j]
        dt = jnp.exp(ldt_ref[0, j])
        mag = jnp.exp(lam_re * dt)
        lb_re = mag * jnp.cos(lam_im * dt)
        lb_im = mag * jnp.sin(lam_im * dt)
        n_re = lb_re - 1.0
        den = lam_re * lam_re + lam_im * lam_im
        q_re = (n_re * lam_re + lb_im * lam_im) / den
        q_im = (lb_im * lam_re - n_re * lam_im) / den
        bt_re = btre_ref[0, j]
        bt_im = btim_ref[0, j]
        bb_re = q_re * bt_re - q_im * bt_im
        bb_im = q_re * bt_im + q_im * bt_re
        c_re = cre_ref[0, j]
        c_im = cim_ref[0, j]
        pw_re = jnp.ones_like(lb_re)
        pw_im = jnp.zeros_like(lb_re)
        for k in range(N_POW):
            xs[k] = (pw_re * bb_re - pw_im * bb_im, pw_re * bb_im + pw_im * bb_re)
            ys[k] = (pw_re * c_re - pw_im * c_im, pw_re * c_im + pw_im * c_re)
            pw_last = (pw_re, pw_im)
            pw_re, pw_im = pw_re * lb_re - pw_im * lb_im, pw_re * lb_im + pw_im * lb_re

        step = [GROUPS_PER_SLAB * (a // GROUPS_PER_SLAB) + (a % GROUPS_PER_SLAB - j) % GROUPS_PER_SLAB
                for a in range(CHUNK)]
        rows = lambda a: slice(a * SSM_GROUP, (a + 1) * SSM_GROUP)

        decay_ref[0, 0, j:j + 1, :] = pw_last[0][0:1]
        decay_ref[0, 1, j:j + 1, :] = pw_last[1][0:1]

        for a, s in enumerate(step):
            win_ref[0, j, rows(a), :LANES] = pair(xs[CHUNK - 1 - s][0], xs[s][0]).astype(BF16)
            win_ref[0, j, rows(a), LANES:] = pair(xs[CHUNK - 1 - s][1], xs[s][1]).astype(BF16)

        z_re = jnp.concatenate([pair(ys[t + 1][0], ys[CHUNK - t][0]) for t in step], axis=0)
        z_nim = jnp.concatenate([pair(-ys[t + 1][1], -ys[CHUNK - t][1]) for t in step], axis=0)
        wout_ref[0, j, :LANES, :] = z_re.T.astype(BF16)
        wout_ref[0, j, LANES:, :] = z_nim.T.astype(BF16)

        kf = lax.dot_general(
            pair(bb_re, -swap(bb_im)),
            jnp.concatenate([pair(ys[m][0], swap(ys[m][1])) for m in range(CHUNK)], axis=0),
            nt, precision=lax.Precision.HIGHEST, preferred_element_type=F32)
        kb = lax.dot_general(
            pair(swap(bb_re), -bb_im),
            jnp.concatenate([pair(swap(ys[CHUNK - 1 - i][0]), ys[CHUNK - 1 - i][1])
                             for i in range(CHUNK)], axis=0),
            nt, precision=lax.Precision.HIGHEST, preferred_element_type=F32)
        for a, s in enumerate(step):
            fwd = pltpu.roll(kf, SSM_GROUP * s, axis=1) if s else kf
            shift_b = (n - SSM_GROUP * (CHUNK - 1 - s)) % n
            bwd = pltpu.roll(kb, shift_b, axis=1) if shift_b else kb
            t_nat = (jnp.where(lane_n >= SSM_GROUP * s, fwd, 0.0)
                     + jnp.where(lane_n < SSM_GROUP * (s + 1), bwd, 0.0))
            for h in range(2):
                half = t_nat[:, h * LANES:(h + 1) * LANES]
                if j:
                    half = pltpu.roll(half, SSM_GROUP * j, axis=1)
                toep_ref[0, j, rows(a), h * LANES:(h + 1) * LANES] = half.astype(BF16)


def _ssm_prep(lam_re, lam_im, log_dt, b_re, b_im, c_re, c_im):
    depth = lam_re.shape[0]
    G, P, C = SSM_GROUPS, SSM_STATE, SSM_GROUP
    n = CHUNK * C
    gb = PREP_GROUPS
    both = lambda a: jnp.concatenate([a[:, 0], a[:, 1]], axis=-1).astype(F32)
    rep = lambda a: jnp.broadcast_to(both(a)[:, :, None, :], (depth, G, C, LANES))
    ldt = rep(jnp.broadcast_to(log_dt[..., None], (depth, 2, G, P)))
    bt = lambda a: both(a.transpose(0, 1, 2, 4, 3))
    spec_in = pl.BlockSpec((1, gb, C, LANES), lambda l, i: (l, i, 0, 0))
    spec_w = pl.BlockSpec((1, gb, n, n), lambda l, i: (l, i, 0, 0))
    shp_w = jax.ShapeDtypeStruct((depth, G, n, n), BF16)
    return pl.pallas_call(
        _ssm_prep_kernel,
        grid=(depth, G // gb),
        in_specs=[spec_in] * 7,
        out_specs=[spec_w, spec_w, spec_w, pl.BlockSpec((1, 2, gb, LANES), lambda l, i: (l, 0, i, 0))],
        out_shape=[shp_w, shp_w, shp_w, jax.ShapeDtypeStruct((depth, 2, G, LANES), F32)],
        compiler_params=pltpu.CompilerParams(
            dimension_semantics=("arbitrary", "arbitrary"), vmem_limit_bytes=VMEM_LIMIT),
        name="ssm_prep",
    )(rep(lam_re), rep(lam_im), ldt, bt(b_re), bt(b_im), both(c_re), both(c_im))


GROUPS_PER_SLAB = LANES // SSM_GROUP
N_SLABS = W_BRANCH // LANES
RELAYOUT_ROWS = 32


def _ssm_kernel(xa_ref, toep_ref, win_ref, wout_ref, decay_ref, d_ref, o_ref, xf, u_s, st, hs, yq):
    seq = xa_ref.shape[0]
    nchunk = seq // CHUNK
    nrb = nchunk // RELAYOUT_ROWS
    gps = GROUPS_PER_SLAB
    lane_rb = lax.broadcasted_iota(jnp.int32, (RELAYOUT_ROWS, LANES), 1)
    seg_masks = [(lane_rb >= SSM_GROUP * sg) & (lane_rb < SSM_GROUP * (sg + 1)) for sg in range(gps)]
    conv_rows = 512

    for q in range(N_SLABS):
        c_lo = q * LANES

        def conv(i, carry, c_lo=c_lo):
            r0 = pl.multiple_of(i * conv_rows, conv_rows)
            xf[pl.ds(r0, conv_rows), :] = xa_ref[pl.ds(r0, conv_rows), c_lo:c_lo + LANES].astype(F32)
            return carry

        lax.fori_loop(0, seq // conv_rows, conv, 0)

        for hh in range(2):
            def fwd_relayout(rb, carry, hh=hh):
                c0 = pl.multiple_of(rb * RELAYOUT_ROWS, RELAYOUT_ROWS)
                rolled = []
                for r in range(gps):
                    xs = xf[pl.ds(CHUNK * c0 + gps * hh + r, RELAYOUT_ROWS, stride=CHUNK), :]
                    rolled.append((xs if r == 0 else pltpu.roll(xs, SSM_GROUP * r, axis=1)).astype(BF16))
                for j in range(gps):
                    out = rolled[(0 - j) % gps]
                    for sg in range(1, gps):
                        out = jnp.where(seg_masks[sg], rolled[(sg - j) % gps], out)
                    u_s[j, pl.ds(c0, RELAYOUT_ROWS), hh * LANES:(hh + 1) * LANES] = out
                return carry

            lax.fori_loop(0, nrb, fwd_relayout, 0, unroll=True)

        for j in range(gps):
            s_in = jnp.dot(u_s[j], win_ref[gps * q + j], preferred_element_type=F32)
            st[0, pl.ds(j, nchunk, stride=gps), :] = s_in[:, :LANES]
            st[1, pl.ds(j, nchunk, stride=gps), :] = s_in[:, LANES:]

        g_lo = gps * q
        a_re = decay_ref[0, g_lo:g_lo + gps, :]
        a_im = decay_ref[1, g_lo:g_lo + gps, :]
        lo, hi = slice(0, SSM_STATE), slice(SSM_STATE, LANES)

        def scan(i, carry):
            hrf, hif, hrb, hib = carry
            rf = pl.ds(pl.multiple_of(i * gps, gps), gps)
            rb_ = pl.ds(pl.multiple_of((nchunk - 1 - i) * gps, gps), gps)
            hs[0, rf, lo] = hrf[:, lo]
            hs[1, rf, lo] = hif[:, lo]
            hs[0, rb_, hi] = hrb[:, hi]
            hs[1, rb_, hi] = hib[:, hi]
            return (a_re * hrf - a_im * hif + st[0, rf, :], a_re * hif + a_im * hrf + st[1, rf, :],
                    a_re * hrb - a_im * hib + st[0, rb_, :], a_re * hib + a_im * hrb + st[1, rb_, :])

        zero = jnp.zeros((gps, LANES), F32)
        lax.fori_loop(0, nchunk, scan, (zero, zero, zero, zero), unroll=8)

        for j in range(gps):
            g = gps * q + j
            h_in = jnp.concatenate([hs[0, pl.ds(j, nchunk, stride=gps), :].astype(BF16),
                                    hs[1, pl.ds(j, nchunk, stride=gps), :].astype(BF16)], axis=-1)
            yq[j] = (jnp.dot(u_s[j], toep_ref[g], preferred_element_type=F32)
                     + jnp.dot(h_in, wout_ref[g], preferred_element_type=F32))

        d_row = d_ref[:, c_lo:c_lo + LANES]
        for hh in range(2):
            def bwd_relayout(rb, carry, hh=hh, d_row=d_row, q=q):
                c0 = pl.multiple_of(rb * RELAYOUT_ROWS, RELAYOUT_ROWS)
                ys = [yq[j, pl.ds(c0, RELAYOUT_ROWS), hh * LANES:(hh + 1) * LANES] for j in range(gps)]
                for r in range(gps):
                    merged = ys[(0 - r) % gps]
                    for sg in range(1, gps):
                        merged = jnp.where(seg_masks[sg], ys[(sg - r) % gps], merged)
                    if r:
                        merged = pltpu.roll(merged, LANES - SSM_GROUP * r, axis=1)
                    tok = pl.ds(CHUNK * c0 + gps * hh + r, RELAYOUT_ROWS, stride=CHUNK)
                    o_ref[q, tok, :] = merged + d_row * xf[tok, :]
                return carry

            lax.fori_loop(0, nrb, bwd_relayout, 0, unroll=True)


def _ssm(proj, toep, w_in, w_out, decay, d_skip, layer, batch, seq):
    nchunk = seq // CHUNK
    n = CHUNK * SSM_GROUP
    wspec = pl.BlockSpec((None, SSM_GROUPS, n, n), lambda b: (layer, 0, 0, 0), pipeline_mode=pl.Buffered(1))
    return pl.pallas_call(
        _ssm_kernel,
        grid=(batch,),
        in_specs=[
            pl.BlockSpec((seq, W_BRANCH), lambda b: (b, COL_XA // N_SLABS)),
            wspec, wspec, wspec,
            pl.BlockSpec((None, 2, SSM_GROUPS, LANES), lambda b: (layer, 0, 0, 0)),
            pl.BlockSpec((None, 1, W_BRANCH), lambda b: (layer, 0, 0)),
        ],
        out_specs=pl.BlockSpec((N_SLABS, seq, LANES), lambda b: (0, b, 0)),
        out_shape=jax.ShapeDtypeStruct((N_SLABS, batch * seq, LANES), F32),
        scratch_shapes=[
            pltpu.VMEM((seq, LANES), F32),
            pltpu.VMEM((GROUPS_PER_SLAB, nchunk, n), BF16),
            pltpu.VMEM((2, nchunk * GROUPS_PER_SLAB, LANES), F32),
            pltpu.VMEM((2, nchunk * GROUPS_PER_SLAB, LANES), F32),
            pltpu.VMEM((GROUPS_PER_SLAB, nchunk, n), F32),
        ],
        compiler_params=pltpu.CompilerParams(
            dimension_semantics=("arbitrary",), vmem_limit_bytes=VMEM_LIMIT),
        name="ssm_mix",
    )(proj, toep, w_in, w_out, decay, d_skip)


OUT_TM = 1024


def _out_proj_kernel(ya_ref, za_ref, yb_ref, yc_ref, x_ref, w_ref, gw_ref, gb_ref, fg_ref, o_ref,
                     w_bf, gw_bf, *, final):
    @pl.when(pl.program_id(0) == 0)
    def _():
        for n in range(MIX_WIDTH // W_BRANCH):
            w_bf[n * W_BRANCH:(n + 1) * W_BRANCH, :] = w_ref[n * W_BRANCH:(n + 1) * W_BRANCH, :].astype(BF16)
        gw_bf[...] = gw_ref[...].astype(BF16)

    y = jnp.concatenate([ya_ref[i] for i in range(N_SLABS)], axis=-1)
    g = _gelu_tanh(y)
    gate = jnp.dot(g.astype(BF16), gw_bf[...], preferred_element_type=F32) + gb_ref[...]
    ya = g * _sigmoid(gate) * _silu(za_ref[...].astype(F32))
    delta = (jnp.dot(ya.astype(BF16), w_bf[:W_BRANCH, :], preferred_element_type=F32)
             + jnp.dot(yb_ref[...], w_bf[W_BRANCH:2 * W_BRANCH, :], preferred_element_type=F32)
             + jnp.dot(yc_ref[...], w_bf[2 * W_BRANCH:, :], preferred_element_type=F32))
    x = x_ref[...] + delta
    if final:
        ms = jnp.mean(x * x, axis=-1, keepdims=True)
        x = x * lax.rsqrt(ms + RMS_EPS) * fg_ref[...]
    o_ref[...] = x


def _out_proj(ya_pre, proj, yb, yc, x2d, w, glu_w, glu_b, final_g, layer, final):
    rows = x2d.shape[0]
    row_blk = lambda width: pl.BlockSpec((OUT_TM, width), lambda i: (i, 0))
    const = lambda shape: pl.BlockSpec((None,) + shape, lambda i: (layer,) + (0,) * len(shape),
                                       pipeline_mode=pl.Buffered(1))
    return pl.pallas_call(
        functools.partial(_out_proj_kernel, final=final),
        grid=(rows // OUT_TM,),
        in_specs=[
            pl.BlockSpec((N_SLABS, OUT_TM, LANES), lambda i: (0, i, 0)),
            pl.BlockSpec((OUT_TM, W_BRANCH), lambda i: (i, COL_ZA // N_SLABS)),
            row_blk(W_BRANCH), row_blk(W_BRANCH), row_blk(D_MODEL),
            const((MIX_WIDTH, D_MODEL)), const((W_BRANCH, W_BRANCH)),
            const((1, W_BRANCH)),
            pl.BlockSpec((None, 1, D_MODEL), lambda i: (0, 0, 0), pipeline_mode=pl.Buffered(1)),
        ],
        out_specs=row_blk(D_MODEL),
        out_shape=jax.ShapeDtypeStruct((rows, D_MODEL), F32),
        scratch_shapes=[pltpu.VMEM((MIX_WIDTH, D_MODEL), BF16), pltpu.VMEM((W_BRANCH, W_BRANCH), BF16)],
        compiler_params=pltpu.CompilerParams(
            dimension_semantics=("arbitrary",), vmem_limit_bytes=VMEM_LIMIT),
        name="out_proj_final" if final else "out_proj",
    )(ya_pre, proj, yb, yc, x2d, w, glu_w, glu_b, final_g)


def kernel(x, norm_g, w_in, w_out, ssm_lam_re, ssm_lam_im, ssm_log_dt, ssm_b_re, ssm_b_im, ssm_c_re,
           ssm_c_im, ssm_d, glu_w, glu_b, na_rpb, t5_bias, final_g):
    batch, seq, _ = x.shape
    depth = w_in.shape[0]
    x2d = x.astype(F32).reshape(batch * seq, D_MODEL)
    toep, s_in, s_out, decay = _ssm_prep(ssm_lam_re, ssm_lam_im, ssm_log_dt, ssm_b_re, ssm_b_im,
                                         ssm_c_re, ssm_c_im)
    dil_bias = _dil_bias_table(t5_bias)
    na_bias = _na_bias_table(na_rpb)
    w_in_f, w_out_f, glu_w_f = w_in.astype(F32), w_out.astype(F32), glu_w.astype(F32)
    row3 = lambda a: a.astype(F32).reshape(a.shape[0], 1, a.shape[1])
    norm_g3, ssm_d3, glu_b3 = row3(norm_g), row3(ssm_d), row3(glu_b)
    final_g3 = final_g.astype(F32).reshape(1, 1, D_MODEL)
    for l in range(depth):
        proj = _in_proj(x2d, norm_g3, w_in_f, l)
        ya_pre = _ssm(proj, toep, s_in, s_out, decay, ssm_d3, l, batch, seq)
        yb = _na(proj, na_bias, l, batch, seq)
        yc = _dil(proj, dil_bias, batch, seq)
        x2d = _out_proj(ya_pre, proj, yb, yc, x2d, w_out_f, glu_w_f, glu_b3, final_g3, l,
                        final=(l == depth - 1))
    return x2d.reshape(batch, seq, D_MODEL).astype(x.dtype)
```

```python
import functools

import numpy as np
import jax
import jax.numpy as jnp
from jax import lax
from jax.experimental import pallas as pl
from jax.experimental.pallas import tpu as pltpu

F32 = jnp.float32
BF16 = jnp.bfloat16

D_MODEL = 1024
HEAD_DIM = 64
W_BRANCH = 512
N_HEADS = W_BRANCH // HEAD_DIM
N_HEAD_PAIRS = N_HEADS // 2
SSM_GROUP = 16
SSM_GROUPS = W_BRANCH // SSM_GROUP
SSM_STATE = 64
GRID_W = 64
NA_ROWS = 8
NA_COLS = 16
DIL_PATTERNS = ((128, 1), (512, 4), (2048, 16))
DIL_HALF = 64
T5_BUCKETS = 32
T5_MAX_DIST = 1024
RMS_EPS = 1e-6
NEG_INF = -1e30
IN_COLS = 10 * W_BRANCH
MIX_WIDTH = 3 * W_BRANCH

COL_XA, COL_ZA, COL_QB, COL_KB, COL_VB, COL_ZB, COL_QC, COL_KC, COL_VC, COL_ZC = (
    4 * i for i in range(10))

LOG2E = float(np.log2(np.e))
Q_SCALE_LOG2 = HEAD_DIM ** -0.5 * LOG2E

LANES = 128
CHUNK = 16
VMEM_LIMIT = 56 * 1024 * 1024


def _sigmoid(z):
    return 0.5 * (1.0 + jnp.tanh(0.5 * z))


def _silu(z):
    return z * _sigmoid(z)


def _gelu_tanh(x):
    return 0.5 * x * (1.0 + jnp.tanh(np.sqrt(2.0 / np.pi).astype(np.float32) * (x + 0.044715 * (x * x * x))))


IN_TM = 512
IN_TN = 512


def _in_proj_kernel(x_ref, g_ref, w_ref, o_ref, w_bf):
    @pl.when(pl.program_id(0) == 0)
    def _():
        for n in range(IN_COLS // IN_TN):
            w_bf[:, n * IN_TN:(n + 1) * IN_TN] = w_ref[:, n * IN_TN:(n + 1) * IN_TN].astype(BF16)

    x = x_ref[...]
    ms = jnp.mean(x * x, axis=-1, keepdims=True)
    h = (x * lax.rsqrt(ms + RMS_EPS) * g_ref[...]).astype(BF16)
    for n in range(IN_COLS // IN_TN):
        o_ref[:, n * IN_TN:(n + 1) * IN_TN] = jnp.dot(
            h, w_bf[:, n * IN_TN:(n + 1) * IN_TN], preferred_element_type=F32).astype(BF16)


def _in_proj(x2d, g, w, layer):
    rows = x2d.shape[0]
    return pl.pallas_call(
        _in_proj_kernel,
        grid=(rows // IN_TM,),
        in_specs=[
            pl.BlockSpec((IN_TM, D_MODEL), lambda i: (i, 0)),
            pl.BlockSpec((None, 1, D_MODEL), lambda i: (layer, 0, 0)),
            pl.BlockSpec((None, D_MODEL, IN_COLS), lambda i: (layer, 0, 0), pipeline_mode=pl.Buffered(1)),
        ],
        out_specs=pl.BlockSpec((IN_TM, IN_COLS), lambda i: (i, 0)),
        out_shape=jax.ShapeDtypeStruct((rows, IN_COLS), BF16),
        scratch_shapes=[pltpu.VMEM((D_MODEL, IN_COLS), BF16)],
        compiler_params=pltpu.CompilerParams(
            dimension_semantics=("arbitrary",), vmem_limit_bytes=VMEM_LIMIT),
        name="in_proj",
    )(x2d, g, w)


def _na_bias_kernel(rpb_ref, o_ref):
    lane = lax.broadcasted_iota(jnp.int32, (GRID_W, LANES), 1)
    j = lax.broadcasted_iota(jnp.int32, (GRID_W, LANES), 0)
    c = lane & (GRID_W - 1)
    col_start = jnp.clip(j - NA_COLS // 2, 0, GRID_W - NA_COLS)
    valid = (c >= col_start) & (c < col_start + NA_COLS)
    first = lane < GRID_W
    even, odd = [], []
    for a in range(2 * NA_ROWS - 1):
        vec = pltpu.roll(rpb_ref[a:a + 1, :], LANES - (NA_COLS - 1), axis=1)
        x = jnp.broadcast_to(vec, (GRID_W, LANES))
        even.append(pltpu.roll(x, 0, axis=1, stride=1, stride_axis=0))
        odd.append(pltpu.roll(x, GRID_W, axis=1, stride=1, stride_axis=0))
    for v in range(NA_ROWS):
        for i in range(NA_ROWS // 2):
            a0 = 2 * i - v + (NA_ROWS - 1)
            tile = jnp.where(first, even[a0], odd[a0 + 1])
            o_ref[v, :, i * LANES:(i + 1) * LANES] = jnp.where(valid, tile * LOG2E, NEG_INF)


def _na_bias_table(rpb):
    depth = rpb.shape[0]
    padded = jnp.pad(rpb.astype(F32), ((0, 0), (0, 0), (0, 1), (0, LANES - (2 * NA_COLS - 1))))
    nkeys = NA_ROWS * GRID_W
    return pl.pallas_call(
        _na_bias_kernel,
        grid=(depth, N_HEADS),
        in_specs=[pl.BlockSpec((None, None, 2 * NA_ROWS, LANES), lambda l, h: (l, h, 0, 0))],
        out_specs=pl.BlockSpec((None, None, NA_ROWS, GRID_W, nkeys), lambda l, h: (l, h, 0, 0, 0)),
        out_shape=jax.ShapeDtypeStruct((depth, N_HEADS, NA_ROWS, GRID_W, nkeys), F32),
        compiler_params=pltpu.CompilerParams(dimension_semantics=("arbitrary", "arbitrary")),
        name="na_bias",
    )(padded)


def _na_kernel(q_ref, k_ref, v_ref, z_ref, bias_ref, o_ref):
    seq = q_ref.shape[0]
    rows = seq // GRID_W
    half_rows = NA_ROWS // 2
    tile_keys = half_rows * GRID_W
    lane = lax.broadcasted_iota(jnp.int32, (GRID_W, LANES), 1)
    first_head = lane < HEAD_DIM
    masks = (first_head, jnp.logical_not(first_head))
    nt = (((1,), (1,)), ((), ()))
    ones = jnp.ones((tile_keys, LANES), BF16)
    row_start = lambda r: min(max(r - NA_ROWS // 2, 0), rows - NA_ROWS)

    for g0 in range(0, rows, NA_GROUP):
        group = range(g0, g0 + NA_GROUP)
        users = {}
        for r in group:
            for half in range(2):
                users.setdefault(row_start(r) + half_rows * half, []).append((r, half))
        qm = {}
        for r in group:
            qb = (q_ref[r * GRID_W:(r + 1) * GRID_W, :].astype(F32) * Q_SCALE_LOG2).astype(BF16)
            for hd in range(2):
                qm[r, hd] = jnp.where(masks[hd], qb, jnp.zeros_like(qb))
        s_half = {}
        for k, us in sorted(users.items()):
            lhs = jnp.concatenate([qm[r, hd] for r, _ in us for hd in range(2)], axis=0)
            part = lax.dot_general(lhs, k_ref[k * GRID_W:k * GRID_W + tile_keys, :], nt,
                                   preferred_element_type=F32)
            for i, (r, half) in enumerate(us):
                for hd in range(2):
                    blk = part[(2 * i + hd) * GRID_W:(2 * i + hd + 1) * GRID_W, :]
                    bias = bias_ref[hd, r - row_start(r), :, half * tile_keys:(half + 1) * tile_keys]
                    s_half[r, hd, half] = (blk + bias).astype(BF16)
        p_half = {}
        for r in group:
            for hd in range(2):
                s0, s1 = s_half[r, hd, 0], s_half[r, hd, 1]
                m = jnp.max(jnp.maximum(s0, s1), axis=-1, keepdims=True)
                p_half[r, hd, 0], p_half[r, hd, 1] = jnp.exp2(s0 - m), jnp.exp2(s1 - m)
        acc = {}
        for k, us in sorted(users.items()):
            lhs = jnp.concatenate([p_half[r, hd, half] for r, half in us for hd in range(2)], axis=0)
            vext = jnp.concatenate([v_ref[k * GRID_W:k * GRID_W + tile_keys, :], ones], axis=-1)
            part = jnp.dot(lhs, vext, preferred_element_type=F32)
            for i, (r, half) in enumerate(us):
                for hd in range(2):
                    blk = part[(2 * i + hd) * GRID_W:(2 * i + hd + 1) * GRID_W, :]
                    acc[r, hd] = blk if (r, hd) not in acc else acc[r, hd] + blk
        for r in group:
            y = jnp.where(first_head, acc[r, 0][:, :LANES], acc[r, 1][:, :LANES])
            d = jnp.where(first_head, acc[r, 0][:, LANES:], acc[r, 1][:, LANES:])
            z = z_ref[r * GRID_W:(r + 1) * GRID_W, :].astype(F32)
            o_ref[r * GRID_W:(r + 1) * GRID_W, :] = (y * (1.0 / d) * _silu(z)).astype(BF16)


NA_GROUP = 16


def _na(proj, bias, layer, batch, seq):
    def col(c0):
        return pl.BlockSpec((seq, LANES), lambda b, p, c0=c0: (b, c0 + p))

    return pl.pallas_call(
        _na_kernel,
        grid=(batch, N_HEAD_PAIRS),
        in_specs=[
            col(COL_QB), col(COL_KB), col(COL_VB), col(COL_ZB),
            pl.BlockSpec((None, 2, NA_ROWS, GRID_W, NA_ROWS * GRID_W), lambda b, p: (layer, p, 0, 0, 0)),
        ],
        out_specs=pl.BlockSpec((seq, LANES), lambda b, p: (b, p)),
        out_shape=jax.ShapeDtypeStruct((batch * seq, W_BRANCH), BF16),
        compiler_params=pltpu.CompilerParams(
            dimension_semantics=("arbitrary", "arbitrary"), vmem_limit_bytes=VMEM_LIMIT),
        name="na_attn",
    )(proj, proj, proj, proj, bias)


DIL_QB = 128
DIL_KW = 256
N_VARIANTS = 3


def _t5_bucket(rel):
    nb = T5_BUCKETS // 2
    max_exact = nb // 2
    n = np.abs(rel)
    large = max_exact + (np.log(np.maximum(n, 1) / max_exact) / np.log(T5_MAX_DIST / max_exact)
                         * (nb - max_exact)).astype(np.int32)
    large = np.minimum(large, nb - 1)
    return (np.where(rel > 0, nb, 0) + np.where(n < max_exact, n, large)).astype(np.int32)


def _dil_bias_table(t5_bias):
    n_pat = len(DIL_PATTERNS)
    lane = np.arange(DIL_VEC)
    n = np.where(lane < DIL_KW, lane, lane - DIL_VEC)
    in_range = (lane < DIL_KW) | (lane > DIL_VEC - DIL_QB)
    pick = np.zeros((n_pat, N_VARIANTS, T5_BUCKETS, DIL_VEC), np.float32)
    neg = np.zeros((n_pat, N_VARIANTS, 1, DIL_VEC), np.float32)
    for pi, (_, d) in enumerate(DIL_PATTERNS):
        for var in range(N_VARIANTS):
            step = n - DIL_HALF * var
            ok = in_range & (np.abs(step) <= DIL_HALF)
            bucket = _t5_bucket(d * np.clip(step, -DIL_HALF, DIL_HALF))
            pick[pi, var, bucket[ok], lane[ok]] = 1.0
            neg[pi, var, 0, ~ok] = NEG_INF
    vecs = jnp.einsum('bh,pvbn->hpvn', t5_bias.astype(F32), pick,
                      precision=lax.Precision.HIGHEST) * LOG2E + neg[:, :, 0][None]
    return pl.pallas_call(
        _dil_bias_kernel,
        grid=(N_HEADS,),
        in_specs=[pl.BlockSpec((None, n_pat * N_VARIANTS, DIL_VEC), lambda h: (h, 0, 0))],
        out_specs=pl.BlockSpec((None, n_pat, N_VARIANTS, DIL_QB, DIL_KW), lambda h: (h, 0, 0, 0, 0)),
        out_shape=jax.ShapeDtypeStruct((N_HEADS, n_pat, N_VARIANTS, DIL_QB, DIL_KW), F32),
        compiler_params=pltpu.CompilerParams(dimension_semantics=("arbitrary",)),
        name="dil_bias",
    )(vecs.reshape(N_HEADS, n_pat * N_VARIANTS, DIL_VEC))


DIL_VEC = 512


def _dil_bias_kernel(vec_ref, o_ref):
    for pi in range(len(DIL_PATTERNS)):
        for var in range(N_VARIANTS):
            i = pi * N_VARIANTS + var
            x = jnp.broadcast_to(vec_ref[i:i + 1, :], (DIL_QB, DIL_VEC))
            o_ref[pi, var] = pltpu.roll(x, 0, axis=1, stride=1, stride_axis=0)[:, :DIL_KW]


DIL_GROUP = 16


def _dil_kernel(q_ref, k_ref, v_ref, z_ref, bias_ref, o_ref, qf, kf, vf, qg, kg, vg, qd,
                q4, k4, v4, m_s, l_s, acc_s):
    seq = q_ref.shape[0]
    conv_rows = 256
    lane = lax.broadcasted_iota(jnp.int32, (DIL_QB, LANES), 1)
    first_head = lane < HEAD_DIM
    nt = (((1,), (1,)), ((), ()))
    ones = jnp.ones((DIL_KW, LANES), BF16)

    def conv(i, carry):
        r0 = pl.multiple_of(i * conv_rows, conv_rows)
        qf[pl.ds(r0, conv_rows), :] = q_ref[pl.ds(r0, conv_rows), :].astype(F32) * Q_SCALE_LOG2
        kf[pl.ds(r0, conv_rows), :] = k_ref[pl.ds(r0, conv_rows), :].astype(F32)
        vf[pl.ds(r0, conv_rows), :] = v_ref[pl.ds(r0, conv_rows), :].astype(F32)
        return carry

    lax.fori_loop(0, seq // conv_rows, conv, 0, unroll=4)

    n4 = seq // 4
    n16 = seq // 16

    def to_order4(i, carry):
        dst = pl.multiple_of(i * conv_rows, conv_rows)
        src = pl.ds(dst // n4 + 4 * (dst % n4), conv_rows, stride=4)
        for a, b, c in ((qf, qg, q4), (kf, kg, k4), (vf, vg, v4)):
            part = a[src, :]
            b[pl.ds(dst, conv_rows), :] = part
            c[pl.ds(dst, conv_rows), :] = part.astype(BF16)
        return carry

    lax.fori_loop(0, seq // conv_rows, to_order4, 0, unroll=True)

    def window(src, res, start, size, n_sub):
        if n_sub == n16:
            rows = pl.ds((res % 4) * n4 + res // 4 + 4 * start, size, stride=4)
            return src[rows, :].astype(BF16)
        return src[pl.ds(pl.multiple_of(res * n_sub + start, DIL_HALF), size), :]

    order4_state = (qf, kf, vf)
    token_state = (m_s, l_s, acc_s)
    plan = ((2, 16, (qg, kg, vg), None, order4_state),
            (1, 4, (q4, k4, v4), order4_state, token_state),
            (0, 1, (qd, k_ref, v_ref), token_state, None))

    for step, (pi, d, (q_src, k_src, v_src), st_in, st_out) in enumerate(plan):
        n_sub = seq // d
        nblk = n_sub // DIL_QB

        if step == 2:
            def scaled_q(i, carry):
                r0 = pl.multiple_of(i * conv_rows, conv_rows)
                qd[pl.ds(r0, conv_rows), :] = (
                    q_ref[pl.ds(r0, conv_rows), :].astype(F32) * Q_SCALE_LOG2).astype(BF16)
                return carry

            lax.fori_loop(0, seq // conv_rows, scaled_q, 0, unroll=True)

        def rows_out(res, i0, step=step):
            if step == 0:
                return pl.ds((res % 4) * n4 + res // 4 + 4 * i0, DIL_QB, stride=4)
            if step == 1:
                return pl.ds(res + 4 * i0, DIL_QB, stride=4)
            return pl.ds(pl.multiple_of(i0, DIL_HALF), DIL_QB)

        n_group = DIL_GROUP

        def group(gi, carry, pi=pi, n_sub=n_sub, nblk=nblk, q_src=q_src, k_src=k_src, v_src=v_src,
                  st_in=st_in, st_out=st_out, rows_out=rows_out, n_group=n_group):
            units = []
            for i in range(n_group):
                u = gi * n_group + i
                res = u // nblk
                i0 = (u % nblk) * DIL_QB
                ks = jnp.clip(i0 - DIL_HALF, 0, n_sub - DIL_KW)
                units.append((res, i0, ks, (i0 - ks) // DIL_HALF))
            s_tiles, m_tiles = [], []
            for res, i0, ks, var in units:
                qb = window(q_src, res, i0, DIL_QB, n_sub)
                kw = window(k_src, res, ks, DIL_KW, n_sub)
                for hd in range(2):
                    keep = first_head if hd == 0 else jnp.logical_not(first_head)
                    qm = jnp.where(keep, qb, jnp.zeros_like(qb))
                    s = lax.dot_general(qm, kw, nt, preferred_element_type=F32) + bias_ref[hd, pi, var]
                    s = s.astype(BF16)
                    s_tiles.append(s)
                    m_tiles.append(jnp.max(s, axis=-1, keepdims=True))
            p_tiles = [jnp.exp2(s - m) for s, m in zip(s_tiles, m_tiles)]
            for i, (res, i0, ks, var) in enumerate(units):
                vw = window(v_src, res, ks, DIL_KW, n_sub)
                vext = jnp.concatenate([vw, ones], axis=-1)
                r0 = jnp.dot(p_tiles[2 * i], vext, preferred_element_type=F32)
                r1 = jnp.dot(p_tiles[2 * i + 1], vext, preferred_element_type=F32)
                m_cur = jnp.where(first_head, m_tiles[2 * i].astype(F32), m_tiles[2 * i + 1].astype(F32))
                l_cur = jnp.where(first_head, r0[:, LANES:], r1[:, LANES:])
                acc_cur = jnp.where(first_head, r0[:, :LANES], r1[:, :LANES])
                dst = rows_out(res, i0)
                if st_in is None:
                    m_new, l_new, acc_new = m_cur, l_cur, acc_cur
                else:
                    src = pl.ds(pl.multiple_of(res * n_sub + i0, DIL_HALF), DIL_QB)
                    m_old = st_in[0][src, :]
                    m_new = jnp.maximum(m_old, m_cur)
                    a_old = jnp.exp2(m_old - m_new)
                    a_cur = jnp.exp2(m_cur - m_new)
                    l_new = a_old * st_in[1][src, :] + a_cur * l_cur
                    acc_new = a_old * st_in[2][src, :] + a_cur * acc_cur
                if st_out is None:
                    z = z_ref[dst, :].astype(F32)
                    o_ref[dst, :] = (acc_new * (1.0 / l_new) * _silu(z)).astype(BF16)
                else:
                    st_out[0][dst, :] = m_new
                    st_out[1][dst, :] = l_new
                    st_out[2][dst, :] = acc_new
            return carry

        lax.fori_loop(0, seq // DIL_QB // n_group, group, 0)


def _dil(proj, bias, batch, seq):
    def col(c0):
        return pl.BlockSpec((seq, LANES), lambda b, p, c0=c0: (b, c0 + p))

    scratch = ([pltpu.VMEM((seq, LANES), F32) for _ in range(6)]
               + [pltpu.VMEM((seq, LANES), BF16) for _ in range(4)]
               + [pltpu.VMEM((seq, LANES), F32) for _ in range(3)])
    return pl.pallas_call(
        _dil_kernel,
        grid=(batch, N_HEAD_PAIRS),
        in_specs=[
            col(COL_QC), col(COL_KC), col(COL_VC), col(COL_ZC),
            pl.BlockSpec((2, len(DIL_PATTERNS), N_VARIANTS, DIL_QB, DIL_KW),
                         lambda b, p: (p, 0, 0, 0, 0)),
        ],
        out_specs=pl.BlockSpec((seq, LANES), lambda b, p: (b, p)),
        out_shape=jax.ShapeDtypeStruct((batch * seq, W_BRANCH), BF16),
        scratch_shapes=scratch,
        compiler_params=pltpu.CompilerParams(
            dimension_semantics=("arbitrary", "arbitrary"), vmem_limit_bytes=VMEM_LIMIT),
        name="dil_attn",
    )(proj, proj, proj, proj, bias)


N_POW = CHUNK + 1
PREP_GROUPS = 8


def _ssm_prep_kernel(lre_ref, lim_ref, ldt_ref, btre_ref, btim_ref, cre_ref, cim_ref,
                     toep_ref, win_ref, wout_ref, decay_ref):
    n = CHUNK * SSM_GROUP
    lane = lax.broadcasted_iota(jnp.int32, (SSM_GROUP, LANES), 1)
    lo = lane < SSM_STATE
    lane_n = lax.broadcasted_iota(jnp.int32, (SSM_GROUP, n), 1)
    nt = (((1,), (1,)), ((), ()))
    pair = lambda x, y: jnp.where(lo, x, y)

    swap = lambda x: pltpu.roll(x, SSM_STATE, axis=1)

    for j in range(PREP_GROUPS):
        xs, ys = {}, {}
        lam_re = lre_ref[0, j]
        lam_im = lim_ref[0, j]
        dt = jnp.exp(ldt_ref[0, j])
        mag = jnp.exp(lam_re * dt)
        lb_re = mag * jnp.cos(lam_im * dt)
        lb_im = mag * jnp.sin(lam_im * dt)
        n_re = lb_re - 1.0
        den = lam_re * lam_re + lam_im * lam_im
        q_re = (n_re * lam_re + lb_im * lam_im) / den
        q_im = (lb_im * lam_re - n_re * lam_im) / den
        bt_re = btre_ref[0, j]
        bt_im = btim_ref[0, j]
        bb_re = q_re * bt_re - q_im * bt_im
        bb_im = q_re * bt_im + q_im * bt_re
        c_re = cre_ref[0, j]
        c_im = cim_ref[0, j]
        pw_re = jnp.ones_like(lb_re)
        pw_im = jnp.zeros_like(lb_re)
        for k in range(N_POW):
            xs[k] = (pw_re * bb_re - pw_im * bb_im, pw_re * bb_im + pw_im * bb_re)
            ys[k] = (pw_re * c_re - pw_im * c_im, pw_re * c_im + pw_im * c_re)
            pw_last = (pw_re, pw_im)
            pw_re, pw_im = pw_re * lb_re - pw_im * lb_im, pw_re * lb_im + pw_im * lb_re

        step = [GROUPS_PER_SLAB * (a // GROUPS_PER_SLAB) + (a % GROUPS_PER_SLAB - j) % GROUPS_PER_SLAB
                for a in range(CHUNK)]
        rows = lambda a: slice(a * SSM_GROUP, (a + 1) * SSM_GROUP)

        decay_ref[0, 0, j:j + 1, :] = pw_last[0][0:1]
        decay_ref[0, 1, j:j + 1, :] = pw_last[1][0:1]

        for a, s in enumerate(step):
            win_ref[0, j, rows(a), :LANES] = pair(xs[CHUNK - 1 - s][0], xs[s][0]).astype(BF16)
            win_ref[0, j, rows(a), LANES:] = pair(xs[CHUNK - 1 - s][1], xs[s][1]).astype(BF16)

        z_re = jnp.concatenate([pair(ys[t + 1][0], ys[CHUNK - t][0]) for t in step], axis=0)
        z_nim = jnp.concatenate([pair(-ys[t + 1][1], -ys[CHUNK - t][1]) for t in step], axis=0)
        wout_ref[0, j, :LANES, :] = z_re.T.astype(BF16)
        wout_ref[0, j, LANES:, :] = z_nim.T.astype(BF16)

        kf = lax.dot_general(
            pair(bb_re, -swap(bb_im)),
            jnp.concatenate([pair(ys[m][0], swap(ys[m][1])) for m in range(CHUNK)], axis=0),
            nt, precision=lax.Precision.HIGHEST, preferred_element_type=F32)
        kb = lax.dot_general(
            pair(swap(bb_re), -bb_im),
            jnp.concatenate([pair(swap(ys[CHUNK - 1 - i][0]), ys[CHUNK - 1 - i][1])
                             for i in range(CHUNK)], axis=0),
            nt, precision=lax.Precision.HIGHEST, preferred_element_type=F32)
        for a, s in enumerate(step):
            fwd = pltpu.roll(kf, SSM_GROUP * s, axis=1) if s else kf
            shift_b = (n - SSM_GROUP * (CHUNK - 1 - s)) % n
            bwd = pltpu.roll(kb, shift_b, axis=1) if shift_b else kb
            t_nat = (jnp.where(lane_n >= SSM_GROUP * s, fwd, 0.0)
                     + jnp.where(lane_n < SSM_GROUP * (s + 1), bwd, 0.0))
            for h in range(2):
                half = t_nat[:, h * LANES:(h + 1) * LANES]
                if j:
                    half = pltpu.roll(half, SSM_GROUP * j, axis=1)
                toep_ref[0, j, rows(a), h * LANES:(h + 1) * LANES] = half.astype(BF16)


def _ssm_prep(lam_re, lam_im, log_dt, b_re, b_im, c_re, c_im):
    depth = lam_re.shape[0]
    G, P, C = SSM_GROUPS, SSM_STATE, SSM_GROUP
    n = CHUNK * C
    gb = PREP_GROUPS
    both = lambda a: jnp.concatenate([a[:, 0], a[:, 1]], axis=-1).astype(F32)
    rep = lambda a: jnp.broadcast_to(both(a)[:, :, None, :], (depth, G, C, LANES))
    ldt = rep(jnp.broadcast_to(log_dt[..., None], (depth, 2, G, P)))
    bt = lambda a: both(a.transpose(0, 1, 2, 4, 3))
    spec_in = pl.BlockSpec((1, gb, C, LANES), lambda l, i: (l, i, 0, 0))
    spec_w = pl.BlockSpec((1, gb, n, n), lambda l, i: (l, i, 0, 0))
    shp_w = jax.ShapeDtypeStruct((depth, G, n, n), BF16)
    return pl.pallas_call(
        _ssm_prep_kernel,
        grid=(depth, G // gb),
        in_specs=[spec_in] * 7,
        out_specs=[spec_w, spec_w, spec_w, pl.BlockSpec((1, 2, gb, LANES), lambda l, i: (l, 0, i, 0))],
        out_shape=[shp_w, shp_w, shp_w, jax.ShapeDtypeStruct((depth, 2, G, LANES), F32)],
        compiler_params=pltpu.CompilerParams(
            dimension_semantics=("arbitrary", "arbitrary"), vmem_limit_bytes=VMEM_LIMIT),
        name="ssm_prep",
    )(rep(lam_re), rep(lam_im), ldt, bt(b_re), bt(b_im), both(c_re), both(c_im))


GROUPS_PER_SLAB = LANES // SSM_GROUP
N_SLABS = W_BRANCH // LANES
RELAYOUT_ROWS = 32
SCAN_SLABS = 2


def _ssm_kernel(xa_ref, toep_ref, win_ref, wout_ref, decay_ref, d_ref, o_ref, xf, u_s, st, hs, yq):
    seq = xa_ref.shape[0]
    nchunk = seq // CHUNK
    nrb = nchunk // RELAYOUT_ROWS
    gps = GROUPS_PER_SLAB
    lane_rb = lax.broadcasted_iota(jnp.int32, (RELAYOUT_ROWS, LANES), 1)
    seg_masks = [(lane_rb >= SSM_GROUP * sg) & (lane_rb < SSM_GROUP * (sg + 1)) for sg in range(gps)]
    conv_rows = 512

    lane_g = lax.broadcasted_iota(jnp.int32, (gps, LANES), 1)
    fwd_lanes = lane_g < SSM_STATE
    lo, hi = slice(0, SSM_STATE), slice(SSM_STATE, LANES)

    for q0 in range(0, N_SLABS, SCAN_SLABS):
        for s in range(SCAN_SLABS):
            q = q0 + s
            c_lo = q * LANES

            def conv(i, carry, c_lo=c_lo, s=s):
                r0 = pl.multiple_of(i * conv_rows, conv_rows)
                xf[s, pl.ds(r0, conv_rows), :] = xa_ref[pl.ds(r0, conv_rows), c_lo:c_lo + LANES].astype(F32)
                return carry

            lax.fori_loop(0, seq // conv_rows, conv, 0)

            for hh in range(2):
                def fwd_relayout(rb, carry, hh=hh, s=s):
                    c0 = pl.multiple_of(rb * RELAYOUT_ROWS, RELAYOUT_ROWS)
                    rolled = []
                    for r in range(gps):
                        xs = xf[s, pl.ds(CHUNK * c0 + gps * hh + r, RELAYOUT_ROWS, stride=CHUNK), :]
                        rolled.append((xs if r == 0 else pltpu.roll(xs, SSM_GROUP * r, axis=1)).astype(BF16))
                    for j in range(gps):
                        out = rolled[(0 - j) % gps]
                        for sg in range(1, gps):
                            out = jnp.where(seg_masks[sg], rolled[(sg - j) % gps], out)
                        u_s[s, j, pl.ds(c0, RELAYOUT_ROWS), hh * LANES:(hh + 1) * LANES] = out
                    return carry

                lax.fori_loop(0, nrb, fwd_relayout, 0, unroll=True)

            for j in range(gps):
                s_in = jnp.dot(u_s[s, j], win_ref[gps * q + j], preferred_element_type=F32)
                st[s, 0, pl.ds(j, nchunk, stride=gps), :] = s_in[:, :LANES]
                st[s, 1, pl.ds(j, nchunk, stride=gps), :] = s_in[:, LANES:]

        a_re = [decay_ref[0, gps * (q0 + s):gps * (q0 + s + 1), :] for s in range(SCAN_SLABS)]
        a_im = [decay_ref[1, gps * (q0 + s):gps * (q0 + s + 1), :] for s in range(SCAN_SLABS)]

        def scan(i, carry):
            rf = pl.ds(pl.multiple_of(i * gps, gps), gps)
            rb_ = pl.ds(pl.multiple_of((nchunk - 1 - i) * gps, gps), gps)
            new = []
            for s in range(SCAN_SLABS):
                h_re, h_im = carry[2 * s], carry[2 * s + 1]
                hs[s, 0, rf, lo] = h_re[:, lo]
                hs[s, 1, rf, lo] = h_im[:, lo]
                hs[s, 0, rb_, hi] = h_re[:, hi]
                hs[s, 1, rb_, hi] = h_im[:, hi]
                in_re = jnp.where(fwd_lanes, st[s, 0, rf, :], st[s, 0, rb_, :])
                in_im = jnp.where(fwd_lanes, st[s, 1, rf, :], st[s, 1, rb_, :])
                new += [a_re[s] * h_re - a_im[s] * h_im + in_re, a_re[s] * h_im + a_im[s] * h_re + in_im]
            return tuple(new)

        zero = jnp.zeros((gps, LANES), F32)
        lax.fori_loop(0, nchunk, scan, (zero,) * (2 * SCAN_SLABS), unroll=8)

        for s in range(SCAN_SLABS):
            q = q0 + s
            c_lo = q * LANES
            for j in range(gps):
                g = gps * q + j
                h_in = jnp.concatenate([hs[s, 0, pl.ds(j, nchunk, stride=gps), :].astype(BF16),
                                        hs[s, 1, pl.ds(j, nchunk, stride=gps), :].astype(BF16)], axis=-1)
                yq[j] = (jnp.dot(u_s[s, j], toep_ref[g], preferred_element_type=F32)
                         + jnp.dot(h_in, wout_ref[g], preferred_element_type=F32))

            d_row = d_ref[:, c_lo:c_lo + LANES]
            for hh in range(2):
                def bwd_relayout(rb, carry, hh=hh, d_row=d_row, q=q, s=s):
                    c0 = pl.multiple_of(rb * RELAYOUT_ROWS, RELAYOUT_ROWS)
                    ys = [yq[j, pl.ds(c0, RELAYOUT_ROWS), hh * LANES:(hh + 1) * LANES] for j in range(gps)]
                    for r in range(gps):
                        merged = ys[(0 - r) % gps]
                        for sg in range(1, gps):
                            merged = jnp.where(seg_masks[sg], ys[(sg - r) % gps], merged)
                        if r:
                            merged = pltpu.roll(merged, LANES - SSM_GROUP * r, axis=1)
                        tok = pl.ds(CHUNK * c0 + gps * hh + r, RELAYOUT_ROWS, stride=CHUNK)
                        o_ref[q, tok, :] = merged + d_row * xf[s, tok, :]
                    return carry

                lax.fori_loop(0, nrb, bwd_relayout, 0, unroll=True)


def _ssm(proj, toep, w_in, w_out, decay, d_skip, layer, batch, seq):
    nchunk = seq // CHUNK
    n = CHUNK * SSM_GROUP
    wspec = pl.BlockSpec((None, SSM_GROUPS, n, n), lambda b: (layer, 0, 0, 0), pipeline_mode=pl.Buffered(1))
    return pl.pallas_call(
        _ssm_kernel,
        grid=(batch,),
        in_specs=[
            pl.BlockSpec((seq, W_BRANCH), lambda b: (b, COL_XA // N_SLABS)),
            wspec, wspec, wspec,
            pl.BlockSpec((None, 2, SSM_GROUPS, LANES), lambda b: (layer, 0, 0, 0)),
            pl.BlockSpec((None, 1, W_BRANCH), lambda b: (layer, 0, 0)),
        ],
        out_specs=pl.BlockSpec((N_SLABS, seq, LANES), lambda b: (0, b, 0)),
        out_shape=jax.ShapeDtypeStruct((N_SLABS, batch * seq, LANES), F32),
        scratch_shapes=[
            pltpu.VMEM((SCAN_SLABS, seq, LANES), F32),
            pltpu.VMEM((SCAN_SLABS, GROUPS_PER_SLAB, nchunk, n), BF16),
            pltpu.VMEM((SCAN_SLABS, 2, nchunk * GROUPS_PER_SLAB, LANES), F32),
            pltpu.VMEM((SCAN_SLABS, 2, nchunk * GROUPS_PER_SLAB, LANES), F32),
            pltpu.VMEM((GROUPS_PER_SLAB, nchunk, n), F32),
        ],
        compiler_params=pltpu.CompilerParams(
            dimension_semantics=("arbitrary",), vmem_limit_bytes=VMEM_LIMIT),
        name="ssm_mix",
    )(proj, toep, w_in, w_out, decay, d_skip)


OUT_TM = 1024


def _out_proj_kernel(ya_ref, za_ref, yb_ref, yc_ref, x_ref, w_ref, gw_ref, gb_ref, fg_ref, o_ref,
                     w_bf, gw_bf, *, final):
    @pl.when(pl.program_id(0) == 0)
    def _():
        for n in range(MIX_WIDTH // W_BRANCH):
            w_bf[n * W_BRANCH:(n + 1) * W_BRANCH, :] = w_ref[n * W_BRANCH:(n + 1) * W_BRANCH, :].astype(BF16)
        gw_bf[...] = gw_ref[...].astype(BF16)

    y = jnp.concatenate([ya_ref[i] for i in range(N_SLABS)], axis=-1)
    g = _gelu_tanh(y)
    gate = jnp.dot(g.astype(BF16), gw_bf[...], preferred_element_type=F32) + gb_ref[...]
    ya = g * _sigmoid(gate) * _silu(za_ref[...].astype(F32))
    delta = (jnp.dot(ya.astype(BF16), w_bf[:W_BRANCH, :], preferred_element_type=F32)
             + jnp.dot(yb_ref[...], w_bf[W_BRANCH:2 * W_BRANCH, :], preferred_element_type=F32)
             + jnp.dot(yc_ref[...], w_bf[2 * W_BRANCH:, :], preferred_element_type=F32))
    x = x_ref[...] + delta
    if final:
        ms = jnp.mean(x * x, axis=-1, keepdims=True)
        x = x * lax.rsqrt(ms + RMS_EPS) * fg_ref[...]
    o_ref[...] = x


def _out_proj(ya_pre, proj, yb, yc, x2d, w, glu_w, glu_b, final_g, layer, final):
    rows = x2d.shape[0]
    row_blk = lambda width: pl.BlockSpec((OUT_TM, width), lambda i: (i, 0))
    const = lambda shape: pl.BlockSpec((None,) + shape, lambda i: (layer,) + (0,) * len(shape),
                                       pipeline_mode=pl.Buffered(1))
    return pl.pallas_call(
        functools.partial(_out_proj_kernel, final=final),
        grid=(rows // OUT_TM,),
        in_specs=[
            pl.BlockSpec((N_SLABS, OUT_TM, LANES), lambda i: (0, i, 0)),
            pl.BlockSpec((OUT_TM, W_BRANCH), lambda i: (i, COL_ZA // N_SLABS)),
            row_blk(W_BRANCH), row_blk(W_BRANCH), row_blk(D_MODEL),
            const((MIX_WIDTH, D_MODEL)), const((W_BRANCH, W_BRANCH)),
            const((1, W_BRANCH)),
            pl.BlockSpec((None, 1, D_MODEL), lambda i: (0, 0, 0), pipeline_mode=pl.Buffered(1)),
        ],
        out_specs=row_blk(D_MODEL),
        out_shape=jax.ShapeDtypeStruct((rows, D_MODEL), F32),
        scratch_shapes=[pltpu.VMEM((MIX_WIDTH, D_MODEL), BF16), pltpu.VMEM((W_BRANCH, W_BRANCH), BF16)],
        compiler_params=pltpu.CompilerParams(
            dimension_semantics=("arbitrary",), vmem_limit_bytes=VMEM_LIMIT),
        name="out_proj_final" if final else "out_proj",
    )(ya_pre, proj, yb, yc, x2d, w, glu_w, glu_b, final_g)


def kernel(x, norm_g, w_in, w_out, ssm_lam_re, ssm_lam_im, ssm_log_dt, ssm_b_re, ssm_b_im, ssm_c_re,
           ssm_c_im, ssm_d, glu_w, glu_b, na_rpb, t5_bias, final_g):
    batch, seq, _ = x.shape
    depth = w_in.shape[0]
    x2d = x.astype(F32).reshape(batch * seq, D_MODEL)
    toep, s_in, s_out, decay = _ssm_prep(ssm_lam_re, ssm_lam_im, ssm_log_dt, ssm_b_re, ssm_b_im,
                                         ssm_c_re, ssm_c_im)
    dil_bias = _dil_bias_table(t5_bias)
    na_bias = _na_bias_table(na_rpb)
    w_in_f, w_out_f, glu_w_f = w_in.astype(F32), w_out.astype(F32), glu_w.astype(F32)
    row3 = lambda a: a.astype(F32).reshape(a.shape[0], 1, a.shape[1])
    norm_g3, ssm_d3, glu_b3 = row3(norm_g), row3(ssm_d), row3(glu_b)
    final_g3 = final_g.astype(F32).reshape(1, 1, D_MODEL)
    for l in range(depth):
        proj = _in_proj(x2d, norm_g3, w_in_f, l)
        ya_pre = _ssm(proj, toep, s_in, s_out, decay, ssm_d3, l, batch, seq)
        yb = _na(proj, na_bias, l, batch, seq)
        yc = _dil(proj, dil_bias, batch, seq)
        x2d = _out_proj(ya_pre, proj, yb, yc, x2d, w_out_f, glu_w_f, glu_b3, final_g3, l,
                        final=(l == depth - 1))
    return x2d.reshape(batch, seq, D_MODEL).astype(x.dtype)
```

```python
import functools

import numpy as np
import jax
import jax.numpy as jnp
from jax import lax
from jax.experimental import pallas as pl
from jax.experimental.pallas import tpu as pltpu

F32 = jnp.float32
BF16 = jnp.bfloat16

D_MODEL = 1024
HEAD_DIM = 64
W_BRANCH = 512
N_HEADS = W_BRANCH // HEAD_DIM
N_HEAD_PAIRS = N_HEADS // 2
SSM_GROUP = 16
SSM_GROUPS = W_BRANCH // SSM_GROUP
SSM_STATE = 64
GRID_W = 64
NA_ROWS = 8
NA_COLS = 16
DIL_PATTERNS = ((128, 1), (512, 4), (2048, 16))
DIL_HALF = 64
T5_BUCKETS = 32
T5_MAX_DIST = 1024
RMS_EPS = 1e-6
NEG_INF = -1e30
IN_COLS = 10 * W_BRANCH
MIX_WIDTH = 3 * W_BRANCH

COL_XA, COL_ZA, COL_QB, COL_KB, COL_VB, COL_ZB, COL_QC, COL_KC, COL_VC, COL_ZC = (
    4 * i for i in range(10))

LOG2E = float(np.log2(np.e))
Q_SCALE_LOG2 = HEAD_DIM ** -0.5 * LOG2E

LANES = 128
CHUNK = 16
VMEM_LIMIT = 56 * 1024 * 1024


def _sigmoid(z):
    return 0.5 * (1.0 + jnp.tanh(0.5 * z))


def _silu(z):
    return z * _sigmoid(z)


def _gelu_tanh(x):
    return 0.5 * x * (1.0 + jnp.tanh(np.sqrt(2.0 / np.pi).astype(np.float32) * (x + 0.044715 * (x * x * x))))


IN_TM = 512
IN_TN = 512


def _in_proj_kernel(x_ref, g_ref, w_ref, o_ref, w_bf):
    @pl.when(pl.program_id(0) == 0)
    def _():
        for n in range(IN_COLS // IN_TN):
            w_bf[:, n * IN_TN:(n + 1) * IN_TN] = w_ref[:, n * IN_TN:(n + 1) * IN_TN].astype(BF16)

    x = x_ref[...]
    ms = jnp.mean(x * x, axis=-1, keepdims=True)
    h = (x * lax.rsqrt(ms + RMS_EPS) * g_ref[...]).astype(BF16)
    for n in range(IN_COLS // IN_TN):
        o_ref[:, n * IN_TN:(n + 1) * IN_TN] = jnp.dot(
            h, w_bf[:, n * IN_TN:(n + 1) * IN_TN], preferred_element_type=F32).astype(BF16)


def _in_proj(x2d, g, w, layer):
    rows = x2d.shape[0]
    return pl.pallas_call(
        _in_proj_kernel,
        grid=(rows // IN_TM,),
        in_specs=[
            pl.BlockSpec((IN_TM, D_MODEL), lambda i: (i, 0)),
            pl.BlockSpec((None, 1, D_MODEL), lambda i: (layer, 0, 0)),
            pl.BlockSpec((None, D_MODEL, IN_COLS), lambda i: (layer, 0, 0), pipeline_mode=pl.Buffered(1)),
        ],
        out_specs=pl.BlockSpec((IN_TM, IN_COLS), lambda i: (i, 0)),
        out_shape=jax.ShapeDtypeStruct((rows, IN_COLS), BF16),
        scratch_shapes=[pltpu.VMEM((D_MODEL, IN_COLS), BF16)],
        compiler_params=pltpu.CompilerParams(
            dimension_semantics=("arbitrary",), vmem_limit_bytes=VMEM_LIMIT),
        name="in_proj",
    )(x2d, g, w)


def _na_bias_kernel(rpb_ref, o_ref):
    lane = lax.broadcasted_iota(jnp.int32, (GRID_W, LANES), 1)
    j = lax.broadcasted_iota(jnp.int32, (GRID_W, LANES), 0)
    c = lane & (GRID_W - 1)
    col_start = jnp.clip(j - NA_COLS // 2, 0, GRID_W - NA_COLS)
    valid = (c >= col_start) & (c < col_start + NA_COLS)
    first = lane < GRID_W
    even, odd = [], []
    for a in range(2 * NA_ROWS - 1):
        vec = pltpu.roll(rpb_ref[a:a + 1, :], LANES - (NA_COLS - 1), axis=1)
        x = jnp.broadcast_to(vec, (GRID_W, LANES))
        even.append(pltpu.roll(x, 0, axis=1, stride=1, stride_axis=0))
        odd.append(pltpu.roll(x, GRID_W, axis=1, stride=1, stride_axis=0))
    for v in range(NA_ROWS):
        for i in range(NA_ROWS // 2):
            a0 = 2 * i - v + (NA_ROWS - 1)
            tile = jnp.where(first, even[a0], odd[a0 + 1])
            o_ref[v, :, i * LANES:(i + 1) * LANES] = jnp.where(valid, tile * LOG2E, NEG_INF)


def _na_bias_table(rpb):
    depth = rpb.shape[0]
    padded = jnp.pad(rpb.astype(F32), ((0, 0), (0, 0), (0, 1), (0, LANES - (2 * NA_COLS - 1))))
    nkeys = NA_ROWS * GRID_W
    return pl.pallas_call(
        _na_bias_kernel,
        grid=(depth, N_HEADS),
        in_specs=[pl.BlockSpec((None, None, 2 * NA_ROWS, LANES), lambda l, h: (l, h, 0, 0))],
        out_specs=pl.BlockSpec((None, None, NA_ROWS, GRID_W, nkeys), lambda l, h: (l, h, 0, 0, 0)),
        out_shape=jax.ShapeDtypeStruct((depth, N_HEADS, NA_ROWS, GRID_W, nkeys), F32),
        compiler_params=pltpu.CompilerParams(dimension_semantics=("arbitrary", "arbitrary")),
        name="na_bias",
    )(padded)


def _na_kernel(q_ref, k_ref, v_ref, z_ref, bias_ref, o_ref):
    seq = q_ref.shape[0]
    rows = seq // GRID_W
    half_rows = NA_ROWS // 2
    tile_keys = half_rows * GRID_W
    lane = lax.broadcasted_iota(jnp.int32, (GRID_W, LANES), 1)
    first_head = lane < HEAD_DIM
    masks = (first_head, jnp.logical_not(first_head))
    nt = (((1,), (1,)), ((), ()))
    ones = jnp.ones((tile_keys, LANES), BF16)
    row_start = lambda r: min(max(r - NA_ROWS // 2, 0), rows - NA_ROWS)

    for g0 in range(0, rows, NA_GROUP):
        group = range(g0, g0 + NA_GROUP)
        users = {}
        for r in group:
            for half in range(2):
                users.setdefault(row_start(r) + half_rows * half, []).append((r, half))
        qm = {}
        for r in group:
            qb = (q_ref[r * GRID_W:(r + 1) * GRID_W, :].astype(F32) * Q_SCALE_LOG2).astype(BF16)
            for hd in range(2):
                qm[r, hd] = jnp.where(masks[hd], qb, jnp.zeros_like(qb))
        s_half = {}
        for k, us in sorted(users.items()):
            lhs = jnp.concatenate([qm[r, hd] for r, _ in us for hd in range(2)], axis=0)
            part = lax.dot_general(lhs, k_ref[k * GRID_W:k * GRID_W + tile_keys, :], nt,
                                   preferred_element_type=F32)
            for i, (r, half) in enumerate(us):
                for hd in range(2):
                    blk = part[(2 * i + hd) * GRID_W:(2 * i + hd + 1) * GRID_W, :]
                    bias = bias_ref[hd, r - row_start(r), :, half * tile_keys:(half + 1) * tile_keys]
                    s_half[r, hd, half] = (blk + bias).astype(BF16)
        p_half = {}
        for r in group:
            for hd in range(2):
                s0, s1 = s_half[r, hd, 0], s_half[r, hd, 1]
                m = jnp.max(jnp.maximum(s0, s1), axis=-1, keepdims=True)
                p_half[r, hd, 0], p_half[r, hd, 1] = jnp.exp2(s0 - m), jnp.exp2(s1 - m)
        acc = {}
        for k, us in sorted(users.items()):
            lhs = jnp.concatenate([p_half[r, hd, half] for r, half in us for hd in range(2)], axis=0)
            vext = jnp.concatenate([v_ref[k * GRID_W:k * GRID_W + tile_keys, :], ones], axis=-1)
            part = jnp.dot(lhs, vext, preferred_element_type=F32)
            for i, (r, half) in enumerate(us):
                for hd in range(2):
                    blk = part[(2 * i + hd) * GRID_W:(2 * i + hd + 1) * GRID_W, :]
                    acc[r, hd] = blk if (r, hd) not in acc else acc[r, hd] + blk
        for r in group:
            y = jnp.where(first_head, acc[r, 0][:, :LANES], acc[r, 1][:, :LANES])
            d = jnp.where(first_head, acc[r, 0][:, LANES:], acc[r, 1][:, LANES:])
            z = z_ref[r * GRID_W:(r + 1) * GRID_W, :].astype(F32)
            o_ref[r * GRID_W:(r + 1) * GRID_W, :] = (y * (1.0 / d) * _silu(z)).astype(BF16)


NA_GROUP = 16


def _na(proj, bias, layer, batch, seq):
    def col(c0):
        return pl.BlockSpec((seq, LANES), lambda b, p, c0=c0: (b, c0 + p))

    return pl.pallas_call(
        _na_kernel,
        grid=(batch, N_HEAD_PAIRS),
        in_specs=[
            col(COL_QB), col(COL_KB), col(COL_VB), col(COL_ZB),
            pl.BlockSpec((None, 2, NA_ROWS, GRID_W, NA_ROWS * GRID_W), lambda b, p: (layer, p, 0, 0, 0)),
        ],
        out_specs=pl.BlockSpec((seq, LANES), lambda b, p: (b, p)),
        out_shape=jax.ShapeDtypeStruct((batch * seq, W_BRANCH), BF16),
        compiler_params=pltpu.CompilerParams(
            dimension_semantics=("arbitrary", "arbitrary"), vmem_limit_bytes=VMEM_LIMIT),
        name="na_attn",
    )(proj, proj, proj, proj, bias)


DIL_QB = 128
DIL_KW = 256
N_VARIANTS = 3


def _t5_bucket(rel):
    nb = T5_BUCKETS // 2
    max_exact = nb // 2
    n = np.abs(rel)
    large = max_exact + (np.log(np.maximum(n, 1) / max_exact) / np.log(T5_MAX_DIST / max_exact)
                         * (nb - max_exact)).astype(np.int32)
    large = np.minimum(large, nb - 1)
    return (np.where(rel > 0, nb, 0) + np.where(n < max_exact, n, large)).astype(np.int32)


def _dil_bias_table(t5_bias):
    n_pat = len(DIL_PATTERNS)
    lane = np.arange(DIL_VEC)
    n = np.where(lane < DIL_KW, lane, lane - DIL_VEC)
    in_range = (lane < DIL_KW) | (lane > DIL_VEC - DIL_QB)
    pick = np.zeros((n_pat, N_VARIANTS, T5_BUCKETS, DIL_VEC), np.float32)
    neg = np.zeros((n_pat, N_VARIANTS, 1, DIL_VEC), np.float32)
    for pi, (_, d) in enumerate(DIL_PATTERNS):
        for var in range(N_VARIANTS):
            step = n - DIL_HALF * var
            ok = in_range & (np.abs(step) <= DIL_HALF)
            bucket = _t5_bucket(d * np.clip(step, -DIL_HALF, DIL_HALF))
            pick[pi, var, bucket[ok], lane[ok]] = 1.0
            neg[pi, var, 0, ~ok] = NEG_INF
    vecs = jnp.einsum('bh,pvbn->hpvn', t5_bias.astype(F32), pick,
                      precision=lax.Precision.HIGHEST) * LOG2E + neg[:, :, 0][None]
    return pl.pallas_call(
        _dil_bias_kernel,
        grid=(N_HEADS,),
        in_specs=[pl.BlockSpec((None, n_pat * N_VARIANTS, DIL_VEC), lambda h: (h, 0, 0))],
        out_specs=pl.BlockSpec((None, n_pat, N_VARIANTS, DIL_QB, DIL_KW), lambda h: (h, 0, 0, 0, 0)),
        out_shape=jax.ShapeDtypeStruct((N_HEADS, n_pat, N_VARIANTS, DIL_QB, DIL_KW), F32),
        compiler_params=pltpu.CompilerParams(dimension_semantics=("arbitrary",)),
        name="dil_bias",
    )(vecs.reshape(N_HEADS, n_pat * N_VARIANTS, DIL_VEC))


DIL_VEC = 512


def _dil_bias_kernel(vec_ref, o_ref):
    for pi in range(len(DIL_PATTERNS)):
        for var in range(N_VARIANTS):
            i = pi * N_VARIANTS + var
            x = jnp.broadcast_to(vec_ref[i:i + 1, :], (DIL_QB, DIL_VEC))
            o_ref[pi, var] = pltpu.roll(x, 0, axis=1, stride=1, stride_axis=0)[:, :DIL_KW]


DIL_GROUP = 16


def _dil_kernel(q_ref, k_ref, v_ref, z_ref, bias_ref, o_ref, qf, kf, vf, qg, kg, vg, qd,
                q4, k4, v4, m_s, l_s, acc_s):
    seq = q_ref.shape[0]
    conv_rows = 256
    lane = lax.broadcasted_iota(jnp.int32, (DIL_QB, LANES), 1)
    first_head = lane < HEAD_DIM
    nt = (((1,), (1,)), ((), ()))
    ones = jnp.ones((DIL_KW, LANES), BF16)

    def conv(i, carry):
        r0 = pl.multiple_of(i * conv_rows, conv_rows)
        qf[pl.ds(r0, conv_rows), :] = q_ref[pl.ds(r0, conv_rows), :].astype(F32) * Q_SCALE_LOG2
        kf[pl.ds(r0, conv_rows), :] = k_ref[pl.ds(r0, conv_rows), :].astype(F32)
        vf[pl.ds(r0, conv_rows), :] = v_ref[pl.ds(r0, conv_rows), :].astype(F32)
        return carry

    lax.fori_loop(0, seq // conv_rows, conv, 0, unroll=4)

    n4 = seq // 4
    n16 = seq // 16

    def to_order4(i, carry):
        dst = pl.multiple_of(i * conv_rows, conv_rows)
        src = pl.ds(dst // n4 + 4 * (dst % n4), conv_rows, stride=4)
        for a, b, c in ((qf, qg, q4), (kf, kg, k4), (vf, vg, v4)):
            part = a[src, :]
            b[pl.ds(dst, conv_rows), :] = part
            c[pl.ds(dst, conv_rows), :] = part.astype(BF16)
        return carry

    lax.fori_loop(0, seq // conv_rows, to_order4, 0, unroll=True)

    def window(src, res, start, size, n_sub):
        if n_sub == n16:
            rows = pl.ds((res % 4) * n4 + res // 4 + 4 * start, size, stride=4)
            return src[rows, :].astype(BF16)
        return src[pl.ds(pl.multiple_of(res * n_sub + start, DIL_HALF), size), :]

    order4_state = (qf, kf, vf)
    token_state = (m_s, l_s, acc_s)
    plan = ((2, 16, (qg, kg, vg), None, order4_state),
            (1, 4, (q4, k4, v4), order4_state, token_state),
            (0, 1, (qd, k_ref, v_ref), token_state, None))

    for step, (pi, d, (q_src, k_src, v_src), st_in, st_out) in enumerate(plan):
        n_sub = seq // d
        nblk = n_sub // DIL_QB

        if step == 2:
            def scaled_q(i, carry):
                r0 = pl.multiple_of(i * conv_rows, conv_rows)
                qd[pl.ds(r0, conv_rows), :] = (
                    q_ref[pl.ds(r0, conv_rows), :].astype(F32) * Q_SCALE_LOG2).astype(BF16)
                return carry

            lax.fori_loop(0, seq // conv_rows, scaled_q, 0, unroll=True)

        def rows_out(res, i0, step=step):
            if step == 0:
                return pl.ds((res % 4) * n4 + res // 4 + 4 * i0, DIL_QB, stride=4)
            if step == 1:
                return pl.ds(res + 4 * i0, DIL_QB, stride=4)
            return pl.ds(pl.multiple_of(i0, DIL_HALF), DIL_QB)

        n_group = DIL_GROUP

        def group(gi, carry, pi=pi, n_sub=n_sub, nblk=nblk, q_src=q_src, k_src=k_src, v_src=v_src,
                  st_in=st_in, st_out=st_out, rows_out=rows_out, n_group=n_group):
            units = []
            for i in range(n_group):
                u = gi * n_group + i
                res = u // nblk
                i0 = (u % nblk) * DIL_QB
                ks = jnp.clip(i0 - DIL_HALF, 0, n_sub - DIL_KW)
                units.append((res, i0, ks, (i0 - ks) // DIL_HALF))
            s_tiles, m_tiles = [], []
            for res, i0, ks, var in units:
                qb = window(q_src, res, i0, DIL_QB, n_sub)
                kw = window(k_src, res, ks, DIL_KW, n_sub)
                for hd in range(2):
                    keep = first_head if hd == 0 else jnp.logical_not(first_head)
                    qm = jnp.where(keep, qb, jnp.zeros_like(qb))
                    s = lax.dot_general(qm, kw, nt, preferred_element_type=F32) + bias_ref[hd, pi, var]
                    s = s.astype(BF16)
                    s_tiles.append(s)
                    m_tiles.append(jnp.max(s, axis=-1, keepdims=True))
            p_tiles = [jnp.exp2(s - m) for s, m in zip(s_tiles, m_tiles)]
            for i, (res, i0, ks, var) in enumerate(units):
                vw = window(v_src, res, ks, DIL_KW, n_sub)
                vext = jnp.concatenate([vw, ones], axis=-1)
                r0 = jnp.dot(p_tiles[2 * i], vext, preferred_element_type=F32)
                r1 = jnp.dot(p_tiles[2 * i + 1], vext, preferred_element_type=F32)
                m_cur = jnp.where(first_head, m_tiles[2 * i].astype(F32), m_tiles[2 * i + 1].astype(F32))
                l_cur = jnp.where(first_head, r0[:, LANES:], r1[:, LANES:])
                acc_cur = jnp.where(first_head, r0[:, :LANES], r1[:, :LANES])
                dst = rows_out(res, i0)
                if st_in is None:
                    m_new, l_new, acc_new = m_cur, l_cur, acc_cur
                else:
                    src = pl.ds(pl.multiple_of(res * n_sub + i0, DIL_HALF), DIL_QB)
                    m_old = st_in[0][src, :]
                    m_new = jnp.maximum(m_old, m_cur)
                    a_old = jnp.exp2(m_old - m_new)
                    a_cur = jnp.exp2(m_cur - m_new)
                    l_new = a_old * st_in[1][src, :] + a_cur * l_cur
                    acc_new = a_old * st_in[2][src, :] + a_cur * acc_cur
                if st_out is None:
                    z = z_ref[dst, :].astype(F32)
                    o_ref[dst, :] = (acc_new * (1.0 / l_new) * _silu(z)).astype(BF16)
                else:
                    st_out[0][dst, :] = m_new
                    st_out[1][dst, :] = l_new
                    st_out[2][dst, :] = acc_new
            return carry

        lax.fori_loop(0, seq // DIL_QB // n_group, group, 0, unroll=True)


def _dil(proj, bias, batch, seq):
    def col(c0):
        return pl.BlockSpec((seq, LANES), lambda b, p, c0=c0: (b, c0 + p))

    scratch = ([pltpu.VMEM((seq, LANES), F32) for _ in range(6)]
               + [pltpu.VMEM((seq, LANES), BF16) for _ in range(4)]
               + [pltpu.VMEM((seq, LANES), F32) for _ in range(3)])
    return pl.pallas_call(
        _dil_kernel,
        grid=(batch, N_HEAD_PAIRS),
        in_specs=[
            col(COL_QC), col(COL_KC), col(COL_VC), col(COL_ZC),
            pl.BlockSpec((2, len(DIL_PATTERNS), N_VARIANTS, DIL_QB, DIL_KW),
                         lambda b, p: (p, 0, 0, 0, 0)),
        ],
        out_specs=pl.BlockSpec((seq, LANES), lambda b, p: (b, p)),
        out_shape=jax.ShapeDtypeStruct((batch * seq, W_BRANCH), BF16),
        scratch_shapes=scratch,
        compiler_params=pltpu.CompilerParams(
            dimension_semantics=("arbitrary", "arbitrary"), vmem_limit_bytes=VMEM_LIMIT),
        name="dil_attn",
    )(proj, proj, proj, proj, bias)


N_POW = CHUNK + 1
PREP_GROUPS = 8


def _ssm_prep_kernel(lre_ref, lim_ref, ldt_ref, btre_ref, btim_ref, cre_ref, cim_ref,
                     toep_ref, win_ref, wout_ref, decay_ref):
    n = CHUNK * SSM_GROUP
    lane = lax.broadcasted_iota(jnp.int32, (SSM_GROUP, LANES), 1)
    lo = lane < SSM_STATE
    lane_n = lax.broadcasted_iota(jnp.int32, (SSM_GROUP, n), 1)
    nt = (((1,), (1,)), ((), ()))
    pair = lambda x, y: jnp.where(lo, x, y)

    swap = lambda x: pltpu.roll(x, SSM_STATE, axis=1)

    for j in range(PREP_GROUPS):
        xs, ys = {}, {}
        lam_re = lre_ref[0, j]
        lam_im = lim_ref[0, j]
        dt = jnp.exp(ldt_ref[0, j])
        mag = jnp.exp(lam_re * dt)
        lb_re = mag * jnp.cos(lam_im * dt)
        lb_im = mag * jnp.sin(lam_im * dt)
        n_re = lb_re - 1.0
        den = lam_re * lam_re + lam_im * lam_im
        q_re = (n_re * lam_re + lb_im * lam_im) / den
        q_im = (lb_im * lam_re - n_re * lam_im) / den
        bt_re = btre_ref[0, j]
        bt_im = btim_ref[0, j]
        bb_re = q_re * bt_re - q_im * bt_im
        bb_im = q_re * bt_im + q_im * bt_re
        c_re = cre_ref[0, j]
        c_im = cim_ref[0, j]
        pw_re = jnp.ones_like(lb_re)
        pw_im = jnp.zeros_like(lb_re)
        for k in range(N_POW):
            xs[k] = (pw_re * bb_re - pw_im * bb_im, pw_re * bb_im + pw_im * bb_re)
            ys[k] = (pw_re * c_re - pw_im * c_im, pw_re * c_im + pw_im * c_re)
            pw_last = (pw_re, pw_im)
            pw_re, pw_im = pw_re * lb_re - pw_im * lb_im, pw_re * lb_im + pw_im * lb_re

        step = [GROUPS_PER_SLAB * (a // GROUPS_PER_SLAB) + (a % GROUPS_PER_SLAB - j) % GROUPS_PER_SLAB
                for a in range(CHUNK)]
        rows = lambda a: slice(a * SSM_GROUP, (a + 1) * SSM_GROUP)

        decay_ref[0, 0, j:j + 1, :] = pw_last[0][0:1]
        decay_ref[0, 1, j:j + 1, :] = pw_last[1][0:1]

        for a, s in enumerate(step):
            win_ref[0, j, rows(a), :LANES] = pair(xs[CHUNK - 1 - s][0], xs[s][0]).astype(BF16)
            win_ref[0, j, rows(a), LANES:] = pair(xs[CHUNK - 1 - s][1], xs[s][1]).astype(BF16)

        z_re = jnp.concatenate([pair(ys[t + 1][0], ys[CHUNK - t][0]) for t in step], axis=0)
        z_nim = jnp.concatenate([pair(-ys[t + 1][1], -ys[CHUNK - t][1]) for t in step], axis=0)
        wout_ref[0, j, :LANES, :] = z_re.T.astype(BF16)
        wout_ref[0, j, LANES:, :] = z_nim.T.astype(BF16)

        kf = lax.dot_general(
            pair(bb_re, -swap(bb_im)),
            jnp.concatenate([pair(ys[m][0], swap(ys[m][1])) for m in range(CHUNK)], axis=0),
            nt, precision=lax.Precision.HIGHEST, preferred_element_type=F32)
        kb = lax.dot_general(
            pair(swap(bb_re), -bb_im),
            jnp.concatenate([pair(swap(ys[CHUNK - 1 - i][0]), ys[CHUNK - 1 - i][1])
                             for i in range(CHUNK)], axis=0),
            nt, precision=lax.Precision.HIGHEST, preferred_element_type=F32)
        for a, s in enumerate(step):
            fwd = pltpu.roll(kf, SSM_GROUP * s, axis=1) if s else kf
            shift_b = (n - SSM_GROUP * (CHUNK - 1 - s)) % n
            bwd = pltpu.roll(kb, shift_b, axis=1) if shift_b else kb
            t_nat = (jnp.where(lane_n >= SSM_GROUP * s, fwd, 0.0)
                     + jnp.where(lane_n < SSM_GROUP * (s + 1), bwd, 0.0))
            for h in range(2):
                half = t_nat[:, h * LANES:(h + 1) * LANES]
                if j:
                    half = pltpu.roll(half, SSM_GROUP * j, axis=1)
                toep_ref[0, j, rows(a), h * LANES:(h + 1) * LANES] = half.astype(BF16)


def _ssm_prep(lam_re, lam_im, log_dt, b_re, b_im, c_re, c_im):
    depth = lam_re.shape[0]
    G, P, C = SSM_GROUPS, SSM_STATE, SSM_GROUP
    n = CHUNK * C
    gb = PREP_GROUPS
    both = lambda a: jnp.concatenate([a[:, 0], a[:, 1]], axis=-1).astype(F32)
    rep = lambda a: jnp.broadcast_to(both(a)[:, :, None, :], (depth, G, C, LANES))
    ldt = rep(jnp.broadcast_to(log_dt[..., None], (depth, 2, G, P)))
    bt = lambda a: both(a.transpose(0, 1, 2, 4, 3))
    spec_in = pl.BlockSpec((1, gb, C, LANES), lambda l, i: (l, i, 0, 0))
    spec_w = pl.BlockSpec((1, gb, n, n), lambda l, i: (l, i, 0, 0))
    shp_w = jax.ShapeDtypeStruct((depth, G, n, n), BF16)
    return pl.pallas_call(
        _ssm_prep_kernel,
        grid=(depth, G // gb),
        in_specs=[spec_in] * 7,
        out_specs=[spec_w, spec_w, spec_w, pl.BlockSpec((1, 2, gb, LANES), lambda l, i: (l, 0, i, 0))],
        out_shape=[shp_w, shp_w, shp_w, jax.ShapeDtypeStruct((depth, 2, G, LANES), F32)],
        compiler_params=pltpu.CompilerParams(
            dimension_semantics=("arbitrary", "arbitrary"), vmem_limit_bytes=VMEM_LIMIT),
        name="ssm_prep",
    )(rep(lam_re), rep(lam_im), ldt, bt(b_re), bt(b_im), both(c_re), both(c_im))


GROUPS_PER_SLAB = LANES // SSM_GROUP
N_SLABS = W_BRANCH // LANES
RELAYOUT_ROWS = 32
SCAN_SLABS = 2


def _ssm_kernel(xa_ref, toep_ref, win_ref, wout_ref, decay_ref, d_ref, o_ref, xf, u_s, st, hs, yq):
    seq = xa_ref.shape[0]
    nchunk = seq // CHUNK
    nrb = nchunk // RELAYOUT_ROWS
    gps = GROUPS_PER_SLAB
    lane_rb = lax.broadcasted_iota(jnp.int32, (RELAYOUT_ROWS, LANES), 1)
    seg_masks = [(lane_rb >= SSM_GROUP * sg) & (lane_rb < SSM_GROUP * (sg + 1)) for sg in range(gps)]
    conv_rows = 512

    lane_g = lax.broadcasted_iota(jnp.int32, (gps, LANES), 1)
    fwd_lanes = lane_g < SSM_STATE
    lo, hi = slice(0, SSM_STATE), slice(SSM_STATE, LANES)

    for q0 in range(0, N_SLABS, SCAN_SLABS):
        for s in range(SCAN_SLABS):
            q = q0 + s
            c_lo = q * LANES

            def conv(i, carry, c_lo=c_lo, s=s):
                r0 = pl.multiple_of(i * conv_rows, conv_rows)
                xf[s, pl.ds(r0, conv_rows), :] = xa_ref[pl.ds(r0, conv_rows), c_lo:c_lo + LANES].astype(F32)
                return carry

            lax.fori_loop(0, seq // conv_rows, conv, 0)

            for hh in range(2):
                def fwd_relayout(rb, carry, hh=hh, s=s):
                    c0 = pl.multiple_of(rb * RELAYOUT_ROWS, RELAYOUT_ROWS)
                    rolled = []
                    for r in range(gps):
                        xs = xf[s, pl.ds(CHUNK * c0 + gps * hh + r, RELAYOUT_ROWS, stride=CHUNK), :]
                        rolled.append((xs if r == 0 else pltpu.roll(xs, SSM_GROUP * r, axis=1)).astype(BF16))
                    for j in range(gps):
                        out = rolled[(0 - j) % gps]
                        for sg in range(1, gps):
                            out = jnp.where(seg_masks[sg], rolled[(sg - j) % gps], out)
                        u_s[s, j, pl.ds(c0, RELAYOUT_ROWS), hh * LANES:(hh + 1) * LANES] = out
                    return carry

                lax.fori_loop(0, nrb, fwd_relayout, 0, unroll=True)

            for j in range(gps):
                s_in = jnp.dot(u_s[s, j], win_ref[gps * q + j], preferred_element_type=F32)
                st[s, 0, pl.ds(j, nchunk, stride=gps), :] = s_in[:, :LANES]
                st[s, 1, pl.ds(j, nchunk, stride=gps), :] = s_in[:, LANES:]

        a_re = [decay_ref[0, gps * (q0 + s):gps * (q0 + s + 1), :] for s in range(SCAN_SLABS)]
        a_im = [decay_ref[1, gps * (q0 + s):gps * (q0 + s + 1), :] for s in range(SCAN_SLABS)]

        def scan(i, carry):
            rf = pl.ds(pl.multiple_of(i * gps, gps), gps)
            rb_ = pl.ds(pl.multiple_of((nchunk - 1 - i) * gps, gps), gps)
            new = []
            for s in range(SCAN_SLABS):
                h_re, h_im = carry[2 * s], carry[2 * s + 1]
                hs[s, 0, rf, lo] = h_re[:, lo]
                hs[s, 1, rf, lo] = h_im[:, lo]
                hs[s, 0, rb_, hi] = h_re[:, hi]
                hs[s, 1, rb_, hi] = h_im[:, hi]
                in_re = jnp.where(fwd_lanes, st[s, 0, rf, :], st[s, 0, rb_, :])
                in_im = jnp.where(fwd_lanes, st[s, 1, rf, :], st[s, 1, rb_, :])
                new += [a_re[s] * h_re - a_im[s] * h_im + in_re, a_re[s] * h_im + a_im[s] * h_re + in_im]
            return tuple(new)

        zero = jnp.zeros((gps, LANES), F32)
        lax.fori_loop(0, nchunk, scan, (zero,) * (2 * SCAN_SLABS), unroll=8)

        for s in range(SCAN_SLABS):
            q = q0 + s
            c_lo = q * LANES
            for j in range(gps):
                g = gps * q + j
                h_in = jnp.concatenate([hs[s, 0, pl.ds(j, nchunk, stride=gps), :].astype(BF16),
                                        hs[s, 1, pl.ds(j, nchunk, stride=gps), :].astype(BF16)], axis=-1)
                yq[j] = (jnp.dot(u_s[s, j], toep_ref[g], preferred_element_type=F32)
                         + jnp.dot(h_in, wout_ref[g], preferred_element_type=F32))

            d_row = d_ref[:, c_lo:c_lo + LANES]
            for hh in range(2):
                def bwd_relayout(rb, carry, hh=hh, d_row=d_row, q=q, s=s):
                    c0 = pl.multiple_of(rb * RELAYOUT_ROWS, RELAYOUT_ROWS)
                    ys = [yq[j, pl.ds(c0, RELAYOUT_ROWS), hh * LANES:(hh + 1) * LANES] for j in range(gps)]
                    for r in range(gps):
                        merged = ys[(0 - r) % gps]
                        for sg in range(1, gps):
                            merged = jnp.where(seg_masks[sg], ys[(sg - r) % gps], merged)
                        if r:
                            merged = pltpu.roll(merged, LANES - SSM_GROUP * r, axis=1)
                        tok = pl.ds(CHUNK * c0 + gps * hh + r, RELAYOUT_ROWS, stride=CHUNK)
                        o_ref[q, tok, :] = merged + d_row * xf[s, tok, :]
                    return carry

                lax.fori_loop(0, nrb, bwd_relayout, 0, unroll=True)


def _ssm(proj, toep, w_in, w_out, decay, d_skip, layer, batch, seq):
    nchunk = seq // CHUNK
    n = CHUNK * SSM_GROUP
    wspec = pl.BlockSpec((None, SSM_GROUPS, n, n), lambda b: (layer, 0, 0, 0), pipeline_mode=pl.Buffered(1))
    return pl.pallas_call(
        _ssm_kernel,
        grid=(batch,),
        in_specs=[
            pl.BlockSpec((seq, W_BRANCH), lambda b: (b, COL_XA // N_SLABS)),
            wspec, wspec, wspec,
            pl.BlockSpec((None, 2, SSM_GROUPS, LANES), lambda b: (layer, 0, 0, 0)),
            pl.BlockSpec((None, 1, W_BRANCH), lambda b: (layer, 0, 0)),
        ],
        out_specs=pl.BlockSpec((N_SLABS, seq, LANES), lambda b: (0, b, 0)),
        out_shape=jax.ShapeDtypeStruct((N_SLABS, batch * seq, LANES), F32),
        scratch_shapes=[
            pltpu.VMEM((SCAN_SLABS, seq, LANES), F32),
            pltpu.VMEM((SCAN_SLABS, GROUPS_PER_SLAB, nchunk, n), BF16),
            pltpu.VMEM((SCAN_SLABS, 2, nchunk * GROUPS_PER_SLAB, LANES), F32),
            pltpu.VMEM((SCAN_SLABS, 2, nchunk * GROUPS_PER_SLAB, LANES), F32),
            pltpu.VMEM((GROUPS_PER_SLAB, nchunk, n), F32),
        ],
        compiler_params=pltpu.CompilerParams(
            dimension_semantics=("arbitrary",), vmem_limit_bytes=VMEM_LIMIT),
        name="ssm_mix",
    )(proj, toep, w_in, w_out, decay, d_skip)


OUT_TM = 1024


def _out_proj_kernel(ya_ref, za_ref, yb_ref, yc_ref, x_ref, w_ref, gw_ref, gb_ref, fg_ref, o_ref,
                     w_bf, gw_bf, *, final):
    @pl.when(pl.program_id(0) == 0)
    def _():
        for n in range(MIX_WIDTH // W_BRANCH):
            w_bf[n * W_BRANCH:(n + 1) * W_BRANCH, :] = w_ref[n * W_BRANCH:(n + 1) * W_BRANCH, :].astype(BF16)
        gw_bf[...] = gw_ref[...].astype(BF16)

    y = jnp.concatenate([ya_ref[i] for i in range(N_SLABS)], axis=-1)
    g = _gelu_tanh(y)
    gate = jnp.dot(g.astype(BF16), gw_bf[...], preferred_element_type=F32) + gb_ref[...]
    ya = g * _sigmoid(gate) * _silu(za_ref[...].astype(F32))
    delta = (jnp.dot(ya.astype(BF16), w_bf[:W_BRANCH, :], preferred_element_type=F32)
             + jnp.dot(yb_ref[...], w_bf[W_BRANCH:2 * W_BRANCH, :], preferred_element_type=F32)
             + jnp.dot(yc_ref[...], w_bf[2 * W_BRANCH:, :], preferred_element_type=F32))
    x = x_ref[...] + delta
    if final:
        ms = jnp.mean(x * x, axis=-1, keepdims=True)
        x = x * lax.rsqrt(ms + RMS_EPS) * fg_ref[...]
    o_ref[...] = x


def _out_proj(ya_pre, proj, yb, yc, x2d, w, glu_w, glu_b, final_g, layer, final):
    rows = x2d.shape[0]
    row_blk = lambda width: pl.BlockSpec((OUT_TM, width), lambda i: (i, 0))
    const = lambda shape: pl.BlockSpec((None,) + shape, lambda i: (layer,) + (0,) * len(shape),
                                       pipeline_mode=pl.Buffered(1))
    return pl.pallas_call(
        functools.partial(_out_proj_kernel, final=final),
        grid=(rows // OUT_TM,),
        in_specs=[
            pl.BlockSpec((N_SLABS, OUT_TM, LANES), lambda i: (0, i, 0)),
            pl.BlockSpec((OUT_TM, W_BRANCH), lambda i: (i, COL_ZA // N_SLABS)),
            row_blk(W_BRANCH), row_blk(W_BRANCH), row_blk(D_MODEL),
            const((MIX_WIDTH, D_MODEL)), const((W_BRANCH, W_BRANCH)),
            const((1, W_BRANCH)),
            pl.BlockSpec((None, 1, D_MODEL), lambda i: (0, 0, 0), pipeline_mode=pl.Buffered(1)),
        ],
        out_specs=row_blk(D_MODEL),
        out_shape=jax.ShapeDtypeStruct((rows, D_MODEL), F32),
        scratch_shapes=[pltpu.VMEM((MIX_WIDTH, D_MODEL), BF16), pltpu.VMEM((W_BRANCH, W_BRANCH), BF16)],
        compiler_params=pltpu.CompilerParams(
            dimension_semantics=("arbitrary",), vmem_limit_bytes=VMEM_LIMIT),
        name="out_proj_final" if final else "out_proj",
    )(ya_pre, proj, yb, yc, x2d, w, glu_w, glu_b, final_g)


def kernel(x, norm_g, w_in, w_out, ssm_lam_re, ssm_lam_im, ssm_log_dt, ssm_b_re, ssm_b_im, ssm_c_re,
           ssm_c_im, ssm_d, glu_w, glu_b, na_rpb, t5_bias, final_g):
    batch, seq, _ = x.shape
    depth = w_in.shape[0]
    x2d = x.astype(F32).reshape(batch * seq, D_MODEL)
    toep, s_in, s_out, decay = _ssm_prep(ssm_lam_re, ssm_lam_im, ssm_log_dt, ssm_b_re, ssm_b_im,
                                         ssm_c_re, ssm_c_im)
    dil_bias = _dil_bias_table(t5_bias)
    na_bias = _na_bias_table(na_rpb)
    w_in_f, w_out_f, glu_w_f = w_in.astype(F32), w_out.astype(F32), glu_w.astype(F32)
    row3 = lambda a: a.astype(F32).reshape(a.shape[0], 1, a.shape[1])
    norm_g3, ssm_d3, glu_b3 = row3(norm_g), row3(ssm_d), row3(glu_b)
    final_g3 = final_g.astype(F32).reshape(1, 1, D_MODEL)
    for l in range(depth):
        proj = _in_proj(x2d, norm_g3, w_in_f, l)
        ya_pre = _ssm(proj, toep, s_in, s_out, decay, ssm_d3, l, batch, seq)
        yb = _na(proj, na_bias, l, batch, seq)
        yc = _dil(proj, dil_bias, batch, seq)
        x2d = _out_proj(ya_pre, proj, yb, yc, x2d, w_out_f, glu_w_f, glu_b3, final_g3, l,
                        final=(l == depth - 1))
    return x2d.reshape(batch, seq, D_MODEL).astype(x.dtype)
```

```python
import functools

import numpy as np
import jax
import jax.numpy as jnp
from jax import lax
from jax.experimental import pallas as pl
from jax.experimental.pallas import tpu as pltpu

F32 = jnp.float32
BF16 = jnp.bfloat16

D_MODEL = 1024
HEAD_DIM = 64
W_BRANCH = 512
N_HEADS = W_BRANCH // HEAD_DIM
N_HEAD_PAIRS = N_HEADS // 2
SSM_GROUP = 16
SSM_GROUPS = W_BRANCH // SSM_GROUP
SSM_STATE = 64
GRID_W = 64
NA_ROWS = 8
NA_COLS = 16
DIL_PATTERNS = ((128, 1), (512, 4), (2048, 16))
DIL_HALF = 64
T5_BUCKETS = 32
T5_MAX_DIST = 1024
RMS_EPS = 1e-6
NEG_INF = -1e30
IN_COLS = 10 * W_BRANCH
MIX_WIDTH = 3 * W_BRANCH

COL_XA, COL_ZA, COL_QB, COL_KB, COL_VB, COL_ZB, COL_QC, COL_KC, COL_VC, COL_ZC = (
    4 * i for i in range(10))

LOG2E = float(np.log2(np.e))
Q_SCALE_LOG2 = HEAD_DIM ** -0.5 * LOG2E

LANES = 128
CHUNK = 16
VMEM_LIMIT = 56 * 1024 * 1024


def _sigmoid(z):
    return 0.5 * (1.0 + jnp.tanh(0.5 * z))


def _silu(z):
    return z * _sigmoid(z)


def _gelu_tanh(x):
    return 0.5 * x * (1.0 + jnp.tanh(np.sqrt(2.0 / np.pi).astype(np.float32) * (x + 0.044715 * (x * x * x))))


IN_TM = 512
IN_TN = 512


DIL_QKV_TILES = range(COL_QC * LANES // IN_TN, COL_ZC * LANES // IN_TN)


def _in_proj_kernel(x_ref, g_ref, w_ref, o_ref, o32_ref, w_bf):
    @pl.when(pl.program_id(0) == 0)
    def _():
        for n in range(IN_COLS // IN_TN):
            w_bf[:, n * IN_TN:(n + 1) * IN_TN] = w_ref[:, n * IN_TN:(n + 1) * IN_TN].astype(BF16)

    x = x_ref[...]
    ms = jnp.mean(x * x, axis=-1, keepdims=True)
    h = (x * lax.rsqrt(ms + RMS_EPS) * g_ref[...]).astype(BF16)
    for n in range(IN_COLS // IN_TN):
        acc = jnp.dot(h, w_bf[:, n * IN_TN:(n + 1) * IN_TN], preferred_element_type=F32)
        o_ref[:, n * IN_TN:(n + 1) * IN_TN] = acc.astype(BF16)
        if n in DIL_QKV_TILES:
            m = n - DIL_QKV_TILES.start
            o32_ref[:, m * IN_TN:(m + 1) * IN_TN] = acc


def _in_proj(x2d, g, w, layer):
    rows = x2d.shape[0]
    return pl.pallas_call(
        _in_proj_kernel,
        grid=(rows // IN_TM,),
        in_specs=[
            pl.BlockSpec((IN_TM, D_MODEL), lambda i: (i, 0)),
            pl.BlockSpec((None, 1, D_MODEL), lambda i: (layer, 0, 0)),
            pl.BlockSpec((None, D_MODEL, IN_COLS), lambda i: (layer, 0, 0), pipeline_mode=pl.Buffered(1)),
        ],
        out_specs=[pl.BlockSpec((IN_TM, IN_COLS), lambda i: (i, 0)),
                   pl.BlockSpec((IN_TM, MIX_WIDTH), lambda i: (i, 0))],
        out_shape=[jax.ShapeDtypeStruct((rows, IN_COLS), BF16),
                   jax.ShapeDtypeStruct((rows, MIX_WIDTH), F32)],
        scratch_shapes=[pltpu.VMEM((D_MODEL, IN_COLS), BF16)],
        compiler_params=pltpu.CompilerParams(
            dimension_semantics=("arbitrary",), vmem_limit_bytes=VMEM_LIMIT),
        name="in_proj",
    )(x2d, g, w)


def _na_bias_kernel(rpb_ref, o_ref):
    lane = lax.broadcasted_iota(jnp.int32, (GRID_W, LANES), 1)
    j = lax.broadcasted_iota(jnp.int32, (GRID_W, LANES), 0)
    c = lane & (GRID_W - 1)
    col_start = jnp.clip(j - NA_COLS // 2, 0, GRID_W - NA_COLS)
    valid = (c >= col_start) & (c < col_start + NA_COLS)
    first = lane < GRID_W
    even, odd = [], []
    for a in range(2 * NA_ROWS - 1):
        vec = pltpu.roll(rpb_ref[a:a + 1, :], LANES - (NA_COLS - 1), axis=1)
        x = jnp.broadcast_to(vec, (GRID_W, LANES))
        even.append(pltpu.roll(x, 0, axis=1, stride=1, stride_axis=0))
        odd.append(pltpu.roll(x, GRID_W, axis=1, stride=1, stride_axis=0))
    for v in range(NA_ROWS):
        for i in range(NA_ROWS // 2):
            a0 = 2 * i - v + (NA_ROWS - 1)
            tile = jnp.where(first, even[a0], odd[a0 + 1])
            o_ref[v, :, i * LANES:(i + 1) * LANES] = jnp.where(valid, tile * LOG2E, NEG_INF)


def _na_bias_table(rpb):
    depth = rpb.shape[0]
    padded = jnp.pad(rpb.astype(F32), ((0, 0), (0, 0), (0, 1), (0, LANES - (2 * NA_COLS - 1))))
    nkeys = NA_ROWS * GRID_W
    return pl.pallas_call(
        _na_bias_kernel,
        grid=(depth, N_HEADS),
        in_specs=[pl.BlockSpec((None, None, 2 * NA_ROWS, LANES), lambda l, h: (l, h, 0, 0))],
        out_specs=pl.BlockSpec((None, None, NA_ROWS, GRID_W, nkeys), lambda l, h: (l, h, 0, 0, 0)),
        out_shape=jax.ShapeDtypeStruct((depth, N_HEADS, NA_ROWS, GRID_W, nkeys), F32),
        compiler_params=pltpu.CompilerParams(dimension_semantics=("arbitrary", "arbitrary")),
        name="na_bias",
    )(padded)


def _na_kernel(q_ref, k_ref, v_ref, z_ref, bias_ref, o_ref):
    seq = q_ref.shape[0]
    rows = seq // GRID_W
    half_rows = NA_ROWS // 2
    tile_keys = half_rows * GRID_W
    lane = lax.broadcasted_iota(jnp.int32, (GRID_W, LANES), 1)
    first_head = lane < HEAD_DIM
    masks = (first_head, jnp.logical_not(first_head))
    nt = (((1,), (1,)), ((), ()))
    ones = jnp.ones((tile_keys, LANES), BF16)
    row_start = lambda r: min(max(r - NA_ROWS // 2, 0), rows - NA_ROWS)

    for g0 in range(0, rows, NA_GROUP):
        group = range(g0, g0 + NA_GROUP)
        users = {}
        for r in group:
            for half in range(2):
                users.setdefault(row_start(r) + half_rows * half, []).append((r, half))
        qm = {}
        for r in group:
            qb = (q_ref[r * GRID_W:(r + 1) * GRID_W, :].astype(F32) * Q_SCALE_LOG2).astype(BF16)
            for hd in range(2):
                qm[r, hd] = jnp.where(masks[hd], qb, jnp.zeros_like(qb))
        s_half = {}
        for k, us in sorted(users.items()):
            lhs = jnp.concatenate([qm[r, hd] for r, _ in us for hd in range(2)], axis=0)
            part = lax.dot_general(lhs, k_ref[k * GRID_W:k * GRID_W + tile_keys, :], nt,
                                   preferred_element_type=F32)
            for i, (r, half) in enumerate(us):
                for hd in range(2):
                    blk = part[(2 * i + hd) * GRID_W:(2 * i + hd + 1) * GRID_W, :]
                    bias = bias_ref[hd, r - row_start(r), :, half * tile_keys:(half + 1) * tile_keys]
                    s_half[r, hd, half] = (blk + bias).astype(BF16)
        p_half = {}
        for r in group:
            for hd in range(2):
                s0, s1 = s_half[r, hd, 0], s_half[r, hd, 1]
                m = jnp.max(jnp.maximum(s0, s1), axis=-1, keepdims=True)
                p_half[r, hd, 0], p_half[r, hd, 1] = jnp.exp2(s0 - m), jnp.exp2(s1 - m)
        acc = {}
        for k, us in sorted(users.items()):
            lhs = jnp.concatenate([p_half[r, hd, half] for r, half in us for hd in range(2)], axis=0)
            vext = jnp.concatenate([v_ref[k * GRID_W:k * GRID_W + tile_keys, :], ones], axis=-1)
            part = jnp.dot(lhs, vext, preferred_element_type=F32)
            for i, (r, half) in enumerate(us):
                for hd in range(2):
                    blk = part[(2 * i + hd) * GRID_W:(2 * i + hd + 1) * GRID_W, :]
                    acc[r, hd] = blk if (r, hd) not in acc else acc[r, hd] + blk
        for r in group:
            y = jnp.where(first_head, acc[r, 0][:, :LANES], acc[r, 1][:, :LANES])
            d = jnp.where(first_head, acc[r, 0][:, LANES:], acc[r, 1][:, LANES:])
            z = z_ref[r * GRID_W:(r + 1) * GRID_W, :].astype(F32)
            o_ref[r * GRID_W:(r + 1) * GRID_W, :] = (y * (1.0 / d) * _silu(z)).astype(BF16)


NA_GROUP = 16


def _na(proj, bias, layer, batch, seq):
    def col(c0):
        return pl.BlockSpec((seq, LANES), lambda b, p, c0=c0: (b, c0 + p))

    return pl.pallas_call(
        _na_kernel,
        grid=(batch, N_HEAD_PAIRS),
        in_specs=[
            col(COL_QB), col(COL_KB), col(COL_VB), col(COL_ZB),
            pl.BlockSpec((None, 2, NA_ROWS, GRID_W, NA_ROWS * GRID_W), lambda b, p: (layer, p, 0, 0, 0)),
        ],
        out_specs=pl.BlockSpec((seq, LANES), lambda b, p: (b, p)),
        out_shape=jax.ShapeDtypeStruct((batch * seq, W_BRANCH), BF16),
        compiler_params=pltpu.CompilerParams(
            dimension_semantics=("arbitrary", "arbitrary"), vmem_limit_bytes=VMEM_LIMIT),
        name="na_attn",
    )(proj, proj, proj, proj, bias)


DIL_QB = 128
DIL_KW = 256
N_VARIANTS = 3


def _t5_bucket(rel):
    nb = T5_BUCKETS // 2
    max_exact = nb // 2
    n = np.abs(rel)
    large = max_exact + (np.log(np.maximum(n, 1) / max_exact) / np.log(T5_MAX_DIST / max_exact)
                         * (nb - max_exact)).astype(np.int32)
    large = np.minimum(large, nb - 1)
    return (np.where(rel > 0, nb, 0) + np.where(n < max_exact, n, large)).astype(np.int32)


def _dil_bias_table(t5_bias):
    n_pat = len(DIL_PATTERNS)
    lane = np.arange(DIL_VEC)
    n = np.where(lane < DIL_KW, lane, lane - DIL_VEC)
    in_range = (lane < DIL_KW) | (lane > DIL_VEC - DIL_QB)
    pick = np.zeros((n_pat, N_VARIANTS, T5_BUCKETS, DIL_VEC), np.float32)
    neg = np.zeros((n_pat, N_VARIANTS, 1, DIL_VEC), np.float32)
    for pi, (_, d) in enumerate(DIL_PATTERNS):
        for var in range(N_VARIANTS):
            step = n - DIL_HALF * var
            ok = in_range & (np.abs(step) <= DIL_HALF)
            bucket = _t5_bucket(d * np.clip(step, -DIL_HALF, DIL_HALF))
            pick[pi, var, bucket[ok], lane[ok]] = 1.0
            neg[pi, var, 0, ~ok] = NEG_INF
    vecs = jnp.einsum('bh,pvbn->hpvn', t5_bias.astype(F32), pick,
                      precision=lax.Precision.HIGHEST) * LOG2E + neg[:, :, 0][None]
    return pl.pallas_call(
        _dil_bias_kernel,
        grid=(N_HEADS,),
        in_specs=[pl.BlockSpec((None, n_pat * N_VARIANTS, DIL_VEC), lambda h: (h, 0, 0))],
        out_specs=pl.BlockSpec((None, n_pat, N_VARIANTS, DIL_QB, DIL_KW), lambda h: (h, 0, 0, 0, 0)),
        out_shape=jax.ShapeDtypeStruct((N_HEADS, n_pat, N_VARIANTS, DIL_QB, DIL_KW), F32),
        compiler_params=pltpu.CompilerParams(dimension_semantics=("arbitrary",)),
        name="dil_bias",
    )(vecs.reshape(N_HEADS, n_pat * N_VARIANTS, DIL_VEC))


DIL_VEC = 512


def _dil_bias_kernel(vec_ref, o_ref):
    for pi in range(len(DIL_PATTERNS)):
        for var in range(N_VARIANTS):
            i = pi * N_VARIANTS + var
            x = jnp.broadcast_to(vec_ref[i:i + 1, :], (DIL_QB, DIL_VEC))
            o_ref[pi, var] = pltpu.roll(x, 0, axis=1, stride=1, stride_axis=0)[:, :DIL_KW]


DIL_GROUP = 16


def _dil_kernel(q32_ref, k32_ref, v32_ref, k_ref, v_ref, z_ref, bias_ref, o_ref, qf, kf, vf, qg, kg, vg, qd,
                q4, k4, v4, m_s, l_s, acc_s):
    seq = k_ref.shape[0]
    conv_rows = 256
    lane = lax.broadcasted_iota(jnp.int32, (DIL_QB, LANES), 1)
    first_head = lane < HEAD_DIM
    nt = (((1,), (1,)), ((), ()))
    ones = jnp.ones((DIL_KW, LANES), BF16)

    n4 = seq // 4
    n16 = seq // 16

    def to_order4(i, carry):
        dst = pl.multiple_of(i * conv_rows, conv_rows)
        src = pl.ds(dst // n4 + 4 * (dst % n4), conv_rows, stride=4)
        for a, b, c in ((q32_ref, qg, q4), (k32_ref, kg, k4), (v32_ref, vg, v4)):
            part = a[src, :]
            if a is q32_ref:
                part = part * Q_SCALE_LOG2
            b[pl.ds(dst, conv_rows), :] = part
            c[pl.ds(dst, conv_rows), :] = part.astype(BF16)
        return carry

    lax.fori_loop(0, seq // conv_rows, to_order4, 0, unroll=True)

    def window(src, res, start, size, n_sub):
        if n_sub == n16:
            rows = pl.ds((res % 4) * n4 + res // 4 + 4 * start, size, stride=4)
            return src[rows, :].astype(BF16)
        return src[pl.ds(pl.multiple_of(res * n_sub + start, DIL_HALF), size), :]

    order4_state = (qf, kf, vf)
    token_state = (m_s, l_s, acc_s)
    plan = ((2, 16, (qg, kg, vg), None, order4_state),
            (1, 4, (q4, k4, v4), order4_state, token_state),
            (0, 1, (qd, k_ref, v_ref), token_state, None))

    for step, (pi, d, (q_src, k_src, v_src), st_in, st_out) in enumerate(plan):
        n_sub = seq // d
        nblk = n_sub // DIL_QB

        if step == 2:
            def scaled_q(i, carry):
                r0 = pl.multiple_of(i * conv_rows, conv_rows)
                qd[pl.ds(r0, conv_rows), :] = (q32_ref[pl.ds(r0, conv_rows), :] * Q_SCALE_LOG2).astype(BF16)
                return carry

            lax.fori_loop(0, seq // conv_rows, scaled_q, 0, unroll=True)

        def rows_out(res, i0, step=step):
            if step == 0:
                return pl.ds((res % 4) * n4 + res // 4 + 4 * i0, DIL_QB, stride=4)
            if step == 1:
                return pl.ds(res + 4 * i0, DIL_QB, stride=4)
            return pl.ds(pl.multiple_of(i0, DIL_HALF), DIL_QB)

        n_group = DIL_GROUP

        def group(gi, carry, pi=pi, n_sub=n_sub, nblk=nblk, q_src=q_src, k_src=k_src, v_src=v_src,
                  st_in=st_in, st_out=st_out, rows_out=rows_out, n_group=n_group):
            units = []
            for i in range(n_group):
                u = gi * n_group + i
                res = u // nblk
                i0 = (u % nblk) * DIL_QB
                ks = jnp.clip(i0 - DIL_HALF, 0, n_sub - DIL_KW)
                units.append((res, i0, ks, (i0 - ks) // DIL_HALF))
            s_tiles, m_tiles = [], []
            for res, i0, ks, var in units:
                qb = window(q_src, res, i0, DIL_QB, n_sub)
                kw = window(k_src, res, ks, DIL_KW, n_sub)
                for hd in range(2):
                    keep = first_head if hd == 0 else jnp.logical_not(first_head)
                    qm = jnp.where(keep, qb, jnp.zeros_like(qb))
                    s = lax.dot_general(qm, kw, nt, preferred_element_type=F32) + bias_ref[hd, pi, var]
                    s = s.astype(BF16)
                    s_tiles.append(s)
                    m_tiles.append(jnp.max(s, axis=-1, keepdims=True))
            p_tiles = [jnp.exp2(s - m) for s, m in zip(s_tiles, m_tiles)]
            for i, (res, i0, ks, var) in enumerate(units):
                vw = window(v_src, res, ks, DIL_KW, n_sub)
                vext = jnp.concatenate([vw, ones], axis=-1)
                r0 = jnp.dot(p_tiles[2 * i], vext, preferred_element_type=F32)
                r1 = jnp.dot(p_tiles[2 * i + 1], vext, preferred_element_type=F32)
                m_cur = jnp.where(first_head, m_tiles[2 * i].astype(F32), m_tiles[2 * i + 1].astype(F32))
                l_cur = jnp.where(first_head, r0[:, LANES:], r1[:, LANES:])
                acc_cur = jnp.where(first_head, r0[:, :LANES], r1[:, :LANES])
                dst = rows_out(res, i0)
                if st_in is None:
                    m_new, l_new, acc_new = m_cur, l_cur, acc_cur
                else:
                    src = pl.ds(pl.multiple_of(res * n_sub + i0, DIL_HALF), DIL_QB)
                    m_old = st_in[0][src, :]
                    m_new = jnp.maximum(m_old, m_cur)
                    a_old = jnp.exp2(m_old - m_new)
                    a_cur = jnp.exp2(m_cur - m_new)
                    l_new = a_old * st_in[1][src, :] + a_cur * l_cur
                    acc_new = a_old * st_in[2][src, :] + a_cur * acc_cur
                if st_out is None:
                    z = z_ref[dst, :].astype(F32)
                    o_ref[dst, :] = (acc_new * (1.0 / l_new) * _silu(z)).astype(BF16)
                else:
                    st_out[0][dst, :] = m_new
                    st_out[1][dst, :] = l_new
                    st_out[2][dst, :] = acc_new
            return carry

        lax.fori_loop(0, seq // DIL_QB // n_group, group, 0, unroll=True)


def _dil(proj, qkv32, bias, batch, seq):
    def col(c0):
        return pl.BlockSpec((seq, LANES), lambda b, p, c0=c0: (b, c0 + p))

    scratch = ([pltpu.VMEM((seq, LANES), F32) for _ in range(6)]
               + [pltpu.VMEM((seq, LANES), BF16) for _ in range(4)]
               + [pltpu.VMEM((seq, LANES), F32) for _ in range(3)])
    return pl.pallas_call(
        _dil_kernel,
        grid=(batch, N_HEAD_PAIRS),
        in_specs=[
            col(0), col(COL_KC - COL_QC), col(COL_VC - COL_QC),
            col(COL_KC), col(COL_VC), col(COL_ZC),
            pl.BlockSpec((2, len(DIL_PATTERNS), N_VARIANTS, DIL_QB, DIL_KW),
                         lambda b, p: (p, 0, 0, 0, 0)),
        ],
        out_specs=pl.BlockSpec((seq, LANES), lambda b, p: (b, p)),
        out_shape=jax.ShapeDtypeStruct((batch * seq, W_BRANCH), BF16),
        scratch_shapes=scratch,
        compiler_params=pltpu.CompilerParams(
            dimension_semantics=("arbitrary", "arbitrary"), vmem_limit_bytes=VMEM_LIMIT),
        name="dil_attn",
    )(qkv32, qkv32, qkv32, proj, proj, proj, bias)


N_POW = CHUNK + 1
PREP_GROUPS = 8


def _ssm_prep_kernel(lre_ref, lim_ref, ldt_ref, btre_ref, btim_ref, cre_ref, cim_ref,
                     toep_ref, win_ref, wout_ref, decay_ref):
    n = CHUNK * SSM_GROUP
    lane = lax.broadcasted_iota(jnp.int32, (SSM_GROUP, LANES), 1)
    lo = lane < SSM_STATE
    lane_n = lax.broadcasted_iota(jnp.int32, (SSM_GROUP, n), 1)
    nt = (((1,), (1,)), ((), ()))
    pair = lambda x, y: jnp.where(lo, x, y)

    swap = lambda x: pltpu.roll(x, SSM_STATE, axis=1)

    for j in range(PREP_GROUPS):
        xs, ys = {}, {}
        lam_re = lre_ref[0, j]
        lam_im = lim_ref[0, j]
        dt = jnp.exp(ldt_ref[0, j])
        mag = jnp.exp(lam_re * dt)
        lb_re = mag * jnp.cos(lam_im * dt)
        lb_im = mag * jnp.sin(lam_im * dt)
        n_re = lb_re - 1.0
        den = lam_re * lam_re + lam_im * lam_im
        q_re = (n_re * lam_re + lb_im * lam_im) / den
        q_im = (lb_im * lam_re - n_re * lam_im) / den
        bt_re = btre_ref[0, j]
        bt_im = btim_ref[0, j]
        bb_re = q_re * bt_re - q_im * bt_im
        bb_im = q_re * bt_im + q_im * bt_re
        c_re = cre_ref[0, j]
        c_im = cim_ref[0, j]
        pw_re = jnp.ones_like(lb_re)
        pw_im = jnp.zeros_like(lb_re)
        for k in range(N_POW):
            xs[k] = (pw_re * bb_re - pw_im * bb_im, pw_re * bb_im + pw_im * bb_re)
            ys[k] = (pw_re * c_re - pw_im * c_im, pw_re * c_im + pw_im * c_re)
            pw_last = (pw_re, pw_im)
            pw_re, pw_im = pw_re * lb_re - pw_im * lb_im, pw_re * lb_im + pw_im * lb_re

        step = [GROUPS_PER_SLAB * (a // GROUPS_PER_SLAB) + (a % GROUPS_PER_SLAB - j) % GROUPS_PER_SLAB
                for a in range(CHUNK)]
        rows = lambda a: slice(a * SSM_GROUP, (a + 1) * SSM_GROUP)

        decay_ref[0, 0, j:j + 1, :] = pw_last[0][0:1]
        decay_ref[0, 1, j:j + 1, :] = pw_last[1][0:1]

        for a, s in enumerate(step):
            win_ref[0, j, rows(a), :LANES] = pair(xs[CHUNK - 1 - s][0], xs[s][0]).astype(BF16)
            win_ref[0, j, rows(a), LANES:] = pair(xs[CHUNK - 1 - s][1], xs[s][1]).astype(BF16)

        z_re = jnp.concatenate([pair(ys[t + 1][0], ys[CHUNK - t][0]) for t in step], axis=0)
        z_nim = jnp.concatenate([pair(-ys[t + 1][1], -ys[CHUNK - t][1]) for t in step], axis=0)
        wout_ref[0, j, :LANES, :] = z_re.T.astype(BF16)
        wout_ref[0, j, LANES:, :] = z_nim.T.astype(BF16)

        kf = lax.dot_general(
            pair(bb_re, -swap(bb_im)),
            jnp.concatenate([pair(ys[m][0], swap(ys[m][1])) for m in range(CHUNK)], axis=0),
            nt, precision=lax.Precision.HIGHEST, preferred_element_type=F32)
        kb = lax.dot_general(
            pair(swap(bb_re), -bb_im),
            jnp.concatenate([pair(swap(ys[CHUNK - 1 - i][0]), ys[CHUNK - 1 - i][1])
                             for i in range(CHUNK)], axis=0),
            nt, precision=lax.Precision.HIGHEST, preferred_element_type=F32)
        for a, s in enumerate(step):
            fwd = pltpu.roll(kf, SSM_GROUP * s, axis=1) if s else kf
            shift_b = (n - SSM_GROUP * (CHUNK - 1 - s)) % n
            bwd = pltpu.roll(kb, shift_b, axis=1) if shift_b else kb
            t_nat = (jnp.where(lane_n >= SSM_GROUP * s, fwd, 0.0)
                     + jnp.where(lane_n < SSM_GROUP * (s + 1), bwd, 0.0))
            for h in range(2):
                half = t_nat[:, h * LANES:(h + 1) * LANES]
                if j:
                    half = pltpu.roll(half, SSM_GROUP * j, axis=1)
                toep_ref[0, j, rows(a), h * LANES:(h + 1) * LANES] = half.astype(BF16)


def _ssm_prep(lam_re, lam_im, log_dt, b_re, b_im, c_re, c_im):
    depth = lam_re.shape[0]
    G, P, C = SSM_GROUPS, SSM_STATE, SSM_GROUP
    n = CHUNK * C
    gb = PREP_GROUPS
    both = lambda a: jnp.concatenate([a[:, 0], a[:, 1]], axis=-1).astype(F32)
    rep = lambda a: jnp.broadcast_to(both(a)[:, :, None, :], (depth, G, C, LANES))
    ldt = rep(jnp.broadcast_to(log_dt[..., None], (depth, 2, G, P)))
    bt = lambda a: both(a.transpose(0, 1, 2, 4, 3))
    spec_in = pl.BlockSpec((1, gb, C, LANES), lambda l, i: (l, i, 0, 0))
    spec_w = pl.BlockSpec((1, gb, n, n), lambda l, i: (l, i, 0, 0))
    shp_w = jax.ShapeDtypeStruct((depth, G, n, n), BF16)
    return pl.pallas_call(
        _ssm_prep_kernel,
        grid=(depth, G // gb),
        in_specs=[spec_in] * 7,
        out_specs=[spec_w, spec_w, spec_w, pl.BlockSpec((1, 2, gb, LANES), lambda l, i: (l, 0, i, 0))],
        out_shape=[shp_w, shp_w, shp_w, jax.ShapeDtypeStruct((depth, 2, G, LANES), F32)],
        compiler_params=pltpu.CompilerParams(
            dimension_semantics=("arbitrary", "arbitrary"), vmem_limit_bytes=VMEM_LIMIT),
        name="ssm_prep",
    )(rep(lam_re), rep(lam_im), ldt, bt(b_re), bt(b_im), both(c_re), both(c_im))


GROUPS_PER_SLAB = LANES // SSM_GROUP
N_SLABS = W_BRANCH // LANES
RELAYOUT_ROWS = 32
SCAN_SLABS = 2


def _ssm_kernel(xa_ref, toep_ref, win_ref, wout_ref, decay_ref, d_ref, o_ref, xf, u_s, st, hs, yq):
    seq = xa_ref.shape[0]
    nchunk = seq // CHUNK
    nrb = nchunk // RELAYOUT_ROWS
    gps = GROUPS_PER_SLAB
    lane_rb = lax.broadcasted_iota(jnp.int32, (RELAYOUT_ROWS, LANES), 1)
    seg_masks = [(lane_rb >= SSM_GROUP * sg) & (lane_rb < SSM_GROUP * (sg + 1)) for sg in range(gps)]
    conv_rows = 512

    lane_g = lax.broadcasted_iota(jnp.int32, (gps, LANES), 1)
    fwd_lanes = lane_g < SSM_STATE
    lo, hi = slice(0, SSM_STATE), slice(SSM_STATE, LANES)

    for q0 in range(0, N_SLABS, SCAN_SLABS):
        for s in range(SCAN_SLABS):
            q = q0 + s
            c_lo = q * LANES

            def conv(i, carry, c_lo=c_lo, s=s):
                r0 = pl.multiple_of(i * conv_rows, conv_rows)
                xf[s, pl.ds(r0, conv_rows), :] = xa_ref[pl.ds(r0, conv_rows), c_lo:c_lo + LANES].astype(F32)
                return carry

            lax.fori_loop(0, seq // conv_rows, conv, 0)

            for hh in range(2):
                def fwd_relayout(rb, carry, hh=hh, s=s):
                    c0 = pl.multiple_of(rb * RELAYOUT_ROWS, RELAYOUT_ROWS)
                    rolled = []
                    for r in range(gps):
                        xs = xf[s, pl.ds(CHUNK * c0 + gps * hh + r, RELAYOUT_ROWS, stride=CHUNK), :]
                        rolled.append((xs if r == 0 else pltpu.roll(xs, SSM_GROUP * r, axis=1)).astype(BF16))
                    for j in range(gps):
                        out = rolled[(0 - j) % gps]
                        for sg in range(1, gps):
                            out = jnp.where(seg_masks[sg], rolled[(sg - j) % gps], out)
                        u_s[s, j, pl.ds(c0, RELAYOUT_ROWS), hh * LANES:(hh + 1) * LANES] = out
                    return carry

                lax.fori_loop(0, nrb, fwd_relayout, 0, unroll=True)

            for j in range(gps):
                s_in = jnp.dot(u_s[s, j], win_ref[gps * q + j], preferred_element_type=F32)
                st[s, 0, pl.ds(j, nchunk, stride=gps), :] = s_in[:, :LANES]
                st[s, 1, pl.ds(j, nchunk, stride=gps), :] = s_in[:, LANES:]

        a_re = [decay_ref[0, gps * (q0 + s):gps * (q0 + s + 1), :] for s in range(SCAN_SLABS)]
        a_im = [decay_ref[1, gps * (q0 + s):gps * (q0 + s + 1), :] for s in range(SCAN_SLABS)]

        def scan(i, carry):
            rf = pl.ds(pl.multiple_of(i * gps, gps), gps)
            rb_ = pl.ds(pl.multiple_of((nchunk - 1 - i) * gps, gps), gps)
            new = []
            for s in range(SCAN_SLABS):
                h_re, h_im = carry[2 * s], carry[2 * s + 1]
                hs[s, 0, rf, lo] = h_re[:, lo]
                hs[s, 1, rf, lo] = h_im[:, lo]
                hs[s, 0, rb_, hi] = h_re[:, hi]
                hs[s, 1, rb_, hi] = h_im[:, hi]
                in_re = jnp.where(fwd_lanes, st[s, 0, rf, :], st[s, 0, rb_, :])
                in_im = jnp.where(fwd_lanes, st[s, 1, rf, :], st[s, 1, rb_, :])
                new += [a_re[s] * h_re - a_im[s] * h_im + in_re, a_re[s] * h_im + a_im[s] * h_re + in_im]
            return tuple(new)

        zero = jnp.zeros((gps, LANES), F32)
        lax.fori_loop(0, nchunk, scan, (zero,) * (2 * SCAN_SLABS), unroll=8)

        for s in range(SCAN_SLABS):
            q = q0 + s
            c_lo = q * LANES
            for j in range(gps):
                g = gps * q + j
                h_in = jnp.concatenate([hs[s, 0, pl.ds(j, nchunk, stride=gps), :].astype(BF16),
                                        hs[s, 1, pl.ds(j, nchunk, stride=gps), :].astype(BF16)], axis=-1)
                yq[j] = (jnp.dot(u_s[s, j], toep_ref[g], preferred_element_type=F32)
                         + jnp.dot(h_in, wout_ref[g], preferred_element_type=F32))

            d_row = d_ref[:, c_lo:c_lo + LANES]
            for hh in range(2):
                def bwd_relayout(rb, carry, hh=hh, d_row=d_row, q=q, s=s):
                    c0 = pl.multiple_of(rb * RELAYOUT_ROWS, RELAYOUT_ROWS)
                    ys = [yq[j, pl.ds(c0, RELAYOUT_ROWS), hh * LANES:(hh + 1) * LANES] for j in range(gps)]
                    for r in range(gps):
                        merged = ys[(0 - r) % gps]
                        for sg in range(1, gps):
                            merged = jnp.where(seg_masks[sg], ys[(sg - r) % gps], merged)
                        if r:
                            merged = pltpu.roll(merged, LANES - SSM_GROUP * r, axis=1)
                        tok = pl.ds(CHUNK * c0 + gps * hh + r, RELAYOUT_ROWS, stride=CHUNK)
                        o_ref[q, tok, :] = merged + d_row * xf[s, tok, :]
                    return carry

                lax.fori_loop(0, nrb, bwd_relayout, 0, unroll=True)


def _ssm(proj, toep, w_in, w_out, decay, d_skip, layer, batch, seq):
    nchunk = seq // CHUNK
    n = CHUNK * SSM_GROUP
    wspec = pl.BlockSpec((None, SSM_GROUPS, n, n), lambda b: (layer, 0, 0, 0), pipeline_mode=pl.Buffered(1))
    return pl.pallas_call(
        _ssm_kernel,
        grid=(batch,),
        in_specs=[
            pl.BlockSpec((seq, W_BRANCH), lambda b: (b, COL_XA // N_SLABS)),
            wspec, wspec, wspec,
            pl.BlockSpec((None, 2, SSM_GROUPS, LANES), lambda b: (layer, 0, 0, 0)),
            pl.BlockSpec((None, 1, W_BRANCH), lambda b: (layer, 0, 0)),
        ],
        out_specs=pl.BlockSpec((N_SLABS, seq, LANES), lambda b: (0, b, 0)),
        out_shape=jax.ShapeDtypeStruct((N_SLABS, batch * seq, LANES), F32),
        scratch_shapes=[
            pltpu.VMEM((SCAN_SLABS, seq, LANES), F32),
            pltpu.VMEM((SCAN_SLABS, GROUPS_PER_SLAB, nchunk, n), BF16),
            pltpu.VMEM((SCAN_SLABS, 2, nchunk * GROUPS_PER_SLAB, LANES), F32),
            pltpu.VMEM((SCAN_SLABS, 2, nchunk * GROUPS_PER_SLAB, LANES), F32),
            pltpu.VMEM((GROUPS_PER_SLAB, nchunk, n), F32),
        ],
        compiler_params=pltpu.CompilerParams(
            dimension_semantics=("arbitrary",), vmem_limit_bytes=VMEM_LIMIT),
        name="ssm_mix",
    )(proj, toep, w_in, w_out, decay, d_skip)


OUT_TM = 1024


def _out_proj_kernel(ya_ref, za_ref, yb_ref, yc_ref, x_ref, w_ref, gw_ref, gb_ref, fg_ref, o_ref,
                     w_bf, gw_bf, *, final):
    @pl.when(pl.program_id(0) == 0)
    def _():
        for n in range(MIX_WIDTH // W_BRANCH):
            w_bf[n * W_BRANCH:(n + 1) * W_BRANCH, :] = w_ref[n * W_BRANCH:(n + 1) * W_BRANCH, :].astype(BF16)
        gw_bf[...] = gw_ref[...].astype(BF16)

    y = jnp.concatenate([ya_ref[i] for i in range(N_SLABS)], axis=-1)
    g = _gelu_tanh(y)
    gate = jnp.dot(g.astype(BF16), gw_bf[...], preferred_element_type=F32) + gb_ref[...]
    ya = g * _sigmoid(gate) * _silu(za_ref[...].astype(F32))
    delta = (jnp.dot(ya.astype(BF16), w_bf[:W_BRANCH, :], preferred_element_type=F32)
             + jnp.dot(yb_ref[...], w_bf[W_BRANCH:2 * W_BRANCH, :], preferred_element_type=F32)
             + jnp.dot(yc_ref[...], w_bf[2 * W_BRANCH:, :], preferred_element_type=F32))
    x = x_ref[...] + delta
    if final:
        ms = jnp.mean(x * x, axis=-1, keepdims=True)
        x = x * lax.rsqrt(ms + RMS_EPS) * fg_ref[...]
    o_ref[...] = x


def _out_proj(ya_pre, proj, yb, yc, x2d, w, glu_w, glu_b, final_g, layer, final):
    rows = x2d.shape[0]
    row_blk = lambda width: pl.BlockSpec((OUT_TM, width), lambda i: (i, 0))
    const = lambda shape: pl.BlockSpec((None,) + shape, lambda i: (layer,) + (0,) * len(shape),
                                       pipeline_mode=pl.Buffered(1))
    return pl.pallas_call(
        functools.partial(_out_proj_kernel, final=final),
        grid=(rows // OUT_TM,),
        in_specs=[
            pl.BlockSpec((N_SLABS, OUT_TM, LANES), lambda i: (0, i, 0)),
            pl.BlockSpec((OUT_TM, W_BRANCH), lambda i: (i, COL_ZA // N_SLABS)),
            row_blk(W_BRANCH), row_blk(W_BRANCH), row_blk(D_MODEL),
            const((MIX_WIDTH, D_MODEL)), const((W_BRANCH, W_BRANCH)),
            const((1, W_BRANCH)),
            pl.BlockSpec((None, 1, D_MODEL), lambda i: (0, 0, 0), pipeline_mode=pl.Buffered(1)),
        ],
        out_specs=row_blk(D_MODEL),
        out_shape=jax.ShapeDtypeStruct((rows, D_MODEL), F32),
        scratch_shapes=[pltpu.VMEM((MIX_WIDTH, D_MODEL), BF16), pltpu.VMEM((W_BRANCH, W_BRANCH), BF16)],
        compiler_params=pltpu.CompilerParams(
            dimension_semantics=("arbitrary",), vmem_limit_bytes=VMEM_LIMIT),
        name="out_proj_final" if final else "out_proj",
    )(ya_pre, proj, yb, yc, x2d, w, glu_w, glu_b, final_g)


def kernel(x, norm_g, w_in, w_out, ssm_lam_re, ssm_lam_im, ssm_log_dt, ssm_b_re, ssm_b_im, ssm_c_re,
           ssm_c_im, ssm_d, glu_w, glu_b, na_rpb, t5_bias, final_g):
    batch, seq, _ = x.shape
    depth = w_in.shape[0]
    x2d = x.astype(F32).reshape(batch * seq, D_MODEL)
    toep, s_in, s_out, decay = _ssm_prep(ssm_lam_re, ssm_lam_im, ssm_log_dt, ssm_b_re, ssm_b_im,
                                         ssm_c_re, ssm_c_im)
    dil_bias = _dil_bias_table(t5_bias)
    na_bias = _na_bias_table(na_rpb)
    w_in_f, w_out_f, glu_w_f = w_in.astype(F32), w_out.astype(F32), glu_w.astype(F32)
    row3 = lambda a: a.astype(F32).reshape(a.shape[0], 1, a.shape[1])
    norm_g3, ssm_d3, glu_b3 = row3(norm_g), row3(ssm_d), row3(glu_b)
    final_g3 = final_g.astype(F32).reshape(1, 1, D_MODEL)
    for l in range(depth):
        proj, dil_qkv32 = _in_proj(x2d, norm_g3, w_in_f, l)
        ya_pre = _ssm(proj, toep, s_in, s_out, decay, ssm_d3, l, batch, seq)
        yb = _na(proj, na_bias, l, batch, seq)
        yc = _dil(proj, dil_qkv32, dil_bias, batch, seq)
        x2d = _out_proj(ya_pre, proj, yb, yc, x2d, w_out_f, glu_w_f, glu_b3, final_g3, l,
                        final=(l == depth - 1))
    return x2d.reshape(batch, seq, D_MODEL).astype(x.dtype)
```

```python
import functools

import numpy as np
import jax
import jax.numpy as jnp
from jax import lax
from jax.experimental import pallas as pl
from jax.experimental.pallas import tpu as pltpu

F32 = jnp.float32
BF16 = jnp.bfloat16

D_MODEL = 1024
HEAD_DIM = 64
W_BRANCH = 512
N_HEADS = W_BRANCH // HEAD_DIM
N_HEAD_PAIRS = N_HEADS // 2
SSM_GROUP = 16
SSM_GROUPS = W_BRANCH // SSM_GROUP
SSM_STATE = 64
GRID_W = 64
NA_ROWS = 8
NA_COLS = 16
DIL_PATTERNS = ((128, 1), (512, 4), (2048, 16))
DIL_HALF = 64
T5_BUCKETS = 32
T5_MAX_DIST = 1024
RMS_EPS = 1e-6
NEG_INF = -1e30
IN_COLS = 10 * W_BRANCH
MIX_WIDTH = 3 * W_BRANCH

COL_XA, COL_ZA, COL_QB, COL_KB, COL_VB, COL_ZB, COL_QC, COL_KC, COL_VC, COL_ZC = (
    4 * i for i in range(10))

LOG2E = float(np.log2(np.e))
Q_SCALE_LOG2 = HEAD_DIM ** -0.5 * LOG2E

LANES = 128
CHUNK = 16
VMEM_LIMIT = 56 * 1024 * 1024


def _sigmoid(z):
    return 0.5 * (1.0 + jnp.tanh(0.5 * z))


def _silu(z):
    return z * _sigmoid(z)


def _gelu_tanh(x):
    return 0.5 * x * (1.0 + jnp.tanh(np.sqrt(2.0 / np.pi).astype(np.float32) * (x + 0.044715 * (x * x * x))))


IN_TM = 512
IN_TN = 512


DIL_QKV_TILES = range(COL_QC * LANES // IN_TN, COL_ZC * LANES // IN_TN)


def _in_proj_kernel(x_ref, g_ref, w_ref, o_ref, o32_ref, w_bf):
    @pl.when(pl.program_id(0) == 0)
    def _():
        for n in range(IN_COLS // IN_TN):
            w_bf[:, n * IN_TN:(n + 1) * IN_TN] = w_ref[:, n * IN_TN:(n + 1) * IN_TN].astype(BF16)

    x = x_ref[...]
    ms = jnp.mean(x * x, axis=-1, keepdims=True)
    h = (x * lax.rsqrt(ms + RMS_EPS) * g_ref[...]).astype(BF16)
    for n in range(IN_COLS // IN_TN):
        acc = jnp.dot(h, w_bf[:, n * IN_TN:(n + 1) * IN_TN], preferred_element_type=F32)
        o_ref[:, n * IN_TN:(n + 1) * IN_TN] = acc.astype(BF16)
        if n in DIL_QKV_TILES:
            m = n - DIL_QKV_TILES.start
            o32_ref[:, m * IN_TN:(m + 1) * IN_TN] = acc


def _in_proj(x2d, g, w, layer):
    rows = x2d.shape[0]
    return pl.pallas_call(
        _in_proj_kernel,
        grid=(rows // IN_TM,),
        in_specs=[
            pl.BlockSpec((IN_TM, D_MODEL), lambda i: (i, 0)),
            pl.BlockSpec((None, 1, D_MODEL), lambda i: (layer, 0, 0)),
            pl.BlockSpec((None, D_MODEL, IN_COLS), lambda i: (layer, 0, 0), pipeline_mode=pl.Buffered(1)),
        ],
        out_specs=[pl.BlockSpec((IN_TM, IN_COLS), lambda i: (i, 0)),
                   pl.BlockSpec((IN_TM, MIX_WIDTH), lambda i: (i, 0))],
        out_shape=[jax.ShapeDtypeStruct((rows, IN_COLS), BF16),
                   jax.ShapeDtypeStruct((rows, MIX_WIDTH), F32)],
        scratch_shapes=[pltpu.VMEM((D_MODEL, IN_COLS), BF16)],
        compiler_params=pltpu.CompilerParams(
            dimension_semantics=("arbitrary",), vmem_limit_bytes=VMEM_LIMIT),
        name="in_proj",
    )(x2d, g, w)


def _na_bias_kernel(rpb_ref, o_ref):
    lane = lax.broadcasted_iota(jnp.int32, (GRID_W, LANES), 1)
    j = lax.broadcasted_iota(jnp.int32, (GRID_W, LANES), 0)
    c = lane & (GRID_W - 1)
    col_start = jnp.clip(j - NA_COLS // 2, 0, GRID_W - NA_COLS)
    valid = (c >= col_start) & (c < col_start + NA_COLS)
    lane1 = lax.broadcasted_iota(jnp.int32, (1, LANES), 1)
    near_zero = (lane1 < GRID_W // 2) | (lane1 >= LANES - GRID_W // 2)
    vecs = [pltpu.roll(rpb_ref[a:a + 1, :], LANES - (NA_COLS - 1), axis=1) for a in range(2 * NA_ROWS - 1)]
    tiles = []
    for a0 in range(2 * NA_ROWS - 2):
        w = jnp.where(near_zero, vecs[a0], pltpu.roll(vecs[a0 + 1], GRID_W, axis=1))
        x = jnp.broadcast_to(w, (GRID_W, LANES))
        tiles.append(pltpu.roll(x, 0, axis=1, stride=1, stride_axis=0))
    for v in range(NA_ROWS):
        for i in range(NA_ROWS // 2):
            a0 = 2 * i - v + (NA_ROWS - 1)
            o_ref[v, :, i * LANES:(i + 1) * LANES] = jnp.where(valid, tiles[a0] * LOG2E, NEG_INF)


def _na_bias_table(rpb):
    depth = rpb.shape[0]
    padded = jnp.pad(rpb.astype(F32), ((0, 0), (0, 0), (0, 1), (0, LANES - (2 * NA_COLS - 1))))
    nkeys = NA_ROWS * GRID_W
    return pl.pallas_call(
        _na_bias_kernel,
        grid=(depth, N_HEADS),
        in_specs=[pl.BlockSpec((None, None, 2 * NA_ROWS, LANES), lambda l, h: (l, h, 0, 0))],
        out_specs=pl.BlockSpec((None, None, NA_ROWS, GRID_W, nkeys), lambda l, h: (l, h, 0, 0, 0)),
        out_shape=jax.ShapeDtypeStruct((depth, N_HEADS, NA_ROWS, GRID_W, nkeys), F32),
        compiler_params=pltpu.CompilerParams(dimension_semantics=("arbitrary", "arbitrary")),
        name="na_bias",
    )(padded)


def _na_kernel(q_ref, k_ref, v_ref, z_ref, bias_ref, o_ref):
    seq = q_ref.shape[0]
    rows = seq // GRID_W
    half_rows = NA_ROWS // 2
    tile_keys = half_rows * GRID_W
    lane = lax.broadcasted_iota(jnp.int32, (GRID_W, LANES), 1)
    first_head = lane < HEAD_DIM
    masks = (first_head, jnp.logical_not(first_head))
    nt = (((1,), (1,)), ((), ()))
    ones = jnp.ones((tile_keys, LANES), BF16)
    row_start = lambda r: min(max(r - NA_ROWS // 2, 0), rows - NA_ROWS)

    for g0 in range(0, rows, NA_GROUP):
        group = range(g0, g0 + NA_GROUP)
        users = {}
        for r in group:
            for half in range(2):
                users.setdefault(row_start(r) + half_rows * half, []).append((r, half))
        qm = {}
        for r in group:
            qb = (q_ref[r * GRID_W:(r + 1) * GRID_W, :].astype(F32) * Q_SCALE_LOG2).astype(BF16)
            for hd in range(2):
                qm[r, hd] = jnp.where(masks[hd], qb, jnp.zeros_like(qb))
        s_half = {}
        for k, us in sorted(users.items()):
            lhs = jnp.concatenate([qm[r, hd] for r, _ in us for hd in range(2)], axis=0)
            part = lax.dot_general(lhs, k_ref[k * GRID_W:k * GRID_W + tile_keys, :], nt,
                                   preferred_element_type=F32)
            for i, (r, half) in enumerate(us):
                for hd in range(2):
                    blk = part[(2 * i + hd) * GRID_W:(2 * i + hd + 1) * GRID_W, :]
                    bias = bias_ref[hd, r - row_start(r), :, half * tile_keys:(half + 1) * tile_keys]
                    s_half[r, hd, half] = (blk + bias).astype(BF16)
        p_half = {}
        for r in group:
            for hd in range(2):
                s0, s1 = s_half[r, hd, 0], s_half[r, hd, 1]
                m = jnp.max(jnp.maximum(s0, s1), axis=-1, keepdims=True)
                p_half[r, hd, 0], p_half[r, hd, 1] = jnp.exp2(s0 - m), jnp.exp2(s1 - m)
        acc = {}
        for k, us in sorted(users.items()):
            lhs = jnp.concatenate([p_half[r, hd, half] for r, half in us for hd in range(2)], axis=0)
            vext = jnp.concatenate([v_ref[k * GRID_W:k * GRID_W + tile_keys, :], ones], axis=-1)
            part = jnp.dot(lhs, vext, preferred_element_type=F32)
            for i, (r, half) in enumerate(us):
                for hd in range(2):
                    blk = part[(2 * i + hd) * GRID_W:(2 * i + hd + 1) * GRID_W, :]
                    acc[r, hd] = blk if (r, hd) not in acc else acc[r, hd] + blk
        for r in group:
            y = jnp.where(first_head, acc[r, 0][:, :LANES], acc[r, 1][:, :LANES])
            d = jnp.where(first_head, acc[r, 0][:, LANES:], acc[r, 1][:, LANES:])
            z = z_ref[r * GRID_W:(r + 1) * GRID_W, :].astype(F32)
            o_ref[r * GRID_W:(r + 1) * GRID_W, :] = (y * (1.0 / d) * _silu(z)).astype(BF16)


NA_GROUP = 16


def _na(proj, bias, layer, batch, seq):
    def col(c0):
        return pl.BlockSpec((seq, LANES), lambda b, p, c0=c0: (b, c0 + p))

    return pl.pallas_call(
        _na_kernel,
        grid=(batch, N_HEAD_PAIRS),
        in_specs=[
            col(COL_QB), col(COL_KB), col(COL_VB), col(COL_ZB),
            pl.BlockSpec((None, 2, NA_ROWS, GRID_W, NA_ROWS * GRID_W), lambda b, p: (layer, p, 0, 0, 0)),
        ],
        out_specs=pl.BlockSpec((seq, LANES), lambda b, p: (b, p)),
        out_shape=jax.ShapeDtypeStruct((batch * seq, W_BRANCH), BF16),
        compiler_params=pltpu.CompilerParams(
            dimension_semantics=("arbitrary", "arbitrary"), vmem_limit_bytes=VMEM_LIMIT),
        name="na_attn",
    )(proj, proj, proj, proj, bias)


DIL_QB = 128
DIL_KW = 256
N_VARIANTS = 3


def _t5_bucket(rel):
    nb = T5_BUCKETS // 2
    max_exact = nb // 2
    n = np.abs(rel)
    large = max_exact + (np.log(np.maximum(n, 1) / max_exact) / np.log(T5_MAX_DIST / max_exact)
                         * (nb - max_exact)).astype(np.int32)
    large = np.minimum(large, nb - 1)
    return (np.where(rel > 0, nb, 0) + np.where(n < max_exact, n, large)).astype(np.int32)


def _dil_bias_table(t5_bias):
    n_pat = len(DIL_PATTERNS)
    lane = np.arange(DIL_VEC)
    n = np.where(lane < DIL_KW, lane, lane - DIL_VEC)
    in_range = (lane < DIL_KW) | (lane > DIL_VEC - DIL_QB)
    pick = np.zeros((n_pat, N_VARIANTS, T5_BUCKETS, DIL_VEC), np.float32)
    neg = np.zeros((n_pat, N_VARIANTS, 1, DIL_VEC), np.float32)
    for pi, (_, d) in enumerate(DIL_PATTERNS):
        for var in range(N_VARIANTS):
            step = n - DIL_HALF * var
            ok = in_range & (np.abs(step) <= DIL_HALF)
            bucket = _t5_bucket(d * np.clip(step, -DIL_HALF, DIL_HALF))
            pick[pi, var, bucket[ok], lane[ok]] = 1.0
            neg[pi, var, 0, ~ok] = NEG_INF
    vecs = jnp.einsum('bh,pvbn->hpvn', t5_bias.astype(F32), pick,
                      precision=lax.Precision.HIGHEST) * LOG2E + neg[:, :, 0][None]
    return pl.pallas_call(
        _dil_bias_kernel,
        grid=(N_HEADS,),
        in_specs=[pl.BlockSpec((None, n_pat * N_VARIANTS, DIL_VEC), lambda h: (h, 0, 0))],
        out_specs=pl.BlockSpec((None, n_pat, N_VARIANTS, DIL_QB, DIL_KW), lambda h: (h, 0, 0, 0, 0)),
        out_shape=jax.ShapeDtypeStruct((N_HEADS, n_pat, N_VARIANTS, DIL_QB, DIL_KW), F32),
        compiler_params=pltpu.CompilerParams(dimension_semantics=("arbitrary",)),
        name="dil_bias",
    )(vecs.reshape(N_HEADS, n_pat * N_VARIANTS, DIL_VEC))


DIL_VEC = 512


def _dil_bias_kernel(vec_ref, o_ref):
    for pi in range(len(DIL_PATTERNS)):
        for var in range(N_VARIANTS):
            i = pi * N_VARIANTS + var
            x = jnp.broadcast_to(vec_ref[i:i + 1, :], (DIL_QB, DIL_VEC))
            o_ref[pi, var] = pltpu.roll(x, 0, axis=1, stride=1, stride_axis=0)[:, :DIL_KW]


DIL_GROUP = 16


def _dil_kernel(q32_ref, k32_ref, v32_ref, k_ref, v_ref, z_ref, bias_ref, o_ref, qf, kf, vf, qg, kg, vg, qd,
                q4, k4, v4, m_s, l_s, acc_s):
    seq = k_ref.shape[0]
    conv_rows = 256
    lane = lax.broadcasted_iota(jnp.int32, (DIL_QB, LANES), 1)
    first_head = lane < HEAD_DIM
    nt = (((1,), (1,)), ((), ()))
    ones = jnp.ones((DIL_KW, LANES), BF16)

    n4 = seq // 4
    n16 = seq // 16

    def to_order4(i, carry):
        dst = pl.multiple_of(i * conv_rows, conv_rows)
        src = pl.ds(dst // n4 + 4 * (dst % n4), conv_rows, stride=4)
        for a, b, c in ((q32_ref, qg, q4), (k32_ref, kg, k4), (v32_ref, vg, v4)):
            part = a[src, :]
            if a is q32_ref:
                part = part * Q_SCALE_LOG2
            b[pl.ds(dst, conv_rows), :] = part
            c[pl.ds(dst, conv_rows), :] = part.astype(BF16)
        return carry

    lax.fori_loop(0, seq // conv_rows, to_order4, 0, unroll=True)

    def window(src, res, start, size, n_sub):
        if n_sub == n16:
            rows = pl.ds((res % 4) * n4 + res // 4 + 4 * start, size, stride=4)
            return src[rows, :].astype(BF16)
        return src[pl.ds(pl.multiple_of(res * n_sub + start, DIL_HALF), size), :]

    order4_state = (qf, kf, vf)
    token_state = (m_s, l_s, acc_s)
    plan = ((2, 16, (qg, kg, vg), None, order4_state),
            (1, 4, (q4, k4, v4), order4_state, token_state),
            (0, 1, (qd, k_ref, v_ref), token_state, None))

    for step, (pi, d, (q_src, k_src, v_src), st_in, st_out) in enumerate(plan):
        n_sub = seq // d
        nblk = n_sub // DIL_QB

        if step == 2:
            def scaled_q(i, carry):
                r0 = pl.multiple_of(i * conv_rows, conv_rows)
                qd[pl.ds(r0, conv_rows), :] = (q32_ref[pl.ds(r0, conv_rows), :] * Q_SCALE_LOG2).astype(BF16)
                return carry

            lax.fori_loop(0, seq // conv_rows, scaled_q, 0, unroll=True)

        def rows_out(res, i0, step=step):
            if step == 0:
                return pl.ds((res % 4) * n4 + res // 4 + 4 * i0, DIL_QB, stride=4)
            if step == 1:
                return pl.ds(res + 4 * i0, DIL_QB, stride=4)
            return pl.ds(pl.multiple_of(i0, DIL_HALF), DIL_QB)

        n_group = DIL_GROUP

        def group(gi, carry, pi=pi, n_sub=n_sub, nblk=nblk, q_src=q_src, k_src=k_src, v_src=v_src,
                  st_in=st_in, st_out=st_out, rows_out=rows_out, n_group=n_group):
            units = []
            for i in range(n_group):
                u = gi * n_group + i
                res = u // nblk
                i0 = (u % nblk) * DIL_QB
                ks = jnp.clip(i0 - DIL_HALF, 0, n_sub - DIL_KW)
                units.append((res, i0, ks, (i0 - ks) // DIL_HALF))
            s_tiles, m_tiles = [], []
            for res, i0, ks, var in units:
                qb = window(q_src, res, i0, DIL_QB, n_sub)
                kw = window(k_src, res, ks, DIL_KW, n_sub)
                for hd in range(2):
                    keep = first_head if hd == 0 else jnp.logical_not(first_head)
                    qm = jnp.where(keep, qb, jnp.zeros_like(qb))
                    s = lax.dot_general(qm, kw, nt, preferred_element_type=F32) + bias_ref[hd, pi, var]
                    s = s.astype(BF16)
                    s_tiles.append(s)
                    m_tiles.append(jnp.max(s, axis=-1, keepdims=True))
            p_tiles = [jnp.exp2(s - m) for s, m in zip(s_tiles, m_tiles)]
            for i, (res, i0, ks, var) in enumerate(units):
                vw = window(v_src, res, ks, DIL_KW, n_sub)
                vext = jnp.concatenate([vw, ones], axis=-1)
                r0 = jnp.dot(p_tiles[2 * i], vext, preferred_element_type=F32)
                r1 = jnp.dot(p_tiles[2 * i + 1], vext, preferred_element_type=F32)
                m_cur = jnp.where(first_head, m_tiles[2 * i].astype(F32), m_tiles[2 * i + 1].astype(F32))
                l_cur = jnp.where(first_head, r0[:, LANES:], r1[:, LANES:])
                acc_cur = jnp.where(first_head, r0[:, :LANES], r1[:, :LANES])
                dst = rows_out(res, i0)
                if st_in is None:
                    m_new, l_new, acc_new = m_cur, l_cur, acc_cur
                else:
                    src = pl.ds(pl.multiple_of(res * n_sub + i0, DIL_HALF), DIL_QB)
                    m_old = st_in[0][src, :]
                    m_new = jnp.maximum(m_old, m_cur)
                    a_old = jnp.exp2(m_old - m_new)
                    a_cur = jnp.exp2(m_cur - m_new)
                    l_new = a_old * st_in[1][src, :] + a_cur * l_cur
                    acc_new = a_old * st_in[2][src, :] + a_cur * acc_cur
                if st_out is None:
                    z = z_ref[dst, :].astype(F32)
                    o_ref[dst, :] = (acc_new * (1.0 / l_new) * _silu(z)).astype(BF16)
                else:
                    st_out[0][dst, :] = m_new
                    st_out[1][dst, :] = l_new
                    st_out[2][dst, :] = acc_new
            return carry

        lax.fori_loop(0, seq // DIL_QB // n_group, group, 0, unroll=True)


def _dil(proj, qkv32, bias, batch, seq):
    def col(c0):
        return pl.BlockSpec((seq, LANES), lambda b, p, c0=c0: (b, c0 + p))

    scratch = ([pltpu.VMEM((seq, LANES), F32) for _ in range(6)]
               + [pltpu.VMEM((seq, LANES), BF16) for _ in range(4)]
               + [pltpu.VMEM((seq, LANES), F32) for _ in range(3)])
    return pl.pallas_call(
        _dil_kernel,
        grid=(batch, N_HEAD_PAIRS),
        in_specs=[
            col(0), col(COL_KC - COL_QC), col(COL_VC - COL_QC),
            col(COL_KC), col(COL_VC), col(COL_ZC),
            pl.BlockSpec((2, len(DIL_PATTERNS), N_VARIANTS, DIL_QB, DIL_KW),
                         lambda b, p: (p, 0, 0, 0, 0)),
        ],
        out_specs=pl.BlockSpec((seq, LANES), lambda b, p: (b, p)),
        out_shape=jax.ShapeDtypeStruct((batch * seq, W_BRANCH), BF16),
        scratch_shapes=scratch,
        compiler_params=pltpu.CompilerParams(
            dimension_semantics=("arbitrary", "arbitrary"), vmem_limit_bytes=VMEM_LIMIT),
        name="dil_attn",
    )(qkv32, qkv32, qkv32, proj, proj, proj, bias)


N_POW = CHUNK + 1
PREP_GROUPS = 8


def _ssm_prep_kernel(lre_ref, lim_ref, ldt_ref, btre_ref, btim_ref, cre_ref, cim_ref,
                     toep_ref, win_ref, wout_ref, decay_ref):
    n = CHUNK * SSM_GROUP
    lane = lax.broadcasted_iota(jnp.int32, (SSM_GROUP, LANES), 1)
    lo = lane < SSM_STATE
    lane_n = lax.broadcasted_iota(jnp.int32, (SSM_GROUP, n), 1)
    nt = (((1,), (1,)), ((), ()))
    pair = lambda x, y: jnp.where(lo, x, y)

    swap = lambda x: pltpu.roll(x, SSM_STATE, axis=1)

    for j in range(PREP_GROUPS):
        xs, ys = {}, {}
        lam_re = lre_ref[0, j]
        lam_im = lim_ref[0, j]
        dt = jnp.exp(ldt_ref[0, j])
        mag = jnp.exp(lam_re * dt)
        lb_re = mag * jnp.cos(lam_im * dt)
        lb_im = mag * jnp.sin(lam_im * dt)
        n_re = lb_re - 1.0
        den = lam_re * lam_re + lam_im * lam_im
        q_re = (n_re * lam_re + lb_im * lam_im) / den
        q_im = (lb_im * lam_re - n_re * lam_im) / den
        bt_re = btre_ref[0, j]
        bt_im = btim_ref[0, j]
        bb_re = q_re * bt_re - q_im * bt_im
        bb_im = q_re * bt_im + q_im * bt_re
        c_re = cre_ref[0, j]
        c_im = cim_ref[0, j]
        pw_re = jnp.ones_like(lb_re)
        pw_im = jnp.zeros_like(lb_re)
        for k in range(N_POW):
            xs[k] = (pw_re * bb_re - pw_im * bb_im, pw_re * bb_im + pw_im * bb_re)
            ys[k] = (pw_re * c_re - pw_im * c_im, pw_re * c_im + pw_im * c_re)
            pw_last = (pw_re, pw_im)
            pw_re, pw_im = pw_re * lb_re - pw_im * lb_im, pw_re * lb_im + pw_im * lb_re

        step = [GROUPS_PER_SLAB * (a // GROUPS_PER_SLAB) + (a % GROUPS_PER_SLAB - j) % GROUPS_PER_SLAB
                for a in range(CHUNK)]
        rows = lambda a: slice(a * SSM_GROUP, (a + 1) * SSM_GROUP)

        decay_ref[0, 0, j:j + 1, :] = pw_last[0][0:1]
        decay_ref[0, 1, j:j + 1, :] = pw_last[1][0:1]

        for a, s in enumerate(step):
            win_ref[0, j, rows(a), :LANES] = pair(xs[CHUNK - 1 - s][0], xs[s][0]).astype(BF16)
            win_ref[0, j, rows(a), LANES:] = pair(xs[CHUNK - 1 - s][1], xs[s][1]).astype(BF16)

        z_re = jnp.concatenate([pair(ys[t + 1][0], ys[CHUNK - t][0]) for t in step], axis=0)
        z_nim = jnp.concatenate([pair(-ys[t + 1][1], -ys[CHUNK - t][1]) for t in step], axis=0)
        wout_ref[0, j, :LANES, :] = z_re.T.astype(BF16)
        wout_ref[0, j, LANES:, :] = z_nim.T.astype(BF16)

        kf = lax.dot_general(
            pair(bb_re, -swap(bb_im)),
            jnp.concatenate([pair(ys[m][0], swap(ys[m][1])) for m in range(CHUNK)], axis=0),
            nt, precision=lax.Precision.HIGHEST, preferred_element_type=F32)
        kb = lax.dot_general(
            pair(swap(bb_re), -bb_im),
            jnp.concatenate([pair(swap(ys[CHUNK - 1 - i][0]), ys[CHUNK - 1 - i][1])
                             for i in range(CHUNK)], axis=0),
            nt, precision=lax.Precision.HIGHEST, preferred_element_type=F32)
        for a, s in enumerate(step):
            fwd = pltpu.roll(kf, SSM_GROUP * s, axis=1) if s else kf
            shift_b = (n - SSM_GROUP * (CHUNK - 1 - s)) % n
            bwd = pltpu.roll(kb, shift_b, axis=1) if shift_b else kb
            t_nat = (jnp.where(lane_n >= SSM_GROUP * s, fwd, 0.0)
                     + jnp.where(lane_n < SSM_GROUP * (s + 1), bwd, 0.0))
            for h in range(2):
                half = t_nat[:, h * LANES:(h + 1) * LANES]
                if j:
                    half = pltpu.roll(half, SSM_GROUP * j, axis=1)
                toep_ref[0, j, rows(a), h * LANES:(h + 1) * LANES] = half.astype(BF16)


def _ssm_prep(lam_re, lam_im, log_dt, b_re, b_im, c_re, c_im):
    depth = lam_re.shape[0]
    G, P, C = SSM_GROUPS, SSM_STATE, SSM_GROUP
    n = CHUNK * C
    gb = PREP_GROUPS
    both = lambda a: jnp.concatenate([a[:, 0], a[:, 1]], axis=-1).astype(F32)
    rep = lambda a: jnp.broadcast_to(both(a)[:, :, None, :], (depth, G, C, LANES))
    ldt = rep(jnp.broadcast_to(log_dt[..., None], (depth, 2, G, P)))
    bt = lambda a: both(a.transpose(0, 1, 2, 4, 3))
    spec_in = pl.BlockSpec((1, gb, C, LANES), lambda l, i: (l, i, 0, 0))
    spec_w = pl.BlockSpec((1, gb, n, n), lambda l, i: (l, i, 0, 0))
    shp_w = jax.ShapeDtypeStruct((depth, G, n, n), BF16)
    return pl.pallas_call(
        _ssm_prep_kernel,
        grid=(depth, G // gb),
        in_specs=[spec_in] * 7,
        out_specs=[spec_w, spec_w, spec_w, pl.BlockSpec((1, 2, gb, LANES), lambda l, i: (l, 0, i, 0))],
        out_shape=[shp_w, shp_w, shp_w, jax.ShapeDtypeStruct((depth, 2, G, LANES), F32)],
        compiler_params=pltpu.CompilerParams(
            dimension_semantics=("arbitrary", "arbitrary"), vmem_limit_bytes=VMEM_LIMIT),
        name="ssm_prep",
    )(rep(lam_re), rep(lam_im), ldt, bt(b_re), bt(b_im), both(c_re), both(c_im))


GROUPS_PER_SLAB = LANES // SSM_GROUP
N_SLABS = W_BRANCH // LANES
RELAYOUT_ROWS = 32
SCAN_SLABS = 2


def _ssm_kernel(xa_ref, toep_ref, win_ref, wout_ref, decay_ref, d_ref, o_ref, xf, u_s, st, hs, yq):
    seq = xa_ref.shape[0]
    nchunk = seq // CHUNK
    nrb = nchunk // RELAYOUT_ROWS
    gps = GROUPS_PER_SLAB
    lane_rb = lax.broadcasted_iota(jnp.int32, (RELAYOUT_ROWS, LANES), 1)
    seg_masks = [(lane_rb >= SSM_GROUP * sg) & (lane_rb < SSM_GROUP * (sg + 1)) for sg in range(gps)]
    conv_rows = 512

    lane_g = lax.broadcasted_iota(jnp.int32, (gps, LANES), 1)
    fwd_lanes = lane_g < SSM_STATE
    lo, hi = slice(0, SSM_STATE), slice(SSM_STATE, LANES)

    for q0 in range(0, N_SLABS, SCAN_SLABS):
        for s in range(SCAN_SLABS):
            q = q0 + s
            c_lo = q * LANES

            def conv(i, carry, c_lo=c_lo, s=s):
                r0 = pl.multiple_of(i * conv_rows, conv_rows)
                xf[s, pl.ds(r0, conv_rows), :] = xa_ref[pl.ds(r0, conv_rows), c_lo:c_lo + LANES].astype(F32)
                return carry

            lax.fori_loop(0, seq // conv_rows, conv, 0)

            for hh in range(2):
                def fwd_relayout(rb, carry, hh=hh, s=s):
                    c0 = pl.multiple_of(rb * RELAYOUT_ROWS, RELAYOUT_ROWS)
                    rolled = []
                    for r in range(gps):
                        xs = xf[s, pl.ds(CHUNK * c0 + gps * hh + r, RELAYOUT_ROWS, stride=CHUNK), :]
                        rolled.append((xs if r == 0 else pltpu.roll(xs, SSM_GROUP * r, axis=1)).astype(BF16))
                    for j in range(gps):
                        out = rolled[(0 - j) % gps]
                        for sg in range(1, gps):
                            out = jnp.where(seg_masks[sg], rolled[(sg - j) % gps], out)
                        u_s[s, j, pl.ds(c0, RELAYOUT_ROWS), hh * LANES:(hh + 1) * LANES] = out
                    return carry

                lax.fori_loop(0, nrb, fwd_relayout, 0, unroll=True)

            for j in range(gps):
                s_in = jnp.dot(u_s[s, j], win_ref[gps * q + j], preferred_element_type=F32)
                st[s, 0, pl.ds(j, nchunk, stride=gps), :] = s_in[:, :LANES]
                st[s, 1, pl.ds(j, nchunk, stride=gps), :] = s_in[:, LANES:]

        a_re = [decay_ref[0, gps * (q0 + s):gps * (q0 + s + 1), :] for s in range(SCAN_SLABS)]
        a_im = [decay_ref[1, gps * (q0 + s):gps * (q0 + s + 1), :] for s in range(SCAN_SLABS)]

        def scan(i, carry):
            rf = pl.ds(pl.multiple_of(i * gps, gps), gps)
            rb_ = pl.ds(pl.multiple_of((nchunk - 1 - i) * gps, gps), gps)
            new = []
            for s in range(SCAN_SLABS):
                h_re, h_im = carry[2 * s], carry[2 * s + 1]
                hs[s, 0, rf, lo] = h_re[:, lo]
                hs[s, 1, rf, lo] = h_im[:, lo]
                hs[s, 0, rb_, hi] = h_re[:, hi]
                hs[s, 1, rb_, hi] = h_im[:, hi]
                in_re = jnp.where(fwd_lanes, st[s, 0, rf, :], st[s, 0, rb_, :])
                in_im = jnp.where(fwd_lanes, st[s, 1, rf, :], st[s, 1, rb_, :])
                new += [a_re[s] * h_re - a_im[s] * h_im + in_re, a_re[s] * h_im + a_im[s] * h_re + in_im]
            return tuple(new)

        zero = jnp.zeros((gps, LANES), F32)
        lax.fori_loop(0, nchunk, scan, (zero,) * (2 * SCAN_SLABS), unroll=8)

        for s in range(SCAN_SLABS):
            q = q0 + s
            c_lo = q * LANES
            for j in range(gps):
                g = gps * q + j
                h_in = jnp.concatenate([hs[s, 0, pl.ds(j, nchunk, stride=gps), :].astype(BF16),
                                        hs[s, 1, pl.ds(j, nchunk, stride=gps), :].astype(BF16)], axis=-1)
                yq[j] = (jnp.dot(u_s[s, j], toep_ref[g], preferred_element_type=F32)
                         + jnp.dot(h_in, wout_ref[g], preferred_element_type=F32))

            d_row = d_ref[:, c_lo:c_lo + LANES]
            for hh in range(2):
                def bwd_relayout(rb, carry, hh=hh, d_row=d_row, q=q, s=s):
                    c0 = pl.multiple_of(rb * RELAYOUT_ROWS, RELAYOUT_ROWS)
                    ys = [yq[j, pl.ds(c0, RELAYOUT_ROWS), hh * LANES:(hh + 1) * LANES] for j in range(gps)]
                    for r in range(gps):
                        merged = ys[(0 - r) % gps]
                        for sg in range(1, gps):
                            merged = jnp.where(seg_masks[sg], ys[(sg - r) % gps], merged)
                        if r:
                            merged = pltpu.roll(merged, LANES - SSM_GROUP * r, axis=1)
                        tok = pl.ds(CHUNK * c0 + gps * hh + r, RELAYOUT_ROWS, stride=CHUNK)
                        o_ref[q, tok, :] = merged + d_row * xf[s, tok, :]
                    return carry

                lax.fori_loop(0, nrb, bwd_relayout, 0, unroll=True)


def _ssm(proj, toep, w_in, w_out, decay, d_skip, layer, batch, seq):
    nchunk = seq // CHUNK
    n = CHUNK * SSM_GROUP
    wspec = pl.BlockSpec((None, SSM_GROUPS, n, n), lambda b: (layer, 0, 0, 0), pipeline_mode=pl.Buffered(1))
    return pl.pallas_call(
        _ssm_kernel,
        grid=(batch,),
        in_specs=[
            pl.BlockSpec((seq, W_BRANCH), lambda b: (b, COL_XA // N_SLABS)),
            wspec, wspec, wspec,
            pl.BlockSpec((None, 2, SSM_GROUPS, LANES), lambda b: (layer, 0, 0, 0)),
            pl.BlockSpec((None, 1, W_BRANCH), lambda b: (layer, 0, 0)),
        ],
        out_specs=pl.BlockSpec((N_SLABS, seq, LANES), lambda b: (0, b, 0)),
        out_shape=jax.ShapeDtypeStruct((N_SLABS, batch * seq, LANES), F32),
        scratch_shapes=[
            pltpu.VMEM((SCAN_SLABS, seq, LANES), F32),
            pltpu.VMEM((SCAN_SLABS, GROUPS_PER_SLAB, nchunk, n), BF16),
            pltpu.VMEM((SCAN_SLABS, 2, nchunk * GROUPS_PER_SLAB, LANES), F32),
            pltpu.VMEM((SCAN_SLABS, 2, nchunk * GROUPS_PER_SLAB, LANES), F32),
            pltpu.VMEM((GROUPS_PER_SLAB, nchunk, n), F32),
        ],
        compiler_params=pltpu.CompilerParams(
            dimension_semantics=("arbitrary",), vmem_limit_bytes=VMEM_LIMIT),
        name="ssm_mix",
    )(proj, toep, w_in, w_out, decay, d_skip)


OUT_TM = 1024


def _out_proj_kernel(ya_ref, za_ref, yb_ref, yc_ref, x_ref, w_ref, gw_ref, gb_ref, fg_ref, o_ref,
                     w_bf, gw_bf, *, final):
    @pl.when(pl.program_id(0) == 0)
    def _():
        for n in range(MIX_WIDTH // W_BRANCH):
            w_bf[n * W_BRANCH:(n + 1) * W_BRANCH, :] = w_ref[n * W_BRANCH:(n + 1) * W_BRANCH, :].astype(BF16)
        gw_bf[...] = gw_ref[...].astype(BF16)

    y = jnp.concatenate([ya_ref[i] for i in range(N_SLABS)], axis=-1)
    g = _gelu_tanh(y)
    gate = jnp.dot(g.astype(BF16), gw_bf[...], preferred_element_type=F32) + gb_ref[...]
    ya = g * _sigmoid(gate) * _silu(za_ref[...].astype(F32))
    delta = (jnp.dot(ya.astype(BF16), w_bf[:W_BRANCH, :], preferred_element_type=F32)
             + jnp.dot(yb_ref[...], w_bf[W_BRANCH:2 * W_BRANCH, :], preferred_element_type=F32)
             + jnp.dot(yc_ref[...], w_bf[2 * W_BRANCH:, :], preferred_element_type=F32))
    x = x_ref[...] + delta
    if final:
        ms = jnp.mean(x * x, axis=-1, keepdims=True)
        x = x * lax.rsqrt(ms + RMS_EPS) * fg_ref[...]
    o_ref[...] = x


def _out_proj(ya_pre, proj, yb, yc, x2d, w, glu_w, glu_b, final_g, layer, final):
    rows = x2d.shape[0]
    row_blk = lambda width: pl.BlockSpec((OUT_TM, width), lambda i: (i, 0))
    const = lambda shape: pl.BlockSpec((None,) + shape, lambda i: (layer,) + (0,) * len(shape),
                                       pipeline_mode=pl.Buffered(1))
    return pl.pallas_call(
        functools.partial(_out_proj_kernel, final=final),
        grid=(rows // OUT_TM,),
        in_specs=[
            pl.BlockSpec((N_SLABS, OUT_TM, LANES), lambda i: (0, i, 0)),
            pl.BlockSpec((OUT_TM, W_BRANCH), lambda i: (i, COL_ZA // N_SLABS)),
            row_blk(W_BRANCH), row_blk(W_BRANCH), row_blk(D_MODEL),
            const((MIX_WIDTH, D_MODEL)), const((W_BRANCH, W_BRANCH)),
            const((1, W_BRANCH)),
            pl.BlockSpec((None, 1, D_MODEL), lambda i: (0, 0, 0), pipeline_mode=pl.Buffered(1)),
        ],
        out_specs=row_blk(D_MODEL),
        out_shape=jax.ShapeDtypeStruct((rows, D_MODEL), F32),
        scratch_shapes=[pltpu.VMEM((MIX_WIDTH, D_MODEL), BF16), pltpu.VMEM((W_BRANCH, W_BRANCH), BF16)],
        compiler_params=pltpu.CompilerParams(
            dimension_semantics=("arbitrary",), vmem_limit_bytes=VMEM_LIMIT),
        name="out_proj_final" if final else "out_proj",
    )(ya_pre, proj, yb, yc, x2d, w, glu_w, glu_b, final_g)


def kernel(x, norm_g, w_in, w_out, ssm_lam_re, ssm_lam_im, ssm_log_dt, ssm_b_re, ssm_b_im, ssm_c_re,
           ssm_c_im, ssm_d, glu_w, glu_b, na_rpb, t5_bias, final_g):
    batch, seq, _ = x.shape
    depth = w_in.shape[0]
    x2d = x.astype(F32).reshape(batch * seq, D_MODEL)
    toep, s_in, s_out, decay = _ssm_prep(ssm_lam_re, ssm_lam_im, ssm_log_dt, ssm_b_re, ssm_b_im,
                                         ssm_c_re, ssm_c_im)
    dil_bias = _dil_bias_table(t5_bias)
    na_bias = _na_bias_table(na_rpb)
    w_in_f, w_out_f, glu_w_f = w_in.astype(F32), w_out.astype(F32), glu_w.astype(F32)
    row3 = lambda a: a.astype(F32).reshape(a.shape[0], 1, a.shape[1])
    norm_g3, ssm_d3, glu_b3 = row3(norm_g), row3(ssm_d), row3(glu_b)
    final_g3 = final_g.astype(F32).reshape(1, 1, D_MODEL)
    for l in range(depth):
        proj, dil_qkv32 = _in_proj(x2d, norm_g3, w_in_f, l)
        ya_pre = _ssm(proj, toep, s_in, s_out, decay, ssm_d3, l, batch, seq)
        yb = _na(proj, na_bias, l, batch, seq)
        yc = _dil(proj, dil_qkv32, dil_bias, batch, seq)
        x2d = _out_proj(ya_pre, proj, yb, yc, x2d, w_out_f, glu_w_f, glu_b3, final_g3, l,
                        final=(l == depth - 1))
    return x2d.reshape(batch, seq, D_MODEL).astype(x.dtype)
```
